```python
import math
import jax
import jax.numpy as jnp
from jax import lax
import numpy as np

D_MODEL = 1024
BATCH = 2
SEQ = 16384
DEPTH = 1
DEC_BATCH = 4
DEC_SEQ = 4096
PAST_LEN = 128

D_MIX = D_MODEL
GM_WIDTH = D_MIX // 2
GM_HEADS = 4
GM_HEAD_DIM = GM_WIDTH // GM_HEADS
CHUNK = 128
S5_WIDTH = D_MIX - GM_WIDTH
S5_GROUP = 16
S5_GROUPS = S5_WIDTH // S5_GROUP
S5_STATE = 64
D_IN = 2 * GM_WIDTH + S5_WIDTH
N_COARSE = 4
N_FINE = 8
N_EXPERTS = N_COARSE * N_FINE
TOP_FINE = 2
D_EXPERT = D_MODEL // 4
EPS = 1e-6
LAMBDA_RE_MAX = -1e-4
DT_MIN = 1e-3
DT_MAX = 1e-1
F32 = jnp.float32

kernel_name = 'hybrid_gmlp_s5_hmoe_encoder'


def rms_norm(x, g):
    xf = x.astype(F32)
    y = xf * lax.rsqrt(jnp.mean(xf * xf, axis=-1, keepdims=True) + EPS)
    return (y * g.astype(F32)).astype(x.dtype)


def layer_norm(x, g, b):
    xf = x.astype(F32)
    mu = jnp.mean(xf, axis=-1, keepdims=True)
    xc = xf - mu
    var = jnp.mean(xc * xc, axis=-1, keepdims=True)
    return (xc * lax.rsqrt(var + EPS) * g.astype(F32) + b.astype(F32)).astype(x.dtype)


def gmlp_group(u, v, ln_g, ln_b, ws, bs):
    b, L, _ = v.shape
    vh = v.reshape(b, L // CHUNK, CHUNK, GM_HEADS, GM_HEAD_DIM)
    vh = layer_norm(vh, ln_g.reshape(GM_HEADS, GM_HEAD_DIM), ln_b.reshape(GM_HEADS, GM_HEAD_DIM))
    s = jnp.einsum('hpq,bcqhd->bcphd', ws, vh) + bs.T[None, None, :, :, None]
    return u * s.reshape(b, L, GM_WIDTH)


def _ssm_combine(left, right):
    a_l, b_l = left
    a_r, b_r = right
    return a_r * a_l, a_r * b_l + b_r


def s5_scan(u, lam_re, lam_im, log_step, b_re, b_im, c_re, c_im, reverse):
    lam = lax.complex(jnp.minimum(lam_re.astype(F32), LAMBDA_RE_MAX), lam_im.astype(F32))
    step = jnp.exp(log_step.astype(F32))[:, None]
    lam_bar = jnp.exp(lam * step)
    b_mat = lax.complex(b_re.astype(F32), b_im.astype(F32))
    b_bar = ((lam_bar - 1.0) / lam)[..., None] * b_mat
    bu = jnp.einsum('blgh,gph->blgp', u.astype(jnp.complex64), b_bar)
    a = jnp.broadcast_to(lam_bar, bu.shape)
    _, states = lax.associative_scan(_ssm_combine, (a, bu), reverse=reverse, axis=1)
    c_mat = lax.complex(c_re.astype(F32), c_im.astype(F32))
    return jnp.einsum('blgp,ghp->blgh', states, c_mat).real


def s5_group(u, lre_f, lim_f, ls_f, bre_f, bim_f, cre_f, cim_f,
             lre_b, lim_b, ls_b, bre_b, bim_b, cre_b, cim_b, d, glu_w, glu_b):
    b, L, _ = u.shape
    uf = u.astype(F32).reshape(b, L, S5_GROUPS, S5_GROUP)
    y = (s5_scan(uf, lre_f, lim_f, ls_f, bre_f, bim_f, cre_f, cim_f, False)
         + s5_scan(uf, lre_b, lim_b, ls_b, bre_b, bim_b, cre_b, cim_b, True)
         + d.astype(F32) * uf)
    g = jax.nn.gelu(y.reshape(b, L, S5_WIDTH)).astype(u.dtype)
    return g * jax.nn.sigmoid(g @ glu_w + glu_b)


def hier_moe(x, r1_w, r1_b, r2_w, r2_b, w_gate, w_up, w_down):
    b, L, d = x.shape
    t = x.reshape(b * L, d)
    logit1 = (t @ r1_w).astype(F32) + r1_b.astype(F32)
    p1 = jax.nn.softmax(logit1, axis=-1)
    grp = jnp.argmax(logit1, axis=-1)
    p_grp = jnp.take_along_axis(p1, grp[:, None], axis=1)
    logit2 = jnp.einsum('td,gde->tge', t, r2_w).astype(F32) + r2_b.astype(F32)
    logit2 = jnp.take_along_axis(logit2, grp[:, None, None], axis=1)[:, 0]
    top_val, top_idx = lax.top_k(logit2, TOP_FINE)
    w_fine = jax.nn.softmax(top_val, axis=-1) * p_grp
    expert = grp[:, None] * N_FINE + top_idx
    combine = jnp.einsum('tk,tke->te', w_fine, jax.nn.one_hot(expert, N_EXPERTS, dtype=F32))
    y = jnp.zeros_like(t)
    for e in range(N_EXPERTS):
        h = jax.nn.silu(t @ w_gate[e]) * (t @ w_up[e])
        y = y + combine[:, e:e + 1].astype(t.dtype) * (h @ w_down[e])
    return y.reshape(b, L, d)


def encoder_layer(x, norm_mix_g, w_in, gm_ln_g, gm_ln_b, gm_ws, gm_bs,
                  lre_f, lim_f, ls_f, bre_f, bim_f, cre_f, cim_f,
                  lre_b, lim_b, ls_b, bre_b, bim_b, cre_b, cim_b,
                  s5_d, s5_glu_w, s5_glu_b, out_norm_gm, out_norm_s5, w_out,
                  norm_ffn_g, r1_w, r1_b, r2_w, r2_b, e_w_gate, e_w_up, e_w_down):
    h = rms_norm(x, norm_mix_g)
    proj = h @ w_in
    u_gm = jax.nn.gelu(proj[..., :GM_WIDTH])
    v_gm = jax.nn.gelu(proj[..., GM_WIDTH:2 * GM_WIDTH])
    u_s5 = proj[..., 2 * GM_WIDTH:]
    y_gm = gmlp_group(u_gm, v_gm, gm_ln_g, gm_ln_b, gm_ws, gm_bs)
    y_s5 = s5_group(u_s5, lre_f, lim_f, ls_f, bre_f, bim_f, cre_f, cim_f,
                    lre_b, lim_b, ls_b, bre_b, bim_b, cre_b, cim_b, s5_d, s5_glu_w, s5_glu_b)
    merged = jnp.concatenate([rms_norm(y_gm, out_norm_gm), rms_norm(y_s5, out_norm_s5)], axis=-1)
    x = x + merged @ w_out
    x = x + hier_moe(rms_norm(x, norm_ffn_g), r1_w, r1_b, r2_w, r2_b, e_w_gate, e_w_up, e_w_down)
    return x


def setup_inputs(seed: int = 0) -> dict:
    key = jax.random.key(seed)
    ks = iter(jax.random.split(key, 64))

    def nrm(shape, scale):
        return scale * jax.random.normal(next(ks), shape, F32)

    def gain(shape):
        return 1.0 + 0.02 * jax.random.normal(next(ks), shape, F32)

    Ld = DEPTH
    gs = (Ld, S5_GROUPS, S5_STATE)

    def s5_dir():
        lam_re = -0.5 + nrm(gs, 0.01)
        lam_im = math.pi * jnp.arange(S5_STATE, dtype=F32) + nrm(gs, 0.01)
        log_step = jax.random.uniform(next(ks), (Ld, S5_GROUPS), F32, math.log(DT_MIN), math.log(DT_MAX))
        b_re = nrm((Ld, S5_GROUPS, S5_STATE, S5_GROUP), (2 * S5_GROUP) ** -0.5)
        b_im = nrm((Ld, S5_GROUPS, S5_STATE, S5_GROUP), (2 * S5_GROUP) ** -0.5)
        c_re = nrm((Ld, S5_GROUPS, S5_GROUP, S5_STATE), S5_STATE ** -0.5)
        c_im = nrm((Ld, S5_GROUPS, S5_GROUP, S5_STATE), S5_STATE ** -0.5)
        return lam_re, lam_im, log_step, b_re, b_im, c_re, c_im

    x_prompt = jax.random.normal(next(ks), (BATCH, SEQ, D_MODEL), F32)
    x_sample = jax.random.normal(next(ks), (DEC_BATCH, DEC_SEQ, D_MODEL), F32)
    norm_mix_g = gain((Ld, D_MODEL))
    w_in = nrm((Ld, D_MODEL, D_IN), D_MODEL ** -0.5)
    gm_ln_g = gain((Ld, GM_WIDTH))
    gm_ln_b = nrm((Ld, GM_WIDTH), 0.02)
    gm_ws = nrm((Ld, GM_HEADS, CHUNK, CHUNK), CHUNK ** -0.5)
    gm_bs = gain((Ld, GM_HEADS, CHUNK))
    f = s5_dir()
    bk = s5_dir()
    s5_d = nrm((Ld, S5_GROUPS, S5_GROUP), 1.0)
    s5_glu_w = nrm((Ld, S5_WIDTH, S5_WIDTH), S5_WIDTH ** -0.5)
    s5_glu_b = nrm((Ld, S5_WIDTH), 0.02)
    out_norm_gm = gain((Ld, GM_WIDTH))
    out_norm_s5 = gain((Ld, S5_WIDTH))
    w_out = nrm((Ld, D_MIX, D_MODEL), D_MIX ** -0.5)
    norm_ffn_g = gain((Ld, D_MODEL))
    r1_w = nrm((Ld, D_MODEL, N_COARSE), D_MODEL ** -0.5)
    r1_b = nrm((Ld, N_COARSE), 0.01)
    r2_w = nrm((Ld, N_COARSE, D_MODEL, N_FINE), D_MODEL ** -0.5)
    r2_b = nrm((Ld, N_COARSE, N_FINE), 0.01)
    e_w_gate = nrm((Ld, N_EXPERTS, D_MODEL, D_EXPERT), D_MODEL ** -0.5)
    e_w_up = nrm((Ld, N_EXPERTS, D_MODEL, D_EXPERT), D_MODEL ** -0.5)
    e_w_down = nrm((Ld, N_EXPERTS, D_EXPERT, D_MODEL), D_EXPERT ** -0.5)
    norm_final_g = gain((D_MODEL,))
    return {
        'x_prompt': x_prompt, 'x_sample': x_sample,
        'norm_mix_g': norm_mix_g, 'w_in': w_in,
        'gm_ln_g': gm_ln_g, 'gm_ln_b': gm_ln_b, 'gm_ws': gm_ws, 'gm_bs': gm_bs,
        's5_lam_re_fwd': f[0], 's5_lam_im_fwd': f[1], 's5_log_step_fwd': f[2],
        's5_b_re_fwd': f[3], 's5_b_im_fwd': f[4], 's5_c_re_fwd': f[5], 's5_c_im_fwd': f[6],
        's5_lam_re_bwd': bk[0], 's5_lam_im_bwd': bk[1], 's5_log_step_bwd': bk[2],
        's5_b_re_bwd': bk[3], 's5_b_im_bwd': bk[4], 's5_c_re_bwd': bk[5], 's5_c_im_bwd': bk[6],
        's5_d': s5_d, 's5_glu_w': s5_glu_w, 's5_glu_b': s5_glu_b,
        'out_norm_gm': out_norm_gm, 'out_norm_s5': out_norm_s5, 'w_out': w_out,
        'norm_ffn_g': norm_ffn_g, 'r1_w': r1_w, 'r1_b': r1_b, 'r2_w': r2_w, 'r2_b': r2_b,
        'e_w_gate': e_w_gate, 'e_w_up': e_w_up, 'e_w_down': e_w_down,
        'norm_final_g': norm_final_g,
    }


def reference(x_prompt, x_sample, norm_mix_g, w_in, gm_ln_g, gm_ln_b, gm_ws, gm_bs,
              s5_lam_re_fwd, s5_lam_im_fwd, s5_log_step_fwd, s5_b_re_fwd, s5_b_im_fwd, s5_c_re_fwd, s5_c_im_fwd,
              s5_lam_re_bwd, s5_lam_im_bwd, s5_log_step_bwd, s5_b_re_bwd, s5_b_im_bwd, s5_c_re_bwd, s5_c_im_bwd,
              s5_d, s5_glu_w, s5_glu_b, out_norm_gm, out_norm_s5, w_out,
              norm_ffn_g, r1_w, r1_b, r2_w, r2_b, e_w_gate, e_w_up, e_w_down, norm_final_g):
    layer_params = (norm_mix_g, w_in, gm_ln_g, gm_ln_b, gm_ws, gm_bs,
                    s5_lam_re_fwd, s5_lam_im_fwd, s5_log_step_fwd, s5_b_re_fwd, s5_b_im_fwd, s5_c_re_fwd, s5_c_im_fwd,
                    s5_lam_re_bwd, s5_lam_im_bwd, s5_log_step_bwd, s5_b_re_bwd, s5_b_im_bwd, s5_c_re_bwd, s5_c_im_bwd,
                    s5_d, s5_glu_w, s5_glu_b, out_norm_gm, out_norm_s5, w_out,
                    norm_ffn_g, r1_w, r1_b, r2_w, r2_b, e_w_gate, e_w_up, e_w_down)

    def trunk(x):
        for l in range(DEPTH):
            x = encoder_layer(x, *[p[l] for p in layer_params])
        return rms_norm(x, norm_final_g)

    y_prompt = trunk(x_prompt)
    y_sample = trunk(x_sample)
    return (y_prompt, y_sample)
```

```python
import functools
import math

import jax
import jax.numpy as jnp
from jax import lax
from jax.experimental import pallas as pl
from jax.experimental.pallas import tpu as pltpu

F32 = jnp.float32
BF16 = jnp.bfloat16

EPS = 1e-6
LAMBDA_RE_MAX = -1e-4
GM_HEADS = 4
CHUNK = 128
S5_GROUP = 16
S5_STATE = 64
N_COARSE = 4
N_FINE = 8
N_EXPERTS = N_COARSE * N_FINE

LANES = 128
SUBLANES = 8
S5_LC = 16
VMEM_LIMIT = 56 * 1024 * 1024

TM_PROJ = 512
TM_MIX = 512
TM_MOE = 1024


def _gelu(x):
    c = math.sqrt(2.0 / math.pi)
    return x * (0.5 * (1.0 + jnp.tanh(c * (x + 0.044715 * (x * x * x)))))


def _rms(x, g):
    ms = jnp.mean(x * x, axis=-1, keepdims=True)
    return x * lax.rsqrt(ms + EPS) * g


def _dot(a, b):
    return jnp.dot(a, b, preferred_element_type=F32)


def _inproj_gmlp_kernel(x_ref, gmix_ref, win_ref, lng_ref, lnb_ref, ws_ref, bs_ref, gout_ref,
                        mgm_ref, us5_ref, y_scr):
    tm = x_ref.shape[0]
    gw = mgm_ref.shape[1]
    hd_dim = gw // GM_HEADS
    n_chunks = tm // CHUNK
    h = _rms(x_ref[...], gmix_ref[...]).astype(BF16)
    proj = _dot(h, win_ref[...])
    us5_ref[...] = proj[:, 2 * gw:]
    u = _gelu(proj[:, :gw])
    v = _gelu(proj[:, gw:2 * gw])
    for hd in range(GM_HEADS):
        lo = hd * hd_dim
        vh = v[:, lo:lo + hd_dim]
        mu = jnp.mean(vh, axis=-1, keepdims=True)
        xc = vh - mu
        var = jnp.mean(xc * xc, axis=-1, keepdims=True)
        vn = (xc * lax.rsqrt(var + EPS) * lng_ref[:, lo:lo + hd_dim]
              + lnb_ref[:, lo:lo + hd_dim]).astype(BF16)
        rhs = jnp.concatenate([vn[c * CHUNK:(c + 1) * CHUNK] for c in range(n_chunks)], axis=1)
        s = _dot(ws_ref[hd], rhs)
        for c in range(n_chunks):
            sc = s[:, c * hd_dim:(c + 1) * hd_dim] + bs_ref[hd]
            y_scr[c * CHUNK:(c + 1) * CHUNK, lo:lo + hd_dim] = u[c * CHUNK:(c + 1) * CHUNK, lo:lo + hd_dim] * sc
    mgm_ref[...] = _rms(y_scr[...], gout_ref[...]).astype(BF16)


def _inproj_gmlp(x2d, gmix, win_bf, lng, lnb, ws_bf, bs_b, gout):
    t, d = x2d.shape
    d_in = win_bf.shape[1]
    gw = lng.shape[1]
    s5w = d_in - 2 * gw
    tm = TM_PROJ
    const = lambda *shape: pl.BlockSpec(shape, lambda i: (0,) * len(shape))
    return pl.pallas_call(
        _inproj_gmlp_kernel,
        grid=(t // tm,),
        in_specs=[
            pl.BlockSpec((tm, d), lambda i: (i, 0)),
            const(1, d), const(d, d_in), const(1, gw), const(1, gw),
            const(GM_HEADS, CHUNK, CHUNK), const(GM_HEADS, CHUNK, gw // GM_HEADS), const(1, gw),
        ],
        out_specs=[pl.BlockSpec((tm, gw), lambda i: (i, 0)),
                   pl.BlockSpec((tm, s5w), lambda i: (i, 0))],
        out_shape=[jax.ShapeDtypeStruct((t, gw), BF16),
                   jax.ShapeDtypeStruct((t, s5w), F32)],
        scratch_shapes=[pltpu.VMEM((tm, gw), F32)],
        compiler_params=pltpu.CompilerParams(dimension_semantics=("parallel",),
                                             vmem_limit_bytes=VMEM_LIMIT),
        name="inproj_gmlp",
    )(x2d, gmix, win_bf, lng, lnb, ws_bf, bs_b, gout)


def _s5_consts(lam_re, lam_im, log_step, b_re, b_im, c_re, c_im, lc):
    lam = lax.complex(jnp.minimum(lam_re.astype(F32), LAMBDA_RE_MAX), lam_im.astype(F32))
    step = jnp.exp(log_step.astype(F32))[:, None]
    lam_dt = lam * step
    b_bar = ((jnp.exp(lam_dt) - 1.0) / lam)[..., None] * lax.complex(b_re.astype(F32), b_im.astype(F32))
    c = lax.complex(c_re.astype(F32), c_im.astype(F32))
    k = jnp.arange(lc + 1, dtype=F32)[:, None, None]
    pw = jnp.exp(k * lam_dt[None])
    return lam_dt, pw, b_bar, c


def _ri(z, swap=False):
    parts = (jnp.imag(z), jnp.real(z)) if swap else (jnp.real(z), jnp.imag(z))
    return jnp.concatenate(parts, axis=-1)


def _s5_operator(fwd, bwd, lc, seg_steps):
    lam_f, pw_f, bb_f, c_f = _s5_consts(*fwd, lc)
    lam_b, pw_b, bb_b, c_b = _s5_consts(*bwd, lc)
    g, p, h = bb_f.shape
    kf = jnp.real(jnp.einsum('ghp,kgp,gpi->gkhi', c_f, pw_f[:lc], bb_f))
    kb = jnp.real(jnp.einsum('ghp,kgp,gpi->gkhi', c_b, pw_b[:lc], bb_b))
    s_idx = jnp.arange(lc)[:, None]
    j_idx = jnp.arange(lc)[None, :]
    lag = j_idx - s_idx
    tf = jnp.where((lag >= 0)[None, :, :, None, None], kf[:, jnp.clip(lag, 0, lc - 1)], 0.0)
    tb = jnp.where((lag <= 0)[None, :, :, None, None], kb[:, jnp.clip(-lag, 0, lc - 1)], 0.0)
    m = (tf + tb).transpose(0, 1, 4, 2, 3).reshape(g, lc * h, lc * h)

    wf = jnp.einsum('sgp,gpi->gsip', pw_f[:lc][::-1], bb_f)
    wb = jnp.einsum('sgp,gpi->gsip', pw_b[:lc], bb_b)
    w1 = jnp.concatenate([_ri(wf), _ri(wf, True), _ri(wb), _ri(wb, True)], axis=-1).reshape(g, lc * h, 8 * p)

    ef = jnp.einsum('ghp,jgp->gpjh', c_f, pw_f[1:])
    eb = jnp.einsum('ghp,jgp->gpjh', c_b, pw_b[1:][::-1])
    w2 = jnp.concatenate([jnp.real(ef), -jnp.imag(ef), jnp.real(eb), -jnp.imag(eb)],
                         axis=1).reshape(g, 4 * p, lc * h)

    def mult(z):
        return [jnp.concatenate([jnp.real(z), jnp.real(z)], -1), jnp.concatenate([-jnp.imag(z), jnp.imag(z)], -1)]

    rows = (mult(jnp.exp(lc * lam_f)) + mult(jnp.exp(lc * lam_b))
            + mult(jnp.exp((lc * seg_steps) * lam_f)) + mult(jnp.exp((lc * seg_steps) * lam_b)))
    sc = jnp.stack(rows, axis=1)
    return m.astype(BF16), w1.astype(BF16), w2.astype(BF16), sc.astype(F32)


def _s5_kernel(x_ref, m_ref, w1_ref, w2_ref, sc_ref, y_ref, loc_scr, sin_scr, *, n_seg):
    rows = x_ref.shape[1]
    steps = rows // SUBLANES
    sw = sc_ref.shape[2]
    x = x_ref[0]
    loc_scr[...] = _dot(x, w1_ref[0])

    def bc(i):
        return jnp.broadcast_to(sc_ref[0, i:i + 1, :], (SUBLANES, sw))

    a1f, a2f, a1b, a2b, p1f, p2f, p1b, p2b = [bc(i) for i in range(8)]

    def step_f(s, f, fs):
        r = pl.multiple_of(s * SUBLANES, SUBLANES)
        lf = loc_scr[pl.ds(r, SUBLANES), 0:sw]
        lfs = loc_scr[pl.ds(r, SUBLANES), sw:2 * sw]
        return a1f * f + a2f * fs + lf, a1f * fs - a2f * f + lfs

    def step_b(s, b, bs):
        r = pl.multiple_of((steps - 1 - s) * SUBLANES, SUBLANES)
        lb = loc_scr[pl.ds(r, SUBLANES), 2 * sw:3 * sw]
        lbs = loc_scr[pl.ds(r, SUBLANES), 3 * sw:4 * sw]
        return a1b * b + a2b * bs + lb, a1b * bs - a2b * b + lbs

    zero = jnp.zeros((SUBLANES, sw), F32)

    def pass1(s, carry):
        f, fs, b, bs = carry
        return step_f(s, f, fs) + step_b(s, b, bs)

    f_end, fs_end, b_end, bs_end = lax.fori_loop(0, steps, pass1, (zero, zero, zero, zero), unroll=4)

    seg = lax.broadcasted_iota(jnp.int32, (SUBLANES, sw), 0) % n_seg
    cf, cfs, cb, cbs = zero, zero, zero, zero
    for _ in range(n_seg - 1):
        ef = f_end + p1f * cf + p2f * cfs
        efs = fs_end + p1f * cfs - p2f * cf
        eb = b_end + p1b * cb + p2b * cbs
        ebs = bs_end + p1b * cbs - p2b * cb
        cf = jnp.where(seg >= 1, pltpu.roll(ef, 1, 0), 0.0)
        cfs = jnp.where(seg >= 1, pltpu.roll(efs, 1, 0), 0.0)
        cb = jnp.where(seg <= n_seg - 2, pltpu.roll(eb, SUBLANES - 1, 0), 0.0)
        cbs = jnp.where(seg <= n_seg - 2, pltpu.roll(ebs, SUBLANES - 1, 0), 0.0)

    def pass2(s, carry):
        f, fs, b, bs = carry
        rf = pl.multiple_of(s * SUBLANES, SUBLANES)
        rb = pl.multiple_of((steps - 1 - s) * SUBLANES, SUBLANES)
        sin_scr[pl.ds(rf, SUBLANES), 0:sw] = f
        sin_scr[pl.ds(rb, SUBLANES), sw:2 * sw] = b
        return step_f(s, f, fs) + step_b(s, b, bs)

    lax.fori_loop(0, steps, pass2, (cf, cfs, cb, cbs), unroll=4)

    y_ref[0] = _dot(x, m_ref[0]) + _dot(sin_scr[...].astype(BF16), w2_ref[0])


def _s5_scan(xg, m, w1, w2, sc, n_seg):
    g, rows, kw = xg.shape
    sw = sc.shape[2]
    blk = lambda a: pl.BlockSpec((1,) + a.shape[1:], lambda i: (i, 0, 0))
    return pl.pallas_call(
        functools.partial(_s5_kernel, n_seg=n_seg),
        grid=(g,),
        in_specs=[blk(xg), blk(m), blk(w1), blk(w2), blk(sc)],
        out_specs=pl.BlockSpec((1, rows, kw), lambda i: (i, 0, 0)),
        out_shape=jax.ShapeDtypeStruct((g, rows, kw), F32),
        scratch_shapes=[pltpu.VMEM((rows, 4 * sw), F32), pltpu.VMEM((rows, 2 * sw), F32)],
        compiler_params=pltpu.CompilerParams(dimension_semantics=("parallel",),
                                             vmem_limit_bytes=VMEM_LIMIT),
        name="s5_scan",
    )(xg, m, w1, w2, sc)


def _mix_route_kernel(ys_ref, us5_ref, mgm_ref, x_ref, d_ref, gluw_ref, glub_ref, gs5_ref,
                      wout_ref, gffn_ref, rwh_ref, rwl_ref, rb_ref,
                      x2_ref, t_ref, cw_ref):
    gw = mgm_ref.shape[1]
    y = ys_ref[...] + d_ref[...] * us5_ref[...]
    g = _gelu(y)
    z = g * jax.nn.sigmoid(_dot(g.astype(BF16), gluw_ref[...]) + glub_ref[...])
    ms5 = _rms(z, gs5_ref[...]).astype(BF16)
    mix = _dot(mgm_ref[...], wout_ref[:gw, :]) + _dot(ms5, wout_ref[gw:, :])
    x2 = x_ref[...] + mix
    x2_ref[...] = x2
    t = _rms(x2, gffn_ref[...])
    t_hi = t.astype(BF16)
    t_ref[...] = t_hi
    t_lo = (t - t_hi.astype(F32)).astype(BF16)
    logits = (_dot(t_hi, rwh_ref[...]) + _dot(t_hi, rwl_ref[...]) + _dot(t_lo, rwh_ref[...])
              + rb_ref[...])
    lane = lax.broadcasted_iota(jnp.int32, logits.shape, 1)
    neg = jnp.float32(-jnp.inf)
    big = jnp.int32(LANES)

    def first_max(mask):
        vals = jnp.where(mask, logits, neg)
        mx = jnp.max(vals, axis=-1, keepdims=True)
        idx = jnp.min(jnp.where(mask & (vals == mx), lane, big), axis=-1, keepdims=True)
        return mx, idx

    coarse = lane < N_COARSE
    m1, grp = first_max(coarse)
    p_grp = 1.0 / jnp.sum(jnp.where(coarse, jnp.exp(logits - m1), 0.0), axis=-1, keepdims=True)
    lo = N_COARSE + grp * N_FINE
    fine = (lane >= lo) & (lane < lo + N_FINE)
    v1, i1 = first_max(fine)
    v2, i2 = first_max(fine & (lane != i1))
    e21 = jnp.exp(v2 - v1)
    w1 = p_grp / (1.0 + e21)
    w2 = p_grp * e21 / (1.0 + e21)
    elane = lane + N_COARSE
    cw_ref[...] = jnp.where(elane == i1, w1, 0.0) + jnp.where(elane == i2, w2, 0.0)


def _mix_route(ys, us5, mgm, x2d, d, gluw_bf, glub, gs5, wout_bf, gffn, rwh, rwl, rb):
    t, dm = x2d.shape
    gw = mgm.shape[1]
    s5w = us5.shape[1]
    tm = TM_MIX
    const = lambda *shape: pl.BlockSpec(shape, lambda i: (0,) * len(shape))
    tile = lambda w: pl.BlockSpec((tm, w), lambda i: (i, 0))
    return pl.pallas_call(
        _mix_route_kernel,
        grid=(t // tm,),
        in_specs=[tile(s5w), tile(s5w), tile(gw), tile(dm),
                  const(1, s5w), const(s5w, s5w), const(1, s5w), const(1, s5w),
                  const(gw + s5w, dm), const(1, dm), const(dm, LANES), const(dm, LANES), const(1, LANES)],
        out_specs=[tile(dm), tile(dm), tile(LANES)],
        out_shape=[jax.ShapeDtypeStruct((t, dm), F32),
                   jax.ShapeDtypeStruct((t, dm), BF16),
                   jax.ShapeDtypeStruct((t, LANES), F32)],
        compiler_params=pltpu.CompilerParams(dimension_semantics=("parallel",),
                                             vmem_limit_bytes=VMEM_LIMIT),
        name="mix_route",
    )(ys, us5, mgm, x2d, d, gluw_bf, glub, gs5, wout_bf, gffn, rwh, rwl, rb)


def _moe_kernel(t_ref, cw_ref, x2_ref, wgu_ref, wd_ref, gfin_ref, o_ref, acc_scr):
    e = pl.program_id(1)
    de = wd_ref.shape[1]

    @pl.when(e == 0)
    def _():
        acc_scr[...] = jnp.zeros_like(acc_scr)

    gu = _dot(t_ref[...], wgu_ref[0])
    hidden = (jax.nn.silu(gu[:, :de]) * gu[:, de:]).astype(BF16)
    ye = _dot(hidden, wd_ref[0])
    cw = cw_ref[...]
    lane = lax.broadcasted_iota(jnp.int32, cw.shape, 1)
    cwe = jnp.sum(jnp.where(lane == e, cw, 0.0), axis=-1, keepdims=True)
    acc_scr[...] += cwe * ye

    @pl.when(e == pl.num_programs(1) - 1)
    def _():
        o_ref[...] = _rms(x2_ref[...] + acc_scr[...], gfin_ref[...])


def _moe(t_bf, cw, x2, wgu_bf, wd_bf, gfin):
    t, dm = x2.shape
    ne, _, de2 = wgu_bf.shape
    tm = TM_MOE
    return pl.pallas_call(
        _moe_kernel,
        grid=(t // tm, ne),
        in_specs=[pl.BlockSpec((tm, dm), lambda i, e: (i, 0)),
                  pl.BlockSpec((tm, LANES), lambda i, e: (i, 0)),
                  pl.BlockSpec((tm, dm), lambda i, e: (i, 0)),
                  pl.BlockSpec((1, dm, de2), lambda i, e: (e, 0, 0)),
                  pl.BlockSpec((1, de2 // 2, dm), lambda i, e: (e, 0, 0)),
                  pl.BlockSpec((1, dm), lambda i, e: (0, 0))],
        out_specs=pl.BlockSpec((tm, dm), lambda i, e: (i, 0)),
        out_shape=jax.ShapeDtypeStruct((t, dm), F32),
        scratch_shapes=[pltpu.VMEM((tm, dm), F32)],
        compiler_params=pltpu.CompilerParams(dimension_semantics=("parallel", "arbitrary"),
                                             vmem_limit_bytes=VMEM_LIMIT),
        name="moe_experts",
    )(t_bf, cw, x2, wgu_bf, wd_bf, gfin)


def _to_groups(us5, b, l, n_seg, lc):
    gh = us5.shape[1]
    g = gh // S5_GROUP
    steps = l // (lc * n_seg)
    x = us5.astype(BF16).reshape(b, n_seg, steps, lc, g, S5_GROUP)
    x = x.transpose(4, 2, 0, 1, 3, 5)
    return x.reshape(g, steps * b * n_seg, lc * S5_GROUP)


def _from_groups(yg, b, l, n_seg, lc):
    g = yg.shape[0]
    steps = l // (lc * n_seg)
    y = yg.reshape(g, steps, b, n_seg, lc, S5_GROUP)
    y = y.transpose(2, 3, 1, 4, 0, 5)
    return y.reshape(b * l, g * S5_GROUP)


def _layer(x, p, s5_ops, gfin):
    b, l, dm = x.shape
    x2d = x.reshape(b * l, dm)
    mgm, us5 = _inproj_gmlp(x2d, p['gmix'], p['win'], p['lng'], p['lnb'], p['ws'], p['bs'], p['gout_gm'])
    n_seg = SUBLANES // b
    m, w1, w2, sc = s5_ops[(l // (S5_LC * n_seg))]
    xg = _to_groups(us5, b, l, n_seg, S5_LC)
    yg = _s5_scan(xg, m, w1, w2, sc, n_seg)
    ys = _from_groups(yg, b, l, n_seg, S5_LC)
    x2, t_bf, cw = _mix_route(ys, us5, mgm, x2d, p['d'], p['gluw'], p['glub'], p['gout_s5'],
                              p['wout'], p['gffn'], p['rwh'], p['rwl'], p['rb'])
    out = _moe(t_bf, cw, x2, p['wgu'], p['wd'], gfin)
    return out.reshape(b, l, dm)


def kernel(x_prompt, x_sample, norm_mix_g, w_in, gm_ln_g, gm_ln_b, gm_ws, gm_bs, s5_lam_re_fwd, s5_lam_im_fwd, s5_log_step_fwd, s5_b_re_fwd, s5_b_im_fwd, s5_c_re_fwd, s5_c_im_fwd, s5_lam_re_bwd, s5_lam_im_bwd, s5_log_step_bwd, s5_b_re_bwd, s5_b_im_bwd, s5_c_re_bwd, s5_c_im_bwd, s5_d, s5_glu_w, s5_glu_b, out_norm_gm, out_norm_s5, w_out, norm_ffn_g, r1_w, r1_b, r2_w, r2_b, e_w_gate, e_w_up, e_w_down, norm_final_g):
    depth = w_in.shape[0]
    gfin = norm_final_g.reshape(1, -1).astype(F32)
    xs = [x_prompt, x_sample]
    for li in range(depth):
        row = lambda a: a[li].reshape(1, -1).astype(F32)
        dm = w_in.shape[1]
        gw = gm_ln_g.shape[1]
        hd_dim = gw // GM_HEADS
        rw = jnp.concatenate([r1_w[li], r2_w[li].transpose(1, 0, 2).reshape(dm, N_EXPERTS)], axis=1).astype(F32)
        rw = jnp.pad(rw, ((0, 0), (0, LANES - rw.shape[1])))
        rwh = rw.astype(BF16)
        rwl = (rw - rwh.astype(F32)).astype(BF16)
        rb = jnp.concatenate([r1_b[li], r2_b[li].reshape(-1)]).astype(F32)
        rb = jnp.pad(rb, (0, LANES - rb.shape[0])).reshape(1, LANES)
        p = dict(
            gmix=row(norm_mix_g), win=w_in[li].astype(BF16), lng=row(gm_ln_g), lnb=row(gm_ln_b),
            ws=gm_ws[li].astype(BF16),
            bs=jnp.broadcast_to(gm_bs[li].astype(F32)[:, :, None], (GM_HEADS, CHUNK, hd_dim)),
            gout_gm=row(out_norm_gm), d=row(s5_d), gluw=s5_glu_w[li].astype(BF16), glub=row(s5_glu_b),
            gout_s5=row(out_norm_s5), wout=w_out[li].astype(BF16), gffn=row(norm_ffn_g),
            rwh=rwh, rwl=rwl, rb=rb,
            wgu=jnp.concatenate([e_w_gate[li], e_w_up[li]], axis=-1).astype(BF16),
            wd=e_w_down[li].astype(BF16),
        )
        fwd = (s5_lam_re_fwd[li], s5_lam_im_fwd[li], s5_log_step_fwd[li], s5_b_re_fwd[li], s5_b_im_fwd[li],
               s5_c_re_fwd[li], s5_c_im_fwd[li])
        bwd = (s5_lam_re_bwd[li], s5_lam_im_bwd[li], s5_log_step_bwd[li], s5_b_re_bwd[li], s5_b_im_bwd[li],
               s5_c_re_bwd[li], s5_c_im_bwd[li])
        s5_ops = {}
        for x in xs:
            seg_steps = x.shape[1] // (S5_LC * (SUBLANES // x.shape[0]))
            if seg_steps not in s5_ops:
                s5_ops[seg_steps] = _s5_operator(fwd, bwd, S5_LC, seg_steps)
        last = li == depth - 1
        assert last, "depth > 1 needs an un-normalised layer output"
        xs = [_layer(x, p, s5_ops, gfin) for x in xs]
    return tuple(xs)
```

```python
import functools
import math

import jax
import jax.numpy as jnp
from jax import lax
from jax.experimental import pallas as pl
from jax.experimental.pallas import tpu as pltpu

F32 = jnp.float32
BF16 = jnp.bfloat16

EPS = 1e-6
LAMBDA_RE_MAX = -1e-4
GM_HEADS = 4
CHUNK = 128
S5_GROUP = 16
S5_STATE = 64
N_COARSE = 4
N_FINE = 8
N_EXPERTS = N_COARSE * N_FINE

LANES = 128
SUBLANES = 8
S5_LC = 16
VMEM_LIMIT = 56 * 1024 * 1024

TM_PROJ = 512
TM_MIX = 512
TM_ROWS = 256
TM_EXPERT = 256


def _gelu(x):
    c = math.sqrt(2.0 / math.pi)
    return x * (0.5 * (1.0 + jnp.tanh(c * (x + 0.044715 * (x * x * x)))))


def _rms(x, g):
    ms = jnp.mean(x * x, axis=-1, keepdims=True)
    return x * lax.rsqrt(ms + EPS) * g


def _dot(a, b):
    return jnp.dot(a, b, preferred_element_type=F32)


def _inproj_gmlp_kernel(x_ref, gmix_ref, win_ref, lng_ref, lnb_ref, ws_ref, bs_ref, gout_ref,
                        mgm_ref, us5_ref, y_scr):
    tm = x_ref.shape[0]
    gw = mgm_ref.shape[1]
    hd_dim = gw // GM_HEADS
    n_chunks = tm // CHUNK
    h = _rms(x_ref[...], gmix_ref[...]).astype(BF16)
    proj = _dot(h, win_ref[...])
    us5_ref[...] = proj[:, 2 * gw:]
    u = _gelu(proj[:, :gw])
    v = _gelu(proj[:, gw:2 * gw])
    for hd in range(GM_HEADS):
        lo = hd * hd_dim
        vh = v[:, lo:lo + hd_dim]
        mu = jnp.mean(vh, axis=-1, keepdims=True)
        xc = vh - mu
        var = jnp.mean(xc * xc, axis=-1, keepdims=True)
        vn = (xc * lax.rsqrt(var + EPS) * lng_ref[:, lo:lo + hd_dim]
              + lnb_ref[:, lo:lo + hd_dim]).astype(BF16)
        rhs = jnp.concatenate([vn[c * CHUNK:(c + 1) * CHUNK] for c in range(n_chunks)], axis=1)
        s = _dot(ws_ref[hd], rhs)
        for c in range(n_chunks):
            sc = s[:, c * hd_dim:(c + 1) * hd_dim] + bs_ref[hd]
            y_scr[c * CHUNK:(c + 1) * CHUNK, lo:lo + hd_dim] = u[c * CHUNK:(c + 1) * CHUNK, lo:lo + hd_dim] * sc
    mgm_ref[...] = _rms(y_scr[...], gout_ref[...]).astype(BF16)


def _inproj_gmlp(x2d, gmix, win_bf, lng, lnb, ws_bf, bs_b, gout):
    t, d = x2d.shape
    d_in = win_bf.shape[1]
    gw = lng.shape[1]
    s5w = d_in - 2 * gw
    tm = TM_PROJ
    const = lambda *shape: pl.BlockSpec(shape, lambda i: (0,) * len(shape))
    return pl.pallas_call(
        _inproj_gmlp_kernel,
        grid=(t // tm,),
        in_specs=[
            pl.BlockSpec((tm, d), lambda i: (i, 0)),
            const(1, d), const(d, d_in), const(1, gw), const(1, gw),
            const(GM_HEADS, CHUNK, CHUNK), const(GM_HEADS, CHUNK, gw // GM_HEADS), const(1, gw),
        ],
        out_specs=[pl.BlockSpec((tm, gw), lambda i: (i, 0)),
                   pl.BlockSpec((tm, s5w), lambda i: (i, 0))],
        out_shape=[jax.ShapeDtypeStruct((t, gw), BF16),
                   jax.ShapeDtypeStruct((t, s5w), F32)],
        scratch_shapes=[pltpu.VMEM((tm, gw), F32)],
        compiler_params=pltpu.CompilerParams(dimension_semantics=("parallel",),
                                             vmem_limit_bytes=VMEM_LIMIT),
        name="inproj_gmlp",
    )(x2d, gmix, win_bf, lng, lnb, ws_bf, bs_b, gout)


def _s5_consts(lam_re, lam_im, log_step, b_re, b_im, c_re, c_im, lc):
    lr = jnp.minimum(lam_re.astype(F32), LAMBDA_RE_MAX)
    li = lam_im.astype(F32)
    step = jnp.exp(log_step.astype(F32))[:, None]
    dr, di = lr * step, li * step
    ar, ai = _cexp(dr, di)
    nr, ni = ar - 1.0, ai
    den = lr * lr + li * li
    qr, qi = (nr * lr + ni * li) / den, (ni * lr - nr * li) / den
    br, bi = b_re.astype(F32), b_im.astype(F32)
    bbr = qr[..., None] * br - qi[..., None] * bi
    bbi = qr[..., None] * bi + qi[..., None] * br
    k = jnp.arange(lc + 1, dtype=F32)[:, None, None]
    pwr, pwi = _cexp(k * dr[None], k * di[None])
    return (dr, di), (pwr, pwi), (bbr, bbi), (c_re.astype(F32), c_im.astype(F32))


def _cexp(zr, zi):
    m = jnp.exp(zr)
    return m * jnp.cos(zi), m * jnp.sin(zi)


def _s5_operator(fwd, bwd, lc, seg_steps):
    consts = [_s5_consts(*fwd, lc), _s5_consts(*bwd, lc)]
    g, p, h = consts[0][2][0].shape
    s_idx = jnp.arange(lc)[:, None]
    j_idx = jnp.arange(lc)[None, :]
    lag = j_idx - s_idx
    m = 0.0
    w1_parts, w2_parts, sc_rows, seg_rows = [], [], [], []
    for direction, (ld, pw, bb, c) in enumerate(consts):
        (dr, di), (pwr, pwi), (bbr, bbi), (cr, ci) = ld, pw, bb, c
        cpr = cr[None] * pwr[:, :, None, :] - ci[None] * pwi[:, :, None, :]
        cpi = cr[None] * pwi[:, :, None, :] + ci[None] * pwr[:, :, None, :]
        kk = (jnp.einsum('kghp,gpi->gkhi', cpr[:lc], bbr) - jnp.einsum('kghp,gpi->gkhi', cpi[:lc], bbi))
        lg = lag if direction == 0 else -lag
        tk = jnp.where((lg >= 0)[None, :, :, None, None], kk[:, jnp.clip(lg, 0, lc - 1)], 0.0)
        m = m + tk.transpose(0, 1, 4, 2, 3).reshape(g, lc * h, lc * h)
        er, ei = (pwr[:lc][::-1], pwi[:lc][::-1]) if direction == 0 else (pwr[:lc], pwi[:lc])
        wr = er[:, :, None, :] * bbr.transpose(0, 2, 1)[None] - ei[:, :, None, :] * bbi.transpose(0, 2, 1)[None]
        wi = er[:, :, None, :] * bbi.transpose(0, 2, 1)[None] + ei[:, :, None, :] * bbr.transpose(0, 2, 1)[None]
        wr, wi = wr.transpose(1, 0, 2, 3), wi.transpose(1, 0, 2, 3)
        w1_parts += [wr, wi, wi, wr]
        sel = slice(1, lc + 1)
        fr, fi = (cpr[sel], cpi[sel]) if direction == 0 else (cpr[sel][::-1], cpi[sel][::-1])
        w2_parts += [fr.transpose(1, 3, 0, 2), -fi.transpose(1, 3, 0, 2)]

        def mult(zr, zi):
            return [jnp.concatenate([zr, zr], -1), jnp.concatenate([-zi, zi], -1)]

        sc_rows += mult(*_cexp(lc * dr, lc * di))
        seg_rows += mult(*_cexp((lc * seg_steps) * dr, (lc * seg_steps) * di))
    w1 = jnp.concatenate(w1_parts, axis=-1).reshape(g, lc * h, 8 * p)
    w2 = jnp.concatenate(w2_parts, axis=1).reshape(g, 4 * p, lc * h)
    sc = jnp.stack(sc_rows + seg_rows, axis=1)
    return m.astype(BF16), w1.astype(BF16), w2.astype(BF16), sc.astype(F32)


def _s5_kernel(x_ref, m_ref, w1_ref, w2_ref, sc_ref, y_ref, loc_scr, sin_scr, *, n_seg):
    rows = x_ref.shape[1]
    steps = rows // SUBLANES
    sw = sc_ref.shape[2]
    x = x_ref[0]
    loc_scr[...] = _dot(x, w1_ref[0])

    def bc(i):
        return jnp.broadcast_to(sc_ref[0, i:i + 1, :], (SUBLANES, sw))

    a1f, a2f, a1b, a2b, p1f, p2f, p1b, p2b = [bc(i) for i in range(8)]

    def step_f(s, f, fs):
        r = pl.multiple_of(s * SUBLANES, SUBLANES)
        lf = loc_scr[pl.ds(r, SUBLANES), 0:sw]
        lfs = loc_scr[pl.ds(r, SUBLANES), sw:2 * sw]
        return a1f * f + a2f * fs + lf, a1f * fs - a2f * f + lfs

    def step_b(s, b, bs):
        r = pl.multiple_of((steps - 1 - s) * SUBLANES, SUBLANES)
        lb = loc_scr[pl.ds(r, SUBLANES), 2 * sw:3 * sw]
        lbs = loc_scr[pl.ds(r, SUBLANES), 3 * sw:4 * sw]
        return a1b * b + a2b * bs + lb, a1b * bs - a2b * b + lbs

    zero = jnp.zeros((SUBLANES, sw), F32)

    def pass1(s, carry):
        f, fs, b, bs = carry
        return step_f(s, f, fs) + step_b(s, b, bs)

    f_end, fs_end, b_end, bs_end = lax.fori_loop(0, steps, pass1, (zero, zero, zero, zero), unroll=4)

    seg = lax.broadcasted_iota(jnp.int32, (SUBLANES, sw), 0) % n_seg
    cf, cfs, cb, cbs = zero, zero, zero, zero
    for _ in range(n_seg - 1):
        ef = f_end + p1f * cf + p2f * cfs
        efs = fs_end + p1f * cfs - p2f * cf
        eb = b_end + p1b * cb + p2b * cbs
        ebs = bs_end + p1b * cbs - p2b * cb
        cf = jnp.where(seg >= 1, pltpu.roll(ef, 1, 0), 0.0)
        cfs = jnp.where(seg >= 1, pltpu.roll(efs, 1, 0), 0.0)
        cb = jnp.where(seg <= n_seg - 2, pltpu.roll(eb, SUBLANES - 1, 0), 0.0)
        cbs = jnp.where(seg <= n_seg - 2, pltpu.roll(ebs, SUBLANES - 1, 0), 0.0)

    def pass2(s, carry):
        f, fs, b, bs = carry
        rf = pl.multiple_of(s * SUBLANES, SUBLANES)
        rb = pl.multiple_of((steps - 1 - s) * SUBLANES, SUBLANES)
        sin_scr[pl.ds(rf, SUBLANES), 0:sw] = f
        sin_scr[pl.ds(rb, SUBLANES), sw:2 * sw] = b
        return step_f(s, f, fs) + step_b(s, b, bs)

    lax.fori_loop(0, steps, pass2, (cf, cfs, cb, cbs), unroll=4)

    y_ref[0] = _dot(x, m_ref[0]) + _dot(sin_scr[...].astype(BF16), w2_ref[0])


def _s5_scan(xg, m, w1, w2, sc, n_seg):
    g, rows, kw = xg.shape
    sw = sc.shape[2]
    blk = lambda a: pl.BlockSpec((1,) + a.shape[1:], lambda i: (i, 0, 0))
    return pl.pallas_call(
        functools.partial(_s5_kernel, n_seg=n_seg),
        grid=(g,),
        in_specs=[blk(xg), blk(m), blk(w1), blk(w2), blk(sc)],
        out_specs=pl.BlockSpec((1, rows, kw), lambda i: (i, 0, 0)),
        out_shape=jax.ShapeDtypeStruct((g, rows, kw), F32),
        scratch_shapes=[pltpu.VMEM((rows, 4 * sw), F32), pltpu.VMEM((rows, 2 * sw), F32)],
        compiler_params=pltpu.CompilerParams(dimension_semantics=("parallel",),
                                             vmem_limit_bytes=VMEM_LIMIT),
        name="s5_scan",
    )(xg, m, w1, w2, sc)


R_E1, R_E2, R_W1, R_W2, R_RANK1, R_RANK2 = range(6)


def _mix_route_kernel(ys_ref, us5_ref, mgm_ref, x_ref, d_ref, gluw_ref, glub_ref, gs5_ref,
                      wout_ref, gffn_ref, rwh_ref, rwl_ref, rb_ref, tri_ref,
                      x2_ref, route_ref, cnt_ref):
    gw = mgm_ref.shape[1]

    @pl.when(pl.program_id(0) == 0)
    def _():
        cnt_ref[...] = jnp.zeros_like(cnt_ref)

    y = ys_ref[...] + d_ref[...] * us5_ref[...]
    g = _gelu(y)
    z = g * jax.nn.sigmoid(_dot(g.astype(BF16), gluw_ref[...]) + glub_ref[...])
    ms5 = _rms(z, gs5_ref[...]).astype(BF16)
    mix = _dot(mgm_ref[...], wout_ref[:gw, :]) + _dot(ms5, wout_ref[gw:, :])
    x2 = x_ref[...] + mix
    x2_ref[...] = x2
    t = _rms(x2, gffn_ref[...])
    t_hi = t.astype(BF16)
    t_lo = (t - t_hi.astype(F32)).astype(BF16)
    logits = (_dot(t_hi, rwh_ref[...]) + _dot(t_hi, rwl_ref[...]) + _dot(t_lo, rwh_ref[...])
              + rb_ref[...])
    lane = lax.broadcasted_iota(jnp.int32, logits.shape, 1)
    neg = jnp.float32(-jnp.inf)
    big = jnp.int32(LANES)

    def first_max(mask):
        vals = jnp.where(mask, logits, neg)
        mx = jnp.max(vals, axis=-1, keepdims=True)
        idx = jnp.min(jnp.where(mask & (vals == mx), lane, big), axis=-1, keepdims=True)
        return mx, idx

    coarse = lane < N_COARSE
    m1, grp = first_max(coarse)
    p_grp = 1.0 / jnp.sum(jnp.where(coarse, jnp.exp(logits - m1), 0.0), axis=-1, keepdims=True)
    lo = N_COARSE + grp * N_FINE
    fine = (lane >= lo) & (lane < lo + N_FINE)
    v1, i1 = first_max(fine)
    v2, i2 = first_max(fine & (lane != i1))
    e21 = jnp.exp(v2 - v1)
    w1 = p_grp / (1.0 + e21)
    w2 = p_grp * e21 / (1.0 + e21)
    e1 = i1 - N_COARSE
    e2 = i2 - N_COARSE
    hit1 = lane == e1
    hit2 = lane == e2
    onehot = jnp.where(hit1 | hit2, 1.0, 0.0)
    before = _dot(tri_ref[...], onehot.astype(BF16)) + cnt_ref[...]
    rank1 = jnp.sum(jnp.where(hit1, before, 0.0), axis=-1, keepdims=True)
    rank2 = jnp.sum(jnp.where(hit2, before, 0.0), axis=-1, keepdims=True)
    tm = onehot.shape[0]
    cnt_ref[...] = before[tm - 1:tm, :] + onehot[tm - 1:tm, :]
    rec = jnp.zeros_like(logits)
    for slot, val in ((R_E1, e1.astype(F32)), (R_E2, e2.astype(F32)), (R_W1, w1), (R_W2, w2),
                      (R_RANK1, rank1), (R_RANK2, rank2)):
        rec = jnp.where(lane == slot, val, rec)
    route_ref[...] = rec


def _mix_route(ys, us5, mgm, x2d, d, gluw_bf, glub, gs5, wout_bf, gffn, rwh, rwl, rb):
    t, dm = x2d.shape
    gw = mgm.shape[1]
    s5w = us5.shape[1]
    tm = TM_MIX
    tri = jnp.tril(jnp.ones((tm, tm), F32), -1).astype(BF16)
    const = lambda *shape: pl.BlockSpec(shape, lambda i: (0,) * len(shape))
    tile = lambda w: pl.BlockSpec((tm, w), lambda i: (i, 0))
    return pl.pallas_call(
        _mix_route_kernel,
        grid=(t // tm,),
        in_specs=[tile(s5w), tile(s5w), tile(gw), tile(dm),
                  const(1, s5w), const(s5w, s5w), const(1, s5w), const(1, s5w),
                  const(gw + s5w, dm), const(1, dm), const(dm, LANES), const(dm, LANES), const(1, LANES),
                  const(tm, tm)],
        out_specs=[tile(dm), tile(LANES), const(1, LANES)],
        out_shape=[jax.ShapeDtypeStruct((t, dm), F32),
                   jax.ShapeDtypeStruct((t, LANES), F32),
                   jax.ShapeDtypeStruct((1, LANES), F32)],
        compiler_params=pltpu.CompilerParams(dimension_semantics=("arbitrary",),
                                             vmem_limit_bytes=VMEM_LIMIT),
        name="mix_route",
    )(ys, us5, mgm, x2d, d, gluw_bf, glub, gs5, wout_bf, gffn, rwh, rwl, rb, tri)


def _row_copy(src, s_row, dst, d_row, sem):
    return pltpu.make_async_copy(src.at[pl.ds(s_row, 1)], dst.at[pl.ds(d_row, 1)], sem)


def _scatter_rows_kernel(pos_ref, t_ref, sorted_in_ref, sorted_ref, sem):
    del sorted_in_ref
    tm = t_ref.shape[0]

    def issue(r, c):
        _row_copy(t_ref, r, sorted_ref, pos_ref[0, 0, r], sem).start()
        _row_copy(t_ref, r, sorted_ref, pos_ref[0, 0, tm + r], sem).start()
        return c

    lax.fori_loop(0, tm, issue, 0, unroll=8)
    for _ in range(2):
        pltpu.make_async_copy(t_ref, sorted_ref.at[pl.ds(0, tm)], sem).wait()


def _scatter_rows(x2, pos, n_sorted):
    t, w = x2.shape
    tm = TM_ROWS
    return pl.pallas_call(
        _scatter_rows_kernel,
        grid=(t // tm,),
        in_specs=[pl.BlockSpec((1, 1, 2 * tm), lambda i: (i, 0, 0), memory_space=pltpu.SMEM),
                  pl.BlockSpec((tm, w), lambda i: (i, 0)),
                  pl.BlockSpec(memory_space=pl.ANY)],
        out_specs=pl.BlockSpec(memory_space=pl.ANY),
        out_shape=jax.ShapeDtypeStruct((n_sorted, w), F32),
        scratch_shapes=[pltpu.SemaphoreType.DMA(())],
        input_output_aliases={2: 0},
        compiler_params=pltpu.CompilerParams(dimension_semantics=("arbitrary",),
                                             vmem_limit_bytes=VMEM_LIMIT),
        name="scatter_rows",
    )(pos, x2, jnp.zeros((n_sorted, w), F32))


def _experts_kernel(te_ref, nt_ref, xs_ref, gffn_ref, wgu_ref, wd_ref, ys_ref):
    i = pl.program_id(0)
    de = wd_ref.shape[1]

    @pl.when(i < nt_ref[0])
    def _():
        t = _rms(xs_ref[...], gffn_ref[...]).astype(BF16)
        gu = _dot(t, wgu_ref[0])
        hidden = (jax.nn.silu(gu[:, :de]) * gu[:, de:]).astype(BF16)
        ys_ref[...] = _dot(hidden, wd_ref[0])

    @pl.when(i >= nt_ref[0])
    def _():
        ys_ref[...] = jnp.zeros_like(ys_ref)


def _experts(tile_expert, n_tiles, x_sorted, gffn, wgu_bf, wd_bf):
    n_sorted, dm = x_sorted.shape
    de2 = wgu_bf.shape[2]
    tm = TM_EXPERT
    return pl.pallas_call(
        _experts_kernel,
        grid_spec=pltpu.PrefetchScalarGridSpec(
            num_scalar_prefetch=2,
            grid=(n_sorted // tm,),
            in_specs=[pl.BlockSpec((tm, dm), lambda i, te, nt: (i, 0)),
                      pl.BlockSpec((1, dm), lambda i, te, nt: (0, 0)),
                      pl.BlockSpec((1, dm, de2), lambda i, te, nt: (te[i], 0, 0)),
                      pl.BlockSpec((1, de2 // 2, dm), lambda i, te, nt: (te[i], 0, 0))],
            out_specs=pl.BlockSpec((tm, dm), lambda i, te, nt: (i, 0)),
        ),
        out_shape=jax.ShapeDtypeStruct((n_sorted, dm), F32),
        compiler_params=pltpu.CompilerParams(dimension_semantics=("arbitrary",),
                                             vmem_limit_bytes=VMEM_LIMIT),
        name="experts",
    )(tile_expert, n_tiles, x_sorted, gffn, wgu_bf, wd_bf)


def _combine_kernel(pos_ref, x2_ref, route_ref, gfin_ref, ys_ref, o_ref, rows_scr, sem):
    tm = x2_ref.shape[0]

    def issue(r, c):
        _row_copy(ys_ref, pos_ref[0, 0, r], rows_scr, r, sem).start()
        _row_copy(ys_ref, pos_ref[0, 0, tm + r], rows_scr, tm + r, sem).start()
        return c

    lax.fori_loop(0, tm, issue, 0, unroll=8)
    pltpu.make_async_copy(ys_ref.at[pl.ds(0, 2 * tm)], rows_scr, sem).wait()
    route = route_ref[...]
    w1 = route[:, R_W1:R_W1 + 1]
    w2 = route[:, R_W2:R_W2 + 1]
    moe = w1 * rows_scr[:tm, :] + w2 * rows_scr[tm:, :]
    o_ref[...] = _rms(x2_ref[...] + moe, gfin_ref[...])


def _combine(pos, x2, route, gfin, y_sorted):
    t, dm = x2.shape
    tm = TM_ROWS
    return pl.pallas_call(
        _combine_kernel,
        grid=(t // tm,),
        in_specs=[pl.BlockSpec((1, 1, 2 * tm), lambda i: (i, 0, 0), memory_space=pltpu.SMEM),
                  pl.BlockSpec((tm, dm), lambda i: (i, 0)),
                  pl.BlockSpec((tm, LANES), lambda i: (i, 0)),
                  pl.BlockSpec((1, dm), lambda i: (0, 0)),
                  pl.BlockSpec(memory_space=pl.ANY)],
        out_specs=pl.BlockSpec((tm, dm), lambda i: (i, 0)),
        out_shape=jax.ShapeDtypeStruct((t, dm), F32),
        scratch_shapes=[pltpu.VMEM((2 * tm, dm), F32), pltpu.SemaphoreType.DMA(())],
        compiler_params=pltpu.CompilerParams(dimension_semantics=("arbitrary",),
                                             vmem_limit_bytes=VMEM_LIMIT),
        name="combine_norm",
    )(pos, x2, route, gfin, y_sorted)


def _sorted_positions(route, counts, tm_rows, tm_expert):
    t = route.shape[0]
    cnt = counts[0, :N_EXPERTS].astype(jnp.int32)
    tiles = (cnt + tm_expert - 1) // tm_expert
    tile_end = jnp.cumsum(tiles)
    base = ((tile_end - tiles) * tm_expert).astype(F32)
    n_tiles_max = (2 * t) // tm_expert + N_EXPERTS
    experts = jnp.arange(N_EXPERTS, dtype=F32)

    def pos_of(e_lane, r_lane):
        e = route[:, e_lane]
        onehot = e[:, None] == experts[None, :]
        return (route[:, r_lane] + jnp.sum(jnp.where(onehot, base[None, :], 0.0), axis=1)).astype(jnp.int32)

    pos = jnp.concatenate([pos_of(R_E1, R_RANK1).reshape(t // tm_rows, tm_rows),
                           pos_of(R_E2, R_RANK2).reshape(t // tm_rows, tm_rows)], axis=1)[:, None, :]
    tile_idx = jnp.arange(n_tiles_max, dtype=jnp.int32)
    tile_expert = jnp.sum((tile_idx[:, None] >= tile_end[None, :]).astype(jnp.int32), axis=1)
    n_tiles = tile_end[-1:].astype(jnp.int32)
    last = jnp.sum((n_tiles - 1 >= tile_end).astype(jnp.int32))
    tile_expert = jnp.where(tile_idx < n_tiles, tile_expert, last).astype(jnp.int32)
    return pos, tile_expert, n_tiles, n_tiles_max * tm_expert


def _to_groups(us5, b, l, n_seg, lc):
    gh = us5.shape[1]
    g = gh // S5_GROUP
    steps = l // (lc * n_seg)
    x = us5.astype(BF16).reshape(b, n_seg, steps, lc, g, S5_GROUP)
    x = x.transpose(4, 2, 0, 1, 3, 5)
    return x.reshape(g, steps * b * n_seg, lc * S5_GROUP)


def _from_groups(yg, b, l, n_seg, lc):
    g = yg.shape[0]
    steps = l // (lc * n_seg)
    y = yg.reshape(g, steps, b, n_seg, lc, S5_GROUP)
    y = y.transpose(2, 3, 1, 4, 0, 5)
    return y.reshape(b * l, g * S5_GROUP)


def _layer(x, p, s5_ops, gfin):
    b, l, dm = x.shape
    x2d = x.reshape(b * l, dm)
    mgm, us5 = _inproj_gmlp(x2d, p['gmix'], p['win'], p['lng'], p['lnb'], p['ws'], p['bs'], p['gout_gm'])
    n_seg = SUBLANES // b
    m, w1, w2, sc = s5_ops[(l // (S5_LC * n_seg))]
    xg = _to_groups(us5, b, l, n_seg, S5_LC)
    yg = _s5_scan(xg, m, w1, w2, sc, n_seg)
    ys = _from_groups(yg, b, l, n_seg, S5_LC)
    x2, route, counts = _mix_route(ys, us5, mgm, x2d, p['d'], p['gluw'], p['glub'], p['gout_s5'],
                                   p['wout'], p['gffn'], p['rwh'], p['rwl'], p['rb'])
    pos, tile_expert, n_tiles, n_sorted = _sorted_positions(route, counts, TM_ROWS, TM_EXPERT)
    x_sorted = _scatter_rows(x2, pos, n_sorted)
    y_sorted = _experts(tile_expert, n_tiles, x_sorted, p['gffn'], p['wgu'], p['wd'])
    out = _combine(pos, x2, route, gfin, y_sorted)
    return out.reshape(b, l, dm)


def kernel(x_prompt, x_sample, norm_mix_g, w_in, gm_ln_g, gm_ln_b, gm_ws, gm_bs, s5_lam_re_fwd, s5_lam_im_fwd, s5_log_step_fwd, s5_b_re_fwd, s5_b_im_fwd, s5_c_re_fwd, s5_c_im_fwd, s5_lam_re_bwd, s5_lam_im_bwd, s5_log_step_bwd, s5_b_re_bwd, s5_b_im_bwd, s5_c_re_bwd, s5_c_im_bwd, s5_d, s5_glu_w, s5_glu_b, out_norm_gm, out_norm_s5, w_out, norm_ffn_g, r1_w, r1_b, r2_w, r2_b, e_w_gate, e_w_up, e_w_down, norm_final_g):
    depth = w_in.shape[0]
    gfin = norm_final_g.reshape(1, -1).astype(F32)
    xs = [x_prompt, x_sample]
    for li in range(depth):
        row = lambda a: a[li].reshape(1, -1).astype(F32)
        dm = w_in.shape[1]
        gw = gm_ln_g.shape[1]
        hd_dim = gw // GM_HEADS
        rw = jnp.concatenate([r1_w[li], r2_w[li].transpose(1, 0, 2).reshape(dm, N_EXPERTS)], axis=1).astype(F32)
        rw = jnp.pad(rw, ((0, 0), (0, LANES - rw.shape[1])))
        rwh = rw.astype(BF16)
        rwl = (rw - rwh.astype(F32)).astype(BF16)
        rb = jnp.concatenate([r1_b[li], r2_b[li].reshape(-1)]).astype(F32)
        rb = jnp.pad(rb, (0, LANES - rb.shape[0])).reshape(1, LANES)
        p = dict(
            gmix=row(norm_mix_g), win=w_in[li].astype(BF16), lng=row(gm_ln_g), lnb=row(gm_ln_b),
            ws=gm_ws[li].astype(BF16),
            bs=jnp.broadcast_to(gm_bs[li].astype(F32)[:, :, None], (GM_HEADS, CHUNK, hd_dim)),
            gout_gm=row(out_norm_gm), d=row(s5_d), gluw=s5_glu_w[li].astype(BF16), glub=row(s5_glu_b),
            gout_s5=row(out_norm_s5), wout=w_out[li].astype(BF16), gffn=row(norm_ffn_g),
            rwh=rwh, rwl=rwl, rb=rb,
            wgu=jnp.concatenate([e_w_gate[li], e_w_up[li]], axis=-1).astype(BF16),
            wd=e_w_down[li].astype(BF16),
        )
        fwd = (s5_lam_re_fwd[li], s5_lam_im_fwd[li], s5_log_step_fwd[li], s5_b_re_fwd[li], s5_b_im_fwd[li],
               s5_c_re_fwd[li], s5_c_im_fwd[li])
        bwd = (s5_lam_re_bwd[li], s5_lam_im_bwd[li], s5_log_step_bwd[li], s5_b_re_bwd[li], s5_b_im_bwd[li],
               s5_c_re_bwd[li], s5_c_im_bwd[li])
        s5_ops = {}
        for x in xs:
            seg_steps = x.shape[1] // (S5_LC * (SUBLANES // x.shape[0]))
            if seg_steps not in s5_ops:
                s5_ops[seg_steps] = _s5_operator(fwd, bwd, S5_LC, seg_steps)
        last = li == depth - 1
        assert last, "depth > 1 needs an un-normalised layer output"
        xs = [_layer(x, p, s5_ops, gfin) for x in xs]
    return tuple(xs)
```

```python
import functools
import math

import jax
import jax.numpy as jnp
from jax import lax
from jax.experimental import pallas as pl
from jax.experimental.pallas import tpu as pltpu

F32 = jnp.float32
BF16 = jnp.bfloat16

EPS = 1e-6
LAMBDA_RE_MAX = -1e-4
GM_HEADS = 4
CHUNK = 128
S5_GROUP = 16
S5_STATE = 64
N_COARSE = 4
N_FINE = 8
N_EXPERTS = N_COARSE * N_FINE

LANES = 128
SUBLANES = 8
S5_LC = 16
VMEM_LIMIT = 56 * 1024 * 1024

TM_PROJ = 512
TM_MIX = 512
TM_ROWS = 256
TM_EXPERT = 256


def _gelu(x):
    c = math.sqrt(2.0 / math.pi)
    return x * (0.5 * (1.0 + jnp.tanh(c * (x + 0.044715 * (x * x * x)))))


def _rms(x, g):
    ms = jnp.mean(x * x, axis=-1, keepdims=True)
    return x * lax.rsqrt(ms + EPS) * g


def _dot(a, b):
    return jnp.dot(a, b, preferred_element_type=F32)


def _inproj_gmlp_kernel(x_ref, gmix_ref, win_ref, lng_ref, lnb_ref, ws_ref, bs_ref, gout_ref,
                        mgm_ref, us5_ref, y_scr):
    tm = x_ref.shape[0]
    gw = mgm_ref.shape[1]
    hd_dim = gw // GM_HEADS
    n_chunks = tm // CHUNK
    h = _rms(x_ref[...], gmix_ref[...]).astype(BF16)
    proj = _dot(h, win_ref[...])
    us5_ref[...] = proj[:, 2 * gw:]
    u = _gelu(proj[:, :gw])
    v = _gelu(proj[:, gw:2 * gw])
    for hd in range(GM_HEADS):
        lo = hd * hd_dim
        vh = v[:, lo:lo + hd_dim]
        mu = jnp.mean(vh, axis=-1, keepdims=True)
        xc = vh - mu
        var = jnp.mean(xc * xc, axis=-1, keepdims=True)
        vn = (xc * lax.rsqrt(var + EPS) * lng_ref[:, lo:lo + hd_dim]
              + lnb_ref[:, lo:lo + hd_dim]).astype(BF16)
        rhs = jnp.concatenate([vn[c * CHUNK:(c + 1) * CHUNK] for c in range(n_chunks)], axis=1)
        s = _dot(ws_ref[hd], rhs)
        for c in range(n_chunks):
            sc = s[:, c * hd_dim:(c + 1) * hd_dim] + bs_ref[hd]
            y_scr[c * CHUNK:(c + 1) * CHUNK, lo:lo + hd_dim] = u[c * CHUNK:(c + 1) * CHUNK, lo:lo + hd_dim] * sc
    mgm_ref[...] = _rms(y_scr[...], gout_ref[...]).astype(BF16)


def _inproj_gmlp(x2d, gmix, win_bf, lng, lnb, ws_bf, bs_b, gout):
    t, d = x2d.shape
    d_in = win_bf.shape[1]
    gw = lng.shape[1]
    s5w = d_in - 2 * gw
    tm = TM_PROJ
    const = lambda *shape: pl.BlockSpec(shape, lambda i: (0,) * len(shape))
    return pl.pallas_call(
        _inproj_gmlp_kernel,
        grid=(t // tm,),
        in_specs=[
            pl.BlockSpec((tm, d), lambda i: (i, 0)),
            const(1, d), const(d, d_in), const(1, gw), const(1, gw),
            const(GM_HEADS, CHUNK, CHUNK), const(GM_HEADS, CHUNK, gw // GM_HEADS), const(1, gw),
        ],
        out_specs=[pl.BlockSpec((tm, gw), lambda i: (i, 0)),
                   pl.BlockSpec((tm, s5w), lambda i: (i, 0))],
        out_shape=[jax.ShapeDtypeStruct((t, gw), BF16),
                   jax.ShapeDtypeStruct((t, s5w), F32)],
        scratch_shapes=[pltpu.VMEM((tm, gw), F32)],
        compiler_params=pltpu.CompilerParams(dimension_semantics=("parallel",),
                                             vmem_limit_bytes=VMEM_LIMIT),
        name="inproj_gmlp",
    )(x2d, gmix, win_bf, lng, lnb, ws_bf, bs_b, gout)


def _s5_consts(lam_re, lam_im, log_step, b_re, b_im, c_re, c_im, lc):
    lr = jnp.minimum(lam_re.astype(F32), LAMBDA_RE_MAX)
    li = lam_im.astype(F32)
    step = jnp.exp(log_step.astype(F32))[:, None]
    dr, di = lr * step, li * step
    ar, ai = _cexp(dr, di)
    nr, ni = ar - 1.0, ai
    den = lr * lr + li * li
    qr, qi = (nr * lr + ni * li) / den, (ni * lr - nr * li) / den
    br, bi = b_re.astype(F32), b_im.astype(F32)
    bbr = qr[..., None] * br - qi[..., None] * bi
    bbi = qr[..., None] * bi + qi[..., None] * br
    k = jnp.arange(lc + 1, dtype=F32)[:, None, None]
    pwr, pwi = _cexp(k * dr[None], k * di[None])
    return (dr, di), (pwr, pwi), (bbr, bbi), (c_re.astype(F32), c_im.astype(F32))


def _cexp(zr, zi):
    m = jnp.exp(zr)
    return m * jnp.cos(zi), m * jnp.sin(zi)


def _s5_operator(fwd, bwd, lc, seg_steps):
    consts = [_s5_consts(*fwd, lc), _s5_consts(*bwd, lc)]
    g, p, h = consts[0][2][0].shape
    s_idx = jnp.arange(lc)[:, None]
    j_idx = jnp.arange(lc)[None, :]
    lag = j_idx - s_idx
    m = 0.0
    w1_parts, w2_parts, sc_rows, seg_rows = [], [], [], []
    for direction, (ld, pw, bb, c) in enumerate(consts):
        (dr, di), (pwr, pwi), (bbr, bbi), (cr, ci) = ld, pw, bb, c
        cpr = cr[None] * pwr[:, :, None, :] - ci[None] * pwi[:, :, None, :]
        cpi = cr[None] * pwi[:, :, None, :] + ci[None] * pwr[:, :, None, :]
        kk = (jnp.einsum('kghp,gpi->gkhi', cpr[:lc], bbr) - jnp.einsum('kghp,gpi->gkhi', cpi[:lc], bbi))
        lg = lag if direction == 0 else -lag
        tk = jnp.where((lg >= 0)[None, :, :, None, None], kk[:, jnp.clip(lg, 0, lc - 1)], 0.0)
        m = m + tk.transpose(0, 1, 4, 2, 3).reshape(g, lc * h, lc * h)
        er, ei = (pwr[:lc][::-1], pwi[:lc][::-1]) if direction == 0 else (pwr[:lc], pwi[:lc])
        wr = er[:, :, None, :] * bbr.transpose(0, 2, 1)[None] - ei[:, :, None, :] * bbi.transpose(0, 2, 1)[None]
        wi = er[:, :, None, :] * bbi.transpose(0, 2, 1)[None] + ei[:, :, None, :] * bbr.transpose(0, 2, 1)[None]
        wr, wi = wr.transpose(1, 0, 2, 3), wi.transpose(1, 0, 2, 3)
        w1_parts += [wr, wi, wi, wr]
        sel = slice(1, lc + 1)
        fr, fi = (cpr[sel], cpi[sel]) if direction == 0 else (cpr[sel][::-1], cpi[sel][::-1])
        w2_parts += [fr.transpose(1, 3, 0, 2), -fi.transpose(1, 3, 0, 2)]

        def mult(zr, zi):
            return [jnp.concatenate([zr, zr], -1), jnp.concatenate([-zi, zi], -1)]

        sc_rows += mult(*_cexp(lc * dr, lc * di))
        seg_rows += mult(*_cexp((lc * seg_steps) * dr, (lc * seg_steps) * di))
    w1 = jnp.concatenate(w1_parts, axis=-1).reshape(g, lc * h, 8 * p)
    w2 = jnp.concatenate(w2_parts, axis=1).reshape(g, 4 * p, lc * h)
    sc = jnp.stack(sc_rows + seg_rows, axis=1)
    return m.astype(BF16), w1.astype(BF16), w2.astype(BF16), sc.astype(F32)


def _s5_kernel(x_ref, m_ref, w1_ref, w2_ref, sc_ref, y_ref, loc_scr, sin_scr, *, n_seg):
    rows = x_ref.shape[1]
    steps = rows // SUBLANES
    sw = sc_ref.shape[2]
    x = x_ref[0]
    loc_scr[...] = _dot(x, w1_ref[0])

    def bc(i):
        return jnp.broadcast_to(sc_ref[0, i:i + 1, :], (SUBLANES, sw))

    a1f, a2f, a1b, a2b, p1f, p2f, p1b, p2b = [bc(i) for i in range(8)]

    def step_f(s, f, fs):
        r = pl.multiple_of(s * SUBLANES, SUBLANES)
        lf = loc_scr[pl.ds(r, SUBLANES), 0:sw]
        lfs = loc_scr[pl.ds(r, SUBLANES), sw:2 * sw]
        return a1f * f + a2f * fs + lf, a1f * fs - a2f * f + lfs

    def step_b(s, b, bs):
        r = pl.multiple_of((steps - 1 - s) * SUBLANES, SUBLANES)
        lb = loc_scr[pl.ds(r, SUBLANES), 2 * sw:3 * sw]
        lbs = loc_scr[pl.ds(r, SUBLANES), 3 * sw:4 * sw]
        return a1b * b + a2b * bs + lb, a1b * bs - a2b * b + lbs

    zero = jnp.zeros((SUBLANES, sw), F32)

    def pass1(s, carry):
        f, fs, b, bs = carry
        return step_f(s, f, fs) + step_b(s, b, bs)

    f_end, fs_end, b_end, bs_end = lax.fori_loop(0, steps, pass1, (zero, zero, zero, zero), unroll=4)

    seg = lax.broadcasted_iota(jnp.int32, (SUBLANES, sw), 0) % n_seg
    cf, cfs, cb, cbs = zero, zero, zero, zero
    for _ in range(n_seg - 1):
        ef = f_end + p1f * cf + p2f * cfs
        efs = fs_end + p1f * cfs - p2f * cf
        eb = b_end + p1b * cb + p2b * cbs
        ebs = bs_end + p1b * cbs - p2b * cb
        cf = jnp.where(seg >= 1, pltpu.roll(ef, 1, 0), 0.0)
        cfs = jnp.where(seg >= 1, pltpu.roll(efs, 1, 0), 0.0)
        cb = jnp.where(seg <= n_seg - 2, pltpu.roll(eb, SUBLANES - 1, 0), 0.0)
        cbs = jnp.where(seg <= n_seg - 2, pltpu.roll(ebs, SUBLANES - 1, 0), 0.0)

    def pass2(s, carry):
        f, fs, b, bs = carry
        rf = pl.multiple_of(s * SUBLANES, SUBLANES)
        rb = pl.multiple_of((steps - 1 - s) * SUBLANES, SUBLANES)
        sin_scr[pl.ds(rf, SUBLANES), 0:sw] = f
        sin_scr[pl.ds(rb, SUBLANES), sw:2 * sw] = b
        return step_f(s, f, fs) + step_b(s, b, bs)

    lax.fori_loop(0, steps, pass2, (cf, cfs, cb, cbs), unroll=4)

    y_ref[0] = _dot(x, m_ref[0]) + _dot(sin_scr[...].astype(BF16), w2_ref[0])


def _s5_scan(xg, m, w1, w2, sc, n_seg):
    g, rows, kw = xg.shape
    sw = sc.shape[2]
    blk = lambda a: pl.BlockSpec((1,) + a.shape[1:], lambda i: (i, 0, 0))
    return pl.pallas_call(
        functools.partial(_s5_kernel, n_seg=n_seg),
        grid=(g,),
        in_specs=[blk(xg), blk(m), blk(w1), blk(w2), blk(sc)],
        out_specs=pl.BlockSpec((1, rows, kw), lambda i: (i, 0, 0)),
        out_shape=jax.ShapeDtypeStruct((g, rows, kw), F32),
        scratch_shapes=[pltpu.VMEM((rows, 4 * sw), F32), pltpu.VMEM((rows, 2 * sw), F32)],
        compiler_params=pltpu.CompilerParams(dimension_semantics=("parallel",),
                                             vmem_limit_bytes=VMEM_LIMIT),
        name="s5_scan",
    )(xg, m, w1, w2, sc)


S5_NM = 8


def _block_transpose8(v, width):
    lane = lax.broadcasted_iota(jnp.int32, v[0].shape, 1)
    for d in (4, 2, 1):
        w = width * d
        hi = ((lane // w) % 2) == 1
        out = list(v)
        for i0 in range(8):
            if i0 & d:
                continue
            i1 = i0 + d
            out[i0] = jnp.where(hi, pltpu.roll(v[i1], w, 1), v[i0])
            out[i1] = jnp.where(hi, v[i1], pltpu.roll(v[i0], 8 * width - w, 1))
        v = out
    return v


def _tile_copies(hbm4, tile, buf, slot, sem, nm, to_hbm):
    copies = []
    for c in range(SUBLANES):
        for j in range(S5_LC):
            h = hbm4.at[c, pl.ds(tile * nm, nm), pl.ds(j, 1), :]
            v = buf.at[slot, j, :, pl.ds(c, 1), :]
            copies.append(pltpu.make_async_copy(v, h, sem.at[slot]) if to_hbm
                          else pltpu.make_async_copy(h, v, sem.at[slot]))
    return copies


def _s5_inproj_kernel(x4_ref, gmix_ref, w_ref, xg_ref, xs, sem, *, nm):
    i = pl.program_id(0)
    n = pl.num_programs(0)
    slot = i % 2
    dm = x4_ref.shape[3]

    @pl.when(i == 0)
    def _():
        for cp in _tile_copies(x4_ref, 0, xs, 0, sem, nm, False):
            cp.start()

    @pl.when(i + 1 < n)
    def _():
        for cp in _tile_copies(x4_ref, i + 1, xs, 1 - slot, sem, nm, False):
            cp.start()

    pltpu.make_async_copy(xs.at[slot], xs.at[slot], sem.at[slot]).wait()
    rows = nm * SUBLANES
    x = xs[slot].reshape(S5_LC * rows, dm)
    z = _dot(_rms(x, gmix_ref[...]).astype(BF16), w_ref[...])
    n_oct = z.shape[1] // LANES
    for q in range(n_oct):
        for a in range(S5_LC // 8):
            blocks = [z[(8 * a + j8) * rows:(8 * a + j8 + 1) * rows, q * LANES:(q + 1) * LANES] for j8 in range(8)]
            for g8, b in enumerate(_block_transpose8(blocks, S5_GROUP)):
                xg_ref[8 * q + g8, :, a * LANES:(a + 1) * LANES] = b.astype(BF16)


def _s5_inproj(x, gmix, w_s5_bf, n_seg):
    b, l, dm = x.shape
    steps = l // (S5_LC * n_seg)
    nm = S5_NM
    s5w = w_s5_bf.shape[1]
    g = s5w // S5_GROUP
    x4 = x.reshape(b * n_seg, steps, S5_LC, dm)
    return pl.pallas_call(
        functools.partial(_s5_inproj_kernel, nm=nm),
        grid=(steps // nm,),
        in_specs=[pl.BlockSpec(memory_space=pl.ANY),
                  pl.BlockSpec((1, dm), lambda i: (0, 0)),
                  pl.BlockSpec((dm, s5w), lambda i: (0, 0))],
        out_specs=pl.BlockSpec((g, nm * SUBLANES, S5_LC * S5_GROUP), lambda i: (0, i, 0)),
        out_shape=jax.ShapeDtypeStruct((g, steps * SUBLANES, S5_LC * S5_GROUP), BF16),
        scratch_shapes=[pltpu.VMEM((2, S5_LC, nm, SUBLANES, dm), F32), pltpu.SemaphoreType.DMA((2,))],
        compiler_params=pltpu.CompilerParams(dimension_semantics=("arbitrary",),
                                             vmem_limit_bytes=VMEM_LIMIT),
        name="s5_inproj",
    )(x4, gmix, w_s5_bf)


def _s5_to_tokens_kernel(yg_ref, ys4_ref, zs, sem, *, nm):
    i = pl.program_id(0)
    n = pl.num_programs(0)
    slot = i % 2
    rows = nm * SUBLANES

    def wait(s):
        pltpu.make_async_copy(zs.at[s], zs.at[s], sem.at[s]).wait()

    @pl.when(i >= 2)
    def _():
        wait(slot)

    n_oct = yg_ref.shape[0] // 8
    for q in range(n_oct):
        for a in range(S5_LC // 8):
            blocks = [yg_ref[8 * q + g8, :, a * LANES:(a + 1) * LANES] for g8 in range(8)]
            for j8, b in enumerate(_block_transpose8(blocks, S5_GROUP)):
                zs[slot, 8 * a + j8, :, :, q * LANES:(q + 1) * LANES] = b.reshape(nm, SUBLANES, LANES)
    for cp in _tile_copies(ys4_ref, i, zs, slot, sem, nm, True):
        cp.start()

    @pl.when(i == n - 1)
    def _():
        wait(1 - slot)
        wait(slot)


def _s5_to_tokens(yg, b, l, n_seg):
    g, rows_total, kw = yg.shape
    steps = rows_total // SUBLANES
    nm = S5_NM
    s5w = g * S5_GROUP
    assert steps // nm >= 2
    ys4 = pl.pallas_call(
        functools.partial(_s5_to_tokens_kernel, nm=nm),
        grid=(steps // nm,),
        in_specs=[pl.BlockSpec((g, nm * SUBLANES, kw), lambda i: (0, i, 0))],
        out_specs=pl.BlockSpec(memory_space=pl.ANY),
        out_shape=jax.ShapeDtypeStruct((b * n_seg, steps, S5_LC, s5w), F32),
        scratch_shapes=[pltpu.VMEM((2, S5_LC, nm, SUBLANES, s5w), F32), pltpu.SemaphoreType.DMA((2,))],
        compiler_params=pltpu.CompilerParams(dimension_semantics=("arbitrary",),
                                             vmem_limit_bytes=VMEM_LIMIT),
        name="s5_to_tokens",
    )(yg)
    return ys4.reshape(b * l, s5w)


R_E1, R_E2, R_W1, R_W2, R_RANK1, R_RANK2 = range(6)


def _mix_route_kernel(ys_ref, us5_ref, mgm_ref, x_ref, d_ref, gluw_ref, glub_ref, gs5_ref,
                      wout_ref, gffn_ref, rwh_ref, rwl_ref, rb_ref, tri_ref,
                      x2_ref, route_ref, cnt_ref):
    gw = mgm_ref.shape[1]

    @pl.when(pl.program_id(0) == 0)
    def _():
        cnt_ref[...] = jnp.zeros_like(cnt_ref)

    y = ys_ref[...] + d_ref[...] * us5_ref[...]
    g = _gelu(y)
    z = g * jax.nn.sigmoid(_dot(g.astype(BF16), gluw_ref[...]) + glub_ref[...])
    ms5 = _rms(z, gs5_ref[...]).astype(BF16)
    mix = _dot(mgm_ref[...], wout_ref[:gw, :]) + _dot(ms5, wout_ref[gw:, :])
    x2 = x_ref[...] + mix
    x2_ref[...] = x2
    t = _rms(x2, gffn_ref[...])
    t_hi = t.astype(BF16)
    t_lo = (t - t_hi.astype(F32)).astype(BF16)
    logits = (_dot(t_hi, rwh_ref[...]) + _dot(t_hi, rwl_ref[...]) + _dot(t_lo, rwh_ref[...])
              + rb_ref[...])
    lane = lax.broadcasted_iota(jnp.int32, logits.shape, 1)
    neg = jnp.float32(-jnp.inf)
    big = jnp.int32(LANES)

    def first_max(mask):
        vals = jnp.where(mask, logits, neg)
        mx = jnp.max(vals, axis=-1, keepdims=True)
        idx = jnp.min(jnp.where(mask & (vals == mx), lane, big), axis=-1, keepdims=True)
        return mx, idx

    coarse = lane < N_COARSE
    m1, grp = first_max(coarse)
    p_grp = 1.0 / jnp.sum(jnp.where(coarse, jnp.exp(logits - m1), 0.0), axis=-1, keepdims=True)
    lo = N_COARSE + grp * N_FINE
    fine = (lane >= lo) & (lane < lo + N_FINE)
    v1, i1 = first_max(fine)
    v2, i2 = first_max(fine & (lane != i1))
    e21 = jnp.exp(v2 - v1)
    w1 = p_grp / (1.0 + e21)
    w2 = p_grp * e21 / (1.0 + e21)
    e1 = i1 - N_COARSE
    e2 = i2 - N_COARSE
    hit1 = lane == e1
    hit2 = lane == e2
    onehot = jnp.where(hit1 | hit2, 1.0, 0.0)
    before = _dot(tri_ref[...], onehot.astype(BF16)) + cnt_ref[...]
    rank1 = jnp.sum(jnp.where(hit1, before, 0.0), axis=-1, keepdims=True)
    rank2 = jnp.sum(jnp.where(hit2, before, 0.0), axis=-1, keepdims=True)
    tm = onehot.shape[0]
    cnt_ref[...] = before[tm - 1:tm, :] + onehot[tm - 1:tm, :]
    rec = jnp.zeros_like(logits)
    for slot, val in ((R_E1, e1.astype(F32)), (R_E2, e2.astype(F32)), (R_W1, w1), (R_W2, w2),
                      (R_RANK1, rank1), (R_RANK2, rank2)):
        rec = jnp.where(lane == slot, val, rec)
    route_ref[...] = rec


def _mix_route(ys, us5, mgm, x2d, d, gluw_bf, glub, gs5, wout_bf, gffn, rwh, rwl, rb):
    t, dm = x2d.shape
    gw = mgm.shape[1]
    s5w = us5.shape[1]
    tm = TM_MIX
    tri = jnp.tril(jnp.ones((tm, tm), F32), -1).astype(BF16)
    const = lambda *shape: pl.BlockSpec(shape, lambda i: (0,) * len(shape))
    tile = lambda w: pl.BlockSpec((tm, w), lambda i: (i, 0))
    return pl.pallas_call(
        _mix_route_kernel,
        grid=(t // tm,),
        in_specs=[tile(s5w), tile(s5w), tile(gw), tile(dm),
                  const(1, s5w), const(s5w, s5w), const(1, s5w), const(1, s5w),
                  const(gw + s5w, dm), const(1, dm), const(dm, LANES), const(dm, LANES), const(1, LANES),
                  const(tm, tm)],
        out_specs=[tile(dm), tile(LANES), const(1, LANES)],
        out_shape=[jax.ShapeDtypeStruct((t, dm), F32),
                   jax.ShapeDtypeStruct((t, LANES), F32),
                   jax.ShapeDtypeStruct((1, LANES), F32)],
        compiler_params=pltpu.CompilerParams(dimension_semantics=("arbitrary",),
                                             vmem_limit_bytes=VMEM_LIMIT),
        name="mix_route",
    )(ys, us5, mgm, x2d, d, gluw_bf, glub, gs5, wout_bf, gffn, rwh, rwl, rb, tri)


def _row_copy(src, s_row, dst, d_row, sem):
    return pltpu.make_async_copy(src.at[pl.ds(s_row, 1)], dst.at[pl.ds(d_row, 1)], sem)


def _scatter_rows_kernel(pos_ref, t_ref, sorted_in_ref, sorted_ref, sem):
    del sorted_in_ref
    tm = t_ref.shape[0]

    def issue(r, c):
        _row_copy(t_ref, r, sorted_ref, pos_ref[0, 0, r], sem).start()
        _row_copy(t_ref, r, sorted_ref, pos_ref[0, 0, tm + r], sem).start()
        return c

    lax.fori_loop(0, tm, issue, 0, unroll=8)
    for _ in range(2):
        pltpu.make_async_copy(t_ref, sorted_ref.at[pl.ds(0, tm)], sem).wait()


def _scatter_rows(x2, pos, n_sorted):
    t, w = x2.shape
    tm = TM_ROWS
    return pl.pallas_call(
        _scatter_rows_kernel,
        grid=(t // tm,),
        in_specs=[pl.BlockSpec((1, 1, 2 * tm), lambda i: (i, 0, 0), memory_space=pltpu.SMEM),
                  pl.BlockSpec((tm, w), lambda i: (i, 0)),
                  pl.BlockSpec(memory_space=pl.ANY)],
        out_specs=pl.BlockSpec(memory_space=pl.ANY),
        out_shape=jax.ShapeDtypeStruct((n_sorted, w), F32),
        scratch_shapes=[pltpu.SemaphoreType.DMA(())],
        input_output_aliases={2: 0},
        compiler_params=pltpu.CompilerParams(dimension_semantics=("arbitrary",),
                                             vmem_limit_bytes=VMEM_LIMIT),
        name="scatter_rows",
    )(pos, x2, jnp.zeros((n_sorted, w), F32))


def _experts_kernel(te_ref, nt_ref, xs_ref, gffn_ref, wgu_ref, wd_ref, ys_ref):
    i = pl.program_id(0)
    de = wd_ref.shape[1]

    @pl.when(i < nt_ref[0])
    def _():
        t = _rms(xs_ref[...], gffn_ref[...]).astype(BF16)
        gu = _dot(t, wgu_ref[0])
        hidden = (jax.nn.silu(gu[:, :de]) * gu[:, de:]).astype(BF16)
        ys_ref[...] = _dot(hidden, wd_ref[0])

    @pl.when(i >= nt_ref[0])
    def _():
        ys_ref[...] = jnp.zeros_like(ys_ref)


def _experts(tile_expert, n_tiles, x_sorted, gffn, wgu_bf, wd_bf):
    n_sorted, dm = x_sorted.shape
    de2 = wgu_bf.shape[2]
    tm = TM_EXPERT
    return pl.pallas_call(
        _experts_kernel,
        grid_spec=pltpu.PrefetchScalarGridSpec(
            num_scalar_prefetch=2,
            grid=(n_sorted // tm,),
            in_specs=[pl.BlockSpec((tm, dm), lambda i, te, nt: (i, 0)),
                      pl.BlockSpec((1, dm), lambda i, te, nt: (0, 0)),
                      pl.BlockSpec((1, dm, de2), lambda i, te, nt: (te[i], 0, 0)),
                      pl.BlockSpec((1, de2 // 2, dm), lambda i, te, nt: (te[i], 0, 0))],
            out_specs=pl.BlockSpec((tm, dm), lambda i, te, nt: (i, 0)),
        ),
        out_shape=jax.ShapeDtypeStruct((n_sorted, dm), F32),
        compiler_params=pltpu.CompilerParams(dimension_semantics=("arbitrary",),
                                             vmem_limit_bytes=VMEM_LIMIT),
        name="experts",
    )(tile_expert, n_tiles, x_sorted, gffn, wgu_bf, wd_bf)


def _combine_kernel(pos_ref, x2_ref, route_ref, gfin_ref, ys_ref, o_ref, rows_scr, sem):
    tm = x2_ref.shape[0]

    def issue(r, c):
        _row_copy(ys_ref, pos_ref[0, 0, r], rows_scr, r, sem).start()
        _row_copy(ys_ref, pos_ref[0, 0, tm + r], rows_scr, tm + r, sem).start()
        return c

    lax.fori_loop(0, tm, issue, 0, unroll=8)
    pltpu.make_async_copy(ys_ref.at[pl.ds(0, 2 * tm)], rows_scr, sem).wait()
    route = route_ref[...]
    w1 = route[:, R_W1:R_W1 + 1]
    w2 = route[:, R_W2:R_W2 + 1]
    moe = w1 * rows_scr[:tm, :] + w2 * rows_scr[tm:, :]
    o_ref[...] = _rms(x2_ref[...] + moe, gfin_ref[...])


def _combine(pos, x2, route, gfin, y_sorted):
    t, dm = x2.shape
    tm = TM_ROWS
    return pl.pallas_call(
        _combine_kernel,
        grid=(t // tm,),
        in_specs=[pl.BlockSpec((1, 1, 2 * tm), lambda i: (i, 0, 0), memory_space=pltpu.SMEM),
                  pl.BlockSpec((tm, dm), lambda i: (i, 0)),
                  pl.BlockSpec((tm, LANES), lambda i: (i, 0)),
                  pl.BlockSpec((1, dm), lambda i: (0, 0)),
                  pl.BlockSpec(memory_space=pl.ANY)],
        out_specs=pl.BlockSpec((tm, dm), lambda i: (i, 0)),
        out_shape=jax.ShapeDtypeStruct((t, dm), F32),
        scratch_shapes=[pltpu.VMEM((2 * tm, dm), F32), pltpu.SemaphoreType.DMA(())],
        compiler_params=pltpu.CompilerParams(dimension_semantics=("arbitrary",),
                                             vmem_limit_bytes=VMEM_LIMIT),
        name="combine_norm",
    )(pos, x2, route, gfin, y_sorted)


def _sorted_positions(route, counts, tm_rows, tm_expert):
    t = route.shape[0]
    cnt = counts[0, :N_EXPERTS].astype(jnp.int32)
    tiles = (cnt + tm_expert - 1) // tm_expert
    tile_end = jnp.cumsum(tiles)
    base = ((tile_end - tiles) * tm_expert).astype(F32)
    n_tiles_max = (2 * t) // tm_expert + N_EXPERTS
    experts = jnp.arange(N_EXPERTS, dtype=F32)

    def pos_of(e_lane, r_lane):
        e = route[:, e_lane]
        onehot = e[:, None] == experts[None, :]
        return (route[:, r_lane] + jnp.sum(jnp.where(onehot, base[None, :], 0.0), axis=1)).astype(jnp.int32)

    pos = jnp.concatenate([pos_of(R_E1, R_RANK1).reshape(t // tm_rows, tm_rows),
                           pos_of(R_E2, R_RANK2).reshape(t // tm_rows, tm_rows)], axis=1)[:, None, :]
    tile_idx = jnp.arange(n_tiles_max, dtype=jnp.int32)
    tile_expert = jnp.sum((tile_idx[:, None] >= tile_end[None, :]).astype(jnp.int32), axis=1)
    n_tiles = tile_end[-1:].astype(jnp.int32)
    last = jnp.sum((n_tiles - 1 >= tile_end).astype(jnp.int32))
    tile_expert = jnp.where(tile_idx < n_tiles, tile_expert, last).astype(jnp.int32)
    return pos, tile_expert, n_tiles, n_tiles_max * tm_expert


def _layer(x, p, s5_ops, gfin):
    b, l, dm = x.shape
    x2d = x.reshape(b * l, dm)
    mgm, us5 = _inproj_gmlp(x2d, p['gmix'], p['win'], p['lng'], p['lnb'], p['ws'], p['bs'], p['gout_gm'])
    n_seg = SUBLANES // b
    m, w1, w2, sc = s5_ops[(l // (S5_LC * n_seg))]
    xg = _s5_inproj(x, p['gmix'], p['win_s5'], n_seg)
    yg = _s5_scan(xg, m, w1, w2, sc, n_seg)
    ys = _s5_to_tokens(yg, b, l, n_seg)
    x2, route, counts = _mix_route(ys, us5, mgm, x2d, p['d'], p['gluw'], p['glub'], p['gout_s5'],
                                   p['wout'], p['gffn'], p['rwh'], p['rwl'], p['rb'])
    pos, tile_expert, n_tiles, n_sorted = _sorted_positions(route, counts, TM_ROWS, TM_EXPERT)
    x_sorted = _scatter_rows(x2, pos, n_sorted)
    y_sorted = _experts(tile_expert, n_tiles, x_sorted, p['gffn'], p['wgu'], p['wd'])
    out = _combine(pos, x2, route, gfin, y_sorted)
    return out.reshape(b, l, dm)


def kernel(x_prompt, x_sample, norm_mix_g, w_in, gm_ln_g, gm_ln_b, gm_ws, gm_bs, s5_lam_re_fwd, s5_lam_im_fwd, s5_log_step_fwd, s5_b_re_fwd, s5_b_im_fwd, s5_c_re_fwd, s5_c_im_fwd, s5_lam_re_bwd, s5_lam_im_bwd, s5_log_step_bwd, s5_b_re_bwd, s5_b_im_bwd, s5_c_re_bwd, s5_c_im_bwd, s5_d, s5_glu_w, s5_glu_b, out_norm_gm, out_norm_s5, w_out, norm_ffn_g, r1_w, r1_b, r2_w, r2_b, e_w_gate, e_w_up, e_w_down, norm_final_g):
    depth = w_in.shape[0]
    gfin = norm_final_g.reshape(1, -1).astype(F32)
    xs = [x_prompt, x_sample]
    for li in range(depth):
        row = lambda a: a[li].reshape(1, -1).astype(F32)
        dm = w_in.shape[1]
        gw = gm_ln_g.shape[1]
        hd_dim = gw // GM_HEADS
        rw = jnp.concatenate([r1_w[li], r2_w[li].transpose(1, 0, 2).reshape(dm, N_EXPERTS)], axis=1).astype(F32)
        rw = jnp.pad(rw, ((0, 0), (0, LANES - rw.shape[1])))
        rwh = rw.astype(BF16)
        rwl = (rw - rwh.astype(F32)).astype(BF16)
        rb = jnp.concatenate([r1_b[li], r2_b[li].reshape(-1)]).astype(F32)
        rb = jnp.pad(rb, (0, LANES - rb.shape[0])).reshape(1, LANES)
        p = dict(
            gmix=row(norm_mix_g), win=w_in[li].astype(BF16), win_s5=w_in[li][:, 2 * gw:].astype(BF16),
            lng=row(gm_ln_g), lnb=row(gm_ln_b),
            ws=gm_ws[li].astype(BF16),
            bs=jnp.broadcast_to(gm_bs[li].astype(F32)[:, :, None], (GM_HEADS, CHUNK, hd_dim)),
            gout_gm=row(out_norm_gm), d=row(s5_d), gluw=s5_glu_w[li].astype(BF16), glub=row(s5_glu_b),
            gout_s5=row(out_norm_s5), wout=w_out[li].astype(BF16), gffn=row(norm_ffn_g),
            rwh=rwh, rwl=rwl, rb=rb,
            wgu=jnp.concatenate([e_w_gate[li], e_w_up[li]], axis=-1).astype(BF16),
            wd=e_w_down[li].astype(BF16),
        )
        fwd = (s5_lam_re_fwd[li], s5_lam_im_fwd[li], s5_log_step_fwd[li], s5_b_re_fwd[li], s5_b_im_fwd[li],
               s5_c_re_fwd[li], s5_c_im_fwd[li])
        bwd = (s5_lam_re_bwd[li], s5_lam_im_bwd[li], s5_log_step_bwd[li], s5_b_re_bwd[li], s5_b_im_bwd[li],
               s5_c_re_bwd[li], s5_c_im_bwd[li])
        s5_ops = {}
        for x in xs:
            seg_steps = x.shape[1] // (S5_LC * (SUBLANES // x.shape[0]))
            if seg_steps not in s5_ops:
                s5_ops[seg_steps] = _s5_operator(fwd, bwd, S5_LC, seg_steps)
        last = li == depth - 1
        assert last, "depth > 1 needs an un-normalised layer output"
        xs = [_layer(x, p, s5_ops, gfin) for x in xs]
    return tuple(xs)
```

```python
import functools
import math

import jax
import jax.numpy as jnp
from jax import lax
from jax.experimental import pallas as pl
from jax.experimental.pallas import tpu as pltpu

F32 = jnp.float32
BF16 = jnp.bfloat16

EPS = 1e-6
LAMBDA_RE_MAX = -1e-4
GM_HEADS = 4
CHUNK = 128
S5_GROUP = 16
S5_STATE = 64
N_COARSE = 4
N_FINE = 8
N_EXPERTS = N_COARSE * N_FINE

LANES = 128
SUBLANES = 8
S5_LC = 16
VMEM_LIMIT = 56 * 1024 * 1024

TM_PROJ = 512
TM_MIX = 512
TM_EXPERT = 256
SEG_ALIGN = 16


def _gelu(x):
    c = math.sqrt(2.0 / math.pi)
    return x * (0.5 * (1.0 + jnp.tanh(c * (x + 0.044715 * (x * x * x)))))


def _rms(x, g):
    ms = jnp.mean(x * x, axis=-1, keepdims=True)
    return x * lax.rsqrt(ms + EPS) * g


def _dot(a, b):
    return jnp.dot(a, b, preferred_element_type=F32)


def _inproj_gmlp_kernel(x_ref, gmix_ref, win_ref, lng_ref, lnb_ref, ws_ref, bs_ref, gout_ref,
                        mgm_ref, us5_ref, y_scr):
    tm = x_ref.shape[0]
    gw = mgm_ref.shape[1]
    hd_dim = gw // GM_HEADS
    n_chunks = tm // CHUNK
    h = _rms(x_ref[...], gmix_ref[...]).astype(BF16)
    proj = _dot(h, win_ref[...])
    us5_ref[...] = proj[:, 2 * gw:]
    u = _gelu(proj[:, :gw])
    v = _gelu(proj[:, gw:2 * gw])
    for hd in range(GM_HEADS):
        lo = hd * hd_dim
        vh = v[:, lo:lo + hd_dim]
        mu = jnp.mean(vh, axis=-1, keepdims=True)
        xc = vh - mu
        var = jnp.mean(xc * xc, axis=-1, keepdims=True)
        vn = (xc * lax.rsqrt(var + EPS) * lng_ref[:, lo:lo + hd_dim]
              + lnb_ref[:, lo:lo + hd_dim]).astype(BF16)
        rhs = jnp.concatenate([vn[c * CHUNK:(c + 1) * CHUNK] for c in range(n_chunks)], axis=1)
        s = _dot(ws_ref[hd], rhs)
        for c in range(n_chunks):
            sc = s[:, c * hd_dim:(c + 1) * hd_dim] + bs_ref[hd]
            y_scr[c * CHUNK:(c + 1) * CHUNK, lo:lo + hd_dim] = u[c * CHUNK:(c + 1) * CHUNK, lo:lo + hd_dim] * sc
    mgm_ref[...] = _rms(y_scr[...], gout_ref[...]).astype(BF16)


def _inproj_gmlp(x2d, gmix, win_bf, lng, lnb, ws_bf, bs_b, gout):
    t, d = x2d.shape
    d_in = win_bf.shape[1]
    gw = lng.shape[1]
    s5w = d_in - 2 * gw
    tm = TM_PROJ
    const = lambda *shape: pl.BlockSpec(shape, lambda i: (0,) * len(shape))
    return pl.pallas_call(
        _inproj_gmlp_kernel,
        grid=(t // tm,),
        in_specs=[
            pl.BlockSpec((tm, d), lambda i: (i, 0)),
            const(1, d), const(d, d_in), const(1, gw), const(1, gw),
            const(GM_HEADS, CHUNK, CHUNK), const(GM_HEADS, CHUNK, gw // GM_HEADS), const(1, gw),
        ],
        out_specs=[pl.BlockSpec((tm, gw), lambda i: (i, 0)),
                   pl.BlockSpec((tm, s5w), lambda i: (i, 0))],
        out_shape=[jax.ShapeDtypeStruct((t, gw), BF16),
                   jax.ShapeDtypeStruct((t, s5w), F32)],
        scratch_shapes=[pltpu.VMEM((tm, gw), F32)],
        compiler_params=pltpu.CompilerParams(dimension_semantics=("parallel",),
                                             vmem_limit_bytes=VMEM_LIMIT),
        name="inproj_gmlp",
    )(x2d, gmix, win_bf, lng, lnb, ws_bf, bs_b, gout)


def _s5_consts(lam_re, lam_im, log_step, b_re, b_im, c_re, c_im, lc):
    lr = jnp.minimum(lam_re.astype(F32), LAMBDA_RE_MAX)
    li = lam_im.astype(F32)
    step = jnp.exp(log_step.astype(F32))[:, None]
    dr, di = lr * step, li * step
    ar, ai = _cexp(dr, di)
    nr, ni = ar - 1.0, ai
    den = lr * lr + li * li
    qr, qi = (nr * lr + ni * li) / den, (ni * lr - nr * li) / den
    br, bi = b_re.astype(F32), b_im.astype(F32)
    bbr = qr[..., None] * br - qi[..., None] * bi
    bbi = qr[..., None] * bi + qi[..., None] * br
    k = jnp.arange(lc + 1, dtype=F32)[:, None, None]
    pwr, pwi = _cexp(k * dr[None], k * di[None])
    return (dr, di), (pwr, pwi), (bbr, bbi), (c_re.astype(F32), c_im.astype(F32))


def _cexp(zr, zi):
    m = jnp.exp(zr)
    return m * jnp.cos(zi), m * jnp.sin(zi)


def _s5_operator(fwd, bwd, lc, seg_steps):
    consts = [_s5_consts(*fwd, lc), _s5_consts(*bwd, lc)]
    g, p, h = consts[0][2][0].shape
    m = 0.0
    w1_parts, w2_parts, sc_rows, seg_rows = [], [], [], []
    for direction, (ld, pw, bb, c) in enumerate(consts):
        (dr, di), (pwr, pwi), (bbr, bbi), (cr, ci) = ld, pw, bb, c
        cpr = cr[None] * pwr[:, :, None, :] - ci[None] * pwi[:, :, None, :]
        cpi = cr[None] * pwi[:, :, None, :] + ci[None] * pwr[:, :, None, :]
        kk = (jnp.einsum('kghp,gpi->gkhi', cpr[:lc], bbr) - jnp.einsum('kghp,gpi->gkhi', cpi[:lc], bbi))
        shifted = jnp.stack([jnp.pad(kk[:, :lc - r], ((0, 0), (r, 0), (0, 0), (0, 0))) for r in range(lc)], axis=1)
        tk = shifted if direction == 0 else shifted.transpose(0, 2, 1, 3, 4)
        m = m + tk.transpose(0, 1, 4, 2, 3).reshape(g, lc * h, lc * h)
        er, ei = (pwr[:lc][::-1], pwi[:lc][::-1]) if direction == 0 else (pwr[:lc], pwi[:lc])
        wr = er[:, :, None, :] * bbr.transpose(0, 2, 1)[None] - ei[:, :, None, :] * bbi.transpose(0, 2, 1)[None]
        wi = er[:, :, None, :] * bbi.transpose(0, 2, 1)[None] + ei[:, :, None, :] * bbr.transpose(0, 2, 1)[None]
        wr, wi = wr.transpose(1, 0, 2, 3), wi.transpose(1, 0, 2, 3)
        w1_parts += [wr, wi, wi, wr]
        sel = slice(1, lc + 1)
        fr, fi = (cpr[sel], cpi[sel]) if direction == 0 else (cpr[sel][::-1], cpi[sel][::-1])
        w2_parts += [fr.transpose(1, 3, 0, 2), -fi.transpose(1, 3, 0, 2)]

        def mult(zr, zi):
            return [jnp.concatenate([zr, zr], -1), jnp.concatenate([-zi, zi], -1)]

        sc_rows += mult(*_cexp(lc * dr, lc * di))
        seg_rows += mult(*_cexp((lc * seg_steps) * dr, (lc * seg_steps) * di))
    w1 = jnp.concatenate(w1_parts, axis=-1).reshape(g, lc * h, 8 * p)
    w2 = jnp.concatenate(w2_parts, axis=1).reshape(g, 4 * p, lc * h)
    sc = jnp.stack(sc_rows + seg_rows, axis=1)
    return m.astype(BF16), w1.astype(BF16), w2.astype(BF16), sc.astype(F32)


def _s5_kernel(x_ref, m_ref, w1_ref, w2_ref, sc_ref, y_ref, loc_scr, sin_scr, *, n_seg):
    rows = x_ref.shape[1]
    steps = rows // SUBLANES
    sw = sc_ref.shape[2]
    x = x_ref[0]
    loc_scr[...] = _dot(x, w1_ref[0])

    def bc(i):
        return jnp.broadcast_to(sc_ref[0, i:i + 1, :], (SUBLANES, sw))

    a1f, a2f, a1b, a2b, p1f, p2f, p1b, p2b = [bc(i) for i in range(8)]

    def step_f(s, f, fs):
        r = pl.multiple_of(s * SUBLANES, SUBLANES)
        lf = loc_scr[pl.ds(r, SUBLANES), 0:sw]
        lfs = loc_scr[pl.ds(r, SUBLANES), sw:2 * sw]
        return a1f * f + a2f * fs + lf, a1f * fs - a2f * f + lfs

    def step_b(s, b, bs):
        r = pl.multiple_of((steps - 1 - s) * SUBLANES, SUBLANES)
        lb = loc_scr[pl.ds(r, SUBLANES), 2 * sw:3 * sw]
        lbs = loc_scr[pl.ds(r, SUBLANES), 3 * sw:4 * sw]
        return a1b * b + a2b * bs + lb, a1b * bs - a2b * b + lbs

    zero = jnp.zeros((SUBLANES, sw), F32)

    def pass1(s, carry):
        f, fs, b, bs = carry
        return step_f(s, f, fs) + step_b(s, b, bs)

    f_end, fs_end, b_end, bs_end = lax.fori_loop(0, steps, pass1, (zero, zero, zero, zero), unroll=4)

    seg = lax.broadcasted_iota(jnp.int32, (SUBLANES, sw), 0) % n_seg
    cf, cfs, cb, cbs = zero, zero, zero, zero
    for _ in range(n_seg - 1):
        ef = f_end + p1f * cf + p2f * cfs
        efs = fs_end + p1f * cfs - p2f * cf
        eb = b_end + p1b * cb + p2b * cbs
        ebs = bs_end + p1b * cbs - p2b * cb
        cf = jnp.where(seg >= 1, pltpu.roll(ef, 1, 0), 0.0)
        cfs = jnp.where(seg >= 1, pltpu.roll(efs, 1, 0), 0.0)
        cb = jnp.where(seg <= n_seg - 2, pltpu.roll(eb, SUBLANES - 1, 0), 0.0)
        cbs = jnp.where(seg <= n_seg - 2, pltpu.roll(ebs, SUBLANES - 1, 0), 0.0)

    def pass2(s, carry):
        f, fs, b, bs = carry
        rf = pl.multiple_of(s * SUBLANES, SUBLANES)
        rb = pl.multiple_of((steps - 1 - s) * SUBLANES, SUBLANES)
        sin_scr[pl.ds(rf, SUBLANES), 0:sw] = f
        sin_scr[pl.ds(rb, SUBLANES), sw:2 * sw] = b
        return step_f(s, f, fs) + step_b(s, b, bs)

    lax.fori_loop(0, steps, pass2, (cf, cfs, cb, cbs), unroll=4)

    y_ref[0] = _dot(x, m_ref[0]) + _dot(sin_scr[...].astype(BF16), w2_ref[0])


def _s5_scan(xg, m, w1, w2, sc, n_seg):
    g, rows, kw = xg.shape
    sw = sc.shape[2]
    blk = lambda a: pl.BlockSpec((1,) + a.shape[1:], lambda i: (i, 0, 0))
    return pl.pallas_call(
        functools.partial(_s5_kernel, n_seg=n_seg),
        grid=(g,),
        in_specs=[blk(xg), blk(m), blk(w1), blk(w2), blk(sc)],
        out_specs=pl.BlockSpec((1, rows, kw), lambda i: (i, 0, 0)),
        out_shape=jax.ShapeDtypeStruct((g, rows, kw), F32),
        scratch_shapes=[pltpu.VMEM((rows, 4 * sw), F32), pltpu.VMEM((rows, 2 * sw), F32)],
        compiler_params=pltpu.CompilerParams(dimension_semantics=("parallel",),
                                             vmem_limit_bytes=VMEM_LIMIT),
        name="s5_scan",
    )(xg, m, w1, w2, sc)


S5_NM = 8


def _block_transpose8(v, width):
    lane = lax.broadcasted_iota(jnp.int32, v[0].shape, 1)
    for d in (4, 2, 1):
        w = width * d
        hi = ((lane // w) % 2) == 1
        out = list(v)
        for i0 in range(8):
            if i0 & d:
                continue
            i1 = i0 + d
            out[i0] = jnp.where(hi, pltpu.roll(v[i1], w, 1), v[i0])
            out[i1] = jnp.where(hi, v[i1], pltpu.roll(v[i0], 8 * width - w, 1))
        v = out
    return v


def _tile_copies(hbm4, tile, buf, slot, sem, nm, to_hbm):
    copies = []
    for c in range(SUBLANES):
        for j in range(S5_LC):
            h = hbm4.at[c, pl.ds(tile * nm, nm), pl.ds(j, 1), :]
            v = buf.at[slot, j, :, pl.ds(c, 1), :]
            copies.append(pltpu.make_async_copy(v, h, sem.at[slot]) if to_hbm
                          else pltpu.make_async_copy(h, v, sem.at[slot]))
    return copies


def _s5_inproj_kernel(x4_ref, gmix_ref, w_ref, xg_ref, xs, sem, *, nm):
    i = pl.program_id(0)
    n = pl.num_programs(0)
    slot = i % 2
    dm = x4_ref.shape[3]

    @pl.when(i == 0)
    def _():
        for cp in _tile_copies(x4_ref, 0, xs, 0, sem, nm, False):
            cp.start()

    @pl.when(i + 1 < n)
    def _():
        for cp in _tile_copies(x4_ref, i + 1, xs, 1 - slot, sem, nm, False):
            cp.start()

    pltpu.make_async_copy(xs.at[slot], xs.at[slot], sem.at[slot]).wait()
    rows = nm * SUBLANES
    x = xs[slot].reshape(S5_LC * rows, dm)
    z = _dot(_rms(x, gmix_ref[...]).astype(BF16), w_ref[...])
    n_oct = z.shape[1] // LANES
    for q in range(n_oct):
        for a in range(S5_LC // 8):
            blocks = [z[(8 * a + j8) * rows:(8 * a + j8 + 1) * rows, q * LANES:(q + 1) * LANES] for j8 in range(8)]
            for g8, b in enumerate(_block_transpose8(blocks, S5_GROUP)):
                xg_ref[8 * q + g8, :, a * LANES:(a + 1) * LANES] = b.astype(BF16)


def _s5_inproj(x, gmix, w_s5_bf, n_seg):
    b, l, dm = x.shape
    steps = l // (S5_LC * n_seg)
    nm = S5_NM
    s5w = w_s5_bf.shape[1]
    g = s5w // S5_GROUP
    x4 = x.reshape(b * n_seg, steps, S5_LC, dm)
    return pl.pallas_call(
        functools.partial(_s5_inproj_kernel, nm=nm),
        grid=(steps // nm,),
        in_specs=[pl.BlockSpec(memory_space=pl.ANY),
                  pl.BlockSpec((1, dm), lambda i: (0, 0)),
                  pl.BlockSpec((dm, s5w), lambda i: (0, 0))],
        out_specs=pl.BlockSpec((g, nm * SUBLANES, S5_LC * S5_GROUP), lambda i: (0, i, 0)),
        out_shape=jax.ShapeDtypeStruct((g, steps * SUBLANES, S5_LC * S5_GROUP), BF16),
        scratch_shapes=[pltpu.VMEM((2, S5_LC, nm, SUBLANES, dm), F32), pltpu.SemaphoreType.DMA((2,))],
        compiler_params=pltpu.CompilerParams(dimension_semantics=("arbitrary",),
                                             vmem_limit_bytes=VMEM_LIMIT),
        name="s5_inproj",
    )(x4, gmix, w_s5_bf)


def _s5_to_tokens_kernel(yg_ref, ys4_ref, zs, sem, *, nm):
    i = pl.program_id(0)
    n = pl.num_programs(0)
    slot = i % 2
    rows = nm * SUBLANES

    def wait(s):
        pltpu.make_async_copy(zs.at[s], zs.at[s], sem.at[s]).wait()

    @pl.when(i >= 2)
    def _():
        wait(slot)

    n_oct = yg_ref.shape[0] // 8
    for q in range(n_oct):
        for a in range(S5_LC // 8):
            blocks = [yg_ref[8 * q + g8, :, a * LANES:(a + 1) * LANES] for g8 in range(8)]
            for j8, b in enumerate(_block_transpose8(blocks, S5_GROUP)):
                zs[slot, 8 * a + j8, :, :, q * LANES:(q + 1) * LANES] = b.reshape(nm, SUBLANES, LANES)
    for cp in _tile_copies(ys4_ref, i, zs, slot, sem, nm, True):
        cp.start()

    @pl.when(i == n - 1)
    def _():
        wait(1 - slot)
        wait(slot)


def _s5_to_tokens(yg, b, l, n_seg):
    g, rows_total, kw = yg.shape
    steps = rows_total // SUBLANES
    nm = S5_NM
    s5w = g * S5_GROUP
    assert steps // nm >= 2
    ys4 = pl.pallas_call(
        functools.partial(_s5_to_tokens_kernel, nm=nm),
        grid=(steps // nm,),
        in_specs=[pl.BlockSpec((g, nm * SUBLANES, kw), lambda i: (0, i, 0))],
        out_specs=pl.BlockSpec(memory_space=pl.ANY),
        out_shape=jax.ShapeDtypeStruct((b * n_seg, steps, S5_LC, s5w), F32),
        scratch_shapes=[pltpu.VMEM((2, S5_LC, nm, SUBLANES, s5w), F32), pltpu.SemaphoreType.DMA((2,))],
        compiler_params=pltpu.CompilerParams(dimension_semantics=("arbitrary",),
                                             vmem_limit_bytes=VMEM_LIMIT),
        name="s5_to_tokens",
    )(yg)
    return ys4.reshape(b * l, s5w)


R_E1, R_E2, R_W1, R_W2, R_RANK1, R_RANK2 = range(6)


def _mix_route_kernel(ys_ref, us5_ref, mgm_ref, x_ref, d_ref, gluw_ref, glub_ref, gs5_ref,
                      wout_ref, gffn_ref, rwh_ref, rwl_ref, rb_ref, tri_ref,
                      x2_ref, t_ref, route_ref, cnt_ref):
    gw = mgm_ref.shape[1]
    y = ys_ref[...] + d_ref[...] * us5_ref[...]
    g = _gelu(y)
    z = g * jax.nn.sigmoid(_dot(g.astype(BF16), gluw_ref[...]) + glub_ref[...])
    ms5 = _rms(z, gs5_ref[...]).astype(BF16)
    mix = _dot(mgm_ref[...], wout_ref[:gw, :]) + _dot(ms5, wout_ref[gw:, :])
    x2 = x_ref[...] + mix
    x2_ref[...] = x2
    t = _rms(x2, gffn_ref[...])
    t_hi = t.astype(BF16)
    t_ref[...] = t_hi
    t_lo = (t - t_hi.astype(F32)).astype(BF16)
    logits = (_dot(t_hi, rwh_ref[...]) + _dot(t_hi, rwl_ref[...]) + _dot(t_lo, rwh_ref[...])
              + rb_ref[...])
    lane = lax.broadcasted_iota(jnp.int32, logits.shape, 1)
    neg = jnp.float32(-jnp.inf)
    big = jnp.int32(LANES)

    def first_max(mask):
        vals = jnp.where(mask, logits, neg)
        mx = jnp.max(vals, axis=-1, keepdims=True)
        idx = jnp.min(jnp.where(mask & (vals == mx), lane, big), axis=-1, keepdims=True)
        return mx, idx

    coarse = lane < N_COARSE
    m1, grp = first_max(coarse)
    p_grp = 1.0 / jnp.sum(jnp.where(coarse, jnp.exp(logits - m1), 0.0), axis=-1, keepdims=True)
    lo = N_COARSE + grp * N_FINE
    fine = (lane >= lo) & (lane < lo + N_FINE)
    v1, i1 = first_max(fine)
    v2, i2 = first_max(fine & (lane != i1))
    e21 = jnp.exp(v2 - v1)
    w1 = p_grp / (1.0 + e21)
    w2 = p_grp * e21 / (1.0 + e21)
    e1 = i1 - N_COARSE
    e2 = i2 - N_COARSE
    hit1 = lane == e1
    hit2 = lane == e2
    onehot = jnp.where(hit1 | hit2, 1.0, 0.0)
    before = _dot(tri_ref[...], onehot.astype(BF16))
    rank1 = jnp.sum(jnp.where(hit1, before, 0.0), axis=-1, keepdims=True)
    rank2 = jnp.sum(jnp.where(hit2, before, 0.0), axis=-1, keepdims=True)
    tm = onehot.shape[0]
    cnt_ref[0] = before[tm - 1:tm, :] + onehot[tm - 1:tm, :]
    rec = jnp.zeros_like(logits)
    for slot, val in ((R_E1, e1.astype(F32)), (R_E2, e2.astype(F32)), (R_W1, w1), (R_W2, w2),
                      (R_RANK1, rank1), (R_RANK2, rank2)):
        rec = jnp.where(lane == slot, val, rec)
    route_ref[...] = rec


def _mix_route(ys, us5, mgm, x2d, d, gluw_bf, glub, gs5, wout_bf, gffn, rwh, rwl, rb):
    t, dm = x2d.shape
    gw = mgm.shape[1]
    s5w = us5.shape[1]
    tm = TM_MIX
    tri = jnp.tril(jnp.ones((tm, tm), F32), -1).astype(BF16)
    const = lambda *shape: pl.BlockSpec(shape, lambda i: (0,) * len(shape))
    tile = lambda w: pl.BlockSpec((tm, w), lambda i: (i, 0))
    return pl.pallas_call(
        _mix_route_kernel,
        grid=(t // tm,),
        in_specs=[tile(s5w), tile(s5w), tile(gw), tile(dm),
                  const(1, s5w), const(s5w, s5w), const(1, s5w), const(1, s5w),
                  const(gw + s5w, dm), const(1, dm), const(dm, LANES), const(dm, LANES), const(1, LANES),
                  const(tm, tm)],
        out_specs=[tile(dm), tile(dm), tile(LANES), pl.BlockSpec((1, 1, LANES), lambda i: (i, 0, 0))],
        out_shape=[jax.ShapeDtypeStruct((t, dm), F32),
                   jax.ShapeDtypeStruct((t, dm), BF16),
                   jax.ShapeDtypeStruct((t, LANES), F32),
                   jax.ShapeDtypeStruct((t // tm, 1, LANES), F32)],
        compiler_params=pltpu.CompilerParams(dimension_semantics=("parallel",),
                                             vmem_limit_bytes=VMEM_LIMIT),
        name="mix_route",
    )(ys, us5, mgm, x2d, d, gluw_bf, glub, gs5, wout_bf, gffn, rwh, rwl, rb, tri)


def _local_rows(tm):
    worst = 2 * tm + N_EXPERTS * (SEG_ALIGN - 1)
    return -(-worst // LANES) * LANES


def _segment_plan(cnt, t, tm_expert):
    c = cnt[:, 0, :N_EXPERTS].astype(jnp.int32)
    n_tok_tiles = c.shape[0]
    al = (c + SEG_ALIGN - 1) // SEG_ALIGN * SEG_ALIGN
    lbase = jnp.cumsum(al, axis=1) - al
    tot = jnp.sum(al, axis=0)
    tot_pad = (tot + tm_expert - 1) // tm_expert * tm_expert
    gbase = jnp.cumsum(tot_pad) - tot_pad
    gpos = gbase[None, :] + jnp.cumsum(al, axis=0) - al
    n_tiles_max = -(-(2 * t + n_tok_tiles * N_EXPERTS * (SEG_ALIGN - 1)) // tm_expert) + N_EXPERTS
    tile_end = jnp.cumsum(tot_pad // tm_expert)
    n_tiles = tile_end[-1:].astype(jnp.int32)
    tile_idx = jnp.arange(n_tiles_max, dtype=jnp.int32)
    tile_expert = jnp.sum((tile_idx[:, None] >= tile_end[None, :]).astype(jnp.int32), axis=1)
    last = jnp.sum((n_tiles - 1 >= tile_end).astype(jnp.int32))
    tile_expert = jnp.where(tile_idx < n_tiles, tile_expert, last).astype(jnp.int32)
    lbase_f = jnp.pad(lbase.astype(F32), ((0, 0), (0, LANES - N_EXPERTS)))[:, None, :]
    flat = lambda a: a.reshape(-1).astype(jnp.int32)
    plan = dict(lb=flat(lbase), nch=flat(al // SEG_ALIGN), gpos=flat(gpos),
                tail_pos=flat(gbase + tot), tail_n=flat((tot_pad - tot) // SEG_ALIGN))
    return plan, lbase_f, tile_expert, n_tiles, n_tiles_max * tm_expert


def _local_positions(route, lbase):
    lane = lax.broadcasted_iota(jnp.int32, route.shape, 1).astype(F32)
    out = []
    for e_lane, r_lane in ((R_E1, R_RANK1), (R_E2, R_RANK2)):
        e = route[:, e_lane:e_lane + 1]
        base = jnp.sum(jnp.where(lane == e, lbase, 0.0), axis=-1, keepdims=True)
        out.append(base + route[:, r_lane:r_lane + 1])
    return out


def _segment_copies(i, lb_ref, nch_ref, gp_ref, local, glob, sem, to_global):
    def seg(e, total):
        k = i * N_EXPERTS + e
        lb, n, gp = lb_ref[k], nch_ref[k], gp_ref[k]

        def chunk(c, carry):
            lo = local.at[pl.ds(pl.multiple_of(lb + c * SEG_ALIGN, SEG_ALIGN), SEG_ALIGN)]
            gl = glob.at[pl.ds(pl.multiple_of(gp + c * SEG_ALIGN, SEG_ALIGN), SEG_ALIGN)]
            (pltpu.make_async_copy(lo, gl, sem) if to_global else pltpu.make_async_copy(gl, lo, sem)).start()
            return carry

        lax.fori_loop(0, n, chunk, 0)
        return total + n

    return lax.fori_loop(0, N_EXPERTS, seg, 0)


def _wait_chunks(n, local, glob, sem):
    def one(c, carry):
        pltpu.make_async_copy(local.at[pl.ds(0, SEG_ALIGN)], glob.at[pl.ds(0, SEG_ALIGN)], sem).wait()
        return carry

    lax.fori_loop(0, n, one, 0)


def _sort_rows_kernel(lb_ref, nch_ref, gp_ref, tpos_ref, tn_ref, nt_ref, t_ref, route_ref, lbase_ref,
                      xs_ref, local_scr, zero_scr, sem):
    i = pl.program_id(0)
    tm = t_ref.shape[0]
    s_rows = local_scr.shape[0]
    lp1, lp2 = _local_positions(route_ref[...], lbase_ref[0])
    lane = lax.broadcasted_iota(jnp.int32, (tm, LANES), 1)
    lp_rows = jnp.where(lane == 0, lp1, jnp.where(lane == 1, lp2, -1.0)).T
    row = lax.broadcasted_iota(jnp.int32, (s_rows, tm), 0).astype(F32)
    onehot = jnp.where((row == lp_rows[0:1, :]) | (row == lp_rows[1:2, :]), 1.0, 0.0).astype(BF16)
    local_scr[...] = _dot(onehot, t_ref[...]).astype(BF16)
    n_started = _segment_copies(i, lb_ref, nch_ref, gp_ref, local_scr, xs_ref, sem, True)
    _wait_chunks(n_started, local_scr, xs_ref, sem)

    @pl.when(i == pl.num_programs(0) - 1)
    def _():
        zero_scr[...] = jnp.zeros_like(zero_scr)
        te = zero_scr.shape[0]
        zero_chunk = zero_scr.at[pl.ds(0, SEG_ALIGN)]

        def tail(e, total):
            def chunk(c, carry):
                dst = xs_ref.at[pl.ds(pl.multiple_of(tpos_ref[e] + c * SEG_ALIGN, SEG_ALIGN), SEG_ALIGN)]
                pltpu.make_async_copy(zero_chunk, dst, sem).start()
                return carry

            lax.fori_loop(0, tn_ref[e], chunk, 0)
            return total + tn_ref[e]

        _wait_chunks(lax.fori_loop(0, N_EXPERTS, tail, 0), zero_scr, xs_ref, sem)

        def unused_tile(j, carry):
            pltpu.make_async_copy(zero_scr, xs_ref.at[pl.ds(pl.multiple_of(j * te, te), te)], sem).start()
            return carry

        def unused_wait(j, carry):
            pltpu.make_async_copy(zero_scr, xs_ref.at[pl.ds(0, te)], sem).wait()
            return carry

        lax.fori_loop(nt_ref[0], xs_ref.shape[0] // te, unused_tile, 0)
        lax.fori_loop(nt_ref[0], xs_ref.shape[0] // te, unused_wait, 0)


def _plan_specs(plan):
    keys = ('lb', 'nch', 'gpos', 'tail_pos', 'tail_n')
    return [plan[k] for k in keys]


def _sort_rows(plan, n_tiles, t_bf, route, lbase_f, n_sorted):
    t, dm = t_bf.shape
    tm = TM_MIX
    s_rows = _local_rows(tm)
    im = lambda i, *_: (i, 0)
    return pl.pallas_call(
        _sort_rows_kernel,
        grid_spec=pltpu.PrefetchScalarGridSpec(
            num_scalar_prefetch=6,
            grid=(t // tm,),
            in_specs=[pl.BlockSpec((tm, dm), im), pl.BlockSpec((tm, LANES), im),
                      pl.BlockSpec((1, 1, LANES), lambda i, *_: (i, 0, 0))],
            out_specs=pl.BlockSpec(memory_space=pl.ANY),
            scratch_shapes=[pltpu.VMEM((s_rows, dm), BF16), pltpu.VMEM((TM_EXPERT, dm), BF16),
                            pltpu.SemaphoreType.DMA(())],
        ),
        out_shape=jax.ShapeDtypeStruct((n_sorted, dm), BF16),
        compiler_params=pltpu.CompilerParams(dimension_semantics=("arbitrary",),
                                             vmem_limit_bytes=VMEM_LIMIT),
        name="sort_rows",
    )(*_plan_specs(plan), n_tiles, t_bf, route, lbase_f)


def _experts_kernel(te_ref, nt_ref, xs_ref, wgu_ref, wd_ref, ys_ref):
    i = pl.program_id(0)
    de = wd_ref.shape[1]

    @pl.when(i < nt_ref[0])
    def _():
        gu = _dot(xs_ref[...], wgu_ref[0])
        hidden = (jax.nn.silu(gu[:, :de]) * gu[:, de:]).astype(BF16)
        ys_ref[...] = _dot(hidden, wd_ref[0]).astype(BF16)

    @pl.when(i >= nt_ref[0])
    def _():
        ys_ref[...] = jnp.zeros_like(ys_ref)


def _experts(tile_expert, n_tiles, x_sorted, wgu_bf, wd_bf):
    n_sorted, dm = x_sorted.shape
    de2 = wgu_bf.shape[2]
    tm = TM_EXPERT
    return pl.pallas_call(
        _experts_kernel,
        grid_spec=pltpu.PrefetchScalarGridSpec(
            num_scalar_prefetch=2,
            grid=(n_sorted // tm,),
            in_specs=[pl.BlockSpec((tm, dm), lambda i, te, nt: (jnp.minimum(i, nt[0] - 1), 0)),
                      pl.BlockSpec((1, dm, de2), lambda i, te, nt: (te[i], 0, 0)),
                      pl.BlockSpec((1, de2 // 2, dm), lambda i, te, nt: (te[i], 0, 0))],
            out_specs=pl.BlockSpec((tm, dm), lambda i, te, nt: (i, 0)),
        ),
        out_shape=jax.ShapeDtypeStruct((n_sorted, dm), BF16),
        compiler_params=pltpu.CompilerParams(dimension_semantics=("arbitrary",),
                                             vmem_limit_bytes=VMEM_LIMIT),
        name="experts",
    )(tile_expert, n_tiles, x_sorted, wgu_bf, wd_bf)


def _combine_kernel(lb_ref, nch_ref, gp_ref, tpos_ref, tn_ref, x2_ref, route_ref, lbase_ref, gfin_ref, ys_ref,
                    o_ref, local_scr, sem):
    del tpos_ref, tn_ref
    i = pl.program_id(0)
    tm = x2_ref.shape[0]
    s_rows = local_scr.shape[0]
    local_scr[...] = jnp.zeros_like(local_scr)
    n_started = _segment_copies(i, lb_ref, nch_ref, gp_ref, local_scr, ys_ref, sem, False)
    route = route_ref[...]
    lp1, lp2 = _local_positions(route, lbase_ref[0])
    col = lax.broadcasted_iota(jnp.int32, (tm, s_rows), 1).astype(F32)
    pick1 = jnp.where(col == lp1, 1.0, 0.0).astype(BF16)
    pick2 = jnp.where(col == lp2, 1.0, 0.0).astype(BF16)
    _wait_chunks(n_started, local_scr, ys_ref, sem)
    y = local_scr[...]
    moe = route[:, R_W1:R_W1 + 1] * _dot(pick1, y) + route[:, R_W2:R_W2 + 1] * _dot(pick2, y)
    o_ref[...] = _rms(x2_ref[...] + moe, gfin_ref[...])


def _combine(plan, x2, route, lbase_f, gfin, y_sorted):
    t, dm = x2.shape
    tm = TM_MIX
    s_rows = _local_rows(tm)
    im = lambda i, *_: (i, 0)
    return pl.pallas_call(
        _combine_kernel,
        grid_spec=pltpu.PrefetchScalarGridSpec(
            num_scalar_prefetch=5,
            grid=(t // tm,),
            in_specs=[pl.BlockSpec((tm, dm), im), pl.BlockSpec((tm, LANES), im),
                      pl.BlockSpec((1, 1, LANES), lambda i, *_: (i, 0, 0)),
                      pl.BlockSpec((1, dm), lambda i, *_: (0, 0)),
                      pl.BlockSpec(memory_space=pl.ANY)],
            out_specs=pl.BlockSpec((tm, dm), im),
            scratch_shapes=[pltpu.VMEM((s_rows, dm), BF16), pltpu.SemaphoreType.DMA(())],
        ),
        out_shape=jax.ShapeDtypeStruct((t, dm), F32),
        compiler_params=pltpu.CompilerParams(dimension_semantics=("arbitrary",),
                                             vmem_limit_bytes=VMEM_LIMIT),
        name="combine_norm",
    )(*_plan_specs(plan), x2, route, lbase_f, gfin, y_sorted)


def _layer(x, p, s5_ops, gfin):
    b, l, dm = x.shape
    x2d = x.reshape(b * l, dm)
    mgm, us5 = _inproj_gmlp(x2d, p['gmix'], p['win'], p['lng'], p['lnb'], p['ws'], p['bs'], p['gout_gm'])
    n_seg = SUBLANES // b
    m, w1, w2, sc = s5_ops[(l // (S5_LC * n_seg))]
    xg = _s5_inproj(x, p['gmix'], p['win_s5'], n_seg)
    yg = _s5_scan(xg, m, w1, w2, sc, n_seg)
    ys = _s5_to_tokens(yg, b, l, n_seg)
    x2, t_bf, route, counts = _mix_route(ys, us5, mgm, x2d, p['d'], p['gluw'], p['glub'], p['gout_s5'],
                                         p['wout'], p['gffn'], p['rwh'], p['rwl'], p['rb'])
    plan, lbase_f, tile_expert, n_tiles, n_sorted = _segment_plan(counts, b * l, TM_EXPERT)
    x_sorted = _sort_rows(plan, n_tiles, t_bf, route, lbase_f, n_sorted)
    y_sorted = _experts(tile_expert, n_tiles, x_sorted, p['wgu'], p['wd'])
    out = _combine(plan, x2, route, lbase_f, gfin, y_sorted)
    return out.reshape(b, l, dm)


def kernel(x_prompt, x_sample, norm_mix_g, w_in, gm_ln_g, gm_ln_b, gm_ws, gm_bs, s5_lam_re_fwd, s5_lam_im_fwd, s5_log_step_fwd, s5_b_re_fwd, s5_b_im_fwd, s5_c_re_fwd, s5_c_im_fwd, s5_lam_re_bwd, s5_lam_im_bwd, s5_log_step_bwd, s5_b_re_bwd, s5_b_im_bwd, s5_c_re_bwd, s5_c_im_bwd, s5_d, s5_glu_w, s5_glu_b, out_norm_gm, out_norm_s5, w_out, norm_ffn_g, r1_w, r1_b, r2_w, r2_b, e_w_gate, e_w_up, e_w_down, norm_final_g):
    depth = w_in.shape[0]
    gfin = norm_final_g.reshape(1, -1).astype(F32)
    xs = [x_prompt, x_sample]
    for li in range(depth):
        row = lambda a: a[li].reshape(1, -1).astype(F32)
        dm = w_in.shape[1]
        gw = gm_ln_g.shape[1]
        hd_dim = gw // GM_HEADS
        rw = jnp.concatenate([r1_w[li], r2_w[li].transpose(1, 0, 2).reshape(dm, N_EXPERTS)], axis=1).astype(F32)
        rw = jnp.pad(rw, ((0, 0), (0, LANES - rw.shape[1])))
        rwh = rw.astype(BF16)
        rwl = (rw - rwh.astype(F32)).astype(BF16)
        rb = jnp.concatenate([r1_b[li], r2_b[li].reshape(-1)]).astype(F32)
        rb = jnp.pad(rb, (0, LANES - rb.shape[0])).reshape(1, LANES)
        p = dict(
            gmix=row(norm_mix_g), win=w_in[li].astype(BF16), win_s5=w_in[li][:, 2 * gw:].astype(BF16),
            lng=row(gm_ln_g), lnb=row(gm_ln_b),
            ws=gm_ws[li].astype(BF16),
            bs=jnp.broadcast_to(gm_bs[li].astype(F32)[:, :, None], (GM_HEADS, CHUNK, hd_dim)),
            gout_gm=row(out_norm_gm), d=row(s5_d), gluw=s5_glu_w[li].astype(BF16), glub=row(s5_glu_b),
            gout_s5=row(out_norm_s5), wout=w_out[li].astype(BF16), gffn=row(norm_ffn_g),
            rwh=rwh, rwl=rwl, rb=rb,
            wgu=jnp.concatenate([e_w_gate[li], e_w_up[li]], axis=-1).astype(BF16),
            wd=e_w_down[li].astype(BF16),
        )
        fwd = (s5_lam_re_fwd[li], s5_lam_im_fwd[li], s5_log_step_fwd[li], s5_b_re_fwd[li], s5_b_im_fwd[li],
               s5_c_re_fwd[li], s5_c_im_fwd[li])
        bwd = (s5_lam_re_bwd[li], s5_lam_im_bwd[li], s5_log_step_bwd[li], s5_b_re_bwd[li], s5_b_im_bwd[li],
               s5_c_re_bwd[li], s5_c_im_bwd[li])
        s5_ops = {}
        for x in xs:
            seg_steps = x.shape[1] // (S5_LC * (SUBLANES // x.shape[0]))
            if seg_steps not in s5_ops:
                s5_ops[seg_steps] = _s5_operator(fwd, bwd, S5_LC, seg_steps)
        last = li == depth - 1
        assert last, "depth > 1 needs an un-normalised layer output"
        xs = [_layer(x, p, s5_ops, gfin) for x in xs]
    return tuple(xs)
```

```python
import functools
import math

import jax
import jax.numpy as jnp
from jax import lax
from jax.experimental import pallas as pl
from jax.experimental.pallas import tpu as pltpu

F32 = jnp.float32
BF16 = jnp.bfloat16

EPS = 1e-6
LAMBDA_RE_MAX = -1e-4
GM_HEADS = 4
CHUNK = 128
S5_GROUP = 16
S5_STATE = 64
N_COARSE = 4
N_FINE = 8
N_EXPERTS = N_COARSE * N_FINE

LANES = 128
SUBLANES = 8
S5_LC = 16
VMEM_LIMIT = 56 * 1024 * 1024

TM_PROJ = 512
TM_MIX = 512
TM_EXPERT = 512
SEG_ALIGN = 16


def _gelu(x):
    c = math.sqrt(2.0 / math.pi)
    return x * (0.5 * (1.0 + jnp.tanh(c * (x + 0.044715 * (x * x * x)))))


def _rms(x, g):
    ms = jnp.mean(x * x, axis=-1, keepdims=True)
    return x * lax.rsqrt(ms + EPS) * g


def _dot(a, b):
    return jnp.dot(a, b, preferred_element_type=F32)


def _inproj_gmlp_kernel(x_ref, gmix_ref, win_ref, lng_ref, lnb_ref, ws_ref, bs_ref, gout_ref,
                        mgm_ref, us5_ref, y_scr):
    tm = x_ref.shape[0]
    gw = mgm_ref.shape[1]
    hd_dim = gw // GM_HEADS
    n_chunks = tm // CHUNK
    h = _rms(x_ref[...], gmix_ref[...]).astype(BF16)
    proj = _dot(h, win_ref[...])
    us5_ref[...] = proj[:, 2 * gw:]
    u = _gelu(proj[:, :gw])
    v = _gelu(proj[:, gw:2 * gw])
    for hd in range(GM_HEADS):
        lo = hd * hd_dim
        vh = v[:, lo:lo + hd_dim]
        mu = jnp.mean(vh, axis=-1, keepdims=True)
        xc = vh - mu
        var = jnp.mean(xc * xc, axis=-1, keepdims=True)
        vn = (xc * lax.rsqrt(var + EPS) * lng_ref[:, lo:lo + hd_dim]
              + lnb_ref[:, lo:lo + hd_dim]).astype(BF16)
        rhs = jnp.concatenate([vn[c * CHUNK:(c + 1) * CHUNK] for c in range(n_chunks)], axis=1)
        s = _dot(ws_ref[hd], rhs)
        for c in range(n_chunks):
            sc = s[:, c * hd_dim:(c + 1) * hd_dim] + bs_ref[hd]
            y_scr[c * CHUNK:(c + 1) * CHUNK, lo:lo + hd_dim] = u[c * CHUNK:(c + 1) * CHUNK, lo:lo + hd_dim] * sc
    mgm_ref[...] = _rms(y_scr[...], gout_ref[...]).astype(BF16)


def _inproj_gmlp(x2d, gmix, win_bf, lng, lnb, ws_bf, bs_b, gout):
    t, d = x2d.shape
    d_in = win_bf.shape[1]
    gw = lng.shape[1]
    s5w = d_in - 2 * gw
    tm = TM_PROJ
    const = lambda *shape: pl.BlockSpec(shape, lambda i: (0,) * len(shape))
    return pl.pallas_call(
        _inproj_gmlp_kernel,
        grid=(t // tm,),
        in_specs=[
            pl.BlockSpec((tm, d), lambda i: (i, 0)),
            const(1, d), const(d, d_in), const(1, gw), const(1, gw),
            const(GM_HEADS, CHUNK, CHUNK), const(GM_HEADS, CHUNK, gw // GM_HEADS), const(1, gw),
        ],
        out_specs=[pl.BlockSpec((tm, gw), lambda i: (i, 0)),
                   pl.BlockSpec((tm, s5w), lambda i: (i, 0))],
        out_shape=[jax.ShapeDtypeStruct((t, gw), BF16),
                   jax.ShapeDtypeStruct((t, s5w), F32)],
        scratch_shapes=[pltpu.VMEM((tm, gw), F32)],
        compiler_params=pltpu.CompilerParams(dimension_semantics=("parallel",),
                                             vmem_limit_bytes=VMEM_LIMIT),
        name="inproj_gmlp",
    )(x2d, gmix, win_bf, lng, lnb, ws_bf, bs_b, gout)


def _s5_consts(lam_re, lam_im, log_step, b_re, b_im, c_re, c_im, lc):
    lr = jnp.minimum(lam_re.astype(F32), LAMBDA_RE_MAX)
    li = lam_im.astype(F32)
    step = jnp.exp(log_step.astype(F32))[:, None]
    dr, di = lr * step, li * step
    ar, ai = _cexp(dr, di)
    nr, ni = ar - 1.0, ai
    den = lr * lr + li * li
    qr, qi = (nr * lr + ni * li) / den, (ni * lr - nr * li) / den
    br, bi = b_re.astype(F32), b_im.astype(F32)
    bbr = qr[..., None] * br - qi[..., None] * bi
    bbi = qr[..., None] * bi + qi[..., None] * br
    k = jnp.arange(lc + 1, dtype=F32)[:, None, None]
    pwr, pwi = _cexp(k * dr[None], k * di[None])
    return (dr, di), (pwr, pwi), (bbr, bbi), (c_re.astype(F32), c_im.astype(F32))


def _cexp(zr, zi):
    m = jnp.exp(zr)
    return m * jnp.cos(zi), m * jnp.sin(zi)


def _s5_operator(fwd, bwd, lc, seg_steps):
    consts = [_s5_consts(*fwd, lc), _s5_consts(*bwd, lc)]
    g, p, h = consts[0][2][0].shape
    m = 0.0
    w1_parts, w2_parts, sc_rows, seg_rows = [], [], [], []
    for direction, (ld, pw, bb, c) in enumerate(consts):
        (dr, di), (pwr, pwi), (bbr, bbi), (cr, ci) = ld, pw, bb, c
        cpr = cr[None] * pwr[:, :, None, :] - ci[None] * pwi[:, :, None, :]
        cpi = cr[None] * pwi[:, :, None, :] + ci[None] * pwr[:, :, None, :]
        kk = (jnp.einsum('kghp,gpi->gkhi', cpr[:lc], bbr) - jnp.einsum('kghp,gpi->gkhi', cpi[:lc], bbi))
        shifted = jnp.stack([jnp.pad(kk[:, :lc - r], ((0, 0), (r, 0), (0, 0), (0, 0))) for r in range(lc)], axis=1)
        tk = shifted if direction == 0 else shifted.transpose(0, 2, 1, 3, 4)
        m = m + tk.transpose(0, 1, 4, 2, 3).reshape(g, lc * h, lc * h)
        er, ei = (pwr[:lc][::-1], pwi[:lc][::-1]) if direction == 0 else (pwr[:lc], pwi[:lc])
        wr = er[:, :, None, :] * bbr.transpose(0, 2, 1)[None] - ei[:, :, None, :] * bbi.transpose(0, 2, 1)[None]
        wi = er[:, :, None, :] * bbi.transpose(0, 2, 1)[None] + ei[:, :, None, :] * bbr.transpose(0, 2, 1)[None]
        wr, wi = wr.transpose(1, 0, 2, 3), wi.transpose(1, 0, 2, 3)
        w1_parts += [wr, wi, wi, wr]
        sel = slice(1, lc + 1)
        fr, fi = (cpr[sel], cpi[sel]) if direction == 0 else (cpr[sel][::-1], cpi[sel][::-1])
        w2_parts += [fr.transpose(1, 3, 0, 2), -fi.transpose(1, 3, 0, 2)]

        def mult(zr, zi):
            return [jnp.concatenate([zr, zr], -1), jnp.concatenate([-zi, zi], -1)]

        sc_rows += mult(*_cexp(lc * dr, lc * di))
        seg_rows += mult(*_cexp((lc * seg_steps) * dr, (lc * seg_steps) * di))
    w1 = jnp.concatenate(w1_parts, axis=-1).reshape(g, lc * h, 8 * p)
    w2 = jnp.concatenate(w2_parts, axis=1).reshape(g, 4 * p, lc * h)
    sc = jnp.stack(sc_rows + seg_rows, axis=1)
    return m.astype(BF16), w1.astype(BF16), w2.astype(BF16), sc.astype(F32)


def _s5_kernel(x_ref, m_ref, w1_ref, w2_ref, sc_ref, y_ref, loc_scr, sin_scr, *, n_seg):
    rows = x_ref.shape[1]
    steps = rows // SUBLANES
    sw = sc_ref.shape[2]
    x = x_ref[0]
    loc_scr[...] = _dot(x, w1_ref[0])

    def bc(i):
        return jnp.broadcast_to(sc_ref[0, i:i + 1, :], (SUBLANES, sw))

    a1f, a2f, a1b, a2b, p1f, p2f, p1b, p2b = [bc(i) for i in range(8)]

    def step_f(s, f, fs):
        r = pl.multiple_of(s * SUBLANES, SUBLANES)
        lf = loc_scr[pl.ds(r, SUBLANES), 0:sw]
        lfs = loc_scr[pl.ds(r, SUBLANES), sw:2 * sw]
        return a1f * f + a2f * fs + lf, a1f * fs - a2f * f + lfs

    def step_b(s, b, bs):
        r = pl.multiple_of((steps - 1 - s) * SUBLANES, SUBLANES)
        lb = loc_scr[pl.ds(r, SUBLANES), 2 * sw:3 * sw]
        lbs = loc_scr[pl.ds(r, SUBLANES), 3 * sw:4 * sw]
        return a1b * b + a2b * bs + lb, a1b * bs - a2b * b + lbs

    zero = jnp.zeros((SUBLANES, sw), F32)

    def pass1(s, carry):
        f, fs, b, bs = carry
        return step_f(s, f, fs) + step_b(s, b, bs)

    f_end, fs_end, b_end, bs_end = lax.fori_loop(0, steps, pass1, (zero, zero, zero, zero), unroll=4)

    seg = lax.broadcasted_iota(jnp.int32, (SUBLANES, sw), 0) % n_seg
    cf, cfs, cb, cbs = zero, zero, zero, zero
    for _ in range(n_seg - 1):
        ef = f_end + p1f * cf + p2f * cfs
        efs = fs_end + p1f * cfs - p2f * cf
        eb = b_end + p1b * cb + p2b * cbs
        ebs = bs_end + p1b * cbs - p2b * cb
        cf = jnp.where(seg >= 1, pltpu.roll(ef, 1, 0), 0.0)
        cfs = jnp.where(seg >= 1, pltpu.roll(efs, 1, 0), 0.0)
        cb = jnp.where(seg <= n_seg - 2, pltpu.roll(eb, SUBLANES - 1, 0), 0.0)
        cbs = jnp.where(seg <= n_seg - 2, pltpu.roll(ebs, SUBLANES - 1, 0), 0.0)

    def pass2(s, carry):
        f, fs, b, bs = carry
        rf = pl.multiple_of(s * SUBLANES, SUBLANES)
        rb = pl.multiple_of((steps - 1 - s) * SUBLANES, SUBLANES)
        sin_scr[pl.ds(rf, SUBLANES), 0:sw] = f
        sin_scr[pl.ds(rb, SUBLANES), sw:2 * sw] = b
        return step_f(s, f, fs) + step_b(s, b, bs)

    lax.fori_loop(0, steps, pass2, (cf, cfs, cb, cbs), unroll=4)

    y_ref[0] = _dot(x, m_ref[0]) + _dot(sin_scr[...].astype(BF16), w2_ref[0])


def _s5_scan(xg, m, w1, w2, sc, n_seg):
    g, rows, kw = xg.shape
    sw = sc.shape[2]
    blk = lambda a: pl.BlockSpec((1,) + a.shape[1:], lambda i: (i, 0, 0))
    return pl.pallas_call(
        functools.partial(_s5_kernel, n_seg=n_seg),
        grid=(g,),
        in_specs=[blk(xg), blk(m), blk(w1), blk(w2), blk(sc)],
        out_specs=pl.BlockSpec((1, rows, kw), lambda i: (i, 0, 0)),
        out_shape=jax.ShapeDtypeStruct((g, rows, kw), F32),
        scratch_shapes=[pltpu.VMEM((rows, 4 * sw), F32), pltpu.VMEM((rows, 2 * sw), F32)],
        compiler_params=pltpu.CompilerParams(dimension_semantics=("parallel",),
                                             vmem_limit_bytes=VMEM_LIMIT),
        name="s5_scan",
    )(xg, m, w1, w2, sc)


S5_NM = 8


def _block_transpose8(v, width):
    lane = lax.broadcasted_iota(jnp.int32, v[0].shape, 1)
    for d in (4, 2, 1):
        w = width * d
        hi = ((lane // w) % 2) == 1
        out = list(v)
        for i0 in range(8):
            if i0 & d:
                continue
            i1 = i0 + d
            out[i0] = jnp.where(hi, pltpu.roll(v[i1], w, 1), v[i0])
            out[i1] = jnp.where(hi, v[i1], pltpu.roll(v[i0], 8 * width - w, 1))
        v = out
    return v


def _tile_copies(hbm4, tile, buf, slot, sem, nm, to_hbm):
    copies = []
    for c in range(SUBLANES):
        for j in range(S5_LC):
            h = hbm4.at[c, pl.ds(tile * nm, nm), pl.ds(j, 1), :]
            v = buf.at[slot, j, :, pl.ds(c, 1), :]
            copies.append(pltpu.make_async_copy(v, h, sem.at[slot]) if to_hbm
                          else pltpu.make_async_copy(h, v, sem.at[slot]))
    return copies


def _s5_inproj_kernel(x4_ref, gmix_ref, w_ref, xg_ref, xs, sem, *, nm):
    i = pl.program_id(0)
    n = pl.num_programs(0)
    slot = i % 2
    dm = x4_ref.shape[3]

    @pl.when(i == 0)
    def _():
        for cp in _tile_copies(x4_ref, 0, xs, 0, sem, nm, False):
            cp.start()

    @pl.when(i + 1 < n)
    def _():
        for cp in _tile_copies(x4_ref, i + 1, xs, 1 - slot, sem, nm, False):
            cp.start()

    pltpu.make_async_copy(xs.at[slot], xs.at[slot], sem.at[slot]).wait()
    rows = nm * SUBLANES
    x = xs[slot].reshape(S5_LC * rows, dm)
    z = _dot(_rms(x, gmix_ref[...]).astype(BF16), w_ref[...])
    n_oct = z.shape[1] // LANES
    for q in range(n_oct):
        for a in range(S5_LC // 8):
            blocks = [z[(8 * a + j8) * rows:(8 * a + j8 + 1) * rows, q * LANES:(q + 1) * LANES] for j8 in range(8)]
            for g8, b in enumerate(_block_transpose8(blocks, S5_GROUP)):
                xg_ref[8 * q + g8, :, a * LANES:(a + 1) * LANES] = b.astype(BF16)


def _s5_inproj(x, gmix, w_s5_bf, n_seg):
    b, l, dm = x.shape
    steps = l // (S5_LC * n_seg)
    nm = S5_NM
    s5w = w_s5_bf.shape[1]
    g = s5w // S5_GROUP
    x4 = x.reshape(b * n_seg, steps, S5_LC, dm)
    return pl.pallas_call(
        functools.partial(_s5_inproj_kernel, nm=nm),
        grid=(steps // nm,),
        in_specs=[pl.BlockSpec(memory_space=pl.ANY),
                  pl.BlockSpec((1, dm), lambda i: (0, 0)),
                  pl.BlockSpec((dm, s5w), lambda i: (0, 0))],
        out_specs=pl.BlockSpec((g, nm * SUBLANES, S5_LC * S5_GROUP), lambda i: (0, i, 0)),
        out_shape=jax.ShapeDtypeStruct((g, steps * SUBLANES, S5_LC * S5_GROUP), BF16),
        scratch_shapes=[pltpu.VMEM((2, S5_LC, nm, SUBLANES, dm), F32), pltpu.SemaphoreType.DMA((2,))],
        compiler_params=pltpu.CompilerParams(dimension_semantics=("arbitrary",),
                                             vmem_limit_bytes=VMEM_LIMIT),
        name="s5_inproj",
    )(x4, gmix, w_s5_bf)


def _s5_to_tokens_kernel(yg_ref, ys4_ref, zs, sem, *, nm):
    i = pl.program_id(0)
    n = pl.num_programs(0)
    slot = i % 2
    rows = nm * SUBLANES

    def wait(s):
        pltpu.make_async_copy(zs.at[s], zs.at[s], sem.at[s]).wait()

    @pl.when(i >= 2)
    def _():
        wait(slot)

    n_oct = yg_ref.shape[0] // 8
    for q in range(n_oct):
        for a in range(S5_LC // 8):
            blocks = [yg_ref[8 * q + g8, :, a * LANES:(a + 1) * LANES] for g8 in range(8)]
            for j8, b in enumerate(_block_transpose8(blocks, S5_GROUP)):
                zs[slot, 8 * a + j8, :, :, q * LANES:(q + 1) * LANES] = b.reshape(nm, SUBLANES, LANES)
    for cp in _tile_copies(ys4_ref, i, zs, slot, sem, nm, True):
        cp.start()

    @pl.when(i == n - 1)
    def _():
        wait(1 - slot)
        wait(slot)


def _s5_to_tokens(yg, b, l, n_seg):
    g, rows_total, kw = yg.shape
    steps = rows_total // SUBLANES
    nm = S5_NM
    s5w = g * S5_GROUP
    assert steps // nm >= 2
    ys4 = pl.pallas_call(
        functools.partial(_s5_to_tokens_kernel, nm=nm),
        grid=(steps // nm,),
        in_specs=[pl.BlockSpec((g, nm * SUBLANES, kw), lambda i: (0, i, 0))],
        out_specs=pl.BlockSpec(memory_space=pl.ANY),
        out_shape=jax.ShapeDtypeStruct((b * n_seg, steps, S5_LC, s5w), F32),
        scratch_shapes=[pltpu.VMEM((2, S5_LC, nm, SUBLANES, s5w), F32), pltpu.SemaphoreType.DMA((2,))],
        compiler_params=pltpu.CompilerParams(dimension_semantics=("arbitrary",),
                                             vmem_limit_bytes=VMEM_LIMIT),
        name="s5_to_tokens",
    )(yg)
    return ys4.reshape(b * l, s5w)


R_E1, R_E2, R_W1, R_W2, R_RANK1, R_RANK2 = range(6)


def _mix_route_kernel(ys_ref, us5_ref, mgm_ref, x_ref, d_ref, gluw_ref, glub_ref, gs5_ref,
                      wout_ref, gffn_ref, rwh_ref, rwl_ref, rb_ref, tri_ref,
                      x2_ref, t_ref, route_ref, cnt_ref):
    gw = mgm_ref.shape[1]
    y = ys_ref[...] + d_ref[...] * us5_ref[...]
    g = _gelu(y)
    z = g * jax.nn.sigmoid(_dot(g.astype(BF16), gluw_ref[...]) + glub_ref[...])
    ms5 = _rms(z, gs5_ref[...]).astype(BF16)
    mix = _dot(mgm_ref[...], wout_ref[:gw, :]) + _dot(ms5, wout_ref[gw:, :])
    x2 = x_ref[...] + mix
    x2_ref[...] = x2
    t = _rms(x2, gffn_ref[...])
    t_hi = t.astype(BF16)
    t_ref[...] = t_hi
    t_lo = (t - t_hi.astype(F32)).astype(BF16)
    logits = (_dot(t_hi, rwh_ref[...]) + _dot(t_hi, rwl_ref[...]) + _dot(t_lo, rwh_ref[...])
              + rb_ref[...])
    lane = lax.broadcasted_iota(jnp.int32, logits.shape, 1).astype(F32)
    neg = jnp.float32(-jnp.inf)

    def first_max(mask):
        vals = jnp.where(mask, logits, neg)
        mx = jnp.max(vals, axis=-1, keepdims=True)
        idx = jnp.min(jnp.where(mask & (vals == mx), lane, float(LANES)), axis=-1, keepdims=True)
        return mx, idx

    coarse = lane < N_COARSE
    m1, grp = first_max(coarse)
    p_grp = 1.0 / jnp.sum(jnp.where(coarse, jnp.exp(logits - m1), 0.0), axis=-1, keepdims=True)
    lo = N_COARSE + grp * N_FINE
    fine = (lane >= lo) & (lane < lo + N_FINE)
    v1, i1 = first_max(fine)
    v2, i2 = first_max(fine & (lane != i1))
    e21 = jnp.exp(v2 - v1)
    w1 = p_grp / (1.0 + e21)
    w2 = p_grp * e21 / (1.0 + e21)
    e1 = i1 - N_COARSE
    e2 = i2 - N_COARSE
    hit1 = lane == e1
    hit2 = lane == e2
    onehot = jnp.where(hit1 | hit2, 1.0, 0.0)
    before = _dot(tri_ref[...], onehot.astype(BF16))
    rank1 = jnp.sum(jnp.where(hit1, before, 0.0), axis=-1, keepdims=True)
    rank2 = jnp.sum(jnp.where(hit2, before, 0.0), axis=-1, keepdims=True)
    tm = onehot.shape[0]
    cnt_ref[0] = before[tm - 1:tm, :] + onehot[tm - 1:tm, :]
    rec = jnp.zeros_like(logits)
    for slot, val in ((R_E1, e1), (R_E2, e2), (R_W1, w1), (R_W2, w2),
                      (R_RANK1, rank1), (R_RANK2, rank2)):
        rec = jnp.where(lane == slot, val, rec)
    route_ref[...] = rec


def _mix_route(ys, us5, mgm, x2d, d, gluw_bf, glub, gs5, wout_bf, gffn, rwh, rwl, rb):
    t, dm = x2d.shape
    gw = mgm.shape[1]
    s5w = us5.shape[1]
    tm = TM_MIX
    tri = jnp.tril(jnp.ones((tm, tm), F32), -1).astype(BF16)
    const = lambda *shape: pl.BlockSpec(shape, lambda i: (0,) * len(shape))
    tile = lambda w: pl.BlockSpec((tm, w), lambda i: (i, 0))
    return pl.pallas_call(
        _mix_route_kernel,
        grid=(t // tm,),
        in_specs=[tile(s5w), tile(s5w), tile(gw), tile(dm),
                  const(1, s5w), const(s5w, s5w), const(1, s5w), const(1, s5w),
                  const(gw + s5w, dm), const(1, dm), const(dm, LANES), const(dm, LANES), const(1, LANES),
                  const(tm, tm)],
        out_specs=[tile(dm), tile(dm), tile(LANES), pl.BlockSpec((1, 1, LANES), lambda i: (i, 0, 0))],
        out_shape=[jax.ShapeDtypeStruct((t, dm), F32),
                   jax.ShapeDtypeStruct((t, dm), BF16),
                   jax.ShapeDtypeStruct((t, LANES), F32),
                   jax.ShapeDtypeStruct((t // tm, 1, LANES), F32)],
        compiler_params=pltpu.CompilerParams(dimension_semantics=("parallel",),
                                             vmem_limit_bytes=VMEM_LIMIT),
        name="mix_route",
    )(ys, us5, mgm, x2d, d, gluw_bf, glub, gs5, wout_bf, gffn, rwh, rwl, rb, tri)


def _local_rows(tm):
    worst = 2 * tm + N_EXPERTS * (SEG_ALIGN - 1)
    return -(-worst // LANES) * LANES


def _segment_plan(cnt, t, tm_expert):
    c = cnt[:, 0, :N_EXPERTS].astype(jnp.int32)
    n_tok_tiles = c.shape[0]
    al = (c + SEG_ALIGN - 1) // SEG_ALIGN * SEG_ALIGN
    lbase = jnp.cumsum(al, axis=1) - al
    tot = jnp.sum(al, axis=0)
    tot_pad = (tot + tm_expert - 1) // tm_expert * tm_expert
    gbase = jnp.cumsum(tot_pad) - tot_pad
    gpos = gbase[None, :] + jnp.cumsum(al, axis=0) - al
    n_tiles_max = -(-(2 * t + n_tok_tiles * N_EXPERTS * (SEG_ALIGN - 1)) // tm_expert) + N_EXPERTS
    tile_end = jnp.cumsum(tot_pad // tm_expert)
    n_tiles = tile_end[-1:].astype(jnp.int32)
    tile_idx = jnp.arange(n_tiles_max, dtype=jnp.int32)
    tile_expert = jnp.sum((tile_idx[:, None] >= tile_end[None, :]).astype(jnp.int32), axis=1)
    last = jnp.sum((n_tiles - 1 >= tile_end).astype(jnp.int32))
    tile_expert = jnp.where(tile_idx < n_tiles, tile_expert, last).astype(jnp.int32)
    lbase_f = jnp.pad(lbase.astype(F32), ((0, 0), (0, LANES - N_EXPERTS)))[:, None, :]
    flat = lambda a: a.reshape(-1).astype(jnp.int32)
    plan = dict(lb=flat(lbase), nch=flat(al // SEG_ALIGN), gpos=flat(gpos),
                tail_pos=flat(gbase + tot), tail_n=flat((tot_pad - tot) // SEG_ALIGN))
    return plan, lbase_f, tile_expert, n_tiles, n_tiles_max * tm_expert


def _local_positions(route, lbase):
    lane = lax.broadcasted_iota(jnp.int32, route.shape, 1).astype(F32)
    out = []
    for e_lane, r_lane in ((R_E1, R_RANK1), (R_E2, R_RANK2)):
        e = route[:, e_lane:e_lane + 1]
        base = jnp.sum(jnp.where(lane == e, lbase, 0.0), axis=-1, keepdims=True)
        out.append(base + route[:, r_lane:r_lane + 1])
    return out


def _segment_copies(i, lb_ref, nch_ref, gp_ref, local, glob, sem, to_global):
    def seg(e, total):
        k = i * N_EXPERTS + e
        lb, n, gp = lb_ref[k], nch_ref[k], gp_ref[k]

        def chunk(c, carry):
            lo = local.at[pl.ds(pl.multiple_of(lb + c * SEG_ALIGN, SEG_ALIGN), SEG_ALIGN)]
            gl = glob.at[pl.ds(pl.multiple_of(gp + c * SEG_ALIGN, SEG_ALIGN), SEG_ALIGN)]
            (pltpu.make_async_copy(lo, gl, sem) if to_global else pltpu.make_async_copy(gl, lo, sem)).start()
            return carry

        lax.fori_loop(0, n, chunk, 0)
        return total + n

    return lax.fori_loop(0, N_EXPERTS, seg, 0)


def _wait_chunks(n, local, glob, sem):
    def one(c, carry):
        pltpu.make_async_copy(local.at[pl.ds(0, SEG_ALIGN)], glob.at[pl.ds(0, SEG_ALIGN)], sem).wait()
        return carry

    lax.fori_loop(0, n, one, 0)


def _sort_rows_kernel(lb_ref, nch_ref, gp_ref, tpos_ref, tn_ref, nt_ref, t_ref, route_ref, lbase_ref,
                      xs_ref, local_scr, zero_scr, sem):
    i = pl.program_id(0)
    tm = t_ref.shape[0]
    s_rows = local_scr.shape[0]
    lp1, lp2 = _local_positions(route_ref[...], lbase_ref[0])
    lane = lax.broadcasted_iota(jnp.int32, (tm, LANES), 1)
    lp_rows = jnp.where(lane == 0, lp1, jnp.where(lane == 1, lp2, -1.0)).T
    row = lax.broadcasted_iota(jnp.int32, (s_rows, tm), 0).astype(F32)
    onehot = jnp.where((row == lp_rows[0:1, :]) | (row == lp_rows[1:2, :]), 1.0, 0.0).astype(BF16)
    local_scr[...] = _dot(onehot, t_ref[...]).astype(BF16)
    n_started = _segment_copies(i, lb_ref, nch_ref, gp_ref, local_scr, xs_ref, sem, True)
    _wait_chunks(n_started, local_scr, xs_ref, sem)

    @pl.when(i == pl.num_programs(0) - 1)
    def _():
        zero_scr[...] = jnp.zeros_like(zero_scr)
        te = zero_scr.shape[0]
        zero_chunk = zero_scr.at[pl.ds(0, SEG_ALIGN)]

        def tail(e, total):
            def chunk(c, carry):
                dst = xs_ref.at[pl.ds(pl.multiple_of(tpos_ref[e] + c * SEG_ALIGN, SEG_ALIGN), SEG_ALIGN)]
                pltpu.make_async_copy(zero_chunk, dst, sem).start()
                return carry

            lax.fori_loop(0, tn_ref[e], chunk, 0)
            return total + tn_ref[e]

        _wait_chunks(lax.fori_loop(0, N_EXPERTS, tail, 0), zero_scr, xs_ref, sem)

        def unused_tile(j, carry):
            pltpu.make_async_copy(zero_scr, xs_ref.at[pl.ds(pl.multiple_of(j * te, te), te)], sem).start()
            return carry

        def unused_wait(j, carry):
            pltpu.make_async_copy(zero_scr, xs_ref.at[pl.ds(0, te)], sem).wait()
            return carry

        lax.fori_loop(nt_ref[0], xs_ref.shape[0] // te, unused_tile, 0)
        lax.fori_loop(nt_ref[0], xs_ref.shape[0] // te, unused_wait, 0)


def _plan_specs(plan):
    keys = ('lb', 'nch', 'gpos', 'tail_pos', 'tail_n')
    return [plan[k] for k in keys]


def _sort_rows(plan, n_tiles, t_bf, route, lbase_f, n_sorted):
    t, dm = t_bf.shape
    tm = TM_MIX
    s_rows = _local_rows(tm)
    im = lambda i, *_: (i, 0)
    return pl.pallas_call(
        _sort_rows_kernel,
        grid_spec=pltpu.PrefetchScalarGridSpec(
            num_scalar_prefetch=6,
            grid=(t // tm,),
            in_specs=[pl.BlockSpec((tm, dm), im), pl.BlockSpec((tm, LANES), im),
                      pl.BlockSpec((1, 1, LANES), lambda i, *_: (i, 0, 0))],
            out_specs=pl.BlockSpec(memory_space=pl.ANY),
            scratch_shapes=[pltpu.VMEM((s_rows, dm), BF16), pltpu.VMEM((TM_EXPERT, dm), BF16),
                            pltpu.SemaphoreType.DMA(())],
        ),
        out_shape=jax.ShapeDtypeStruct((n_sorted, dm), BF16),
        compiler_params=pltpu.CompilerParams(dimension_semantics=("arbitrary",),
                                             vmem_limit_bytes=VMEM_LIMIT),
        name="sort_rows",
    )(*_plan_specs(plan), n_tiles, t_bf, route, lbase_f)


def _experts_kernel(te_ref, nt_ref, xs_ref, wg_ref, wu_ref, wd_ref, ys_ref):
    i = pl.program_id(0)

    @pl.when(i < nt_ref[0])
    def _():
        x = xs_ref[...]
        gate = _dot(x, wg_ref[0].astype(BF16))
        up = _dot(x, wu_ref[0].astype(BF16))
        hidden = (jax.nn.silu(gate) * up).astype(BF16)
        ys_ref[...] = _dot(hidden, wd_ref[0].astype(BF16)).astype(BF16)

    @pl.when(i >= nt_ref[0])
    def _():
        ys_ref[...] = jnp.zeros_like(ys_ref)


def _experts(tile_expert, n_tiles, x_sorted, w_gate, w_up, w_down):
    n_sorted, dm = x_sorted.shape
    de = w_gate.shape[2]
    tm = TM_EXPERT
    return pl.pallas_call(
        _experts_kernel,
        grid_spec=pltpu.PrefetchScalarGridSpec(
            num_scalar_prefetch=2,
            grid=(n_sorted // tm,),
            in_specs=[pl.BlockSpec((tm, dm), lambda i, te, nt: (jnp.minimum(i, nt[0] - 1), 0)),
                      pl.BlockSpec((1, dm, de), lambda i, te, nt: (te[i], 0, 0)),
                      pl.BlockSpec((1, dm, de), lambda i, te, nt: (te[i], 0, 0)),
                      pl.BlockSpec((1, de, dm), lambda i, te, nt: (te[i], 0, 0))],
            out_specs=pl.BlockSpec((tm, dm), lambda i, te, nt: (i, 0)),
        ),
        out_shape=jax.ShapeDtypeStruct((n_sorted, dm), BF16),
        compiler_params=pltpu.CompilerParams(dimension_semantics=("arbitrary",),
                                             vmem_limit_bytes=VMEM_LIMIT),
        name="experts",
    )(tile_expert, n_tiles, x_sorted, w_gate, w_up, w_down)


def _combine_kernel(lb_ref, nch_ref, gp_ref, tpos_ref, tn_ref, x2_ref, route_ref, lbase_ref, gfin_ref, ys_ref,
                    o_ref, local_scr, sem):
    del tpos_ref, tn_ref
    i = pl.program_id(0)
    tm = x2_ref.shape[0]
    s_rows = local_scr.shape[0]
    local_scr[...] = jnp.zeros_like(local_scr)
    n_started = _segment_copies(i, lb_ref, nch_ref, gp_ref, local_scr, ys_ref, sem, False)
    route = route_ref[...]
    lp1, lp2 = _local_positions(route, lbase_ref[0])
    col = lax.broadcasted_iota(jnp.int32, (tm, s_rows), 1).astype(F32)
    pick = (jnp.where(col == lp1, route[:, R_W1:R_W1 + 1], 0.0)
            + jnp.where(col == lp2, route[:, R_W2:R_W2 + 1], 0.0)).astype(BF16)
    _wait_chunks(n_started, local_scr, ys_ref, sem)
    o_ref[...] = _rms(x2_ref[...] + _dot(pick, local_scr[...]), gfin_ref[...])


def _combine(plan, x2, route, lbase_f, gfin, y_sorted):
    t, dm = x2.shape
    tm = TM_MIX
    s_rows = _local_rows(tm)
    im = lambda i, *_: (i, 0)
    return pl.pallas_call(
        _combine_kernel,
        grid_spec=pltpu.PrefetchScalarGridSpec(
            num_scalar_prefetch=5,
            grid=(t // tm,),
            in_specs=[pl.BlockSpec((tm, dm), im), pl.BlockSpec((tm, LANES), im),
                      pl.BlockSpec((1, 1, LANES), lambda i, *_: (i, 0, 0)),
                      pl.BlockSpec((1, dm), lambda i, *_: (0, 0)),
                      pl.BlockSpec(memory_space=pl.ANY)],
            out_specs=pl.BlockSpec((tm, dm), im),
            scratch_shapes=[pltpu.VMEM((s_rows, dm), BF16), pltpu.SemaphoreType.DMA(())],
        ),
        out_shape=jax.ShapeDtypeStruct((t, dm), F32),
        compiler_params=pltpu.CompilerParams(dimension_semantics=("arbitrary",),
                                             vmem_limit_bytes=VMEM_LIMIT),
        name="combine_norm",
    )(*_plan_specs(plan), x2, route, lbase_f, gfin, y_sorted)


def _layer(x, p, s5_ops, gfin):
    b, l, dm = x.shape
    x2d = x.reshape(b * l, dm)
    mgm, us5 = _inproj_gmlp(x2d, p['gmix'], p['win'], p['lng'], p['lnb'], p['ws'], p['bs'], p['gout_gm'])
    n_seg = SUBLANES // b
    m, w1, w2, sc = s5_ops[(l // (S5_LC * n_seg))]
    xg = _s5_inproj(x, p['gmix'], p['win_s5'], n_seg)
    yg = _s5_scan(xg, m, w1, w2, sc, n_seg)
    ys = _s5_to_tokens(yg, b, l, n_seg)
    x2, t_bf, route, counts = _mix_route(ys, us5, mgm, x2d, p['d'], p['gluw'], p['glub'], p['gout_s5'],
                                         p['wout'], p['gffn'], p['rwh'], p['rwl'], p['rb'])
    plan, lbase_f, tile_expert, n_tiles, n_sorted = _segment_plan(counts, b * l, TM_EXPERT)
    x_sorted = _sort_rows(plan, n_tiles, t_bf, route, lbase_f, n_sorted)
    y_sorted = _experts(tile_expert, n_tiles, x_sorted, p['w_gate'], p['w_up'], p['w_down'])
    out = _combine(plan, x2, route, lbase_f, gfin, y_sorted)
    return out.reshape(b, l, dm)


def kernel(x_prompt, x_sample, norm_mix_g, w_in, gm_ln_g, gm_ln_b, gm_ws, gm_bs, s5_lam_re_fwd, s5_lam_im_fwd, s5_log_step_fwd, s5_b_re_fwd, s5_b_im_fwd, s5_c_re_fwd, s5_c_im_fwd, s5_lam_re_bwd, s5_lam_im_bwd, s5_log_step_bwd, s5_b_re_bwd, s5_b_im_bwd, s5_c_re_bwd, s5_c_im_bwd, s5_d, s5_glu_w, s5_glu_b, out_norm_gm, out_norm_s5, w_out, norm_ffn_g, r1_w, r1_b, r2_w, r2_b, e_w_gate, e_w_up, e_w_down, norm_final_g):
    depth = w_in.shape[0]
    gfin = norm_final_g.reshape(1, -1).astype(F32)
    xs = [x_prompt, x_sample]
    for li in range(depth):
        row = lambda a: a[li].reshape(1, -1).astype(F32)
        dm = w_in.shape[1]
        gw = gm_ln_g.shape[1]
        hd_dim = gw // GM_HEADS
        rw = jnp.concatenate([r1_w[li], r2_w[li].transpose(1, 0, 2).reshape(dm, N_EXPERTS)], axis=1).astype(F32)
        rw = jnp.pad(rw, ((0, 0), (0, LANES - rw.shape[1])))
        rwh = rw.astype(BF16)
        rwl = (rw - rwh.astype(F32)).astype(BF16)
        rb = jnp.concatenate([r1_b[li], r2_b[li].reshape(-1)]).astype(F32)
        rb = jnp.pad(rb, (0, LANES - rb.shape[0])).reshape(1, LANES)
        p = dict(
            gmix=row(norm_mix_g), win=w_in[li].astype(BF16), win_s5=w_in[li][:, 2 * gw:].astype(BF16),
            lng=row(gm_ln_g), lnb=row(gm_ln_b),
            ws=gm_ws[li].astype(BF16),
            bs=jnp.broadcast_to(gm_bs[li].astype(F32)[:, :, None], (GM_HEADS, CHUNK, hd_dim)),
            gout_gm=row(out_norm_gm), d=row(s5_d), gluw=s5_glu_w[li].astype(BF16), glub=row(s5_glu_b),
            gout_s5=row(out_norm_s5), wout=w_out[li].astype(BF16), gffn=row(norm_ffn_g),
            rwh=rwh, rwl=rwl, rb=rb,
            w_gate=e_w_gate[li], w_up=e_w_up[li], w_down=e_w_down[li],
        )
        fwd = (s5_lam_re_fwd[li], s5_lam_im_fwd[li], s5_log_step_fwd[li], s5_b_re_fwd[li], s5_b_im_fwd[li],
               s5_c_re_fwd[li], s5_c_im_fwd[li])
        bwd = (s5_lam_re_bwd[li], s5_lam_im_bwd[li], s5_log_step_bwd[li], s5_b_re_bwd[li], s5_b_im_bwd[li],
               s5_c_re_bwd[li], s5_c_im_bwd[li])
        s5_ops = {}
        for x in xs:
            seg_steps = x.shape[1] // (S5_LC * (SUBLANES // x.shape[0]))
            if seg_steps not in s5_ops:
                s5_ops[seg_steps] = _s5_operator(fwd, bwd, S5_LC, seg_steps)
        last = li == depth - 1
        assert last, "depth > 1 needs an un-normalised layer output"
        xs = [_layer(x, p, s5_ops, gfin) for x in xs]
    return tuple(xs)
```

```python
import functools
import math

import jax
import jax.numpy as jnp
from jax import lax
from jax.experimental import pallas as pl
from jax.experimental.pallas import tpu as pltpu

F32 = jnp.float32
BF16 = jnp.bfloat16

EPS = 1e-6
LAMBDA_RE_MAX = -1e-4
GM_HEADS = 4
CHUNK = 128
S5_GROUP = 16
S5_STATE = 64
N_COARSE = 4
N_FINE = 8
N_EXPERTS = N_COARSE * N_FINE

LANES = 128
SUBLANES = 8
S5_LC = 16
VMEM_LIMIT = 56 * 1024 * 1024

TM_PROJ = 512
TM_MIX = 512
TM_EXPERT = 512
SEG_ALIGN = 16


def _gelu(x):
    c = math.sqrt(2.0 / math.pi)
    return x * (0.5 * (1.0 + jnp.tanh(c * (x + 0.044715 * (x * x * x)))))


def _rms(x, g):
    ms = jnp.mean(x * x, axis=-1, keepdims=True)
    return x * lax.rsqrt(ms + EPS) * g


def _dot(a, b):
    return jnp.dot(a, b, preferred_element_type=F32)


def _inproj_gmlp_kernel(x_ref, gmix_ref, win_ref, lng_ref, lnb_ref, ws_ref, bs_ref, gout_ref,
                        mgm_ref, us5_ref, y_scr):
    tm = x_ref.shape[0]
    gw = mgm_ref.shape[1]
    hd_dim = gw // GM_HEADS
    n_chunks = tm // CHUNK
    h = _rms(x_ref[...], gmix_ref[...]).astype(BF16)
    proj = _dot(h, win_ref[...])
    us5_ref[...] = proj[:, 2 * gw:]
    u = _gelu(proj[:, :gw])
    v = _gelu(proj[:, gw:2 * gw])
    for hd in range(GM_HEADS):
        lo = hd * hd_dim
        vh = v[:, lo:lo + hd_dim]
        mu = jnp.mean(vh, axis=-1, keepdims=True)
        xc = vh - mu
        var = jnp.mean(xc * xc, axis=-1, keepdims=True)
        vn = (xc * lax.rsqrt(var + EPS) * lng_ref[:, lo:lo + hd_dim]
              + lnb_ref[:, lo:lo + hd_dim]).astype(BF16)
        rhs = jnp.concatenate([vn[c * CHUNK:(c + 1) * CHUNK] for c in range(n_chunks)], axis=1)
        s = _dot(ws_ref[hd], rhs)
        for c in range(n_chunks):
            sc = s[:, c * hd_dim:(c + 1) * hd_dim] + bs_ref[hd]
            y_scr[c * CHUNK:(c + 1) * CHUNK, lo:lo + hd_dim] = u[c * CHUNK:(c + 1) * CHUNK, lo:lo + hd_dim] * sc
    mgm_ref[...] = _rms(y_scr[...], gout_ref[...]).astype(BF16)


def _inproj_gmlp(x2d, gmix, win_bf, lng, lnb, ws_bf, bs_b, gout):
    t, d = x2d.shape
    d_in = win_bf.shape[1]
    gw = lng.shape[1]
    s5w = d_in - 2 * gw
    tm = TM_PROJ
    const = lambda *shape: pl.BlockSpec(shape, lambda i: (0,) * len(shape))
    return pl.pallas_call(
        _inproj_gmlp_kernel,
        grid=(t // tm,),
        in_specs=[
            pl.BlockSpec((tm, d), lambda i: (i, 0)),
            const(1, d), const(d, d_in), const(1, gw), const(1, gw),
            const(GM_HEADS, CHUNK, CHUNK), const(GM_HEADS, CHUNK, gw // GM_HEADS), const(1, gw),
        ],
        out_specs=[pl.BlockSpec((tm, gw), lambda i: (i, 0)),
                   pl.BlockSpec((tm, s5w), lambda i: (i, 0))],
        out_shape=[jax.ShapeDtypeStruct((t, gw), BF16),
                   jax.ShapeDtypeStruct((t, s5w), F32)],
        scratch_shapes=[pltpu.VMEM((tm, gw), F32)],
        compiler_params=pltpu.CompilerParams(dimension_semantics=("parallel",),
                                             vmem_limit_bytes=VMEM_LIMIT),
        name="inproj_gmlp",
    )(x2d, gmix, win_bf, lng, lnb, ws_bf, bs_b, gout)


def _s5_consts(lam_re, lam_im, log_step, b_re, b_im, c_re, c_im, lc):
    lr = jnp.minimum(lam_re.astype(F32), LAMBDA_RE_MAX)
    li = lam_im.astype(F32)
    step = jnp.exp(log_step.astype(F32))[:, None]
    dr, di = lr * step, li * step
    ar, ai = _cexp(dr, di)
    nr, ni = ar - 1.0, ai
    den = lr * lr + li * li
    qr, qi = (nr * lr + ni * li) / den, (ni * lr - nr * li) / den
    br, bi = b_re.astype(F32), b_im.astype(F32)
    bbr = qr[..., None] * br - qi[..., None] * bi
    bbi = qr[..., None] * bi + qi[..., None] * br
    k = jnp.arange(lc + 1, dtype=F32)[:, None, None]
    pwr, pwi = _cexp(k * dr[None], k * di[None])
    return (dr, di), (pwr, pwi), (bbr, bbi), (c_re.astype(F32), c_im.astype(F32))


def _cexp(zr, zi):
    m = jnp.exp(zr)
    return m * jnp.cos(zi), m * jnp.sin(zi)


def _s5_operator(fwd, bwd, lc, seg_steps):
    consts = [_s5_consts(*fwd, lc), _s5_consts(*bwd, lc)]
    g, p, h = consts[0][2][0].shape
    m = 0.0
    w1_parts, w2_parts, sc_rows, seg_rows = [], [], [], []
    for direction, (ld, pw, bb, c) in enumerate(consts):
        (dr, di), (pwr, pwi), (bbr, bbi), (cr, ci) = ld, pw, bb, c
        cpr = cr[None] * pwr[:, :, None, :] - ci[None] * pwi[:, :, None, :]
        cpi = cr[None] * pwi[:, :, None, :] + ci[None] * pwr[:, :, None, :]
        kk = (jnp.einsum('kghp,gpi->gkhi', cpr[:lc], bbr) - jnp.einsum('kghp,gpi->gkhi', cpi[:lc], bbi))
        shifted = jnp.stack([jnp.pad(kk[:, :lc - r], ((0, 0), (r, 0), (0, 0), (0, 0))) for r in range(lc)], axis=1)
        tk = shifted if direction == 0 else shifted.transpose(0, 2, 1, 3, 4)
        m = m + tk.transpose(0, 1, 4, 2, 3).reshape(g, lc * h, lc * h)
        er, ei = (pwr[:lc][::-1], pwi[:lc][::-1]) if direction == 0 else (pwr[:lc], pwi[:lc])
        wr = er[:, :, None, :] * bbr.transpose(0, 2, 1)[None] - ei[:, :, None, :] * bbi.transpose(0, 2, 1)[None]
        wi = er[:, :, None, :] * bbi.transpose(0, 2, 1)[None] + ei[:, :, None, :] * bbr.transpose(0, 2, 1)[None]
        wr, wi = wr.transpose(1, 0, 2, 3), wi.transpose(1, 0, 2, 3)
        w1_parts += [wr, wi, wi, wr]
        sel = slice(1, lc + 1)
        fr, fi = (cpr[sel], cpi[sel]) if direction == 0 else (cpr[sel][::-1], cpi[sel][::-1])
        w2_parts += [fr.transpose(1, 3, 0, 2), -fi.transpose(1, 3, 0, 2)]

        def mult(zr, zi):
            return [jnp.concatenate([zr, zr], -1), jnp.concatenate([-zi, zi], -1)]

        sc_rows += mult(*_cexp(lc * dr, lc * di))
        seg_rows += mult(*_cexp((lc * seg_steps) * dr, (lc * seg_steps) * di))
    w1 = jnp.concatenate(w1_parts, axis=-1).reshape(g, lc * h, 8 * p)
    w2 = jnp.concatenate(w2_parts, axis=1).reshape(g, 4 * p, lc * h)
    sc = jnp.stack(sc_rows + seg_rows, axis=1)
    return m.astype(BF16), w1.astype(BF16), w2.astype(BF16), sc.astype(F32)


def _s5_kernel(x_ref, m_ref, w1_ref, w2_ref, sc_ref, y_ref, loc_scr, sin_scr, *, n_seg):
    rows = x_ref.shape[1]
    steps = rows // SUBLANES
    sw = sc_ref.shape[2]
    x = x_ref[0]
    loc_scr[...] = _dot(x, w1_ref[0])

    def bc(i):
        return jnp.broadcast_to(sc_ref[0, i:i + 1, :], (SUBLANES, sw))

    a1f, a2f, a1b, a2b, p1f, p2f, p1b, p2b = [bc(i) for i in range(8)]

    def step_f(s, f, fs):
        r = pl.multiple_of(s * SUBLANES, SUBLANES)
        lf = loc_scr[pl.ds(r, SUBLANES), 0:sw]
        lfs = loc_scr[pl.ds(r, SUBLANES), sw:2 * sw]
        return a1f * f + a2f * fs + lf, a1f * fs - a2f * f + lfs

    def step_b(s, b, bs):
        r = pl.multiple_of((steps - 1 - s) * SUBLANES, SUBLANES)
        lb = loc_scr[pl.ds(r, SUBLANES), 2 * sw:3 * sw]
        lbs = loc_scr[pl.ds(r, SUBLANES), 3 * sw:4 * sw]
        return a1b * b + a2b * bs + lb, a1b * bs - a2b * b + lbs

    zero = jnp.zeros((SUBLANES, sw), F32)

    def pass1(s, carry):
        f, fs, b, bs = carry
        return step_f(s, f, fs) + step_b(s, b, bs)

    f_end, fs_end, b_end, bs_end = lax.fori_loop(0, steps, pass1, (zero, zero, zero, zero), unroll=4)

    seg = lax.broadcasted_iota(jnp.int32, (SUBLANES, sw), 0) % n_seg
    cf, cfs, cb, cbs = zero, zero, zero, zero
    for _ in range(n_seg - 1):
        ef = f_end + p1f * cf + p2f * cfs
        efs = fs_end + p1f * cfs - p2f * cf
        eb = b_end + p1b * cb + p2b * cbs
        ebs = bs_end + p1b * cbs - p2b * cb
        cf = jnp.where(seg >= 1, pltpu.roll(ef, 1, 0), 0.0)
        cfs = jnp.where(seg >= 1, pltpu.roll(efs, 1, 0), 0.0)
        cb = jnp.where(seg <= n_seg - 2, pltpu.roll(eb, SUBLANES - 1, 0), 0.0)
        cbs = jnp.where(seg <= n_seg - 2, pltpu.roll(ebs, SUBLANES - 1, 0), 0.0)

    def pass2(s, carry):
        f, fs, b, bs = carry
        rf = pl.multiple_of(s * SUBLANES, SUBLANES)
        rb = pl.multiple_of((steps - 1 - s) * SUBLANES, SUBLANES)
        sin_scr[pl.ds(rf, SUBLANES), 0:sw] = f
        sin_scr[pl.ds(rb, SUBLANES), sw:2 * sw] = b
        return step_f(s, f, fs) + step_b(s, b, bs)

    lax.fori_loop(0, steps, pass2, (cf, cfs, cb, cbs), unroll=4)

    y_ref[0] = _dot(x, m_ref[0]) + _dot(sin_scr[...].astype(BF16), w2_ref[0])


def _s5_scan(xg, m, w1, w2, sc, n_seg):
    g, rows, kw = xg.shape
    sw = sc.shape[2]
    blk = lambda a: pl.BlockSpec((1,) + a.shape[1:], lambda i: (i, 0, 0))
    return pl.pallas_call(
        functools.partial(_s5_kernel, n_seg=n_seg),
        grid=(g,),
        in_specs=[blk(xg), blk(m), blk(w1), blk(w2), blk(sc)],
        out_specs=pl.BlockSpec((1, rows, kw), lambda i: (i, 0, 0)),
        out_shape=jax.ShapeDtypeStruct((g, rows, kw), F32),
        scratch_shapes=[pltpu.VMEM((rows, 4 * sw), F32), pltpu.VMEM((rows, 2 * sw), F32)],
        compiler_params=pltpu.CompilerParams(dimension_semantics=("parallel",),
                                             vmem_limit_bytes=VMEM_LIMIT),
        name="s5_scan",
    )(xg, m, w1, w2, sc)


S5_NM = 8


def _block_transpose8(v, width):
    lane = lax.broadcasted_iota(jnp.int32, v[0].shape, 1)
    for d in (4, 2, 1):
        w = width * d
        hi = ((lane // w) % 2) == 1
        out = list(v)
        for i0 in range(8):
            if i0 & d:
                continue
            i1 = i0 + d
            out[i0] = jnp.where(hi, pltpu.roll(v[i1], w, 1), v[i0])
            out[i1] = jnp.where(hi, v[i1], pltpu.roll(v[i0], 8 * width - w, 1))
        v = out
    return v


def _tile_copies(hbm4, tile, buf, slot, sem, nm, to_hbm):
    copies = []
    for c in range(SUBLANES):
        for j in range(S5_LC):
            h = hbm4.at[c, pl.ds(tile * nm, nm), pl.ds(j, 1), :]
            v = buf.at[slot, j, :, pl.ds(c, 1), :]
            copies.append(pltpu.make_async_copy(v, h, sem.at[slot]) if to_hbm
                          else pltpu.make_async_copy(h, v, sem.at[slot]))
    return copies


def _s5_inproj_kernel(x4_ref, gmix_ref, w_ref, xg_ref, xs, sem, *, nm):
    i = pl.program_id(0)
    n = pl.num_programs(0)
    slot = i % 2
    dm = x4_ref.shape[3]

    @pl.when(i == 0)
    def _():
        for cp in _tile_copies(x4_ref, 0, xs, 0, sem, nm, False):
            cp.start()

    @pl.when(i + 1 < n)
    def _():
        for cp in _tile_copies(x4_ref, i + 1, xs, 1 - slot, sem, nm, False):
            cp.start()

    pltpu.make_async_copy(xs.at[slot], xs.at[slot], sem.at[slot]).wait()
    rows = nm * SUBLANES
    x = xs[slot].reshape(S5_LC * rows, dm)
    z = _dot(_rms(x, gmix_ref[...]).astype(BF16), w_ref[...])
    n_oct = z.shape[1] // LANES
    for q in range(n_oct):
        for a in range(S5_LC // 8):
            blocks = [z[(8 * a + j8) * rows:(8 * a + j8 + 1) * rows, q * LANES:(q + 1) * LANES] for j8 in range(8)]
            for g8, b in enumerate(_block_transpose8(blocks, S5_GROUP)):
                xg_ref[8 * q + g8, :, a * LANES:(a + 1) * LANES] = b.astype(BF16)


def _s5_inproj(x, gmix, w_s5_bf, n_seg):
    b, l, dm = x.shape
    steps = l // (S5_LC * n_seg)
    nm = S5_NM
    s5w = w_s5_bf.shape[1]
    g = s5w // S5_GROUP
    x4 = x.reshape(b * n_seg, steps, S5_LC, dm)
    return pl.pallas_call(
        functools.partial(_s5_inproj_kernel, nm=nm),
        grid=(steps // nm,),
        in_specs=[pl.BlockSpec(memory_space=pl.ANY),
                  pl.BlockSpec((1, dm), lambda i: (0, 0)),
                  pl.BlockSpec((dm, s5w), lambda i: (0, 0))],
        out_specs=pl.BlockSpec((g, nm * SUBLANES, S5_LC * S5_GROUP), lambda i: (0, i, 0)),
        out_shape=jax.ShapeDtypeStruct((g, steps * SUBLANES, S5_LC * S5_GROUP), BF16),
        scratch_shapes=[pltpu.VMEM((2, S5_LC, nm, SUBLANES, dm), F32), pltpu.SemaphoreType.DMA((2,))],
        compiler_params=pltpu.CompilerParams(dimension_semantics=("arbitrary",),
                                             vmem_limit_bytes=VMEM_LIMIT),
        name="s5_inproj",
    )(x4, gmix, w_s5_bf)


def _s5_to_tokens_kernel(yg_ref, ys4_ref, zs, sem, *, nm):
    i = pl.program_id(0)
    n = pl.num_programs(0)
    slot = i % 2
    rows = nm * SUBLANES

    def wait(s):
        pltpu.make_async_copy(zs.at[s], zs.at[s], sem.at[s]).wait()

    @pl.when(i >= 2)
    def _():
        wait(slot)

    n_oct = yg_ref.shape[0] // 8
    for q in range(n_oct):
        for a in range(S5_LC // 8):
            blocks = [yg_ref[8 * q + g8, :, a * LANES:(a + 1) * LANES] for g8 in range(8)]
            for j8, b in enumerate(_block_transpose8(blocks, S5_GROUP)):
                zs[slot, 8 * a + j8, :, :, q * LANES:(q + 1) * LANES] = b.reshape(nm, SUBLANES, LANES)
    for cp in _tile_copies(ys4_ref, i, zs, slot, sem, nm, True):
        cp.start()

    @pl.when(i == n - 1)
    def _():
        wait(1 - slot)
        wait(slot)


def _s5_to_tokens(yg, b, l, n_seg):
    g, rows_total, kw = yg.shape
    steps = rows_total // SUBLANES
    nm = S5_NM
    s5w = g * S5_GROUP
    assert steps // nm >= 2
    ys4 = pl.pallas_call(
        functools.partial(_s5_to_tokens_kernel, nm=nm),
        grid=(steps // nm,),
        in_specs=[pl.BlockSpec((g, nm * SUBLANES, kw), lambda i: (0, i, 0))],
        out_specs=pl.BlockSpec(memory_space=pl.ANY),
        out_shape=jax.ShapeDtypeStruct((b * n_seg, steps, S5_LC, s5w), F32),
        scratch_shapes=[pltpu.VMEM((2, S5_LC, nm, SUBLANES, s5w), F32), pltpu.SemaphoreType.DMA((2,))],
        compiler_params=pltpu.CompilerParams(dimension_semantics=("arbitrary",),
                                             vmem_limit_bytes=VMEM_LIMIT),
        name="s5_to_tokens",
    )(yg)
    return ys4.reshape(b * l, s5w)


R_E1, R_E2, R_W1, R_W2, R_RANK1, R_RANK2 = range(6)


def _mix_route_kernel(ys_ref, us5_ref, mgm_ref, x_ref, d_ref, gluw_ref, glub_ref, gs5_ref,
                      wout_ref, gffn_ref, rwh_ref, rwl_ref, rb_ref, tri_ref,
                      x2_ref, t_ref, route_ref, cnt_ref):
    gw = mgm_ref.shape[1]
    y = ys_ref[...] + d_ref[...] * us5_ref[...]
    g = _gelu(y)
    z = g * jax.nn.sigmoid(_dot(g.astype(BF16), gluw_ref[...]) + glub_ref[...])
    ms5 = _rms(z, gs5_ref[...]).astype(BF16)
    mix = _dot(mgm_ref[...], wout_ref[:gw, :]) + _dot(ms5, wout_ref[gw:, :])
    x2 = x_ref[...] + mix
    x2_ref[...] = x2
    t = _rms(x2, gffn_ref[...])
    t_hi = t.astype(BF16)
    t_ref[...] = t_hi
    t_lo = (t - t_hi.astype(F32)).astype(BF16)
    logits = (_dot(t_hi, rwh_ref[...]) + _dot(t_hi, rwl_ref[...]) + _dot(t_lo, rwh_ref[...])
              + rb_ref[...])
    lane = lax.broadcasted_iota(jnp.int32, logits.shape, 1).astype(F32)
    neg = jnp.float32(-jnp.inf)

    def first_max(mask):
        vals = jnp.where(mask, logits, neg)
        mx = jnp.max(vals, axis=-1, keepdims=True)
        idx = jnp.min(jnp.where(mask & (vals == mx), lane, float(LANES)), axis=-1, keepdims=True)
        return mx, idx

    coarse = lane < N_COARSE
    m1, grp = first_max(coarse)
    p_grp = 1.0 / jnp.sum(jnp.where(coarse, jnp.exp(logits - m1), 0.0), axis=-1, keepdims=True)
    lo = N_COARSE + grp * N_FINE
    fine = (lane >= lo) & (lane < lo + N_FINE)
    v1, i1 = first_max(fine)
    v2, i2 = first_max(fine & (lane != i1))
    e21 = jnp.exp(v2 - v1)
    w1 = p_grp / (1.0 + e21)
    w2 = p_grp * e21 / (1.0 + e21)
    e1 = i1 - N_COARSE
    e2 = i2 - N_COARSE
    hit1 = lane == e1
    hit2 = lane == e2
    onehot = jnp.where(hit1 | hit2, 1.0, 0.0)
    before = _dot(tri_ref[...], onehot.astype(BF16))
    rank1 = jnp.sum(jnp.where(hit1, before, 0.0), axis=-1, keepdims=True)
    rank2 = jnp.sum(jnp.where(hit2, before, 0.0), axis=-1, keepdims=True)
    tm = onehot.shape[0]
    cnt_ref[0] = before[tm - 1:tm, :] + onehot[tm - 1:tm, :]
    rec = jnp.zeros_like(logits)
    for slot, val in ((R_E1, e1), (R_E2, e2), (R_W1, w1), (R_W2, w2),
                      (R_RANK1, rank1), (R_RANK2, rank2)):
        rec = jnp.where(lane == slot, val, rec)
    route_ref[...] = rec


def _mix_route(ys, us5, mgm, x2d, d, gluw_bf, glub, gs5, wout_bf, gffn, rwh, rwl, rb):
    t, dm = x2d.shape
    gw = mgm.shape[1]
    s5w = us5.shape[1]
    tm = TM_MIX
    tri = jnp.tril(jnp.ones((tm, tm), F32), -1).astype(BF16)
    const = lambda *shape: pl.BlockSpec(shape, lambda i: (0,) * len(shape))
    tile = lambda w: pl.BlockSpec((tm, w), lambda i: (i, 0))
    return pl.pallas_call(
        _mix_route_kernel,
        grid=(t // tm,),
        in_specs=[tile(s5w), tile(s5w), tile(gw), tile(dm),
                  const(1, s5w), const(s5w, s5w), const(1, s5w), const(1, s5w),
                  const(gw + s5w, dm), const(1, dm), const(dm, LANES), const(dm, LANES), const(1, LANES),
                  const(tm, tm)],
        out_specs=[tile(dm), tile(dm), tile(LANES), pl.BlockSpec((1, 1, LANES), lambda i: (i, 0, 0))],
        out_shape=[jax.ShapeDtypeStruct((t, dm), F32),
                   jax.ShapeDtypeStruct((t, dm), BF16),
                   jax.ShapeDtypeStruct((t, LANES), F32),
                   jax.ShapeDtypeStruct((t // tm, 1, LANES), F32)],
        compiler_params=pltpu.CompilerParams(dimension_semantics=("parallel",),
                                             vmem_limit_bytes=VMEM_LIMIT),
        name="mix_route",
    )(ys, us5, mgm, x2d, d, gluw_bf, glub, gs5, wout_bf, gffn, rwh, rwl, rb, tri)


def _local_rows(tm):
    worst = 2 * tm + N_EXPERTS * (SEG_ALIGN - 1)
    return -(-worst // LANES) * LANES


def _segment_plan(cnt, t, tm_expert):
    c = cnt[:, 0, :N_EXPERTS].astype(jnp.int32)
    n_tok_tiles = c.shape[0]
    al = (c + SEG_ALIGN - 1) // SEG_ALIGN * SEG_ALIGN
    lbase = jnp.cumsum(al, axis=1) - al
    tot = jnp.sum(al, axis=0)
    tot_pad = (tot + tm_expert - 1) // tm_expert * tm_expert
    gbase = jnp.cumsum(tot_pad) - tot_pad
    gpos = gbase[None, :] + jnp.cumsum(al, axis=0) - al
    n_tiles_max = -(-(2 * t + n_tok_tiles * N_EXPERTS * (SEG_ALIGN - 1)) // tm_expert) + N_EXPERTS
    tile_end = jnp.cumsum(tot_pad // tm_expert)
    n_tiles = tile_end[-1:].astype(jnp.int32)
    tile_idx = jnp.arange(n_tiles_max, dtype=jnp.int32)
    tile_expert = jnp.sum((tile_idx[:, None] >= tile_end[None, :]).astype(jnp.int32), axis=1)
    last = jnp.sum((n_tiles - 1 >= tile_end).astype(jnp.int32))
    tile_expert = jnp.where(tile_idx < n_tiles, tile_expert, last).astype(jnp.int32)
    lbase_f = jnp.pad(lbase.astype(F32), ((0, 0), (0, LANES - N_EXPERTS)))[:, None, :]
    flat = lambda a: a.reshape(-1).astype(jnp.int32)
    plan = dict(lb=flat(lbase), nch=flat(al // SEG_ALIGN), gpos=flat(gpos),
                tail_pos=flat(gbase + tot), tail_n=flat((tot_pad - tot) // SEG_ALIGN))
    return plan, lbase_f, tile_expert, n_tiles, n_tiles_max * tm_expert


def _local_positions(route, lbase):
    lane = lax.broadcasted_iota(jnp.int32, route.shape, 1).astype(F32)
    out = []
    for e_lane, r_lane in ((R_E1, R_RANK1), (R_E2, R_RANK2)):
        e = route[:, e_lane:e_lane + 1]
        base = jnp.sum(jnp.where(lane == e, lbase, 0.0), axis=-1, keepdims=True)
        out.append(base + route[:, r_lane:r_lane + 1])
    return out


def _segment_copies(i, lb_ref, nch_ref, gp_ref, local, glob, sem, to_global):
    def seg(e, total):
        k = i * N_EXPERTS + e
        lb, n, gp = lb_ref[k], nch_ref[k], gp_ref[k]

        def chunk(c, carry):
            lo = local.at[pl.ds(pl.multiple_of(lb + c * SEG_ALIGN, SEG_ALIGN), SEG_ALIGN)]
            gl = glob.at[pl.ds(pl.multiple_of(gp + c * SEG_ALIGN, SEG_ALIGN), SEG_ALIGN)]
            (pltpu.make_async_copy(lo, gl, sem) if to_global else pltpu.make_async_copy(gl, lo, sem)).start()
            return carry

        lax.fori_loop(0, n, chunk, 0)
        return total + n

    return lax.fori_loop(0, N_EXPERTS, seg, 0)


def _wait_chunks(n, local, glob, sem):
    def one(c, carry):
        pltpu.make_async_copy(local.at[pl.ds(0, SEG_ALIGN)], glob.at[pl.ds(0, SEG_ALIGN)], sem).wait()
        return carry

    lax.fori_loop(0, n, one, 0)


def _sort_rows_kernel(lb_ref, nch_ref, gp_ref, tpos_ref, tn_ref, nt_ref, t_ref, route_ref, lbase_ref,
                      xs_ref, local_scr, zero_scr, cnt_scr, sem, zsem):
    i = pl.program_id(0)
    n = pl.num_programs(0)
    slot = i % 2
    tm = t_ref.shape[0]
    s_rows = local_scr.shape[1]
    local = local_scr.at[slot]

    @pl.when(i >= 2)
    def _():
        _wait_chunks(cnt_scr[slot], local, xs_ref, sem.at[slot])

    lp1, lp2 = _local_positions(route_ref[...], lbase_ref[0])
    lane = lax.broadcasted_iota(jnp.int32, (tm, LANES), 1)
    lp_rows = jnp.where(lane == 0, lp1, jnp.where(lane == 1, lp2, -1.0)).T
    row = lax.broadcasted_iota(jnp.int32, (s_rows, tm), 0).astype(F32)
    onehot = jnp.where((row == lp_rows[0:1, :]) | (row == lp_rows[1:2, :]), 1.0, 0.0).astype(BF16)
    local_scr[slot] = _dot(onehot, t_ref[...]).astype(BF16)
    cnt_scr[slot] = _segment_copies(i, lb_ref, nch_ref, gp_ref, local, xs_ref, sem.at[slot], True)

    @pl.when(i == n - 1)
    def _():
        @pl.when(n >= 2)
        def _():
            _wait_chunks(cnt_scr[1 - slot], local_scr.at[1 - slot], xs_ref, sem.at[1 - slot])

        _wait_chunks(cnt_scr[slot], local, xs_ref, sem.at[slot])
        zero_scr[...] = jnp.zeros_like(zero_scr)
        te = zero_scr.shape[0]
        zero_chunk = zero_scr.at[pl.ds(0, SEG_ALIGN)]

        def tail(e, total):
            def chunk(c, carry):
                dst = xs_ref.at[pl.ds(pl.multiple_of(tpos_ref[e] + c * SEG_ALIGN, SEG_ALIGN), SEG_ALIGN)]
                pltpu.make_async_copy(zero_chunk, dst, zsem).start()
                return carry

            lax.fori_loop(0, tn_ref[e], chunk, 0)
            return total + tn_ref[e]

        _wait_chunks(lax.fori_loop(0, N_EXPERTS, tail, 0), zero_scr, xs_ref, zsem)

        def unused_tile(j, carry):
            pltpu.make_async_copy(zero_scr, xs_ref.at[pl.ds(pl.multiple_of(j * te, te), te)], zsem).start()
            return carry

        def unused_wait(j, carry):
            pltpu.make_async_copy(zero_scr, xs_ref.at[pl.ds(0, te)], zsem).wait()
            return carry

        lax.fori_loop(nt_ref[0], xs_ref.shape[0] // te, unused_tile, 0)
        lax.fori_loop(nt_ref[0], xs_ref.shape[0] // te, unused_wait, 0)


def _plan_specs(plan):
    keys = ('lb', 'nch', 'gpos', 'tail_pos', 'tail_n')
    return [plan[k] for k in keys]


def _sort_rows(plan, n_tiles, t_bf, route, lbase_f, n_sorted):
    t, dm = t_bf.shape
    tm = TM_MIX
    s_rows = _local_rows(tm)
    im = lambda i, *_: (i, 0)
    return pl.pallas_call(
        _sort_rows_kernel,
        grid_spec=pltpu.PrefetchScalarGridSpec(
            num_scalar_prefetch=6,
            grid=(t // tm,),
            in_specs=[pl.BlockSpec((tm, dm), im), pl.BlockSpec((tm, LANES), im),
                      pl.BlockSpec((1, 1, LANES), lambda i, *_: (i, 0, 0))],
            out_specs=pl.BlockSpec(memory_space=pl.ANY),
            scratch_shapes=[pltpu.VMEM((2, s_rows, dm), BF16), pltpu.VMEM((TM_EXPERT, dm), BF16),
                            pltpu.SMEM((2,), jnp.int32), pltpu.SemaphoreType.DMA((2,)),
                            pltpu.SemaphoreType.DMA(())],
        ),
        out_shape=jax.ShapeDtypeStruct((n_sorted, dm), BF16),
        compiler_params=pltpu.CompilerParams(dimension_semantics=("arbitrary",),
                                             vmem_limit_bytes=VMEM_LIMIT),
        name="sort_rows",
    )(*_plan_specs(plan), n_tiles, t_bf, route, lbase_f)


def _experts_kernel(te_ref, nt_ref, xs_ref, wg_ref, wu_ref, wd_ref, ys_ref):
    i = pl.program_id(0)

    @pl.when(i < nt_ref[0])
    def _():
        x = xs_ref[...]
        gate = _dot(x, wg_ref[0].astype(BF16))
        up = _dot(x, wu_ref[0].astype(BF16))
        hidden = (jax.nn.silu(gate) * up).astype(BF16)
        ys_ref[...] = _dot(hidden, wd_ref[0].astype(BF16)).astype(BF16)

    @pl.when(i >= nt_ref[0])
    def _():
        ys_ref[...] = jnp.zeros_like(ys_ref)


def _experts(tile_expert, n_tiles, x_sorted, w_gate, w_up, w_down):
    n_sorted, dm = x_sorted.shape
    de = w_gate.shape[2]
    tm = TM_EXPERT
    return pl.pallas_call(
        _experts_kernel,
        grid_spec=pltpu.PrefetchScalarGridSpec(
            num_scalar_prefetch=2,
            grid=(n_sorted // tm,),
            in_specs=[pl.BlockSpec((tm, dm), lambda i, te, nt: (jnp.minimum(i, nt[0] - 1), 0)),
                      pl.BlockSpec((1, dm, de), lambda i, te, nt: (te[i], 0, 0)),
                      pl.BlockSpec((1, dm, de), lambda i, te, nt: (te[i], 0, 0)),
                      pl.BlockSpec((1, de, dm), lambda i, te, nt: (te[i], 0, 0))],
            out_specs=pl.BlockSpec((tm, dm), lambda i, te, nt: (i, 0)),
        ),
        out_shape=jax.ShapeDtypeStruct((n_sorted, dm), BF16),
        compiler_params=pltpu.CompilerParams(dimension_semantics=("arbitrary",),
                                             vmem_limit_bytes=VMEM_LIMIT),
        name="experts",
    )(tile_expert, n_tiles, x_sorted, w_gate, w_up, w_down)


def _combine_kernel(lb_ref, nch_ref, gp_ref, tpos_ref, tn_ref, x2_ref, route_ref, lbase_ref, gfin_ref, ys_ref,
                    o_ref, local_scr, cnt_scr, sem):
    del tpos_ref, tn_ref
    i = pl.program_id(0)
    n = pl.num_programs(0)
    slot = i % 2
    tm = x2_ref.shape[0]
    s_rows = local_scr.shape[1]

    def fetch(tile, s):
        last = tile * N_EXPERTS + N_EXPERTS - 1
        used = lb_ref[last] + nch_ref[last] * SEG_ALIGN

        def clear(r, carry):
            local_scr[s, pl.ds(pl.multiple_of(r * SEG_ALIGN, SEG_ALIGN), SEG_ALIGN), :] = jnp.zeros(
                (SEG_ALIGN, local_scr.shape[2]), BF16)
            return carry

        lax.fori_loop(used // SEG_ALIGN, s_rows // SEG_ALIGN, clear, 0)
        cnt_scr[s] = _segment_copies(tile, lb_ref, nch_ref, gp_ref, local_scr.at[s], ys_ref, sem.at[s], False)

    @pl.when(i == 0)
    def _():
        fetch(0, 0)

    @pl.when(i + 1 < n)
    def _():
        fetch(i + 1, 1 - slot)

    route = route_ref[...]
    lp1, lp2 = _local_positions(route, lbase_ref[0])
    col = lax.broadcasted_iota(jnp.int32, (tm, s_rows), 1).astype(F32)
    pick = (jnp.where(col == lp1, route[:, R_W1:R_W1 + 1], 0.0)
            + jnp.where(col == lp2, route[:, R_W2:R_W2 + 1], 0.0)).astype(BF16)
    _wait_chunks(cnt_scr[slot], local_scr.at[slot], ys_ref, sem.at[slot])
    o_ref[...] = _rms(x2_ref[...] + _dot(pick, local_scr[slot]), gfin_ref[...])


def _combine(plan, x2, route, lbase_f, gfin, y_sorted):
    t, dm = x2.shape
    tm = TM_MIX
    s_rows = _local_rows(tm)
    im = lambda i, *_: (i, 0)
    return pl.pallas_call(
        _combine_kernel,
        grid_spec=pltpu.PrefetchScalarGridSpec(
            num_scalar_prefetch=5,
            grid=(t // tm,),
            in_specs=[pl.BlockSpec((tm, dm), im), pl.BlockSpec((tm, LANES), im),
                      pl.BlockSpec((1, 1, LANES), lambda i, *_: (i, 0, 0)),
                      pl.BlockSpec((1, dm), lambda i, *_: (0, 0)),
                      pl.BlockSpec(memory_space=pl.ANY)],
            out_specs=pl.BlockSpec((tm, dm), im),
            scratch_shapes=[pltpu.VMEM((2, s_rows, dm), BF16), pltpu.SMEM((2,), jnp.int32),
                            pltpu.SemaphoreType.DMA((2,))],
        ),
        out_shape=jax.ShapeDtypeStruct((t, dm), F32),
        compiler_params=pltpu.CompilerParams(dimension_semantics=("arbitrary",),
                                             vmem_limit_bytes=VMEM_LIMIT),
        name="combine_norm",
    )(*_plan_specs(plan), x2, route, lbase_f, gfin, y_sorted)


def _layer(x, p, s5_ops, gfin):
    b, l, dm = x.shape
    x2d = x.reshape(b * l, dm)
    mgm, us5 = _inproj_gmlp(x2d, p['gmix'], p['win'], p['lng'], p['lnb'], p['ws'], p['bs'], p['gout_gm'])
    n_seg = SUBLANES // b
    m, w1, w2, sc = s5_ops[(l // (S5_LC * n_seg))]
    xg = _s5_inproj(x, p['gmix'], p['win_s5'], n_seg)
    yg = _s5_scan(xg, m, w1, w2, sc, n_seg)
    ys = _s5_to_tokens(yg, b, l, n_seg)
    x2, t_bf, route, counts = _mix_route(ys, us5, mgm, x2d, p['d'], p['gluw'], p['glub'], p['gout_s5'],
                                         p['wout'], p['gffn'], p['rwh'], p['rwl'], p['rb'])
    plan, lbase_f, tile_expert, n_tiles, n_sorted = _segment_plan(counts, b * l, TM_EXPERT)
    x_sorted = _sort_rows(plan, n_tiles, t_bf, route, lbase_f, n_sorted)
    y_sorted = _experts(tile_expert, n_tiles, x_sorted, p['w_gate'], p['w_up'], p['w_down'])
    out = _combine(plan, x2, route, lbase_f, gfin, y_sorted)
    return out.reshape(b, l, dm)


def kernel(x_prompt, x_sample, norm_mix_g, w_in, gm_ln_g, gm_ln_b, gm_ws, gm_bs, s5_lam_re_fwd, s5_lam_im_fwd, s5_log_step_fwd, s5_b_re_fwd, s5_b_im_fwd, s5_c_re_fwd, s5_c_im_fwd, s5_lam_re_bwd, s5_lam_im_bwd, s5_log_step_bwd, s5_b_re_bwd, s5_b_im_bwd, s5_c_re_bwd, s5_c_im_bwd, s5_d, s5_glu_w, s5_glu_b, out_norm_gm, out_norm_s5, w_out, norm_ffn_g, r1_w, r1_b, r2_w, r2_b, e_w_gate, e_w_up, e_w_down, norm_final_g):
    depth = w_in.shape[0]
    gfin = norm_final_g.reshape(1, -1).astype(F32)
    xs = [x_prompt, x_sample]
    for li in range(depth):
        row = lambda a: a[li].reshape(1, -1).astype(F32)
        dm = w_in.shape[1]
        gw = gm_ln_g.shape[1]
        hd_dim = gw // GM_HEADS
        rw = jnp.concatenate([r1_w[li], r2_w[li].transpose(1, 0, 2).reshape(dm, N_EXPERTS)], axis=1).astype(F32)
        rw = jnp.pad(rw, ((0, 0), (0, LANES - rw.shape[1])))
        rwh = rw.astype(BF16)
        rwl = (rw - rwh.astype(F32)).astype(BF16)
        rb = jnp.concatenate([r1_b[li], r2_b[li].reshape(-1)]).astype(F32)
        rb = jnp.pad(rb, (0, LANES - rb.shape[0])).reshape(1, LANES)
        p = dict(
            gmix=row(norm_mix_g), win=w_in[li].astype(BF16), win_s5=w_in[li][:, 2 * gw:].astype(BF16),
            lng=row(gm_ln_g), lnb=row(gm_ln_b),
            ws=gm_ws[li].astype(BF16),
            bs=jnp.broadcast_to(gm_bs[li].astype(F32)[:, :, None], (GM_HEADS, CHUNK, hd_dim)),
            gout_gm=row(out_norm_gm), d=row(s5_d), gluw=s5_glu_w[li].astype(BF16), glub=row(s5_glu_b),
            gout_s5=row(out_norm_s5), wout=w_out[li].astype(BF16), gffn=row(norm_ffn_g),
            rwh=rwh, rwl=rwl, rb=rb,
            w_gate=e_w_gate[li], w_up=e_w_up[li], w_down=e_w_down[li],
        )
        fwd = (s5_lam_re_fwd[li], s5_lam_im_fwd[li], s5_log_step_fwd[li], s5_b_re_fwd[li], s5_b_im_fwd[li],
               s5_c_re_fwd[li], s5_c_im_fwd[li])
        bwd = (s5_lam_re_bwd[li], s5_lam_im_bwd[li], s5_log_step_bwd[li], s5_b_re_bwd[li], s5_b_im_bwd[li],
               s5_c_re_bwd[li], s5_c_im_bwd[li])
        s5_ops = {}
        for x in xs:
            seg_steps = x.shape[1] // (S5_LC * (SUBLANES // x.shape[0]))
            if seg_steps not in s5_ops:
                s5_ops[seg_steps] = _s5_operator(fwd, bwd, S5_LC, seg_steps)
        last = li == depth - 1
        assert last, "depth > 1 needs an un-normalised layer output"
        xs = [_layer(x, p, s5_ops, gfin) for x in xs]
    return tuple(xs)
```

```python
import functools
import math

import jax
import jax.numpy as jnp
from jax import lax
from jax.experimental import pallas as pl
from jax.experimental.pallas import tpu as pltpu

F32 = jnp.float32
BF16 = jnp.bfloat16

EPS = 1e-6
LAMBDA_RE_MAX = -1e-4
GM_HEADS = 4
CHUNK = 128
S5_GROUP = 16
S5_STATE = 64
N_COARSE = 4
N_FINE = 8
N_EXPERTS = N_COARSE * N_FINE

LANES = 128
SUBLANES = 8
S5_LC = 16
VMEM_LIMIT = 56 * 1024 * 1024

TM_PROJ = 512
TM_MIX = 512
TM_EXPERT = 512
SEG_ALIGN = 16


def _gelu(x):
    c = math.sqrt(2.0 / math.pi)
    return x * (0.5 * (1.0 + jnp.tanh(c * (x + 0.044715 * (x * x * x)))))


def _rms(x, g):
    ms = jnp.mean(x * x, axis=-1, keepdims=True)
    return x * lax.rsqrt(ms + EPS) * g


def _dot(a, b):
    return jnp.dot(a, b, preferred_element_type=F32)


def _inproj_gmlp_kernel(x_ref, gmix_ref, win_ref, lng_ref, lnb_ref, ws_ref, bs_ref, gout_ref,
                        mgm_ref, y_scr):
    tm = x_ref.shape[0]
    gw = mgm_ref.shape[1]
    hd_dim = gw // GM_HEADS
    n_chunks = tm // CHUNK
    h = _rms(x_ref[...], gmix_ref[...]).astype(BF16)
    proj = _dot(h, win_ref[...])
    u = _gelu(proj[:, :gw])
    v = _gelu(proj[:, gw:2 * gw])
    for hd in range(GM_HEADS):
        lo = hd * hd_dim
        vh = v[:, lo:lo + hd_dim]
        mu = jnp.mean(vh, axis=-1, keepdims=True)
        xc = vh - mu
        var = jnp.mean(xc * xc, axis=-1, keepdims=True)
        vn = (xc * lax.rsqrt(var + EPS) * lng_ref[:, lo:lo + hd_dim]
              + lnb_ref[:, lo:lo + hd_dim]).astype(BF16)
        rhs = jnp.concatenate([vn[c * CHUNK:(c + 1) * CHUNK] for c in range(n_chunks)], axis=1)
        s = _dot(ws_ref[hd], rhs)
        for c in range(n_chunks):
            sc = s[:, c * hd_dim:(c + 1) * hd_dim] + bs_ref[hd]
            y_scr[c * CHUNK:(c + 1) * CHUNK, lo:lo + hd_dim] = u[c * CHUNK:(c + 1) * CHUNK, lo:lo + hd_dim] * sc
    mgm_ref[...] = _rms(y_scr[...], gout_ref[...]).astype(BF16)


def _inproj_gmlp(x2d, gmix, win_bf, lng, lnb, ws_bf, bs_b, gout):
    t, d = x2d.shape
    d_in = win_bf.shape[1]
    gw = lng.shape[1]
    tm = TM_PROJ
    const = lambda *shape: pl.BlockSpec(shape, lambda i: (0,) * len(shape))
    return pl.pallas_call(
        _inproj_gmlp_kernel,
        grid=(t // tm,),
        in_specs=[
            pl.BlockSpec((tm, d), lambda i: (i, 0)),
            const(1, d), const(d, d_in), const(1, gw), const(1, gw),
            const(GM_HEADS, CHUNK, CHUNK), const(GM_HEADS, CHUNK, gw // GM_HEADS), const(1, gw),
        ],
        out_specs=pl.BlockSpec((tm, gw), lambda i: (i, 0)),
        out_shape=jax.ShapeDtypeStruct((t, gw), BF16),
        scratch_shapes=[pltpu.VMEM((tm, gw), F32)],
        compiler_params=pltpu.CompilerParams(dimension_semantics=("parallel",),
                                             vmem_limit_bytes=VMEM_LIMIT),
        name="inproj_gmlp",
    )(x2d, gmix, win_bf, lng, lnb, ws_bf, bs_b, gout)


def _s5_consts(lam_re, lam_im, log_step, b_re, b_im, c_re, c_im, lc):
    lr = jnp.minimum(lam_re.astype(F32), LAMBDA_RE_MAX)
    li = lam_im.astype(F32)
    step = jnp.exp(log_step.astype(F32))[:, None]
    dr, di = lr * step, li * step
    ar, ai = _cexp(dr, di)
    nr, ni = ar - 1.0, ai
    den = lr * lr + li * li
    qr, qi = (nr * lr + ni * li) / den, (ni * lr - nr * li) / den
    br, bi = b_re.astype(F32), b_im.astype(F32)
    bbr = qr[..., None] * br - qi[..., None] * bi
    bbi = qr[..., None] * bi + qi[..., None] * br
    k = jnp.arange(lc + 1, dtype=F32)[:, None, None]
    pwr, pwi = _cexp(k * dr[None], k * di[None])
    return (dr, di), (pwr, pwi), (bbr, bbi), (c_re.astype(F32), c_im.astype(F32))


def _cexp(zr, zi):
    m = jnp.exp(zr)
    return m * jnp.cos(zi), m * jnp.sin(zi)


def _s5_operator(fwd, bwd, d, lc, seg_steps):
    consts = [_s5_consts(*fwd, lc), _s5_consts(*bwd, lc)]
    g, p, h = consts[0][2][0].shape
    m = 0.0
    w1_parts, w2_parts, sc_rows, seg_rows = [], [], [], []
    for direction, (ld, pw, bb, c) in enumerate(consts):
        (dr, di), (pwr, pwi), (bbr, bbi), (cr, ci) = ld, pw, bb, c
        crt, cit = cr.transpose(0, 2, 1), ci.transpose(0, 2, 1)
        pwrt, pwit = pwr.transpose(1, 2, 0), pwi.transpose(1, 2, 0)
        cpr = crt[:, :, None, :] * pwrt[:, :, :, None] - cit[:, :, None, :] * pwit[:, :, :, None]
        cpi = crt[:, :, None, :] * pwit[:, :, :, None] + cit[:, :, None, :] * pwrt[:, :, :, None]
        kk = (jnp.einsum('gpi,gpkh->gikh', bbr, cpr[:, :, :lc]) - jnp.einsum('gpi,gpkh->gikh', bbi, cpi[:, :, :lc]))
        rows = []
        for s in range(lc):
            if direction == 0:
                rows.append(jnp.pad(kk[:, :, :lc - s], ((0, 0), (0, 0), (s, 0), (0, 0))))
            else:
                rows.append(jnp.pad(jnp.flip(kk[:, :, :s + 1], 2), ((0, 0), (0, 0), (0, lc - 1 - s), (0, 0))))
        m = m + jnp.stack(rows, axis=1).reshape(g, lc * h, lc * h)
        er, ei = pwrt[:, :, :lc].transpose(0, 2, 1), pwit[:, :, :lc].transpose(0, 2, 1)
        if direction == 0:
            er, ei = jnp.flip(er, 1), jnp.flip(ei, 1)
        bbrt, bbit = bbr.transpose(0, 2, 1), bbi.transpose(0, 2, 1)
        wr = er[:, :, None, :] * bbrt[:, None] - ei[:, :, None, :] * bbit[:, None]
        wi = er[:, :, None, :] * bbit[:, None] + ei[:, :, None, :] * bbrt[:, None]
        w1_parts += [wr, wi, wi, wr]
        fr, fi = cpr[:, :, 1:lc + 1], cpi[:, :, 1:lc + 1]
        if direction == 1:
            fr, fi = jnp.flip(fr, 2), jnp.flip(fi, 2)
        w2_parts += [fr, -fi]

        def mult(zr, zi):
            return [jnp.concatenate([zr, zr], -1), jnp.concatenate([-zi, zi], -1)]

        sc_rows += mult(*_cexp(lc * dr, lc * di))
        seg_rows += mult(*_cexp((lc * seg_steps) * dr, (lc * seg_steps) * di))
    m = m + jnp.eye(lc * h, dtype=F32)[None] * jnp.tile(d.astype(F32), (1, lc))[:, None, :]
    w1 = jnp.concatenate(w1_parts, axis=-1).reshape(g, lc * h, 8 * p)
    w2 = jnp.concatenate(w2_parts, axis=1).reshape(g, 4 * p, lc * h)
    sc = jnp.stack(sc_rows + seg_rows, axis=1)
    return m.astype(BF16), w1.astype(BF16), w2.astype(BF16), sc.astype(F32)


def _s5_kernel(x_ref, m_ref, w1_ref, w2_ref, sc_ref, y_ref, loc_scr, sin_scr, *, n_seg):
    rows = x_ref.shape[1]
    steps = rows // SUBLANES
    sw = sc_ref.shape[2]
    x = x_ref[0]
    loc_scr[...] = _dot(x, w1_ref[0])

    def bc(i):
        return jnp.broadcast_to(sc_ref[0, i:i + 1, :], (SUBLANES, sw))

    a1f, a2f, a1b, a2b, p1f, p2f, p1b, p2b = [bc(i) for i in range(8)]

    def step_f(s, f, fs):
        r = pl.multiple_of(s * SUBLANES, SUBLANES)
        lf = loc_scr[pl.ds(r, SUBLANES), 0:sw]
        lfs = loc_scr[pl.ds(r, SUBLANES), sw:2 * sw]
        return a1f * f + a2f * fs + lf, a1f * fs - a2f * f + lfs

    def step_b(s, b, bs):
        r = pl.multiple_of((steps - 1 - s) * SUBLANES, SUBLANES)
        lb = loc_scr[pl.ds(r, SUBLANES), 2 * sw:3 * sw]
        lbs = loc_scr[pl.ds(r, SUBLANES), 3 * sw:4 * sw]
        return a1b * b + a2b * bs + lb, a1b * bs - a2b * b + lbs

    zero = jnp.zeros((SUBLANES, sw), F32)

    def pass1(s, carry):
        f, fs, b, bs = carry
        return step_f(s, f, fs) + step_b(s, b, bs)

    f_end, fs_end, b_end, bs_end = lax.fori_loop(0, steps, pass1, (zero, zero, zero, zero), unroll=4)

    seg = lax.broadcasted_iota(jnp.int32, (SUBLANES, sw), 0) % n_seg
    cf, cfs, cb, cbs = zero, zero, zero, zero
    for _ in range(n_seg - 1):
        ef = f_end + p1f * cf + p2f * cfs
        efs = fs_end + p1f * cfs - p2f * cf
        eb = b_end + p1b * cb + p2b * cbs
        ebs = bs_end + p1b * cbs - p2b * cb
        cf = jnp.where(seg >= 1, pltpu.roll(ef, 1, 0), 0.0)
        cfs = jnp.where(seg >= 1, pltpu.roll(efs, 1, 0), 0.0)
        cb = jnp.where(seg <= n_seg - 2, pltpu.roll(eb, SUBLANES - 1, 0), 0.0)
        cbs = jnp.where(seg <= n_seg - 2, pltpu.roll(ebs, SUBLANES - 1, 0), 0.0)

    def pass2(s, carry):
        f, fs, b, bs = carry
        rf = pl.multiple_of(s * SUBLANES, SUBLANES)
        rb = pl.multiple_of((steps - 1 - s) * SUBLANES, SUBLANES)
        sin_scr[pl.ds(rf, SUBLANES), 0:sw] = f
        sin_scr[pl.ds(rb, SUBLANES), sw:2 * sw] = b
        return step_f(s, f, fs) + step_b(s, b, bs)

    lax.fori_loop(0, steps, pass2, (cf, cfs, cb, cbs), unroll=4)

    y_ref[0] = _dot(x, m_ref[0]) + _dot(sin_scr[...].astype(BF16), w2_ref[0])


def _s5_scan(xg, m, w1, w2, sc, n_seg):
    g, rows, kw = xg.shape
    sw = sc.shape[2]
    blk = lambda a: pl.BlockSpec((1,) + a.shape[1:], lambda i: (i, 0, 0))
    return pl.pallas_call(
        functools.partial(_s5_kernel, n_seg=n_seg),
        grid=(g,),
        in_specs=[blk(xg), blk(m), blk(w1), blk(w2), blk(sc)],
        out_specs=pl.BlockSpec((1, rows, kw), lambda i: (i, 0, 0)),
        out_shape=jax.ShapeDtypeStruct((g, rows, kw), F32),
        scratch_shapes=[pltpu.VMEM((rows, 4 * sw), F32), pltpu.VMEM((rows, 2 * sw), F32)],
        compiler_params=pltpu.CompilerParams(dimension_semantics=("parallel",),
                                             vmem_limit_bytes=VMEM_LIMIT),
        name="s5_scan",
    )(xg, m, w1, w2, sc)


S5_NM = 8


def _block_transpose8(v, width):
    lane = lax.broadcasted_iota(jnp.int32, v[0].shape, 1)
    for d in (4, 2, 1):
        w = width * d
        hi = ((lane // w) % 2) == 1
        out = list(v)
        for i0 in range(8):
            if i0 & d:
                continue
            i1 = i0 + d
            out[i0] = jnp.where(hi, pltpu.roll(v[i1], w, 1), v[i0])
            out[i1] = jnp.where(hi, v[i1], pltpu.roll(v[i0], 8 * width - w, 1))
        v = out
    return v


def _tile_copies(hbm4, tile, buf, slot, sem, nm, to_hbm):
    copies = []
    for c in range(SUBLANES):
        for j in range(S5_LC):
            h = hbm4.at[c, pl.ds(tile * nm, nm), pl.ds(j, 1), :]
            v = buf.at[slot, j, :, pl.ds(c, 1), :]
            copies.append(pltpu.make_async_copy(v, h, sem.at[slot]) if to_hbm
                          else pltpu.make_async_copy(h, v, sem.at[slot]))
    return copies


def _s5_inproj_kernel(x4_ref, gmix_ref, w_ref, xg_ref, xs, sem, *, nm):
    i = pl.program_id(0)
    n = pl.num_programs(0)
    slot = i % 2
    dm = x4_ref.shape[3]

    @pl.when(i == 0)
    def _():
        for cp in _tile_copies(x4_ref, 0, xs, 0, sem, nm, False):
            cp.start()

    @pl.when(i + 1 < n)
    def _():
        for cp in _tile_copies(x4_ref, i + 1, xs, 1 - slot, sem, nm, False):
            cp.start()

    pltpu.make_async_copy(xs.at[slot], xs.at[slot], sem.at[slot]).wait()
    rows = nm * SUBLANES
    x = xs[slot].reshape(S5_LC * rows, dm)
    z = _dot(_rms(x, gmix_ref[...]).astype(BF16), w_ref[...])
    n_oct = z.shape[1] // LANES
    for q in range(n_oct):
        for a in range(S5_LC // 8):
            blocks = [z[(8 * a + j8) * rows:(8 * a + j8 + 1) * rows, q * LANES:(q + 1) * LANES] for j8 in range(8)]
            for g8, b in enumerate(_block_transpose8(blocks, S5_GROUP)):
                xg_ref[8 * q + g8, :, a * LANES:(a + 1) * LANES] = b.astype(BF16)


def _s5_inproj(x, gmix, w_s5_bf, n_seg):
    b, l, dm = x.shape
    steps = l // (S5_LC * n_seg)
    nm = S5_NM
    s5w = w_s5_bf.shape[1]
    g = s5w // S5_GROUP
    x4 = x.reshape(b * n_seg, steps, S5_LC, dm)
    return pl.pallas_call(
        functools.partial(_s5_inproj_kernel, nm=nm),
        grid=(steps // nm,),
        in_specs=[pl.BlockSpec(memory_space=pl.ANY),
                  pl.BlockSpec((1, dm), lambda i: (0, 0)),
                  pl.BlockSpec((dm, s5w), lambda i: (0, 0))],
        out_specs=pl.BlockSpec((g, nm * SUBLANES, S5_LC * S5_GROUP), lambda i: (0, i, 0)),
        out_shape=jax.ShapeDtypeStruct((g, steps * SUBLANES, S5_LC * S5_GROUP), BF16),
        scratch_shapes=[pltpu.VMEM((2, S5_LC, nm, SUBLANES, dm), F32), pltpu.SemaphoreType.DMA((2,))],
        compiler_params=pltpu.CompilerParams(dimension_semantics=("arbitrary",),
                                             vmem_limit_bytes=VMEM_LIMIT),
        name="s5_inproj",
    )(x4, gmix, w_s5_bf)


def _s5_to_tokens_kernel(yg_ref, ys4_ref, zs, sem, *, nm):
    i = pl.program_id(0)
    n = pl.num_programs(0)
    slot = i % 2
    rows = nm * SUBLANES

    def wait(s):
        pltpu.make_async_copy(zs.at[s], zs.at[s], sem.at[s]).wait()

    @pl.when(i >= 2)
    def _():
        wait(slot)

    n_oct = yg_ref.shape[0] // 8
    for q in range(n_oct):
        for a in range(S5_LC // 8):
            blocks = [yg_ref[8 * q + g8, :, a * LANES:(a + 1) * LANES] for g8 in range(8)]
            for j8, b in enumerate(_block_transpose8(blocks, S5_GROUP)):
                zs[slot, 8 * a + j8, :, :, q * LANES:(q + 1) * LANES] = b.reshape(nm, SUBLANES, LANES)
    for cp in _tile_copies(ys4_ref, i, zs, slot, sem, nm, True):
        cp.start()

    @pl.when(i == n - 1)
    def _():
        wait(1 - slot)
        wait(slot)


def _s5_to_tokens(yg, b, l, n_seg):
    g, rows_total, kw = yg.shape
    steps = rows_total // SUBLANES
    nm = S5_NM
    s5w = g * S5_GROUP
    assert steps // nm >= 2
    ys4 = pl.pallas_call(
        functools.partial(_s5_to_tokens_kernel, nm=nm),
        grid=(steps // nm,),
        in_specs=[pl.BlockSpec((g, nm * SUBLANES, kw), lambda i: (0, i, 0))],
        out_specs=pl.BlockSpec(memory_space=pl.ANY),
        out_shape=jax.ShapeDtypeStruct((b * n_seg, steps, S5_LC, s5w), F32),
        scratch_shapes=[pltpu.VMEM((2, S5_LC, nm, SUBLANES, s5w), F32), pltpu.SemaphoreType.DMA((2,))],
        compiler_params=pltpu.CompilerParams(dimension_semantics=("arbitrary",),
                                             vmem_limit_bytes=VMEM_LIMIT),
        name="s5_to_tokens",
    )(yg)
    return ys4.reshape(b * l, s5w)


R_E1, R_E2, R_W1, R_W2, R_RANK1, R_RANK2 = range(6)


def _mix_route_kernel(ys_ref, mgm_ref, x_ref, gluw_ref, glub_ref, gs5_ref,
                      wout_ref, gffn_ref, rwh_ref, rwl_ref, rb_ref, tri_ref,
                      x2_ref, t_ref, route_ref, cnt_ref):
    gw = mgm_ref.shape[1]
    g = _gelu(ys_ref[...])
    z = g * jax.nn.sigmoid(_dot(g.astype(BF16), gluw_ref[...]) + glub_ref[...])
    ms5 = _rms(z, gs5_ref[...]).astype(BF16)
    mix = _dot(mgm_ref[...], wout_ref[:gw, :]) + _dot(ms5, wout_ref[gw:, :])
    x2 = x_ref[...] + mix
    x2_ref[...] = x2
    t = _rms(x2, gffn_ref[...])
    t_hi = t.astype(BF16)
    t_ref[...] = t_hi
    t_lo = (t - t_hi.astype(F32)).astype(BF16)
    logits = (_dot(t_hi, rwh_ref[...]) + _dot(t_hi, rwl_ref[...]) + _dot(t_lo, rwh_ref[...])
              + rb_ref[...])
    lane = lax.broadcasted_iota(jnp.int32, logits.shape, 1).astype(F32)
    neg = jnp.float32(-jnp.inf)

    def first_max(mask):
        vals = jnp.where(mask, logits, neg)
        mx = jnp.max(vals, axis=-1, keepdims=True)
        idx = jnp.min(jnp.where(mask & (vals == mx), lane, float(LANES)), axis=-1, keepdims=True)
        return mx, idx

    coarse = lane < N_COARSE
    m1, grp = first_max(coarse)
    p_grp = 1.0 / jnp.sum(jnp.where(coarse, jnp.exp(logits - m1), 0.0), axis=-1, keepdims=True)
    lo = N_COARSE + grp * N_FINE
    fine = (lane >= lo) & (lane < lo + N_FINE)
    v1, i1 = first_max(fine)
    v2, i2 = first_max(fine & (lane != i1))
    e21 = jnp.exp(v2 - v1)
    w1 = p_grp / (1.0 + e21)
    w2 = p_grp * e21 / (1.0 + e21)
    e1 = i1 - N_COARSE
    e2 = i2 - N_COARSE
    hit1 = lane == e1
    hit2 = lane == e2
    onehot = jnp.where(hit1 | hit2, 1.0, 0.0)
    before = _dot(tri_ref[...], onehot.astype(BF16))
    rank1 = jnp.sum(jnp.where(hit1, before, 0.0), axis=-1, keepdims=True)
    rank2 = jnp.sum(jnp.where(hit2, before, 0.0), axis=-1, keepdims=True)
    tm = onehot.shape[0]
    cnt_ref[0] = before[tm - 1:tm, :] + onehot[tm - 1:tm, :]
    rec = jnp.zeros_like(logits)
    for slot, val in ((R_E1, e1), (R_E2, e2), (R_W1, w1), (R_W2, w2),
                      (R_RANK1, rank1), (R_RANK2, rank2)):
        rec = jnp.where(lane == slot, val, rec)
    route_ref[...] = rec


def _mix_route(ys, mgm, x2d, gluw_bf, glub, gs5, wout_bf, gffn, rwh, rwl, rb):
    t, dm = x2d.shape
    gw = mgm.shape[1]
    s5w = ys.shape[1]
    tm = TM_MIX
    tri = jnp.tril(jnp.ones((tm, tm), F32), -1).astype(BF16)
    const = lambda *shape: pl.BlockSpec(shape, lambda i: (0,) * len(shape))
    tile = lambda w: pl.BlockSpec((tm, w), lambda i: (i, 0))
    return pl.pallas_call(
        _mix_route_kernel,
        grid=(t // tm,),
        in_specs=[tile(s5w), tile(gw), tile(dm),
                  const(s5w, s5w), const(1, s5w), const(1, s5w),
                  const(gw + s5w, dm), const(1, dm), const(dm, LANES), const(dm, LANES), const(1, LANES),
                  const(tm, tm)],
        out_specs=[tile(dm), tile(dm), tile(LANES), pl.BlockSpec((1, 1, LANES), lambda i: (i, 0, 0))],
        out_shape=[jax.ShapeDtypeStruct((t, dm), F32),
                   jax.ShapeDtypeStruct((t, dm), BF16),
                   jax.ShapeDtypeStruct((t, LANES), F32),
                   jax.ShapeDtypeStruct((t // tm, 1, LANES), F32)],
        compiler_params=pltpu.CompilerParams(dimension_semantics=("parallel",),
                                             vmem_limit_bytes=VMEM_LIMIT),
        name="mix_route",
    )(ys, mgm, x2d, gluw_bf, glub, gs5, wout_bf, gffn, rwh, rwl, rb, tri)


def _local_rows(tm):
    worst = 2 * tm + N_EXPERTS * (SEG_ALIGN - 1)
    return -(-worst // LANES) * LANES


def _segment_plan(cnt, t, tm_expert):
    c = cnt[:, 0, :N_EXPERTS].astype(jnp.int32)
    n_tok_tiles = c.shape[0]
    al = (c + SEG_ALIGN - 1) // SEG_ALIGN * SEG_ALIGN
    lbase = jnp.cumsum(al, axis=1) - al
    tot = jnp.sum(al, axis=0)
    tot_pad = (tot + tm_expert - 1) // tm_expert * tm_expert
    gbase = jnp.cumsum(tot_pad) - tot_pad
    gpos = gbase[None, :] + jnp.cumsum(al, axis=0) - al
    n_tiles_max = -(-(2 * t + n_tok_tiles * N_EXPERTS * (SEG_ALIGN - 1)) // tm_expert) + N_EXPERTS
    tile_end = jnp.cumsum(tot_pad // tm_expert)
    n_tiles = tile_end[-1:].astype(jnp.int32)
    tile_idx = jnp.arange(n_tiles_max, dtype=jnp.int32)
    tile_expert = jnp.sum((tile_idx[:, None] >= tile_end[None, :]).astype(jnp.int32), axis=1)
    last = jnp.sum((n_tiles - 1 >= tile_end).astype(jnp.int32))
    tile_expert = jnp.where(tile_idx < n_tiles, tile_expert, last).astype(jnp.int32)
    lbase_f = jnp.pad(lbase.astype(F32), ((0, 0), (0, LANES - N_EXPERTS)))[:, None, :]
    flat = lambda a: a.reshape(-1).astype(jnp.int32)
    plan = dict(lb=flat(lbase), nch=flat(al // SEG_ALIGN), gpos=flat(gpos),
                tail_pos=flat(gbase + tot), tail_n=flat((tot_pad - tot) // SEG_ALIGN))
    return plan, lbase_f, tile_expert, n_tiles, n_tiles_max * tm_expert


def _local_positions(route, lbase):
    lane = lax.broadcasted_iota(jnp.int32, route.shape, 1).astype(F32)
    out = []
    for e_lane, r_lane in ((R_E1, R_RANK1), (R_E2, R_RANK2)):
        e = route[:, e_lane:e_lane + 1]
        base = jnp.sum(jnp.where(lane == e, lbase, 0.0), axis=-1, keepdims=True)
        out.append(base + route[:, r_lane:r_lane + 1])
    return out


def _segment_copies(i, lb_ref, nch_ref, gp_ref, local, glob, sem, to_global):
    def seg(e, total):
        k = i * N_EXPERTS + e
        lb, n, gp = lb_ref[k], nch_ref[k], gp_ref[k]

        def chunk(c, carry):
            lo = local.at[pl.ds(pl.multiple_of(lb + c * SEG_ALIGN, SEG_ALIGN), SEG_ALIGN)]
            gl = glob.at[pl.ds(pl.multiple_of(gp + c * SEG_ALIGN, SEG_ALIGN), SEG_ALIGN)]
            (pltpu.make_async_copy(lo, gl, sem) if to_global else pltpu.make_async_copy(gl, lo, sem)).start()
            return carry

        lax.fori_loop(0, n, chunk, 0)
        return total + n

    return lax.fori_loop(0, N_EXPERTS, seg, 0)


def _wait_chunks(n, local, glob, sem):
    def one(c, carry):
        pltpu.make_async_copy(local.at[pl.ds(0, SEG_ALIGN)], glob.at[pl.ds(0, SEG_ALIGN)], sem).wait()
        return carry

    lax.fori_loop(0, n, one, 0)


def _sort_rows_kernel(lb_ref, nch_ref, gp_ref, tpos_ref, tn_ref, nt_ref, t_ref, route_ref, lbase_ref,
                      xs_ref, local_scr, zero_scr, cnt_scr, sem, zsem):
    i = pl.program_id(0)
    n = pl.num_programs(0)
    slot = i % 2
    tm = t_ref.shape[0]
    s_rows = local_scr.shape[1]
    local = local_scr.at[slot]

    @pl.when(i >= 2)
    def _():
        _wait_chunks(cnt_scr[slot], local, xs_ref, sem.at[slot])

    lp1, lp2 = _local_positions(route_ref[...], lbase_ref[0])
    lane = lax.broadcasted_iota(jnp.int32, (tm, LANES), 1)
    lp_rows = jnp.where(lane == 0, lp1, jnp.where(lane == 1, lp2, -1.0)).T
    row = lax.broadcasted_iota(jnp.int32, (s_rows, tm), 0).astype(F32)
    onehot = jnp.where((row == lp_rows[0:1, :]) | (row == lp_rows[1:2, :]), 1.0, 0.0).astype(BF16)
    local_scr[slot] = _dot(onehot, t_ref[...]).astype(BF16)
    cnt_scr[slot] = _segment_copies(i, lb_ref, nch_ref, gp_ref, local, xs_ref, sem.at[slot], True)

    @pl.when(i == n - 1)
    def _():
        @pl.when(n >= 2)
        def _():
            _wait_chunks(cnt_scr[1 - slot], local_scr.at[1 - slot], xs_ref, sem.at[1 - slot])

        _wait_chunks(cnt_scr[slot], local, xs_ref, sem.at[slot])
        zero_scr[...] = jnp.zeros_like(zero_scr)
        te = zero_scr.shape[0]
        zero_chunk = zero_scr.at[pl.ds(0, SEG_ALIGN)]

        def tail(e, total):
            def chunk(c, carry):
                dst = xs_ref.at[pl.ds(pl.multiple_of(tpos_ref[e] + c * SEG_ALIGN, SEG_ALIGN), SEG_ALIGN)]
                pltpu.make_async_copy(zero_chunk, dst, zsem).start()
                return carry

            lax.fori_loop(0, tn_ref[e], chunk, 0)
            return total + tn_ref[e]

        _wait_chunks(lax.fori_loop(0, N_EXPERTS, tail, 0), zero_scr, xs_ref, zsem)

        def unused_tile(j, carry):
            pltpu.make_async_copy(zero_scr, xs_ref.at[pl.ds(pl.multiple_of(j * te, te), te)], zsem).start()
            return carry

        def unused_wait(j, carry):
            pltpu.make_async_copy(zero_scr, xs_ref.at[pl.ds(0, te)], zsem).wait()
            return carry

        lax.fori_loop(nt_ref[0], xs_ref.shape[0] // te, unused_tile, 0)
        lax.fori_loop(nt_ref[0], xs_ref.shape[0] // te, unused_wait, 0)


def _plan_specs(plan):
    keys = ('lb', 'nch', 'gpos', 'tail_pos', 'tail_n')
    return [plan[k] for k in keys]


def _sort_rows(plan, n_tiles, t_bf, route, lbase_f, n_sorted):
    t, dm = t_bf.shape
    tm = TM_MIX
    s_rows = _local_rows(tm)
    im = lambda i, *_: (i, 0)
    return pl.pallas_call(
        _sort_rows_kernel,
        grid_spec=pltpu.PrefetchScalarGridSpec(
            num_scalar_prefetch=6,
            grid=(t // tm,),
            in_specs=[pl.BlockSpec((tm, dm), im), pl.BlockSpec((tm, LANES), im),
                      pl.BlockSpec((1, 1, LANES), lambda i, *_: (i, 0, 0))],
            out_specs=pl.BlockSpec(memory_space=pl.ANY),
            scratch_shapes=[pltpu.VMEM((2, s_rows, dm), BF16), pltpu.VMEM((TM_EXPERT, dm), BF16),
                            pltpu.SMEM((2,), jnp.int32), pltpu.SemaphoreType.DMA((2,)),
                            pltpu.SemaphoreType.DMA(())],
        ),
        out_shape=jax.ShapeDtypeStruct((n_sorted, dm), BF16),
        compiler_params=pltpu.CompilerParams(dimension_semantics=("arbitrary",),
                                             vmem_limit_bytes=VMEM_LIMIT),
        name="sort_rows",
    )(*_plan_specs(plan), n_tiles, t_bf, route, lbase_f)


def _experts_kernel(te_ref, nt_ref, xs_ref, wg_ref, wu_ref, wd_ref, ys_ref):
    i = pl.program_id(0)

    @pl.when(i < nt_ref[0])
    def _():
        x = xs_ref[...]
        gate = _dot(x, wg_ref[0].astype(BF16))
        up = _dot(x, wu_ref[0].astype(BF16))
        hidden = (jax.nn.silu(gate) * up).astype(BF16)
        ys_ref[...] = _dot(hidden, wd_ref[0].astype(BF16)).astype(BF16)

    @pl.when(i >= nt_ref[0])
    def _():
        ys_ref[...] = jnp.zeros_like(ys_ref)


def _experts(tile_expert, n_tiles, x_sorted, w_gate, w_up, w_down):
    n_sorted, dm = x_sorted.shape
    de = w_gate.shape[2]
    tm = TM_EXPERT
    return pl.pallas_call(
        _experts_kernel,
        grid_spec=pltpu.PrefetchScalarGridSpec(
            num_scalar_prefetch=2,
            grid=(n_sorted // tm,),
            in_specs=[pl.BlockSpec((tm, dm), lambda i, te, nt: (jnp.minimum(i, nt[0] - 1), 0)),
                      pl.BlockSpec((1, dm, de), lambda i, te, nt: (te[i], 0, 0)),
                      pl.BlockSpec((1, dm, de), lambda i, te, nt: (te[i], 0, 0)),
                      pl.BlockSpec((1, de, dm), lambda i, te, nt: (te[i], 0, 0))],
            out_specs=pl.BlockSpec((tm, dm), lambda i, te, nt: (i, 0)),
        ),
        out_shape=jax.ShapeDtypeStruct((n_sorted, dm), BF16),
        compiler_params=pltpu.CompilerParams(dimension_semantics=("arbitrary",),
                                             vmem_limit_bytes=VMEM_LIMIT),
        name="experts",
    )(tile_expert, n_tiles, x_sorted, w_gate, w_up, w_down)


def _combine_kernel(lb_ref, nch_ref, gp_ref, tpos_ref, tn_ref, x2_ref, route_ref, lbase_ref, gfin_ref, ys_ref,
                    o_ref, local_scr, cnt_scr, sem):
    del tpos_ref, tn_ref
    i = pl.program_id(0)
    n = pl.num_programs(0)
    slot = i % 2
    tm = x2_ref.shape[0]
    s_rows = local_scr.shape[1]

    def fetch(tile, s):
        last = tile * N_EXPERTS + N_EXPERTS - 1
        used = lb_ref[last] + nch_ref[last] * SEG_ALIGN

        def clear(r, carry):
            local_scr[s, pl.ds(pl.multiple_of(r * SEG_ALIGN, SEG_ALIGN), SEG_ALIGN), :] = jnp.zeros(
                (SEG_ALIGN, local_scr.shape[2]), BF16)
            return carry

        lax.fori_loop(used // SEG_ALIGN, s_rows // SEG_ALIGN, clear, 0)
        cnt_scr[s] = _segment_copies(tile, lb_ref, nch_ref, gp_ref, local_scr.at[s], ys_ref, sem.at[s], False)

    @pl.when(i == 0)
    def _():
        fetch(0, 0)

    @pl.when(i + 1 < n)
    def _():
        fetch(i + 1, 1 - slot)

    route = route_ref[...]
    lp1, lp2 = _local_positions(route, lbase_ref[0])
    col = lax.broadcasted_iota(jnp.int32, (tm, s_rows), 1).astype(F32)
    pick = (jnp.where(col == lp1, route[:, R_W1:R_W1 + 1], 0.0)
            + jnp.where(col == lp2, route[:, R_W2:R_W2 + 1], 0.0)).astype(BF16)
    _wait_chunks(cnt_scr[slot], local_scr.at[slot], ys_ref, sem.at[slot])
    o_ref[...] = _rms(x2_ref[...] + _dot(pick, local_scr[slot]), gfin_ref[...])


def _combine(plan, x2, route, lbase_f, gfin, y_sorted):
    t, dm = x2.shape
    tm = TM_MIX
    s_rows = _local_rows(tm)
    im = lambda i, *_: (i, 0)
    return pl.pallas_call(
        _combine_kernel,
        grid_spec=pltpu.PrefetchScalarGridSpec(
            num_scalar_prefetch=5,
            grid=(t // tm,),
            in_specs=[pl.BlockSpec((tm, dm), im), pl.BlockSpec((tm, LANES), im),
                      pl.BlockSpec((1, 1, LANES), lambda i, *_: (i, 0, 0)),
                      pl.BlockSpec((1, dm), lambda i, *_: (0, 0)),
                      pl.BlockSpec(memory_space=pl.ANY)],
            out_specs=pl.BlockSpec((tm, dm), im),
            scratch_shapes=[pltpu.VMEM((2, s_rows, dm), BF16), pltpu.SMEM((2,), jnp.int32),
                            pltpu.SemaphoreType.DMA((2,))],
        ),
        out_shape=jax.ShapeDtypeStruct((t, dm), F32),
        compiler_params=pltpu.CompilerParams(dimension_semantics=("arbitrary",),
                                             vmem_limit_bytes=VMEM_LIMIT),
        name="combine_norm",
    )(*_plan_specs(plan), x2, route, lbase_f, gfin, y_sorted)


def _layer(x, p, s5_ops, gfin):
    b, l, dm = x.shape
    x2d = x.reshape(b * l, dm)
    mgm = _inproj_gmlp(x2d, p['gmix'], p['win_gm'], p['lng'], p['lnb'], p['ws'], p['bs'], p['gout_gm'])
    n_seg = SUBLANES // b
    m, w1, w2, sc = s5_ops[(l // (S5_LC * n_seg))]
    xg = _s5_inproj(x, p['gmix'], p['win_s5'], n_seg)
    yg = _s5_scan(xg, m, w1, w2, sc, n_seg)
    ys = _s5_to_tokens(yg, b, l, n_seg)
    x2, t_bf, route, counts = _mix_route(ys, mgm, x2d, p['gluw'], p['glub'], p['gout_s5'],
                                         p['wout'], p['gffn'], p['rwh'], p['rwl'], p['rb'])
    plan, lbase_f, tile_expert, n_tiles, n_sorted = _segment_plan(counts, b * l, TM_EXPERT)
    x_sorted = _sort_rows(plan, n_tiles, t_bf, route, lbase_f, n_sorted)
    y_sorted = _experts(tile_expert, n_tiles, x_sorted, p['w_gate'], p['w_up'], p['w_down'])
    out = _combine(plan, x2, route, lbase_f, gfin, y_sorted)
    return out.reshape(b, l, dm)


def kernel(x_prompt, x_sample, norm_mix_g, w_in, gm_ln_g, gm_ln_b, gm_ws, gm_bs, s5_lam_re_fwd, s5_lam_im_fwd, s5_log_step_fwd, s5_b_re_fwd, s5_b_im_fwd, s5_c_re_fwd, s5_c_im_fwd, s5_lam_re_bwd, s5_lam_im_bwd, s5_log_step_bwd, s5_b_re_bwd, s5_b_im_bwd, s5_c_re_bwd, s5_c_im_bwd, s5_d, s5_glu_w, s5_glu_b, out_norm_gm, out_norm_s5, w_out, norm_ffn_g, r1_w, r1_b, r2_w, r2_b, e_w_gate, e_w_up, e_w_down, norm_final_g):
    depth = w_in.shape[0]
    gfin = norm_final_g.reshape(1, -1).astype(F32)
    xs = [x_prompt, x_sample]
    for li in range(depth):
        row = lambda a: a[li].reshape(1, -1).astype(F32)
        dm = w_in.shape[1]
        gw = gm_ln_g.shape[1]
        hd_dim = gw // GM_HEADS
        rw = jnp.concatenate([r1_w[li], r2_w[li].transpose(1, 0, 2).reshape(dm, N_EXPERTS)], axis=1).astype(F32)
        rw = jnp.pad(rw, ((0, 0), (0, LANES - rw.shape[1])))
        rwh = rw.astype(BF16)
        rwl = (rw - rwh.astype(F32)).astype(BF16)
        rb = jnp.concatenate([r1_b[li], r2_b[li].reshape(-1)]).astype(F32)
        rb = jnp.pad(rb, (0, LANES - rb.shape[0])).reshape(1, LANES)
        p = dict(
            gmix=row(norm_mix_g), win_gm=w_in[li][:, :2 * gw].astype(BF16), win_s5=w_in[li][:, 2 * gw:].astype(BF16),
            lng=row(gm_ln_g), lnb=row(gm_ln_b),
            ws=gm_ws[li].astype(BF16),
            bs=jnp.broadcast_to(gm_bs[li].astype(F32)[:, :, None], (GM_HEADS, CHUNK, hd_dim)),
            gout_gm=row(out_norm_gm), gluw=s5_glu_w[li].astype(BF16), glub=row(s5_glu_b),
            gout_s5=row(out_norm_s5), wout=w_out[li].astype(BF16), gffn=row(norm_ffn_g),
            rwh=rwh, rwl=rwl, rb=rb,
            w_gate=e_w_gate[li], w_up=e_w_up[li], w_down=e_w_down[li],
        )
        fwd = (s5_lam_re_fwd[li], s5_lam_im_fwd[li], s5_log_step_fwd[li], s5_b_re_fwd[li], s5_b_im_fwd[li],
               s5_c_re_fwd[li], s5_c_im_fwd[li])
        bwd = (s5_lam_re_bwd[li], s5_lam_im_bwd[li], s5_log_step_bwd[li], s5_b_re_bwd[li], s5_b_im_bwd[li],
               s5_c_re_bwd[li], s5_c_im_bwd[li])
        s5_ops = {}
        for x in xs:
            seg_steps = x.shape[1] // (S5_LC * (SUBLANES // x.shape[0]))
            if seg_steps not in s5_ops:
                s5_ops[seg_steps] = _s5_operator(fwd, bwd, s5_d[li], S5_LC, seg_steps)
        last = li == depth - 1
        assert last, "depth > 1 needs an un-normalised layer output"
        xs = [_layer(x, p, s5_ops, gfin) for x in xs]
    return tuple(xs)
```

```python
import functools
import math

import jax
import jax.numpy as jnp
from jax import lax
from jax.experimental import pallas as pl
from jax.experimental.pallas import tpu as pltpu

F32 = jnp.float32
BF16 = jnp.bfloat16

EPS = 1e-6
LAMBDA_RE_MAX = -1e-4
GM_HEADS = 4
CHUNK = 128
S5_GROUP = 16
S5_STATE = 64
N_COARSE = 4
N_FINE = 8
N_EXPERTS = N_COARSE * N_FINE

LANES = 128
SUBLANES = 8
S5_LC = 16
VMEM_LIMIT = 56 * 1024 * 1024

TM_PROJ = 512
TM_MIX = 512
MIX_TILES = 2
KB = 256
TM_EXPERT = 512
SEG_ALIGN = 16


def _gelu(x):
    c = math.sqrt(2.0 / math.pi)
    return x * (0.5 * (1.0 + jnp.tanh(c * (x + 0.044715 * (x * x * x)))))


def _rms(x, g):
    ms = jnp.mean(x * x, axis=-1, keepdims=True)
    return x * lax.rsqrt(ms + EPS) * g


def _dot(a, b):
    return jnp.dot(a, b, preferred_element_type=F32)


def _inproj_gmlp_kernel(x_ref, gmix_ref, win_ref, lng_ref, lnb_ref, ws_ref, bs_ref, gout_ref,
                        mgm_ref, us5_ref, y_scr):
    tm = x_ref.shape[0]
    gw = mgm_ref.shape[1]
    hd_dim = gw // GM_HEADS
    n_chunks = tm // CHUNK
    h = _rms(x_ref[...], gmix_ref[...]).astype(BF16)
    proj = _dot(h, win_ref[...])
    us5_ref[...] = proj[:, 2 * gw:]
    u = _gelu(proj[:, :gw])
    v = _gelu(proj[:, gw:2 * gw])
    for hd in range(GM_HEADS):
        lo = hd * hd_dim
        vh = v[:, lo:lo + hd_dim]
        mu = jnp.mean(vh, axis=-1, keepdims=True)
        xc = vh - mu
        var = jnp.mean(xc * xc, axis=-1, keepdims=True)
        vn = (xc * lax.rsqrt(var + EPS) * lng_ref[:, lo:lo + hd_dim]
              + lnb_ref[:, lo:lo + hd_dim]).astype(BF16)
        rhs = jnp.concatenate([vn[c * CHUNK:(c + 1) * CHUNK] for c in range(n_chunks)], axis=1)
        s = _dot(ws_ref[hd], rhs)
        for c in range(n_chunks):
            sc = s[:, c * hd_dim:(c + 1) * hd_dim] + bs_ref[hd]
            y_scr[c * CHUNK:(c + 1) * CHUNK, lo:lo + hd_dim] = u[c * CHUNK:(c + 1) * CHUNK, lo:lo + hd_dim] * sc
    mgm_ref[...] = _rms(y_scr[...], gout_ref[...]).astype(BF16)


def _inproj_gmlp(x2d, gmix, win_bf, lng, lnb, ws_bf, bs_b, gout):
    t, d = x2d.shape
    d_in = win_bf.shape[1]
    gw = lng.shape[1]
    s5w = d_in - 2 * gw
    tm = TM_PROJ
    const = lambda *shape: pl.BlockSpec(shape, lambda i: (0,) * len(shape))
    return pl.pallas_call(
        _inproj_gmlp_kernel,
        grid=(t // tm,),
        in_specs=[
            pl.BlockSpec((tm, d), lambda i: (i, 0)),
            const(1, d), const(d, d_in), const(1, gw), const(1, gw),
            const(GM_HEADS, CHUNK, CHUNK), const(GM_HEADS, CHUNK, gw // GM_HEADS), const(1, gw),
        ],
        out_specs=[pl.BlockSpec((tm, gw), lambda i: (i, 0)),
                   pl.BlockSpec((tm, s5w), lambda i: (i, 0))],
        out_shape=[jax.ShapeDtypeStruct((t, gw), BF16),
                   jax.ShapeDtypeStruct((t, s5w), F32)],
        scratch_shapes=[pltpu.VMEM((tm, gw), F32)],
        compiler_params=pltpu.CompilerParams(dimension_semantics=("parallel",),
                                             vmem_limit_bytes=VMEM_LIMIT),
        name="inproj_gmlp",
    )(x2d, gmix, win_bf, lng, lnb, ws_bf, bs_b, gout)


def _s5_consts(lam_re, lam_im, log_step, b_re, b_im, c_re, c_im, lc):
    lr = jnp.minimum(lam_re.astype(F32), LAMBDA_RE_MAX)
    li = lam_im.astype(F32)
    step = jnp.exp(log_step.astype(F32))[:, None]
    dr, di = lr * step, li * step
    ar, ai = _cexp(dr, di)
    nr, ni = ar - 1.0, ai
    den = lr * lr + li * li
    qr, qi = (nr * lr + ni * li) / den, (ni * lr - nr * li) / den
    br, bi = b_re.astype(F32), b_im.astype(F32)
    bbr = qr[..., None] * br - qi[..., None] * bi
    bbi = qr[..., None] * bi + qi[..., None] * br
    k = jnp.arange(lc + 1, dtype=F32)[:, None, None]
    pwr, pwi = _cexp(k * dr[None], k * di[None])
    return (dr, di), (pwr, pwi), (bbr, bbi), (c_re.astype(F32), c_im.astype(F32))


def _cexp(zr, zi):
    m = jnp.exp(zr)
    return m * jnp.cos(zi), m * jnp.sin(zi)


def _s5_operator(fwd, bwd, lc, seg_steps):
    consts = [_s5_consts(*fwd, lc), _s5_consts(*bwd, lc)]
    g, p, h = consts[0][2][0].shape
    lags, w1_parts, w2_parts, sc_rows, seg_rows = [], [], [], [], []
    for direction, (ld, pw, bb, c) in enumerate(consts):
        (dr, di), (pwr, pwi), (bbr, bbi), (cr, ci) = ld, pw, bb, c
        crt, cit = cr.transpose(0, 2, 1), ci.transpose(0, 2, 1)
        pwrt, pwit = pwr.transpose(1, 2, 0), pwi.transpose(1, 2, 0)
        cpr = crt[:, :, None, :] * pwrt[:, :, :, None] - cit[:, :, None, :] * pwit[:, :, :, None]
        cpi = crt[:, :, None, :] * pwit[:, :, :, None] + cit[:, :, None, :] * pwrt[:, :, :, None]
        kk = (jnp.einsum('gpi,gpkh->gikh', bbr, cpr[:, :, :lc], precision=lax.Precision.HIGHEST)
              - jnp.einsum('gpi,gpkh->gikh', bbi, cpi[:, :, :lc], precision=lax.Precision.HIGHEST))
        lags.append((kk if direction == 0 else jnp.flip(kk, 2)).reshape(g, h, lc * h))
        er, ei = pwrt[:, :, :lc].transpose(0, 2, 1), pwit[:, :, :lc].transpose(0, 2, 1)
        if direction == 0:
            er, ei = jnp.flip(er, 1), jnp.flip(ei, 1)
        bbrt, bbit = bbr.transpose(0, 2, 1), bbi.transpose(0, 2, 1)
        wr = er[:, :, None, :] * bbrt[:, None] - ei[:, :, None, :] * bbit[:, None]
        wi = er[:, :, None, :] * bbit[:, None] + ei[:, :, None, :] * bbrt[:, None]
        w1_parts += [wr, wi, wi, wr]
        fr, fi = cpr[:, :, 1:lc + 1], cpi[:, :, 1:lc + 1]
        if direction == 1:
            fr, fi = jnp.flip(fr, 2), jnp.flip(fi, 2)
        w2_parts += [fr, -fi]

        def mult(zr, zi):
            return [jnp.concatenate([zr, zr], -1), jnp.concatenate([-zi, zi], -1)]

        sc_rows += mult(*_cexp(lc * dr, lc * di))
        seg_rows += mult(*_cexp((lc * seg_steps) * dr, (lc * seg_steps) * di))
    w1 = jnp.concatenate(w1_parts, axis=-1).reshape(g, lc * h, 8 * p)
    w2 = jnp.concatenate(w2_parts, axis=1).reshape(g, 4 * p, lc * h)
    sc = jnp.stack(sc_rows + seg_rows, axis=1)
    return lags[0], lags[1], w1.astype(BF16), w2.astype(BF16), sc.astype(F32)


def _s5_kernel(x_ref, kf_ref, kb_ref, w1_ref, w2_ref, sc_ref, y_ref, loc_scr, sin_scr, m_scr, *, n_seg):
    rows = x_ref.shape[1]
    steps = rows // SUBLANES
    sw = sc_ref.shape[2]
    x = x_ref[0]

    kf, kb = kf_ref[0], kb_ref[0]
    hch, kw = kf.shape
    lc = kw // hch
    lane = lax.broadcasted_iota(jnp.int32, kf.shape, 1)
    for s in range(lc):
        f = kf if s == 0 else jnp.where(lane >= s * hch, pltpu.roll(kf, s * hch, 1), 0.0)
        left = (lc - 1 - s) * hch
        b = kb if left == 0 else pltpu.roll(kb, kw - left, 1)
        m_scr[s * hch:(s + 1) * hch, :] = (f + jnp.where(lane < (s + 1) * hch, b, 0.0)).astype(BF16)
    loc_scr[...] = _dot(x, w1_ref[0])

    def bc(i):
        return jnp.broadcast_to(sc_ref[0, i:i + 1, :], (SUBLANES, sw))

    a1f, a2f, a1b, a2b, p1f, p2f, p1b, p2b = [bc(i) for i in range(8)]

    def step_f(s, f, fs):
        r = pl.multiple_of(s * SUBLANES, SUBLANES)
        lf = loc_scr[pl.ds(r, SUBLANES), 0:sw]
        lfs = loc_scr[pl.ds(r, SUBLANES), sw:2 * sw]
        return a1f * f + a2f * fs + lf, a1f * fs - a2f * f + lfs

    def step_b(s, b, bs):
        r = pl.multiple_of((steps - 1 - s) * SUBLANES, SUBLANES)
        lb = loc_scr[pl.ds(r, SUBLANES), 2 * sw:3 * sw]
        lbs = loc_scr[pl.ds(r, SUBLANES), 3 * sw:4 * sw]
        return a1b * b + a2b * bs + lb, a1b * bs - a2b * b + lbs

    zero = jnp.zeros((SUBLANES, sw), F32)

    def pass1(s, carry):
        f, fs, b, bs = carry
        return step_f(s, f, fs) + step_b(s, b, bs)

    f_end, fs_end, b_end, bs_end = lax.fori_loop(0, steps, pass1, (zero, zero, zero, zero), unroll=4)

    seg = lax.broadcasted_iota(jnp.int32, (SUBLANES, sw), 0) % n_seg
    cf, cfs, cb, cbs = zero, zero, zero, zero
    for _ in range(n_seg - 1):
        ef = f_end + p1f * cf + p2f * cfs
        efs = fs_end + p1f * cfs - p2f * cf
        eb = b_end + p1b * cb + p2b * cbs
        ebs = bs_end + p1b * cbs - p2b * cb
        cf = jnp.where(seg >= 1, pltpu.roll(ef, 1, 0), 0.0)
        cfs = jnp.where(seg >= 1, pltpu.roll(efs, 1, 0), 0.0)
        cb = jnp.where(seg <= n_seg - 2, pltpu.roll(eb, SUBLANES - 1, 0), 0.0)
        cbs = jnp.where(seg <= n_seg - 2, pltpu.roll(ebs, SUBLANES - 1, 0), 0.0)

    def pass2(s, carry):
        f, fs, b, bs = carry
        rf = pl.multiple_of(s * SUBLANES, SUBLANES)
        rb = pl.multiple_of((steps - 1 - s) * SUBLANES, SUBLANES)
        sin_scr[pl.ds(rf, SUBLANES), 0:sw] = f
        sin_scr[pl.ds(rb, SUBLANES), sw:2 * sw] = b
        return step_f(s, f, fs) + step_b(s, b, bs)

    lax.fori_loop(0, steps, pass2, (cf, cfs, cb, cbs), unroll=4)

    y_ref[0] = _dot(x, m_scr[...]) + _dot(sin_scr[...].astype(BF16), w2_ref[0])


def _s5_scan(xg, kf, kb, w1, w2, sc, n_seg):
    g, rows, kw = xg.shape
    sw = sc.shape[2]
    blk = lambda a: pl.BlockSpec((1,) + a.shape[1:], lambda i: (i, 0, 0))
    return pl.pallas_call(
        functools.partial(_s5_kernel, n_seg=n_seg),
        grid=(g,),
        in_specs=[blk(xg), blk(kf), blk(kb), blk(w1), blk(w2), blk(sc)],
        out_specs=pl.BlockSpec((1, rows, kw), lambda i: (i, 0, 0)),
        out_shape=jax.ShapeDtypeStruct((g, rows, kw), F32),
        scratch_shapes=[pltpu.VMEM((rows, 4 * sw), F32), pltpu.VMEM((rows, 2 * sw), F32),
                        pltpu.VMEM((kw, kw), BF16)],
        compiler_params=pltpu.CompilerParams(dimension_semantics=("parallel",),
                                             vmem_limit_bytes=VMEM_LIMIT),
        name="s5_scan",
    )(xg, kf, kb, w1, w2, sc)


S5_NM = 8


def _block_transpose8(v, width):
    lane = lax.broadcasted_iota(jnp.int32, v[0].shape, 1)
    for d in (4, 2, 1):
        w = width * d
        hi = ((lane // w) % 2) == 1
        out = list(v)
        for i0 in range(8):
            if i0 & d:
                continue
            i1 = i0 + d
            out[i0] = jnp.where(hi, pltpu.roll(v[i1], w, 1), v[i0])
            out[i1] = jnp.where(hi, v[i1], pltpu.roll(v[i0], 8 * width - w, 1))
        v = out
    return v


def _tile_copies(hbm4, tile, buf, slot, sem, nm, to_hbm):
    copies = []
    for c in range(SUBLANES):
        for j in range(S5_LC):
            h = hbm4.at[c, pl.ds(tile * nm, nm), pl.ds(j, 1), :]
            v = buf.at[slot, j, :, pl.ds(c, 1), :]
            copies.append(pltpu.make_async_copy(v, h, sem.at[slot]) if to_hbm
                          else pltpu.make_async_copy(h, v, sem.at[slot]))
    return copies


def _s5_inproj_kernel(x4_ref, gmix_ref, w_ref, xg_ref, xs, sem, *, nm):
    i = pl.program_id(0)
    n = pl.num_programs(0)
    slot = i % 2
    dm = x4_ref.shape[3]

    @pl.when(i == 0)
    def _():
        for cp in _tile_copies(x4_ref, 0, xs, 0, sem, nm, False):
            cp.start()

    @pl.when(i + 1 < n)
    def _():
        for cp in _tile_copies(x4_ref, i + 1, xs, 1 - slot, sem, nm, False):
            cp.start()

    pltpu.make_async_copy(xs.at[slot], xs.at[slot], sem.at[slot]).wait()
    rows = nm * SUBLANES
    x = xs[slot].reshape(S5_LC * rows, dm)
    z = _dot(_rms(x, gmix_ref[...]).astype(BF16), w_ref[...])
    n_oct = z.shape[1] // LANES
    for q in range(n_oct):
        for a in range(S5_LC // 8):
            blocks = [z[(8 * a + j8) * rows:(8 * a + j8 + 1) * rows, q * LANES:(q + 1) * LANES] for j8 in range(8)]
            for g8, b in enumerate(_block_transpose8(blocks, S5_GROUP)):
                xg_ref[8 * q + g8, :, a * LANES:(a + 1) * LANES] = b.astype(BF16)


def _s5_inproj(x, gmix, w_s5_bf, n_seg):
    b, l, dm = x.shape
    steps = l // (S5_LC * n_seg)
    nm = S5_NM
    s5w = w_s5_bf.shape[1]
    g = s5w // S5_GROUP
    x4 = x.reshape(b * n_seg, steps, S5_LC, dm)
    return pl.pallas_call(
        functools.partial(_s5_inproj_kernel, nm=nm),
        grid=(steps // nm,),
        in_specs=[pl.BlockSpec(memory_space=pl.ANY),
                  pl.BlockSpec((1, dm), lambda i: (0, 0)),
                  pl.BlockSpec((dm, s5w), lambda i: (0, 0))],
        out_specs=pl.BlockSpec((g, nm * SUBLANES, S5_LC * S5_GROUP), lambda i: (0, i, 0)),
        out_shape=jax.ShapeDtypeStruct((g, steps * SUBLANES, S5_LC * S5_GROUP), BF16),
        scratch_shapes=[pltpu.VMEM((2, S5_LC, nm, SUBLANES, dm), F32), pltpu.SemaphoreType.DMA((2,))],
        compiler_params=pltpu.CompilerParams(dimension_semantics=("arbitrary",),
                                             vmem_limit_bytes=VMEM_LIMIT),
        name="s5_inproj",
    )(x4, gmix, w_s5_bf)


def _s5_to_tokens_kernel(yg_ref, ys4_ref, zs, sem, *, nm):
    i = pl.program_id(0)
    n = pl.num_programs(0)
    slot = i % 2
    rows = nm * SUBLANES

    def wait(s):
        pltpu.make_async_copy(zs.at[s], zs.at[s], sem.at[s]).wait()

    @pl.when(i >= 2)
    def _():
        wait(slot)

    n_oct = yg_ref.shape[0] // 8
    for q in range(n_oct):
        for a in range(S5_LC // 8):
            blocks = [yg_ref[8 * q + g8, :, a * LANES:(a + 1) * LANES] for g8 in range(8)]
            for j8, b in enumerate(_block_transpose8(blocks, S5_GROUP)):
                zs[slot, 8 * a + j8, :, :, q * LANES:(q + 1) * LANES] = b.reshape(nm, SUBLANES, LANES)
    for cp in _tile_copies(ys4_ref, i, zs, slot, sem, nm, True):
        cp.start()

    @pl.when(i == n - 1)
    def _():
        wait(1 - slot)
        wait(slot)


def _s5_to_tokens(yg, b, l, n_seg):
    g, rows_total, kw = yg.shape
    steps = rows_total // SUBLANES
    nm = S5_NM
    s5w = g * S5_GROUP
    assert steps // nm >= 2
    ys4 = pl.pallas_call(
        functools.partial(_s5_to_tokens_kernel, nm=nm),
        grid=(steps // nm,),
        in_specs=[pl.BlockSpec((g, nm * SUBLANES, kw), lambda i: (0, i, 0))],
        out_specs=pl.BlockSpec(memory_space=pl.ANY),
        out_shape=jax.ShapeDtypeStruct((b * n_seg, steps, S5_LC, s5w), F32),
        scratch_shapes=[pltpu.VMEM((2, S5_LC, nm, SUBLANES, s5w), F32), pltpu.SemaphoreType.DMA((2,))],
        compiler_params=pltpu.CompilerParams(dimension_semantics=("arbitrary",),
                                             vmem_limit_bytes=VMEM_LIMIT),
        name="s5_to_tokens",
    )(yg)
    return ys4.reshape(b * l, s5w)


R_E1, R_E2, R_W1, R_W2, R_RANK1, R_RANK2 = range(6)


def _mix_route_kernel(ys_ref, us5_ref, mgm_ref, x_ref, d_ref, gluw_ref, glub_ref, gs5_ref,
                      wout_ref, gffn_ref, rwh_ref, rwl_ref, rb_ref, tri_ref,
                      x2_ref, t_ref, route_ref, cnt_ref):
    for k in range(x_ref.shape[0] // TM_MIX):
        rows = slice(k * TM_MIX, (k + 1) * TM_MIX)
        _mix_route_tile(ys_ref[rows, :], us5_ref[rows, :], mgm_ref[rows, :], x_ref[rows, :], d_ref, gluw_ref,
                        glub_ref, gs5_ref, wout_ref, gffn_ref, rwh_ref, rwl_ref, rb_ref, tri_ref,
                        x2_ref.at[rows, :], t_ref.at[rows, :], route_ref.at[rows, :], cnt_ref.at[k])


def _mix_route_tile(ys, us5, mgm, x, d_ref, gluw_ref, glub_ref, gs5_ref,
                    wout_ref, gffn_ref, rwh_ref, rwl_ref, rb_ref, tri_ref,
                    x2_ref, t_ref, route_ref, cnt_ref):
    gw = mgm.shape[1]
    y = ys + d_ref[...] * us5
    g = _gelu(y)
    z = g * jax.nn.sigmoid(_dot(g.astype(BF16), gluw_ref[...]) + glub_ref[...])
    ms5 = _rms(z, gs5_ref[...]).astype(BF16)
    mix = _dot(mgm, wout_ref[:gw, :]) + _dot(ms5, wout_ref[gw:, :])
    x2 = x + mix
    x2_ref[...] = x2
    t = _rms(x2, gffn_ref[...])
    t_hi = t.astype(BF16)
    t_ref[...] = t_hi
    t_lo = (t - t_hi.astype(F32)).astype(BF16)
    logits = (_dot(t_hi, rwh_ref[...]) + _dot(t_hi, rwl_ref[...]) + _dot(t_lo, rwh_ref[...])
              + rb_ref[...])
    lane = lax.broadcasted_iota(jnp.int32, logits.shape, 1).astype(F32)
    neg = jnp.float32(-jnp.inf)

    def first_max(mask):
        vals = jnp.where(mask, logits, neg)
        mx = jnp.max(vals, axis=-1, keepdims=True)
        idx = jnp.min(jnp.where(mask & (vals == mx), lane, float(LANES)), axis=-1, keepdims=True)
        return mx, idx

    coarse = lane < N_COARSE
    m1, grp = first_max(coarse)
    p_grp = 1.0 / jnp.sum(jnp.where(coarse, jnp.exp(logits - m1), 0.0), axis=-1, keepdims=True)
    lo = N_COARSE + grp * N_FINE
    fine = (lane >= lo) & (lane < lo + N_FINE)
    v1, i1 = first_max(fine)
    v2, i2 = first_max(fine & (lane != i1))
    e21 = jnp.exp(v2 - v1)
    w1 = p_grp / (1.0 + e21)
    w2 = p_grp * e21 / (1.0 + e21)
    e1 = i1 - N_COARSE
    e2 = i2 - N_COARSE
    hit1 = lane == e1
    hit2 = lane == e2
    onehot = jnp.where(hit1 | hit2, 1.0, 0.0)
    before = _dot(tri_ref[...], onehot.astype(BF16))
    rank1 = jnp.sum(jnp.where(hit1, before, 0.0), axis=-1, keepdims=True)
    rank2 = jnp.sum(jnp.where(hit2, before, 0.0), axis=-1, keepdims=True)
    tm = onehot.shape[0]
    cnt_ref[...] = before[tm - 1:tm, :] + onehot[tm - 1:tm, :]
    rec = jnp.zeros_like(logits)
    for slot, val in ((R_E1, e1), (R_E2, e2), (R_W1, w1), (R_W2, w2),
                      (R_RANK1, rank1), (R_RANK2, rank2)):
        rec = jnp.where(lane == slot, val, rec)
    route_ref[...] = rec


def _mix_route(ys, us5, mgm, x2d, d, gluw_bf, glub, gs5, wout_bf, gffn, rwh, rwl, rb):
    t, dm = x2d.shape
    gw = mgm.shape[1]
    s5w = us5.shape[1]
    tm = TM_MIX * MIX_TILES
    tri = jnp.tril(jnp.ones((TM_MIX, TM_MIX), F32), -1).astype(BF16)
    const = lambda *shape: pl.BlockSpec(shape, lambda i: (0,) * len(shape))
    tile = lambda w: pl.BlockSpec((tm, w), lambda i: (i, 0))
    return pl.pallas_call(
        _mix_route_kernel,
        grid=(t // tm,),
        in_specs=[tile(s5w), tile(s5w), tile(gw), tile(dm),
                  const(1, s5w), const(s5w, s5w), const(1, s5w), const(1, s5w),
                  const(gw + s5w, dm), const(1, dm), const(dm, LANES), const(dm, LANES), const(1, LANES),
                  const(TM_MIX, TM_MIX)],
        out_specs=[tile(dm), tile(dm), tile(LANES), pl.BlockSpec((MIX_TILES, 1, LANES), lambda i: (i, 0, 0))],
        out_shape=[jax.ShapeDtypeStruct((t, dm), F32),
                   jax.ShapeDtypeStruct((t, dm), BF16),
                   jax.ShapeDtypeStruct((t, LANES), F32),
                   jax.ShapeDtypeStruct((t // TM_MIX, 1, LANES), F32)],
        compiler_params=pltpu.CompilerParams(dimension_semantics=("parallel",),
                                             vmem_limit_bytes=VMEM_LIMIT),
        name="mix_route",
    )(ys, us5, mgm, x2d, d, gluw_bf, glub, gs5, wout_bf, gffn, rwh, rwl, rb, tri)


def _local_rows(tm):
    worst = 2 * tm + N_EXPERTS * (SEG_ALIGN - 1)
    return -(-worst // LANES) * LANES


def _segment_plan(cnt, t, tm_expert):
    c = cnt[:, 0, :N_EXPERTS].astype(jnp.int32)
    n_tok_tiles = c.shape[0]
    al = (c + SEG_ALIGN - 1) // SEG_ALIGN * SEG_ALIGN
    lbase = jnp.cumsum(al, axis=1) - al
    tot = jnp.sum(al, axis=0)
    tot_pad = (tot + tm_expert - 1) // tm_expert * tm_expert
    gbase = jnp.cumsum(tot_pad) - tot_pad
    gpos = gbase[None, :] + jnp.cumsum(al, axis=0) - al
    n_tiles_max = -(-(2 * t + n_tok_tiles * N_EXPERTS * (SEG_ALIGN - 1)) // tm_expert) + N_EXPERTS
    tile_end = jnp.cumsum(tot_pad // tm_expert)
    n_tiles = tile_end[-1:].astype(jnp.int32)
    tile_idx = jnp.arange(n_tiles_max, dtype=jnp.int32)
    tile_expert = jnp.sum((tile_idx[:, None] >= tile_end[None, :]).astype(jnp.int32), axis=1)
    last = jnp.sum((n_tiles - 1 >= tile_end).astype(jnp.int32))
    tile_expert = jnp.where(tile_idx < n_tiles, tile_expert, last).astype(jnp.int32)
    lbase_f = jnp.pad(lbase.astype(F32), ((0, 0), (0, LANES - N_EXPERTS)))[:, None, :]
    flat = lambda a: a.reshape(-1).astype(jnp.int32)
    plan = dict(lb=flat(lbase), nch=flat(al // SEG_ALIGN), gpos=flat(gpos),
                tail_pos=flat(gbase + tot), tail_n=flat((tot_pad - tot) // SEG_ALIGN))
    return plan, lbase_f, tile_expert, n_tiles, n_tiles_max * tm_expert


def _local_positions(route, lbase):
    lane = lax.broadcasted_iota(jnp.int32, route.shape, 1).astype(F32)
    out = []
    for e_lane, r_lane in ((R_E1, R_RANK1), (R_E2, R_RANK2)):
        e = route[:, e_lane:e_lane + 1]
        base = jnp.sum(jnp.where(lane == e, lbase, 0.0), axis=-1, keepdims=True)
        out.append(base + route[:, r_lane:r_lane + 1])
    return out


def _segment_copies(i, lb_ref, nch_ref, gp_ref, local, glob, sem, to_global):
    def seg(e, total):
        k = i * N_EXPERTS + e
        lb, n, gp = lb_ref[k], nch_ref[k], gp_ref[k]

        def chunk(c, carry):
            lo = local.at[pl.ds(pl.multiple_of(lb + c * SEG_ALIGN, SEG_ALIGN), SEG_ALIGN)]
            gl = glob.at[pl.ds(pl.multiple_of(gp + c * SEG_ALIGN, SEG_ALIGN), SEG_ALIGN)]
            (pltpu.make_async_copy(lo, gl, sem) if to_global else pltpu.make_async_copy(gl, lo, sem)).start()
            return carry

        lax.fori_loop(0, n, chunk, 0)
        return total + n

    return lax.fori_loop(0, N_EXPERTS, seg, 0)


def _wait_chunks(n, local, glob, sem):
    def one(c, carry):
        pltpu.make_async_copy(local.at[pl.ds(0, SEG_ALIGN)], glob.at[pl.ds(0, SEG_ALIGN)], sem).wait()
        return carry

    lax.fori_loop(0, n, one, 0)


def _sort_rows_kernel(lb_ref, nch_ref, gp_ref, tpos_ref, tn_ref, nt_ref, t_ref, route_ref, lbase_ref,
                      xs_ref, local_scr, zero_scr, cnt_scr, sem, zsem):
    i = pl.program_id(0)
    n = pl.num_programs(0)
    slot = i % 2
    tm = t_ref.shape[0]
    s_rows = local_scr.shape[1]
    local = local_scr.at[slot]

    @pl.when(i >= 2)
    def _():
        _wait_chunks(cnt_scr[slot], local, xs_ref, sem.at[slot])

    lp1, lp2 = _local_positions(route_ref[...], lbase_ref[0])
    lane = lax.broadcasted_iota(jnp.int32, (tm, LANES), 1)
    lp_rows = jnp.where(lane == 0, lp1, jnp.where(lane == 1, lp2, -1.0)).T
    row = lax.broadcasted_iota(jnp.int32, (s_rows, tm), 0).astype(F32)
    onehot = jnp.where((row == lp_rows[0:1, :]) | (row == lp_rows[1:2, :]), 1.0, 0.0).astype(BF16)
    local_scr[slot] = _dot(onehot, t_ref[...]).astype(BF16)
    cnt_scr[slot] = _segment_copies(i, lb_ref, nch_ref, gp_ref, local, xs_ref, sem.at[slot], True)

    @pl.when(i == n - 1)
    def _():
        @pl.when(n >= 2)
        def _():
            _wait_chunks(cnt_scr[1 - slot], local_scr.at[1 - slot], xs_ref, sem.at[1 - slot])

        _wait_chunks(cnt_scr[slot], local, xs_ref, sem.at[slot])
        zero_scr[...] = jnp.zeros_like(zero_scr)
        te = zero_scr.shape[0]
        zero_chunk = zero_scr.at[pl.ds(0, SEG_ALIGN)]

        def tail(e, total):
            def chunk(c, carry):
                dst = xs_ref.at[pl.ds(pl.multiple_of(tpos_ref[e] + c * SEG_ALIGN, SEG_ALIGN), SEG_ALIGN)]
                pltpu.make_async_copy(zero_chunk, dst, zsem).start()
                return carry

            lax.fori_loop(0, tn_ref[e], chunk, 0)
            return total + tn_ref[e]

        _wait_chunks(lax.fori_loop(0, N_EXPERTS, tail, 0), zero_scr, xs_ref, zsem)

        def unused_tile(j, carry):
            pltpu.make_async_copy(zero_scr, xs_ref.at[pl.ds(pl.multiple_of(j * te, te), te)], zsem).start()
            return carry

        def unused_wait(j, carry):
            pltpu.make_async_copy(zero_scr, xs_ref.at[pl.ds(0, te)], zsem).wait()
            return carry

        lax.fori_loop(nt_ref[0], xs_ref.shape[0] // te, unused_tile, 0)
        lax.fori_loop(nt_ref[0], xs_ref.shape[0] // te, unused_wait, 0)


def _plan_specs(plan):
    keys = ('lb', 'nch', 'gpos', 'tail_pos', 'tail_n')
    return [plan[k] for k in keys]


def _sort_rows(plan, n_tiles, t_bf, route, lbase_f, n_sorted):
    t, dm = t_bf.shape
    tm = TM_MIX
    s_rows = _local_rows(tm)
    im = lambda i, *_: (i, 0)
    return pl.pallas_call(
        _sort_rows_kernel,
        grid_spec=pltpu.PrefetchScalarGridSpec(
            num_scalar_prefetch=6,
            grid=(t // tm,),
            in_specs=[pl.BlockSpec((tm, dm), im), pl.BlockSpec((tm, LANES), im),
                      pl.BlockSpec((1, 1, LANES), lambda i, *_: (i, 0, 0))],
            out_specs=pl.BlockSpec(memory_space=pl.ANY),
            scratch_shapes=[pltpu.VMEM((2, s_rows, dm), BF16), pltpu.VMEM((TM_EXPERT, dm), BF16),
                            pltpu.SMEM((2,), jnp.int32), pltpu.SemaphoreType.DMA((2,)),
                            pltpu.SemaphoreType.DMA(())],
        ),
        out_shape=jax.ShapeDtypeStruct((n_sorted, dm), BF16),
        compiler_params=pltpu.CompilerParams(dimension_semantics=("arbitrary",),
                                             vmem_limit_bytes=VMEM_LIMIT),
        name="sort_rows",
    )(*_plan_specs(plan), n_tiles, t_bf, route, lbase_f)


def _experts_kernel(te_ref, nt_ref, xs_ref, wg_ref, wu_ref, wd_ref, ys_ref, wg_bf, wu_bf, wd_bf):
    i = pl.program_id(0)

    @pl.when((i == 0) | (te_ref[i] != te_ref[jnp.maximum(i - 1, 0)]))
    def _():
        wg_bf[...] = wg_ref[0].astype(BF16)
        wu_bf[...] = wu_ref[0].astype(BF16)
        wd_bf[...] = wd_ref[0].astype(BF16)

    @pl.when(i < nt_ref[0])
    def _():
        x = xs_ref[...]
        hidden = (jax.nn.silu(_dot(x, wg_bf[...])) * _dot(x, wu_bf[...])).astype(BF16)
        ys_ref[...] = _dot(hidden, wd_bf[...]).astype(BF16)

    @pl.when(i >= nt_ref[0])
    def _():
        ys_ref[...] = jnp.zeros_like(ys_ref)


def _experts(tile_expert, n_tiles, x_sorted, w_gate, w_up, w_down):
    n_sorted, dm = x_sorted.shape
    de = w_gate.shape[2]
    tm = TM_EXPERT
    return pl.pallas_call(
        _experts_kernel,
        grid_spec=pltpu.PrefetchScalarGridSpec(
            num_scalar_prefetch=2,
            grid=(n_sorted // tm,),
            in_specs=[pl.BlockSpec((tm, dm), lambda i, te, nt: (jnp.minimum(i, nt[0] - 1), 0)),
                      pl.BlockSpec((1, dm, de), lambda i, te, nt: (te[i], 0, 0)),
                      pl.BlockSpec((1, dm, de), lambda i, te, nt: (te[i], 0, 0)),
                      pl.BlockSpec((1, de, dm), lambda i, te, nt: (te[i], 0, 0))],
            out_specs=pl.BlockSpec((tm, dm), lambda i, te, nt: (i, 0)),
            scratch_shapes=[pltpu.VMEM((dm, de), BF16), pltpu.VMEM((dm, de), BF16), pltpu.VMEM((de, dm), BF16)],
        ),
        out_shape=jax.ShapeDtypeStruct((n_sorted, dm), BF16),
        compiler_params=pltpu.CompilerParams(dimension_semantics=("arbitrary",),
                                             vmem_limit_bytes=VMEM_LIMIT),
        name="experts",
    )(tile_expert, n_tiles, x_sorted, w_gate, w_up, w_down)


def _combine_kernel(lb_ref, nch_ref, gp_ref, tpos_ref, tn_ref, x2_ref, route_ref, lbase_ref, gfin_ref, ys_ref,
                    o_ref, local_scr, cnt_scr, sem):
    del tpos_ref, tn_ref
    i = pl.program_id(0)
    n = pl.num_programs(0)
    slot = i % 2
    tm = x2_ref.shape[0]
    s_rows = local_scr.shape[1]

    def fetch(tile, s):
        last = tile * N_EXPERTS + N_EXPERTS - 1
        used = lb_ref[last] + nch_ref[last] * SEG_ALIGN

        def clear(r, carry):
            local_scr[s, pl.ds(pl.multiple_of(r * SEG_ALIGN, SEG_ALIGN), SEG_ALIGN), :] = jnp.zeros(
                (SEG_ALIGN, local_scr.shape[2]), BF16)
            return carry

        lax.fori_loop(used // SEG_ALIGN, s_rows // SEG_ALIGN, clear, 0)
        cnt_scr[s] = _segment_copies(tile, lb_ref, nch_ref, gp_ref, local_scr.at[s], ys_ref, sem.at[s], False)

    @pl.when(i == 0)
    def _():
        fetch(0, 0)

    @pl.when(i + 1 < n)
    def _():
        fetch(i + 1, 1 - slot)

    route = route_ref[...]
    lp1, lp2 = _local_positions(route, lbase_ref[0])
    _wait_chunks(cnt_scr[slot], local_scr.at[slot], ys_ref, sem.at[slot])
    w1, w2 = route[:, R_W1:R_W1 + 1], route[:, R_W2:R_W2 + 1]
    col0 = lax.broadcasted_iota(jnp.int32, (tm, KB), 1).astype(F32)
    moe = None
    for k in range(s_rows // KB):
        col = col0 + float(k * KB)
        pick = (jnp.where(col == lp1, w1, 0.0) + jnp.where(col == lp2, w2, 0.0)).astype(BF16)
        part = _dot(pick, local_scr[slot, k * KB:(k + 1) * KB, :])
        moe = part if moe is None else moe + part
    o_ref[...] = _rms(x2_ref[...] + moe, gfin_ref[...])


def _combine(plan, x2, route, lbase_f, gfin, y_sorted):
    t, dm = x2.shape
    tm = TM_MIX
    s_rows = _local_rows(tm)
    im = lambda i, *_: (i, 0)
    return pl.pallas_call(
        _combine_kernel,
        grid_spec=pltpu.PrefetchScalarGridSpec(
            num_scalar_prefetch=5,
            grid=(t // tm,),
            in_specs=[pl.BlockSpec((tm, dm), im), pl.BlockSpec((tm, LANES), im),
                      pl.BlockSpec((1, 1, LANES), lambda i, *_: (i, 0, 0)),
                      pl.BlockSpec((1, dm), lambda i, *_: (0, 0)),
                      pl.BlockSpec(memory_space=pl.ANY)],
            out_specs=pl.BlockSpec((tm, dm), im),
            scratch_shapes=[pltpu.VMEM((2, s_rows, dm), BF16), pltpu.SMEM((2,), jnp.int32),
                            pltpu.SemaphoreType.DMA((2,))],
        ),
        out_shape=jax.ShapeDtypeStruct((t, dm), F32),
        compiler_params=pltpu.CompilerParams(dimension_semantics=("arbitrary",),
                                             vmem_limit_bytes=VMEM_LIMIT),
        name="combine_norm",
    )(*_plan_specs(plan), x2, route, lbase_f, gfin, y_sorted)


def _layer(x, p, s5_ops, gfin):
    b, l, dm = x.shape
    x2d = x.reshape(b * l, dm)
    mgm, us5 = _inproj_gmlp(x2d, p['gmix'], p['win'], p['lng'], p['lnb'], p['ws'], p['bs'], p['gout_gm'])
    n_seg = SUBLANES // b
    kf, kb, w1, w2, sc = s5_ops[(l // (S5_LC * n_seg))]
    xg = _s5_inproj(x, p['gmix'], p['win_s5'], n_seg)
    yg = _s5_scan(xg, kf, kb, w1, w2, sc, n_seg)
    ys = _s5_to_tokens(yg, b, l, n_seg)
    x2, t_bf, route, counts = _mix_route(ys, us5, mgm, x2d, p['d'], p['gluw'], p['glub'], p['gout_s5'],
                                         p['wout'], p['gffn'], p['rwh'], p['rwl'], p['rb'])
    plan, lbase_f, tile_expert, n_tiles, n_sorted = _segment_plan(counts, b * l, TM_EXPERT)
    x_sorted = _sort_rows(plan, n_tiles, t_bf, route, lbase_f, n_sorted)
    y_sorted = _experts(tile_expert, n_tiles, x_sorted, p['w_gate'], p['w_up'], p['w_down'])
    out = _combine(plan, x2, route, lbase_f, gfin, y_sorted)
    return out.reshape(b, l, dm)


def kernel(x_prompt, x_sample, norm_mix_g, w_in, gm_ln_g, gm_ln_b, gm_ws, gm_bs, s5_lam_re_fwd, s5_lam_im_fwd, s5_log_step_fwd, s5_b_re_fwd, s5_b_im_fwd, s5_c_re_fwd, s5_c_im_fwd, s5_lam_re_bwd, s5_lam_im_bwd, s5_log_step_bwd, s5_b_re_bwd, s5_b_im_bwd, s5_c_re_bwd, s5_c_im_bwd, s5_d, s5_glu_w, s5_glu_b, out_norm_gm, out_norm_s5, w_out, norm_ffn_g, r1_w, r1_b, r2_w, r2_b, e_w_gate, e_w_up, e_w_down, norm_final_g):
    depth = w_in.shape[0]
    gfin = norm_final_g.reshape(1, -1).astype(F32)
    xs = [x_prompt, x_sample]
    for li in range(depth):
        row = lambda a: a[li].reshape(1, -1).astype(F32)
        dm = w_in.shape[1]
        gw = gm_ln_g.shape[1]
        hd_dim = gw // GM_HEADS
        rw = jnp.concatenate([r1_w[li], r2_w[li].transpose(1, 0, 2).reshape(dm, N_EXPERTS)], axis=1).astype(F32)
        rw = jnp.pad(rw, ((0, 0), (0, LANES - rw.shape[1])))
        rwh = rw.astype(BF16)
        rwl = (rw - rwh.astype(F32)).astype(BF16)
        rb = jnp.concatenate([r1_b[li], r2_b[li].reshape(-1)]).astype(F32)
        rb = jnp.pad(rb, (0, LANES - rb.shape[0])).reshape(1, LANES)
        p = dict(
            gmix=row(norm_mix_g), win=w_in[li].astype(BF16), win_s5=w_in[li][:, 2 * gw:].astype(BF16),
            lng=row(gm_ln_g), lnb=row(gm_ln_b),
            ws=gm_ws[li].astype(BF16),
            bs=jnp.broadcast_to(gm_bs[li].astype(F32)[:, :, None], (GM_HEADS, CHUNK, hd_dim)),
            gout_gm=row(out_norm_gm), d=row(s5_d), gluw=s5_glu_w[li].astype(BF16), glub=row(s5_glu_b),
            gout_s5=row(out_norm_s5), wout=w_out[li].astype(BF16), gffn=row(norm_ffn_g),
            rwh=rwh, rwl=rwl, rb=rb,
            w_gate=e_w_gate[li], w_up=e_w_up[li], w_down=e_w_down[li],
        )
        fwd = (s5_lam_re_fwd[li], s5_lam_im_fwd[li], s5_log_step_fwd[li], s5_b_re_fwd[li], s5_b_im_fwd[li],
               s5_c_re_fwd[li], s5_c_im_fwd[li])
        bwd = (s5_lam_re_bwd[li], s5_lam_im_bwd[li], s5_log_step_bwd[li], s5_b_re_bwd[li], s5_b_im_bwd[li],
               s5_c_re_bwd[li], s5_c_im_bwd[li])
        s5_ops = {}
        for x in xs:
            seg_steps = x.shape[1] // (S5_LC * (SUBLANES // x.shape[0]))
            if seg_steps not in s5_ops:
                s5_ops[seg_steps] = _s5_operator(fwd, bwd, S5_LC, seg_steps)
        last = li == depth - 1
        assert last, "depth > 1 needs an un-normalised layer output"
        xs = [_layer(x, p, s5_ops, gfin) for x in xs]
    return tuple(xs)
```

```python
import functools
import math

import jax
import jax.numpy as jnp
from jax import lax
from jax.experimental import pallas as pl
from jax.experimental.pallas import tpu as pltpu

F32 = jnp.float32
BF16 = jnp.bfloat16

EPS = 1e-6
LAMBDA_RE_MAX = -1e-4
GM_HEADS = 4
CHUNK = 128
S5_GROUP = 16
S5_STATE = 64
N_COARSE = 4
N_FINE = 8
N_EXPERTS = N_COARSE * N_FINE

LANES = 128
SUBLANES = 8
S5_LC = 16
VMEM_LIMIT = 56 * 1024 * 1024

TM_PROJ = 512
TM_MIX = 512
MIX_TILES = 2
KB = 256
TM_EXPERT = 512
SEG_ALIGN = 16


def _gelu(x):
    c = math.sqrt(2.0 / math.pi)
    return x * (0.5 * (1.0 + jnp.tanh(c * (x + 0.044715 * (x * x * x)))))


def _rms(x, g):
    ms = jnp.mean(x * x, axis=-1, keepdims=True)
    return x * lax.rsqrt(ms + EPS) * g


def _dot(a, b):
    return jnp.dot(a, b, preferred_element_type=F32)


def _inproj_gmlp_kernel(x_ref, gmix_ref, win_ref, lng_ref, lnb_ref, ws_ref, bs_ref, gout_ref,
                        mgm_ref, us5_ref, y_scr):
    tm = x_ref.shape[0]
    gw = mgm_ref.shape[1]
    hd_dim = gw // GM_HEADS
    n_chunks = tm // CHUNK
    h = _rms(x_ref[...], gmix_ref[...]).astype(BF16)
    proj = _dot(h, win_ref[...])
    us5_ref[...] = proj[:, 2 * gw:]
    u = _gelu(proj[:, :gw])
    v = _gelu(proj[:, gw:2 * gw])
    for hd in range(GM_HEADS):
        lo = hd * hd_dim
        vh = v[:, lo:lo + hd_dim]
        mu = jnp.mean(vh, axis=-1, keepdims=True)
        xc = vh - mu
        var = jnp.mean(xc * xc, axis=-1, keepdims=True)
        vn = (xc * lax.rsqrt(var + EPS) * lng_ref[:, lo:lo + hd_dim]
              + lnb_ref[:, lo:lo + hd_dim]).astype(BF16)
        rhs = jnp.concatenate([vn[c * CHUNK:(c + 1) * CHUNK] for c in range(n_chunks)], axis=1)
        s = _dot(ws_ref[hd], rhs)
        for c in range(n_chunks):
            sc = s[:, c * hd_dim:(c + 1) * hd_dim] + bs_ref[hd]
            y_scr[c * CHUNK:(c + 1) * CHUNK, lo:lo + hd_dim] = u[c * CHUNK:(c + 1) * CHUNK, lo:lo + hd_dim] * sc
    mgm_ref[...] = _rms(y_scr[...], gout_ref[...]).astype(BF16)


def _inproj_gmlp(x2d, gmix, win_bf, lng, lnb, ws_bf, bs_b, gout):
    t, d = x2d.shape
    d_in = win_bf.shape[1]
    gw = lng.shape[1]
    s5w = d_in - 2 * gw
    tm = TM_PROJ
    const = lambda *shape: pl.BlockSpec(shape, lambda i: (0,) * len(shape))
    return pl.pallas_call(
        _inproj_gmlp_kernel,
        grid=(t // tm,),
        in_specs=[
            pl.BlockSpec((tm, d), lambda i: (i, 0)),
            const(1, d), const(d, d_in), const(1, gw), const(1, gw),
            const(GM_HEADS, CHUNK, CHUNK), const(GM_HEADS, CHUNK, gw // GM_HEADS), const(1, gw),
        ],
        out_specs=[pl.BlockSpec((tm, gw), lambda i: (i, 0)),
                   pl.BlockSpec((tm, s5w), lambda i: (i, 0))],
        out_shape=[jax.ShapeDtypeStruct((t, gw), BF16),
                   jax.ShapeDtypeStruct((t, s5w), F32)],
        scratch_shapes=[pltpu.VMEM((tm, gw), F32)],
        compiler_params=pltpu.CompilerParams(dimension_semantics=("parallel",),
                                             vmem_limit_bytes=VMEM_LIMIT),
        name="inproj_gmlp",
    )(x2d, gmix, win_bf, lng, lnb, ws_bf, bs_b, gout)


def _s5_consts(lam_re, lam_im, log_step, b_re, b_im, c_re, c_im, lc):
    lr = jnp.minimum(lam_re.astype(F32), LAMBDA_RE_MAX)
    li = lam_im.astype(F32)
    step = jnp.exp(log_step.astype(F32))[:, None]
    dr, di = lr * step, li * step
    ar, ai = _cexp(dr, di)
    nr, ni = ar - 1.0, ai
    den = lr * lr + li * li
    qr, qi = (nr * lr + ni * li) / den, (ni * lr - nr * li) / den
    br, bi = b_re.astype(F32), b_im.astype(F32)
    bbr = qr[..., None] * br - qi[..., None] * bi
    bbi = qr[..., None] * bi + qi[..., None] * br
    k = jnp.arange(lc + 1, dtype=F32)[:, None, None]
    pwr, pwi = _cexp(k * dr[None], k * di[None])
    return (dr, di), (pwr, pwi), (bbr, bbi), (c_re.astype(F32), c_im.astype(F32))


def _cexp(zr, zi):
    m = jnp.exp(zr)
    return m * jnp.cos(zi), m * jnp.sin(zi)


def _s5_operator(fwd, bwd, lc, seg_steps):
    consts = [_s5_consts(*fwd, lc), _s5_consts(*bwd, lc)]
    g, p, h = consts[0][2][0].shape
    lags, w1_parts, w2_parts, sc_rows, seg_rows = [], [], [], [], []
    for direction, (ld, pw, bb, c) in enumerate(consts):
        (dr, di), (pwr, pwi), (bbr, bbi), (cr, ci) = ld, pw, bb, c
        crt, cit = cr.transpose(0, 2, 1), ci.transpose(0, 2, 1)
        pwrt, pwit = pwr.transpose(1, 2, 0), pwi.transpose(1, 2, 0)
        cpr = crt[:, :, None, :] * pwrt[:, :, :, None] - cit[:, :, None, :] * pwit[:, :, :, None]
        cpi = crt[:, :, None, :] * pwit[:, :, :, None] + cit[:, :, None, :] * pwrt[:, :, :, None]
        ck = jnp.concatenate([cpr[:, :, :lc], cpi[:, :, :lc]], axis=1)
        if direction == 1:
            ck = jnp.flip(ck, 2)
        lags += [jnp.concatenate([bbr.transpose(0, 2, 1), -bbi.transpose(0, 2, 1)], axis=-1),
                 ck.reshape(g, 2 * p, lc * h)]
        er, ei = pwrt[:, :, :lc].transpose(0, 2, 1), pwit[:, :, :lc].transpose(0, 2, 1)
        if direction == 0:
            er, ei = jnp.flip(er, 1), jnp.flip(ei, 1)
        bbrt, bbit = bbr.transpose(0, 2, 1), bbi.transpose(0, 2, 1)
        wr = er[:, :, None, :] * bbrt[:, None] - ei[:, :, None, :] * bbit[:, None]
        wi = er[:, :, None, :] * bbit[:, None] + ei[:, :, None, :] * bbrt[:, None]
        w1_parts += [wr, wi, wi, wr]
        fr, fi = cpr[:, :, 1:lc + 1], cpi[:, :, 1:lc + 1]
        if direction == 1:
            fr, fi = jnp.flip(fr, 2), jnp.flip(fi, 2)
        w2_parts += [fr, -fi]

        def mult(zr, zi):
            return [jnp.concatenate([zr, zr], -1), jnp.concatenate([-zi, zi], -1)]

        sc_rows += mult(*_cexp(lc * dr, lc * di))
        seg_rows += mult(*_cexp((lc * seg_steps) * dr, (lc * seg_steps) * di))
    w1 = jnp.concatenate(w1_parts, axis=-1).reshape(g, lc * h, 8 * p)
    w2 = jnp.concatenate(w2_parts, axis=1).reshape(g, 4 * p, lc * h)
    sc = jnp.stack(sc_rows + seg_rows, axis=1)
    return tuple(lags) + (w1.astype(BF16), w2.astype(BF16), sc.astype(F32))


def _s5_kernel(x_ref, bbf_ref, cpf_ref, bbb_ref, cpb_ref, w1_ref, w2_ref, sc_ref, y_ref,
               loc_scr, sin_scr, m_scr, *, n_seg):
    rows = x_ref.shape[1]
    steps = rows // SUBLANES
    sw = sc_ref.shape[2]
    x = x_ref[0]

    kf = jnp.dot(bbf_ref[0], cpf_ref[0], precision=lax.Precision.HIGHEST, preferred_element_type=F32)
    kb = jnp.dot(bbb_ref[0], cpb_ref[0], precision=lax.Precision.HIGHEST, preferred_element_type=F32)
    hch, kw = kf.shape
    lc = kw // hch
    lane = lax.broadcasted_iota(jnp.int32, kf.shape, 1)
    for s in range(lc):
        f = kf if s == 0 else jnp.where(lane >= s * hch, pltpu.roll(kf, s * hch, 1), 0.0)
        left = (lc - 1 - s) * hch
        b = kb if left == 0 else pltpu.roll(kb, kw - left, 1)
        m_scr[s * hch:(s + 1) * hch, :] = (f + jnp.where(lane < (s + 1) * hch, b, 0.0)).astype(BF16)
    loc_scr[...] = _dot(x, w1_ref[0])

    def bc(i):
        return jnp.broadcast_to(sc_ref[0, i:i + 1, :], (SUBLANES, sw))

    a1f, a2f, a1b, a2b, p1f, p2f, p1b, p2b = [bc(i) for i in range(8)]

    def step_f(s, f, fs):
        r = pl.multiple_of(s * SUBLANES, SUBLANES)
        lf = loc_scr[pl.ds(r, SUBLANES), 0:sw]
        lfs = loc_scr[pl.ds(r, SUBLANES), sw:2 * sw]
        return a1f * f + a2f * fs + lf, a1f * fs - a2f * f + lfs

    def step_b(s, b, bs):
        r = pl.multiple_of((steps - 1 - s) * SUBLANES, SUBLANES)
        lb = loc_scr[pl.ds(r, SUBLANES), 2 * sw:3 * sw]
        lbs = loc_scr[pl.ds(r, SUBLANES), 3 * sw:4 * sw]
        return a1b * b + a2b * bs + lb, a1b * bs - a2b * b + lbs

    zero = jnp.zeros((SUBLANES, sw), F32)

    def pass1(s, carry):
        f, fs, b, bs = carry
        return step_f(s, f, fs) + step_b(s, b, bs)

    f_end, fs_end, b_end, bs_end = lax.fori_loop(0, steps, pass1, (zero, zero, zero, zero), unroll=4)

    seg = lax.broadcasted_iota(jnp.int32, (SUBLANES, sw), 0) % n_seg
    cf, cfs, cb, cbs = zero, zero, zero, zero
    for _ in range(n_seg - 1):
        ef = f_end + p1f * cf + p2f * cfs
        efs = fs_end + p1f * cfs - p2f * cf
        eb = b_end + p1b * cb + p2b * cbs
        ebs = bs_end + p1b * cbs - p2b * cb
        cf = jnp.where(seg >= 1, pltpu.roll(ef, 1, 0), 0.0)
        cfs = jnp.where(seg >= 1, pltpu.roll(efs, 1, 0), 0.0)
        cb = jnp.where(seg <= n_seg - 2, pltpu.roll(eb, SUBLANES - 1, 0), 0.0)
        cbs = jnp.where(seg <= n_seg - 2, pltpu.roll(ebs, SUBLANES - 1, 0), 0.0)

    def pass2(s, carry):
        f, fs, b, bs = carry
        rf = pl.multiple_of(s * SUBLANES, SUBLANES)
        rb = pl.multiple_of((steps - 1 - s) * SUBLANES, SUBLANES)
        sin_scr[pl.ds(rf, SUBLANES), 0:sw] = f
        sin_scr[pl.ds(rb, SUBLANES), sw:2 * sw] = b
        return step_f(s, f, fs) + step_b(s, b, bs)

    lax.fori_loop(0, steps, pass2, (cf, cfs, cb, cbs), unroll=4)

    y_ref[0] = _dot(x, m_scr[...]) + _dot(sin_scr[...].astype(BF16), w2_ref[0])


def _s5_scan(xg, lag_factors, w1, w2, sc, n_seg):
    g, rows, kw = xg.shape
    sw = sc.shape[2]
    blk = lambda a: pl.BlockSpec((1,) + a.shape[1:], lambda i: (i, 0, 0))
    return pl.pallas_call(
        functools.partial(_s5_kernel, n_seg=n_seg),
        grid=(g,),
        in_specs=[blk(xg)] + [blk(a) for a in lag_factors] + [blk(w1), blk(w2), blk(sc)],
        out_specs=pl.BlockSpec((1, rows, kw), lambda i: (i, 0, 0)),
        out_shape=jax.ShapeDtypeStruct((g, rows, kw), F32),
        scratch_shapes=[pltpu.VMEM((rows, 4 * sw), F32), pltpu.VMEM((rows, 2 * sw), F32),
                        pltpu.VMEM((kw, kw), BF16)],
        compiler_params=pltpu.CompilerParams(dimension_semantics=("parallel",),
                                             vmem_limit_bytes=VMEM_LIMIT),
        name="s5_scan",
    )(xg, *lag_factors, w1, w2, sc)


S5_NM = 8


def _block_transpose8(v, width):
    lane = lax.broadcasted_iota(jnp.int32, v[0].shape, 1)
    for d in (4, 2, 1):
        w = width * d
        hi = ((lane // w) % 2) == 1
        out = list(v)
        for i0 in range(8):
            if i0 & d:
                continue
            i1 = i0 + d
            out[i0] = jnp.where(hi, pltpu.roll(v[i1], w, 1), v[i0])
            out[i1] = jnp.where(hi, v[i1], pltpu.roll(v[i0], 8 * width - w, 1))
        v = out
    return v


def _tile_copies(hbm4, tile, buf, slot, sem, nm, to_hbm):
    copies = []
    for c in range(SUBLANES):
        for j in range(S5_LC):
            h = hbm4.at[c, pl.ds(tile * nm, nm), pl.ds(j, 1), :]
            v = buf.at[slot, j, :, pl.ds(c, 1), :]
            copies.append(pltpu.make_async_copy(v, h, sem.at[slot]) if to_hbm
                          else pltpu.make_async_copy(h, v, sem.at[slot]))
    return copies


def _s5_inproj_kernel(x4_ref, gmix_ref, w_ref, xg_ref, xs, sem, *, nm):
    i = pl.program_id(0)
    n = pl.num_programs(0)
    slot = i % 2
    dm = x4_ref.shape[3]

    @pl.when(i == 0)
    def _():
        for cp in _tile_copies(x4_ref, 0, xs, 0, sem, nm, False):
            cp.start()

    @pl.when(i + 1 < n)
    def _():
        for cp in _tile_copies(x4_ref, i + 1, xs, 1 - slot, sem, nm, False):
            cp.start()

    pltpu.make_async_copy(xs.at[slot], xs.at[slot], sem.at[slot]).wait()
    rows = nm * SUBLANES
    x = xs[slot].reshape(S5_LC * rows, dm)
    z = _dot(_rms(x, gmix_ref[...]).astype(BF16), w_ref[...])
    n_oct = z.shape[1] // LANES
    for q in range(n_oct):
        for a in range(S5_LC // 8):
            blocks = [z[(8 * a + j8) * rows:(8 * a + j8 + 1) * rows, q * LANES:(q + 1) * LANES] for j8 in range(8)]
            for g8, b in enumerate(_block_transpose8(blocks, S5_GROUP)):
                xg_ref[8 * q + g8, :, a * LANES:(a + 1) * LANES] = b.astype(BF16)


def _s5_inproj(x, gmix, w_s5_bf, n_seg):
    b, l, dm = x.shape
    steps = l // (S5_LC * n_seg)
    nm = S5_NM
    s5w = w_s5_bf.shape[1]
    g = s5w // S5_GROUP
    x4 = x.reshape(b * n_seg, steps, S5_LC, dm)
    return pl.pallas_call(
        functools.partial(_s5_inproj_kernel, nm=nm),
        grid=(steps // nm,),
        in_specs=[pl.BlockSpec(memory_space=pl.ANY),
                  pl.BlockSpec((1, dm), lambda i: (0, 0)),
                  pl.BlockSpec((dm, s5w), lambda i: (0, 0))],
        out_specs=pl.BlockSpec((g, nm * SUBLANES, S5_LC * S5_GROUP), lambda i: (0, i, 0)),
        out_shape=jax.ShapeDtypeStruct((g, steps * SUBLANES, S5_LC * S5_GROUP), BF16),
        scratch_shapes=[pltpu.VMEM((2, S5_LC, nm, SUBLANES, dm), F32), pltpu.SemaphoreType.DMA((2,))],
        compiler_params=pltpu.CompilerParams(dimension_semantics=("arbitrary",),
                                             vmem_limit_bytes=VMEM_LIMIT),
        name="s5_inproj",
    )(x4, gmix, w_s5_bf)


def _s5_to_tokens_kernel(yg_ref, ys4_ref, zs, sem, *, nm):
    i = pl.program_id(0)
    n = pl.num_programs(0)
    slot = i % 2
    rows = nm * SUBLANES

    def wait(s):
        pltpu.make_async_copy(zs.at[s], zs.at[s], sem.at[s]).wait()

    @pl.when(i >= 2)
    def _():
        wait(slot)

    n_oct = yg_ref.shape[0] // 8
    for q in range(n_oct):
        for a in range(S5_LC // 8):
            blocks = [yg_ref[8 * q + g8, :, a * LANES:(a + 1) * LANES] for g8 in range(8)]
            for j8, b in enumerate(_block_transpose8(blocks, S5_GROUP)):
                zs[slot, 8 * a + j8, :, :, q * LANES:(q + 1) * LANES] = b.reshape(nm, SUBLANES, LANES)
    for cp in _tile_copies(ys4_ref, i, zs, slot, sem, nm, True):
        cp.start()

    @pl.when(i == n - 1)
    def _():
        wait(1 - slot)
        wait(slot)


def _s5_to_tokens(yg, b, l, n_seg):
    g, rows_total, kw = yg.shape
    steps = rows_total // SUBLANES
    nm = S5_NM
    s5w = g * S5_GROUP
    assert steps // nm >= 2
    ys4 = pl.pallas_call(
        functools.partial(_s5_to_tokens_kernel, nm=nm),
        grid=(steps // nm,),
        in_specs=[pl.BlockSpec((g, nm * SUBLANES, kw), lambda i: (0, i, 0))],
        out_specs=pl.BlockSpec(memory_space=pl.ANY),
        out_shape=jax.ShapeDtypeStruct((b * n_seg, steps, S5_LC, s5w), F32),
        scratch_shapes=[pltpu.VMEM((2, S5_LC, nm, SUBLANES, s5w), F32), pltpu.SemaphoreType.DMA((2,))],
        compiler_params=pltpu.CompilerParams(dimension_semantics=("arbitrary",),
                                             vmem_limit_bytes=VMEM_LIMIT),
        name="s5_to_tokens",
    )(yg)
    return ys4.reshape(b * l, s5w)


R_E1, R_E2, R_W1, R_W2, R_RANK1, R_RANK2 = range(6)


def _mix_route_kernel(ys_ref, us5_ref, mgm_ref, x_ref, d_ref, gluw_ref, glub_ref, gs5_ref,
                      wout_ref, gffn_ref, rwh_ref, rwl_ref, rb_ref, tri_ref,
                      x2_ref, t_ref, route_ref, cnt_ref):
    for k in range(x_ref.shape[0] // TM_MIX):
        rows = slice(k * TM_MIX, (k + 1) * TM_MIX)
        _mix_route_tile(ys_ref[rows, :], us5_ref[rows, :], mgm_ref[rows, :], x_ref[rows, :], d_ref, gluw_ref,
                        glub_ref, gs5_ref, wout_ref, gffn_ref, rwh_ref, rwl_ref, rb_ref, tri_ref,
                        x2_ref.at[rows, :], t_ref.at[rows, :], route_ref.at[rows, :], cnt_ref.at[k])


def _mix_route_tile(ys, us5, mgm, x, d_ref, gluw_ref, glub_ref, gs5_ref,
                    wout_ref, gffn_ref, rwh_ref, rwl_ref, rb_ref, tri_ref,
                    x2_ref, t_ref, route_ref, cnt_ref):
    gw = mgm.shape[1]
    y = ys + d_ref[...] * us5
    g = _gelu(y)
    z = g * jax.nn.sigmoid(_dot(g.astype(BF16), gluw_ref[...]) + glub_ref[...])
    ms5 = _rms(z, gs5_ref[...]).astype(BF16)
    mix = _dot(mgm, wout_ref[:gw, :]) + _dot(ms5, wout_ref[gw:, :])
    x2 = x + mix
    x2_ref[...] = x2
    t = _rms(x2, gffn_ref[...])
    t_hi = t.astype(BF16)
    t_ref[...] = t_hi
    t_lo = (t - t_hi.astype(F32)).astype(BF16)
    logits = (_dot(t_hi, rwh_ref[...]) + _dot(t_hi, rwl_ref[...]) + _dot(t_lo, rwh_ref[...])
              + rb_ref[...])
    lane = lax.broadcasted_iota(jnp.int32, logits.shape, 1).astype(F32)
    neg = jnp.float32(-jnp.inf)

    def first_max(mask):
        vals = jnp.where(mask, logits, neg)
        mx = jnp.max(vals, axis=-1, keepdims=True)
        idx = jnp.min(jnp.where(mask & (vals == mx), lane, float(LANES)), axis=-1, keepdims=True)
        return mx, idx

    coarse = lane < N_COARSE
    m1, grp = first_max(coarse)
    p_grp = 1.0 / jnp.sum(jnp.where(coarse, jnp.exp(logits - m1), 0.0), axis=-1, keepdims=True)
    lo = N_COARSE + grp * N_FINE
    fine = (lane >= lo) & (lane < lo + N_FINE)
    v1, i1 = first_max(fine)
    v2, i2 = first_max(fine & (lane != i1))
    e21 = jnp.exp(v2 - v1)
    w1 = p_grp / (1.0 + e21)
    w2 = p_grp * e21 / (1.0 + e21)
    e1 = i1 - N_COARSE
    e2 = i2 - N_COARSE
    hit1 = lane == e1
    hit2 = lane == e2
    onehot = jnp.where(hit1 | hit2, 1.0, 0.0)
    before = _dot(tri_ref[...], onehot.astype(BF16))
    rank1 = jnp.sum(jnp.where(hit1, before, 0.0), axis=-1, keepdims=True)
    rank2 = jnp.sum(jnp.where(hit2, before, 0.0), axis=-1, keepdims=True)
    tm = onehot.shape[0]
    cnt_ref[...] = before[tm - 1:tm, :] + onehot[tm - 1:tm, :]
    rec = jnp.zeros_like(logits)
    for slot, val in ((R_E1, e1), (R_E2, e2), (R_W1, w1), (R_W2, w2),
                      (R_RANK1, rank1), (R_RANK2, rank2)):
        rec = jnp.where(lane == slot, val, rec)
    route_ref[...] = rec


def _mix_route(ys, us5, mgm, x2d, d, gluw_bf, glub, gs5, wout_bf, gffn, rwh, rwl, rb):
    t, dm = x2d.shape
    gw = mgm.shape[1]
    s5w = us5.shape[1]
    tm = TM_MIX * MIX_TILES
    tri = jnp.tril(jnp.ones((TM_MIX, TM_MIX), F32), -1).astype(BF16)
    const = lambda *shape: pl.BlockSpec(shape, lambda i: (0,) * len(shape))
    tile = lambda w: pl.BlockSpec((tm, w), lambda i: (i, 0))
    return pl.pallas_call(
        _mix_route_kernel,
        grid=(t // tm,),
        in_specs=[tile(s5w), tile(s5w), tile(gw), tile(dm),
                  const(1, s5w), const(s5w, s5w), const(1, s5w), const(1, s5w),
                  const(gw + s5w, dm), const(1, dm), const(dm, LANES), const(dm, LANES), const(1, LANES),
                  const(TM_MIX, TM_MIX)],
        out_specs=[tile(dm), tile(dm), tile(LANES), pl.BlockSpec((MIX_TILES, 1, LANES), lambda i: (i, 0, 0))],
        out_shape=[jax.ShapeDtypeStruct((t, dm), F32),
                   jax.ShapeDtypeStruct((t, dm), BF16),
                   jax.ShapeDtypeStruct((t, LANES), F32),
                   jax.ShapeDtypeStruct((t // TM_MIX, 1, LANES), F32)],
        compiler_params=pltpu.CompilerParams(dimension_semantics=("parallel",),
                                             vmem_limit_bytes=VMEM_LIMIT),
        name="mix_route",
    )(ys, us5, mgm, x2d, d, gluw_bf, glub, gs5, wout_bf, gffn, rwh, rwl, rb, tri)


def _local_rows(tm):
    worst = 2 * tm + N_EXPERTS * (SEG_ALIGN - 1)
    return -(-worst // LANES) * LANES


def _segment_plan(cnt, t, tm_expert):
    c = cnt[:, 0, :N_EXPERTS].astype(jnp.int32)
    n_tok_tiles = c.shape[0]
    al = (c + SEG_ALIGN - 1) // SEG_ALIGN * SEG_ALIGN
    lbase = jnp.cumsum(al, axis=1) - al
    tot = jnp.sum(al, axis=0)
    tot_pad = (tot + tm_expert - 1) // tm_expert * tm_expert
    gbase = jnp.cumsum(tot_pad) - tot_pad
    gpos = gbase[None, :] + jnp.cumsum(al, axis=0) - al
    n_tiles_max = -(-(2 * t + n_tok_tiles * N_EXPERTS * (SEG_ALIGN - 1)) // tm_expert) + N_EXPERTS
    tile_end = jnp.cumsum(tot_pad // tm_expert)
    n_tiles = tile_end[-1:].astype(jnp.int32)
    tile_idx = jnp.arange(n_tiles_max, dtype=jnp.int32)
    tile_expert = jnp.sum((tile_idx[:, None] >= tile_end[None, :]).astype(jnp.int32), axis=1)
    last = jnp.sum((n_tiles - 1 >= tile_end).astype(jnp.int32))
    tile_expert = jnp.where(tile_idx < n_tiles, tile_expert, last).astype(jnp.int32)
    ids = jnp.arange(N_EXPERTS, dtype=jnp.int32)
    later_used = (ids[None, :] > ids[:, None]) & (tot_pad[None, :] > 0)
    next_expert = jnp.min(jnp.where(later_used, ids[None, :], N_EXPERTS), axis=1)
    next_expert = jnp.where(next_expert == N_EXPERTS, ids, next_expert).astype(jnp.int32)
    lbase_f = jnp.pad(lbase.astype(F32), ((0, 0), (0, LANES - N_EXPERTS)))[:, None, :]
    flat = lambda a: a.reshape(-1).astype(jnp.int32)
    plan = dict(lb=flat(lbase), nch=flat(al // SEG_ALIGN), gpos=flat(gpos),
                tail_pos=flat(gbase + tot), tail_n=flat((tot_pad - tot) // SEG_ALIGN))
    return plan, lbase_f, (tile_expert, n_tiles, next_expert), n_tiles_max * tm_expert


def _local_positions(route, lbase):
    lane = lax.broadcasted_iota(jnp.int32, route.shape, 1).astype(F32)
    out = []
    for e_lane, r_lane in ((R_E1, R_RANK1), (R_E2, R_RANK2)):
        e = route[:, e_lane:e_lane + 1]
        base = jnp.sum(jnp.where(lane == e, lbase, 0.0), axis=-1, keepdims=True)
        out.append(base + route[:, r_lane:r_lane + 1])
    return out


def _segment_copies(i, lb_ref, nch_ref, gp_ref, local, glob, sem, to_global):
    def seg(e, total):
        k = i * N_EXPERTS + e
        lb, n, gp = lb_ref[k], nch_ref[k], gp_ref[k]

        def chunk(c, carry):
            lo = local.at[pl.ds(pl.multiple_of(lb + c * SEG_ALIGN, SEG_ALIGN), SEG_ALIGN)]
            gl = glob.at[pl.ds(pl.multiple_of(gp + c * SEG_ALIGN, SEG_ALIGN), SEG_ALIGN)]
            (pltpu.make_async_copy(lo, gl, sem) if to_global else pltpu.make_async_copy(gl, lo, sem)).start()
            return carry

        lax.fori_loop(0, n, chunk, 0)
        return total + n

    return lax.fori_loop(0, N_EXPERTS, seg, 0)


def _wait_chunks(n, local, glob, sem):
    def one(c, carry):
        pltpu.make_async_copy(local.at[pl.ds(0, SEG_ALIGN)], glob.at[pl.ds(0, SEG_ALIGN)], sem).wait()
        return carry

    lax.fori_loop(0, n, one, 0)


def _sort_rows_kernel(lb_ref, nch_ref, gp_ref, tpos_ref, tn_ref, nt_ref, t_ref, route_ref, lbase_ref,
                      xs_ref, local_scr, zero_scr, cnt_scr, sem, zsem):
    i = pl.program_id(0)
    n = pl.num_programs(0)
    slot = i % 2
    tm = t_ref.shape[0]
    s_rows = local_scr.shape[1]
    local = local_scr.at[slot]

    @pl.when(i >= 2)
    def _():
        _wait_chunks(cnt_scr[slot], local, xs_ref, sem.at[slot])

    lp1, lp2 = _local_positions(route_ref[...], lbase_ref[0])
    lane = lax.broadcasted_iota(jnp.int32, (tm, LANES), 1)
    lp_rows = jnp.where(lane == 0, lp1, jnp.where(lane == 1, lp2, -1.0)).T
    row = lax.broadcasted_iota(jnp.int32, (s_rows, tm), 0).astype(F32)
    onehot = jnp.where((row == lp_rows[0:1, :]) | (row == lp_rows[1:2, :]), 1.0, 0.0).astype(BF16)
    local_scr[slot] = _dot(onehot, t_ref[...]).astype(BF16)
    cnt_scr[slot] = _segment_copies(i, lb_ref, nch_ref, gp_ref, local, xs_ref, sem.at[slot], True)

    @pl.when(i == n - 1)
    def _():
        @pl.when(n >= 2)
        def _():
            _wait_chunks(cnt_scr[1 - slot], local_scr.at[1 - slot], xs_ref, sem.at[1 - slot])

        _wait_chunks(cnt_scr[slot], local, xs_ref, sem.at[slot])
        zero_scr[...] = jnp.zeros_like(zero_scr)
        te = zero_scr.shape[0]
        zero_chunk = zero_scr.at[pl.ds(0, SEG_ALIGN)]

        def tail(e, total):
            def chunk(c, carry):
                dst = xs_ref.at[pl.ds(pl.multiple_of(tpos_ref[e] + c * SEG_ALIGN, SEG_ALIGN), SEG_ALIGN)]
                pltpu.make_async_copy(zero_chunk, dst, zsem).start()
                return carry

            lax.fori_loop(0, tn_ref[e], chunk, 0)
            return total + tn_ref[e]

        _wait_chunks(lax.fori_loop(0, N_EXPERTS, tail, 0), zero_scr, xs_ref, zsem)

        def unused_tile(j, carry):
            pltpu.make_async_copy(zero_scr, xs_ref.at[pl.ds(pl.multiple_of(j * te, te), te)], zsem).start()
            return carry

        def unused_wait(j, carry):
            pltpu.make_async_copy(zero_scr, xs_ref.at[pl.ds(0, te)], zsem).wait()
            return carry

        lax.fori_loop(nt_ref[0], xs_ref.shape[0] // te, unused_tile, 0)
        lax.fori_loop(nt_ref[0], xs_ref.shape[0] // te, unused_wait, 0)


def _plan_specs(plan):
    keys = ('lb', 'nch', 'gpos', 'tail_pos', 'tail_n')
    return [plan[k] for k in keys]


def _sort_rows(plan, n_tiles, t_bf, route, lbase_f, n_sorted):
    t, dm = t_bf.shape
    tm = TM_MIX
    s_rows = _local_rows(tm)
    im = lambda i, *_: (i, 0)
    return pl.pallas_call(
        _sort_rows_kernel,
        grid_spec=pltpu.PrefetchScalarGridSpec(
            num_scalar_prefetch=6,
            grid=(t // tm,),
            in_specs=[pl.BlockSpec((tm, dm), im), pl.BlockSpec((tm, LANES), im),
                      pl.BlockSpec((1, 1, LANES), lambda i, *_: (i, 0, 0))],
            out_specs=pl.BlockSpec(memory_space=pl.ANY),
            scratch_shapes=[pltpu.VMEM((2, s_rows, dm), BF16), pltpu.VMEM((TM_EXPERT, dm), BF16),
                            pltpu.SMEM((2,), jnp.int32), pltpu.SemaphoreType.DMA((2,)),
                            pltpu.SemaphoreType.DMA(())],
        ),
        out_shape=jax.ShapeDtypeStruct((n_sorted, dm), BF16),
        compiler_params=pltpu.CompilerParams(dimension_semantics=("arbitrary",),
                                             vmem_limit_bytes=VMEM_LIMIT),
        name="sort_rows",
    )(*_plan_specs(plan), n_tiles, t_bf, route, lbase_f)


def _expert_weight_copies(e, slot, hbm, stage, sem):
    return [pltpu.make_async_copy(h.at[e], s.at[slot], sem.at[slot]) for h, s in zip(hbm, stage)]


def _experts_kernel(te_ref, nt_ref, nxt_ref, xs_ref, wg_ref, wu_ref, wd_ref, ys_ref,
                    sg, su, sd, wg_bf, wu_bf, wd_bf, slot_scr, sem):
    i = pl.program_id(0)
    e = te_ref[i]
    hbm, stage = (wg_ref, wu_ref, wd_ref), (sg, su, sd)

    @pl.when(i == 0)
    def _():
        slot_scr[0] = 0
        for cp in _expert_weight_copies(e, 0, hbm, stage, sem):
            cp.start()

    @pl.when((i == 0) | (e != te_ref[jnp.maximum(i - 1, 0)]))
    def _():
        slot = slot_scr[0]
        for cp in _expert_weight_copies(e, slot, hbm, stage, sem):
            cp.wait()
        wg_bf[...] = sg[slot].astype(BF16)
        wu_bf[...] = su[slot].astype(BF16)
        wd_bf[...] = sd[slot].astype(BF16)
        nxt = nxt_ref[e]

        @pl.when(nxt != e)
        def _():
            for cp in _expert_weight_copies(nxt, 1 - slot, hbm, stage, sem):
                cp.start()

        slot_scr[0] = 1 - slot

    @pl.when(i < nt_ref[0])
    def _():
        x = xs_ref[...]
        hidden = (jax.nn.silu(_dot(x, wg_bf[...])) * _dot(x, wu_bf[...])).astype(BF16)
        ys_ref[...] = _dot(hidden, wd_bf[...]).astype(BF16)

    @pl.when(i >= nt_ref[0])
    def _():
        ys_ref[...] = jnp.zeros_like(ys_ref)


def _experts(tile_expert, n_tiles, next_expert, x_sorted, w_gate, w_up, w_down):
    n_sorted, dm = x_sorted.shape
    de = w_gate.shape[2]
    tm = TM_EXPERT
    return pl.pallas_call(
        _experts_kernel,
        grid_spec=pltpu.PrefetchScalarGridSpec(
            num_scalar_prefetch=3,
            grid=(n_sorted // tm,),
            in_specs=[pl.BlockSpec((tm, dm), lambda i, te, nt, nx: (jnp.minimum(i, nt[0] - 1), 0)),
                      pl.BlockSpec(memory_space=pl.ANY), pl.BlockSpec(memory_space=pl.ANY),
                      pl.BlockSpec(memory_space=pl.ANY)],
            out_specs=pl.BlockSpec((tm, dm), lambda i, te, nt, nx: (i, 0)),
            scratch_shapes=[pltpu.VMEM((2, dm, de), F32), pltpu.VMEM((2, dm, de), F32), pltpu.VMEM((2, de, dm), F32),
                            pltpu.VMEM((dm, de), BF16), pltpu.VMEM((dm, de), BF16), pltpu.VMEM((de, dm), BF16),
                            pltpu.SMEM((1,), jnp.int32), pltpu.SemaphoreType.DMA((2,))],
        ),
        out_shape=jax.ShapeDtypeStruct((n_sorted, dm), BF16),
        compiler_params=pltpu.CompilerParams(dimension_semantics=("arbitrary",),
                                             vmem_limit_bytes=VMEM_LIMIT),
        name="experts",
    )(tile_expert, n_tiles, next_expert, x_sorted, w_gate, w_up, w_down)


def _combine_kernel(lb_ref, nch_ref, gp_ref, tpos_ref, tn_ref, x2_ref, route_ref, lbase_ref, gfin_ref, ys_ref,
                    o_ref, local_scr, cnt_scr, sem):
    del tpos_ref, tn_ref
    i = pl.program_id(0)
    n = pl.num_programs(0)
    slot = i % 2
    tm = x2_ref.shape[0]
    s_rows = local_scr.shape[1]

    def fetch(tile, s):
        last = tile * N_EXPERTS + N_EXPERTS - 1
        used = lb_ref[last] + nch_ref[last] * SEG_ALIGN

        def clear(r, carry):
            local_scr[s, pl.ds(pl.multiple_of(r * SEG_ALIGN, SEG_ALIGN), SEG_ALIGN), :] = jnp.zeros(
                (SEG_ALIGN, local_scr.shape[2]), BF16)
            return carry

        lax.fori_loop(used // SEG_ALIGN, s_rows // SEG_ALIGN, clear, 0)
        cnt_scr[s] = _segment_copies(tile, lb_ref, nch_ref, gp_ref, local_scr.at[s], ys_ref, sem.at[s], False)

    @pl.when(i == 0)
    def _():
        fetch(0, 0)

    @pl.when(i + 1 < n)
    def _():
        fetch(i + 1, 1 - slot)

    route = route_ref[...]
    lp1, lp2 = _local_positions(route, lbase_ref[0])
    _wait_chunks(cnt_scr[slot], local_scr.at[slot], ys_ref, sem.at[slot])
    w1, w2 = route[:, R_W1:R_W1 + 1], route[:, R_W2:R_W2 + 1]
    col0 = lax.broadcasted_iota(jnp.int32, (tm, KB), 1).astype(F32)
    moe = None
    for k in range(s_rows // KB):
        col = col0 + float(k * KB)
        pick = (jnp.where(col == lp1, w1, 0.0) + jnp.where(col == lp2, w2, 0.0)).astype(BF16)
        part = _dot(pick, local_scr[slot, k * KB:(k + 1) * KB, :])
        moe = part if moe is None else moe + part
    o_ref[...] = _rms(x2_ref[...] + moe, gfin_ref[...])


def _combine(plan, x2, route, lbase_f, gfin, y_sorted):
    t, dm = x2.shape
    tm = TM_MIX
    s_rows = _local_rows(tm)
    im = lambda i, *_: (i, 0)
    return pl.pallas_call(
        _combine_kernel,
        grid_spec=pltpu.PrefetchScalarGridSpec(
            num_scalar_prefetch=5,
            grid=(t // tm,),
            in_specs=[pl.BlockSpec((tm, dm), im), pl.BlockSpec((tm, LANES), im),
                      pl.BlockSpec((1, 1, LANES), lambda i, *_: (i, 0, 0)),
                      pl.BlockSpec((1, dm), lambda i, *_: (0, 0)),
                      pl.BlockSpec(memory_space=pl.ANY)],
            out_specs=pl.BlockSpec((tm, dm), im),
            scratch_shapes=[pltpu.VMEM((2, s_rows, dm), BF16), pltpu.SMEM((2,), jnp.int32),
                            pltpu.SemaphoreType.DMA((2,))],
        ),
        out_shape=jax.ShapeDtypeStruct((t, dm), F32),
        compiler_params=pltpu.CompilerParams(dimension_semantics=("arbitrary",),
                                             vmem_limit_bytes=VMEM_LIMIT),
        name="combine_norm",
    )(*_plan_specs(plan), x2, route, lbase_f, gfin, y_sorted)


def _layer(x, p, s5_ops, gfin):
    b, l, dm = x.shape
    x2d = x.reshape(b * l, dm)
    mgm, us5 = _inproj_gmlp(x2d, p['gmix'], p['win'], p['lng'], p['lnb'], p['ws'], p['bs'], p['gout_gm'])
    n_seg = SUBLANES // b
    *lag_factors, w1, w2, sc = s5_ops[(l // (S5_LC * n_seg))]
    xg = _s5_inproj(x, p['gmix'], p['win_s5'], n_seg)
    yg = _s5_scan(xg, lag_factors, w1, w2, sc, n_seg)
    ys = _s5_to_tokens(yg, b, l, n_seg)
    x2, t_bf, route, counts = _mix_route(ys, us5, mgm, x2d, p['d'], p['gluw'], p['glub'], p['gout_s5'],
                                         p['wout'], p['gffn'], p['rwh'], p['rwl'], p['rb'])
    plan, lbase_f, tiles, n_sorted = _segment_plan(counts, b * l, TM_EXPERT)
    x_sorted = _sort_rows(plan, tiles[1], t_bf, route, lbase_f, n_sorted)
    y_sorted = _experts(*tiles, x_sorted, p['w_gate'], p['w_up'], p['w_down'])
    out = _combine(plan, x2, route, lbase_f, gfin, y_sorted)
    return out.reshape(b, l, dm)


def kernel(x_prompt, x_sample, norm_mix_g, w_in, gm_ln_g, gm_ln_b, gm_ws, gm_bs, s5_lam_re_fwd, s5_lam_im_fwd, s5_log_step_fwd, s5_b_re_fwd, s5_b_im_fwd, s5_c_re_fwd, s5_c_im_fwd, s5_lam_re_bwd, s5_lam_im_bwd, s5_log_step_bwd, s5_b_re_bwd, s5_b_im_bwd, s5_c_re_bwd, s5_c_im_bwd, s5_d, s5_glu_w, s5_glu_b, out_norm_gm, out_norm_s5, w_out, norm_ffn_g, r1_w, r1_b, r2_w, r2_b, e_w_gate, e_w_up, e_w_down, norm_final_g):
    depth = w_in.shape[0]
    gfin = norm_final_g.reshape(1, -1).astype(F32)
    xs = [x_prompt, x_sample]
    for li in range(depth):
        row = lambda a: a[li].reshape(1, -1).astype(F32)
        dm = w_in.shape[1]
        gw = gm_ln_g.shape[1]
        hd_dim = gw // GM_HEADS
        rw = jnp.concatenate([r1_w[li], r2_w[li].transpose(1, 0, 2).reshape(dm, N_EXPERTS)], axis=1).astype(F32)
        rw = jnp.pad(rw, ((0, 0), (0, LANES - rw.shape[1])))
        rwh = rw.astype(BF16)
        rwl = (rw - rwh.astype(F32)).astype(BF16)
        rb = jnp.concatenate([r1_b[li], r2_b[li].reshape(-1)]).astype(F32)
        rb = jnp.pad(rb, (0, LANES - rb.shape[0])).reshape(1, LANES)
        p = dict(
            gmix=row(norm_mix_g), win=w_in[li].astype(BF16), win_s5=w_in[li][:, 2 * gw:].astype(BF16),
            lng=row(gm_ln_g), lnb=row(gm_ln_b),
            ws=gm_ws[li].astype(BF16),
            bs=jnp.broadcast_to(gm_bs[li].astype(F32)[:, :, None], (GM_HEADS, CHUNK, hd_dim)),
            gout_gm=row(out_norm_gm), d=row(s5_d), gluw=s5_glu_w[li].astype(BF16), glub=row(s5_glu_b),
            gout_s5=row(out_norm_s5), wout=w_out[li].astype(BF16), gffn=row(norm_ffn_g),
            rwh=rwh, rwl=rwl, rb=rb,
            w_gate=e_w_gate[li], w_up=e_w_up[li], w_down=e_w_down[li],
        )
        fwd = (s5_lam_re_fwd[li], s5_lam_im_fwd[li], s5_log_step_fwd[li], s5_b_re_fwd[li], s5_b_im_fwd[li],
               s5_c_re_fwd[li], s5_c_im_fwd[li])
        bwd = (s5_lam_re_bwd[li], s5_lam_im_bwd[li], s5_log_step_bwd[li], s5_b_re_bwd[li], s5_b_im_bwd[li],
               s5_c_re_bwd[li], s5_c_im_bwd[li])
        s5_ops = {}
        for x in xs:
            seg_steps = x.shape[1] // (S5_LC * (SUBLANES // x.shape[0]))
            if seg_steps not in s5_ops:
                s5_ops[seg_steps] = _s5_operator(fwd, bwd, S5_LC, seg_steps)
        last = li == depth - 1
        assert last, "depth > 1 needs an un-normalised layer output"
        xs = [_layer(x, p, s5_ops, gfin) for x in xs]
    return tuple(xs)
```

```python
import functools
import math

import jax
import jax.numpy as jnp
from jax import lax
from jax.experimental import pallas as pl
from jax.experimental.pallas import tpu as pltpu

F32 = jnp.float32
BF16 = jnp.bfloat16

EPS = 1e-6
LAMBDA_RE_MAX = -1e-4
GM_HEADS = 4
CHUNK = 128
S5_GROUP = 16
S5_STATE = 64
N_COARSE = 4
N_FINE = 8
N_EXPERTS = N_COARSE * N_FINE

LANES = 128
SUBLANES = 8
S5_LC = 16
VMEM_LIMIT = 56 * 1024 * 1024

TM_PROJ = 512
TM_MIX = 512
MIX_TILES = 2
KB = 256
TM_EXPERT = 512
SEG_ALIGN = 16


def _gelu(x):
    c = math.sqrt(2.0 / math.pi)
    return x * (0.5 * (1.0 + jnp.tanh(c * (x + 0.044715 * (x * x * x)))))


def _rms(x, g):
    ms = jnp.mean(x * x, axis=-1, keepdims=True)
    return x * lax.rsqrt(ms + EPS) * g


def _dot(a, b):
    return jnp.dot(a, b, preferred_element_type=F32)


def _inproj_gmlp_kernel(x_ref, gmix_ref, win_ref, lng_ref, lnb_ref, ws_ref, bs_ref, gout_ref,
                        mgm_ref, us5_ref, y_scr):
    tm = x_ref.shape[0]
    gw = mgm_ref.shape[1]
    hd_dim = gw // GM_HEADS
    n_chunks = tm // CHUNK
    h = _rms(x_ref[...], gmix_ref[...]).astype(BF16)
    proj = _dot(h, win_ref[...])
    us5_ref[...] = proj[:, 2 * gw:]
    u = _gelu(proj[:, :gw])
    v = _gelu(proj[:, gw:2 * gw])
    for hd in range(GM_HEADS):
        lo = hd * hd_dim
        vh = v[:, lo:lo + hd_dim]
        mu = jnp.mean(vh, axis=-1, keepdims=True)
        xc = vh - mu
        var = jnp.mean(xc * xc, axis=-1, keepdims=True)
        vn = (xc * lax.rsqrt(var + EPS) * lng_ref[:, lo:lo + hd_dim]
              + lnb_ref[:, lo:lo + hd_dim]).astype(BF16)
        rhs = jnp.concatenate([vn[c * CHUNK:(c + 1) * CHUNK] for c in range(n_chunks)], axis=1)
        s = _dot(ws_ref[hd], rhs)
        for c in range(n_chunks):
            sc = s[:, c * hd_dim:(c + 1) * hd_dim] + bs_ref[hd]
            y_scr[c * CHUNK:(c + 1) * CHUNK, lo:lo + hd_dim] = u[c * CHUNK:(c + 1) * CHUNK, lo:lo + hd_dim] * sc
    mgm_ref[...] = _rms(y_scr[...], gout_ref[...]).astype(BF16)


def _inproj_gmlp(x2d, gmix, win_bf, lng, lnb, ws_bf, bs_b, gout):
    t, d = x2d.shape
    d_in = win_bf.shape[1]
    gw = lng.shape[1]
    s5w = d_in - 2 * gw
    tm = TM_PROJ
    const = lambda *shape: pl.BlockSpec(shape, lambda i: (0,) * len(shape))
    return pl.pallas_call(
        _inproj_gmlp_kernel,
        grid=(t // tm,),
        in_specs=[
            pl.BlockSpec((tm, d), lambda i: (i, 0)),
            const(1, d), const(d, d_in), const(1, gw), const(1, gw),
            const(GM_HEADS, CHUNK, CHUNK), const(GM_HEADS, CHUNK, gw // GM_HEADS), const(1, gw),
        ],
        out_specs=[pl.BlockSpec((tm, gw), lambda i: (i, 0)),
                   pl.BlockSpec((tm, s5w), lambda i: (i, 0))],
        out_shape=[jax.ShapeDtypeStruct((t, gw), BF16),
                   jax.ShapeDtypeStruct((t, s5w), F32)],
        scratch_shapes=[pltpu.VMEM((tm, gw), F32)],
        compiler_params=pltpu.CompilerParams(dimension_semantics=("parallel",),
                                             vmem_limit_bytes=VMEM_LIMIT),
        name="inproj_gmlp",
    )(x2d, gmix, win_bf, lng, lnb, ws_bf, bs_b, gout)


def _s5_consts(lam_re, lam_im, log_step, b_re, b_im, c_re, c_im, lc):
    lr = jnp.minimum(lam_re.astype(F32), LAMBDA_RE_MAX)
    li = lam_im.astype(F32)
    step = jnp.exp(log_step.astype(F32))[:, None]
    dr, di = lr * step, li * step
    ar, ai = _cexp(dr, di)
    nr, ni = ar - 1.0, ai
    den = lr * lr + li * li
    qr, qi = (nr * lr + ni * li) / den, (ni * lr - nr * li) / den
    br, bi = b_re.astype(F32), b_im.astype(F32)
    bbr = qr[..., None] * br - qi[..., None] * bi
    bbi = qr[..., None] * bi + qi[..., None] * br
    k = jnp.arange(lc + 1, dtype=F32)[:, None, None]
    pwr, pwi = _cexp(k * dr[None], k * di[None])
    return (dr, di), (pwr, pwi), (bbr, bbi), (c_re.astype(F32), c_im.astype(F32))


def _cexp(zr, zi):
    m = jnp.exp(zr)
    return m * jnp.cos(zi), m * jnp.sin(zi)


def _s5_operator(fwd, bwd, lc, seg_steps):
    consts = [_s5_consts(*fwd, lc), _s5_consts(*bwd, lc)]
    g, p, h = consts[0][2][0].shape
    lags, w1_parts, w2_parts, sc_rows, seg_rows = [], [], [], [], []
    for direction, (ld, pw, bb, c) in enumerate(consts):
        (dr, di), (pwr, pwi), (bbr, bbi), (cr, ci) = ld, pw, bb, c
        crt, cit = cr.transpose(0, 2, 1), ci.transpose(0, 2, 1)
        pwrt, pwit = pwr.transpose(1, 2, 0), pwi.transpose(1, 2, 0)
        cpr = crt[:, :, None, :] * pwrt[:, :, :, None] - cit[:, :, None, :] * pwit[:, :, :, None]
        cpi = crt[:, :, None, :] * pwit[:, :, :, None] + cit[:, :, None, :] * pwrt[:, :, :, None]
        ck = jnp.concatenate([cpr[:, :, :lc], cpi[:, :, :lc]], axis=1)
        if direction == 1:
            ck = jnp.flip(ck, 2)
        lags += [jnp.concatenate([bbr.transpose(0, 2, 1), -bbi.transpose(0, 2, 1)], axis=-1),
                 ck.reshape(g, 2 * p, lc * h)]
        er, ei = pwrt[:, :, :lc].transpose(0, 2, 1), pwit[:, :, :lc].transpose(0, 2, 1)
        if direction == 0:
            er, ei = jnp.flip(er, 1), jnp.flip(ei, 1)
        bbrt, bbit = bbr.transpose(0, 2, 1), bbi.transpose(0, 2, 1)
        wr = er[:, :, None, :] * bbrt[:, None] - ei[:, :, None, :] * bbit[:, None]
        wi = er[:, :, None, :] * bbit[:, None] + ei[:, :, None, :] * bbrt[:, None]
        w1_parts += [wr, wi, wi, wr]
        fr, fi = cpr[:, :, 1:lc + 1], cpi[:, :, 1:lc + 1]
        if direction == 1:
            fr, fi = jnp.flip(fr, 2), jnp.flip(fi, 2)
        w2_parts += [fr, -fi]

        def mult(zr, zi):
            return [jnp.concatenate([zr, zr], -1), jnp.concatenate([-zi, zi], -1)]

        sc_rows += mult(*_cexp(lc * dr, lc * di))
        seg_rows += mult(*_cexp((lc * seg_steps) * dr, (lc * seg_steps) * di))
    w1 = jnp.concatenate(w1_parts, axis=-1).reshape(g, lc * h, 8 * p)
    w2 = jnp.concatenate(w2_parts, axis=1).reshape(g, 4 * p, lc * h)
    sc = jnp.stack(sc_rows + seg_rows, axis=1)
    return tuple(lags) + (w1.astype(BF16), w2.astype(BF16), sc.astype(F32))


def _s5_kernel(x_ref, bbf_ref, cpf_ref, bbb_ref, cpb_ref, w1_ref, w2_ref, sc_ref, y_ref,
               loc_scr, sin_scr, m_scr, *, n_seg):
    rows = x_ref.shape[1]
    steps = rows // SUBLANES
    sw = sc_ref.shape[2]
    x = x_ref[0]

    kf = jnp.dot(bbf_ref[0], cpf_ref[0], precision=lax.Precision.HIGHEST, preferred_element_type=F32)
    kb = jnp.dot(bbb_ref[0], cpb_ref[0], precision=lax.Precision.HIGHEST, preferred_element_type=F32)
    hch, kw = kf.shape
    lc = kw // hch
    lane = lax.broadcasted_iota(jnp.int32, kf.shape, 1)
    for s in range(lc):
        f = kf if s == 0 else jnp.where(lane >= s * hch, pltpu.roll(kf, s * hch, 1), 0.0)
        left = (lc - 1 - s) * hch
        b = kb if left == 0 else pltpu.roll(kb, kw - left, 1)
        m_scr[s * hch:(s + 1) * hch, :] = (f + jnp.where(lane < (s + 1) * hch, b, 0.0)).astype(BF16)
    loc_scr[...] = _dot(x, w1_ref[0])

    def bc(i):
        return jnp.broadcast_to(sc_ref[0, i:i + 1, :], (SUBLANES, sw))

    a1f, a2f, a1b, a2b, p1f, p2f, p1b, p2b = [bc(i) for i in range(8)]

    def step_f(s, f, fs):
        r = pl.multiple_of(s * SUBLANES, SUBLANES)
        lf = loc_scr[pl.ds(r, SUBLANES), 0:sw]
        lfs = loc_scr[pl.ds(r, SUBLANES), sw:2 * sw]
        return a1f * f + a2f * fs + lf, a1f * fs - a2f * f + lfs

    def step_b(s, b, bs):
        r = pl.multiple_of((steps - 1 - s) * SUBLANES, SUBLANES)
        lb = loc_scr[pl.ds(r, SUBLANES), 2 * sw:3 * sw]
        lbs = loc_scr[pl.ds(r, SUBLANES), 3 * sw:4 * sw]
        return a1b * b + a2b * bs + lb, a1b * bs - a2b * b + lbs

    zero = jnp.zeros((SUBLANES, sw), F32)

    def pass1(s, carry):
        f, fs, b, bs = carry
        return step_f(s, f, fs) + step_b(s, b, bs)

    f_end, fs_end, b_end, bs_end = lax.fori_loop(0, steps, pass1, (zero, zero, zero, zero), unroll=4)

    seg = lax.broadcasted_iota(jnp.int32, (SUBLANES, sw), 0) % n_seg
    cf, cfs, cb, cbs = zero, zero, zero, zero
    for _ in range(n_seg - 1):
        ef = f_end + p1f * cf + p2f * cfs
        efs = fs_end + p1f * cfs - p2f * cf
        eb = b_end + p1b * cb + p2b * cbs
        ebs = bs_end + p1b * cbs - p2b * cb
        cf = jnp.where(seg >= 1, pltpu.roll(ef, 1, 0), 0.0)
        cfs = jnp.where(seg >= 1, pltpu.roll(efs, 1, 0), 0.0)
        cb = jnp.where(seg <= n_seg - 2, pltpu.roll(eb, SUBLANES - 1, 0), 0.0)
        cbs = jnp.where(seg <= n_seg - 2, pltpu.roll(ebs, SUBLANES - 1, 0), 0.0)

    def pass2(s, carry):
        f, fs, b, bs = carry
        rf = pl.multiple_of(s * SUBLANES, SUBLANES)
        rb = pl.multiple_of((steps - 1 - s) * SUBLANES, SUBLANES)
        sin_scr[pl.ds(rf, SUBLANES), 0:sw] = f
        sin_scr[pl.ds(rb, SUBLANES), sw:2 * sw] = b
        return step_f(s, f, fs) + step_b(s, b, bs)

    lax.fori_loop(0, steps, pass2, (cf, cfs, cb, cbs), unroll=4)

    y_ref[0] = _dot(x, m_scr[...]) + _dot(sin_scr[...].astype(BF16), w2_ref[0])


def _s5_scan(xg, lag_factors, w1, w2, sc, n_seg):
    g, rows, kw = xg.shape
    sw = sc.shape[2]
    blk = lambda a: pl.BlockSpec((1,) + a.shape[1:], lambda i: (i, 0, 0))
    return pl.pallas_call(
        functools.partial(_s5_kernel, n_seg=n_seg),
        grid=(g,),
        in_specs=[blk(xg)] + [blk(a) for a in lag_factors] + [blk(w1), blk(w2), blk(sc)],
        out_specs=pl.BlockSpec((1, rows, kw), lambda i: (i, 0, 0)),
        out_shape=jax.ShapeDtypeStruct((g, rows, kw), F32),
        scratch_shapes=[pltpu.VMEM((rows, 4 * sw), F32), pltpu.VMEM((rows, 2 * sw), F32),
                        pltpu.VMEM((kw, kw), BF16)],
        compiler_params=pltpu.CompilerParams(dimension_semantics=("parallel",),
                                             vmem_limit_bytes=VMEM_LIMIT),
        name="s5_scan",
    )(xg, *lag_factors, w1, w2, sc)


S5_NM = 8


def _block_transpose8(v, width):
    lane = lax.broadcasted_iota(jnp.int32, v[0].shape, 1)
    for d in (4, 2, 1):
        w = width * d
        hi = ((lane // w) % 2) == 1
        out = list(v)
        for i0 in range(8):
            if i0 & d:
                continue
            i1 = i0 + d
            out[i0] = jnp.where(hi, pltpu.roll(v[i1], w, 1), v[i0])
            out[i1] = jnp.where(hi, v[i1], pltpu.roll(v[i0], 8 * width - w, 1))
        v = out
    return v


def _tile_copies(hbm4, tile, buf, slot, sem, nm, to_hbm):
    copies = []
    for c in range(SUBLANES):
        for j in range(S5_LC):
            h = hbm4.at[c, pl.ds(tile * nm, nm), pl.ds(j, 1), :]
            v = buf.at[slot, j, :, pl.ds(c, 1), :]
            copies.append(pltpu.make_async_copy(v, h, sem.at[slot]) if to_hbm
                          else pltpu.make_async_copy(h, v, sem.at[slot]))
    return copies


def _s5_inproj_kernel(x4_ref, gmix_ref, w_ref, xg_ref, xs, sem, *, nm):
    i = pl.program_id(0)
    n = pl.num_programs(0)
    slot = i % 2
    dm = x4_ref.shape[3]

    @pl.when(i == 0)
    def _():
        for cp in _tile_copies(x4_ref, 0, xs, 0, sem, nm, False):
            cp.start()

    @pl.when(i + 1 < n)
    def _():
        for cp in _tile_copies(x4_ref, i + 1, xs, 1 - slot, sem, nm, False):
            cp.start()

    pltpu.make_async_copy(xs.at[slot], xs.at[slot], sem.at[slot]).wait()
    rows = nm * SUBLANES
    x = xs[slot].reshape(S5_LC * rows, dm)
    z = _dot(_rms(x, gmix_ref[...]).astype(BF16), w_ref[...])
    n_oct = z.shape[1] // LANES
    for q in range(n_oct):
        for a in range(S5_LC // 8):
            blocks = [z[(8 * a + j8) * rows:(8 * a + j8 + 1) * rows, q * LANES:(q + 1) * LANES] for j8 in range(8)]
            for g8, b in enumerate(_block_transpose8(blocks, S5_GROUP)):
                xg_ref[8 * q + g8, :, a * LANES:(a + 1) * LANES] = b.astype(BF16)


def _s5_inproj(x, gmix, w_s5_bf, n_seg):
    b, l, dm = x.shape
    steps = l // (S5_LC * n_seg)
    nm = S5_NM
    s5w = w_s5_bf.shape[1]
    g = s5w // S5_GROUP
    x4 = x.reshape(b * n_seg, steps, S5_LC, dm)
    return pl.pallas_call(
        functools.partial(_s5_inproj_kernel, nm=nm),
        grid=(steps // nm,),
        in_specs=[pl.BlockSpec(memory_space=pl.ANY),
                  pl.BlockSpec((1, dm), lambda i: (0, 0)),
                  pl.BlockSpec((dm, s5w), lambda i: (0, 0))],
        out_specs=pl.BlockSpec((g, nm * SUBLANES, S5_LC * S5_GROUP), lambda i: (0, i, 0)),
        out_shape=jax.ShapeDtypeStruct((g, steps * SUBLANES, S5_LC * S5_GROUP), BF16),
        scratch_shapes=[pltpu.VMEM((2, S5_LC, nm, SUBLANES, dm), F32), pltpu.SemaphoreType.DMA((2,))],
        compiler_params=pltpu.CompilerParams(dimension_semantics=("arbitrary",),
                                             vmem_limit_bytes=VMEM_LIMIT),
        name="s5_inproj",
    )(x4, gmix, w_s5_bf)


def _s5_to_tokens_kernel(yg_ref, ys4_ref, zs, sem, *, nm):
    i = pl.program_id(0)
    n = pl.num_programs(0)
    slot = i % 2
    rows = nm * SUBLANES

    def wait(s):
        pltpu.make_async_copy(zs.at[s], zs.at[s], sem.at[s]).wait()

    @pl.when(i >= 2)
    def _():
        wait(slot)

    n_oct = yg_ref.shape[0] // 8
    for q in range(n_oct):
        for a in range(S5_LC // 8):
            blocks = [yg_ref[8 * q + g8, :, a * LANES:(a + 1) * LANES] for g8 in range(8)]
            for j8, b in enumerate(_block_transpose8(blocks, S5_GROUP)):
                zs[slot, 8 * a + j8, :, :, q * LANES:(q + 1) * LANES] = b.reshape(nm, SUBLANES, LANES)
    for cp in _tile_copies(ys4_ref, i, zs, slot, sem, nm, True):
        cp.start()

    @pl.when(i == n - 1)
    def _():
        wait(1 - slot)
        wait(slot)


def _s5_to_tokens(yg, b, l, n_seg):
    g, rows_total, kw = yg.shape
    steps = rows_total // SUBLANES
    nm = S5_NM
    s5w = g * S5_GROUP
    assert steps // nm >= 2
    ys4 = pl.pallas_call(
        functools.partial(_s5_to_tokens_kernel, nm=nm),
        grid=(steps // nm,),
        in_specs=[pl.BlockSpec((g, nm * SUBLANES, kw), lambda i: (0, i, 0))],
        out_specs=pl.BlockSpec(memory_space=pl.ANY),
        out_shape=jax.ShapeDtypeStruct((b * n_seg, steps, S5_LC, s5w), F32),
        scratch_shapes=[pltpu.VMEM((2, S5_LC, nm, SUBLANES, s5w), F32), pltpu.SemaphoreType.DMA((2,))],
        compiler_params=pltpu.CompilerParams(dimension_semantics=("arbitrary",),
                                             vmem_limit_bytes=VMEM_LIMIT),
        name="s5_to_tokens",
    )(yg)
    return ys4.reshape(b * l, s5w)


R_E1, R_E2, R_W1, R_W2, R_RANK1, R_RANK2 = range(6)


def _mix_route_kernel(ys_ref, us5_ref, mgm_ref, x_ref, d_ref, gluw_ref, glub_ref, gs5_ref,
                      wout_ref, gffn_ref, rwh_ref, rwl_ref, rb_ref, tri_ref,
                      x2_ref, t_ref, route_ref, cnt_ref):
    for k in range(x_ref.shape[0] // TM_MIX):
        rows = slice(k * TM_MIX, (k + 1) * TM_MIX)
        _mix_route_tile(ys_ref[rows, :], us5_ref[rows, :], mgm_ref[rows, :], x_ref[rows, :], d_ref, gluw_ref,
                        glub_ref, gs5_ref, wout_ref, gffn_ref, rwh_ref, rwl_ref, rb_ref, tri_ref,
                        x2_ref.at[rows, :], t_ref.at[rows, :], route_ref.at[rows, :], cnt_ref.at[k])


def _mix_route_tile(ys, us5, mgm, x, d_ref, gluw_ref, glub_ref, gs5_ref,
                    wout_ref, gffn_ref, rwh_ref, rwl_ref, rb_ref, tri_ref,
                    x2_ref, t_ref, route_ref, cnt_ref):
    gw = mgm.shape[1]
    y = ys + d_ref[...] * us5
    g = _gelu(y)
    z = g * jax.nn.sigmoid(_dot(g.astype(BF16), gluw_ref[...]) + glub_ref[...])
    ms5 = _rms(z, gs5_ref[...]).astype(BF16)
    mix = _dot(mgm, wout_ref[:gw, :]) + _dot(ms5, wout_ref[gw:, :])
    x2 = x + mix
    x2_ref[...] = x2
    t = _rms(x2, gffn_ref[...])
    t_hi = t.astype(BF16)
    t_ref[...] = t_hi
    t_lo = (t - t_hi.astype(F32)).astype(BF16)
    logits = (_dot(t_hi, rwh_ref[...]) + _dot(t_hi, rwl_ref[...]) + _dot(t_lo, rwh_ref[...])
              + rb_ref[...])
    lane = lax.broadcasted_iota(jnp.int32, logits.shape, 1).astype(F32)
    neg = jnp.float32(-jnp.inf)

    def first_max(mask):
        vals = jnp.where(mask, logits, neg)
        mx = jnp.max(vals, axis=-1, keepdims=True)
        idx = jnp.min(jnp.where(mask & (vals == mx), lane, float(LANES)), axis=-1, keepdims=True)
        return mx, idx

    coarse = lane < N_COARSE
    m1, grp = first_max(coarse)
    p_grp = 1.0 / jnp.sum(jnp.where(coarse, jnp.exp(logits - m1), 0.0), axis=-1, keepdims=True)
    lo = N_COARSE + grp * N_FINE
    fine = (lane >= lo) & (lane < lo + N_FINE)
    v1, i1 = first_max(fine)
    v2, i2 = first_max(fine & (lane != i1))
    e21 = jnp.exp(v2 - v1)
    w1 = p_grp / (1.0 + e21)
    w2 = p_grp * e21 / (1.0 + e21)
    e1 = i1 - N_COARSE
    e2 = i2 - N_COARSE
    hit1 = lane == e1
    hit2 = lane == e2
    onehot = jnp.where(hit1 | hit2, 1.0, 0.0)
    before = _dot(tri_ref[...], onehot.astype(BF16))
    rank1 = jnp.sum(jnp.where(hit1, before, 0.0), axis=-1, keepdims=True)
    rank2 = jnp.sum(jnp.where(hit2, before, 0.0), axis=-1, keepdims=True)
    tm = onehot.shape[0]
    cnt_ref[...] = before[tm - 1:tm, :] + onehot[tm - 1:tm, :]
    rec = jnp.zeros_like(logits)
    for slot, val in ((R_E1, e1), (R_E2, e2), (R_W1, w1), (R_W2, w2),
                      (R_RANK1, rank1), (R_RANK2, rank2)):
        rec = jnp.where(lane == slot, val, rec)
    route_ref[...] = rec


def _mix_route(ys, us5, mgm, x2d, d, gluw_bf, glub, gs5, wout_bf, gffn, rwh, rwl, rb):
    t, dm = x2d.shape
    gw = mgm.shape[1]
    s5w = us5.shape[1]
    tm = TM_MIX * MIX_TILES
    tri = jnp.tril(jnp.ones((TM_MIX, TM_MIX), F32), -1).astype(BF16)
    const = lambda *shape: pl.BlockSpec(shape, lambda i: (0,) * len(shape))
    tile = lambda w: pl.BlockSpec((tm, w), lambda i: (i, 0))
    return pl.pallas_call(
        _mix_route_kernel,
        grid=(t // tm,),
        in_specs=[tile(s5w), tile(s5w), tile(gw), tile(dm),
                  const(1, s5w), const(s5w, s5w), const(1, s5w), const(1, s5w),
                  const(gw + s5w, dm), const(1, dm), const(dm, LANES), const(dm, LANES), const(1, LANES),
                  const(TM_MIX, TM_MIX)],
        out_specs=[tile(dm), tile(dm), tile(LANES), pl.BlockSpec((MIX_TILES, 1, LANES), lambda i: (i, 0, 0))],
        out_shape=[jax.ShapeDtypeStruct((t, dm), F32),
                   jax.ShapeDtypeStruct((t, dm), BF16),
                   jax.ShapeDtypeStruct((t, LANES), F32),
                   jax.ShapeDtypeStruct((t // TM_MIX, 1, LANES), F32)],
        compiler_params=pltpu.CompilerParams(dimension_semantics=("parallel",),
                                             vmem_limit_bytes=VMEM_LIMIT),
        name="mix_route",
    )(ys, us5, mgm, x2d, d, gluw_bf, glub, gs5, wout_bf, gffn, rwh, rwl, rb, tri)


def _local_rows(tm):
    worst = 2 * tm + N_EXPERTS * (SEG_ALIGN - 1)
    return -(-worst // LANES) * LANES


def _segment_plan(cnt, t, tm_expert):
    c = cnt[:, 0, :N_EXPERTS].astype(jnp.int32)
    n_tok_tiles = c.shape[0]
    al = (c + SEG_ALIGN - 1) // SEG_ALIGN * SEG_ALIGN
    lbase = jnp.cumsum(al, axis=1) - al
    tot = jnp.sum(al, axis=0)
    tot_pad = (tot + tm_expert - 1) // tm_expert * tm_expert
    gbase = jnp.cumsum(tot_pad) - tot_pad
    gpos = gbase[None, :] + jnp.cumsum(al, axis=0) - al
    n_tiles_max = -(-(2 * t + n_tok_tiles * N_EXPERTS * (SEG_ALIGN - 1)) // tm_expert) + N_EXPERTS
    tile_end = jnp.cumsum(tot_pad // tm_expert)
    n_tiles = tile_end[-1:].astype(jnp.int32)
    tile_idx = jnp.arange(n_tiles_max, dtype=jnp.int32)
    tile_expert = jnp.sum((tile_idx[:, None] >= tile_end[None, :]).astype(jnp.int32), axis=1)
    last = jnp.sum((n_tiles - 1 >= tile_end).astype(jnp.int32))
    tile_expert = jnp.where(tile_idx < n_tiles, tile_expert, last).astype(jnp.int32)
    ids = jnp.arange(N_EXPERTS, dtype=jnp.int32)
    later_used = (ids[None, :] > ids[:, None]) & (tot_pad[None, :] > 0)
    next_expert = jnp.min(jnp.where(later_used, ids[None, :], N_EXPERTS), axis=1)
    next_expert = jnp.where(next_expert == N_EXPERTS, ids, next_expert).astype(jnp.int32)
    lbase_f = jnp.pad(lbase.astype(F32), ((0, 0), (0, LANES - N_EXPERTS)))[:, None, :]
    flat = lambda a: a.reshape(-1).astype(jnp.int32)
    nch = al // SEG_ALIGN
    cum = jnp.cumsum(nch, axis=1)
    q = jnp.arange(_local_rows(TM_MIX) // SEG_ALIGN, dtype=jnp.int32)[None, :, None]
    seg_of_q = jnp.sum((q >= cum[:, None, :]).astype(jnp.int32), axis=2)
    in_seg = seg_of_q[:, :, None] == jnp.arange(N_EXPERTS, dtype=jnp.int32)[None, None, :]
    pick = lambda a: jnp.sum(jnp.where(in_seg, a[:, None, :], 0), axis=2)
    dst = pick(gpos) + (q[:, :, 0] - pick(cum - nch)) * SEG_ALIGN
    plan = dict(dst=flat(dst), n_chunks=flat(cum[:, -1]),
                tail_pos=flat(gbase + tot), tail_n=flat((tot_pad - tot) // SEG_ALIGN))
    return plan, lbase_f, (tile_expert, n_tiles, next_expert), n_tiles_max * tm_expert


def _local_positions(route, lbase):
    lane = lax.broadcasted_iota(jnp.int32, route.shape, 1).astype(F32)
    out = []
    for e_lane, r_lane in ((R_E1, R_RANK1), (R_E2, R_RANK2)):
        e = route[:, e_lane:e_lane + 1]
        base = jnp.sum(jnp.where(lane == e, lbase, 0.0), axis=-1, keepdims=True)
        out.append(base + route[:, r_lane:r_lane + 1])
    return out


WAIT_GROUP = 8


def _segment_copies(i, dst_ref, nq_ref, local, glob, sem, to_global):
    per_tile = local.shape[0] // SEG_ALIGN
    n = nq_ref[i]

    def chunk(q, carry):
        lo = local.at[pl.ds(pl.multiple_of(q * SEG_ALIGN, SEG_ALIGN), SEG_ALIGN)]
        gl = glob.at[pl.ds(pl.multiple_of(dst_ref[i * per_tile + q], SEG_ALIGN), SEG_ALIGN)]
        (pltpu.make_async_copy(lo, gl, sem) if to_global else pltpu.make_async_copy(gl, lo, sem)).start()
        return carry

    lax.fori_loop(0, n, chunk, 0)
    return n


def _wait_chunks(n, local, glob, sem):
    def wait_rows(rows):
        def one(c, carry):
            pltpu.make_async_copy(local.at[pl.ds(0, rows)], glob.at[pl.ds(0, rows)], sem).wait()
            return carry
        return one

    lax.fori_loop(0, n // WAIT_GROUP, wait_rows(WAIT_GROUP * SEG_ALIGN), 0)
    lax.fori_loop(0, n % WAIT_GROUP, wait_rows(SEG_ALIGN), 0)


def _sort_rows_kernel(dst_ref, nq_ref, tpos_ref, tn_ref, nt_ref, t_ref, route_ref, lbase_ref,
                      xs_ref, local_scr, zero_scr, cnt_scr, sem, zsem):
    i = pl.program_id(0)
    n = pl.num_programs(0)
    slot = i % 2
    tm = t_ref.shape[0]
    s_rows = local_scr.shape[1]
    local = local_scr.at[slot]

    @pl.when(i >= 2)
    def _():
        _wait_chunks(cnt_scr[slot], local, xs_ref, sem.at[slot])

    lp1, lp2 = _local_positions(route_ref[...], lbase_ref[0])
    lane = lax.broadcasted_iota(jnp.int32, (tm, LANES), 1)
    lp_rows = jnp.where(lane == 0, lp1, jnp.where(lane == 1, lp2, -1.0)).T
    row = lax.broadcasted_iota(jnp.int32, (s_rows, tm), 0).astype(F32)
    onehot = jnp.where((row == lp_rows[0:1, :]) | (row == lp_rows[1:2, :]), 1.0, 0.0).astype(BF16)
    local_scr[slot] = _dot(onehot, t_ref[...]).astype(BF16)
    cnt_scr[slot] = _segment_copies(i, dst_ref, nq_ref, local, xs_ref, sem.at[slot], True)

    @pl.when(i == n - 1)
    def _():
        @pl.when(n >= 2)
        def _():
            _wait_chunks(cnt_scr[1 - slot], local_scr.at[1 - slot], xs_ref, sem.at[1 - slot])

        _wait_chunks(cnt_scr[slot], local, xs_ref, sem.at[slot])
        zero_scr[...] = jnp.zeros_like(zero_scr)
        te = zero_scr.shape[0]
        zero_chunk = zero_scr.at[pl.ds(0, SEG_ALIGN)]

        def tail(e, total):
            def chunk(c, carry):
                dst = xs_ref.at[pl.ds(pl.multiple_of(tpos_ref[e] + c * SEG_ALIGN, SEG_ALIGN), SEG_ALIGN)]
                pltpu.make_async_copy(zero_chunk, dst, zsem).start()
                return carry

            lax.fori_loop(0, tn_ref[e], chunk, 0)
            return total + tn_ref[e]

        _wait_chunks(lax.fori_loop(0, N_EXPERTS, tail, 0), zero_scr, xs_ref, zsem)

        def unused_tile(j, carry):
            pltpu.make_async_copy(zero_scr, xs_ref.at[pl.ds(pl.multiple_of(j * te, te), te)], zsem).start()
            return carry

        def unused_wait(j, carry):
            pltpu.make_async_copy(zero_scr, xs_ref.at[pl.ds(0, te)], zsem).wait()
            return carry

        lax.fori_loop(nt_ref[0], xs_ref.shape[0] // te, unused_tile, 0)
        lax.fori_loop(nt_ref[0], xs_ref.shape[0] // te, unused_wait, 0)


def _plan_specs(plan):
    keys = ('dst', 'n_chunks', 'tail_pos', 'tail_n')
    return [plan[k] for k in keys]


def _sort_rows(plan, n_tiles, t_bf, route, lbase_f, n_sorted):
    t, dm = t_bf.shape
    tm = TM_MIX
    s_rows = _local_rows(tm)
    im = lambda i, *_: (i, 0)
    return pl.pallas_call(
        _sort_rows_kernel,
        grid_spec=pltpu.PrefetchScalarGridSpec(
            num_scalar_prefetch=5,
            grid=(t // tm,),
            in_specs=[pl.BlockSpec((tm, dm), im), pl.BlockSpec((tm, LANES), im),
                      pl.BlockSpec((1, 1, LANES), lambda i, *_: (i, 0, 0))],
            out_specs=pl.BlockSpec(memory_space=pl.ANY),
            scratch_shapes=[pltpu.VMEM((2, s_rows, dm), BF16), pltpu.VMEM((TM_EXPERT, dm), BF16),
                            pltpu.SMEM((2,), jnp.int32), pltpu.SemaphoreType.DMA((2,)),
                            pltpu.SemaphoreType.DMA(())],
        ),
        out_shape=jax.ShapeDtypeStruct((n_sorted, dm), BF16),
        compiler_params=pltpu.CompilerParams(dimension_semantics=("arbitrary",),
                                             vmem_limit_bytes=VMEM_LIMIT),
        name="sort_rows",
    )(*_plan_specs(plan), n_tiles, t_bf, route, lbase_f)


def _expert_weight_copies(e, slot, hbm, stage, sem):
    return [pltpu.make_async_copy(h.at[e], s.at[slot], sem.at[slot]) for h, s in zip(hbm, stage)]


def _experts_kernel(te_ref, nt_ref, nxt_ref, xs_ref, wg_ref, wu_ref, wd_ref, ys_ref,
                    sg, su, sd, wg_bf, wu_bf, wd_bf, slot_scr, sem):
    i = pl.program_id(0)
    e = te_ref[i]
    hbm, stage = (wg_ref, wu_ref, wd_ref), (sg, su, sd)

    @pl.when(i == 0)
    def _():
        slot_scr[0] = 0
        for cp in _expert_weight_copies(e, 0, hbm, stage, sem):
            cp.start()

    @pl.when((i == 0) | (e != te_ref[jnp.maximum(i - 1, 0)]))
    def _():
        slot = slot_scr[0]
        for cp in _expert_weight_copies(e, slot, hbm, stage, sem):
            cp.wait()
        wg_bf[...] = sg[slot].astype(BF16)
        wu_bf[...] = su[slot].astype(BF16)
        wd_bf[...] = sd[slot].astype(BF16)
        nxt = nxt_ref[e]

        @pl.when(nxt != e)
        def _():
            for cp in _expert_weight_copies(nxt, 1 - slot, hbm, stage, sem):
                cp.start()

        slot_scr[0] = 1 - slot

    @pl.when(i < nt_ref[0])
    def _():
        x = xs_ref[...]
        hidden = (jax.nn.silu(_dot(x, wg_bf[...])) * _dot(x, wu_bf[...])).astype(BF16)
        ys_ref[...] = _dot(hidden, wd_bf[...]).astype(BF16)

    @pl.when(i >= nt_ref[0])
    def _():
        ys_ref[...] = jnp.zeros_like(ys_ref)


def _experts(tile_expert, n_tiles, next_expert, x_sorted, w_gate, w_up, w_down):
    n_sorted, dm = x_sorted.shape
    de = w_gate.shape[2]
    tm = TM_EXPERT
    return pl.pallas_call(
        _experts_kernel,
        grid_spec=pltpu.PrefetchScalarGridSpec(
            num_scalar_prefetch=3,
            grid=(n_sorted // tm,),
            in_specs=[pl.BlockSpec((tm, dm), lambda i, te, nt, nx: (jnp.minimum(i, nt[0] - 1), 0)),
                      pl.BlockSpec(memory_space=pl.ANY), pl.BlockSpec(memory_space=pl.ANY),
                      pl.BlockSpec(memory_space=pl.ANY)],
            out_specs=pl.BlockSpec((tm, dm), lambda i, te, nt, nx: (i, 0)),
            scratch_shapes=[pltpu.VMEM((2, dm, de), F32), pltpu.VMEM((2, dm, de), F32), pltpu.VMEM((2, de, dm), F32),
                            pltpu.VMEM((dm, de), BF16), pltpu.VMEM((dm, de), BF16), pltpu.VMEM((de, dm), BF16),
                            pltpu.SMEM((1,), jnp.int32), pltpu.SemaphoreType.DMA((2,))],
        ),
        out_shape=jax.ShapeDtypeStruct((n_sorted, dm), BF16),
        compiler_params=pltpu.CompilerParams(dimension_semantics=("arbitrary",),
                                             vmem_limit_bytes=VMEM_LIMIT),
        name="experts",
    )(tile_expert, n_tiles, next_expert, x_sorted, w_gate, w_up, w_down)


def _combine_kernel(dst_ref, nq_ref, tpos_ref, tn_ref, x2_ref, route_ref, lbase_ref, gfin_ref, ys_ref,
                    o_ref, local_scr, cnt_scr, sem):
    del tpos_ref, tn_ref
    i = pl.program_id(0)
    n = pl.num_programs(0)
    slot = i % 2
    tm = x2_ref.shape[0]
    s_rows = local_scr.shape[1]

    def fetch(tile, s):
        def clear(r, carry):
            local_scr[s, pl.ds(pl.multiple_of(r * SEG_ALIGN, SEG_ALIGN), SEG_ALIGN), :] = jnp.zeros(
                (SEG_ALIGN, local_scr.shape[2]), BF16)
            return carry

        lax.fori_loop(nq_ref[tile], s_rows // SEG_ALIGN, clear, 0)
        cnt_scr[s] = _segment_copies(tile, dst_ref, nq_ref, local_scr.at[s], ys_ref, sem.at[s], False)

    @pl.when(i == 0)
    def _():
        fetch(0, 0)

    @pl.when(i + 1 < n)
    def _():
        fetch(i + 1, 1 - slot)

    route = route_ref[...]
    lp1, lp2 = _local_positions(route, lbase_ref[0])
    _wait_chunks(cnt_scr[slot], local_scr.at[slot], ys_ref, sem.at[slot])
    w1, w2 = route[:, R_W1:R_W1 + 1], route[:, R_W2:R_W2 + 1]
    col0 = lax.broadcasted_iota(jnp.int32, (tm, KB), 1).astype(F32)
    moe = None
    for k in range(s_rows // KB):
        col = col0 + float(k * KB)
        pick = (jnp.where(col == lp1, w1, 0.0) + jnp.where(col == lp2, w2, 0.0)).astype(BF16)
        part = _dot(pick, local_scr[slot, k * KB:(k + 1) * KB, :])
        moe = part if moe is None else moe + part
    o_ref[...] = _rms(x2_ref[...] + moe, gfin_ref[...])


def _combine(plan, x2, route, lbase_f, gfin, y_sorted):
    t, dm = x2.shape
    tm = TM_MIX
    s_rows = _local_rows(tm)
    im = lambda i, *_: (i, 0)
    return pl.pallas_call(
        _combine_kernel,
        grid_spec=pltpu.PrefetchScalarGridSpec(
            num_scalar_prefetch=4,
            grid=(t // tm,),
            in_specs=[pl.BlockSpec((tm, dm), im), pl.BlockSpec((tm, LANES), im),
                      pl.BlockSpec((1, 1, LANES), lambda i, *_: (i, 0, 0)),
                      pl.BlockSpec((1, dm), lambda i, *_: (0, 0)),
                      pl.BlockSpec(memory_space=pl.ANY)],
            out_specs=pl.BlockSpec((tm, dm), im),
            scratch_shapes=[pltpu.VMEM((2, s_rows, dm), BF16), pltpu.SMEM((2,), jnp.int32),
                            pltpu.SemaphoreType.DMA((2,))],
        ),
        out_shape=jax.ShapeDtypeStruct((t, dm), F32),
        compiler_params=pltpu.CompilerParams(dimension_semantics=("arbitrary",),
                                             vmem_limit_bytes=VMEM_LIMIT),
        name="combine_norm",
    )(*_plan_specs(plan), x2, route, lbase_f, gfin, y_sorted)


def _layer(x, p, s5_ops, gfin):
    b, l, dm = x.shape
    x2d = x.reshape(b * l, dm)
    mgm, us5 = _inproj_gmlp(x2d, p['gmix'], p['win'], p['lng'], p['lnb'], p['ws'], p['bs'], p['gout_gm'])
    n_seg = SUBLANES // b
    *lag_factors, w1, w2, sc = s5_ops[(l // (S5_LC * n_seg))]
    xg = _s5_inproj(x, p['gmix'], p['win_s5'], n_seg)
    yg = _s5_scan(xg, lag_factors, w1, w2, sc, n_seg)
    ys = _s5_to_tokens(yg, b, l, n_seg)
    x2, t_bf, route, counts = _mix_route(ys, us5, mgm, x2d, p['d'], p['gluw'], p['glub'], p['gout_s5'],
                                         p['wout'], p['gffn'], p['rwh'], p['rwl'], p['rb'])
    plan, lbase_f, tiles, n_sorted = _segment_plan(counts, b * l, TM_EXPERT)
    x_sorted = _sort_rows(plan, tiles[1], t_bf, route, lbase_f, n_sorted)
    y_sorted = _experts(*tiles, x_sorted, p['w_gate'], p['w_up'], p['w_down'])
    out = _combine(plan, x2, route, lbase_f, gfin, y_sorted)
    return out.reshape(b, l, dm)


def kernel(x_prompt, x_sample, norm_mix_g, w_in, gm_ln_g, gm_ln_b, gm_ws, gm_bs, s5_lam_re_fwd, s5_lam_im_fwd, s5_log_step_fwd, s5_b_re_fwd, s5_b_im_fwd, s5_c_re_fwd, s5_c_im_fwd, s5_lam_re_bwd, s5_lam_im_bwd, s5_log_step_bwd, s5_b_re_bwd, s5_b_im_bwd, s5_c_re_bwd, s5_c_im_bwd, s5_d, s5_glu_w, s5_glu_b, out_norm_gm, out_norm_s5, w_out, norm_ffn_g, r1_w, r1_b, r2_w, r2_b, e_w_gate, e_w_up, e_w_down, norm_final_g):
    depth = w_in.shape[0]
    gfin = norm_final_g.reshape(1, -1).astype(F32)
    xs = [x_prompt, x_sample]
    for li in range(depth):
        row = lambda a: a[li].reshape(1, -1).astype(F32)
        dm = w_in.shape[1]
        gw = gm_ln_g.shape[1]
        hd_dim = gw // GM_HEADS
        rw = jnp.concatenate([r1_w[li], r2_w[li].transpose(1, 0, 2).reshape(dm, N_EXPERTS)], axis=1).astype(F32)
        rw = jnp.pad(rw, ((0, 0), (0, LANES - rw.shape[1])))
        rwh = rw.astype(BF16)
        rwl = (rw - rwh.astype(F32)).astype(BF16)
        rb = jnp.concatenate([r1_b[li], r2_b[li].reshape(-1)]).astype(F32)
        rb = jnp.pad(rb, (0, LANES - rb.shape[0])).reshape(1, LANES)
        p = dict(
            gmix=row(norm_mix_g), win=w_in[li].astype(BF16), win_s5=w_in[li][:, 2 * gw:].astype(BF16),
            lng=row(gm_ln_g), lnb=row(gm_ln_b),
            ws=gm_ws[li].astype(BF16),
            bs=jnp.broadcast_to(gm_bs[li].astype(F32)[:, :, None], (GM_HEADS, CHUNK, hd_dim)),
            gout_gm=row(out_norm_gm), d=row(s5_d), gluw=s5_glu_w[li].astype(BF16), glub=row(s5_glu_b),
            gout_s5=row(out_norm_s5), wout=w_out[li].astype(BF16), gffn=row(norm_ffn_g),
            rwh=rwh, rwl=rwl, rb=rb,
            w_gate=e_w_gate[li], w_up=e_w_up[li], w_down=e_w_down[li],
        )
        fwd = (s5_lam_re_fwd[li], s5_lam_im_fwd[li], s5_log_step_fwd[li], s5_b_re_fwd[li], s5_b_im_fwd[li],
               s5_c_re_fwd[li], s5_c_im_fwd[li])
        bwd = (s5_lam_re_bwd[li], s5_lam_im_bwd[li], s5_log_step_bwd[li], s5_b_re_bwd[li], s5_b_im_bwd[li],
               s5_c_re_bwd[li], s5_c_im_bwd[li])
        s5_ops = {}
        for x in xs:
            seg_steps = x.shape[1] // (S5_LC * (SUBLANES // x.shape[0]))
            if seg_steps not in s5_ops:
                s5_ops[seg_steps] = _s5_operator(fwd, bwd, S5_LC, seg_steps)
        last = li == depth - 1
        assert last, "depth > 1 needs an un-normalised layer output"
        xs = [_layer(x, p, s5_ops, gfin) for x in xs]
    return tuple(xs)
```

```python
import functools
import math

import jax
import jax.numpy as jnp
from jax import lax
from jax.experimental import pallas as pl
from jax.experimental.pallas import tpu as pltpu

F32 = jnp.float32
BF16 = jnp.bfloat16

EPS = 1e-6
LAMBDA_RE_MAX = -1e-4
GM_HEADS = 4
CHUNK = 128
S5_GROUP = 16
S5_STATE = 64
N_COARSE = 4
N_FINE = 8
N_EXPERTS = N_COARSE * N_FINE

LANES = 128
SUBLANES = 8
S5_LC = 16
VMEM_LIMIT = 56 * 1024 * 1024

TM_PROJ = 1024
TM_MIX = 512
MIX_TILES = 2
KB = 256
TM_EXPERT = 512
SEG_ALIGN = 16


def _gelu(x):
    c = math.sqrt(2.0 / math.pi)
    return x * (0.5 * (1.0 + jnp.tanh(c * (x + 0.044715 * (x * x * x)))))


def _rms(x, g):
    ms = jnp.mean(x * x, axis=-1, keepdims=True)
    return x * lax.rsqrt(ms + EPS) * g


def _dot(a, b):
    return jnp.dot(a, b, preferred_element_type=F32)


def _inproj_gmlp_kernel(x_ref, gmix_ref, win_ref, lng_ref, lnb_ref, ws_ref, bs_ref, gout_ref,
                        mgm_ref, us5_ref, y_scr):
    tm = x_ref.shape[0]
    gw = mgm_ref.shape[1]
    hd_dim = gw // GM_HEADS
    n_chunks = tm // CHUNK
    h = _rms(x_ref[...], gmix_ref[...]).astype(BF16)
    proj = _dot(h, win_ref[...])
    us5_ref[...] = proj[:, 2 * gw:]
    u = _gelu(proj[:, :gw])
    v = _gelu(proj[:, gw:2 * gw])
    for hd in range(GM_HEADS):
        lo = hd * hd_dim
        vh = v[:, lo:lo + hd_dim]
        mu = jnp.mean(vh, axis=-1, keepdims=True)
        xc = vh - mu
        var = jnp.mean(xc * xc, axis=-1, keepdims=True)
        vn = (xc * lax.rsqrt(var + EPS) * lng_ref[:, lo:lo + hd_dim]
              + lnb_ref[:, lo:lo + hd_dim]).astype(BF16)
        rhs = jnp.concatenate([vn[c * CHUNK:(c + 1) * CHUNK] for c in range(n_chunks)], axis=1)
        s = _dot(ws_ref[hd], rhs)
        for c in range(n_chunks):
            sc = s[:, c * hd_dim:(c + 1) * hd_dim] + bs_ref[hd]
            y_scr[c * CHUNK:(c + 1) * CHUNK, lo:lo + hd_dim] = u[c * CHUNK:(c + 1) * CHUNK, lo:lo + hd_dim] * sc
    mgm_ref[...] = _rms(y_scr[...], gout_ref[...]).astype(BF16)


def _inproj_gmlp(x2d, gmix, win_bf, lng, lnb, ws_bf, bs_b, gout):
    t, d = x2d.shape
    d_in = win_bf.shape[1]
    gw = lng.shape[1]
    s5w = d_in - 2 * gw
    tm = TM_PROJ
    const = lambda *shape: pl.BlockSpec(shape, lambda i: (0,) * len(shape))
    return pl.pallas_call(
        _inproj_gmlp_kernel,
        grid=(t // tm,),
        in_specs=[
            pl.BlockSpec((tm, d), lambda i: (i, 0)),
            const(1, d), const(d, d_in), const(1, gw), const(1, gw),
            const(GM_HEADS, CHUNK, CHUNK), const(GM_HEADS, CHUNK, gw // GM_HEADS), const(1, gw),
        ],
        out_specs=[pl.BlockSpec((tm, gw), lambda i: (i, 0)),
                   pl.BlockSpec((tm, s5w), lambda i: (i, 0))],
        out_shape=[jax.ShapeDtypeStruct((t, gw), BF16),
                   jax.ShapeDtypeStruct((t, s5w), F32)],
        scratch_shapes=[pltpu.VMEM((tm, gw), F32)],
        compiler_params=pltpu.CompilerParams(dimension_semantics=("parallel",),
                                             vmem_limit_bytes=VMEM_LIMIT),
        name="inproj_gmlp",
    )(x2d, gmix, win_bf, lng, lnb, ws_bf, bs_b, gout)


def _s5_consts(lam_re, lam_im, log_step, b_re, b_im, c_re, c_im, lc):
    lr = jnp.minimum(lam_re.astype(F32), LAMBDA_RE_MAX)
    li = lam_im.astype(F32)
    step = jnp.exp(log_step.astype(F32))[:, None]
    dr, di = lr * step, li * step
    ar, ai = _cexp(dr, di)
    nr, ni = ar - 1.0, ai
    den = lr * lr + li * li
    qr, qi = (nr * lr + ni * li) / den, (ni * lr - nr * li) / den
    br, bi = b_re.astype(F32), b_im.astype(F32)
    bbr = qr[..., None] * br - qi[..., None] * bi
    bbi = qr[..., None] * bi + qi[..., None] * br
    k = jnp.arange(lc + 1, dtype=F32)[:, None, None]
    pwr, pwi = _cexp(k * dr[None], k * di[None])
    return (dr, di), (pwr, pwi), (bbr, bbi), (c_re.astype(F32), c_im.astype(F32))


def _cexp(zr, zi):
    m = jnp.exp(zr)
    return m * jnp.cos(zi), m * jnp.sin(zi)


def _s5_operator(fwd, bwd, lc, seg_steps):
    consts = [_s5_consts(*fwd, lc), _s5_consts(*bwd, lc)]
    g, p, h = consts[0][2][0].shape
    lags, w1_parts, w2_parts, sc_rows, seg_rows = [], [], [], [], []
    for direction, (ld, pw, bb, c) in enumerate(consts):
        (dr, di), (pwr, pwi), (bbr, bbi), (cr, ci) = ld, pw, bb, c
        crt, cit = cr.transpose(0, 2, 1), ci.transpose(0, 2, 1)
        pwrt, pwit = pwr.transpose(1, 2, 0), pwi.transpose(1, 2, 0)
        cpr = crt[:, :, None, :] * pwrt[:, :, :, None] - cit[:, :, None, :] * pwit[:, :, :, None]
        cpi = crt[:, :, None, :] * pwit[:, :, :, None] + cit[:, :, None, :] * pwrt[:, :, :, None]
        ck = jnp.concatenate([cpr[:, :, :lc], cpi[:, :, :lc]], axis=1)
        if direction == 1:
            ck = jnp.flip(ck, 2)
        lags += [jnp.concatenate([bbr.transpose(0, 2, 1), -bbi.transpose(0, 2, 1)], axis=-1),
                 ck.reshape(g, 2 * p, lc * h)]
        er, ei = pwrt[:, :, :lc].transpose(0, 2, 1), pwit[:, :, :lc].transpose(0, 2, 1)
        if direction == 0:
            er, ei = jnp.flip(er, 1), jnp.flip(ei, 1)
        bbrt, bbit = bbr.transpose(0, 2, 1), bbi.transpose(0, 2, 1)
        wr = er[:, :, None, :] * bbrt[:, None] - ei[:, :, None, :] * bbit[:, None]
        wi = er[:, :, None, :] * bbit[:, None] + ei[:, :, None, :] * bbrt[:, None]
        w1_parts += [wr, wi, wi, wr]
        fr, fi = cpr[:, :, 1:lc + 1], cpi[:, :, 1:lc + 1]
        if direction == 1:
            fr, fi = jnp.flip(fr, 2), jnp.flip(fi, 2)
        w2_parts += [fr, -fi]

        def mult(zr, zi):
            return [jnp.concatenate([zr, zr], -1), jnp.concatenate([-zi, zi], -1)]

        sc_rows += mult(*_cexp(lc * dr, lc * di))
        seg_rows += mult(*_cexp((lc * seg_steps) * dr, (lc * seg_steps) * di))
    w1 = jnp.concatenate(w1_parts, axis=-1).reshape(g, lc * h, 8 * p)
    w2 = jnp.concatenate(w2_parts, axis=1).reshape(g, 4 * p, lc * h)
    sc = jnp.stack(sc_rows + seg_rows, axis=1)
    return tuple(lags) + (w1.astype(BF16), w2.astype(BF16), sc.astype(F32))


def _s5_kernel(x_ref, bbf_ref, cpf_ref, bbb_ref, cpb_ref, w1_ref, w2_ref, sc_ref, y_ref,
               loc_scr, sin_scr, m_scr, *, n_seg):
    rows = x_ref.shape[1]
    steps = rows // SUBLANES
    sw = sc_ref.shape[2]
    x = x_ref[0]

    kf = jnp.dot(bbf_ref[0], cpf_ref[0], precision=lax.Precision.HIGHEST, preferred_element_type=F32)
    kb = jnp.dot(bbb_ref[0], cpb_ref[0], precision=lax.Precision.HIGHEST, preferred_element_type=F32)
    hch, kw = kf.shape
    lc = kw // hch
    lane = lax.broadcasted_iota(jnp.int32, kf.shape, 1)
    for s in range(lc):
        f = kf if s == 0 else jnp.where(lane >= s * hch, pltpu.roll(kf, s * hch, 1), 0.0)
        left = (lc - 1 - s) * hch
        b = kb if left == 0 else pltpu.roll(kb, kw - left, 1)
        m_scr[s * hch:(s + 1) * hch, :] = (f + jnp.where(lane < (s + 1) * hch, b, 0.0)).astype(BF16)
    loc_scr[...] = _dot(x, w1_ref[0])

    def bc(i):
        return jnp.broadcast_to(sc_ref[0, i:i + 1, :], (SUBLANES, sw))

    a1f, a2f, a1b, a2b, p1f, p2f, p1b, p2b = [bc(i) for i in range(8)]

    def step_f(s, f, fs):
        r = pl.multiple_of(s * SUBLANES, SUBLANES)
        lf = loc_scr[pl.ds(r, SUBLANES), 0:sw]
        lfs = loc_scr[pl.ds(r, SUBLANES), sw:2 * sw]
        return a1f * f + a2f * fs + lf, a1f * fs - a2f * f + lfs

    def step_b(s, b, bs):
        r = pl.multiple_of((steps - 1 - s) * SUBLANES, SUBLANES)
        lb = loc_scr[pl.ds(r, SUBLANES), 2 * sw:3 * sw]
        lbs = loc_scr[pl.ds(r, SUBLANES), 3 * sw:4 * sw]
        return a1b * b + a2b * bs + lb, a1b * bs - a2b * b + lbs

    zero = jnp.zeros((SUBLANES, sw), F32)

    def pass1(s, carry):
        f, fs, b, bs = carry
        return step_f(s, f, fs) + step_b(s, b, bs)

    f_end, fs_end, b_end, bs_end = lax.fori_loop(0, steps, pass1, (zero, zero, zero, zero), unroll=4)

    seg = lax.broadcasted_iota(jnp.int32, (SUBLANES, sw), 0) % n_seg
    cf, cfs, cb, cbs = zero, zero, zero, zero
    for _ in range(n_seg - 1):
        ef = f_end + p1f * cf + p2f * cfs
        efs = fs_end + p1f * cfs - p2f * cf
        eb = b_end + p1b * cb + p2b * cbs
        ebs = bs_end + p1b * cbs - p2b * cb
        cf = jnp.where(seg >= 1, pltpu.roll(ef, 1, 0), 0.0)
        cfs = jnp.where(seg >= 1, pltpu.roll(efs, 1, 0), 0.0)
        cb = jnp.where(seg <= n_seg - 2, pltpu.roll(eb, SUBLANES - 1, 0), 0.0)
        cbs = jnp.where(seg <= n_seg - 2, pltpu.roll(ebs, SUBLANES - 1, 0), 0.0)

    def pass2(s, carry):
        f, fs, b, bs = carry
        rf = pl.multiple_of(s * SUBLANES, SUBLANES)
        rb = pl.multiple_of((steps - 1 - s) * SUBLANES, SUBLANES)
        sin_scr[pl.ds(rf, SUBLANES), 0:sw] = f
        sin_scr[pl.ds(rb, SUBLANES), sw:2 * sw] = b
        return step_f(s, f, fs) + step_b(s, b, bs)

    lax.fori_loop(0, steps, pass2, (cf, cfs, cb, cbs), unroll=4)

    y_ref[0] = _dot(x, m_scr[...]) + _dot(sin_scr[...].astype(BF16), w2_ref[0])


def _s5_scan(xg, lag_factors, w1, w2, sc, n_seg):
    g, rows, kw = xg.shape
    sw = sc.shape[2]
    blk = lambda a: pl.BlockSpec((1,) + a.shape[1:], lambda i: (i, 0, 0))
    return pl.pallas_call(
        functools.partial(_s5_kernel, n_seg=n_seg),
        grid=(g,),
        in_specs=[blk(xg)] + [blk(a) for a in lag_factors] + [blk(w1), blk(w2), blk(sc)],
        out_specs=pl.BlockSpec((1, rows, kw), lambda i: (i, 0, 0)),
        out_shape=jax.ShapeDtypeStruct((g, rows, kw), F32),
        scratch_shapes=[pltpu.VMEM((rows, 4 * sw), F32), pltpu.VMEM((rows, 2 * sw), F32),
                        pltpu.VMEM((kw, kw), BF16)],
        compiler_params=pltpu.CompilerParams(dimension_semantics=("parallel",),
                                             vmem_limit_bytes=VMEM_LIMIT),
        name="s5_scan",
    )(xg, *lag_factors, w1, w2, sc)


S5_NM = 16


def _block_transpose8(v, width):
    lane = lax.broadcasted_iota(jnp.int32, v[0].shape, 1)
    for d in (4, 2, 1):
        w = width * d
        hi = ((lane // w) % 2) == 1
        out = list(v)
        for i0 in range(8):
            if i0 & d:
                continue
            i1 = i0 + d
            out[i0] = jnp.where(hi, pltpu.roll(v[i1], w, 1), v[i0])
            out[i1] = jnp.where(hi, v[i1], pltpu.roll(v[i0], 8 * width - w, 1))
        v = out
    return v


def _tile_copies(hbm4, tile, buf, slot, sem, nm, to_hbm):
    copies = []
    for c in range(SUBLANES):
        for j in range(S5_LC):
            h = hbm4.at[c, pl.ds(tile * nm, nm), pl.ds(j, 1), :]
            v = buf.at[slot, j, :, pl.ds(c, 1), :]
            copies.append(pltpu.make_async_copy(v, h, sem.at[slot]) if to_hbm
                          else pltpu.make_async_copy(h, v, sem.at[slot]))
    return copies


def _s5_inproj_kernel(x4_ref, gmix_ref, w_ref, xg_ref, xs, sem, *, nm):
    i = pl.program_id(0)
    n = pl.num_programs(0)
    slot = i % 2
    dm = x4_ref.shape[3]

    @pl.when(i == 0)
    def _():
        for cp in _tile_copies(x4_ref, 0, xs, 0, sem, nm, False):
            cp.start()

    @pl.when(i + 1 < n)
    def _():
        for cp in _tile_copies(x4_ref, i + 1, xs, 1 - slot, sem, nm, False):
            cp.start()

    pltpu.make_async_copy(xs.at[slot], xs.at[slot], sem.at[slot]).wait()
    rows = nm * SUBLANES
    x = xs[slot].reshape(S5_LC * rows, dm)
    z = _dot(_rms(x, gmix_ref[...]).astype(BF16), w_ref[...])
    n_oct = z.shape[1] // LANES
    for q in range(n_oct):
        for a in range(S5_LC // 8):
            blocks = [z[(8 * a + j8) * rows:(8 * a + j8 + 1) * rows, q * LANES:(q + 1) * LANES] for j8 in range(8)]
            for g8, b in enumerate(_block_transpose8(blocks, S5_GROUP)):
                xg_ref[8 * q + g8, :, a * LANES:(a + 1) * LANES] = b.astype(BF16)


def _s5_inproj(x, gmix, w_s5_bf, n_seg):
    b, l, dm = x.shape
    steps = l // (S5_LC * n_seg)
    nm = S5_NM
    s5w = w_s5_bf.shape[1]
    g = s5w // S5_GROUP
    x4 = x.reshape(b * n_seg, steps, S5_LC, dm)
    return pl.pallas_call(
        functools.partial(_s5_inproj_kernel, nm=nm),
        grid=(steps // nm,),
        in_specs=[pl.BlockSpec(memory_space=pl.ANY),
                  pl.BlockSpec((1, dm), lambda i: (0, 0)),
                  pl.BlockSpec((dm, s5w), lambda i: (0, 0))],
        out_specs=pl.BlockSpec((g, nm * SUBLANES, S5_LC * S5_GROUP), lambda i: (0, i, 0)),
        out_shape=jax.ShapeDtypeStruct((g, steps * SUBLANES, S5_LC * S5_GROUP), BF16),
        scratch_shapes=[pltpu.VMEM((2, S5_LC, nm, SUBLANES, dm), F32), pltpu.SemaphoreType.DMA((2,))],
        compiler_params=pltpu.CompilerParams(dimension_semantics=("arbitrary",),
                                             vmem_limit_bytes=VMEM_LIMIT),
        name="s5_inproj",
    )(x4, gmix, w_s5_bf)


def _s5_to_tokens_kernel(yg_ref, ys4_ref, zs, sem, *, nm):
    i = pl.program_id(0)
    n = pl.num_programs(0)
    slot = i % 2
    rows = nm * SUBLANES

    def wait(s):
        pltpu.make_async_copy(zs.at[s], zs.at[s], sem.at[s]).wait()

    @pl.when(i >= 2)
    def _():
        wait(slot)

    n_oct = yg_ref.shape[0] // 8
    for q in range(n_oct):
        for a in range(S5_LC // 8):
            blocks = [yg_ref[8 * q + g8, :, a * LANES:(a + 1) * LANES] for g8 in range(8)]
            for j8, b in enumerate(_block_transpose8(blocks, S5_GROUP)):
                zs[slot, 8 * a + j8, :, :, q * LANES:(q + 1) * LANES] = b.reshape(nm, SUBLANES, LANES)
    for cp in _tile_copies(ys4_ref, i, zs, slot, sem, nm, True):
        cp.start()

    @pl.when(i == n - 1)
    def _():
        wait(1 - slot)
        wait(slot)


def _s5_to_tokens(yg, b, l, n_seg):
    g, rows_total, kw = yg.shape
    steps = rows_total // SUBLANES
    nm = S5_NM
    s5w = g * S5_GROUP
    assert steps // nm >= 2
    ys4 = pl.pallas_call(
        functools.partial(_s5_to_tokens_kernel, nm=nm),
        grid=(steps // nm,),
        in_specs=[pl.BlockSpec((g, nm * SUBLANES, kw), lambda i: (0, i, 0))],
        out_specs=pl.BlockSpec(memory_space=pl.ANY),
        out_shape=jax.ShapeDtypeStruct((b * n_seg, steps, S5_LC, s5w), F32),
        scratch_shapes=[pltpu.VMEM((2, S5_LC, nm, SUBLANES, s5w), F32), pltpu.SemaphoreType.DMA((2,))],
        compiler_params=pltpu.CompilerParams(dimension_semantics=("arbitrary",),
                                             vmem_limit_bytes=VMEM_LIMIT),
        name="s5_to_tokens",
    )(yg)
    return ys4.reshape(b * l, s5w)


R_E1, R_E2, R_W1, R_W2, R_RANK1, R_RANK2 = range(6)


def _mix_route_kernel(ys_ref, us5_ref, mgm_ref, x_ref, d_ref, gluw_ref, glub_ref, gs5_ref,
                      wout_ref, gffn_ref, rwh_ref, rwl_ref, rb_ref, tri_ref,
                      x2_ref, t_ref, route_ref, cnt_ref):
    for k in range(x_ref.shape[0] // TM_MIX):
        rows = slice(k * TM_MIX, (k + 1) * TM_MIX)
        _mix_route_tile(ys_ref[rows, :], us5_ref[rows, :], mgm_ref[rows, :], x_ref[rows, :], d_ref, gluw_ref,
                        glub_ref, gs5_ref, wout_ref, gffn_ref, rwh_ref, rwl_ref, rb_ref, tri_ref,
                        x2_ref.at[rows, :], t_ref.at[rows, :], route_ref.at[rows, :], cnt_ref.at[k])


def _mix_route_tile(ys, us5, mgm, x, d_ref, gluw_ref, glub_ref, gs5_ref,
                    wout_ref, gffn_ref, rwh_ref, rwl_ref, rb_ref, tri_ref,
                    x2_ref, t_ref, route_ref, cnt_ref):
    gw = mgm.shape[1]
    y = ys + d_ref[...] * us5
    g = _gelu(y)
    z = g * jax.nn.sigmoid(_dot(g.astype(BF16), gluw_ref[...]) + glub_ref[...])
    ms5 = _rms(z, gs5_ref[...]).astype(BF16)
    mix = _dot(mgm, wout_ref[:gw, :]) + _dot(ms5, wout_ref[gw:, :])
    x2 = x + mix
    x2_ref[...] = x2
    t = _rms(x2, gffn_ref[...])
    t_hi = t.astype(BF16)
    t_ref[...] = t_hi
    t_lo = (t - t_hi.astype(F32)).astype(BF16)
    logits = (_dot(t_hi, rwh_ref[...]) + _dot(t_hi, rwl_ref[...]) + _dot(t_lo, rwh_ref[...])
              + rb_ref[...])
    lane = lax.broadcasted_iota(jnp.int32, logits.shape, 1).astype(F32)
    neg = jnp.float32(-jnp.inf)

    def first_max(mask):
        vals = jnp.where(mask, logits, neg)
        mx = jnp.max(vals, axis=-1, keepdims=True)
        idx = jnp.min(jnp.where(mask & (vals == mx), lane, float(LANES)), axis=-1, keepdims=True)
        return mx, idx

    coarse = lane < N_COARSE
    m1, grp = first_max(coarse)
    p_grp = 1.0 / jnp.sum(jnp.where(coarse, jnp.exp(logits - m1), 0.0), axis=-1, keepdims=True)
    lo = N_COARSE + grp * N_FINE
    fine = (lane >= lo) & (lane < lo + N_FINE)
    v1, i1 = first_max(fine)
    v2, i2 = first_max(fine & (lane != i1))
    e21 = jnp.exp(v2 - v1)
    w1 = p_grp / (1.0 + e21)
    w2 = p_grp * e21 / (1.0 + e21)
    e1 = i1 - N_COARSE
    e2 = i2 - N_COARSE
    hit1 = lane == e1
    hit2 = lane == e2
    onehot = jnp.where(hit1 | hit2, 1.0, 0.0)
    before = _dot(tri_ref[...], onehot.astype(BF16))
    rank1 = jnp.sum(jnp.where(hit1, before, 0.0), axis=-1, keepdims=True)
    rank2 = jnp.sum(jnp.where(hit2, before, 0.0), axis=-1, keepdims=True)
    tm = onehot.shape[0]
    cnt_ref[...] = before[tm - 1:tm, :] + onehot[tm - 1:tm, :]
    rec = jnp.zeros_like(logits)
    for slot, val in ((R_E1, e1), (R_E2, e2), (R_W1, w1), (R_W2, w2),
                      (R_RANK1, rank1), (R_RANK2, rank2)):
        rec = jnp.where(lane == slot, val, rec)
    route_ref[...] = rec


def _mix_route(ys, us5, mgm, x2d, d, gluw_bf, glub, gs5, wout_bf, gffn, rwh, rwl, rb):
    t, dm = x2d.shape
    gw = mgm.shape[1]
    s5w = us5.shape[1]
    tm = TM_MIX * MIX_TILES
    tri = jnp.tril(jnp.ones((TM_MIX, TM_MIX), F32), -1).astype(BF16)
    const = lambda *shape: pl.BlockSpec(shape, lambda i: (0,) * len(shape))
    tile = lambda w: pl.BlockSpec((tm, w), lambda i: (i, 0))
    return pl.pallas_call(
        _mix_route_kernel,
        grid=(t // tm,),
        in_specs=[tile(s5w), tile(s5w), tile(gw), tile(dm),
                  const(1, s5w), const(s5w, s5w), const(1, s5w), const(1, s5w),
                  const(gw + s5w, dm), const(1, dm), const(dm, LANES), const(dm, LANES), const(1, LANES),
                  const(TM_MIX, TM_MIX)],
        out_specs=[tile(dm), tile(dm), tile(LANES), pl.BlockSpec((MIX_TILES, 1, LANES), lambda i: (i, 0, 0))],
        out_shape=[jax.ShapeDtypeStruct((t, dm), F32),
                   jax.ShapeDtypeStruct((t, dm), BF16),
                   jax.ShapeDtypeStruct((t, LANES), F32),
                   jax.ShapeDtypeStruct((t // TM_MIX, 1, LANES), F32)],
        compiler_params=pltpu.CompilerParams(dimension_semantics=("parallel",),
                                             vmem_limit_bytes=VMEM_LIMIT),
        name="mix_route",
    )(ys, us5, mgm, x2d, d, gluw_bf, glub, gs5, wout_bf, gffn, rwh, rwl, rb, tri)


def _local_rows(tm):
    worst = 2 * tm + N_EXPERTS * (SEG_ALIGN - 1)
    return -(-worst // LANES) * LANES


def _segment_plan(cnt, t, tm_expert):
    c = cnt[:, 0, :N_EXPERTS].astype(jnp.int32)
    n_tok_tiles = c.shape[0]
    al = (c + SEG_ALIGN - 1) // SEG_ALIGN * SEG_ALIGN
    lbase = jnp.cumsum(al, axis=1) - al
    tot = jnp.sum(al, axis=0)
    tot_pad = (tot + tm_expert - 1) // tm_expert * tm_expert
    gbase = jnp.cumsum(tot_pad) - tot_pad
    gpos = gbase[None, :] + jnp.cumsum(al, axis=0) - al
    n_tiles_max = -(-(2 * t + n_tok_tiles * N_EXPERTS * (SEG_ALIGN - 1)) // tm_expert) + N_EXPERTS
    tile_end = jnp.cumsum(tot_pad // tm_expert)
    n_tiles = tile_end[-1:].astype(jnp.int32)
    tile_idx = jnp.arange(n_tiles_max, dtype=jnp.int32)
    tile_expert = jnp.sum((tile_idx[:, None] >= tile_end[None, :]).astype(jnp.int32), axis=1)
    last = jnp.sum((n_tiles - 1 >= tile_end).astype(jnp.int32))
    tile_expert = jnp.where(tile_idx < n_tiles, tile_expert, last).astype(jnp.int32)
    ids = jnp.arange(N_EXPERTS, dtype=jnp.int32)
    later_used = (ids[None, :] > ids[:, None]) & (tot_pad[None, :] > 0)
    next_expert = jnp.min(jnp.where(later_used, ids[None, :], N_EXPERTS), axis=1)
    next_expert = jnp.where(next_expert == N_EXPERTS, ids, next_expert).astype(jnp.int32)
    lbase_f = jnp.pad(lbase.astype(F32), ((0, 0), (0, LANES - N_EXPERTS)))[:, None, :]
    flat = lambda a: a.reshape(-1).astype(jnp.int32)
    nch = al // SEG_ALIGN
    cum = jnp.cumsum(nch, axis=1)
    q = jnp.arange(_local_rows(TM_MIX) // SEG_ALIGN, dtype=jnp.int32)[None, :, None]
    seg_of_q = jnp.sum((q >= cum[:, None, :]).astype(jnp.int32), axis=2)
    in_seg = seg_of_q[:, :, None] == jnp.arange(N_EXPERTS, dtype=jnp.int32)[None, None, :]
    pick = lambda a: jnp.sum(jnp.where(in_seg, a[:, None, :], 0), axis=2)
    dst = pick(gpos) + (q[:, :, 0] - pick(cum - nch)) * SEG_ALIGN
    plan = dict(dst=flat(dst), n_chunks=flat(cum[:, -1]),
                tail_pos=flat(gbase + tot), tail_n=flat((tot_pad - tot) // SEG_ALIGN))
    return plan, lbase_f, (tile_expert, n_tiles, next_expert), n_tiles_max * tm_expert


def _local_positions(route, lbase):
    lane = lax.broadcasted_iota(jnp.int32, route.shape, 1).astype(F32)
    out = []
    for e_lane, r_lane in ((R_E1, R_RANK1), (R_E2, R_RANK2)):
        e = route[:, e_lane:e_lane + 1]
        base = jnp.sum(jnp.where(lane == e, lbase, 0.0), axis=-1, keepdims=True)
        out.append(base + route[:, r_lane:r_lane + 1])
    return out


WAIT_GROUP = 8
ISSUE_GROUP = 4


def _segment_copies(i, dst_ref, nq_ref, local, glob, sem, to_global):
    per_tile = local.shape[0] // SEG_ALIGN
    n = nq_ref[i]

    def start(q):
        lo = local.at[pl.ds(pl.multiple_of(q * SEG_ALIGN, SEG_ALIGN), SEG_ALIGN)]
        gl = glob.at[pl.ds(pl.multiple_of(dst_ref[i * per_tile + q], SEG_ALIGN), SEG_ALIGN)]
        (pltpu.make_async_copy(lo, gl, sem) if to_global else pltpu.make_async_copy(gl, lo, sem)).start()

    def group(k, carry):
        for u in range(ISSUE_GROUP):
            start(k * ISSUE_GROUP + u)
        return carry

    def single(q, carry):
        start(q)
        return carry

    full = n // ISSUE_GROUP
    lax.fori_loop(0, full, group, 0)
    lax.fori_loop(full * ISSUE_GROUP, n, single, 0)
    return n


def _wait_chunks(n, local, glob, sem):
    def wait_rows(rows):
        def one(c, carry):
            pltpu.make_async_copy(local.at[pl.ds(0, rows)], glob.at[pl.ds(0, rows)], sem).wait()
            return carry
        return one

    lax.fori_loop(0, n // WAIT_GROUP, wait_rows(WAIT_GROUP * SEG_ALIGN), 0)
    lax.fori_loop(0, n % WAIT_GROUP, wait_rows(SEG_ALIGN), 0)


def _sort_rows_kernel(dst_ref, nq_ref, tpos_ref, tn_ref, nt_ref, t_ref, route_ref, lbase_ref,
                      xs_ref, local_scr, zero_scr, cnt_scr, sem, zsem):
    i = pl.program_id(0)
    n = pl.num_programs(0)
    slot = i % 2
    tm = t_ref.shape[0]
    s_rows = local_scr.shape[1]
    local = local_scr.at[slot]

    @pl.when(i >= 2)
    def _():
        _wait_chunks(cnt_scr[slot], local, xs_ref, sem.at[slot])

    lp1, lp2 = _local_positions(route_ref[...], lbase_ref[0])
    lane = lax.broadcasted_iota(jnp.int32, (tm, LANES), 1)
    lp_rows = jnp.where(lane == 0, lp1, jnp.where(lane == 1, lp2, -1.0)).T
    row = lax.broadcasted_iota(jnp.int32, (s_rows, tm), 0).astype(F32)
    onehot = jnp.where((row == lp_rows[0:1, :]) | (row == lp_rows[1:2, :]), 1.0, 0.0).astype(BF16)
    local_scr[slot] = _dot(onehot, t_ref[...]).astype(BF16)
    cnt_scr[slot] = _segment_copies(i, dst_ref, nq_ref, local, xs_ref, sem.at[slot], True)

    @pl.when(i == n - 1)
    def _():
        @pl.when(n >= 2)
        def _():
            _wait_chunks(cnt_scr[1 - slot], local_scr.at[1 - slot], xs_ref, sem.at[1 - slot])

        _wait_chunks(cnt_scr[slot], local, xs_ref, sem.at[slot])
        zero_scr[...] = jnp.zeros_like(zero_scr)
        te = zero_scr.shape[0]
        zero_chunk = zero_scr.at[pl.ds(0, SEG_ALIGN)]

        def tail(e, total):
            def chunk(c, carry):
                dst = xs_ref.at[pl.ds(pl.multiple_of(tpos_ref[e] + c * SEG_ALIGN, SEG_ALIGN), SEG_ALIGN)]
                pltpu.make_async_copy(zero_chunk, dst, zsem).start()
                return carry

            lax.fori_loop(0, tn_ref[e], chunk, 0)
            return total + tn_ref[e]

        _wait_chunks(lax.fori_loop(0, N_EXPERTS, tail, 0), zero_scr, xs_ref, zsem)

        def unused_tile(j, carry):
            pltpu.make_async_copy(zero_scr, xs_ref.at[pl.ds(pl.multiple_of(j * te, te), te)], zsem).start()
            return carry

        def unused_wait(j, carry):
            pltpu.make_async_copy(zero_scr, xs_ref.at[pl.ds(0, te)], zsem).wait()
            return carry

        lax.fori_loop(nt_ref[0], xs_ref.shape[0] // te, unused_tile, 0)
        lax.fori_loop(nt_ref[0], xs_ref.shape[0] // te, unused_wait, 0)


def _plan_specs(plan):
    keys = ('dst', 'n_chunks', 'tail_pos', 'tail_n')
    return [plan[k] for k in keys]


def _sort_rows(plan, n_tiles, t_bf, route, lbase_f, n_sorted):
    t, dm = t_bf.shape
    tm = TM_MIX
    s_rows = _local_rows(tm)
    im = lambda i, *_: (i, 0)
    return pl.pallas_call(
        _sort_rows_kernel,
        grid_spec=pltpu.PrefetchScalarGridSpec(
            num_scalar_prefetch=5,
            grid=(t // tm,),
            in_specs=[pl.BlockSpec((tm, dm), im), pl.BlockSpec((tm, LANES), im),
                      pl.BlockSpec((1, 1, LANES), lambda i, *_: (i, 0, 0))],
            out_specs=pl.BlockSpec(memory_space=pl.ANY),
            scratch_shapes=[pltpu.VMEM((2, s_rows, dm), BF16), pltpu.VMEM((TM_EXPERT, dm), BF16),
                            pltpu.SMEM((2,), jnp.int32), pltpu.SemaphoreType.DMA((2,)),
                            pltpu.SemaphoreType.DMA(())],
        ),
        out_shape=jax.ShapeDtypeStruct((n_sorted, dm), BF16),
        compiler_params=pltpu.CompilerParams(dimension_semantics=("arbitrary",),
                                             vmem_limit_bytes=VMEM_LIMIT),
        name="sort_rows",
    )(*_plan_specs(plan), n_tiles, t_bf, route, lbase_f)


def _expert_weight_copies(e, slot, hbm, stage, sem):
    return [pltpu.make_async_copy(h.at[e], s.at[slot], sem.at[slot]) for h, s in zip(hbm, stage)]


def _experts_kernel(te_ref, nt_ref, nxt_ref, xs_ref, wg_ref, wu_ref, wd_ref, ys_ref,
                    sg, su, sd, wg_bf, wu_bf, wd_bf, slot_scr, sem):
    i = pl.program_id(0)
    e = te_ref[i]
    hbm, stage = (wg_ref, wu_ref, wd_ref), (sg, su, sd)

    @pl.when(i == 0)
    def _():
        slot_scr[0] = 0
        for cp in _expert_weight_copies(e, 0, hbm, stage, sem):
            cp.start()

    @pl.when((i == 0) | (e != te_ref[jnp.maximum(i - 1, 0)]))
    def _():
        slot = slot_scr[0]
        for cp in _expert_weight_copies(e, slot, hbm, stage, sem):
            cp.wait()
        wg_bf[...] = sg[slot].astype(BF16)
        wu_bf[...] = su[slot].astype(BF16)
        wd_bf[...] = sd[slot].astype(BF16)
        nxt = nxt_ref[e]

        @pl.when(nxt != e)
        def _():
            for cp in _expert_weight_copies(nxt, 1 - slot, hbm, stage, sem):
                cp.start()

        slot_scr[0] = 1 - slot

    @pl.when(i < nt_ref[0])
    def _():
        x = xs_ref[...]
        hidden = (jax.nn.silu(_dot(x, wg_bf[...])) * _dot(x, wu_bf[...])).astype(BF16)
        ys_ref[...] = _dot(hidden, wd_bf[...]).astype(BF16)

    @pl.when(i >= nt_ref[0])
    def _():
        ys_ref[...] = jnp.zeros_like(ys_ref)


def _experts(tile_expert, n_tiles, next_expert, x_sorted, w_gate, w_up, w_down):
    n_sorted, dm = x_sorted.shape
    de = w_gate.shape[2]
    tm = TM_EXPERT
    return pl.pallas_call(
        _experts_kernel,
        grid_spec=pltpu.PrefetchScalarGridSpec(
            num_scalar_prefetch=3,
            grid=(n_sorted // tm,),
            in_specs=[pl.BlockSpec((tm, dm), lambda i, te, nt, nx: (jnp.minimum(i, nt[0] - 1), 0)),
                      pl.BlockSpec(memory_space=pl.ANY), pl.BlockSpec(memory_space=pl.ANY),
                      pl.BlockSpec(memory_space=pl.ANY)],
            out_specs=pl.BlockSpec((tm, dm), lambda i, te, nt, nx: (i, 0)),
            scratch_shapes=[pltpu.VMEM((2, dm, de), F32), pltpu.VMEM((2, dm, de), F32), pltpu.VMEM((2, de, dm), F32),
                            pltpu.VMEM((dm, de), BF16), pltpu.VMEM((dm, de), BF16), pltpu.VMEM((de, dm), BF16),
                            pltpu.SMEM((1,), jnp.int32), pltpu.SemaphoreType.DMA((2,))],
        ),
        out_shape=jax.ShapeDtypeStruct((n_sorted, dm), BF16),
        compiler_params=pltpu.CompilerParams(dimension_semantics=("arbitrary",),
                                             vmem_limit_bytes=VMEM_LIMIT),
        name="experts",
    )(tile_expert, n_tiles, next_expert, x_sorted, w_gate, w_up, w_down)


def _combine_kernel(dst_ref, nq_ref, tpos_ref, tn_ref, x2_ref, route_ref, lbase_ref, gfin_ref, ys_ref,
                    o_ref, local_scr, cnt_scr, sem):
    del tpos_ref, tn_ref
    i = pl.program_id(0)
    n = pl.num_programs(0)
    slot = i % 2
    tm = x2_ref.shape[0]
    s_rows = local_scr.shape[1]

    def fetch(tile, s):
        def clear(r, carry):
            local_scr[s, pl.ds(pl.multiple_of(r * SEG_ALIGN, SEG_ALIGN), SEG_ALIGN), :] = jnp.zeros(
                (SEG_ALIGN, local_scr.shape[2]), BF16)
            return carry

        lax.fori_loop(nq_ref[tile], s_rows // SEG_ALIGN, clear, 0)
        cnt_scr[s] = _segment_copies(tile, dst_ref, nq_ref, local_scr.at[s], ys_ref, sem.at[s], False)

    @pl.when(i == 0)
    def _():
        fetch(0, 0)

    @pl.when(i + 1 < n)
    def _():
        fetch(i + 1, 1 - slot)

    route = route_ref[...]
    lp1, lp2 = _local_positions(route, lbase_ref[0])
    _wait_chunks(cnt_scr[slot], local_scr.at[slot], ys_ref, sem.at[slot])
    w1, w2 = route[:, R_W1:R_W1 + 1], route[:, R_W2:R_W2 + 1]
    col0 = lax.broadcasted_iota(jnp.int32, (tm, KB), 1).astype(F32)
    moe = None
    for k in range(s_rows // KB):
        col = col0 + float(k * KB)
        pick = (jnp.where(col == lp1, w1, 0.0) + jnp.where(col == lp2, w2, 0.0)).astype(BF16)
        part = _dot(pick, local_scr[slot, k * KB:(k + 1) * KB, :])
        moe = part if moe is None else moe + part
    o_ref[...] = _rms(x2_ref[...] + moe, gfin_ref[...])


def _combine(plan, x2, route, lbase_f, gfin, y_sorted):
    t, dm = x2.shape
    tm = TM_MIX
    s_rows = _local_rows(tm)
    im = lambda i, *_: (i, 0)
    return pl.pallas_call(
        _combine_kernel,
        grid_spec=pltpu.PrefetchScalarGridSpec(
            num_scalar_prefetch=4,
            grid=(t // tm,),
            in_specs=[pl.BlockSpec((tm, dm), im), pl.BlockSpec((tm, LANES), im),
                      pl.BlockSpec((1, 1, LANES), lambda i, *_: (i, 0, 0)),
                      pl.BlockSpec((1, dm), lambda i, *_: (0, 0)),
                      pl.BlockSpec(memory_space=pl.ANY)],
            out_specs=pl.BlockSpec((tm, dm), im),
            scratch_shapes=[pltpu.VMEM((2, s_rows, dm), BF16), pltpu.SMEM((2,), jnp.int32),
                            pltpu.SemaphoreType.DMA((2,))],
        ),
        out_shape=jax.ShapeDtypeStruct((t, dm), F32),
        compiler_params=pltpu.CompilerParams(dimension_semantics=("arbitrary",),
                                             vmem_limit_bytes=VMEM_LIMIT),
        name="combine_norm",
    )(*_plan_specs(plan), x2, route, lbase_f, gfin, y_sorted)


def _layer(x, p, s5_ops, gfin):
    b, l, dm = x.shape
    x2d = x.reshape(b * l, dm)
    mgm, us5 = _inproj_gmlp(x2d, p['gmix'], p['win'], p['lng'], p['lnb'], p['ws'], p['bs'], p['gout_gm'])
    n_seg = SUBLANES // b
    *lag_factors, w1, w2, sc = s5_ops[(l // (S5_LC * n_seg))]
    xg = _s5_inproj(x, p['gmix'], p['win_s5'], n_seg)
    yg = _s5_scan(xg, lag_factors, w1, w2, sc, n_seg)
    ys = _s5_to_tokens(yg, b, l, n_seg)
    x2, t_bf, route, counts = _mix_route(ys, us5, mgm, x2d, p['d'], p['gluw'], p['glub'], p['gout_s5'],
                                         p['wout'], p['gffn'], p['rwh'], p['rwl'], p['rb'])
    plan, lbase_f, tiles, n_sorted = _segment_plan(counts, b * l, TM_EXPERT)
    x_sorted = _sort_rows(plan, tiles[1], t_bf, route, lbase_f, n_sorted)
    y_sorted = _experts(*tiles, x_sorted, p['w_gate'], p['w_up'], p['w_down'])
    out = _combine(plan, x2, route, lbase_f, gfin, y_sorted)
    return out.reshape(b, l, dm)


def kernel(x_prompt, x_sample, norm_mix_g, w_in, gm_ln_g, gm_ln_b, gm_ws, gm_bs, s5_lam_re_fwd, s5_lam_im_fwd, s5_log_step_fwd, s5_b_re_fwd, s5_b_im_fwd, s5_c_re_fwd, s5_c_im_fwd, s5_lam_re_bwd, s5_lam_im_bwd, s5_log_step_bwd, s5_b_re_bwd, s5_b_im_bwd, s5_c_re_bwd, s5_c_im_bwd, s5_d, s5_glu_w, s5_glu_b, out_norm_gm, out_norm_s5, w_out, norm_ffn_g, r1_w, r1_b, r2_w, r2_b, e_w_gate, e_w_up, e_w_down, norm_final_g):
    depth = w_in.shape[0]
    gfin = norm_final_g.reshape(1, -1).astype(F32)
    xs = [x_prompt, x_sample]
    for li in range(depth):
        row = lambda a: a[li].reshape(1, -1).astype(F32)
        dm = w_in.shape[1]
        gw = gm_ln_g.shape[1]
        hd_dim = gw // GM_HEADS
        rw = jnp.concatenate([r1_w[li], r2_w[li].transpose(1, 0, 2).reshape(dm, N_EXPERTS)], axis=1).astype(F32)
        rw = jnp.pad(rw, ((0, 0), (0, LANES - rw.shape[1])))
        rwh = rw.astype(BF16)
        rwl = (rw - rwh.astype(F32)).astype(BF16)
        rb = jnp.concatenate([r1_b[li], r2_b[li].reshape(-1)]).astype(F32)
        rb = jnp.pad(rb, (0, LANES - rb.shape[0])).reshape(1, LANES)
        p = dict(
            gmix=row(norm_mix_g), win=w_in[li].astype(BF16), win_s5=w_in[li][:, 2 * gw:].astype(BF16),
            lng=row(gm_ln_g), lnb=row(gm_ln_b),
            ws=gm_ws[li].astype(BF16),
            bs=jnp.broadcast_to(gm_bs[li].astype(F32)[:, :, None], (GM_HEADS, CHUNK, hd_dim)),
            gout_gm=row(out_norm_gm), d=row(s5_d), gluw=s5_glu_w[li].astype(BF16), glub=row(s5_glu_b),
            gout_s5=row(out_norm_s5), wout=w_out[li].astype(BF16), gffn=row(norm_ffn_g),
            rwh=rwh, rwl=rwl, rb=rb,
            w_gate=e_w_gate[li], w_up=e_w_up[li], w_down=e_w_down[li],
        )
        fwd = (s5_lam_re_fwd[li], s5_lam_im_fwd[li], s5_log_step_fwd[li], s5_b_re_fwd[li], s5_b_im_fwd[li],
               s5_c_re_fwd[li], s5_c_im_fwd[li])
        bwd = (s5_lam_re_bwd[li], s5_lam_im_bwd[li], s5_log_step_bwd[li], s5_b_re_bwd[li], s5_b_im_bwd[li],
               s5_c_re_bwd[li], s5_c_im_bwd[li])
        s5_ops = {}
        for x in xs:
            seg_steps = x.shape[1] // (S5_LC * (SUBLANES // x.shape[0]))
            if seg_steps not in s5_ops:
                s5_ops[seg_steps] = _s5_operator(fwd, bwd, S5_LC, seg_steps)
        last = li == depth - 1
        assert last, "depth > 1 needs an un-normalised layer output"
        xs = [_layer(x, p, s5_ops, gfin) for x in xs]
    return tuple(xs)
```

```python
import functools
import math

import jax
import jax.numpy as jnp
from jax import lax
from jax.experimental import pallas as pl
from jax.experimental.pallas import tpu as pltpu

F32 = jnp.float32
BF16 = jnp.bfloat16

EPS = 1e-6
LAMBDA_RE_MAX = -1e-4
GM_HEADS = 4
CHUNK = 128
S5_GROUP = 16
S5_STATE = 64
N_COARSE = 4
N_FINE = 8
N_EXPERTS = N_COARSE * N_FINE

LANES = 128
SUBLANES = 8
S5_LC = 16
VMEM_LIMIT = 56 * 1024 * 1024

TM_PROJ = 1024
TM_MIX = 512
MIX_TILES = 2
KB = 256
TM_EXPERT = 512
SEG_ALIGN = 16


def _gelu(x):
    c = math.sqrt(2.0 / math.pi)
    return x * (0.5 * (1.0 + jnp.tanh(c * (x + 0.044715 * (x * x * x)))))


def _rms(x, g):
    ms = jnp.mean(x * x, axis=-1, keepdims=True)
    return x * lax.rsqrt(ms + EPS) * g


def _dot(a, b):
    return jnp.dot(a, b, preferred_element_type=F32)


def _inproj_gmlp_kernel(x_ref, gmix_ref, win_ref, lng_ref, lnb_ref, ws_ref, bs_ref, gout_ref,
                        mgm_ref, us5_ref, y_scr):
    tm = x_ref.shape[0]
    gw = mgm_ref.shape[1]
    hd_dim = gw // GM_HEADS
    n_chunks = tm // CHUNK
    h = _rms(x_ref[...], gmix_ref[...]).astype(BF16)
    proj = _dot(h, win_ref[...])
    us5_ref[...] = proj[:, 2 * gw:]
    u = _gelu(proj[:, :gw])
    v = _gelu(proj[:, gw:2 * gw])
    for hd in range(GM_HEADS):
        lo = hd * hd_dim
        vh = v[:, lo:lo + hd_dim]
        mu = jnp.mean(vh, axis=-1, keepdims=True)
        xc = vh - mu
        var = jnp.mean(xc * xc, axis=-1, keepdims=True)
        vn = (xc * lax.rsqrt(var + EPS) * lng_ref[:, lo:lo + hd_dim]
              + lnb_ref[:, lo:lo + hd_dim]).astype(BF16)
        rhs = jnp.concatenate([vn[c * CHUNK:(c + 1) * CHUNK] for c in range(n_chunks)], axis=1)
        s = _dot(ws_ref[hd], rhs)
        for c in range(n_chunks):
            sc = s[:, c * hd_dim:(c + 1) * hd_dim] + bs_ref[hd]
            y_scr[c * CHUNK:(c + 1) * CHUNK, lo:lo + hd_dim] = u[c * CHUNK:(c + 1) * CHUNK, lo:lo + hd_dim] * sc
    mgm_ref[...] = _rms(y_scr[...], gout_ref[...]).astype(BF16)


def _inproj_gmlp(x2d, gmix, win_bf, lng, lnb, ws_bf, bs_b, gout):
    t, d = x2d.shape
    d_in = win_bf.shape[1]
    gw = lng.shape[1]
    s5w = d_in - 2 * gw
    tm = TM_PROJ
    const = lambda *shape: pl.BlockSpec(shape, lambda i: (0,) * len(shape))
    return pl.pallas_call(
        _inproj_gmlp_kernel,
        grid=(t // tm,),
        in_specs=[
            pl.BlockSpec((tm, d), lambda i: (i, 0)),
            const(1, d), const(d, d_in), const(1, gw), const(1, gw),
            const(GM_HEADS, CHUNK, CHUNK), const(GM_HEADS, CHUNK, gw // GM_HEADS), const(1, gw),
        ],
        out_specs=[pl.BlockSpec((tm, gw), lambda i: (i, 0)),
                   pl.BlockSpec((tm, s5w), lambda i: (i, 0))],
        out_shape=[jax.ShapeDtypeStruct((t, gw), BF16),
                   jax.ShapeDtypeStruct((t, s5w), F32)],
        scratch_shapes=[pltpu.VMEM((tm, gw), F32)],
        compiler_params=pltpu.CompilerParams(dimension_semantics=("parallel",),
                                             vmem_limit_bytes=VMEM_LIMIT),
        name="inproj_gmlp",
    )(x2d, gmix, win_bf, lng, lnb, ws_bf, bs_b, gout)


def _s5_consts(lam_re, lam_im, log_step, b_re, b_im, c_re, c_im, lc):
    lr = jnp.minimum(lam_re.astype(F32), LAMBDA_RE_MAX)
    li = lam_im.astype(F32)
    step = jnp.exp(log_step.astype(F32))[:, None]
    dr, di = lr * step, li * step
    ar, ai = _cexp(dr, di)
    nr, ni = ar - 1.0, ai
    den = lr * lr + li * li
    qr, qi = (nr * lr + ni * li) / den, (ni * lr - nr * li) / den
    br, bi = b_re.astype(F32), b_im.astype(F32)
    bbr = qr[..., None] * br - qi[..., None] * bi
    bbi = qr[..., None] * bi + qi[..., None] * br
    k = jnp.arange(lc + 1, dtype=F32)[:, None, None]
    pwr, pwi = _cexp(k * dr[None], k * di[None])
    return (dr, di), (pwr, pwi), (bbr, bbi), (c_re.astype(F32), c_im.astype(F32))


def _cexp(zr, zi):
    m = jnp.exp(zr)
    return m * jnp.cos(zi), m * jnp.sin(zi)


def _s5_operator(fwd, bwd, lc, seg_steps):
    consts = [_s5_consts(*fwd, lc), _s5_consts(*bwd, lc)]
    g, p, h = consts[0][2][0].shape
    lags, w1_parts, w2_parts, sc_rows, seg_rows = [], [], [], [], []
    for direction, (ld, pw, bb, c) in enumerate(consts):
        (dr, di), (pwr, pwi), (bbr, bbi), (cr, ci) = ld, pw, bb, c
        crt, cit = cr.transpose(0, 2, 1), ci.transpose(0, 2, 1)
        pwrt, pwit = pwr.transpose(1, 2, 0), pwi.transpose(1, 2, 0)
        cpr = crt[:, :, None, :] * pwrt[:, :, :, None] - cit[:, :, None, :] * pwit[:, :, :, None]
        cpi = crt[:, :, None, :] * pwit[:, :, :, None] + cit[:, :, None, :] * pwrt[:, :, :, None]
        ck = jnp.concatenate([cpr[:, :, :lc], cpi[:, :, :lc]], axis=1)
        if direction == 1:
            ck = jnp.flip(ck, 2)
        lags += [jnp.concatenate([bbr.transpose(0, 2, 1), -bbi.transpose(0, 2, 1)], axis=-1),
                 ck.reshape(g, 2 * p, lc * h)]
        er, ei = pwrt[:, :, :lc].transpose(0, 2, 1), pwit[:, :, :lc].transpose(0, 2, 1)
        if direction == 0:
            er, ei = jnp.flip(er, 1), jnp.flip(ei, 1)
        bbrt, bbit = bbr.transpose(0, 2, 1), bbi.transpose(0, 2, 1)
        wr = er[:, :, None, :] * bbrt[:, None] - ei[:, :, None, :] * bbit[:, None]
        wi = er[:, :, None, :] * bbit[:, None] + ei[:, :, None, :] * bbrt[:, None]
        w1_parts += [wr, wi, wi, wr]
        fr, fi = cpr[:, :, 1:lc + 1], cpi[:, :, 1:lc + 1]
        if direction == 1:
            fr, fi = jnp.flip(fr, 2), jnp.flip(fi, 2)
        w2_parts += [fr, -fi]

        def mult(zr, zi):
            return [jnp.concatenate([zr, zr], -1), jnp.concatenate([-zi, zi], -1)]

        sc_rows += mult(*_cexp(lc * dr, lc * di))
        seg_rows += mult(*_cexp((lc * seg_steps) * dr, (lc * seg_steps) * di))
    w1 = jnp.concatenate(w1_parts, axis=-1).reshape(g, lc * h, 8 * p)
    w2 = jnp.concatenate(w2_parts, axis=1).reshape(g, 4 * p, lc * h)
    sc = jnp.stack(sc_rows + seg_rows, axis=1)
    return tuple(lags) + (w1.astype(BF16), w2.astype(BF16), sc.astype(F32))


S5_GPS = 2


def _s5_kernel(x_ref, bbf_ref, cpf_ref, bbb_ref, cpb_ref, w1_ref, w2_ref, sc_ref, y_ref,
               loc_scr, sin_scr, m_scr, *, n_seg):
    gps, rows, kw = x_ref.shape
    steps = rows // SUBLANES
    sw = sc_ref.shape[2]

    for gi in range(gps):
        kf = jnp.dot(bbf_ref[gi], cpf_ref[gi], precision=lax.Precision.HIGHEST, preferred_element_type=F32)
        kb = jnp.dot(bbb_ref[gi], cpb_ref[gi], precision=lax.Precision.HIGHEST, preferred_element_type=F32)
        hch = kf.shape[0]
        lc = kw // hch
        lane = lax.broadcasted_iota(jnp.int32, kf.shape, 1)
        for s in range(lc):
            f = kf if s == 0 else jnp.where(lane >= s * hch, pltpu.roll(kf, s * hch, 1), 0.0)
            left = (lc - 1 - s) * hch
            b = kb if left == 0 else pltpu.roll(kb, kw - left, 1)
            m_scr[gi, s * hch:(s + 1) * hch, :] = (f + jnp.where(lane < (s + 1) * hch, b, 0.0)).astype(BF16)
        loc_scr[gi] = _dot(x_ref[gi], w1_ref[gi])

    def bc(gi, i):
        return jnp.broadcast_to(sc_ref[gi, i:i + 1, :], (SUBLANES, sw))

    mult = [[bc(gi, i) for i in range(8)] for gi in range(gps)]

    def step(gi, s, state):
        f, fs, b, bs = state
        a1f, a2f, a1b, a2b = mult[gi][:4]
        rf = pl.multiple_of(s * SUBLANES, SUBLANES)
        rb = pl.multiple_of((steps - 1 - s) * SUBLANES, SUBLANES)
        lf = loc_scr[gi, pl.ds(rf, SUBLANES), 0:sw]
        lfs = loc_scr[gi, pl.ds(rf, SUBLANES), sw:2 * sw]
        lb = loc_scr[gi, pl.ds(rb, SUBLANES), 2 * sw:3 * sw]
        lbs = loc_scr[gi, pl.ds(rb, SUBLANES), 3 * sw:4 * sw]
        return (a1f * f + a2f * fs + lf, a1f * fs - a2f * f + lfs,
                a1b * b + a2b * bs + lb, a1b * bs - a2b * b + lbs)

    zero = jnp.zeros((SUBLANES, sw), F32)

    def pass1(s, carry):
        return tuple(step(gi, s, carry[gi]) for gi in range(gps))

    ends = lax.fori_loop(0, steps, pass1, tuple((zero,) * 4 for _ in range(gps)), unroll=4)

    seg = lax.broadcasted_iota(jnp.int32, (SUBLANES, sw), 0) % n_seg
    enter = []
    for gi in range(gps):
        f_end, fs_end, b_end, bs_end = ends[gi]
        p1f, p2f, p1b, p2b = mult[gi][4:]
        cf, cfs, cb, cbs = zero, zero, zero, zero
        for _ in range(n_seg - 1):
            ef = f_end + p1f * cf + p2f * cfs
            efs = fs_end + p1f * cfs - p2f * cf
            eb = b_end + p1b * cb + p2b * cbs
            ebs = bs_end + p1b * cbs - p2b * cb
            cf = jnp.where(seg >= 1, pltpu.roll(ef, 1, 0), 0.0)
            cfs = jnp.where(seg >= 1, pltpu.roll(efs, 1, 0), 0.0)
            cb = jnp.where(seg <= n_seg - 2, pltpu.roll(eb, SUBLANES - 1, 0), 0.0)
            cbs = jnp.where(seg <= n_seg - 2, pltpu.roll(ebs, SUBLANES - 1, 0), 0.0)
        enter.append((cf, cfs, cb, cbs))

    def pass2(s, carry):
        rf = pl.multiple_of(s * SUBLANES, SUBLANES)
        rb = pl.multiple_of((steps - 1 - s) * SUBLANES, SUBLANES)
        for gi in range(gps):
            sin_scr[gi, pl.ds(rf, SUBLANES), 0:sw] = carry[gi][0]
            sin_scr[gi, pl.ds(rb, SUBLANES), sw:2 * sw] = carry[gi][2]
        return tuple(step(gi, s, carry[gi]) for gi in range(gps))

    lax.fori_loop(0, steps, pass2, tuple(enter), unroll=4)

    for gi in range(gps):
        y_ref[gi] = _dot(x_ref[gi], m_scr[gi]) + _dot(sin_scr[gi].astype(BF16), w2_ref[gi])


def _s5_scan(xg, lag_factors, w1, w2, sc, n_seg):
    g, rows, kw = xg.shape
    sw = sc.shape[2]
    gps = S5_GPS
    blk = lambda a: pl.BlockSpec((gps,) + a.shape[1:], lambda i: (i, 0, 0))
    return pl.pallas_call(
        functools.partial(_s5_kernel, n_seg=n_seg),
        grid=(g // gps,),
        in_specs=[blk(xg)] + [blk(a) for a in lag_factors] + [blk(w1), blk(w2), blk(sc)],
        out_specs=pl.BlockSpec((gps, rows, kw), lambda i: (i, 0, 0)),
        out_shape=jax.ShapeDtypeStruct((g, rows, kw), F32),
        scratch_shapes=[pltpu.VMEM((gps, rows, 4 * sw), F32), pltpu.VMEM((gps, rows, 2 * sw), F32),
                        pltpu.VMEM((gps, kw, kw), BF16)],
        compiler_params=pltpu.CompilerParams(dimension_semantics=("parallel",),
                                             vmem_limit_bytes=VMEM_LIMIT),
        name="s5_scan",
    )(xg, *lag_factors, w1, w2, sc)


S5_NM = 16


def _block_transpose8(v, width):
    lane = lax.broadcasted_iota(jnp.int32, v[0].shape, 1)
    for d in (4, 2, 1):
        w = width * d
        hi = ((lane // w) % 2) == 1
        out = list(v)
        for i0 in range(8):
            if i0 & d:
                continue
            i1 = i0 + d
            out[i0] = jnp.where(hi, pltpu.roll(v[i1], w, 1), v[i0])
            out[i1] = jnp.where(hi, v[i1], pltpu.roll(v[i0], 8 * width - w, 1))
        v = out
    return v


def _tile_copies(hbm4, tile, buf, slot, sem, nm, to_hbm):
    copies = []
    for c in range(SUBLANES):
        for j in range(S5_LC):
            h = hbm4.at[c, pl.ds(tile * nm, nm), pl.ds(j, 1), :]
            v = buf.at[slot, j, :, pl.ds(c, 1), :]
            copies.append(pltpu.make_async_copy(v, h, sem.at[slot]) if to_hbm
                          else pltpu.make_async_copy(h, v, sem.at[slot]))
    return copies


def _s5_inproj_kernel(x4_ref, gmix_ref, w_ref, xg_ref, xs, sem, *, nm):
    i = pl.program_id(0)
    n = pl.num_programs(0)
    slot = i % 2
    dm = x4_ref.shape[3]

    @pl.when(i == 0)
    def _():
        for cp in _tile_copies(x4_ref, 0, xs, 0, sem, nm, False):
            cp.start()

    @pl.when(i + 1 < n)
    def _():
        for cp in _tile_copies(x4_ref, i + 1, xs, 1 - slot, sem, nm, False):
            cp.start()

    pltpu.make_async_copy(xs.at[slot], xs.at[slot], sem.at[slot]).wait()
    rows = nm * SUBLANES
    x = xs[slot].reshape(S5_LC * rows, dm)
    z = _dot(_rms(x, gmix_ref[...]).astype(BF16), w_ref[...])
    n_oct = z.shape[1] // LANES
    for q in range(n_oct):
        for a in range(S5_LC // 8):
            blocks = [z[(8 * a + j8) * rows:(8 * a + j8 + 1) * rows, q * LANES:(q + 1) * LANES] for j8 in range(8)]
            for g8, b in enumerate(_block_transpose8(blocks, S5_GROUP)):
                xg_ref[8 * q + g8, :, a * LANES:(a + 1) * LANES] = b.astype(BF16)


def _s5_inproj(x, gmix, w_s5_bf, n_seg):
    b, l, dm = x.shape
    steps = l // (S5_LC * n_seg)
    nm = S5_NM
    s5w = w_s5_bf.shape[1]
    g = s5w // S5_GROUP
    x4 = x.reshape(b * n_seg, steps, S5_LC, dm)
    return pl.pallas_call(
        functools.partial(_s5_inproj_kernel, nm=nm),
        grid=(steps // nm,),
        in_specs=[pl.BlockSpec(memory_space=pl.ANY),
                  pl.BlockSpec((1, dm), lambda i: (0, 0)),
                  pl.BlockSpec((dm, s5w), lambda i: (0, 0))],
        out_specs=pl.BlockSpec((g, nm * SUBLANES, S5_LC * S5_GROUP), lambda i: (0, i, 0)),
        out_shape=jax.ShapeDtypeStruct((g, steps * SUBLANES, S5_LC * S5_GROUP), BF16),
        scratch_shapes=[pltpu.VMEM((2, S5_LC, nm, SUBLANES, dm), F32), pltpu.SemaphoreType.DMA((2,))],
        compiler_params=pltpu.CompilerParams(dimension_semantics=("arbitrary",),
                                             vmem_limit_bytes=VMEM_LIMIT),
        name="s5_inproj",
    )(x4, gmix, w_s5_bf)


def _s5_to_tokens_kernel(yg_ref, ys4_ref, zs, sem, *, nm):
    i = pl.program_id(0)
    n = pl.num_programs(0)
    slot = i % 2
    rows = nm * SUBLANES

    def wait(s):
        pltpu.make_async_copy(zs.at[s], zs.at[s], sem.at[s]).wait()

    @pl.when(i >= 2)
    def _():
        wait(slot)

    n_oct = yg_ref.shape[0] // 8
    for q in range(n_oct):
        for a in range(S5_LC // 8):
            blocks = [yg_ref[8 * q + g8, :, a * LANES:(a + 1) * LANES] for g8 in range(8)]
            for j8, b in enumerate(_block_transpose8(blocks, S5_GROUP)):
                zs[slot, 8 * a + j8, :, :, q * LANES:(q + 1) * LANES] = b.reshape(nm, SUBLANES, LANES)
    for cp in _tile_copies(ys4_ref, i, zs, slot, sem, nm, True):
        cp.start()

    @pl.when(i == n - 1)
    def _():
        wait(1 - slot)
        wait(slot)


def _s5_to_tokens(yg, b, l, n_seg):
    g, rows_total, kw = yg.shape
    steps = rows_total // SUBLANES
    nm = S5_NM
    s5w = g * S5_GROUP
    assert steps // nm >= 2
    ys4 = pl.pallas_call(
        functools.partial(_s5_to_tokens_kernel, nm=nm),
        grid=(steps // nm,),
        in_specs=[pl.BlockSpec((g, nm * SUBLANES, kw), lambda i: (0, i, 0))],
        out_specs=pl.BlockSpec(memory_space=pl.ANY),
        out_shape=jax.ShapeDtypeStruct((b * n_seg, steps, S5_LC, s5w), F32),
        scratch_shapes=[pltpu.VMEM((2, S5_LC, nm, SUBLANES, s5w), F32), pltpu.SemaphoreType.DMA((2,))],
        compiler_params=pltpu.CompilerParams(dimension_semantics=("arbitrary",),
                                             vmem_limit_bytes=VMEM_LIMIT),
        name="s5_to_tokens",
    )(yg)
    return ys4.reshape(b * l, s5w)


R_E1, R_E2, R_W1, R_W2, R_RANK1, R_RANK2 = range(6)


def _mix_route_kernel(ys_ref, us5_ref, mgm_ref, x_ref, d_ref, gluw_ref, glub_ref, gs5_ref,
                      wout_ref, gffn_ref, rwh_ref, rwl_ref, rb_ref, tri_ref,
                      x2_ref, t_ref, route_ref, cnt_ref):
    for k in range(x_ref.shape[0] // TM_MIX):
        rows = slice(k * TM_MIX, (k + 1) * TM_MIX)
        _mix_route_tile(ys_ref[rows, :], us5_ref[rows, :], mgm_ref[rows, :], x_ref[rows, :], d_ref, gluw_ref,
                        glub_ref, gs5_ref, wout_ref, gffn_ref, rwh_ref, rwl_ref, rb_ref, tri_ref,
                        x2_ref.at[rows, :], t_ref.at[rows, :], route_ref.at[rows, :], cnt_ref.at[k])


def _mix_route_tile(ys, us5, mgm, x, d_ref, gluw_ref, glub_ref, gs5_ref,
                    wout_ref, gffn_ref, rwh_ref, rwl_ref, rb_ref, tri_ref,
                    x2_ref, t_ref, route_ref, cnt_ref):
    gw = mgm.shape[1]
    y = ys + d_ref[...] * us5
    g = _gelu(y)
    z = g * jax.nn.sigmoid(_dot(g.astype(BF16), gluw_ref[...]) + glub_ref[...])
    ms5 = _rms(z, gs5_ref[...]).astype(BF16)
    mix = _dot(mgm, wout_ref[:gw, :]) + _dot(ms5, wout_ref[gw:, :])
    x2 = x + mix
    x2_ref[...] = x2
    t = _rms(x2, gffn_ref[...])
    t_hi = t.astype(BF16)
    t_ref[...] = t_hi
    t_lo = (t - t_hi.astype(F32)).astype(BF16)
    logits = (_dot(t_hi, rwh_ref[...]) + _dot(t_hi, rwl_ref[...]) + _dot(t_lo, rwh_ref[...])
              + rb_ref[...])
    lane = lax.broadcasted_iota(jnp.int32, logits.shape, 1).astype(F32)
    neg = jnp.float32(-jnp.inf)

    def first_max(mask):
        vals = jnp.where(mask, logits, neg)
        mx = jnp.max(vals, axis=-1, keepdims=True)
        idx = jnp.min(jnp.where(mask & (vals == mx), lane, float(LANES)), axis=-1, keepdims=True)
        return mx, idx

    coarse = lane < N_COARSE
    m1, grp = first_max(coarse)
    p_grp = 1.0 / jnp.sum(jnp.where(coarse, jnp.exp(logits - m1), 0.0), axis=-1, keepdims=True)
    lo = N_COARSE + grp * N_FINE
    fine = (lane >= lo) & (lane < lo + N_FINE)
    v1, i1 = first_max(fine)
    v2, i2 = first_max(fine & (lane != i1))
    e21 = jnp.exp(v2 - v1)
    w1 = p_grp / (1.0 + e21)
    w2 = p_grp * e21 / (1.0 + e21)
    e1 = i1 - N_COARSE
    e2 = i2 - N_COARSE
    hit1 = lane == e1
    hit2 = lane == e2
    onehot = jnp.where(hit1 | hit2, 1.0, 0.0)
    before = _dot(tri_ref[...], onehot.astype(BF16))
    rank1 = jnp.sum(jnp.where(hit1, before, 0.0), axis=-1, keepdims=True)
    rank2 = jnp.sum(jnp.where(hit2, before, 0.0), axis=-1, keepdims=True)
    tm = onehot.shape[0]
    cnt_ref[...] = before[tm - 1:tm, :] + onehot[tm - 1:tm, :]
    rec = jnp.zeros_like(logits)
    for slot, val in ((R_E1, e1), (R_E2, e2), (R_W1, w1), (R_W2, w2),
                      (R_RANK1, rank1), (R_RANK2, rank2)):
        rec = jnp.where(lane == slot, val, rec)
    route_ref[...] = rec


def _mix_route(ys, us5, mgm, x2d, d, gluw_bf, glub, gs5, wout_bf, gffn, rwh, rwl, rb):
    t, dm = x2d.shape
    gw = mgm.shape[1]
    s5w = us5.shape[1]
    tm = TM_MIX * MIX_TILES
    tri = jnp.tril(jnp.ones((TM_MIX, TM_MIX), F32), -1).astype(BF16)
    const = lambda *shape: pl.BlockSpec(shape, lambda i: (0,) * len(shape))
    tile = lambda w: pl.BlockSpec((tm, w), lambda i: (i, 0))
    return pl.pallas_call(
        _mix_route_kernel,
        grid=(t // tm,),
        in_specs=[tile(s5w), tile(s5w), tile(gw), tile(dm),
                  const(1, s5w), const(s5w, s5w), const(1, s5w), const(1, s5w),
                  const(gw + s5w, dm), const(1, dm), const(dm, LANES), const(dm, LANES), const(1, LANES),
                  const(TM_MIX, TM_MIX)],
        out_specs=[tile(dm), tile(dm), tile(LANES), pl.BlockSpec((MIX_TILES, 1, LANES), lambda i: (i, 0, 0))],
        out_shape=[jax.ShapeDtypeStruct((t, dm), F32),
                   jax.ShapeDtypeStruct((t, dm), BF16),
                   jax.ShapeDtypeStruct((t, LANES), F32),
                   jax.ShapeDtypeStruct((t // TM_MIX, 1, LANES), F32)],
        compiler_params=pltpu.CompilerParams(dimension_semantics=("parallel",),
                                             vmem_limit_bytes=VMEM_LIMIT),
        name="mix_route",
    )(ys, us5, mgm, x2d, d, gluw_bf, glub, gs5, wout_bf, gffn, rwh, rwl, rb, tri)


def _local_rows(tm):
    worst = 2 * tm + N_EXPERTS * (SEG_ALIGN - 1)
    return -(-worst // LANES) * LANES


def _segment_plan(cnt, t, tm_expert):
    c = cnt[:, 0, :N_EXPERTS].astype(jnp.int32)
    n_tok_tiles = c.shape[0]
    al = (c + SEG_ALIGN - 1) // SEG_ALIGN * SEG_ALIGN
    lbase = jnp.cumsum(al, axis=1) - al
    tot = jnp.sum(al, axis=0)
    tot_pad = (tot + tm_expert - 1) // tm_expert * tm_expert
    gbase = jnp.cumsum(tot_pad) - tot_pad
    gpos = gbase[None, :] + jnp.cumsum(al, axis=0) - al
    n_tiles_max = -(-(2 * t + n_tok_tiles * N_EXPERTS * (SEG_ALIGN - 1)) // tm_expert) + N_EXPERTS
    tile_end = jnp.cumsum(tot_pad // tm_expert)
    n_tiles = tile_end[-1:].astype(jnp.int32)
    tile_idx = jnp.arange(n_tiles_max, dtype=jnp.int32)
    tile_expert = jnp.sum((tile_idx[:, None] >= tile_end[None, :]).astype(jnp.int32), axis=1)
    last = jnp.sum((n_tiles - 1 >= tile_end).astype(jnp.int32))
    tile_expert = jnp.where(tile_idx < n_tiles, tile_expert, last).astype(jnp.int32)
    ids = jnp.arange(N_EXPERTS, dtype=jnp.int32)
    later_used = (ids[None, :] > ids[:, None]) & (tot_pad[None, :] > 0)
    next_expert = jnp.min(jnp.where(later_used, ids[None, :], N_EXPERTS), axis=1)
    next_expert = jnp.where(next_expert == N_EXPERTS, ids, next_expert).astype(jnp.int32)
    lbase_f = jnp.pad(lbase.astype(F32), ((0, 0), (0, LANES - N_EXPERTS)))[:, None, :]
    flat = lambda a: a.reshape(-1).astype(jnp.int32)
    nch = al // SEG_ALIGN
    cum = jnp.cumsum(nch, axis=1)
    q = jnp.arange(_local_rows(TM_MIX) // SEG_ALIGN, dtype=jnp.int32)[None, :, None]
    seg_of_q = jnp.sum((q >= cum[:, None, :]).astype(jnp.int32), axis=2)
    in_seg = seg_of_q[:, :, None] == jnp.arange(N_EXPERTS, dtype=jnp.int32)[None, None, :]
    pick = lambda a: jnp.sum(jnp.where(in_seg, a[:, None, :], 0), axis=2)
    dst = pick(gpos) + (q[:, :, 0] - pick(cum - nch)) * SEG_ALIGN
    plan = dict(dst=flat(dst), n_chunks=flat(cum[:, -1]),
                tail_pos=flat(gbase + tot), tail_n=flat((tot_pad - tot) // SEG_ALIGN))
    return plan, lbase_f, (tile_expert, n_tiles, next_expert), n_tiles_max * tm_expert


def _local_positions(route, lbase):
    lane = lax.broadcasted_iota(jnp.int32, route.shape, 1).astype(F32)
    out = []
    for e_lane, r_lane in ((R_E1, R_RANK1), (R_E2, R_RANK2)):
        e = route[:, e_lane:e_lane + 1]
        base = jnp.sum(jnp.where(lane == e, lbase, 0.0), axis=-1, keepdims=True)
        out.append(base + route[:, r_lane:r_lane + 1])
    return out


WAIT_GROUP = 8
ISSUE_GROUP = 4


def _segment_copies(i, dst_ref, nq_ref, local, glob, sem, to_global):
    per_tile = local.shape[0] // SEG_ALIGN
    n = nq_ref[i]

    def start(q):
        lo = local.at[pl.ds(pl.multiple_of(q * SEG_ALIGN, SEG_ALIGN), SEG_ALIGN)]
        gl = glob.at[pl.ds(pl.multiple_of(dst_ref[i * per_tile + q], SEG_ALIGN), SEG_ALIGN)]
        (pltpu.make_async_copy(lo, gl, sem) if to_global else pltpu.make_async_copy(gl, lo, sem)).start()

    def group(k, carry):
        for u in range(ISSUE_GROUP):
            start(k * ISSUE_GROUP + u)
        return carry

    def single(q, carry):
        start(q)
        return carry

    full = n // ISSUE_GROUP
    lax.fori_loop(0, full, group, 0)
    lax.fori_loop(full * ISSUE_GROUP, n, single, 0)
    return n


def _wait_chunks(n, local, glob, sem):
    def wait_rows(rows):
        def one(c, carry):
            pltpu.make_async_copy(local.at[pl.ds(0, rows)], glob.at[pl.ds(0, rows)], sem).wait()
            return carry
        return one

    lax.fori_loop(0, n // WAIT_GROUP, wait_rows(WAIT_GROUP * SEG_ALIGN), 0)
    lax.fori_loop(0, n % WAIT_GROUP, wait_rows(SEG_ALIGN), 0)


def _sort_rows_kernel(dst_ref, nq_ref, tpos_ref, tn_ref, nt_ref, t_ref, route_ref, lbase_ref,
                      xs_ref, local_scr, zero_scr, cnt_scr, sem, zsem):
    i = pl.program_id(0)
    n = pl.num_programs(0)
    slot = i % 2
    tm = t_ref.shape[0]
    s_rows = local_scr.shape[1]
    local = local_scr.at[slot]

    @pl.when(i >= 2)
    def _():
        _wait_chunks(cnt_scr[slot], local, xs_ref, sem.at[slot])

    lp1, lp2 = _local_positions(route_ref[...], lbase_ref[0])
    lane = lax.broadcasted_iota(jnp.int32, (tm, LANES), 1)
    lp_rows = jnp.where(lane == 0, lp1, jnp.where(lane == 1, lp2, -1.0)).T
    row = lax.broadcasted_iota(jnp.int32, (s_rows, tm), 0).astype(F32)
    onehot = jnp.where((row == lp_rows[0:1, :]) | (row == lp_rows[1:2, :]), 1.0, 0.0).astype(BF16)
    local_scr[slot] = _dot(onehot, t_ref[...]).astype(BF16)
    cnt_scr[slot] = _segment_copies(i, dst_ref, nq_ref, local, xs_ref, sem.at[slot], True)

    @pl.when(i == n - 1)
    def _():
        @pl.when(n >= 2)
        def _():
            _wait_chunks(cnt_scr[1 - slot], local_scr.at[1 - slot], xs_ref, sem.at[1 - slot])

        _wait_chunks(cnt_scr[slot], local, xs_ref, sem.at[slot])
        zero_scr[...] = jnp.zeros_like(zero_scr)
        te = zero_scr.shape[0]
        zero_chunk = zero_scr.at[pl.ds(0, SEG_ALIGN)]

        def tail(e, total):
            def chunk(c, carry):
                dst = xs_ref.at[pl.ds(pl.multiple_of(tpos_ref[e] + c * SEG_ALIGN, SEG_ALIGN), SEG_ALIGN)]
                pltpu.make_async_copy(zero_chunk, dst, zsem).start()
                return carry

            lax.fori_loop(0, tn_ref[e], chunk, 0)
            return total + tn_ref[e]

        _wait_chunks(lax.fori_loop(0, N_EXPERTS, tail, 0), zero_scr, xs_ref, zsem)

        def unused_tile(j, carry):
            pltpu.make_async_copy(zero_scr, xs_ref.at[pl.ds(pl.multiple_of(j * te, te), te)], zsem).start()
            return carry

        def unused_wait(j, carry):
            pltpu.make_async_copy(zero_scr, xs_ref.at[pl.ds(0, te)], zsem).wait()
            return carry

        lax.fori_loop(nt_ref[0], xs_ref.shape[0] // te, unused_tile, 0)
        lax.fori_loop(nt_ref[0], xs_ref.shape[0] // te, unused_wait, 0)


def _plan_specs(plan):
    keys = ('dst', 'n_chunks', 'tail_pos', 'tail_n')
    return [plan[k] for k in keys]


def _sort_rows(plan, n_tiles, t_bf, route, lbase_f, n_sorted):
    t, dm = t_bf.shape
    tm = TM_MIX
    s_rows = _local_rows(tm)
    im = lambda i, *_: (i, 0)
    return pl.pallas_call(
        _sort_rows_kernel,
        grid_spec=pltpu.PrefetchScalarGridSpec(
            num_scalar_prefetch=5,
            grid=(t // tm,),
            in_specs=[pl.BlockSpec((tm, dm), im), pl.BlockSpec((tm, LANES), im),
                      pl.BlockSpec((1, 1, LANES), lambda i, *_: (i, 0, 0))],
            out_specs=pl.BlockSpec(memory_space=pl.ANY),
            scratch_shapes=[pltpu.VMEM((2, s_rows, dm), BF16), pltpu.VMEM((TM_EXPERT, dm), BF16),
                            pltpu.SMEM((2,), jnp.int32), pltpu.SemaphoreType.DMA((2,)),
                            pltpu.SemaphoreType.DMA(())],
        ),
        out_shape=jax.ShapeDtypeStruct((n_sorted, dm), BF16),
        compiler_params=pltpu.CompilerParams(dimension_semantics=("arbitrary",),
                                             vmem_limit_bytes=VMEM_LIMIT),
        name="sort_rows",
    )(*_plan_specs(plan), n_tiles, t_bf, route, lbase_f)


def _expert_weight_copies(e, slot, hbm, stage, sem):
    return [pltpu.make_async_copy(h.at[e], s.at[slot], sem.at[slot]) for h, s in zip(hbm, stage)]


def _experts_kernel(te_ref, nt_ref, nxt_ref, xs_ref, wg_ref, wu_ref, wd_ref, ys_ref,
                    sg, su, sd, wg_bf, wu_bf, wd_bf, slot_scr, sem):
    i = pl.program_id(0)
    e = te_ref[i]
    hbm, stage = (wg_ref, wu_ref, wd_ref), (sg, su, sd)

    @pl.when(i == 0)
    def _():
        slot_scr[0] = 0
        for cp in _expert_weight_copies(e, 0, hbm, stage, sem):
            cp.start()

    @pl.when((i == 0) | (e != te_ref[jnp.maximum(i - 1, 0)]))
    def _():
        slot = slot_scr[0]
        for cp in _expert_weight_copies(e, slot, hbm, stage, sem):
            cp.wait()
        wg_bf[...] = sg[slot].astype(BF16)
        wu_bf[...] = su[slot].astype(BF16)
        wd_bf[...] = sd[slot].astype(BF16)
        nxt = nxt_ref[e]

        @pl.when(nxt != e)
        def _():
            for cp in _expert_weight_copies(nxt, 1 - slot, hbm, stage, sem):
                cp.start()

        slot_scr[0] = 1 - slot

    @pl.when(i < nt_ref[0])
    def _():
        x = xs_ref[...]
        hidden = (jax.nn.silu(_dot(x, wg_bf[...])) * _dot(x, wu_bf[...])).astype(BF16)
        ys_ref[...] = _dot(hidden, wd_bf[...]).astype(BF16)

    @pl.when(i >= nt_ref[0])
    def _():
        ys_ref[...] = jnp.zeros_like(ys_ref)


def _experts(tile_expert, n_tiles, next_expert, x_sorted, w_gate, w_up, w_down):
    n_sorted, dm = x_sorted.shape
    de = w_gate.shape[2]
    tm = TM_EXPERT
    return pl.pallas_call(
        _experts_kernel,
        grid_spec=pltpu.PrefetchScalarGridSpec(
            num_scalar_prefetch=3,
            grid=(n_sorted // tm,),
            in_specs=[pl.BlockSpec((tm, dm), lambda i, te, nt, nx: (jnp.minimum(i, nt[0] - 1), 0)),
                      pl.BlockSpec(memory_space=pl.ANY), pl.BlockSpec(memory_space=pl.ANY),
                      pl.BlockSpec(memory_space=pl.ANY)],
            out_specs=pl.BlockSpec((tm, dm), lambda i, te, nt, nx: (i, 0)),
            scratch_shapes=[pltpu.VMEM((2, dm, de), F32), pltpu.VMEM((2, dm, de), F32), pltpu.VMEM((2, de, dm), F32),
                            pltpu.VMEM((dm, de), BF16), pltpu.VMEM((dm, de), BF16), pltpu.VMEM((de, dm), BF16),
                            pltpu.SMEM((1,), jnp.int32), pltpu.SemaphoreType.DMA((2,))],
        ),
        out_shape=jax.ShapeDtypeStruct((n_sorted, dm), BF16),
        compiler_params=pltpu.CompilerParams(dimension_semantics=("arbitrary",),
                                             vmem_limit_bytes=VMEM_LIMIT),
        name="experts",
    )(tile_expert, n_tiles, next_expert, x_sorted, w_gate, w_up, w_down)


def _combine_kernel(dst_ref, nq_ref, tpos_ref, tn_ref, x2_ref, route_ref, lbase_ref, gfin_ref, ys_ref,
                    o_ref, local_scr, cnt_scr, sem):
    del tpos_ref, tn_ref
    i = pl.program_id(0)
    n = pl.num_programs(0)
    slot = i % 2
    tm = x2_ref.shape[0]
    s_rows = local_scr.shape[1]

    def fetch(tile, s):
        def clear(r, carry):
            local_scr[s, pl.ds(pl.multiple_of(r * SEG_ALIGN, SEG_ALIGN), SEG_ALIGN), :] = jnp.zeros(
                (SEG_ALIGN, local_scr.shape[2]), BF16)
            return carry

        lax.fori_loop(nq_ref[tile], s_rows // SEG_ALIGN, clear, 0)
        cnt_scr[s] = _segment_copies(tile, dst_ref, nq_ref, local_scr.at[s], ys_ref, sem.at[s], False)

    @pl.when(i == 0)
    def _():
        fetch(0, 0)

    @pl.when(i + 1 < n)
    def _():
        fetch(i + 1, 1 - slot)

    route = route_ref[...]
    lp1, lp2 = _local_positions(route, lbase_ref[0])
    _wait_chunks(cnt_scr[slot], local_scr.at[slot], ys_ref, sem.at[slot])
    w1, w2 = route[:, R_W1:R_W1 + 1], route[:, R_W2:R_W2 + 1]
    col0 = lax.broadcasted_iota(jnp.int32, (tm, KB), 1).astype(F32)
    moe = None
    for k in range(s_rows // KB):
        col = col0 + float(k * KB)
        pick = (jnp.where(col == lp1, w1, 0.0) + jnp.where(col == lp2, w2, 0.0)).astype(BF16)
        part = _dot(pick, local_scr[slot, k * KB:(k + 1) * KB, :])
        moe = part if moe is None else moe + part
    o_ref[...] = _rms(x2_ref[...] + moe, gfin_ref[...])


def _combine(plan, x2, route, lbase_f, gfin, y_sorted):
    t, dm = x2.shape
    tm = TM_MIX
    s_rows = _local_rows(tm)
    im = lambda i, *_: (i, 0)
    return pl.pallas_call(
        _combine_kernel,
        grid_spec=pltpu.PrefetchScalarGridSpec(
            num_scalar_prefetch=4,
            grid=(t // tm,),
            in_specs=[pl.BlockSpec((tm, dm), im), pl.BlockSpec((tm, LANES), im),
                      pl.BlockSpec((1, 1, LANES), lambda i, *_: (i, 0, 0)),
                      pl.BlockSpec((1, dm), lambda i, *_: (0, 0)),
                      pl.BlockSpec(memory_space=pl.ANY)],
            out_specs=pl.BlockSpec((tm, dm), im),
            scratch_shapes=[pltpu.VMEM((2, s_rows, dm), BF16), pltpu.SMEM((2,), jnp.int32),
                            pltpu.SemaphoreType.DMA((2,))],
        ),
        out_shape=jax.ShapeDtypeStruct((t, dm), F32),
        compiler_params=pltpu.CompilerParams(dimension_semantics=("arbitrary",),
                                             vmem_limit_bytes=VMEM_LIMIT),
        name="combine_norm",
    )(*_plan_specs(plan), x2, route, lbase_f, gfin, y_sorted)


def _layer(x, p, s5_ops, gfin):
    b, l, dm = x.shape
    x2d = x.reshape(b * l, dm)
    mgm, us5 = _inproj_gmlp(x2d, p['gmix'], p['win'], p['lng'], p['lnb'], p['ws'], p['bs'], p['gout_gm'])
    n_seg = SUBLANES // b
    *lag_factors, w1, w2, sc = s5_ops[(l // (S5_LC * n_seg))]
    xg = _s5_inproj(x, p['gmix'], p['win_s5'], n_seg)
    yg = _s5_scan(xg, lag_factors, w1, w2, sc, n_seg)
    ys = _s5_to_tokens(yg, b, l, n_seg)
    x2, t_bf, route, counts = _mix_route(ys, us5, mgm, x2d, p['d'], p['gluw'], p['glub'], p['gout_s5'],
                                         p['wout'], p['gffn'], p['rwh'], p['rwl'], p['rb'])
    plan, lbase_f, tiles, n_sorted = _segment_plan(counts, b * l, TM_EXPERT)
    x_sorted = _sort_rows(plan, tiles[1], t_bf, route, lbase_f, n_sorted)
    y_sorted = _experts(*tiles, x_sorted, p['w_gate'], p['w_up'], p['w_down'])
    out = _combine(plan, x2, route, lbase_f, gfin, y_sorted)
    return out.reshape(b, l, dm)


def kernel(x_prompt, x_sample, norm_mix_g, w_in, gm_ln_g, gm_ln_b, gm_ws, gm_bs, s5_lam_re_fwd, s5_lam_im_fwd, s5_log_step_fwd, s5_b_re_fwd, s5_b_im_fwd, s5_c_re_fwd, s5_c_im_fwd, s5_lam_re_bwd, s5_lam_im_bwd, s5_log_step_bwd, s5_b_re_bwd, s5_b_im_bwd, s5_c_re_bwd, s5_c_im_bwd, s5_d, s5_glu_w, s5_glu_b, out_norm_gm, out_norm_s5, w_out, norm_ffn_g, r1_w, r1_b, r2_w, r2_b, e_w_gate, e_w_up, e_w_down, norm_final_g):
    depth = w_in.shape[0]
    gfin = norm_final_g.reshape(1, -1).astype(F32)
    xs = [x_prompt, x_sample]
    for li in range(depth):
        row = lambda a: a[li].reshape(1, -1).astype(F32)
        dm = w_in.shape[1]
        gw = gm_ln_g.shape[1]
        hd_dim = gw // GM_HEADS
        rw = jnp.concatenate([r1_w[li], r2_w[li].transpose(1, 0, 2).reshape(dm, N_EXPERTS)], axis=1).astype(F32)
        rw = jnp.pad(rw, ((0, 0), (0, LANES - rw.shape[1])))
        rwh = rw.astype(BF16)
        rwl = (rw - rwh.astype(F32)).astype(BF16)
        rb = jnp.concatenate([r1_b[li], r2_b[li].reshape(-1)]).astype(F32)
        rb = jnp.pad(rb, (0, LANES - rb.shape[0])).reshape(1, LANES)
        p = dict(
            gmix=row(norm_mix_g), win=w_in[li].astype(BF16), win_s5=w_in[li][:, 2 * gw:].astype(BF16),
            lng=row(gm_ln_g), lnb=row(gm_ln_b),
            ws=gm_ws[li].astype(BF16),
            bs=jnp.broadcast_to(gm_bs[li].astype(F32)[:, :, None], (GM_HEADS, CHUNK, hd_dim)),
            gout_gm=row(out_norm_gm), d=row(s5_d), gluw=s5_glu_w[li].astype(BF16), glub=row(s5_glu_b),
            gout_s5=row(out_norm_s5), wout=w_out[li].astype(BF16), gffn=row(norm_ffn_g),
            rwh=rwh, rwl=rwl, rb=rb,
            w_gate=e_w_gate[li], w_up=e_w_up[li], w_down=e_w_down[li],
        )
        fwd = (s5_lam_re_fwd[li], s5_lam_im_fwd[li], s5_log_step_fwd[li], s5_b_re_fwd[li], s5_b_im_fwd[li],
               s5_c_re_fwd[li], s5_c_im_fwd[li])
        bwd = (s5_lam_re_bwd[li], s5_lam_im_bwd[li], s5_log_step_bwd[li], s5_b_re_bwd[li], s5_b_im_bwd[li],
               s5_c_re_bwd[li], s5_c_im_bwd[li])
        s5_ops = {}
        for x in xs:
            seg_steps = x.shape[1] // (S5_LC * (SUBLANES // x.shape[0]))
            if seg_steps not in s5_ops:
                s5_ops[seg_steps] = _s5_operator(fwd, bwd, S5_LC, seg_steps)
        last = li == depth - 1
        assert last, "depth > 1 needs an un-normalised layer output"
        xs = [_layer(x, p, s5_ops, gfin) for x in xs]
    return tuple(xs)
```

```python
import functools
import math

import jax
import jax.numpy as jnp
from jax import lax
from jax.experimental import pallas as pl
from jax.experimental.pallas import tpu as pltpu

F32 = jnp.float32
BF16 = jnp.bfloat16

EPS = 1e-6
LAMBDA_RE_MAX = -1e-4
GM_HEADS = 4
CHUNK = 128
S5_GROUP = 16
S5_STATE = 64
N_COARSE = 4
N_FINE = 8
N_EXPERTS = N_COARSE * N_FINE

LANES = 128
SUBLANES = 8
S5_LC = 16
VMEM_LIMIT = 56 * 1024 * 1024

TM_PROJ = 1024
TM_MIX = 512
MIX_TILES = 2
KB = 256
TM_EXPERT = 512
SEG_ALIGN = 16


def _gelu(x):
    c = math.sqrt(2.0 / math.pi)
    return x * (0.5 * (1.0 + jnp.tanh(c * (x + 0.044715 * (x * x * x)))))


def _rms(x, g):
    ms = jnp.mean(x * x, axis=-1, keepdims=True)
    return x * lax.rsqrt(ms + EPS) * g


def _dot(a, b):
    return jnp.dot(a, b, preferred_element_type=F32)


def _lockstep(tiles):
    while tiles:
        tiles = [t for t in tiles if next(t, "done") != "done"]


def _inproj_gmlp_kernel(x_ref, gmix_ref, win_ref, lng_ref, lnb_ref, ws_ref, bs_ref, gout_ref,
                        mgm_ref, us5_ref, y_scr):
    tm = x_ref.shape[0]
    gw = mgm_ref.shape[1]
    hd_dim = gw // GM_HEADS
    n_chunks = tm // CHUNK
    h = _rms(x_ref[...], gmix_ref[...]).astype(BF16)
    proj = _dot(h, win_ref[...])
    us5_ref[...] = proj[:, 2 * gw:]
    u = _gelu(proj[:, :gw])
    v = _gelu(proj[:, gw:2 * gw])
    for hd in range(GM_HEADS):
        lo = hd * hd_dim
        vh = v[:, lo:lo + hd_dim]
        mu = jnp.mean(vh, axis=-1, keepdims=True)
        xc = vh - mu
        var = jnp.mean(xc * xc, axis=-1, keepdims=True)
        vn = (xc * lax.rsqrt(var + EPS) * lng_ref[:, lo:lo + hd_dim]
              + lnb_ref[:, lo:lo + hd_dim]).astype(BF16)
        rhs = jnp.concatenate([vn[c * CHUNK:(c + 1) * CHUNK] for c in range(n_chunks)], axis=1)
        s = _dot(ws_ref[hd], rhs)
        for c in range(n_chunks):
            sc = s[:, c * hd_dim:(c + 1) * hd_dim] + bs_ref[hd]
            y_scr[c * CHUNK:(c + 1) * CHUNK, lo:lo + hd_dim] = u[c * CHUNK:(c + 1) * CHUNK, lo:lo + hd_dim] * sc
    mgm_ref[...] = _rms(y_scr[...], gout_ref[...]).astype(BF16)


def _inproj_gmlp(x2d, gmix, win_bf, lng, lnb, ws_bf, bs_b, gout):
    t, d = x2d.shape
    d_in = win_bf.shape[1]
    gw = lng.shape[1]
    s5w = d_in - 2 * gw
    tm = TM_PROJ
    const = lambda *shape: pl.BlockSpec(shape, lambda i: (0,) * len(shape))
    return pl.pallas_call(
        _inproj_gmlp_kernel,
        grid=(t // tm,),
        in_specs=[
            pl.BlockSpec((tm, d), lambda i: (i, 0)),
            const(1, d), const(d, d_in), const(1, gw), const(1, gw),
            const(GM_HEADS, CHUNK, CHUNK), const(GM_HEADS, CHUNK, gw // GM_HEADS), const(1, gw),
        ],
        out_specs=[pl.BlockSpec((tm, gw), lambda i: (i, 0)),
                   pl.BlockSpec((tm, s5w), lambda i: (i, 0))],
        out_shape=[jax.ShapeDtypeStruct((t, gw), BF16),
                   jax.ShapeDtypeStruct((t, s5w), F32)],
        scratch_shapes=[pltpu.VMEM((tm, gw), F32)],
        compiler_params=pltpu.CompilerParams(dimension_semantics=("parallel",),
                                             vmem_limit_bytes=VMEM_LIMIT),
        name="inproj_gmlp",
    )(x2d, gmix, win_bf, lng, lnb, ws_bf, bs_b, gout)


def _s5_consts(lam_re, lam_im, log_step, b_re, b_im, c_re, c_im, lc):
    lr = jnp.minimum(lam_re.astype(F32), LAMBDA_RE_MAX)
    li = lam_im.astype(F32)
    step = jnp.exp(log_step.astype(F32))[:, None]
    dr, di = lr * step, li * step
    ar, ai = _cexp(dr, di)
    nr, ni = ar - 1.0, ai
    den = lr * lr + li * li
    qr, qi = (nr * lr + ni * li) / den, (ni * lr - nr * li) / den
    br, bi = b_re.astype(F32), b_im.astype(F32)
    bbr = qr[..., None] * br - qi[..., None] * bi
    bbi = qr[..., None] * bi + qi[..., None] * br
    k = jnp.arange(lc + 1, dtype=F32)[:, None, None]
    pwr, pwi = _cexp(k * dr[None], k * di[None])
    return (dr, di), (pwr, pwi), (bbr, bbi), (c_re.astype(F32), c_im.astype(F32))


def _cexp(zr, zi):
    m = jnp.exp(zr)
    return m * jnp.cos(zi), m * jnp.sin(zi)


def _s5_operator(fwd, bwd, lc, seg_steps):
    consts = [_s5_consts(*fwd, lc), _s5_consts(*bwd, lc)]
    g, p, h = consts[0][2][0].shape
    lags, w1_parts, w2_parts, sc_rows, seg_rows = [], [], [], [], []
    for direction, (ld, pw, bb, c) in enumerate(consts):
        (dr, di), (pwr, pwi), (bbr, bbi), (cr, ci) = ld, pw, bb, c
        crt, cit = cr.transpose(0, 2, 1), ci.transpose(0, 2, 1)
        pwrt, pwit = pwr.transpose(1, 2, 0), pwi.transpose(1, 2, 0)
        cpr = crt[:, :, None, :] * pwrt[:, :, :, None] - cit[:, :, None, :] * pwit[:, :, :, None]
        cpi = crt[:, :, None, :] * pwit[:, :, :, None] + cit[:, :, None, :] * pwrt[:, :, :, None]
        ck = jnp.concatenate([cpr[:, :, :lc], cpi[:, :, :lc]], axis=1)
        if direction == 1:
            ck = jnp.flip(ck, 2)
        lags += [jnp.concatenate([bbr.transpose(0, 2, 1), -bbi.transpose(0, 2, 1)], axis=-1),
                 ck.reshape(g, 2 * p, lc * h)]
        er, ei = pwrt[:, :, :lc].transpose(0, 2, 1), pwit[:, :, :lc].transpose(0, 2, 1)
        if direction == 0:
            er, ei = jnp.flip(er, 1), jnp.flip(ei, 1)
        bbrt, bbit = bbr.transpose(0, 2, 1), bbi.transpose(0, 2, 1)
        wr = er[:, :, None, :] * bbrt[:, None] - ei[:, :, None, :] * bbit[:, None]
        wi = er[:, :, None, :] * bbit[:, None] + ei[:, :, None, :] * bbrt[:, None]
        w1_parts += [wr, wi, wi, wr]
        fr, fi = cpr[:, :, 1:lc + 1], cpi[:, :, 1:lc + 1]
        if direction == 1:
            fr, fi = jnp.flip(fr, 2), jnp.flip(fi, 2)
        w2_parts += [fr, -fi]

        def mult(zr, zi):
            return [jnp.concatenate([zr, zr], -1), jnp.concatenate([-zi, zi], -1)]

        sc_rows += mult(*_cexp(lc * dr, lc * di))
        seg_rows += mult(*_cexp((lc * seg_steps) * dr, (lc * seg_steps) * di))
    w1 = jnp.concatenate(w1_parts, axis=-1).reshape(g, lc * h, 8 * p)
    w2 = jnp.concatenate(w2_parts, axis=1).reshape(g, 4 * p, lc * h)
    sc = jnp.stack(sc_rows + seg_rows, axis=1)
    return tuple(lags) + (w1.astype(BF16), w2.astype(BF16), sc.astype(F32))


S5_GPS = 2


def _s5_kernel(x_ref, bbf_ref, cpf_ref, bbb_ref, cpb_ref, w1_ref, w2_ref, sc_ref, y_ref,
               loc_scr, sin_scr, m_scr, *, n_seg):
    gps, rows, kw = x_ref.shape
    steps = rows // SUBLANES
    sw = sc_ref.shape[2]

    for gi in range(gps):
        kf = jnp.dot(bbf_ref[gi], cpf_ref[gi], precision=lax.Precision.HIGHEST, preferred_element_type=F32)
        kb = jnp.dot(bbb_ref[gi], cpb_ref[gi], precision=lax.Precision.HIGHEST, preferred_element_type=F32)
        hch = kf.shape[0]
        lc = kw // hch
        lane = lax.broadcasted_iota(jnp.int32, kf.shape, 1)
        for s in range(lc):
            f = kf if s == 0 else jnp.where(lane >= s * hch, pltpu.roll(kf, s * hch, 1), 0.0)
            left = (lc - 1 - s) * hch
            b = kb if left == 0 else pltpu.roll(kb, kw - left, 1)
            m_scr[gi, s * hch:(s + 1) * hch, :] = (f + jnp.where(lane < (s + 1) * hch, b, 0.0)).astype(BF16)
        loc_scr[gi] = _dot(x_ref[gi], w1_ref[gi])

    def bc(gi, i):
        return jnp.broadcast_to(sc_ref[gi, i:i + 1, :], (SUBLANES, sw))

    mult = [[bc(gi, i) for i in range(8)] for gi in range(gps)]

    def step(gi, s, state):
        f, fs, b, bs = state
        a1f, a2f, a1b, a2b = mult[gi][:4]
        rf = pl.multiple_of(s * SUBLANES, SUBLANES)
        rb = pl.multiple_of((steps - 1 - s) * SUBLANES, SUBLANES)
        lf = loc_scr[gi, pl.ds(rf, SUBLANES), 0:sw]
        lfs = loc_scr[gi, pl.ds(rf, SUBLANES), sw:2 * sw]
        lb = loc_scr[gi, pl.ds(rb, SUBLANES), 2 * sw:3 * sw]
        lbs = loc_scr[gi, pl.ds(rb, SUBLANES), 3 * sw:4 * sw]
        return (a1f * f + a2f * fs + lf, a1f * fs - a2f * f + lfs,
                a1b * b + a2b * bs + lb, a1b * bs - a2b * b + lbs)

    zero = jnp.zeros((SUBLANES, sw), F32)

    def pass1(s, carry):
        return tuple(step(gi, s, carry[gi]) for gi in range(gps))

    ends = lax.fori_loop(0, steps, pass1, tuple((zero,) * 4 for _ in range(gps)), unroll=4)

    seg = lax.broadcasted_iota(jnp.int32, (SUBLANES, sw), 0) % n_seg
    enter = []
    for gi in range(gps):
        f_end, fs_end, b_end, bs_end = ends[gi]
        p1f, p2f, p1b, p2b = mult[gi][4:]
        cf, cfs, cb, cbs = zero, zero, zero, zero
        for _ in range(n_seg - 1):
            ef = f_end + p1f * cf + p2f * cfs
            efs = fs_end + p1f * cfs - p2f * cf
            eb = b_end + p1b * cb + p2b * cbs
            ebs = bs_end + p1b * cbs - p2b * cb
            cf = jnp.where(seg >= 1, pltpu.roll(ef, 1, 0), 0.0)
            cfs = jnp.where(seg >= 1, pltpu.roll(efs, 1, 0), 0.0)
            cb = jnp.where(seg <= n_seg - 2, pltpu.roll(eb, SUBLANES - 1, 0), 0.0)
            cbs = jnp.where(seg <= n_seg - 2, pltpu.roll(ebs, SUBLANES - 1, 0), 0.0)
        enter.append((cf, cfs, cb, cbs))

    def pass2(s, carry):
        rf = pl.multiple_of(s * SUBLANES, SUBLANES)
        rb = pl.multiple_of((steps - 1 - s) * SUBLANES, SUBLANES)
        for gi in range(gps):
            sin_scr[gi, pl.ds(rf, SUBLANES), 0:sw] = carry[gi][0]
            sin_scr[gi, pl.ds(rb, SUBLANES), sw:2 * sw] = carry[gi][2]
        return tuple(step(gi, s, carry[gi]) for gi in range(gps))

    lax.fori_loop(0, steps, pass2, tuple(enter), unroll=4)

    for gi in range(gps):
        y_ref[gi] = _dot(x_ref[gi], m_scr[gi]) + _dot(sin_scr[gi].astype(BF16), w2_ref[gi])


def _s5_scan(xg, lag_factors, w1, w2, sc, n_seg):
    g, rows, kw = xg.shape
    sw = sc.shape[2]
    gps = S5_GPS
    blk = lambda a: pl.BlockSpec((gps,) + a.shape[1:], lambda i: (i, 0, 0))
    return pl.pallas_call(
        functools.partial(_s5_kernel, n_seg=n_seg),
        grid=(g // gps,),
        in_specs=[blk(xg)] + [blk(a) for a in lag_factors] + [blk(w1), blk(w2), blk(sc)],
        out_specs=pl.BlockSpec((gps, rows, kw), lambda i: (i, 0, 0)),
        out_shape=jax.ShapeDtypeStruct((g, rows, kw), F32),
        scratch_shapes=[pltpu.VMEM((gps, rows, 4 * sw), F32), pltpu.VMEM((gps, rows, 2 * sw), F32),
                        pltpu.VMEM((gps, kw, kw), BF16)],
        compiler_params=pltpu.CompilerParams(dimension_semantics=("parallel",),
                                             vmem_limit_bytes=VMEM_LIMIT),
        name="s5_scan",
    )(xg, *lag_factors, w1, w2, sc)


S5_NM = 16


def _block_transpose8(v, width):
    lane = lax.broadcasted_iota(jnp.int32, v[0].shape, 1)
    for d in (4, 2, 1):
        w = width * d
        hi = ((lane // w) % 2) == 1
        out = list(v)
        for i0 in range(8):
            if i0 & d:
                continue
            i1 = i0 + d
            out[i0] = jnp.where(hi, pltpu.roll(v[i1], w, 1), v[i0])
            out[i1] = jnp.where(hi, v[i1], pltpu.roll(v[i0], 8 * width - w, 1))
        v = out
    return v


def _tile_copies(hbm4, tile, buf, slot, sem, nm, to_hbm):
    copies = []
    for c in range(SUBLANES):
        for j in range(S5_LC):
            h = hbm4.at[c, pl.ds(tile * nm, nm), pl.ds(j, 1), :]
            v = buf.at[slot, j, :, pl.ds(c, 1), :]
            copies.append(pltpu.make_async_copy(v, h, sem.at[slot]) if to_hbm
                          else pltpu.make_async_copy(h, v, sem.at[slot]))
    return copies


def _s5_inproj_kernel(x4_ref, gmix_ref, w_ref, xg_ref, xs, sem, *, nm):
    i = pl.program_id(0)
    n = pl.num_programs(0)
    slot = i % 2
    dm = x4_ref.shape[3]

    @pl.when(i == 0)
    def _():
        for cp in _tile_copies(x4_ref, 0, xs, 0, sem, nm, False):
            cp.start()

    @pl.when(i + 1 < n)
    def _():
        for cp in _tile_copies(x4_ref, i + 1, xs, 1 - slot, sem, nm, False):
            cp.start()

    pltpu.make_async_copy(xs.at[slot], xs.at[slot], sem.at[slot]).wait()
    rows = nm * SUBLANES
    x = xs[slot].reshape(S5_LC * rows, dm)
    z = _dot(_rms(x, gmix_ref[...]).astype(BF16), w_ref[...])
    n_oct = z.shape[1] // LANES
    for q in range(n_oct):
        for a in range(S5_LC // 8):
            blocks = [z[(8 * a + j8) * rows:(8 * a + j8 + 1) * rows, q * LANES:(q + 1) * LANES] for j8 in range(8)]
            for g8, b in enumerate(_block_transpose8(blocks, S5_GROUP)):
                xg_ref[8 * q + g8, :, a * LANES:(a + 1) * LANES] = b.astype(BF16)


def _s5_inproj(x, gmix, w_s5_bf, n_seg):
    b, l, dm = x.shape
    steps = l // (S5_LC * n_seg)
    nm = S5_NM
    s5w = w_s5_bf.shape[1]
    g = s5w // S5_GROUP
    x4 = x.reshape(b * n_seg, steps, S5_LC, dm)
    return pl.pallas_call(
        functools.partial(_s5_inproj_kernel, nm=nm),
        grid=(steps // nm,),
        in_specs=[pl.BlockSpec(memory_space=pl.ANY),
                  pl.BlockSpec((1, dm), lambda i: (0, 0)),
                  pl.BlockSpec((dm, s5w), lambda i: (0, 0))],
        out_specs=pl.BlockSpec((g, nm * SUBLANES, S5_LC * S5_GROUP), lambda i: (0, i, 0)),
        out_shape=jax.ShapeDtypeStruct((g, steps * SUBLANES, S5_LC * S5_GROUP), BF16),
        scratch_shapes=[pltpu.VMEM((2, S5_LC, nm, SUBLANES, dm), F32), pltpu.SemaphoreType.DMA((2,))],
        compiler_params=pltpu.CompilerParams(dimension_semantics=("arbitrary",),
                                             vmem_limit_bytes=VMEM_LIMIT),
        name="s5_inproj",
    )(x4, gmix, w_s5_bf)


def _s5_to_tokens_kernel(yg_ref, ys4_ref, zs, sem, *, nm):
    i = pl.program_id(0)
    n = pl.num_programs(0)
    slot = i % 2
    rows = nm * SUBLANES

    def wait(s):
        pltpu.make_async_copy(zs.at[s], zs.at[s], sem.at[s]).wait()

    @pl.when(i >= 2)
    def _():
        wait(slot)

    n_oct = yg_ref.shape[0] // 8
    for q in range(n_oct):
        for a in range(S5_LC // 8):
            blocks = [yg_ref[8 * q + g8, :, a * LANES:(a + 1) * LANES] for g8 in range(8)]
            for j8, b in enumerate(_block_transpose8(blocks, S5_GROUP)):
                zs[slot, 8 * a + j8, :, :, q * LANES:(q + 1) * LANES] = b.reshape(nm, SUBLANES, LANES)
    for cp in _tile_copies(ys4_ref, i, zs, slot, sem, nm, True):
        cp.start()

    @pl.when(i == n - 1)
    def _():
        wait(1 - slot)
        wait(slot)


def _s5_to_tokens(yg, b, l, n_seg):
    g, rows_total, kw = yg.shape
    steps = rows_total // SUBLANES
    nm = S5_NM
    s5w = g * S5_GROUP
    assert steps // nm >= 2
    ys4 = pl.pallas_call(
        functools.partial(_s5_to_tokens_kernel, nm=nm),
        grid=(steps // nm,),
        in_specs=[pl.BlockSpec((g, nm * SUBLANES, kw), lambda i: (0, i, 0))],
        out_specs=pl.BlockSpec(memory_space=pl.ANY),
        out_shape=jax.ShapeDtypeStruct((b * n_seg, steps, S5_LC, s5w), F32),
        scratch_shapes=[pltpu.VMEM((2, S5_LC, nm, SUBLANES, s5w), F32), pltpu.SemaphoreType.DMA((2,))],
        compiler_params=pltpu.CompilerParams(dimension_semantics=("arbitrary",),
                                             vmem_limit_bytes=VMEM_LIMIT),
        name="s5_to_tokens",
    )(yg)
    return ys4.reshape(b * l, s5w)


R_E1, R_E2, R_W1, R_W2, R_RANK1, R_RANK2 = range(6)


def _mix_route_kernel(ys_ref, us5_ref, mgm_ref, x_ref, d_ref, gluw_ref, glub_ref, gs5_ref,
                      wout_ref, gffn_ref, rwh_ref, rwl_ref, rb_ref, tri_ref,
                      x2_ref, t_ref, route_ref, cnt_ref):
    tiles = []
    for k in range(x_ref.shape[0] // TM_MIX):
        rows = slice(k * TM_MIX, (k + 1) * TM_MIX)
        tiles.append(_mix_route_tile(ys_ref.at[rows, :], us5_ref.at[rows, :], mgm_ref.at[rows, :], x_ref.at[rows, :],
                                     d_ref, gluw_ref, glub_ref, gs5_ref, wout_ref, gffn_ref, rwh_ref, rwl_ref,
                                     rb_ref, tri_ref, x2_ref.at[rows, :], t_ref.at[rows, :], route_ref.at[rows, :],
                                     cnt_ref.at[k]))
    _lockstep(tiles)


def _mix_route_tile(ys_ref, us5_ref, mgm_ref, x_ref, d_ref, gluw_ref, glub_ref, gs5_ref,
                    wout_ref, gffn_ref, rwh_ref, rwl_ref, rb_ref, tri_ref,
                    x2_ref, t_ref, route_ref, cnt_ref):
    gw = mgm_ref.shape[1]
    y = ys_ref[...] + d_ref[...] * us5_ref[...]
    g = _gelu(y)
    yield
    gate = _dot(g.astype(BF16), gluw_ref[...])
    yield
    z = g * jax.nn.sigmoid(gate + glub_ref[...])
    ms5 = _rms(z, gs5_ref[...]).astype(BF16)
    yield
    mix = _dot(mgm_ref[...], wout_ref[:gw, :]) + _dot(ms5, wout_ref[gw:, :])
    yield
    x2 = x_ref[...] + mix
    x2_ref[...] = x2
    t = _rms(x2, gffn_ref[...])
    t_hi = t.astype(BF16)
    t_ref[...] = t_hi
    t_lo = (t - t_hi.astype(F32)).astype(BF16)
    yield
    logits = (_dot(t_hi, rwh_ref[...]) + _dot(t_hi, rwl_ref[...]) + _dot(t_lo, rwh_ref[...])
              + rb_ref[...])
    yield
    lane = lax.broadcasted_iota(jnp.int32, logits.shape, 1).astype(F32)
    neg = jnp.float32(-jnp.inf)

    def first_max(mask):
        vals = jnp.where(mask, logits, neg)
        mx = jnp.max(vals, axis=-1, keepdims=True)
        idx = jnp.min(jnp.where(mask & (vals == mx), lane, float(LANES)), axis=-1, keepdims=True)
        return mx, idx

    coarse = lane < N_COARSE
    m1, grp = first_max(coarse)
    p_grp = 1.0 / jnp.sum(jnp.where(coarse, jnp.exp(logits - m1), 0.0), axis=-1, keepdims=True)
    lo = N_COARSE + grp * N_FINE
    fine = (lane >= lo) & (lane < lo + N_FINE)
    v1, i1 = first_max(fine)
    v2, i2 = first_max(fine & (lane != i1))
    e21 = jnp.exp(v2 - v1)
    w1 = p_grp / (1.0 + e21)
    w2 = p_grp * e21 / (1.0 + e21)
    e1 = i1 - N_COARSE
    e2 = i2 - N_COARSE
    hit1 = lane == e1
    hit2 = lane == e2
    onehot = jnp.where(hit1 | hit2, 1.0, 0.0)
    before = _dot(tri_ref[...], onehot.astype(BF16))
    rank1 = jnp.sum(jnp.where(hit1, before, 0.0), axis=-1, keepdims=True)
    rank2 = jnp.sum(jnp.where(hit2, before, 0.0), axis=-1, keepdims=True)
    tm = onehot.shape[0]
    cnt_ref[...] = before[tm - 1:tm, :] + onehot[tm - 1:tm, :]
    rec = jnp.zeros_like(logits)
    for slot, val in ((R_E1, e1), (R_E2, e2), (R_W1, w1), (R_W2, w2),
                      (R_RANK1, rank1), (R_RANK2, rank2)):
        rec = jnp.where(lane == slot, val, rec)
    route_ref[...] = rec


def _mix_route(ys, us5, mgm, x2d, d, gluw_bf, glub, gs5, wout_bf, gffn, rwh, rwl, rb):
    t, dm = x2d.shape
    gw = mgm.shape[1]
    s5w = us5.shape[1]
    tm = TM_MIX * MIX_TILES
    tri = jnp.tril(jnp.ones((TM_MIX, TM_MIX), F32), -1).astype(BF16)
    const = lambda *shape: pl.BlockSpec(shape, lambda i: (0,) * len(shape))
    tile = lambda w: pl.BlockSpec((tm, w), lambda i: (i, 0))
    return pl.pallas_call(
        _mix_route_kernel,
        grid=(t // tm,),
        in_specs=[tile(s5w), tile(s5w), tile(gw), tile(dm),
                  const(1, s5w), const(s5w, s5w), const(1, s5w), const(1, s5w),
                  const(gw + s5w, dm), const(1, dm), const(dm, LANES), const(dm, LANES), const(1, LANES),
                  const(TM_MIX, TM_MIX)],
        out_specs=[tile(dm), tile(dm), tile(LANES), pl.BlockSpec((MIX_TILES, 1, LANES), lambda i: (i, 0, 0))],
        out_shape=[jax.ShapeDtypeStruct((t, dm), F32),
                   jax.ShapeDtypeStruct((t, dm), BF16),
                   jax.ShapeDtypeStruct((t, LANES), F32),
                   jax.ShapeDtypeStruct((t // TM_MIX, 1, LANES), F32)],
        compiler_params=pltpu.CompilerParams(dimension_semantics=("parallel",),
                                             vmem_limit_bytes=VMEM_LIMIT),
        name="mix_route",
    )(ys, us5, mgm, x2d, d, gluw_bf, glub, gs5, wout_bf, gffn, rwh, rwl, rb, tri)


def _local_rows(tm):
    worst = 2 * tm + N_EXPERTS * (SEG_ALIGN - 1)
    return -(-worst // LANES) * LANES


def _segment_plan(cnt, t, tm_expert):
    c = cnt[:, 0, :N_EXPERTS].astype(jnp.int32)
    n_tok_tiles = c.shape[0]
    al = (c + SEG_ALIGN - 1) // SEG_ALIGN * SEG_ALIGN
    lbase = jnp.cumsum(al, axis=1) - al
    tot = jnp.sum(al, axis=0)
    tot_pad = (tot + tm_expert - 1) // tm_expert * tm_expert
    gbase = jnp.cumsum(tot_pad) - tot_pad
    gpos = gbase[None, :] + jnp.cumsum(al, axis=0) - al
    n_tiles_max = -(-(2 * t + n_tok_tiles * N_EXPERTS * (SEG_ALIGN - 1)) // tm_expert) + N_EXPERTS
    tile_end = jnp.cumsum(tot_pad // tm_expert)
    n_tiles = tile_end[-1:].astype(jnp.int32)
    tile_idx = jnp.arange(n_tiles_max, dtype=jnp.int32)
    tile_expert = jnp.sum((tile_idx[:, None] >= tile_end[None, :]).astype(jnp.int32), axis=1)
    last = jnp.sum((n_tiles - 1 >= tile_end).astype(jnp.int32))
    tile_expert = jnp.where(tile_idx < n_tiles, tile_expert, last).astype(jnp.int32)
    ids = jnp.arange(N_EXPERTS, dtype=jnp.int32)
    later_used = (ids[None, :] > ids[:, None]) & (tot_pad[None, :] > 0)
    next_expert = jnp.min(jnp.where(later_used, ids[None, :], N_EXPERTS), axis=1)
    next_expert = jnp.where(next_expert == N_EXPERTS, ids, next_expert).astype(jnp.int32)
    lbase_f = jnp.pad(lbase.astype(F32), ((0, 0), (0, LANES - N_EXPERTS)))[:, None, :]
    flat = lambda a: a.reshape(-1).astype(jnp.int32)
    nch = al // SEG_ALIGN
    cum = jnp.cumsum(nch, axis=1)
    q = jnp.arange(_local_rows(TM_MIX) // SEG_ALIGN, dtype=jnp.int32)[None, :, None]
    seg_of_q = jnp.sum((q >= cum[:, None, :]).astype(jnp.int32), axis=2)
    in_seg = seg_of_q[:, :, None] == jnp.arange(N_EXPERTS, dtype=jnp.int32)[None, None, :]
    pick = lambda a: jnp.sum(jnp.where(in_seg, a[:, None, :], 0), axis=2)
    dst = pick(gpos) + (q[:, :, 0] - pick(cum - nch)) * SEG_ALIGN
    plan = dict(dst=flat(dst), n_chunks=flat(cum[:, -1]),
                tail_pos=flat(gbase + tot), tail_n=flat((tot_pad - tot) // SEG_ALIGN))
    return plan, lbase_f, (tile_expert, n_tiles, next_expert), n_tiles_max * tm_expert


def _local_positions(route, lbase):
    lane = lax.broadcasted_iota(jnp.int32, route.shape, 1).astype(F32)
    out = []
    for e_lane, r_lane in ((R_E1, R_RANK1), (R_E2, R_RANK2)):
        e = route[:, e_lane:e_lane + 1]
        base = jnp.sum(jnp.where(lane == e, lbase, 0.0), axis=-1, keepdims=True)
        out.append(base + route[:, r_lane:r_lane + 1])
    return out


WAIT_GROUP = 8
ISSUE_GROUP = 4


def _segment_copies(i, dst_ref, nq_ref, local, glob, sem, to_global):
    per_tile = local.shape[0] // SEG_ALIGN
    n = nq_ref[i]

    def start(q):
        lo = local.at[pl.ds(pl.multiple_of(q * SEG_ALIGN, SEG_ALIGN), SEG_ALIGN)]
        gl = glob.at[pl.ds(pl.multiple_of(dst_ref[i * per_tile + q], SEG_ALIGN), SEG_ALIGN)]
        (pltpu.make_async_copy(lo, gl, sem) if to_global else pltpu.make_async_copy(gl, lo, sem)).start()

    def group(k, carry):
        for u in range(ISSUE_GROUP):
            start(k * ISSUE_GROUP + u)
        return carry

    def single(q, carry):
        start(q)
        return carry

    full = n // ISSUE_GROUP
    lax.fori_loop(0, full, group, 0)
    lax.fori_loop(full * ISSUE_GROUP, n, single, 0)
    return n


def _wait_chunks(n, local, glob, sem):
    def wait_rows(rows):
        def one(c, carry):
            pltpu.make_async_copy(local.at[pl.ds(0, rows)], glob.at[pl.ds(0, rows)], sem).wait()
            return carry
        return one

    lax.fori_loop(0, n // WAIT_GROUP, wait_rows(WAIT_GROUP * SEG_ALIGN), 0)
    lax.fori_loop(0, n % WAIT_GROUP, wait_rows(SEG_ALIGN), 0)


def _sort_rows_kernel(dst_ref, nq_ref, tpos_ref, tn_ref, nt_ref, t_ref, route_ref, lbase_ref,
                      xs_ref, local_scr, zero_scr, cnt_scr, sem, zsem):
    i = pl.program_id(0)
    n = pl.num_programs(0)
    slot = i % 2
    tm = t_ref.shape[0]
    s_rows = local_scr.shape[1]
    local = local_scr.at[slot]

    @pl.when(i >= 2)
    def _():
        _wait_chunks(cnt_scr[slot], local, xs_ref, sem.at[slot])

    lp1, lp2 = _local_positions(route_ref[...], lbase_ref[0])
    lane = lax.broadcasted_iota(jnp.int32, (tm, LANES), 1)
    lp_rows = jnp.where(lane == 0, lp1, jnp.where(lane == 1, lp2, -1.0)).T
    row = lax.broadcasted_iota(jnp.int32, (s_rows, tm), 0).astype(F32)
    onehot = jnp.where((row == lp_rows[0:1, :]) | (row == lp_rows[1:2, :]), 1.0, 0.0).astype(BF16)
    local_scr[slot] = _dot(onehot, t_ref[...]).astype(BF16)
    cnt_scr[slot] = _segment_copies(i, dst_ref, nq_ref, local, xs_ref, sem.at[slot], True)

    @pl.when(i == n - 1)
    def _():
        @pl.when(n >= 2)
        def _():
            _wait_chunks(cnt_scr[1 - slot], local_scr.at[1 - slot], xs_ref, sem.at[1 - slot])

        _wait_chunks(cnt_scr[slot], local, xs_ref, sem.at[slot])
        zero_scr[...] = jnp.zeros_like(zero_scr)
        te = zero_scr.shape[0]
        zero_chunk = zero_scr.at[pl.ds(0, SEG_ALIGN)]

        def tail(e, total):
            def chunk(c, carry):
                dst = xs_ref.at[pl.ds(pl.multiple_of(tpos_ref[e] + c * SEG_ALIGN, SEG_ALIGN), SEG_ALIGN)]
                pltpu.make_async_copy(zero_chunk, dst, zsem).start()
                return carry

            lax.fori_loop(0, tn_ref[e], chunk, 0)
            return total + tn_ref[e]

        _wait_chunks(lax.fori_loop(0, N_EXPERTS, tail, 0), zero_scr, xs_ref, zsem)

        def unused_tile(j, carry):
            pltpu.make_async_copy(zero_scr, xs_ref.at[pl.ds(pl.multiple_of(j * te, te), te)], zsem).start()
            return carry

        def unused_wait(j, carry):
            pltpu.make_async_copy(zero_scr, xs_ref.at[pl.ds(0, te)], zsem).wait()
            return carry

        lax.fori_loop(nt_ref[0], xs_ref.shape[0] // te, unused_tile, 0)
        lax.fori_loop(nt_ref[0], xs_ref.shape[0] // te, unused_wait, 0)


def _plan_specs(plan):
    keys = ('dst', 'n_chunks', 'tail_pos', 'tail_n')
    return [plan[k] for k in keys]


def _sort_rows(plan, n_tiles, t_bf, route, lbase_f, n_sorted):
    t, dm = t_bf.shape
    tm = TM_MIX
    s_rows = _local_rows(tm)
    im = lambda i, *_: (i, 0)
    return pl.pallas_call(
        _sort_rows_kernel,
        grid_spec=pltpu.PrefetchScalarGridSpec(
            num_scalar_prefetch=5,
            grid=(t // tm,),
            in_specs=[pl.BlockSpec((tm, dm), im), pl.BlockSpec((tm, LANES), im),
                      pl.BlockSpec((1, 1, LANES), lambda i, *_: (i, 0, 0))],
            out_specs=pl.BlockSpec(memory_space=pl.ANY),
            scratch_shapes=[pltpu.VMEM((2, s_rows, dm), BF16), pltpu.VMEM((TM_EXPERT, dm), BF16),
                            pltpu.SMEM((2,), jnp.int32), pltpu.SemaphoreType.DMA((2,)),
                            pltpu.SemaphoreType.DMA(())],
        ),
        out_shape=jax.ShapeDtypeStruct((n_sorted, dm), BF16),
        compiler_params=pltpu.CompilerParams(dimension_semantics=("arbitrary",),
                                             vmem_limit_bytes=VMEM_LIMIT),
        name="sort_rows",
    )(*_plan_specs(plan), n_tiles, t_bf, route, lbase_f)


def _expert_weight_copies(e, slot, hbm, stage, sem):
    return [pltpu.make_async_copy(h.at[e], s.at[slot], sem.at[slot]) for h, s in zip(hbm, stage)]


def _experts_kernel(te_ref, nt_ref, nxt_ref, xs_ref, wg_ref, wu_ref, wd_ref, ys_ref,
                    sg, su, sd, wg_bf, wu_bf, wd_bf, slot_scr, sem):
    i = pl.program_id(0)
    e = te_ref[i]
    hbm, stage = (wg_ref, wu_ref, wd_ref), (sg, su, sd)

    @pl.when(i == 0)
    def _():
        slot_scr[0] = 0
        for cp in _expert_weight_copies(e, 0, hbm, stage, sem):
            cp.start()

    @pl.when((i == 0) | (e != te_ref[jnp.maximum(i - 1, 0)]))
    def _():
        slot = slot_scr[0]
        for cp in _expert_weight_copies(e, slot, hbm, stage, sem):
            cp.wait()
        wg_bf[...] = sg[slot].astype(BF16)
        wu_bf[...] = su[slot].astype(BF16)
        wd_bf[...] = sd[slot].astype(BF16)
        nxt = nxt_ref[e]

        @pl.when(nxt != e)
        def _():
            for cp in _expert_weight_copies(nxt, 1 - slot, hbm, stage, sem):
                cp.start()

        slot_scr[0] = 1 - slot

    @pl.when(i < nt_ref[0])
    def _():
        x = xs_ref[...]
        hidden = (jax.nn.silu(_dot(x, wg_bf[...])) * _dot(x, wu_bf[...])).astype(BF16)
        ys_ref[...] = _dot(hidden, wd_bf[...]).astype(BF16)

    @pl.when(i >= nt_ref[0])
    def _():
        ys_ref[...] = jnp.zeros_like(ys_ref)


def _experts(tile_expert, n_tiles, next_expert, x_sorted, w_gate, w_up, w_down):
    n_sorted, dm = x_sorted.shape
    de = w_gate.shape[2]
    tm = TM_EXPERT
    return pl.pallas_call(
        _experts_kernel,
        grid_spec=pltpu.PrefetchScalarGridSpec(
            num_scalar_prefetch=3,
            grid=(n_sorted // tm,),
            in_specs=[pl.BlockSpec((tm, dm), lambda i, te, nt, nx: (jnp.minimum(i, nt[0] - 1), 0)),
                      pl.BlockSpec(memory_space=pl.ANY), pl.BlockSpec(memory_space=pl.ANY),
                      pl.BlockSpec(memory_space=pl.ANY)],
            out_specs=pl.BlockSpec((tm, dm), lambda i, te, nt, nx: (i, 0)),
            scratch_shapes=[pltpu.VMEM((2, dm, de), F32), pltpu.VMEM((2, dm, de), F32), pltpu.VMEM((2, de, dm), F32),
                            pltpu.VMEM((dm, de), BF16), pltpu.VMEM((dm, de), BF16), pltpu.VMEM((de, dm), BF16),
                            pltpu.SMEM((1,), jnp.int32), pltpu.SemaphoreType.DMA((2,))],
        ),
        out_shape=jax.ShapeDtypeStruct((n_sorted, dm), BF16),
        compiler_params=pltpu.CompilerParams(dimension_semantics=("arbitrary",),
                                             vmem_limit_bytes=VMEM_LIMIT),
        name="experts",
    )(tile_expert, n_tiles, next_expert, x_sorted, w_gate, w_up, w_down)


def _combine_kernel(dst_ref, nq_ref, tpos_ref, tn_ref, x2_ref, route_ref, lbase_ref, gfin_ref, ys_ref,
                    o_ref, local_scr, cnt_scr, sem):
    del tpos_ref, tn_ref
    i = pl.program_id(0)
    n = pl.num_programs(0)
    slot = i % 2
    tm = x2_ref.shape[0]
    s_rows = local_scr.shape[1]

    def fetch(tile, s):
        def clear(r, carry):
            local_scr[s, pl.ds(pl.multiple_of(r * SEG_ALIGN, SEG_ALIGN), SEG_ALIGN), :] = jnp.zeros(
                (SEG_ALIGN, local_scr.shape[2]), BF16)
            return carry

        lax.fori_loop(nq_ref[tile], s_rows // SEG_ALIGN, clear, 0)
        cnt_scr[s] = _segment_copies(tile, dst_ref, nq_ref, local_scr.at[s], ys_ref, sem.at[s], False)

    @pl.when(i == 0)
    def _():
        fetch(0, 0)

    @pl.when(i + 1 < n)
    def _():
        fetch(i + 1, 1 - slot)

    route = route_ref[...]
    lp1, lp2 = _local_positions(route, lbase_ref[0])
    _wait_chunks(cnt_scr[slot], local_scr.at[slot], ys_ref, sem.at[slot])
    w1, w2 = route[:, R_W1:R_W1 + 1], route[:, R_W2:R_W2 + 1]
    col0 = lax.broadcasted_iota(jnp.int32, (tm, KB), 1).astype(F32)
    moe = None
    for k in range(s_rows // KB):
        col = col0 + float(k * KB)
        pick = (jnp.where(col == lp1, w1, 0.0) + jnp.where(col == lp2, w2, 0.0)).astype(BF16)
        part = _dot(pick, local_scr[slot, k * KB:(k + 1) * KB, :])
        moe = part if moe is None else moe + part
    o_ref[...] = _rms(x2_ref[...] + moe, gfin_ref[...])


def _combine(plan, x2, route, lbase_f, gfin, y_sorted):
    t, dm = x2.shape
    tm = TM_MIX
    s_rows = _local_rows(tm)
    im = lambda i, *_: (i, 0)
    return pl.pallas_call(
        _combine_kernel,
        grid_spec=pltpu.PrefetchScalarGridSpec(
            num_scalar_prefetch=4,
            grid=(t // tm,),
            in_specs=[pl.BlockSpec((tm, dm), im), pl.BlockSpec((tm, LANES), im),
                      pl.BlockSpec((1, 1, LANES), lambda i, *_: (i, 0, 0)),
                      pl.BlockSpec((1, dm), lambda i, *_: (0, 0)),
                      pl.BlockSpec(memory_space=pl.ANY)],
            out_specs=pl.BlockSpec((tm, dm), im),
            scratch_shapes=[pltpu.VMEM((2, s_rows, dm), BF16), pltpu.SMEM((2,), jnp.int32),
                            pltpu.SemaphoreType.DMA((2,))],
        ),
        out_shape=jax.ShapeDtypeStruct((t, dm), F32),
        compiler_params=pltpu.CompilerParams(dimension_semantics=("arbitrary",),
                                             vmem_limit_bytes=VMEM_LIMIT),
        name="combine_norm",
    )(*_plan_specs(plan), x2, route, lbase_f, gfin, y_sorted)


def _layer(x, p, s5_ops, gfin):
    b, l, dm = x.shape
    x2d = x.reshape(b * l, dm)
    mgm, us5 = _inproj_gmlp(x2d, p['gmix'], p['win'], p['lng'], p['lnb'], p['ws'], p['bs'], p['gout_gm'])
    n_seg = SUBLANES // b
    *lag_factors, w1, w2, sc = s5_ops[(l // (S5_LC * n_seg))]
    xg = _s5_inproj(x, p['gmix'], p['win_s5'], n_seg)
    yg = _s5_scan(xg, lag_factors, w1, w2, sc, n_seg)
    ys = _s5_to_tokens(yg, b, l, n_seg)
    x2, t_bf, route, counts = _mix_route(ys, us5, mgm, x2d, p['d'], p['gluw'], p['glub'], p['gout_s5'],
                                         p['wout'], p['gffn'], p['rwh'], p['rwl'], p['rb'])
    plan, lbase_f, tiles, n_sorted = _segment_plan(counts, b * l, TM_EXPERT)
    x_sorted = _sort_rows(plan, tiles[1], t_bf, route, lbase_f, n_sorted)
    y_sorted = _experts(*tiles, x_sorted, p['w_gate'], p['w_up'], p['w_down'])
    out = _combine(plan, x2, route, lbase_f, gfin, y_sorted)
    return out.reshape(b, l, dm)


def kernel(x_prompt, x_sample, norm_mix_g, w_in, gm_ln_g, gm_ln_b, gm_ws, gm_bs, s5_lam_re_fwd, s5_lam_im_fwd, s5_log_step_fwd, s5_b_re_fwd, s5_b_im_fwd, s5_c_re_fwd, s5_c_im_fwd, s5_lam_re_bwd, s5_lam_im_bwd, s5_log_step_bwd, s5_b_re_bwd, s5_b_im_bwd, s5_c_re_bwd, s5_c_im_bwd, s5_d, s5_glu_w, s5_glu_b, out_norm_gm, out_norm_s5, w_out, norm_ffn_g, r1_w, r1_b, r2_w, r2_b, e_w_gate, e_w_up, e_w_down, norm_final_g):
    depth = w_in.shape[0]
    gfin = norm_final_g.reshape(1, -1).astype(F32)
    xs = [x_prompt, x_sample]
    for li in range(depth):
        row = lambda a: a[li].reshape(1, -1).astype(F32)
        dm = w_in.shape[1]
        gw = gm_ln_g.shape[1]
        hd_dim = gw // GM_HEADS
        rw = jnp.concatenate([r1_w[li], r2_w[li].transpose(1, 0, 2).reshape(dm, N_EXPERTS)], axis=1).astype(F32)
        rw = jnp.pad(rw, ((0, 0), (0, LANES - rw.shape[1])))
        rwh = rw.astype(BF16)
        rwl = (rw - rwh.astype(F32)).astype(BF16)
        rb = jnp.concatenate([r1_b[li], r2_b[li].reshape(-1)]).astype(F32)
        rb = jnp.pad(rb, (0, LANES - rb.shape[0])).reshape(1, LANES)
        p = dict(
            gmix=row(norm_mix_g), win=w_in[li].astype(BF16), win_s5=w_in[li][:, 2 * gw:].astype(BF16),
            lng=row(gm_ln_g), lnb=row(gm_ln_b),
            ws=gm_ws[li].astype(BF16),
            bs=jnp.broadcast_to(gm_bs[li].astype(F32)[:, :, None], (GM_HEADS, CHUNK, hd_dim)),
            gout_gm=row(out_norm_gm), d=row(s5_d), gluw=s5_glu_w[li].astype(BF16), glub=row(s5_glu_b),
            gout_s5=row(out_norm_s5), wout=w_out[li].astype(BF16), gffn=row(norm_ffn_g),
            rwh=rwh, rwl=rwl, rb=rb,
            w_gate=e_w_gate[li], w_up=e_w_up[li], w_down=e_w_down[li],
        )
        fwd = (s5_lam_re_fwd[li], s5_lam_im_fwd[li], s5_log_step_fwd[li], s5_b_re_fwd[li], s5_b_im_fwd[li],
               s5_c_re_fwd[li], s5_c_im_fwd[li])
        bwd = (s5_lam_re_bwd[li], s5_lam_im_bwd[li], s5_log_step_bwd[li], s5_b_re_bwd[li], s5_b_im_bwd[li],
               s5_c_re_bwd[li], s5_c_im_bwd[li])
        s5_ops = {}
        for x in xs:
            seg_steps = x.shape[1] // (S5_LC * (SUBLANES // x.shape[0]))
            if seg_steps not in s5_ops:
                s5_ops[seg_steps] = _s5_operator(fwd, bwd, S5_LC, seg_steps)
        last = li == depth - 1
        assert last, "depth > 1 needs an un-normalised layer output"
        xs = [_layer(x, p, s5_ops, gfin) for x in xs]
    return tuple(xs)
```

```python
import functools
import math

import jax
import jax.numpy as jnp
from jax import lax
from jax.experimental import pallas as pl
from jax.experimental.pallas import tpu as pltpu

F32 = jnp.float32
BF16 = jnp.bfloat16

EPS = 1e-6
LAMBDA_RE_MAX = -1e-4
GM_HEADS = 4
CHUNK = 128
S5_GROUP = 16
S5_STATE = 64
N_COARSE = 4
N_FINE = 8
N_EXPERTS = N_COARSE * N_FINE

LANES = 128
SUBLANES = 8
S5_LC = 16
VMEM_LIMIT = 56 * 1024 * 1024

TM_PROJ = 1024
TM_MIX = 512
MIX_TILES = 2
KB = 256
TM_EXPERT = 512
SEG_ALIGN = 16


def _gelu(x):
    c = math.sqrt(2.0 / math.pi)
    return x * (0.5 * (1.0 + jnp.tanh(c * (x + 0.044715 * (x * x * x)))))


def _rms(x, g):
    ms = jnp.mean(x * x, axis=-1, keepdims=True)
    return x * lax.rsqrt(ms + EPS) * g


def _dot(a, b):
    return jnp.dot(a, b, preferred_element_type=F32)


def _lockstep(tiles):
    while tiles:
        tiles = [t for t in tiles if next(t, "done") != "done"]


def _inproj_gmlp_kernel(x_ref, gmix_ref, win_ref, lng_ref, lnb_ref, ws_ref, bs_ref, gout_ref,
                        mgm_ref, us5_ref, y_scr):
    tm = x_ref.shape[0]
    gw = mgm_ref.shape[1]
    hd_dim = gw // GM_HEADS
    n_chunks = tm // CHUNK
    h = _rms(x_ref[...], gmix_ref[...]).astype(BF16)
    proj = _dot(h, win_ref[...])
    us5_ref[...] = proj[:, 2 * gw:]
    u = _gelu(proj[:, :gw])
    v = _gelu(proj[:, gw:2 * gw])
    for hd in range(GM_HEADS):
        lo = hd * hd_dim
        vh = v[:, lo:lo + hd_dim]
        mu = jnp.mean(vh, axis=-1, keepdims=True)
        xc = vh - mu
        var = jnp.mean(xc * xc, axis=-1, keepdims=True)
        vn = (xc * lax.rsqrt(var + EPS) * lng_ref[:, lo:lo + hd_dim]
              + lnb_ref[:, lo:lo + hd_dim]).astype(BF16)
        rhs = jnp.concatenate([vn[c * CHUNK:(c + 1) * CHUNK] for c in range(n_chunks)], axis=1)
        s = _dot(ws_ref[hd], rhs)
        for c in range(n_chunks):
            sc = s[:, c * hd_dim:(c + 1) * hd_dim] + bs_ref[hd]
            y_scr[c * CHUNK:(c + 1) * CHUNK, lo:lo + hd_dim] = u[c * CHUNK:(c + 1) * CHUNK, lo:lo + hd_dim] * sc
    mgm_ref[...] = _rms(y_scr[...], gout_ref[...]).astype(BF16)


def _inproj_gmlp(x2d, gmix, win_bf, lng, lnb, ws_bf, bs_b, gout):
    t, d = x2d.shape
    d_in = win_bf.shape[1]
    gw = lng.shape[1]
    s5w = d_in - 2 * gw
    tm = TM_PROJ
    const = lambda *shape: pl.BlockSpec(shape, lambda i: (0,) * len(shape))
    return pl.pallas_call(
        _inproj_gmlp_kernel,
        grid=(t // tm,),
        in_specs=[
            pl.BlockSpec((tm, d), lambda i: (i, 0)),
            const(1, d), const(d, d_in), const(1, gw), const(1, gw),
            const(GM_HEADS, CHUNK, CHUNK), const(GM_HEADS, CHUNK, gw // GM_HEADS), const(1, gw),
        ],
        out_specs=[pl.BlockSpec((tm, gw), lambda i: (i, 0)),
                   pl.BlockSpec((tm, s5w), lambda i: (i, 0))],
        out_shape=[jax.ShapeDtypeStruct((t, gw), BF16),
                   jax.ShapeDtypeStruct((t, s5w), F32)],
        scratch_shapes=[pltpu.VMEM((tm, gw), F32)],
        compiler_params=pltpu.CompilerParams(dimension_semantics=("parallel",),
                                             vmem_limit_bytes=VMEM_LIMIT),
        name="inproj_gmlp",
    )(x2d, gmix, win_bf, lng, lnb, ws_bf, bs_b, gout)


def _s5_consts(lam_re, lam_im, log_step, b_re, b_im, c_re, c_im, lc):
    lr = jnp.minimum(lam_re.astype(F32), LAMBDA_RE_MAX)
    li = lam_im.astype(F32)
    step = jnp.exp(log_step.astype(F32))[:, None]
    dr, di = lr * step, li * step
    ar, ai = _cexp(dr, di)
    nr, ni = ar - 1.0, ai
    den = lr * lr + li * li
    qr, qi = (nr * lr + ni * li) / den, (ni * lr - nr * li) / den
    br, bi = b_re.astype(F32), b_im.astype(F32)
    bbr = qr[..., None] * br - qi[..., None] * bi
    bbi = qr[..., None] * bi + qi[..., None] * br
    k = jnp.arange(lc + 1, dtype=F32)[:, None, None]
    pwr, pwi = _cexp(k * dr[None], k * di[None])
    return (dr, di), (pwr, pwi), (bbr, bbi), (c_re.astype(F32), c_im.astype(F32))


def _cexp(zr, zi):
    m = jnp.exp(zr)
    return m * jnp.cos(zi), m * jnp.sin(zi)


def _s5_operator(fwd, bwd, lc, seg_steps):
    consts = [_s5_consts(*fwd, lc), _s5_consts(*bwd, lc)]
    g, p, h = consts[0][2][0].shape
    lags, w1_parts, w2_parts, sc_rows, seg_rows = [], [], [], [], []
    for direction, (ld, pw, bb, c) in enumerate(consts):
        (dr, di), (pwr, pwi), (bbr, bbi), (cr, ci) = ld, pw, bb, c
        crt, cit = cr.transpose(0, 2, 1), ci.transpose(0, 2, 1)
        pwrt, pwit = pwr.transpose(1, 2, 0), pwi.transpose(1, 2, 0)
        cpr = crt[:, :, None, :] * pwrt[:, :, :, None] - cit[:, :, None, :] * pwit[:, :, :, None]
        cpi = crt[:, :, None, :] * pwit[:, :, :, None] + cit[:, :, None, :] * pwrt[:, :, :, None]
        ck = jnp.concatenate([cpr[:, :, :lc], cpi[:, :, :lc]], axis=1)
        if direction == 1:
            ck = jnp.flip(ck, 2)
        lags += [jnp.concatenate([bbr.transpose(0, 2, 1), -bbi.transpose(0, 2, 1)], axis=-1),
                 ck.reshape(g, 2 * p, lc * h)]
        er, ei = pwrt[:, :, :lc].transpose(0, 2, 1), pwit[:, :, :lc].transpose(0, 2, 1)
        if direction == 0:
            er, ei = jnp.flip(er, 1), jnp.flip(ei, 1)
        bbrt, bbit = bbr.transpose(0, 2, 1), bbi.transpose(0, 2, 1)
        wr = er[:, :, None, :] * bbrt[:, None] - ei[:, :, None, :] * bbit[:, None]
        wi = er[:, :, None, :] * bbit[:, None] + ei[:, :, None, :] * bbrt[:, None]
        w1_parts += [wr, wi, wi, wr]
        fr, fi = cpr[:, :, 1:lc + 1], cpi[:, :, 1:lc + 1]
        if direction == 1:
            fr, fi = jnp.flip(fr, 2), jnp.flip(fi, 2)
        w2_parts += [fr, -fi]

        def mult(zr, zi):
            return [jnp.concatenate([zr, zr], -1), jnp.concatenate([-zi, zi], -1)]

        sc_rows += mult(*_cexp(lc * dr, lc * di))
        seg_rows += mult(*_cexp((lc * seg_steps) * dr, (lc * seg_steps) * di))
    w1 = jnp.concatenate(w1_parts, axis=-1).reshape(g, lc * h, 8 * p)
    w2 = jnp.concatenate(w2_parts, axis=1).reshape(g, 4 * p, lc * h)
    sc = jnp.stack(sc_rows + seg_rows, axis=1)
    return tuple(lags) + (w1.astype(BF16), w2.astype(BF16), sc.astype(F32))


S5_GPS = 2


def _s5_kernel(x_ref, bbf_ref, cpf_ref, bbb_ref, cpb_ref, w1_ref, w2_ref, sc_ref, y_ref,
               loc_scr, sin_scr, m_scr, *, n_seg):
    gps, rows, kw = x_ref.shape
    steps = rows // SUBLANES
    sw = sc_ref.shape[2]

    for gi in range(gps):
        kf = jnp.dot(bbf_ref[gi], cpf_ref[gi], precision=lax.Precision.HIGHEST, preferred_element_type=F32)
        kb = jnp.dot(bbb_ref[gi], cpb_ref[gi], precision=lax.Precision.HIGHEST, preferred_element_type=F32)
        hch = kf.shape[0]
        lc = kw // hch
        lane = lax.broadcasted_iota(jnp.int32, kf.shape, 1)
        for s in range(lc):
            f = kf if s == 0 else jnp.where(lane >= s * hch, pltpu.roll(kf, s * hch, 1), 0.0)
            left = (lc - 1 - s) * hch
            b = kb if left == 0 else pltpu.roll(kb, kw - left, 1)
            m_scr[gi, s * hch:(s + 1) * hch, :] = (f + jnp.where(lane < (s + 1) * hch, b, 0.0)).astype(BF16)
        loc_scr[gi] = _dot(x_ref[gi], w1_ref[gi])

    def bc(gi, i):
        return jnp.broadcast_to(sc_ref[gi, i:i + 1, :], (SUBLANES, sw))

    mult = [[bc(gi, i) for i in range(8)] for gi in range(gps)]

    def step(gi, s, state):
        f, fs, b, bs = state
        a1f, a2f, a1b, a2b = mult[gi][:4]
        rf = pl.multiple_of(s * SUBLANES, SUBLANES)
        rb = pl.multiple_of((steps - 1 - s) * SUBLANES, SUBLANES)
        lf = loc_scr[gi, pl.ds(rf, SUBLANES), 0:sw]
        lfs = loc_scr[gi, pl.ds(rf, SUBLANES), sw:2 * sw]
        lb = loc_scr[gi, pl.ds(rb, SUBLANES), 2 * sw:3 * sw]
        lbs = loc_scr[gi, pl.ds(rb, SUBLANES), 3 * sw:4 * sw]
        return (a1f * f + a2f * fs + lf, a1f * fs - a2f * f + lfs,
                a1b * b + a2b * bs + lb, a1b * bs - a2b * b + lbs)

    zero = jnp.zeros((SUBLANES, sw), F32)

    def pass1(s, carry):
        return tuple(step(gi, s, carry[gi]) for gi in range(gps))

    ends = lax.fori_loop(0, steps, pass1, tuple((zero,) * 4 for _ in range(gps)), unroll=4)

    seg = lax.broadcasted_iota(jnp.int32, (SUBLANES, sw), 0) % n_seg
    enter = []
    for gi in range(gps):
        f_end, fs_end, b_end, bs_end = ends[gi]
        p1f, p2f, p1b, p2b = mult[gi][4:]
        cf, cfs, cb, cbs = zero, zero, zero, zero
        for _ in range(n_seg - 1):
            ef = f_end + p1f * cf + p2f * cfs
            efs = fs_end + p1f * cfs - p2f * cf
            eb = b_end + p1b * cb + p2b * cbs
            ebs = bs_end + p1b * cbs - p2b * cb
            cf = jnp.where(seg >= 1, pltpu.roll(ef, 1, 0), 0.0)
            cfs = jnp.where(seg >= 1, pltpu.roll(efs, 1, 0), 0.0)
            cb = jnp.where(seg <= n_seg - 2, pltpu.roll(eb, SUBLANES - 1, 0), 0.0)
            cbs = jnp.where(seg <= n_seg - 2, pltpu.roll(ebs, SUBLANES - 1, 0), 0.0)
        enter.append((cf, cfs, cb, cbs))

    def pass2(s, carry):
        rf = pl.multiple_of(s * SUBLANES, SUBLANES)
        rb = pl.multiple_of((steps - 1 - s) * SUBLANES, SUBLANES)
        for gi in range(gps):
            sin_scr[gi, pl.ds(rf, SUBLANES), 0:sw] = carry[gi][0]
            sin_scr[gi, pl.ds(rb, SUBLANES), sw:2 * sw] = carry[gi][2]
        return tuple(step(gi, s, carry[gi]) for gi in range(gps))

    lax.fori_loop(0, steps, pass2, tuple(enter), unroll=4)

    for gi in range(gps):
        y_ref[gi] = _dot(x_ref[gi], m_scr[gi]) + _dot(sin_scr[gi].astype(BF16), w2_ref[gi])


def _s5_scan(xg, lag_factors, w1, w2, sc, n_seg):
    g, rows, kw = xg.shape
    sw = sc.shape[2]
    gps = S5_GPS
    blk = lambda a: pl.BlockSpec((gps,) + a.shape[1:], lambda i: (i, 0, 0))
    return pl.pallas_call(
        functools.partial(_s5_kernel, n_seg=n_seg),
        grid=(g // gps,),
        in_specs=[blk(xg)] + [blk(a) for a in lag_factors] + [blk(w1), blk(w2), blk(sc)],
        out_specs=pl.BlockSpec((gps, rows, kw), lambda i: (i, 0, 0)),
        out_shape=jax.ShapeDtypeStruct((g, rows, kw), F32),
        scratch_shapes=[pltpu.VMEM((gps, rows, 4 * sw), F32), pltpu.VMEM((gps, rows, 2 * sw), F32),
                        pltpu.VMEM((gps, kw, kw), BF16)],
        compiler_params=pltpu.CompilerParams(dimension_semantics=("parallel",),
                                             vmem_limit_bytes=VMEM_LIMIT),
        name="s5_scan",
    )(xg, *lag_factors, w1, w2, sc)


S5_NM = 16


def _block_transpose8(v, width):
    lane = lax.broadcasted_iota(jnp.int32, v[0].shape, 1)
    for d in (4, 2, 1):
        w = width * d
        hi = ((lane // w) % 2) == 1
        out = list(v)
        for i0 in range(8):
            if i0 & d:
                continue
            i1 = i0 + d
            out[i0] = jnp.where(hi, pltpu.roll(v[i1], w, 1), v[i0])
            out[i1] = jnp.where(hi, v[i1], pltpu.roll(v[i0], 8 * width - w, 1))
        v = out
    return v


def _tile_copies(hbm4, tile, buf, slot, sem, nm, to_hbm):
    copies = []
    for c in range(SUBLANES):
        for j in range(S5_LC):
            h = hbm4.at[c, pl.ds(tile * nm, nm), pl.ds(j, 1), :]
            v = buf.at[slot, j, :, pl.ds(c, 1), :]
            copies.append(pltpu.make_async_copy(v, h, sem.at[slot]) if to_hbm
                          else pltpu.make_async_copy(h, v, sem.at[slot]))
    return copies


def _s5_inproj_kernel(x4_ref, gmix_ref, w_ref, xg_ref, xs, sem, *, nm):
    i = pl.program_id(0)
    n = pl.num_programs(0)
    slot = i % 2
    dm = x4_ref.shape[3]

    @pl.when(i == 0)
    def _():
        for cp in _tile_copies(x4_ref, 0, xs, 0, sem, nm, False):
            cp.start()

    @pl.when(i + 1 < n)
    def _():
        for cp in _tile_copies(x4_ref, i + 1, xs, 1 - slot, sem, nm, False):
            cp.start()

    pltpu.make_async_copy(xs.at[slot], xs.at[slot], sem.at[slot]).wait()
    rows = nm * SUBLANES
    x = xs[slot].reshape(S5_LC * rows, dm)
    z = _dot(_rms(x, gmix_ref[...]).astype(BF16), w_ref[...])
    n_oct = z.shape[1] // LANES
    for q in range(n_oct):
        for a in range(S5_LC // 8):
            blocks = [z[(8 * a + j8) * rows:(8 * a + j8 + 1) * rows, q * LANES:(q + 1) * LANES] for j8 in range(8)]
            for g8, b in enumerate(_block_transpose8(blocks, S5_GROUP)):
                xg_ref[8 * q + g8, :, a * LANES:(a + 1) * LANES] = b.astype(BF16)


def _s5_inproj(x, gmix, w_s5_bf, n_seg):
    b, l, dm = x.shape
    steps = l // (S5_LC * n_seg)
    nm = S5_NM
    s5w = w_s5_bf.shape[1]
    g = s5w // S5_GROUP
    x4 = x.reshape(b * n_seg, steps, S5_LC, dm)
    return pl.pallas_call(
        functools.partial(_s5_inproj_kernel, nm=nm),
        grid=(steps // nm,),
        in_specs=[pl.BlockSpec(memory_space=pl.ANY),
                  pl.BlockSpec((1, dm), lambda i: (0, 0)),
                  pl.BlockSpec((dm, s5w), lambda i: (0, 0))],
        out_specs=pl.BlockSpec((g, nm * SUBLANES, S5_LC * S5_GROUP), lambda i: (0, i, 0)),
        out_shape=jax.ShapeDtypeStruct((g, steps * SUBLANES, S5_LC * S5_GROUP), BF16),
        scratch_shapes=[pltpu.VMEM((2, S5_LC, nm, SUBLANES, dm), F32), pltpu.SemaphoreType.DMA((2,))],
        compiler_params=pltpu.CompilerParams(dimension_semantics=("arbitrary",),
                                             vmem_limit_bytes=VMEM_LIMIT),
        name="s5_inproj",
    )(x4, gmix, w_s5_bf)


def _s5_to_tokens_kernel(yg_ref, ys4_ref, zs, sem, *, nm):
    i = pl.program_id(0)
    n = pl.num_programs(0)
    slot = i % 2
    rows = nm * SUBLANES

    def wait(s):
        pltpu.make_async_copy(zs.at[s], zs.at[s], sem.at[s]).wait()

    @pl.when(i >= 2)
    def _():
        wait(slot)

    n_oct = yg_ref.shape[0] // 8
    for q in range(n_oct):
        for a in range(S5_LC // 8):
            blocks = [yg_ref[8 * q + g8, :, a * LANES:(a + 1) * LANES] for g8 in range(8)]
            for j8, b in enumerate(_block_transpose8(blocks, S5_GROUP)):
                zs[slot, 8 * a + j8, :, :, q * LANES:(q + 1) * LANES] = b.reshape(nm, SUBLANES, LANES)
    for cp in _tile_copies(ys4_ref, i, zs, slot, sem, nm, True):
        cp.start()

    @pl.when(i == n - 1)
    def _():
        wait(1 - slot)
        wait(slot)


def _s5_to_tokens(yg, b, l, n_seg):
    g, rows_total, kw = yg.shape
    steps = rows_total // SUBLANES
    nm = S5_NM
    s5w = g * S5_GROUP
    assert steps // nm >= 2
    ys4 = pl.pallas_call(
        functools.partial(_s5_to_tokens_kernel, nm=nm),
        grid=(steps // nm,),
        in_specs=[pl.BlockSpec((g, nm * SUBLANES, kw), lambda i: (0, i, 0))],
        out_specs=pl.BlockSpec(memory_space=pl.ANY),
        out_shape=jax.ShapeDtypeStruct((b * n_seg, steps, S5_LC, s5w), F32),
        scratch_shapes=[pltpu.VMEM((2, S5_LC, nm, SUBLANES, s5w), F32), pltpu.SemaphoreType.DMA((2,))],
        compiler_params=pltpu.CompilerParams(dimension_semantics=("arbitrary",),
                                             vmem_limit_bytes=VMEM_LIMIT),
        name="s5_to_tokens",
    )(yg)
    return ys4.reshape(b * l, s5w)


R_E1, R_E2, R_W1, R_W2, R_RANK1, R_RANK2 = range(6)


def _mix_route_kernel(ys_ref, us5_ref, mgm_ref, x_ref, d_ref, gluw_ref, glub_ref, gs5_ref,
                      wout_ref, gffn_ref, rwh_ref, rwl_ref, rb_ref, tri_ref,
                      x2_ref, t_ref, route_ref, cnt_ref):
    tiles = []
    for k in range(x_ref.shape[0] // TM_MIX):
        rows = slice(k * TM_MIX, (k + 1) * TM_MIX)
        tiles.append(_mix_route_tile(ys_ref.at[rows, :], us5_ref.at[rows, :], mgm_ref.at[rows, :], x_ref.at[rows, :],
                                     d_ref, gluw_ref, glub_ref, gs5_ref, wout_ref, gffn_ref, rwh_ref, rwl_ref,
                                     rb_ref, tri_ref, x2_ref.at[rows, :], t_ref.at[rows, :], route_ref.at[rows, :],
                                     cnt_ref.at[k]))
    _lockstep(tiles)


def _mix_route_tile(ys_ref, us5_ref, mgm_ref, x_ref, d_ref, gluw_ref, glub_ref, gs5_ref,
                    wout_ref, gffn_ref, rwh_ref, rwl_ref, rb_ref, tri_ref,
                    x2_ref, t_ref, route_ref, cnt_ref):
    gw = mgm_ref.shape[1]
    y = ys_ref[...] + d_ref[...] * us5_ref[...]
    g = _gelu(y)
    yield
    gate = _dot(g.astype(BF16), gluw_ref[...])
    yield
    z = g * jax.nn.sigmoid(gate + glub_ref[...])
    ms5 = _rms(z, gs5_ref[...]).astype(BF16)
    yield
    mix = _dot(mgm_ref[...], wout_ref[:gw, :]) + _dot(ms5, wout_ref[gw:, :])
    yield
    x2 = x_ref[...] + mix
    x2_ref[...] = x2
    t = _rms(x2, gffn_ref[...])
    t_hi = t.astype(BF16)
    t_ref[...] = t_hi
    t_lo = (t - t_hi.astype(F32)).astype(BF16)
    yield
    logits = (_dot(t_hi, rwh_ref[...]) + _dot(t_hi, rwl_ref[...]) + _dot(t_lo, rwh_ref[...])
              + rb_ref[...])
    yield
    lane = lax.broadcasted_iota(jnp.int32, logits.shape, 1).astype(F32)
    neg = jnp.float32(-jnp.inf)

    def first_max(mask):
        vals = jnp.where(mask, logits, neg)
        mx = jnp.max(vals, axis=-1, keepdims=True)
        idx = jnp.min(jnp.where(mask & (vals == mx), lane, float(LANES)), axis=-1, keepdims=True)
        return mx, idx

    coarse = lane < N_COARSE
    m1, grp = first_max(coarse)
    p_grp = 1.0 / jnp.sum(jnp.where(coarse, jnp.exp(logits - m1), 0.0), axis=-1, keepdims=True)
    lo = N_COARSE + grp * N_FINE
    fine = (lane >= lo) & (lane < lo + N_FINE)
    v1, i1 = first_max(fine)
    v2, i2 = first_max(fine & (lane != i1))
    e21 = jnp.exp(v2 - v1)
    w1 = p_grp / (1.0 + e21)
    w2 = p_grp * e21 / (1.0 + e21)
    e1 = i1 - N_COARSE
    e2 = i2 - N_COARSE
    hit1 = lane == e1
    hit2 = lane == e2
    onehot = jnp.where(hit1 | hit2, 1.0, 0.0)
    before = _dot(tri_ref[...], onehot.astype(BF16))
    rank1 = jnp.sum(jnp.where(hit1, before, 0.0), axis=-1, keepdims=True)
    rank2 = jnp.sum(jnp.where(hit2, before, 0.0), axis=-1, keepdims=True)
    tm = onehot.shape[0]
    cnt_ref[...] = before[tm - 1:tm, :] + onehot[tm - 1:tm, :]
    rec = jnp.zeros_like(logits)
    for slot, val in ((R_E1, e1), (R_E2, e2), (R_W1, w1), (R_W2, w2),
                      (R_RANK1, rank1), (R_RANK2, rank2)):
        rec = jnp.where(lane == slot, val, rec)
    route_ref[...] = rec


def _mix_route(ys, us5, mgm, x2d, d, gluw_bf, glub, gs5, wout_bf, gffn, rwh, rwl, rb):
    t, dm = x2d.shape
    gw = mgm.shape[1]
    s5w = us5.shape[1]
    tm = TM_MIX * MIX_TILES
    tri = jnp.tril(jnp.ones((TM_MIX, TM_MIX), F32), -1).astype(BF16)
    const = lambda *shape: pl.BlockSpec(shape, lambda i: (0,) * len(shape))
    tile = lambda w: pl.BlockSpec((tm, w), lambda i: (i, 0))
    return pl.pallas_call(
        _mix_route_kernel,
        grid=(t // tm,),
        in_specs=[tile(s5w), tile(s5w), tile(gw), tile(dm),
                  const(1, s5w), const(s5w, s5w), const(1, s5w), const(1, s5w),
                  const(gw + s5w, dm), const(1, dm), const(dm, LANES), const(dm, LANES), const(1, LANES),
                  const(TM_MIX, TM_MIX)],
        out_specs=[tile(dm), tile(dm), tile(LANES), pl.BlockSpec((MIX_TILES, 1, LANES), lambda i: (i, 0, 0))],
        out_shape=[jax.ShapeDtypeStruct((t, dm), F32),
                   jax.ShapeDtypeStruct((t, dm), BF16),
                   jax.ShapeDtypeStruct((t, LANES), F32),
                   jax.ShapeDtypeStruct((t // TM_MIX, 1, LANES), F32)],
        compiler_params=pltpu.CompilerParams(dimension_semantics=("parallel",),
                                             vmem_limit_bytes=VMEM_LIMIT),
        name="mix_route",
    )(ys, us5, mgm, x2d, d, gluw_bf, glub, gs5, wout_bf, gffn, rwh, rwl, rb, tri)


def _local_rows(tm):
    worst = 2 * tm + N_EXPERTS * (SEG_ALIGN - 1)
    return -(-worst // LANES) * LANES


def _segment_plan(cnt, t, tm_expert):
    c = cnt[:, 0, :N_EXPERTS].astype(jnp.int32)
    n_tok_tiles = c.shape[0]
    al = (c + SEG_ALIGN - 1) // SEG_ALIGN * SEG_ALIGN
    lbase = jnp.cumsum(al, axis=1) - al
    tot = jnp.sum(al, axis=0)
    tot_pad = (tot + tm_expert - 1) // tm_expert * tm_expert
    gbase = jnp.cumsum(tot_pad) - tot_pad
    gpos = gbase[None, :] + jnp.cumsum(al, axis=0) - al
    n_tiles_max = -(-(2 * t + n_tok_tiles * N_EXPERTS * (SEG_ALIGN - 1)) // tm_expert) + N_EXPERTS
    tile_end = jnp.cumsum(tot_pad // tm_expert)
    n_tiles = tile_end[-1:].astype(jnp.int32)
    tile_idx = jnp.arange(n_tiles_max, dtype=jnp.int32)
    tile_expert = jnp.sum((tile_idx[:, None] >= tile_end[None, :]).astype(jnp.int32), axis=1)
    last = jnp.sum((n_tiles - 1 >= tile_end).astype(jnp.int32))
    tile_expert = jnp.where(tile_idx < n_tiles, tile_expert, last).astype(jnp.int32)
    ids = jnp.arange(N_EXPERTS, dtype=jnp.int32)
    later_used = (ids[None, :] > ids[:, None]) & (tot_pad[None, :] > 0)
    next_expert = jnp.min(jnp.where(later_used, ids[None, :], N_EXPERTS), axis=1)
    next_expert = jnp.where(next_expert == N_EXPERTS, ids, next_expert).astype(jnp.int32)
    lbase_f = jnp.pad(lbase.astype(F32), ((0, 0), (0, LANES - N_EXPERTS)))[:, None, :]
    flat = lambda a: a.reshape(-1).astype(jnp.int32)
    nch = al // SEG_ALIGN
    cum = jnp.cumsum(nch, axis=1)
    q = jnp.arange(_local_rows(TM_MIX) // SEG_ALIGN, dtype=jnp.int32)[None, :, None]
    seg_of_q = jnp.sum((q >= cum[:, None, :]).astype(jnp.int32), axis=2)
    in_seg = seg_of_q[:, :, None] == jnp.arange(N_EXPERTS, dtype=jnp.int32)[None, None, :]
    pick = lambda a: jnp.sum(jnp.where(in_seg, a[:, None, :], 0), axis=2)
    dst = pick(gpos) + (q[:, :, 0] - pick(cum - nch)) * SEG_ALIGN
    plan = dict(dst=flat(dst), n_chunks=flat(cum[:, -1]),
                tail_pos=flat(gbase + tot), tail_n=flat((tot_pad - tot) // SEG_ALIGN))
    return plan, lbase_f, (tile_expert, n_tiles, next_expert), n_tiles_max * tm_expert


def _local_positions(route, lbase):
    lane = lax.broadcasted_iota(jnp.int32, route.shape, 1).astype(F32)
    out = []
    for e_lane, r_lane in ((R_E1, R_RANK1), (R_E2, R_RANK2)):
        e = route[:, e_lane:e_lane + 1]
        base = jnp.sum(jnp.where(lane == e, lbase, 0.0), axis=-1, keepdims=True)
        out.append(base + route[:, r_lane:r_lane + 1])
    return out


WAIT_GROUP = 8
ISSUE_GROUP = 4


def _segment_copies(i, dst_ref, nq_ref, local, glob, sem, to_global):
    per_tile = local.shape[0] // SEG_ALIGN
    n = nq_ref[i]

    def start(q):
        lo = local.at[pl.ds(pl.multiple_of(q * SEG_ALIGN, SEG_ALIGN), SEG_ALIGN)]
        gl = glob.at[pl.ds(pl.multiple_of(dst_ref[i * per_tile + q], SEG_ALIGN), SEG_ALIGN)]
        (pltpu.make_async_copy(lo, gl, sem) if to_global else pltpu.make_async_copy(gl, lo, sem)).start()

    def group(k, carry):
        for u in range(ISSUE_GROUP):
            start(k * ISSUE_GROUP + u)
        return carry

    def single(q, carry):
        start(q)
        return carry

    full = n // ISSUE_GROUP
    lax.fori_loop(0, full, group, 0)
    lax.fori_loop(full * ISSUE_GROUP, n, single, 0)
    return n


def _wait_chunks(n, local, glob, sem):
    def wait_rows(rows):
        def one(c, carry):
            pltpu.make_async_copy(local.at[pl.ds(0, rows)], glob.at[pl.ds(0, rows)], sem).wait()
            return carry
        return one

    lax.fori_loop(0, n // WAIT_GROUP, wait_rows(WAIT_GROUP * SEG_ALIGN), 0)
    lax.fori_loop(0, n % WAIT_GROUP, wait_rows(SEG_ALIGN), 0)


def _sort_tile(t_ref, route_ref, lbase, local):
    tm = t_ref.shape[0]
    s_rows = local.shape[0]
    lp1, lp2 = _local_positions(route_ref[...], lbase)
    lane = lax.broadcasted_iota(jnp.int32, (tm, LANES), 1)
    lp_rows = jnp.where(lane == 0, lp1, jnp.where(lane == 1, lp2, -1.0)).T
    row = lax.broadcasted_iota(jnp.int32, (s_rows, tm), 0).astype(F32)
    onehot = jnp.where((row == lp_rows[0:1, :]) | (row == lp_rows[1:2, :]), 1.0, 0.0).astype(BF16)
    yield
    local[...] = _dot(onehot, t_ref[...]).astype(BF16)


def _sort_rows_kernel(dst_ref, nq_ref, tpos_ref, tn_ref, nt_ref, t_ref, route_ref, lbase_ref,
                      xs_ref, local_scr, zero_scr, cnt_scr, sem, zsem):
    i = pl.program_id(0)
    n = pl.num_programs(0)
    slot = i % 2
    per_step = local_scr.shape[1]

    def buf(s, k):
        return local_scr.at[s, k], sem.at[s * per_step + k], s * per_step + k

    def wait_slot(s):
        for k in range(per_step):
            local, sm, c = buf(s, k)
            _wait_chunks(cnt_scr[c], local, xs_ref, sm)

    @pl.when(i >= 2)
    def _():
        wait_slot(slot)

    tiles = []
    for k in range(per_step):
        rows = slice(k * TM_MIX, (k + 1) * TM_MIX)
        tiles.append(_sort_tile(t_ref.at[rows, :], route_ref.at[rows, :], lbase_ref[k], buf(slot, k)[0]))
    _lockstep(tiles)
    for k in range(per_step):
        local, sm, c = buf(slot, k)
        cnt_scr[c] = _segment_copies(i * per_step + k, dst_ref, nq_ref, local, xs_ref, sm, True)

    @pl.when(i == n - 1)
    def _():
        @pl.when(n >= 2)
        def _():
            wait_slot(1 - slot)

        wait_slot(slot)
        zero_scr[...] = jnp.zeros_like(zero_scr)
        te = zero_scr.shape[0]
        zero_chunk = zero_scr.at[pl.ds(0, SEG_ALIGN)]

        def tail(e, total):
            def chunk(c, carry):
                dst = xs_ref.at[pl.ds(pl.multiple_of(tpos_ref[e] + c * SEG_ALIGN, SEG_ALIGN), SEG_ALIGN)]
                pltpu.make_async_copy(zero_chunk, dst, zsem).start()
                return carry

            lax.fori_loop(0, tn_ref[e], chunk, 0)
            return total + tn_ref[e]

        _wait_chunks(lax.fori_loop(0, N_EXPERTS, tail, 0), zero_scr, xs_ref, zsem)

        def unused_tile(j, carry):
            pltpu.make_async_copy(zero_scr, xs_ref.at[pl.ds(pl.multiple_of(j * te, te), te)], zsem).start()
            return carry

        def unused_wait(j, carry):
            pltpu.make_async_copy(zero_scr, xs_ref.at[pl.ds(0, te)], zsem).wait()
            return carry

        lax.fori_loop(nt_ref[0], xs_ref.shape[0] // te, unused_tile, 0)
        lax.fori_loop(nt_ref[0], xs_ref.shape[0] // te, unused_wait, 0)


def _plan_specs(plan):
    keys = ('dst', 'n_chunks', 'tail_pos', 'tail_n')
    return [plan[k] for k in keys]


MOE_TILES = 2


def _sort_rows(plan, n_tiles, t_bf, route, lbase_f, n_sorted):
    t, dm = t_bf.shape
    tm = TM_MIX * MOE_TILES
    s_rows = _local_rows(TM_MIX)
    im = lambda i, *_: (i, 0)
    return pl.pallas_call(
        _sort_rows_kernel,
        grid_spec=pltpu.PrefetchScalarGridSpec(
            num_scalar_prefetch=5,
            grid=(t // tm,),
            in_specs=[pl.BlockSpec((tm, dm), im), pl.BlockSpec((tm, LANES), im),
                      pl.BlockSpec((MOE_TILES, 1, LANES), lambda i, *_: (i, 0, 0))],
            out_specs=pl.BlockSpec(memory_space=pl.ANY),
            scratch_shapes=[pltpu.VMEM((2, MOE_TILES, s_rows, dm), BF16), pltpu.VMEM((TM_EXPERT, dm), BF16),
                            pltpu.SMEM((2 * MOE_TILES,), jnp.int32), pltpu.SemaphoreType.DMA((2 * MOE_TILES,)),
                            pltpu.SemaphoreType.DMA(())],
        ),
        out_shape=jax.ShapeDtypeStruct((n_sorted, dm), BF16),
        compiler_params=pltpu.CompilerParams(dimension_semantics=("arbitrary",),
                                             vmem_limit_bytes=VMEM_LIMIT),
        name="sort_rows",
    )(*_plan_specs(plan), n_tiles, t_bf, route, lbase_f)


def _expert_weight_copies(e, slot, hbm, stage, sem):
    return [pltpu.make_async_copy(h.at[e], s.at[slot], sem.at[slot]) for h, s in zip(hbm, stage)]


def _experts_kernel(te_ref, nt_ref, nxt_ref, xs_ref, wg_ref, wu_ref, wd_ref, ys_ref,
                    sg, su, sd, wg_bf, wu_bf, wd_bf, slot_scr, sem):
    i = pl.program_id(0)
    e = te_ref[i]
    hbm, stage = (wg_ref, wu_ref, wd_ref), (sg, su, sd)

    @pl.when(i == 0)
    def _():
        slot_scr[0] = 0
        for cp in _expert_weight_copies(e, 0, hbm, stage, sem):
            cp.start()

    @pl.when((i == 0) | (e != te_ref[jnp.maximum(i - 1, 0)]))
    def _():
        slot = slot_scr[0]
        for cp in _expert_weight_copies(e, slot, hbm, stage, sem):
            cp.wait()
        wg_bf[...] = sg[slot].astype(BF16)
        wu_bf[...] = su[slot].astype(BF16)
        wd_bf[...] = sd[slot].astype(BF16)
        nxt = nxt_ref[e]

        @pl.when(nxt != e)
        def _():
            for cp in _expert_weight_copies(nxt, 1 - slot, hbm, stage, sem):
                cp.start()

        slot_scr[0] = 1 - slot

    @pl.when(i < nt_ref[0])
    def _():
        x = xs_ref[...]
        hidden = (jax.nn.silu(_dot(x, wg_bf[...])) * _dot(x, wu_bf[...])).astype(BF16)
        ys_ref[...] = _dot(hidden, wd_bf[...]).astype(BF16)

    @pl.when(i >= nt_ref[0])
    def _():
        ys_ref[...] = jnp.zeros_like(ys_ref)


def _experts(tile_expert, n_tiles, next_expert, x_sorted, w_gate, w_up, w_down):
    n_sorted, dm = x_sorted.shape
    de = w_gate.shape[2]
    tm = TM_EXPERT
    return pl.pallas_call(
        _experts_kernel,
        grid_spec=pltpu.PrefetchScalarGridSpec(
            num_scalar_prefetch=3,
            grid=(n_sorted // tm,),
            in_specs=[pl.BlockSpec((tm, dm), lambda i, te, nt, nx: (jnp.minimum(i, nt[0] - 1), 0)),
                      pl.BlockSpec(memory_space=pl.ANY), pl.BlockSpec(memory_space=pl.ANY),
                      pl.BlockSpec(memory_space=pl.ANY)],
            out_specs=pl.BlockSpec((tm, dm), lambda i, te, nt, nx: (i, 0)),
            scratch_shapes=[pltpu.VMEM((2, dm, de), F32), pltpu.VMEM((2, dm, de), F32), pltpu.VMEM((2, de, dm), F32),
                            pltpu.VMEM((dm, de), BF16), pltpu.VMEM((dm, de), BF16), pltpu.VMEM((de, dm), BF16),
                            pltpu.SMEM((1,), jnp.int32), pltpu.SemaphoreType.DMA((2,))],
        ),
        out_shape=jax.ShapeDtypeStruct((n_sorted, dm), BF16),
        compiler_params=pltpu.CompilerParams(dimension_semantics=("arbitrary",),
                                             vmem_limit_bytes=VMEM_LIMIT),
        name="experts",
    )(tile_expert, n_tiles, next_expert, x_sorted, w_gate, w_up, w_down)


def _combine_tile(x2_ref, route_ref, lbase, gfin_ref, local, o_ref):
    tm = x2_ref.shape[0]
    s_rows = local.shape[0]
    route = route_ref[...]
    lp1, lp2 = _local_positions(route, lbase)
    w1, w2 = route[:, R_W1:R_W1 + 1], route[:, R_W2:R_W2 + 1]
    col0 = lax.broadcasted_iota(jnp.int32, (tm, KB), 1).astype(F32)
    moe = None
    for k in range(s_rows // KB):
        col = col0 + float(k * KB)
        pick = (jnp.where(col == lp1, w1, 0.0) + jnp.where(col == lp2, w2, 0.0)).astype(BF16)
        part = _dot(pick, local[k * KB:(k + 1) * KB, :])
        moe = part if moe is None else moe + part
    yield
    o_ref[...] = _rms(x2_ref[...] + moe, gfin_ref[...])


def _combine_kernel(dst_ref, nq_ref, tpos_ref, tn_ref, x2_ref, route_ref, lbase_ref, gfin_ref, ys_ref,
                    o_ref, local_scr, cnt_scr, sem):
    del tpos_ref, tn_ref
    i = pl.program_id(0)
    n = pl.num_programs(0)
    slot = i % 2
    per_step = local_scr.shape[1]
    s_rows = local_scr.shape[2]

    def fetch(step, s):
        for k in range(per_step):
            tile = step * per_step + k

            def clear(r, carry, k=k):
                local_scr[s, k, pl.ds(pl.multiple_of(r * SEG_ALIGN, SEG_ALIGN), SEG_ALIGN), :] = jnp.zeros(
                    (SEG_ALIGN, local_scr.shape[3]), BF16)
                return carry

            lax.fori_loop(nq_ref[tile], s_rows // SEG_ALIGN, clear, 0)
            cnt_scr[s * per_step + k] = _segment_copies(tile, dst_ref, nq_ref, local_scr.at[s, k], ys_ref,
                                                        sem.at[s * per_step + k], False)

    @pl.when(i == 0)
    def _():
        fetch(0, 0)

    @pl.when(i + 1 < n)
    def _():
        fetch(i + 1, 1 - slot)

    tiles = []
    for k in range(per_step):
        c = slot * per_step + k
        _wait_chunks(cnt_scr[c], local_scr.at[slot, k], ys_ref, sem.at[c])
        rows = slice(k * TM_MIX, (k + 1) * TM_MIX)
        tiles.append(_combine_tile(x2_ref.at[rows, :], route_ref.at[rows, :], lbase_ref[k], gfin_ref,
                                   local_scr.at[slot, k], o_ref.at[rows, :]))
    _lockstep(tiles)


def _combine(plan, x2, route, lbase_f, gfin, y_sorted):
    t, dm = x2.shape
    tm = TM_MIX * MOE_TILES
    s_rows = _local_rows(TM_MIX)
    im = lambda i, *_: (i, 0)
    return pl.pallas_call(
        _combine_kernel,
        grid_spec=pltpu.PrefetchScalarGridSpec(
            num_scalar_prefetch=4,
            grid=(t // tm,),
            in_specs=[pl.BlockSpec((tm, dm), im), pl.BlockSpec((tm, LANES), im),
                      pl.BlockSpec((MOE_TILES, 1, LANES), lambda i, *_: (i, 0, 0)),
                      pl.BlockSpec((1, dm), lambda i, *_: (0, 0)),
                      pl.BlockSpec(memory_space=pl.ANY)],
            out_specs=pl.BlockSpec((tm, dm), im),
            scratch_shapes=[pltpu.VMEM((2, MOE_TILES, s_rows, dm), BF16), pltpu.SMEM((2 * MOE_TILES,), jnp.int32),
                            pltpu.SemaphoreType.DMA((2 * MOE_TILES,))],
        ),
        out_shape=jax.ShapeDtypeStruct((t, dm), F32),
        compiler_params=pltpu.CompilerParams(dimension_semantics=("arbitrary",),
                                             vmem_limit_bytes=VMEM_LIMIT),
        name="combine_norm",
    )(*_plan_specs(plan), x2, route, lbase_f, gfin, y_sorted)


def _layer(x, p, s5_ops, gfin):
    b, l, dm = x.shape
    x2d = x.reshape(b * l, dm)
    mgm, us5 = _inproj_gmlp(x2d, p['gmix'], p['win'], p['lng'], p['lnb'], p['ws'], p['bs'], p['gout_gm'])
    n_seg = SUBLANES // b
    *lag_factors, w1, w2, sc = s5_ops[(l // (S5_LC * n_seg))]
    xg = _s5_inproj(x, p['gmix'], p['win_s5'], n_seg)
    yg = _s5_scan(xg, lag_factors, w1, w2, sc, n_seg)
    ys = _s5_to_tokens(yg, b, l, n_seg)
    x2, t_bf, route, counts = _mix_route(ys, us5, mgm, x2d, p['d'], p['gluw'], p['glub'], p['gout_s5'],
                                         p['wout'], p['gffn'], p['rwh'], p['rwl'], p['rb'])
    plan, lbase_f, tiles, n_sorted = _segment_plan(counts, b * l, TM_EXPERT)
    x_sorted = _sort_rows(plan, tiles[1], t_bf, route, lbase_f, n_sorted)
    y_sorted = _experts(*tiles, x_sorted, p['w_gate'], p['w_up'], p['w_down'])
    out = _combine(plan, x2, route, lbase_f, gfin, y_sorted)
    return out.reshape(b, l, dm)


def kernel(x_prompt, x_sample, norm_mix_g, w_in, gm_ln_g, gm_ln_b, gm_ws, gm_bs, s5_lam_re_fwd, s5_lam_im_fwd, s5_log_step_fwd, s5_b_re_fwd, s5_b_im_fwd, s5_c_re_fwd, s5_c_im_fwd, s5_lam_re_bwd, s5_lam_im_bwd, s5_log_step_bwd, s5_b_re_bwd, s5_b_im_bwd, s5_c_re_bwd, s5_c_im_bwd, s5_d, s5_glu_w, s5_glu_b, out_norm_gm, out_norm_s5, w_out, norm_ffn_g, r1_w, r1_b, r2_w, r2_b, e_w_gate, e_w_up, e_w_down, norm_final_g):
    depth = w_in.shape[0]
    gfin = norm_final_g.reshape(1, -1).astype(F32)
    xs = [x_prompt, x_sample]
    for li in range(depth):
        row = lambda a: a[li].reshape(1, -1).astype(F32)
        dm = w_in.shape[1]
        gw = gm_ln_g.shape[1]
        hd_dim = gw // GM_HEADS
        rw = jnp.concatenate([r1_w[li], r2_w[li].transpose(1, 0, 2).reshape(dm, N_EXPERTS)], axis=1).astype(F32)
        rw = jnp.pad(rw, ((0, 0), (0, LANES - rw.shape[1])))
        rwh = rw.astype(BF16)
        rwl = (rw - rwh.astype(F32)).astype(BF16)
        rb = jnp.concatenate([r1_b[li], r2_b[li].reshape(-1)]).astype(F32)
        rb = jnp.pad(rb, (0, LANES - rb.shape[0])).reshape(1, LANES)
        p = dict(
            gmix=row(norm_mix_g), win=w_in[li].astype(BF16), win_s5=w_in[li][:, 2 * gw:].astype(BF16),
            lng=row(gm_ln_g), lnb=row(gm_ln_b),
            ws=gm_ws[li].astype(BF16),
            bs=jnp.broadcast_to(gm_bs[li].astype(F32)[:, :, None], (GM_HEADS, CHUNK, hd_dim)),
            gout_gm=row(out_norm_gm), d=row(s5_d), gluw=s5_glu_w[li].astype(BF16), glub=row(s5_glu_b),
            gout_s5=row(out_norm_s5), wout=w_out[li].astype(BF16), gffn=row(norm_ffn_g),
            rwh=rwh, rwl=rwl, rb=rb,
            w_gate=e_w_gate[li], w_up=e_w_up[li], w_down=e_w_down[li],
        )
        fwd = (s5_lam_re_fwd[li], s5_lam_im_fwd[li], s5_log_step_fwd[li], s5_b_re_fwd[li], s5_b_im_fwd[li],
               s5_c_re_fwd[li], s5_c_im_fwd[li])
        bwd = (s5_lam_re_bwd[li], s5_lam_im_bwd[li], s5_log_step_bwd[li], s5_b_re_bwd[li], s5_b_im_bwd[li],
               s5_c_re_bwd[li], s5_c_im_bwd[li])
        s5_ops = {}
        for x in xs:
            seg_steps = x.shape[1] // (S5_LC * (SUBLANES // x.shape[0]))
            if seg_steps not in s5_ops:
                s5_ops[seg_steps] = _s5_operator(fwd, bwd, S5_LC, seg_steps)
        last = li == depth - 1
        assert last, "depth > 1 needs an un-normalised layer output"
        xs = [_layer(x, p, s5_ops, gfin) for x in xs]
    return tuple(xs)
```

```python
import functools
import math

import jax
import jax.numpy as jnp
from jax import lax
from jax.experimental import pallas as pl
from jax.experimental.pallas import tpu as pltpu

F32 = jnp.float32
BF16 = jnp.bfloat16

EPS = 1e-6
LAMBDA_RE_MAX = -1e-4
GM_HEADS = 4
CHUNK = 128
S5_GROUP = 16
S5_STATE = 64
N_COARSE = 4
N_FINE = 8
N_EXPERTS = N_COARSE * N_FINE

LANES = 128
SUBLANES = 8
S5_LC = 16
VMEM_LIMIT = 56 * 1024 * 1024

TM_PROJ = 1024
TM_MIX = 512
MIX_TILES = 2
KB = 256
TM_EXPERT = 1024
SEG_ALIGN = 16


def _gelu(x):
    c = math.sqrt(2.0 / math.pi)
    half = 0.5 * x
    return half + half * jnp.tanh(x * (c + (c * 0.044715) * (x * x)))


def _rms(x, g):
    ms = jnp.mean(x * x, axis=-1, keepdims=True)
    return x * lax.rsqrt(ms + EPS) * g


def _dot(a, b):
    return jnp.dot(a, b, preferred_element_type=F32)


def _lockstep(tiles):
    while tiles:
        tiles = [t for t in tiles if next(t, "done") != "done"]


def _inproj_gmlp_kernel(x_ref, gmix_ref, win_ref, lng_ref, lnb_ref, ws_ref, bs_ref, gout_ref,
                        mgm_ref, us5_ref, y_scr):
    tm = x_ref.shape[0]
    gw = mgm_ref.shape[1]
    hd_dim = gw // GM_HEADS
    n_chunks = tm // CHUNK
    h = _rms(x_ref[...], gmix_ref[...]).astype(BF16)
    proj = _dot(h, win_ref[...])
    us5_ref[...] = proj[:, 2 * gw:]
    u = _gelu(proj[:, :gw])
    v = _gelu(proj[:, gw:2 * gw])
    for hd in range(GM_HEADS):
        lo = hd * hd_dim
        vh = v[:, lo:lo + hd_dim]
        mu = jnp.mean(vh, axis=-1, keepdims=True)
        xc = vh - mu
        var = jnp.mean(xc * xc, axis=-1, keepdims=True)
        vn = (xc * lax.rsqrt(var + EPS) * lng_ref[:, lo:lo + hd_dim]
              + lnb_ref[:, lo:lo + hd_dim]).astype(BF16)
        rhs = jnp.concatenate([vn[c * CHUNK:(c + 1) * CHUNK] for c in range(n_chunks)], axis=1)
        s = _dot(ws_ref[hd], rhs)
        for c in range(n_chunks):
            sc = s[:, c * hd_dim:(c + 1) * hd_dim] + bs_ref[hd]
            y_scr[c * CHUNK:(c + 1) * CHUNK, lo:lo + hd_dim] = u[c * CHUNK:(c + 1) * CHUNK, lo:lo + hd_dim] * sc
    mgm_ref[...] = _rms(y_scr[...], gout_ref[...]).astype(BF16)


def _inproj_gmlp(x2d, gmix, win_bf, lng, lnb, ws_bf, bs_b, gout):
    t, d = x2d.shape
    d_in = win_bf.shape[1]
    gw = lng.shape[1]
    s5w = d_in - 2 * gw
    tm = TM_PROJ
    const = lambda *shape: pl.BlockSpec(shape, lambda i: (0,) * len(shape))
    return pl.pallas_call(
        _inproj_gmlp_kernel,
        grid=(t // tm,),
        in_specs=[
            pl.BlockSpec((tm, d), lambda i: (i, 0)),
            const(1, d), const(d, d_in), const(1, gw), const(1, gw),
            const(GM_HEADS, CHUNK, CHUNK), const(GM_HEADS, CHUNK, gw // GM_HEADS), const(1, gw),
        ],
        out_specs=[pl.BlockSpec((tm, gw), lambda i: (i, 0)),
                   pl.BlockSpec((tm, s5w), lambda i: (i, 0))],
        out_shape=[jax.ShapeDtypeStruct((t, gw), BF16),
                   jax.ShapeDtypeStruct((t, s5w), F32)],
        scratch_shapes=[pltpu.VMEM((tm, gw), F32)],
        compiler_params=pltpu.CompilerParams(dimension_semantics=("parallel",),
                                             vmem_limit_bytes=VMEM_LIMIT),
        name="inproj_gmlp",
    )(x2d, gmix, win_bf, lng, lnb, ws_bf, bs_b, gout)


def _s5_consts(lam_re, lam_im, log_step, b_re, b_im, c_re, c_im, lc):
    lr = jnp.minimum(lam_re.astype(F32), LAMBDA_RE_MAX)
    li = lam_im.astype(F32)
    step = jnp.exp(log_step.astype(F32))[:, None]
    dr, di = lr * step, li * step
    ar, ai = _cexp(dr, di)
    nr, ni = ar - 1.0, ai
    den = lr * lr + li * li
    qr, qi = (nr * lr + ni * li) / den, (ni * lr - nr * li) / den
    br, bi = b_re.astype(F32), b_im.astype(F32)
    bbr = qr[..., None] * br - qi[..., None] * bi
    bbi = qr[..., None] * bi + qi[..., None] * br
    k = jnp.arange(lc + 1, dtype=F32)[:, None, None]
    pwr, pwi = _cexp(k * dr[None], k * di[None])
    return (dr, di), (pwr, pwi), (bbr, bbi), (c_re.astype(F32), c_im.astype(F32))


def _cexp(zr, zi):
    m = jnp.exp(zr)
    return m * jnp.cos(zi), m * jnp.sin(zi)


def _s5_operator(fwd, bwd, lc, seg_steps):
    consts = [_s5_consts(*fwd, lc), _s5_consts(*bwd, lc)]
    g, p, h = consts[0][2][0].shape
    lags, w1_parts, w2_parts, sc_rows, seg_rows = [], [], [], [], []
    for direction, (ld, pw, bb, c) in enumerate(consts):
        (dr, di), (pwr, pwi), (bbr, bbi), (cr, ci) = ld, pw, bb, c
        crt, cit = cr.transpose(0, 2, 1), ci.transpose(0, 2, 1)
        pwrt, pwit = pwr.transpose(1, 2, 0), pwi.transpose(1, 2, 0)
        cpr = crt[:, :, None, :] * pwrt[:, :, :, None] - cit[:, :, None, :] * pwit[:, :, :, None]
        cpi = crt[:, :, None, :] * pwit[:, :, :, None] + cit[:, :, None, :] * pwrt[:, :, :, None]
        ck = jnp.concatenate([cpr[:, :, :lc], cpi[:, :, :lc]], axis=1)
        if direction == 1:
            ck = jnp.flip(ck, 2)
        lags += [jnp.concatenate([bbr.transpose(0, 2, 1), -bbi.transpose(0, 2, 1)], axis=-1),
                 ck.reshape(g, 2 * p, lc * h)]
        er, ei = pwrt[:, :, :lc].transpose(0, 2, 1), pwit[:, :, :lc].transpose(0, 2, 1)
        if direction == 0:
            er, ei = jnp.flip(er, 1), jnp.flip(ei, 1)
        bbrt, bbit = bbr.transpose(0, 2, 1), bbi.transpose(0, 2, 1)
        e1 = jnp.concatenate([er, er], -1)[:, :, None, :]
        e2 = jnp.concatenate([-ei, ei], -1)[:, :, None, :]
        b_ri = jnp.concatenate([bbrt, bbit], -1)[:, None]
        b_ir = jnp.concatenate([bbit, bbrt], -1)[:, None]
        w1_parts += [e1 * b_ri + e2 * b_ir, e1 * b_ir - e2 * b_ri]
        fr, fi = cpr[:, :, 1:lc + 1], cpi[:, :, 1:lc + 1]
        if direction == 1:
            fr, fi = jnp.flip(fr, 2), jnp.flip(fi, 2)
        w2_parts += [fr, -fi]

        def mult(zr, zi):
            return [jnp.concatenate([zr, zr], -1), jnp.concatenate([-zi, zi], -1)]

        sc_rows += mult(*_cexp(lc * dr, lc * di))
        seg_rows += mult(*_cexp((lc * seg_steps) * dr, (lc * seg_steps) * di))
    w1 = jnp.concatenate(w1_parts, axis=-1).reshape(g, lc * h, 8 * p)
    w2 = jnp.concatenate(w2_parts, axis=1).reshape(g, 4 * p, lc * h)
    sc = jnp.stack(sc_rows + seg_rows, axis=1)
    return tuple(lags) + (w1.astype(BF16), w2.astype(BF16), sc.astype(F32))


S5_GPS = 2


def _s5_kernel(x_ref, bbf_ref, cpf_ref, bbb_ref, cpb_ref, w1_ref, w2_ref, sc_ref, y_ref,
               loc_scr, sin_scr, m_scr, *, n_seg):
    gps, rows, kw = x_ref.shape
    steps = rows // SUBLANES
    sw = sc_ref.shape[2]

    for gi in range(gps):
        kf = jnp.dot(bbf_ref[gi], cpf_ref[gi], precision=lax.Precision.HIGHEST, preferred_element_type=F32)
        kb = jnp.dot(bbb_ref[gi], cpb_ref[gi], precision=lax.Precision.HIGHEST, preferred_element_type=F32)
        hch = kf.shape[0]
        lc = kw // hch
        lane = lax.broadcasted_iota(jnp.int32, kf.shape, 1)
        for s in range(lc):
            f = kf if s == 0 else jnp.where(lane >= s * hch, pltpu.roll(kf, s * hch, 1), 0.0)
            left = (lc - 1 - s) * hch
            b = kb if left == 0 else pltpu.roll(kb, kw - left, 1)
            m_scr[gi, s * hch:(s + 1) * hch, :] = (f + jnp.where(lane < (s + 1) * hch, b, 0.0)).astype(BF16)
        loc_scr[gi] = _dot(x_ref[gi], w1_ref[gi])

    def bc(gi, i):
        return jnp.broadcast_to(sc_ref[gi, i:i + 1, :], (SUBLANES, sw))

    mult = [[bc(gi, i) for i in range(8)] for gi in range(gps)]

    def step(gi, s, state):
        f, fs, b, bs = state
        a1f, a2f, a1b, a2b = mult[gi][:4]
        rf = pl.multiple_of(s * SUBLANES, SUBLANES)
        rb = pl.multiple_of((steps - 1 - s) * SUBLANES, SUBLANES)
        lf = loc_scr[gi, pl.ds(rf, SUBLANES), 0:sw]
        lfs = loc_scr[gi, pl.ds(rf, SUBLANES), sw:2 * sw]
        lb = loc_scr[gi, pl.ds(rb, SUBLANES), 2 * sw:3 * sw]
        lbs = loc_scr[gi, pl.ds(rb, SUBLANES), 3 * sw:4 * sw]
        return (a1f * f + a2f * fs + lf, a1f * fs - a2f * f + lfs,
                a1b * b + a2b * bs + lb, a1b * bs - a2b * b + lbs)

    zero = jnp.zeros((SUBLANES, sw), F32)

    def pass1(s, carry):
        return tuple(step(gi, s, carry[gi]) for gi in range(gps))

    ends = lax.fori_loop(0, steps, pass1, tuple((zero,) * 4 for _ in range(gps)), unroll=4)

    seg = lax.broadcasted_iota(jnp.int32, (SUBLANES, sw), 0) % n_seg
    enter = []
    for gi in range(gps):
        f_end, fs_end, b_end, bs_end = ends[gi]
        p1f, p2f, p1b, p2b = mult[gi][4:]
        cf, cfs, cb, cbs = zero, zero, zero, zero
        for _ in range(n_seg - 1):
            ef = f_end + p1f * cf + p2f * cfs
            efs = fs_end + p1f * cfs - p2f * cf
            eb = b_end + p1b * cb + p2b * cbs
            ebs = bs_end + p1b * cbs - p2b * cb
            cf = jnp.where(seg >= 1, pltpu.roll(ef, 1, 0), 0.0)
            cfs = jnp.where(seg >= 1, pltpu.roll(efs, 1, 0), 0.0)
            cb = jnp.where(seg <= n_seg - 2, pltpu.roll(eb, SUBLANES - 1, 0), 0.0)
            cbs = jnp.where(seg <= n_seg - 2, pltpu.roll(ebs, SUBLANES - 1, 0), 0.0)
        enter.append((cf, cfs, cb, cbs))

    def pass2(s, carry):
        rf = pl.multiple_of(s * SUBLANES, SUBLANES)
        rb = pl.multiple_of((steps - 1 - s) * SUBLANES, SUBLANES)
        for gi in range(gps):
            sin_scr[gi, pl.ds(rf, SUBLANES), 0:sw] = carry[gi][0]
            sin_scr[gi, pl.ds(rb, SUBLANES), sw:2 * sw] = carry[gi][2]
        return tuple(step(gi, s, carry[gi]) for gi in range(gps))

    lax.fori_loop(0, steps, pass2, tuple(enter), unroll=4)

    for gi in range(gps):
        y_ref[gi] = _dot(x_ref[gi], m_scr[gi]) + _dot(sin_scr[gi].astype(BF16), w2_ref[gi])


def _s5_scan(xg, lag_factors, w1, w2, sc, n_seg):
    g, rows, kw = xg.shape
    sw = sc.shape[2]
    gps = S5_GPS
    blk = lambda a: pl.BlockSpec((gps,) + a.shape[1:], lambda i: (i, 0, 0))
    return pl.pallas_call(
        functools.partial(_s5_kernel, n_seg=n_seg),
        grid=(g // gps,),
        in_specs=[blk(xg)] + [blk(a) for a in lag_factors] + [blk(w1), blk(w2), blk(sc)],
        out_specs=pl.BlockSpec((gps, rows, kw), lambda i: (i, 0, 0)),
        out_shape=jax.ShapeDtypeStruct((g, rows, kw), F32),
        scratch_shapes=[pltpu.VMEM((gps, rows, 4 * sw), F32), pltpu.VMEM((gps, rows, 2 * sw), F32),
                        pltpu.VMEM((gps, kw, kw), BF16)],
        compiler_params=pltpu.CompilerParams(dimension_semantics=("parallel",),
                                             vmem_limit_bytes=VMEM_LIMIT),
        name="s5_scan",
    )(xg, *lag_factors, w1, w2, sc)


S5_NM = 16


def _block_transpose8(v, width):
    lane = lax.broadcasted_iota(jnp.int32, v[0].shape, 1)
    for d in (4, 2, 1):
        w = width * d
        hi = ((lane // w) % 2) == 1
        out = list(v)
        for i0 in range(8):
            if i0 & d:
                continue
            i1 = i0 + d
            out[i0] = jnp.where(hi, pltpu.roll(v[i1], w, 1), v[i0])
            out[i1] = jnp.where(hi, v[i1], pltpu.roll(v[i0], 8 * width - w, 1))
        v = out
    return v


def _tile_copies(hbm4, tile, buf, slot, sem, nm, to_hbm):
    copies = []
    for c in range(SUBLANES):
        for j in range(S5_LC):
            h = hbm4.at[c, pl.ds(tile * nm, nm), pl.ds(j, 1), :]
            v = buf.at[slot, j, :, pl.ds(c, 1), :]
            copies.append(pltpu.make_async_copy(v, h, sem.at[slot]) if to_hbm
                          else pltpu.make_async_copy(h, v, sem.at[slot]))
    return copies


def _s5_inproj_kernel(x4_ref, gmix_ref, w_ref, xg_ref, xs, sem, *, nm):
    i = pl.program_id(0)
    n = pl.num_programs(0)
    slot = i % 2
    dm = x4_ref.shape[3]

    @pl.when(i == 0)
    def _():
        for cp in _tile_copies(x4_ref, 0, xs, 0, sem, nm, False):
            cp.start()

    @pl.when(i + 1 < n)
    def _():
        for cp in _tile_copies(x4_ref, i + 1, xs, 1 - slot, sem, nm, False):
            cp.start()

    pltpu.make_async_copy(xs.at[slot], xs.at[slot], sem.at[slot]).wait()
    rows = nm * SUBLANES
    x = xs[slot].reshape(S5_LC * rows, dm)
    z = _dot(_rms(x, gmix_ref[...]).astype(BF16), w_ref[...])
    n_oct = z.shape[1] // LANES
    for q in range(n_oct):
        for a in range(S5_LC // 8):
            blocks = [z[(8 * a + j8) * rows:(8 * a + j8 + 1) * rows, q * LANES:(q + 1) * LANES] for j8 in range(8)]
            for g8, b in enumerate(_block_transpose8(blocks, S5_GROUP)):
                xg_ref[8 * q + g8, :, a * LANES:(a + 1) * LANES] = b.astype(BF16)


def _s5_inproj(x, gmix, w_s5_bf, n_seg):
    b, l, dm = x.shape
    steps = l // (S5_LC * n_seg)
    nm = S5_NM
    s5w = w_s5_bf.shape[1]
    g = s5w // S5_GROUP
    x4 = x.reshape(b * n_seg, steps, S5_LC, dm)
    return pl.pallas_call(
        functools.partial(_s5_inproj_kernel, nm=nm),
        grid=(steps // nm,),
        in_specs=[pl.BlockSpec(memory_space=pl.ANY),
                  pl.BlockSpec((1, dm), lambda i: (0, 0)),
                  pl.BlockSpec((dm, s5w), lambda i: (0, 0))],
        out_specs=pl.BlockSpec((g, nm * SUBLANES, S5_LC * S5_GROUP), lambda i: (0, i, 0)),
        out_shape=jax.ShapeDtypeStruct((g, steps * SUBLANES, S5_LC * S5_GROUP), BF16),
        scratch_shapes=[pltpu.VMEM((2, S5_LC, nm, SUBLANES, dm), F32), pltpu.SemaphoreType.DMA((2,))],
        compiler_params=pltpu.CompilerParams(dimension_semantics=("arbitrary",),
                                             vmem_limit_bytes=VMEM_LIMIT),
        name="s5_inproj",
    )(x4, gmix, w_s5_bf)


def _s5_to_tokens_kernel(yg_ref, ys4_ref, zs, sem, *, nm):
    i = pl.program_id(0)
    n = pl.num_programs(0)
    slot = i % 2
    rows = nm * SUBLANES

    def wait(s):
        pltpu.make_async_copy(zs.at[s], zs.at[s], sem.at[s]).wait()

    @pl.when(i >= 2)
    def _():
        wait(slot)

    n_oct = yg_ref.shape[0] // 8
    for q in range(n_oct):
        for a in range(S5_LC // 8):
            blocks = [yg_ref[8 * q + g8, :, a * LANES:(a + 1) * LANES] for g8 in range(8)]
            for j8, b in enumerate(_block_transpose8(blocks, S5_GROUP)):
                zs[slot, 8 * a + j8, :, :, q * LANES:(q + 1) * LANES] = b.reshape(nm, SUBLANES, LANES)
    for cp in _tile_copies(ys4_ref, i, zs, slot, sem, nm, True):
        cp.start()

    @pl.when(i == n - 1)
    def _():
        wait(1 - slot)
        wait(slot)


def _s5_to_tokens(yg, b, l, n_seg):
    g, rows_total, kw = yg.shape
    steps = rows_total // SUBLANES
    nm = S5_NM
    s5w = g * S5_GROUP
    assert steps // nm >= 2
    ys4 = pl.pallas_call(
        functools.partial(_s5_to_tokens_kernel, nm=nm),
        grid=(steps // nm,),
        in_specs=[pl.BlockSpec((g, nm * SUBLANES, kw), lambda i: (0, i, 0))],
        out_specs=pl.BlockSpec(memory_space=pl.ANY),
        out_shape=jax.ShapeDtypeStruct((b * n_seg, steps, S5_LC, s5w), F32),
        scratch_shapes=[pltpu.VMEM((2, S5_LC, nm, SUBLANES, s5w), F32), pltpu.SemaphoreType.DMA((2,))],
        compiler_params=pltpu.CompilerParams(dimension_semantics=("arbitrary",),
                                             vmem_limit_bytes=VMEM_LIMIT),
        name="s5_to_tokens",
    )(yg)
    return ys4.reshape(b * l, s5w)


R_E1, R_E2, R_W1, R_W2, R_RANK1, R_RANK2 = range(6)


def _mix_route_kernel(ys_ref, us5_ref, mgm_ref, x_ref, d_ref, gluw_ref, glub_ref, gs5_ref,
                      wout_ref, gffn_ref, rwh_ref, rwl_ref, rb_ref, tri_ref,
                      x2_ref, t_ref, route_ref, cnt_ref):
    tiles = []
    for k in range(x_ref.shape[0] // TM_MIX):
        rows = slice(k * TM_MIX, (k + 1) * TM_MIX)
        tiles.append(_mix_route_tile(ys_ref.at[rows, :], us5_ref.at[rows, :], mgm_ref.at[rows, :], x_ref.at[rows, :],
                                     d_ref, gluw_ref, glub_ref, gs5_ref, wout_ref, gffn_ref, rwh_ref, rwl_ref,
                                     rb_ref, tri_ref, x2_ref.at[rows, :], t_ref.at[rows, :], route_ref.at[rows, :],
                                     cnt_ref.at[k]))
    _lockstep(tiles)


def _mix_route_tile(ys_ref, us5_ref, mgm_ref, x_ref, d_ref, gluw_ref, glub_ref, gs5_ref,
                    wout_ref, gffn_ref, rwh_ref, rwl_ref, rb_ref, tri_ref,
                    x2_ref, t_ref, route_ref, cnt_ref):
    gw = mgm_ref.shape[1]
    y = ys_ref[...] + d_ref[...] * us5_ref[...]
    g = _gelu(y)
    yield
    gate = _dot(g.astype(BF16), gluw_ref[...])
    yield
    z = g * jax.nn.sigmoid(gate + glub_ref[...])
    ms5 = _rms(z, gs5_ref[...]).astype(BF16)
    yield
    mix = _dot(mgm_ref[...], wout_ref[:gw, :]) + _dot(ms5, wout_ref[gw:, :])
    yield
    x2 = x_ref[...] + mix
    x2_ref[...] = x2
    t = _rms(x2, gffn_ref[...])
    t_hi = t.astype(BF16)
    t_ref[...] = t_hi
    t_lo = (t - t_hi.astype(F32)).astype(BF16)
    yield
    logits = (_dot(t_hi, rwh_ref[...]) + _dot(t_hi, rwl_ref[...]) + _dot(t_lo, rwh_ref[...])
              + rb_ref[...])
    yield
    lane = lax.broadcasted_iota(jnp.int32, logits.shape, 1).astype(F32)
    neg = jnp.float32(-jnp.inf)

    def first_max(mask):
        vals = jnp.where(mask, logits, neg)
        mx = jnp.max(vals, axis=-1, keepdims=True)
        idx = jnp.min(jnp.where(mask & (vals == mx), lane, float(LANES)), axis=-1, keepdims=True)
        return mx, idx

    coarse = lane < N_COARSE
    m1, grp = first_max(coarse)
    p_grp = 1.0 / jnp.sum(jnp.where(coarse, jnp.exp(logits - m1), 0.0), axis=-1, keepdims=True)
    lo = N_COARSE + grp * N_FINE
    fine = (lane >= lo) & (lane < lo + N_FINE)
    v1, i1 = first_max(fine)
    v2, i2 = first_max(fine & (lane != i1))
    e21 = jnp.exp(v2 - v1)
    w1 = p_grp / (1.0 + e21)
    w2 = p_grp * e21 / (1.0 + e21)
    e1 = i1 - N_COARSE
    e2 = i2 - N_COARSE
    hit1 = lane == e1
    hit2 = lane == e2
    onehot = jnp.where(hit1 | hit2, 1.0, 0.0)
    before = _dot(tri_ref[...], onehot.astype(BF16))
    rank1 = jnp.sum(jnp.where(hit1, before, 0.0), axis=-1, keepdims=True)
    rank2 = jnp.sum(jnp.where(hit2, before, 0.0), axis=-1, keepdims=True)
    tm = onehot.shape[0]
    cnt_ref[...] = before[tm - 1:tm, :] + onehot[tm - 1:tm, :]
    rec = jnp.zeros_like(logits)
    for slot, val in ((R_E1, e1), (R_E2, e2), (R_W1, w1), (R_W2, w2),
                      (R_RANK1, rank1), (R_RANK2, rank2)):
        rec = jnp.where(lane == slot, val, rec)
    route_ref[...] = rec


def _mix_route(ys, us5, mgm, x2d, d, gluw_bf, glub, gs5, wout_bf, gffn, rwh, rwl, rb):
    t, dm = x2d.shape
    gw = mgm.shape[1]
    s5w = us5.shape[1]
    tm = TM_MIX * MIX_TILES
    tri = jnp.tril(jnp.ones((TM_MIX, TM_MIX), F32), -1).astype(BF16)
    const = lambda *shape: pl.BlockSpec(shape, lambda i: (0,) * len(shape))
    tile = lambda w: pl.BlockSpec((tm, w), lambda i: (i, 0))
    return pl.pallas_call(
        _mix_route_kernel,
        grid=(t // tm,),
        in_specs=[tile(s5w), tile(s5w), tile(gw), tile(dm),
                  const(1, s5w), const(s5w, s5w), const(1, s5w), const(1, s5w),
                  const(gw + s5w, dm), const(1, dm), const(dm, LANES), const(dm, LANES), const(1, LANES),
                  const(TM_MIX, TM_MIX)],
        out_specs=[tile(dm), tile(dm), tile(LANES), pl.BlockSpec((MIX_TILES, 1, LANES), lambda i: (i, 0, 0))],
        out_shape=[jax.ShapeDtypeStruct((t, dm), F32),
                   jax.ShapeDtypeStruct((t, dm), BF16),
                   jax.ShapeDtypeStruct((t, LANES), F32),
                   jax.ShapeDtypeStruct((t // TM_MIX, 1, LANES), F32)],
        compiler_params=pltpu.CompilerParams(dimension_semantics=("parallel",),
                                             vmem_limit_bytes=VMEM_LIMIT),
        name="mix_route",
    )(ys, us5, mgm, x2d, d, gluw_bf, glub, gs5, wout_bf, gffn, rwh, rwl, rb, tri)


def _local_rows(tm):
    worst = 2 * tm + N_EXPERTS * (SEG_ALIGN - 1)
    return -(-worst // LANES) * LANES


def _segment_plan(cnt, t, tm_expert):
    c = cnt[:, 0, :N_EXPERTS].astype(jnp.int32)
    n_tok_tiles = c.shape[0]
    al = (c + SEG_ALIGN - 1) // SEG_ALIGN * SEG_ALIGN
    lbase = jnp.cumsum(al, axis=1) - al
    tot = jnp.sum(al, axis=0)
    tot_pad = (tot + tm_expert - 1) // tm_expert * tm_expert
    gbase = jnp.cumsum(tot_pad) - tot_pad
    gpos = gbase[None, :] + jnp.cumsum(al, axis=0) - al
    n_tiles_max = -(-(2 * t + n_tok_tiles * N_EXPERTS * (SEG_ALIGN - 1)) // tm_expert) + N_EXPERTS
    tile_end = jnp.cumsum(tot_pad // tm_expert)
    n_tiles = tile_end[-1:].astype(jnp.int32)
    tile_idx = jnp.arange(n_tiles_max, dtype=jnp.int32)
    tile_expert = jnp.sum((tile_idx[:, None] >= tile_end[None, :]).astype(jnp.int32), axis=1)
    last = jnp.sum((n_tiles - 1 >= tile_end).astype(jnp.int32))
    tile_expert = jnp.where(tile_idx < n_tiles, tile_expert, last).astype(jnp.int32)
    ids = jnp.arange(N_EXPERTS, dtype=jnp.int32)
    later_used = (ids[None, :] > ids[:, None]) & (tot_pad[None, :] > 0)
    next_expert = jnp.min(jnp.where(later_used, ids[None, :], N_EXPERTS), axis=1)
    next_expert = jnp.where(next_expert == N_EXPERTS, ids, next_expert).astype(jnp.int32)
    lbase_f = jnp.pad(lbase.astype(F32), ((0, 0), (0, LANES - N_EXPERTS)))[:, None, :]
    flat = lambda a: a.reshape(-1).astype(jnp.int32)
    nch = al // SEG_ALIGN
    cum = jnp.cumsum(nch, axis=1)
    q = jnp.arange(_local_rows(TM_MIX) // SEG_ALIGN, dtype=jnp.int32)[None, :, None]
    seg_of_q = jnp.sum((q >= cum[:, None, :]).astype(jnp.int32), axis=2)
    in_seg = seg_of_q[:, :, None] == jnp.arange(N_EXPERTS, dtype=jnp.int32)[None, None, :]
    pick = lambda a: jnp.sum(jnp.where(in_seg, a[:, None, :], 0), axis=2)
    dst = pick(gpos) + (q[:, :, 0] - pick(cum - nch)) * SEG_ALIGN
    plan = dict(dst=flat(dst), n_chunks=flat(cum[:, -1]),
                tail_pos=flat(gbase + tot), tail_n=flat((tot_pad - tot) // SEG_ALIGN))
    return plan, lbase_f, (tile_expert, n_tiles, next_expert), n_tiles_max * tm_expert


def _local_positions(route, lbase):
    lane = lax.broadcasted_iota(jnp.int32, route.shape, 1).astype(F32)
    out = []
    for e_lane, r_lane in ((R_E1, R_RANK1), (R_E2, R_RANK2)):
        e = route[:, e_lane:e_lane + 1]
        base = jnp.sum(jnp.where(lane == e, lbase, 0.0), axis=-1, keepdims=True)
        out.append(base + route[:, r_lane:r_lane + 1])
    return out


WAIT_GROUP = 8
ISSUE_GROUP = 4


def _segment_copies(i, dst_ref, nq_ref, local, glob, sem, to_global):
    per_tile = local.shape[0] // SEG_ALIGN
    n = nq_ref[i]

    def start(q):
        lo = local.at[pl.ds(pl.multiple_of(q * SEG_ALIGN, SEG_ALIGN), SEG_ALIGN)]
        gl = glob.at[pl.ds(pl.multiple_of(dst_ref[i * per_tile + q], SEG_ALIGN), SEG_ALIGN)]
        (pltpu.make_async_copy(lo, gl, sem) if to_global else pltpu.make_async_copy(gl, lo, sem)).start()

    def group(k, carry):
        for u in range(ISSUE_GROUP):
            start(k * ISSUE_GROUP + u)
        return carry

    def single(q, carry):
        start(q)
        return carry

    full = n // ISSUE_GROUP
    lax.fori_loop(0, full, group, 0)
    lax.fori_loop(full * ISSUE_GROUP, n, single, 0)
    return n


def _wait_chunks(n, local, glob, sem):
    def wait_rows(rows):
        def one(c, carry):
            pltpu.make_async_copy(local.at[pl.ds(0, rows)], glob.at[pl.ds(0, rows)], sem).wait()
            return carry
        return one

    lax.fori_loop(0, n // WAIT_GROUP, wait_rows(WAIT_GROUP * SEG_ALIGN), 0)
    lax.fori_loop(0, n % WAIT_GROUP, wait_rows(SEG_ALIGN), 0)


def _sort_tile(t_ref, route_ref, lbase, local):
    tm = t_ref.shape[0]
    s_rows = local.shape[0]
    lp1, lp2 = _local_positions(route_ref[...], lbase)
    lane = lax.broadcasted_iota(jnp.int32, (tm, LANES), 1)
    lp_rows = jnp.where(lane == 0, lp1, jnp.where(lane == 1, lp2, -1.0)).T
    row = lax.broadcasted_iota(jnp.int32, (s_rows, tm), 0).astype(F32)
    onehot = jnp.where((row == lp_rows[0:1, :]) | (row == lp_rows[1:2, :]), 1.0, 0.0).astype(BF16)
    yield
    local[...] = _dot(onehot, t_ref[...]).astype(BF16)


def _sort_rows_kernel(dst_ref, nq_ref, tpos_ref, tn_ref, nt_ref, t_ref, route_ref, lbase_ref,
                      xs_ref, local_scr, zero_scr, cnt_scr, sem, zsem):
    i = pl.program_id(0)
    n = pl.num_programs(0)
    slot = i % 2
    per_step = local_scr.shape[1]

    def buf(s, k):
        return local_scr.at[s, k], sem.at[s * per_step + k], s * per_step + k

    def wait_slot(s):
        for k in range(per_step):
            local, sm, c = buf(s, k)
            _wait_chunks(cnt_scr[c], local, xs_ref, sm)

    @pl.when(i >= 2)
    def _():
        wait_slot(slot)

    tiles = []
    for k in range(per_step):
        rows = slice(k * TM_MIX, (k + 1) * TM_MIX)
        tiles.append(_sort_tile(t_ref.at[rows, :], route_ref.at[rows, :], lbase_ref[k], buf(slot, k)[0]))
    _lockstep(tiles)
    for k in range(per_step):
        local, sm, c = buf(slot, k)
        cnt_scr[c] = _segment_copies(i * per_step + k, dst_ref, nq_ref, local, xs_ref, sm, True)

    @pl.when(i == n - 1)
    def _():
        @pl.when(n >= 2)
        def _():
            wait_slot(1 - slot)

        wait_slot(slot)
        zero_scr[...] = jnp.zeros_like(zero_scr)
        te = zero_scr.shape[0]
        zero_chunk = zero_scr.at[pl.ds(0, SEG_ALIGN)]

        def tail(e, total):
            def chunk(c, carry):
                dst = xs_ref.at[pl.ds(pl.multiple_of(tpos_ref[e] + c * SEG_ALIGN, SEG_ALIGN), SEG_ALIGN)]
                pltpu.make_async_copy(zero_chunk, dst, zsem).start()
                return carry

            lax.fori_loop(0, tn_ref[e], chunk, 0)
            return total + tn_ref[e]

        _wait_chunks(lax.fori_loop(0, N_EXPERTS, tail, 0), zero_scr, xs_ref, zsem)

        def unused_tile(j, carry):
            pltpu.make_async_copy(zero_scr, xs_ref.at[pl.ds(pl.multiple_of(j * te, te), te)], zsem).start()
            return carry

        def unused_wait(j, carry):
            pltpu.make_async_copy(zero_scr, xs_ref.at[pl.ds(0, te)], zsem).wait()
            return carry

        lax.fori_loop(nt_ref[0], xs_ref.shape[0] // te, unused_tile, 0)
        lax.fori_loop(nt_ref[0], xs_ref.shape[0] // te, unused_wait, 0)


def _plan_specs(plan):
    keys = ('dst', 'n_chunks', 'tail_pos', 'tail_n')
    return [plan[k] for k in keys]


MOE_TILES = 2


def _sort_rows(plan, n_tiles, t_bf, route, lbase_f, n_sorted):
    t, dm = t_bf.shape
    tm = TM_MIX * MOE_TILES
    s_rows = _local_rows(TM_MIX)
    im = lambda i, *_: (i, 0)
    return pl.pallas_call(
        _sort_rows_kernel,
        grid_spec=pltpu.PrefetchScalarGridSpec(
            num_scalar_prefetch=5,
            grid=(t // tm,),
            in_specs=[pl.BlockSpec((tm, dm), im), pl.BlockSpec((tm, LANES), im),
                      pl.BlockSpec((MOE_TILES, 1, LANES), lambda i, *_: (i, 0, 0))],
            out_specs=pl.BlockSpec(memory_space=pl.ANY),
            scratch_shapes=[pltpu.VMEM((2, MOE_TILES, s_rows, dm), BF16), pltpu.VMEM((TM_EXPERT, dm), BF16),
                            pltpu.SMEM((2 * MOE_TILES,), jnp.int32), pltpu.SemaphoreType.DMA((2 * MOE_TILES,)),
                            pltpu.SemaphoreType.DMA(())],
        ),
        out_shape=jax.ShapeDtypeStruct((n_sorted, dm), BF16),
        compiler_params=pltpu.CompilerParams(dimension_semantics=("arbitrary",),
                                             vmem_limit_bytes=VMEM_LIMIT),
        name="sort_rows",
    )(*_plan_specs(plan), n_tiles, t_bf, route, lbase_f)


def _expert_weight_copies(e, slot, hbm, stage, sem):
    return [pltpu.make_async_copy(h.at[e], s.at[slot], sem.at[slot]) for h, s in zip(hbm, stage)]


def _experts_kernel(te_ref, nt_ref, nxt_ref, xs_ref, wg_ref, wu_ref, wd_ref, ys_ref,
                    sg, su, sd, wg_bf, wu_bf, wd_bf, slot_scr, sem):
    i = pl.program_id(0)
    e = te_ref[i]
    hbm, stage = (wg_ref, wu_ref, wd_ref), (sg, su, sd)

    @pl.when(i == 0)
    def _():
        slot_scr[0] = 0
        for cp in _expert_weight_copies(e, 0, hbm, stage, sem):
            cp.start()

    @pl.when((i == 0) | (e != te_ref[jnp.maximum(i - 1, 0)]))
    def _():
        slot = slot_scr[0]
        for cp in _expert_weight_copies(e, slot, hbm, stage, sem):
            cp.wait()
        wg_bf[...] = sg[slot].astype(BF16)
        wu_bf[...] = su[slot].astype(BF16)
        wd_bf[...] = sd[slot].astype(BF16)
        nxt = nxt_ref[e]

        @pl.when(nxt != e)
        def _():
            for cp in _expert_weight_copies(nxt, 1 - slot, hbm, stage, sem):
                cp.start()

        slot_scr[0] = 1 - slot

    @pl.when(i < nt_ref[0])
    def _():
        x = xs_ref[...]
        hidden = (jax.nn.silu(_dot(x, wg_bf[...])) * _dot(x, wu_bf[...])).astype(BF16)
        ys_ref[...] = _dot(hidden, wd_bf[...]).astype(BF16)

    @pl.when(i >= nt_ref[0])
    def _():
        ys_ref[...] = jnp.zeros_like(ys_ref)


def _experts(tile_expert, n_tiles, next_expert, x_sorted, w_gate, w_up, w_down):
    n_sorted, dm = x_sorted.shape
    de = w_gate.shape[2]
    tm = TM_EXPERT
    return pl.pallas_call(
        _experts_kernel,
        grid_spec=pltpu.PrefetchScalarGridSpec(
            num_scalar_prefetch=3,
            grid=(n_sorted // tm,),
            in_specs=[pl.BlockSpec((tm, dm), lambda i, te, nt, nx: (jnp.minimum(i, nt[0] - 1), 0)),
                      pl.BlockSpec(memory_space=pl.ANY), pl.BlockSpec(memory_space=pl.ANY),
                      pl.BlockSpec(memory_space=pl.ANY)],
            out_specs=pl.BlockSpec((tm, dm), lambda i, te, nt, nx: (i, 0)),
            scratch_shapes=[pltpu.VMEM((2, dm, de), F32), pltpu.VMEM((2, dm, de), F32), pltpu.VMEM((2, de, dm), F32),
                            pltpu.VMEM((dm, de), BF16), pltpu.VMEM((dm, de), BF16), pltpu.VMEM((de, dm), BF16),
                            pltpu.SMEM((1,), jnp.int32), pltpu.SemaphoreType.DMA((2,))],
        ),
        out_shape=jax.ShapeDtypeStruct((n_sorted, dm), BF16),
        compiler_params=pltpu.CompilerParams(dimension_semantics=("arbitrary",),
                                             vmem_limit_bytes=VMEM_LIMIT),
        name="experts",
    )(tile_expert, n_tiles, next_expert, x_sorted, w_gate, w_up, w_down)


def _combine_tile(x2_ref, route_ref, lbase, gfin_ref, local, o_ref):
    tm = x2_ref.shape[0]
    s_rows = local.shape[0]
    route = route_ref[...]
    lp1, lp2 = _local_positions(route, lbase)
    w1, w2 = route[:, R_W1:R_W1 + 1], route[:, R_W2:R_W2 + 1]
    col0 = lax.broadcasted_iota(jnp.int32, (tm, KB), 1).astype(F32)
    moe = None
    for k in range(s_rows // KB):
        col = col0 + float(k * KB)
        pick = (jnp.where(col == lp1, w1, 0.0) + jnp.where(col == lp2, w2, 0.0)).astype(BF16)
        part = _dot(pick, local[k * KB:(k + 1) * KB, :])
        moe = part if moe is None else moe + part
    yield
    o_ref[...] = _rms(x2_ref[...] + moe, gfin_ref[...])


def _combine_kernel(dst_ref, nq_ref, tpos_ref, tn_ref, x2_ref, route_ref, lbase_ref, gfin_ref, ys_ref,
                    o_ref, local_scr, cnt_scr, sem):
    del tpos_ref, tn_ref
    i = pl.program_id(0)
    n = pl.num_programs(0)
    slot = i % 2
    per_step = local_scr.shape[1]
    s_rows = local_scr.shape[2]

    def fetch(step, s):
        for k in range(per_step):
            tile = step * per_step + k

            def clear(r, carry, k=k):
                local_scr[s, k, pl.ds(pl.multiple_of(r * SEG_ALIGN, SEG_ALIGN), SEG_ALIGN), :] = jnp.zeros(
                    (SEG_ALIGN, local_scr.shape[3]), BF16)
                return carry

            lax.fori_loop(nq_ref[tile], s_rows // SEG_ALIGN, clear, 0)
            cnt_scr[s * per_step + k] = _segment_copies(tile, dst_ref, nq_ref, local_scr.at[s, k], ys_ref,
                                                        sem.at[s * per_step + k], False)

    @pl.when(i == 0)
    def _():
        fetch(0, 0)

    @pl.when(i + 1 < n)
    def _():
        fetch(i + 1, 1 - slot)

    tiles = []
    for k in range(per_step):
        c = slot * per_step + k
        _wait_chunks(cnt_scr[c], local_scr.at[slot, k], ys_ref, sem.at[c])
        rows = slice(k * TM_MIX, (k + 1) * TM_MIX)
        tiles.append(_combine_tile(x2_ref.at[rows, :], route_ref.at[rows, :], lbase_ref[k], gfin_ref,
                                   local_scr.at[slot, k], o_ref.at[rows, :]))
    _lockstep(tiles)


def _combine(plan, x2, route, lbase_f, gfin, y_sorted):
    t, dm = x2.shape
    tm = TM_MIX * MOE_TILES
    s_rows = _local_rows(TM_MIX)
    im = lambda i, *_: (i, 0)
    return pl.pallas_call(
        _combine_kernel,
        grid_spec=pltpu.PrefetchScalarGridSpec(
            num_scalar_prefetch=4,
            grid=(t // tm,),
            in_specs=[pl.BlockSpec((tm, dm), im), pl.BlockSpec((tm, LANES), im),
                      pl.BlockSpec((MOE_TILES, 1, LANES), lambda i, *_: (i, 0, 0)),
                      pl.BlockSpec((1, dm), lambda i, *_: (0, 0)),
                      pl.BlockSpec(memory_space=pl.ANY)],
            out_specs=pl.BlockSpec((tm, dm), im),
            scratch_shapes=[pltpu.VMEM((2, MOE_TILES, s_rows, dm), BF16), pltpu.SMEM((2 * MOE_TILES,), jnp.int32),
                            pltpu.SemaphoreType.DMA((2 * MOE_TILES,))],
        ),
        out_shape=jax.ShapeDtypeStruct((t, dm), F32),
        compiler_params=pltpu.CompilerParams(dimension_semantics=("arbitrary",),
                                             vmem_limit_bytes=VMEM_LIMIT),
        name="combine_norm",
    )(*_plan_specs(plan), x2, route, lbase_f, gfin, y_sorted)


def _layer(x, p, s5_ops, gfin):
    b, l, dm = x.shape
    x2d = x.reshape(b * l, dm)
    mgm, us5 = _inproj_gmlp(x2d, p['gmix'], p['win'], p['lng'], p['lnb'], p['ws'], p['bs'], p['gout_gm'])
    n_seg = SUBLANES // b
    *lag_factors, w1, w2, sc = s5_ops[(l // (S5_LC * n_seg))]
    xg = _s5_inproj(x, p['gmix'], p['win_s5'], n_seg)
    yg = _s5_scan(xg, lag_factors, w1, w2, sc, n_seg)
    ys = _s5_to_tokens(yg, b, l, n_seg)
    x2, t_bf, route, counts = _mix_route(ys, us5, mgm, x2d, p['d'], p['gluw'], p['glub'], p['gout_s5'],
                                         p['wout'], p['gffn'], p['rwh'], p['rwl'], p['rb'])
    plan, lbase_f, tiles, n_sorted = _segment_plan(counts, b * l, TM_EXPERT)
    x_sorted = _sort_rows(plan, tiles[1], t_bf, route, lbase_f, n_sorted)
    y_sorted = _experts(*tiles, x_sorted, p['w_gate'], p['w_up'], p['w_down'])
    out = _combine(plan, x2, route, lbase_f, gfin, y_sorted)
    return out.reshape(b, l, dm)


def kernel(x_prompt, x_sample, norm_mix_g, w_in, gm_ln_g, gm_ln_b, gm_ws, gm_bs, s5_lam_re_fwd, s5_lam_im_fwd, s5_log_step_fwd, s5_b_re_fwd, s5_b_im_fwd, s5_c_re_fwd, s5_c_im_fwd, s5_lam_re_bwd, s5_lam_im_bwd, s5_log_step_bwd, s5_b_re_bwd, s5_b_im_bwd, s5_c_re_bwd, s5_c_im_bwd, s5_d, s5_glu_w, s5_glu_b, out_norm_gm, out_norm_s5, w_out, norm_ffn_g, r1_w, r1_b, r2_w, r2_b, e_w_gate, e_w_up, e_w_down, norm_final_g):
    depth = w_in.shape[0]
    gfin = norm_final_g.reshape(1, -1).astype(F32)
    xs = [x_prompt, x_sample]
    for li in range(depth):
        row = lambda a: a[li].reshape(1, -1).astype(F32)
        dm = w_in.shape[1]
        gw = gm_ln_g.shape[1]
        hd_dim = gw // GM_HEADS
        rw = jnp.concatenate([r1_w[li], r2_w[li].transpose(1, 0, 2).reshape(dm, N_EXPERTS)], axis=1).astype(F32)
        rw = jnp.pad(rw, ((0, 0), (0, LANES - rw.shape[1])))
        rwh = rw.astype(BF16)
        rwl = (rw - rwh.astype(F32)).astype(BF16)
        rb = jnp.concatenate([r1_b[li], r2_b[li].reshape(-1)]).astype(F32)
        rb = jnp.pad(rb, (0, LANES - rb.shape[0])).reshape(1, LANES)
        p = dict(
            gmix=row(norm_mix_g), win=w_in[li].astype(BF16), win_s5=w_in[li][:, 2 * gw:].astype(BF16),
            lng=row(gm_ln_g), lnb=row(gm_ln_b),
            ws=gm_ws[li].astype(BF16),
            bs=jnp.broadcast_to(gm_bs[li].astype(F32)[:, :, None], (GM_HEADS, CHUNK, hd_dim)),
            gout_gm=row(out_norm_gm), d=row(s5_d), gluw=s5_glu_w[li].astype(BF16), glub=row(s5_glu_b),
            gout_s5=row(out_norm_s5), wout=w_out[li].astype(BF16), gffn=row(norm_ffn_g),
            rwh=rwh, rwl=rwl, rb=rb,
            w_gate=e_w_gate[li], w_up=e_w_up[li], w_down=e_w_down[li],
        )
        fwd = (s5_lam_re_fwd[li], s5_lam_im_fwd[li], s5_log_step_fwd[li], s5_b_re_fwd[li], s5_b_im_fwd[li],
               s5_c_re_fwd[li], s5_c_im_fwd[li])
        bwd = (s5_lam_re_bwd[li], s5_lam_im_bwd[li], s5_log_step_bwd[li], s5_b_re_bwd[li], s5_b_im_bwd[li],
               s5_c_re_bwd[li], s5_c_im_bwd[li])
        s5_ops = {}
        for x in xs:
            seg_steps = x.shape[1] // (S5_LC * (SUBLANES // x.shape[0]))
            if seg_steps not in s5_ops:
                s5_ops[seg_steps] = _s5_operator(fwd, bwd, S5_LC, seg_steps)
        last = li == depth - 1
        assert last, "depth > 1 needs an un-normalised layer output"
        xs = [_layer(x, p, s5_ops, gfin) for x in xs]
    return tuple(xs)
```

```python
import functools
import math

import jax
import jax.numpy as jnp
from jax import lax
from jax.experimental import pallas as pl
from jax.experimental.pallas import tpu as pltpu

F32 = jnp.float32
BF16 = jnp.bfloat16

EPS = 1e-6
LAMBDA_RE_MAX = -1e-4
GM_HEADS = 4
CHUNK = 128
S5_GROUP = 16
S5_STATE = 64
N_COARSE = 4
N_FINE = 8
N_EXPERTS = N_COARSE * N_FINE

LANES = 128
SUBLANES = 8
S5_LC = 16
VMEM_LIMIT = 56 * 1024 * 1024

TM_PROJ = 1024
TM_MIX = 512
MIX_TILES = 2
KB = 256
TM_EXPERT = 1024
SEG_ALIGN = 16


def _gelu(x):
    c = math.sqrt(2.0 / math.pi)
    half = 0.5 * x
    return half + half * jnp.tanh(x * (c + (c * 0.044715) * (x * x)))


def _rms(x, g):
    ms = jnp.mean(x * x, axis=-1, keepdims=True)
    return x * lax.rsqrt(ms + EPS) * g


def _dot(a, b):
    return jnp.dot(a, b, preferred_element_type=F32)


def _lockstep(tiles):
    while tiles:
        tiles = [t for t in tiles if next(t, "done") != "done"]


def _inproj_gmlp_kernel(x_ref, gmix_ref, win_ref, lng_ref, lnb_ref, ws_ref, bs_ref, gout_ref,
                        mgm_ref, us5_ref, y_scr):
    tm = x_ref.shape[0]
    gw = mgm_ref.shape[1]
    hd_dim = gw // GM_HEADS
    n_chunks = tm // CHUNK
    h = _rms(x_ref[...], gmix_ref[...]).astype(BF16)
    proj = _dot(h, win_ref[...])
    us5_ref[...] = proj[:, 2 * gw:]
    for hd in range(GM_HEADS):
        lo = hd * hd_dim
        vh = _gelu(proj[:, gw + lo:gw + lo + hd_dim])
        mu = jnp.mean(vh, axis=-1, keepdims=True)
        xc = vh - mu
        var = jnp.mean(xc * xc, axis=-1, keepdims=True)
        vn = (xc * lax.rsqrt(var + EPS) * lng_ref[:, lo:lo + hd_dim]
              + lnb_ref[:, lo:lo + hd_dim]).astype(BF16)
        rhs = jnp.concatenate([vn[c * CHUNK:(c + 1) * CHUNK] for c in range(n_chunks)], axis=1)
        s = _dot(ws_ref[hd], rhs)
        for c in range(n_chunks):
            sc = s[:, c * hd_dim:(c + 1) * hd_dim] + bs_ref[hd]
            u = _gelu(proj[c * CHUNK:(c + 1) * CHUNK, lo:lo + hd_dim])
            y_scr[c * CHUNK:(c + 1) * CHUNK, lo:lo + hd_dim] = u * sc
    mgm_ref[...] = _rms(y_scr[...], gout_ref[...]).astype(BF16)


def _inproj_gmlp(x2d, gmix, win_bf, lng, lnb, ws_bf, bs_b, gout):
    t, d = x2d.shape
    d_in = win_bf.shape[1]
    gw = lng.shape[1]
    s5w = d_in - 2 * gw
    tm = TM_PROJ
    const = lambda *shape: pl.BlockSpec(shape, lambda i: (0,) * len(shape))
    return pl.pallas_call(
        _inproj_gmlp_kernel,
        grid=(t // tm,),
        in_specs=[
            pl.BlockSpec((tm, d), lambda i: (i, 0)),
            const(1, d), const(d, d_in), const(1, gw), const(1, gw),
            const(GM_HEADS, CHUNK, CHUNK), const(GM_HEADS, CHUNK, gw // GM_HEADS), const(1, gw),
        ],
        out_specs=[pl.BlockSpec((tm, gw), lambda i: (i, 0)),
                   pl.BlockSpec((tm, s5w), lambda i: (i, 0))],
        out_shape=[jax.ShapeDtypeStruct((t, gw), BF16),
                   jax.ShapeDtypeStruct((t, s5w), F32)],
        scratch_shapes=[pltpu.VMEM((tm, gw), F32)],
        compiler_params=pltpu.CompilerParams(dimension_semantics=("parallel",),
                                             vmem_limit_bytes=VMEM_LIMIT),
        name="inproj_gmlp",
    )(x2d, gmix, win_bf, lng, lnb, ws_bf, bs_b, gout)


def _s5_consts(lam_re, lam_im, log_step, b_re, b_im, c_re, c_im, lc):
    lr = jnp.minimum(lam_re.astype(F32), LAMBDA_RE_MAX)
    li = lam_im.astype(F32)
    step = jnp.exp(log_step.astype(F32))[:, None]
    dr, di = lr * step, li * step
    ar, ai = _cexp(dr, di)
    nr, ni = ar - 1.0, ai
    den = lr * lr + li * li
    qr, qi = (nr * lr + ni * li) / den, (ni * lr - nr * li) / den
    br, bi = b_re.astype(F32), b_im.astype(F32)
    bbr = qr[..., None] * br - qi[..., None] * bi
    bbi = qr[..., None] * bi + qi[..., None] * br
    k = jnp.arange(lc + 1, dtype=F32)[:, None, None]
    pwr, pwi = _cexp(k * dr[None], k * di[None])
    return (dr, di), (pwr, pwi), (bbr, bbi), (c_re.astype(F32), c_im.astype(F32))


def _cexp(zr, zi):
    m = jnp.exp(zr)
    return m * jnp.cos(zi), m * jnp.sin(zi)


def _s5_operator(fwd, bwd, lc, seg_steps):
    consts = [_s5_consts(*fwd, lc), _s5_consts(*bwd, lc)]
    g, p, h = consts[0][2][0].shape
    lags, w1_parts, w2_parts, sc_rows, seg_rows = [], [], [], [], []
    for direction, (ld, pw, bb, c) in enumerate(consts):
        (dr, di), (pwr, pwi), (bbr, bbi), (cr, ci) = ld, pw, bb, c
        crt, cit = cr.transpose(0, 2, 1), ci.transpose(0, 2, 1)
        pwrt, pwit = pwr.transpose(1, 2, 0), pwi.transpose(1, 2, 0)
        cpr = crt[:, :, None, :] * pwrt[:, :, :, None] - cit[:, :, None, :] * pwit[:, :, :, None]
        cpi = crt[:, :, None, :] * pwit[:, :, :, None] + cit[:, :, None, :] * pwrt[:, :, :, None]
        ck = jnp.concatenate([cpr[:, :, :lc], cpi[:, :, :lc]], axis=1)
        if direction == 1:
            ck = jnp.flip(ck, 2)
        lags += [jnp.concatenate([bbr.transpose(0, 2, 1), -bbi.transpose(0, 2, 1)], axis=-1),
                 ck.reshape(g, 2 * p, lc * h)]
        er, ei = pwrt[:, :, :lc].transpose(0, 2, 1), pwit[:, :, :lc].transpose(0, 2, 1)
        if direction == 0:
            er, ei = jnp.flip(er, 1), jnp.flip(ei, 1)
        bbrt, bbit = bbr.transpose(0, 2, 1), bbi.transpose(0, 2, 1)
        e1 = jnp.concatenate([er, er], -1)[:, :, None, :]
        e2 = jnp.concatenate([-ei, ei], -1)[:, :, None, :]
        b_ri = jnp.concatenate([bbrt, bbit], -1)[:, None]
        b_ir = jnp.concatenate([bbit, bbrt], -1)[:, None]
        w1_parts += [e1 * b_ri + e2 * b_ir, e1 * b_ir - e2 * b_ri]
        fr, fi = cpr[:, :, 1:lc + 1], cpi[:, :, 1:lc + 1]
        if direction == 1:
            fr, fi = jnp.flip(fr, 2), jnp.flip(fi, 2)
        w2_parts += [fr, -fi]

        def mult(zr, zi):
            return [jnp.concatenate([zr, zr], -1), jnp.concatenate([-zi, zi], -1)]

        sc_rows += mult(*_cexp(lc * dr, lc * di))
        seg_rows += mult(*_cexp((lc * seg_steps) * dr, (lc * seg_steps) * di))
    w1 = jnp.concatenate(w1_parts, axis=-1).reshape(g, lc * h, 8 * p)
    w2 = jnp.concatenate(w2_parts, axis=1).reshape(g, 4 * p, lc * h)
    sc = jnp.stack(sc_rows + seg_rows, axis=1)
    return tuple(lags) + (w1.astype(BF16), w2.astype(BF16), sc.astype(F32))


S5_GPS = 2


def _s5_kernel(x_ref, bbf_ref, cpf_ref, bbb_ref, cpb_ref, w1_ref, w2_ref, sc_ref, y_ref,
               loc_scr, sin_scr, m_scr, *, n_seg):
    gps, rows, kw = x_ref.shape
    steps = rows // SUBLANES
    sw = sc_ref.shape[2]

    for gi in range(gps):
        kf = jnp.dot(bbf_ref[gi], cpf_ref[gi], precision=lax.Precision.HIGHEST, preferred_element_type=F32)
        kb = jnp.dot(bbb_ref[gi], cpb_ref[gi], precision=lax.Precision.HIGHEST, preferred_element_type=F32)
        hch = kf.shape[0]
        lc = kw // hch
        lane = lax.broadcasted_iota(jnp.int32, kf.shape, 1)
        for s in range(lc):
            f = kf if s == 0 else jnp.where(lane >= s * hch, pltpu.roll(kf, s * hch, 1), 0.0)
            left = (lc - 1 - s) * hch
            b = kb if left == 0 else pltpu.roll(kb, kw - left, 1)
            m_scr[gi, s * hch:(s + 1) * hch, :] = (f + jnp.where(lane < (s + 1) * hch, b, 0.0)).astype(BF16)
        loc_scr[gi] = _dot(x_ref[gi], w1_ref[gi])

    def bc(gi, i):
        return jnp.broadcast_to(sc_ref[gi, i:i + 1, :], (SUBLANES, sw))

    mult = [[bc(gi, i) for i in range(8)] for gi in range(gps)]

    def step(gi, s, state):
        f, fs, b, bs = state
        a1f, a2f, a1b, a2b = mult[gi][:4]
        rf = pl.multiple_of(s * SUBLANES, SUBLANES)
        rb = pl.multiple_of((steps - 1 - s) * SUBLANES, SUBLANES)
        lf = loc_scr[gi, pl.ds(rf, SUBLANES), 0:sw]
        lfs = loc_scr[gi, pl.ds(rf, SUBLANES), sw:2 * sw]
        lb = loc_scr[gi, pl.ds(rb, SUBLANES), 2 * sw:3 * sw]
        lbs = loc_scr[gi, pl.ds(rb, SUBLANES), 3 * sw:4 * sw]
        return (a1f * f + a2f * fs + lf, a1f * fs - a2f * f + lfs,
                a1b * b + a2b * bs + lb, a1b * bs - a2b * b + lbs)

    zero = jnp.zeros((SUBLANES, sw), F32)

    def pass1(s, carry):
        return tuple(step(gi, s, carry[gi]) for gi in range(gps))

    ends = lax.fori_loop(0, steps, pass1, tuple((zero,) * 4 for _ in range(gps)), unroll=4)

    seg = lax.broadcasted_iota(jnp.int32, (SUBLANES, sw), 0) % n_seg
    enter = []
    for gi in range(gps):
        f_end, fs_end, b_end, bs_end = ends[gi]
        p1f, p2f, p1b, p2b = mult[gi][4:]
        cf, cfs, cb, cbs = zero, zero, zero, zero
        for _ in range(n_seg - 1):
            ef = f_end + p1f * cf + p2f * cfs
            efs = fs_end + p1f * cfs - p2f * cf
            eb = b_end + p1b * cb + p2b * cbs
            ebs = bs_end + p1b * cbs - p2b * cb
            cf = jnp.where(seg >= 1, pltpu.roll(ef, 1, 0), 0.0)
            cfs = jnp.where(seg >= 1, pltpu.roll(efs, 1, 0), 0.0)
            cb = jnp.where(seg <= n_seg - 2, pltpu.roll(eb, SUBLANES - 1, 0), 0.0)
            cbs = jnp.where(seg <= n_seg - 2, pltpu.roll(ebs, SUBLANES - 1, 0), 0.0)
        enter.append((cf, cfs, cb, cbs))

    def pass2(s, carry):
        rf = pl.multiple_of(s * SUBLANES, SUBLANES)
        rb = pl.multiple_of((steps - 1 - s) * SUBLANES, SUBLANES)
        for gi in range(gps):
            sin_scr[gi, pl.ds(rf, SUBLANES), 0:sw] = carry[gi][0]
            sin_scr[gi, pl.ds(rb, SUBLANES), sw:2 * sw] = carry[gi][2]
        return tuple(step(gi, s, carry[gi]) for gi in range(gps))

    lax.fori_loop(0, steps, pass2, tuple(enter), unroll=4)

    for gi in range(gps):
        y_ref[gi] = _dot(x_ref[gi], m_scr[gi]) + _dot(sin_scr[gi].astype(BF16), w2_ref[gi])


def _s5_scan(xg, lag_factors, w1, w2, sc, n_seg):
    g, rows, kw = xg.shape
    sw = sc.shape[2]
    gps = S5_GPS
    blk = lambda a: pl.BlockSpec((gps,) + a.shape[1:], lambda i: (i, 0, 0))
    return pl.pallas_call(
        functools.partial(_s5_kernel, n_seg=n_seg),
        grid=(g // gps,),
        in_specs=[blk(xg)] + [blk(a) for a in lag_factors] + [blk(w1), blk(w2), blk(sc)],
        out_specs=pl.BlockSpec((gps, rows, kw), lambda i: (i, 0, 0)),
        out_shape=jax.ShapeDtypeStruct((g, rows, kw), F32),
        scratch_shapes=[pltpu.VMEM((gps, rows, 4 * sw), F32), pltpu.VMEM((gps, rows, 2 * sw), F32),
                        pltpu.VMEM((gps, kw, kw), BF16)],
        compiler_params=pltpu.CompilerParams(dimension_semantics=("parallel",),
                                             vmem_limit_bytes=VMEM_LIMIT),
        name="s5_scan",
    )(xg, *lag_factors, w1, w2, sc)


S5_NM = 16


def _block_transpose8(groups, width):
    lane = lax.broadcasted_iota(jnp.int32, groups[0][0].shape, 1)
    for d in (4, 2, 1):
        w = width * d
        hi = ((lane // w) % 2) == 1
        nxt = []
        for v in groups:
            out = list(v)
            for i0 in range(8):
                if i0 & d:
                    continue
                i1 = i0 + d
                out[i0] = jnp.where(hi, pltpu.roll(v[i1], w, 1), v[i0])
                out[i1] = jnp.where(hi, v[i1], pltpu.roll(v[i0], 8 * width - w, 1))
            nxt.append(out)
        groups = nxt
    return groups


def _tile_copies(hbm4, tile, buf, slot, sem, nm, to_hbm):
    copies = []
    for c in range(SUBLANES):
        for j in range(S5_LC):
            h = hbm4.at[c, pl.ds(tile * nm, nm), pl.ds(j, 1), :]
            v = buf.at[slot, j, :, pl.ds(c, 1), :]
            copies.append(pltpu.make_async_copy(v, h, sem.at[slot]) if to_hbm
                          else pltpu.make_async_copy(h, v, sem.at[slot]))
    return copies


def _s5_inproj_kernel(x4_ref, gmix_ref, w_ref, xg_ref, xs, sem, *, nm):
    i = pl.program_id(0)
    n = pl.num_programs(0)
    slot = i % 2
    dm = x4_ref.shape[3]

    @pl.when(i == 0)
    def _():
        for cp in _tile_copies(x4_ref, 0, xs, 0, sem, nm, False):
            cp.start()

    @pl.when(i + 1 < n)
    def _():
        for cp in _tile_copies(x4_ref, i + 1, xs, 1 - slot, sem, nm, False):
            cp.start()

    pltpu.make_async_copy(xs.at[slot], xs.at[slot], sem.at[slot]).wait()
    rows = nm * SUBLANES
    x = xs[slot].reshape(S5_LC * rows, dm)
    z = _dot(_rms(x, gmix_ref[...]).astype(BF16), w_ref[...])
    n_oct = z.shape[1] // LANES
    combos = [(q, a) for q in range(n_oct) for a in range(S5_LC // 8)]
    blocks = [[z[(8 * a + j8) * rows:(8 * a + j8 + 1) * rows, q * LANES:(q + 1) * LANES] for j8 in range(8)]
              for q, a in combos]
    for (q, a), out in zip(combos, _block_transpose8(blocks, S5_GROUP)):
        for g8, b in enumerate(out):
            xg_ref[8 * q + g8, :, a * LANES:(a + 1) * LANES] = b.astype(BF16)


def _s5_inproj(x, gmix, w_s5_bf, n_seg):
    b, l, dm = x.shape
    steps = l // (S5_LC * n_seg)
    nm = S5_NM
    s5w = w_s5_bf.shape[1]
    g = s5w // S5_GROUP
    x4 = x.reshape(b * n_seg, steps, S5_LC, dm)
    return pl.pallas_call(
        functools.partial(_s5_inproj_kernel, nm=nm),
        grid=(steps // nm,),
        in_specs=[pl.BlockSpec(memory_space=pl.ANY),
                  pl.BlockSpec((1, dm), lambda i: (0, 0)),
                  pl.BlockSpec((dm, s5w), lambda i: (0, 0))],
        out_specs=pl.BlockSpec((g, nm * SUBLANES, S5_LC * S5_GROUP), lambda i: (0, i, 0)),
        out_shape=jax.ShapeDtypeStruct((g, steps * SUBLANES, S5_LC * S5_GROUP), BF16),
        scratch_shapes=[pltpu.VMEM((2, S5_LC, nm, SUBLANES, dm), F32), pltpu.SemaphoreType.DMA((2,))],
        compiler_params=pltpu.CompilerParams(dimension_semantics=("arbitrary",),
                                             vmem_limit_bytes=VMEM_LIMIT),
        name="s5_inproj",
    )(x4, gmix, w_s5_bf)


def _s5_to_tokens_kernel(yg_ref, ys4_ref, zs, sem, *, nm):
    i = pl.program_id(0)
    n = pl.num_programs(0)
    slot = i % 2
    rows = nm * SUBLANES

    def wait(s):
        pltpu.make_async_copy(zs.at[s], zs.at[s], sem.at[s]).wait()

    @pl.when(i >= 2)
    def _():
        wait(slot)

    n_oct = yg_ref.shape[0] // 8
    combos = [(q, a) for q in range(n_oct) for a in range(S5_LC // 8)]
    blocks = [[yg_ref[8 * q + g8, :, a * LANES:(a + 1) * LANES] for g8 in range(8)] for q, a in combos]
    for (q, a), out in zip(combos, _block_transpose8(blocks, S5_GROUP)):
        for j8, b in enumerate(out):
            zs[slot, 8 * a + j8, :, :, q * LANES:(q + 1) * LANES] = b.reshape(nm, SUBLANES, LANES)
    for cp in _tile_copies(ys4_ref, i, zs, slot, sem, nm, True):
        cp.start()

    @pl.when(i == n - 1)
    def _():
        wait(1 - slot)
        wait(slot)


def _s5_to_tokens(yg, b, l, n_seg):
    g, rows_total, kw = yg.shape
    steps = rows_total // SUBLANES
    nm = S5_NM
    s5w = g * S5_GROUP
    assert steps // nm >= 2
    ys4 = pl.pallas_call(
        functools.partial(_s5_to_tokens_kernel, nm=nm),
        grid=(steps // nm,),
        in_specs=[pl.BlockSpec((g, nm * SUBLANES, kw), lambda i: (0, i, 0))],
        out_specs=pl.BlockSpec(memory_space=pl.ANY),
        out_shape=jax.ShapeDtypeStruct((b * n_seg, steps, S5_LC, s5w), F32),
        scratch_shapes=[pltpu.VMEM((2, S5_LC, nm, SUBLANES, s5w), F32), pltpu.SemaphoreType.DMA((2,))],
        compiler_params=pltpu.CompilerParams(dimension_semantics=("arbitrary",),
                                             vmem_limit_bytes=VMEM_LIMIT),
        name="s5_to_tokens",
    )(yg)
    return ys4.reshape(b * l, s5w)


R_E1, R_E2, R_W1, R_W2, R_RANK1, R_RANK2 = range(6)


def _mix_route_kernel(ys_ref, us5_ref, mgm_ref, x_ref, d_ref, gluw_ref, glub_ref, gs5_ref,
                      wout_ref, gffn_ref, rw_ref, rb_ref, tri_ref,
                      x2_ref, t_ref, route_ref, cnt_ref):
    tiles = []
    for k in range(x_ref.shape[0] // TM_MIX):
        rows = slice(k * TM_MIX, (k + 1) * TM_MIX)
        tiles.append(_mix_route_tile(ys_ref.at[rows, :], us5_ref.at[rows, :], mgm_ref.at[rows, :], x_ref.at[rows, :],
                                     d_ref, gluw_ref, glub_ref, gs5_ref, wout_ref, gffn_ref, rw_ref,
                                     rb_ref, tri_ref, x2_ref.at[rows, :], t_ref.at[rows, :], route_ref.at[rows, :],
                                     cnt_ref.at[k]))
    _lockstep(tiles)


def _mix_route_tile(ys_ref, us5_ref, mgm_ref, x_ref, d_ref, gluw_ref, glub_ref, gs5_ref,
                    wout_ref, gffn_ref, rw_ref, rb_ref, tri_ref,
                    x2_ref, t_ref, route_ref, cnt_ref):
    gw = mgm_ref.shape[1]
    y = ys_ref[...] + d_ref[...] * us5_ref[...]
    g = _gelu(y)
    yield
    gate = _dot(g.astype(BF16), gluw_ref[...])
    yield
    z = g * jax.nn.sigmoid(gate + glub_ref[...])
    ms5 = _rms(z, gs5_ref[...]).astype(BF16)
    yield
    mix = _dot(mgm_ref[...], wout_ref[:gw, :]) + _dot(ms5, wout_ref[gw:, :])
    yield
    x2 = x_ref[...] + mix
    x2_ref[...] = x2
    t = _rms(x2, gffn_ref[...])
    t_hi = t.astype(BF16)
    t_ref[...] = t_hi
    t_lo = (t - t_hi.astype(F32)).astype(BF16)
    yield
    hl = _dot(t_hi, rw_ref[...])
    logits = (hl[:, :LANES] + hl[:, LANES:] + _dot(t_lo, rw_ref[:, :LANES])
              + rb_ref[...])
    yield
    lane = lax.broadcasted_iota(jnp.int32, logits.shape, 1).astype(F32)
    neg = jnp.float32(-jnp.inf)

    def first_max(mask):
        vals = jnp.where(mask, logits, neg)
        mx = jnp.max(vals, axis=-1, keepdims=True)
        idx = jnp.min(jnp.where(mask & (vals == mx), lane, float(LANES)), axis=-1, keepdims=True)
        return mx, idx

    coarse = lane < N_COARSE
    m1, grp = first_max(coarse)
    p_grp = 1.0 / jnp.sum(jnp.where(coarse, jnp.exp(logits - m1), 0.0), axis=-1, keepdims=True)
    lo = N_COARSE + grp * N_FINE
    fine = (lane >= lo) & (lane < lo + N_FINE)
    v1, i1 = first_max(fine)
    v2, i2 = first_max(fine & (lane != i1))
    e21 = jnp.exp(v2 - v1)
    w1 = p_grp / (1.0 + e21)
    w2 = p_grp * e21 / (1.0 + e21)
    e1 = i1 - N_COARSE
    e2 = i2 - N_COARSE
    hit1 = lane == e1
    hit2 = lane == e2
    onehot = jnp.where(hit1 | hit2, 1.0, 0.0)
    before = _dot(tri_ref[...], onehot.astype(BF16))
    rank1 = jnp.sum(jnp.where(hit1, before, 0.0), axis=-1, keepdims=True)
    rank2 = jnp.sum(jnp.where(hit2, before, 0.0), axis=-1, keepdims=True)
    tm = onehot.shape[0]
    cnt_ref[...] = before[tm - 1:tm, :] + onehot[tm - 1:tm, :]
    rec = jnp.zeros_like(logits)
    for slot, val in ((R_E1, e1), (R_E2, e2), (R_W1, w1), (R_W2, w2),
                      (R_RANK1, rank1), (R_RANK2, rank2)):
        rec = jnp.where(lane == slot, val, rec)
    route_ref[...] = rec


def _mix_route(ys, us5, mgm, x2d, d, gluw_bf, glub, gs5, wout_bf, gffn, rw_hl, rb):
    t, dm = x2d.shape
    gw = mgm.shape[1]
    s5w = us5.shape[1]
    tm = TM_MIX * MIX_TILES
    tri = jnp.tril(jnp.ones((TM_MIX, TM_MIX), F32), -1).astype(BF16)
    const = lambda *shape: pl.BlockSpec(shape, lambda i: (0,) * len(shape))
    tile = lambda w: pl.BlockSpec((tm, w), lambda i: (i, 0))
    return pl.pallas_call(
        _mix_route_kernel,
        grid=(t // tm,),
        in_specs=[tile(s5w), tile(s5w), tile(gw), tile(dm),
                  const(1, s5w), const(s5w, s5w), const(1, s5w), const(1, s5w),
                  const(gw + s5w, dm), const(1, dm), const(dm, 2 * LANES), const(1, LANES),
                  const(TM_MIX, TM_MIX)],
        out_specs=[tile(dm), tile(dm), tile(LANES), pl.BlockSpec((MIX_TILES, 1, LANES), lambda i: (i, 0, 0))],
        out_shape=[jax.ShapeDtypeStruct((t, dm), F32),
                   jax.ShapeDtypeStruct((t, dm), BF16),
                   jax.ShapeDtypeStruct((t, LANES), F32),
                   jax.ShapeDtypeStruct((t // TM_MIX, 1, LANES), F32)],
        compiler_params=pltpu.CompilerParams(dimension_semantics=("parallel",),
                                             vmem_limit_bytes=VMEM_LIMIT),
        name="mix_route",
    )(ys, us5, mgm, x2d, d, gluw_bf, glub, gs5, wout_bf, gffn, rw_hl, rb, tri)


def _local_rows(tm):
    worst = 2 * tm + N_EXPERTS * (SEG_ALIGN - 1)
    return -(-worst // LANES) * LANES


def _segment_plan(cnt, t, tm_expert):
    c = cnt[:, 0, :N_EXPERTS].astype(jnp.int32)
    n_tok_tiles = c.shape[0]
    al = (c + SEG_ALIGN - 1) // SEG_ALIGN * SEG_ALIGN
    lbase = jnp.cumsum(al, axis=1) - al
    tot = jnp.sum(al, axis=0)
    tot_pad = (tot + tm_expert - 1) // tm_expert * tm_expert
    gbase = jnp.cumsum(tot_pad) - tot_pad
    gpos = gbase[None, :] + jnp.cumsum(al, axis=0) - al
    n_tiles_max = -(-(2 * t + n_tok_tiles * N_EXPERTS * (SEG_ALIGN - 1)) // tm_expert) + N_EXPERTS
    tile_end = jnp.cumsum(tot_pad // tm_expert)
    n_tiles = tile_end[-1:].astype(jnp.int32)
    tile_idx = jnp.arange(n_tiles_max, dtype=jnp.int32)
    tile_expert = jnp.sum((tile_idx[:, None] >= tile_end[None, :]).astype(jnp.int32), axis=1)
    last = jnp.sum((n_tiles - 1 >= tile_end).astype(jnp.int32))
    tile_expert = jnp.where(tile_idx < n_tiles, tile_expert, last).astype(jnp.int32)
    ids = jnp.arange(N_EXPERTS, dtype=jnp.int32)
    later_used = (ids[None, :] > ids[:, None]) & (tot_pad[None, :] > 0)
    next_expert = jnp.min(jnp.where(later_used, ids[None, :], N_EXPERTS), axis=1)
    next_expert = jnp.where(next_expert == N_EXPERTS, ids, next_expert).astype(jnp.int32)
    lbase_f = jnp.pad(lbase.astype(F32), ((0, 0), (0, LANES - N_EXPERTS)))[:, None, :]
    flat = lambda a: a.reshape(-1).astype(jnp.int32)
    nch = al // SEG_ALIGN
    cum = jnp.cumsum(nch, axis=1)
    q = jnp.arange(_local_rows(TM_MIX) // SEG_ALIGN, dtype=jnp.int32)[None, :, None]
    seg_of_q = jnp.sum((q >= cum[:, None, :]).astype(jnp.int32), axis=2)
    in_seg = seg_of_q[:, :, None] == jnp.arange(N_EXPERTS, dtype=jnp.int32)[None, None, :]
    pick = lambda a: jnp.sum(jnp.where(in_seg, a[:, None, :], 0), axis=2)
    dst = pick(gpos) + (q[:, :, 0] - pick(cum - nch)) * SEG_ALIGN
    plan = dict(dst=flat(dst), n_chunks=flat(cum[:, -1]),
                tail_pos=flat(gbase + tot), tail_n=flat((tot_pad - tot) // SEG_ALIGN))
    return plan, lbase_f, (tile_expert, n_tiles, next_expert), n_tiles_max * tm_expert


def _local_positions(route, lbase):
    lane = lax.broadcasted_iota(jnp.int32, route.shape, 1).astype(F32)
    out = []
    for e_lane, r_lane in ((R_E1, R_RANK1), (R_E2, R_RANK2)):
        e = route[:, e_lane:e_lane + 1]
        base = jnp.sum(jnp.where(lane == e, lbase, 0.0), axis=-1, keepdims=True)
        out.append(base + route[:, r_lane:r_lane + 1])
    return out


WAIT_GROUP = 8
ISSUE_GROUP = 4


def _segment_copies(i, dst_ref, nq_ref, local, glob, sem, to_global):
    per_tile = local.shape[0] // SEG_ALIGN
    n = nq_ref[i]

    def start(q):
        lo = local.at[pl.ds(pl.multiple_of(q * SEG_ALIGN, SEG_ALIGN), SEG_ALIGN)]
        gl = glob.at[pl.ds(pl.multiple_of(dst_ref[i * per_tile + q], SEG_ALIGN), SEG_ALIGN)]
        (pltpu.make_async_copy(lo, gl, sem) if to_global else pltpu.make_async_copy(gl, lo, sem)).start()

    def group(k, carry):
        for u in range(ISSUE_GROUP):
            start(k * ISSUE_GROUP + u)
        return carry

    def single(q, carry):
        start(q)
        return carry

    full = n // ISSUE_GROUP
    lax.fori_loop(0, full, group, 0)
    lax.fori_loop(full * ISSUE_GROUP, n, single, 0)
    return n


def _wait_chunks(n, local, glob, sem):
    def wait_rows(rows):
        def one(c, carry):
            pltpu.make_async_copy(local.at[pl.ds(0, rows)], glob.at[pl.ds(0, rows)], sem).wait()
            return carry
        return one

    lax.fori_loop(0, n // WAIT_GROUP, wait_rows(WAIT_GROUP * SEG_ALIGN), 0)
    lax.fori_loop(0, n % WAIT_GROUP, wait_rows(SEG_ALIGN), 0)


def _sort_tile(t_ref, route_ref, lbase, local):
    tm = t_ref.shape[0]
    s_rows = local.shape[0]
    lp1, lp2 = _local_positions(route_ref[...], lbase)
    lane = lax.broadcasted_iota(jnp.int32, (tm, LANES), 1)
    lp_rows = jnp.where(lane == 0, lp1, jnp.where(lane == 1, lp2, -1.0)).T
    row = lax.broadcasted_iota(jnp.int32, (s_rows, tm), 0).astype(F32)
    onehot = jnp.where((row == lp_rows[0:1, :]) | (row == lp_rows[1:2, :]), 1.0, 0.0).astype(BF16)
    yield
    local[...] = _dot(onehot, t_ref[...]).astype(BF16)


def _sort_rows_kernel(dst_ref, nq_ref, tpos_ref, tn_ref, nt_ref, t_ref, route_ref, lbase_ref,
                      xs_ref, local_scr, zero_scr, cnt_scr, sem, zsem):
    i = pl.program_id(0)
    n = pl.num_programs(0)
    slot = i % 2
    per_step = local_scr.shape[1]

    def buf(s, k):
        return local_scr.at[s, k], sem.at[s * per_step + k], s * per_step + k

    def wait_slot(s):
        for k in range(per_step):
            local, sm, c = buf(s, k)
            _wait_chunks(cnt_scr[c], local, xs_ref, sm)

    @pl.when(i >= 2)
    def _():
        wait_slot(slot)

    tiles = []
    for k in range(per_step):
        rows = slice(k * TM_MIX, (k + 1) * TM_MIX)
        tiles.append(_sort_tile(t_ref.at[rows, :], route_ref.at[rows, :], lbase_ref[k], buf(slot, k)[0]))
    _lockstep(tiles)
    for k in range(per_step):
        local, sm, c = buf(slot, k)
        cnt_scr[c] = _segment_copies(i * per_step + k, dst_ref, nq_ref, local, xs_ref, sm, True)

    @pl.when(i == n - 1)
    def _():
        @pl.when(n >= 2)
        def _():
            wait_slot(1 - slot)

        wait_slot(slot)
        zero_scr[...] = jnp.zeros_like(zero_scr)
        te = zero_scr.shape[0]
        zero_chunk = zero_scr.at[pl.ds(0, SEG_ALIGN)]

        def tail(e, total):
            def chunk(c, carry):
                dst = xs_ref.at[pl.ds(pl.multiple_of(tpos_ref[e] + c * SEG_ALIGN, SEG_ALIGN), SEG_ALIGN)]
                pltpu.make_async_copy(zero_chunk, dst, zsem).start()
                return carry

            lax.fori_loop(0, tn_ref[e], chunk, 0)
            return total + tn_ref[e]

        _wait_chunks(lax.fori_loop(0, N_EXPERTS, tail, 0), zero_scr, xs_ref, zsem)

        def unused_tile(j, carry):
            pltpu.make_async_copy(zero_scr, xs_ref.at[pl.ds(pl.multiple_of(j * te, te), te)], zsem).start()
            return carry

        def unused_wait(j, carry):
            pltpu.make_async_copy(zero_scr, xs_ref.at[pl.ds(0, te)], zsem).wait()
            return carry

        lax.fori_loop(nt_ref[0], xs_ref.shape[0] // te, unused_tile, 0)
        lax.fori_loop(nt_ref[0], xs_ref.shape[0] // te, unused_wait, 0)


def _plan_specs(plan):
    keys = ('dst', 'n_chunks', 'tail_pos', 'tail_n')
    return [plan[k] for k in keys]


MOE_TILES = 2


def _sort_rows(plan, n_tiles, t_bf, route, lbase_f, n_sorted):
    t, dm = t_bf.shape
    tm = TM_MIX * MOE_TILES
    s_rows = _local_rows(TM_MIX)
    im = lambda i, *_: (i, 0)
    return pl.pallas_call(
        _sort_rows_kernel,
        grid_spec=pltpu.PrefetchScalarGridSpec(
            num_scalar_prefetch=5,
            grid=(t // tm,),
            in_specs=[pl.BlockSpec((tm, dm), im), pl.BlockSpec((tm, LANES), im),
                      pl.BlockSpec((MOE_TILES, 1, LANES), lambda i, *_: (i, 0, 0))],
            out_specs=pl.BlockSpec(memory_space=pl.ANY),
            scratch_shapes=[pltpu.VMEM((2, MOE_TILES, s_rows, dm), BF16), pltpu.VMEM((TM_EXPERT, dm), BF16),
                            pltpu.SMEM((2 * MOE_TILES,), jnp.int32), pltpu.SemaphoreType.DMA((2 * MOE_TILES,)),
                            pltpu.SemaphoreType.DMA(())],
        ),
        out_shape=jax.ShapeDtypeStruct((n_sorted, dm), BF16),
        compiler_params=pltpu.CompilerParams(dimension_semantics=("arbitrary",),
                                             vmem_limit_bytes=VMEM_LIMIT),
        name="sort_rows",
    )(*_plan_specs(plan), n_tiles, t_bf, route, lbase_f)


def _expert_weight_copies(e, slot, hbm, stage, sem):
    return [pltpu.make_async_copy(h.at[e], s.at[slot], sem.at[slot]) for h, s in zip(hbm, stage)]


def _experts_kernel(te_ref, nt_ref, nxt_ref, xs_ref, wg_ref, wu_ref, wd_ref, ys_ref,
                    sg, su, sd, wg_bf, wu_bf, wd_bf, slot_scr, sem):
    i = pl.program_id(0)
    e = te_ref[i]
    hbm, stage = (wg_ref, wu_ref, wd_ref), (sg, su, sd)

    @pl.when(i == 0)
    def _():
        slot_scr[0] = 0
        for cp in _expert_weight_copies(e, 0, hbm, stage, sem):
            cp.start()

    @pl.when((i == 0) | (e != te_ref[jnp.maximum(i - 1, 0)]))
    def _():
        slot = slot_scr[0]
        for cp in _expert_weight_copies(e, slot, hbm, stage, sem):
            cp.wait()
        wg_bf[...] = sg[slot].astype(BF16)
        wu_bf[...] = su[slot].astype(BF16)
        wd_bf[...] = sd[slot].astype(BF16)
        nxt = nxt_ref[e]

        @pl.when(nxt != e)
        def _():
            for cp in _expert_weight_copies(nxt, 1 - slot, hbm, stage, sem):
                cp.start()

        slot_scr[0] = 1 - slot

    @pl.when(i < nt_ref[0])
    def _():
        x = xs_ref[...]
        hidden = (jax.nn.silu(_dot(x, wg_bf[...])) * _dot(x, wu_bf[...])).astype(BF16)
        ys_ref[...] = _dot(hidden, wd_bf[...]).astype(BF16)

    @pl.when(i >= nt_ref[0])
    def _():
        ys_ref[...] = jnp.zeros_like(ys_ref)


def _experts(tile_expert, n_tiles, next_expert, x_sorted, w_gate, w_up, w_down):
    n_sorted, dm = x_sorted.shape
    de = w_gate.shape[2]
    tm = TM_EXPERT
    return pl.pallas_call(
        _experts_kernel,
        grid_spec=pltpu.PrefetchScalarGridSpec(
            num_scalar_prefetch=3,
            grid=(n_sorted // tm,),
            in_specs=[pl.BlockSpec((tm, dm), lambda i, te, nt, nx: (jnp.minimum(i, nt[0] - 1), 0)),
                      pl.BlockSpec(memory_space=pl.ANY), pl.BlockSpec(memory_space=pl.ANY),
                      pl.BlockSpec(memory_space=pl.ANY)],
            out_specs=pl.BlockSpec((tm, dm), lambda i, te, nt, nx: (i, 0)),
            scratch_shapes=[pltpu.VMEM((2, dm, de), F32), pltpu.VMEM((2, dm, de), F32), pltpu.VMEM((2, de, dm), F32),
                            pltpu.VMEM((dm, de), BF16), pltpu.VMEM((dm, de), BF16), pltpu.VMEM((de, dm), BF16),
                            pltpu.SMEM((1,), jnp.int32), pltpu.SemaphoreType.DMA((2,))],
        ),
        out_shape=jax.ShapeDtypeStruct((n_sorted, dm), BF16),
        compiler_params=pltpu.CompilerParams(dimension_semantics=("arbitrary",),
                                             vmem_limit_bytes=VMEM_LIMIT),
        name="experts",
    )(tile_expert, n_tiles, next_expert, x_sorted, w_gate, w_up, w_down)


def _combine_tile(x2_ref, route_ref, lbase, gfin_ref, local, o_ref):
    tm = x2_ref.shape[0]
    s_rows = local.shape[0]
    route = route_ref[...]
    lp1, lp2 = _local_positions(route, lbase)
    w1, w2 = route[:, R_W1:R_W1 + 1], route[:, R_W2:R_W2 + 1]
    col0 = lax.broadcasted_iota(jnp.int32, (tm, KB), 1).astype(F32)
    moe = None
    for k in range(s_rows // KB):
        col = col0 + float(k * KB)
        pick = (jnp.where(col == lp1, w1, 0.0) + jnp.where(col == lp2, w2, 0.0)).astype(BF16)
        part = _dot(pick, local[k * KB:(k + 1) * KB, :])
        moe = part if moe is None else moe + part
    yield
    o_ref[...] = _rms(x2_ref[...] + moe, gfin_ref[...])


def _combine_kernel(dst_ref, nq_ref, tpos_ref, tn_ref, x2_ref, route_ref, lbase_ref, gfin_ref, ys_ref,
                    o_ref, local_scr, cnt_scr, sem):
    del tpos_ref, tn_ref
    i = pl.program_id(0)
    n = pl.num_programs(0)
    slot = i % 2
    per_step = local_scr.shape[1]
    s_rows = local_scr.shape[2]

    def fetch(step, s):
        for k in range(per_step):
            tile = step * per_step + k

            def clear(r, carry, k=k):
                local_scr[s, k, pl.ds(pl.multiple_of(r * SEG_ALIGN, SEG_ALIGN), SEG_ALIGN), :] = jnp.zeros(
                    (SEG_ALIGN, local_scr.shape[3]), BF16)
                return carry

            lax.fori_loop(nq_ref[tile], s_rows // SEG_ALIGN, clear, 0)
            cnt_scr[s * per_step + k] = _segment_copies(tile, dst_ref, nq_ref, local_scr.at[s, k], ys_ref,
                                                        sem.at[s * per_step + k], False)

    @pl.when(i == 0)
    def _():
        fetch(0, 0)

    @pl.when(i + 1 < n)
    def _():
        fetch(i + 1, 1 - slot)

    tiles = []
    for k in range(per_step):
        c = slot * per_step + k
        _wait_chunks(cnt_scr[c], local_scr.at[slot, k], ys_ref, sem.at[c])
        rows = slice(k * TM_MIX, (k + 1) * TM_MIX)
        tiles.append(_combine_tile(x2_ref.at[rows, :], route_ref.at[rows, :], lbase_ref[k], gfin_ref,
                                   local_scr.at[slot, k], o_ref.at[rows, :]))
    _lockstep(tiles)


def _combine(plan, x2, route, lbase_f, gfin, y_sorted):
    t, dm = x2.shape
    tm = TM_MIX * MOE_TILES
    s_rows = _local_rows(TM_MIX)
    im = lambda i, *_: (i, 0)
    return pl.pallas_call(
        _combine_kernel,
        grid_spec=pltpu.PrefetchScalarGridSpec(
            num_scalar_prefetch=4,
            grid=(t // tm,),
            in_specs=[pl.BlockSpec((tm, dm), im), pl.BlockSpec((tm, LANES), im),
                      pl.BlockSpec((MOE_TILES, 1, LANES), lambda i, *_: (i, 0, 0)),
                      pl.BlockSpec((1, dm), lambda i, *_: (0, 0)),
                      pl.BlockSpec(memory_space=pl.ANY)],
            out_specs=pl.BlockSpec((tm, dm), im),
            scratch_shapes=[pltpu.VMEM((2, MOE_TILES, s_rows, dm), BF16), pltpu.SMEM((2 * MOE_TILES,), jnp.int32),
                            pltpu.SemaphoreType.DMA((2 * MOE_TILES,))],
        ),
        out_shape=jax.ShapeDtypeStruct((t, dm), F32),
        compiler_params=pltpu.CompilerParams(dimension_semantics=("arbitrary",),
                                             vmem_limit_bytes=VMEM_LIMIT),
        name="combine_norm",
    )(*_plan_specs(plan), x2, route, lbase_f, gfin, y_sorted)


def _layer(x, p, s5_ops, gfin):
    b, l, dm = x.shape
    x2d = x.reshape(b * l, dm)
    mgm, us5 = _inproj_gmlp(x2d, p['gmix'], p['win'], p['lng'], p['lnb'], p['ws'], p['bs'], p['gout_gm'])
    n_seg = SUBLANES // b
    *lag_factors, w1, w2, sc = s5_ops[(l // (S5_LC * n_seg))]
    xg = _s5_inproj(x, p['gmix'], p['win_s5'], n_seg)
    yg = _s5_scan(xg, lag_factors, w1, w2, sc, n_seg)
    ys = _s5_to_tokens(yg, b, l, n_seg)
    x2, t_bf, route, counts = _mix_route(ys, us5, mgm, x2d, p['d'], p['gluw'], p['glub'], p['gout_s5'],
                                         p['wout'], p['gffn'], p['rw_hl'], p['rb'])
    plan, lbase_f, tiles, n_sorted = _segment_plan(counts, b * l, TM_EXPERT)
    x_sorted = _sort_rows(plan, tiles[1], t_bf, route, lbase_f, n_sorted)
    y_sorted = _experts(*tiles, x_sorted, p['w_gate'], p['w_up'], p['w_down'])
    out = _combine(plan, x2, route, lbase_f, gfin, y_sorted)
    return out.reshape(b, l, dm)


def kernel(x_prompt, x_sample, norm_mix_g, w_in, gm_ln_g, gm_ln_b, gm_ws, gm_bs, s5_lam_re_fwd, s5_lam_im_fwd, s5_log_step_fwd, s5_b_re_fwd, s5_b_im_fwd, s5_c_re_fwd, s5_c_im_fwd, s5_lam_re_bwd, s5_lam_im_bwd, s5_log_step_bwd, s5_b_re_bwd, s5_b_im_bwd, s5_c_re_bwd, s5_c_im_bwd, s5_d, s5_glu_w, s5_glu_b, out_norm_gm, out_norm_s5, w_out, norm_ffn_g, r1_w, r1_b, r2_w, r2_b, e_w_gate, e_w_up, e_w_down, norm_final_g):
    depth = w_in.shape[0]
    gfin = norm_final_g.reshape(1, -1).astype(F32)
    xs = [x_prompt, x_sample]
    for li in range(depth):
        row = lambda a: a[li].reshape(1, -1).astype(F32)
        dm = w_in.shape[1]
        gw = gm_ln_g.shape[1]
        hd_dim = gw // GM_HEADS
        rw = jnp.concatenate([r1_w[li], r2_w[li].transpose(1, 0, 2).reshape(dm, N_EXPERTS)], axis=1).astype(F32)
        rw = jnp.pad(rw, ((0, 0), (0, LANES - rw.shape[1])))
        rwh = rw.astype(BF16)
        rwl = (rw - rwh.astype(F32)).astype(BF16)
        rb = jnp.concatenate([r1_b[li], r2_b[li].reshape(-1)]).astype(F32)
        rb = jnp.pad(rb, (0, LANES - rb.shape[0])).reshape(1, LANES)
        p = dict(
            gmix=row(norm_mix_g), win=w_in[li].astype(BF16), win_s5=w_in[li][:, 2 * gw:].astype(BF16),
            lng=row(gm_ln_g), lnb=row(gm_ln_b),
            ws=gm_ws[li].astype(BF16),
            bs=jnp.broadcast_to(gm_bs[li].astype(F32)[:, :, None], (GM_HEADS, CHUNK, hd_dim)),
            gout_gm=row(out_norm_gm), d=row(s5_d), gluw=s5_glu_w[li].astype(BF16), glub=row(s5_glu_b),
            gout_s5=row(out_norm_s5), wout=w_out[li].astype(BF16), gffn=row(norm_ffn_g),
            rw_hl=jnp.concatenate([rwh, rwl], axis=1), rb=rb,
            w_gate=e_w_gate[li], w_up=e_w_up[li], w_down=e_w_down[li],
        )
        fwd = (s5_lam_re_fwd[li], s5_lam_im_fwd[li], s5_log_step_fwd[li], s5_b_re_fwd[li], s5_b_im_fwd[li],
               s5_c_re_fwd[li], s5_c_im_fwd[li])
        bwd = (s5_lam_re_bwd[li], s5_lam_im_bwd[li], s5_log_step_bwd[li], s5_b_re_bwd[li], s5_b_im_bwd[li],
               s5_c_re_bwd[li], s5_c_im_bwd[li])
        s5_ops = {}
        for x in xs:
            seg_steps = x.shape[1] // (S5_LC * (SUBLANES // x.shape[0]))
            if seg_steps not in s5_ops:
                s5_ops[seg_steps] = _s5_operator(fwd, bwd, S5_LC, seg_steps)
        last = li == depth - 1
        assert last, "depth > 1 needs an un-normalised layer output"
        xs = [_layer(x, p, s5_ops, gfin) for x in xs]
    return tuple(xs)
```

```python
import functools
import math

import jax
import jax.numpy as jnp
from jax import lax
from jax.experimental import pallas as pl
from jax.experimental.pallas import tpu as pltpu

F32 = jnp.float32
BF16 = jnp.bfloat16

EPS = 1e-6
LAMBDA_RE_MAX = -1e-4
GM_HEADS = 4
CHUNK = 128
S5_GROUP = 16
S5_STATE = 64
N_COARSE = 4
N_FINE = 8
N_EXPERTS = N_COARSE * N_FINE

LANES = 128
SUBLANES = 8
S5_LC = 16
VMEM_LIMIT = 56 * 1024 * 1024

TM_PROJ = 1024
TM_MIX = 512
MIX_TILES = 2
KB = 256
TM_EXPERT = 1024
SEG_ALIGN = 16


def _gelu(x):
    c = math.sqrt(2.0 / math.pi)
    half = 0.5 * x
    return half + half * jnp.tanh(x * (c + (c * 0.044715) * (x * x)))


def _rms(x, g):
    ms = jnp.mean(x * x, axis=-1, keepdims=True)
    return x * lax.rsqrt(ms + EPS) * g


def _dot(a, b):
    return jnp.dot(a, b, preferred_element_type=F32)


def _lockstep(tiles):
    while tiles:
        tiles = [t for t in tiles if next(t, "done") != "done"]


def _inproj_gmlp_kernel(x_ref, gmix_ref, win_ref, lng_ref, lnb_ref, ws_ref, bs_ref, gout_ref,
                        mgm_ref, us5_ref, y_scr):
    tm = x_ref.shape[0]
    gw = mgm_ref.shape[1]
    hd_dim = gw // GM_HEADS
    n_chunks = tm // CHUNK
    h = _rms(x_ref[...], gmix_ref[...]).astype(BF16)
    proj = _dot(h, win_ref[...])
    us5_ref[...] = proj[:, 2 * gw:]
    for hd in range(GM_HEADS):
        lo = hd * hd_dim
        vh = _gelu(proj[:, gw + lo:gw + lo + hd_dim])
        mu = jnp.mean(vh, axis=-1, keepdims=True)
        xc = vh - mu
        var = jnp.mean(xc * xc, axis=-1, keepdims=True)
        vn = (xc * lax.rsqrt(var + EPS) * lng_ref[:, lo:lo + hd_dim]
              + lnb_ref[:, lo:lo + hd_dim]).astype(BF16)
        rhs = jnp.concatenate([vn[c * CHUNK:(c + 1) * CHUNK] for c in range(n_chunks)], axis=1)
        s = _dot(ws_ref[hd], rhs)
        for c in range(n_chunks):
            sc = s[:, c * hd_dim:(c + 1) * hd_dim] + bs_ref[hd]
            u = _gelu(proj[c * CHUNK:(c + 1) * CHUNK, lo:lo + hd_dim])
            y_scr[c * CHUNK:(c + 1) * CHUNK, lo:lo + hd_dim] = u * sc
    mgm_ref[...] = _rms(y_scr[...], gout_ref[...]).astype(BF16)


def _inproj_gmlp(x2d, gmix, win_bf, lng, lnb, ws_bf, bs_b, gout):
    t, d = x2d.shape
    d_in = win_bf.shape[1]
    gw = lng.shape[1]
    s5w = d_in - 2 * gw
    tm = TM_PROJ
    const = lambda *shape: pl.BlockSpec(shape, lambda i: (0,) * len(shape))
    return pl.pallas_call(
        _inproj_gmlp_kernel,
        grid=(t // tm,),
        in_specs=[
            pl.BlockSpec((tm, d), lambda i: (i, 0)),
            const(1, d), const(d, d_in), const(1, gw), const(1, gw),
            const(GM_HEADS, CHUNK, CHUNK), const(GM_HEADS, CHUNK, gw // GM_HEADS), const(1, gw),
        ],
        out_specs=[pl.BlockSpec((tm, gw), lambda i: (i, 0)),
                   pl.BlockSpec((tm, s5w), lambda i: (i, 0))],
        out_shape=[jax.ShapeDtypeStruct((t, gw), BF16),
                   jax.ShapeDtypeStruct((t, s5w), F32)],
        scratch_shapes=[pltpu.VMEM((tm, gw), F32)],
        compiler_params=pltpu.CompilerParams(dimension_semantics=("parallel",),
                                             vmem_limit_bytes=VMEM_LIMIT),
        name="inproj_gmlp",
    )(x2d, gmix, win_bf, lng, lnb, ws_bf, bs_b, gout)


def _s5_consts(lam_re, lam_im, log_step, b_re, b_im, c_re, c_im, lc):
    lr = jnp.minimum(lam_re.astype(F32), LAMBDA_RE_MAX)
    li = lam_im.astype(F32)
    step = jnp.exp(log_step.astype(F32))[:, None]
    dr, di = lr * step, li * step
    ar, ai = _cexp(dr, di)
    nr, ni = ar - 1.0, ai
    den = lr * lr + li * li
    qr, qi = (nr * lr + ni * li) / den, (ni * lr - nr * li) / den
    br, bi = b_re.astype(F32), b_im.astype(F32)
    bbr = qr[..., None] * br - qi[..., None] * bi
    bbi = qr[..., None] * bi + qi[..., None] * br
    k = jnp.arange(lc + 1, dtype=F32)[:, None, None]
    pwr, pwi = _cexp(k * dr[None], k * di[None])
    return (dr, di), (pwr, pwi), (bbr, bbi), (c_re.astype(F32), c_im.astype(F32))


def _cexp(zr, zi):
    m = jnp.exp(zr)
    return m * jnp.cos(zi), m * jnp.sin(zi)


def _s5_operator(fwd, bwd, lc, seg_steps):
    consts = [_s5_consts(*fwd, lc), _s5_consts(*bwd, lc)]
    g, p, h = consts[0][2][0].shape
    lags, w1_parts, w2_parts, sc_rows, seg_rows = [], [], [], [], []
    for direction, (ld, pw, bb, c) in enumerate(consts):
        (dr, di), (pwr, pwi), (bbr, bbi), (cr, ci) = ld, pw, bb, c
        crt, cit = cr.transpose(0, 2, 1), ci.transpose(0, 2, 1)
        pwrt, pwit = pwr.transpose(1, 2, 0), pwi.transpose(1, 2, 0)
        cpr = crt[:, :, None, :] * pwrt[:, :, :, None] - cit[:, :, None, :] * pwit[:, :, :, None]
        cpi = crt[:, :, None, :] * pwit[:, :, :, None] + cit[:, :, None, :] * pwrt[:, :, :, None]
        ck = jnp.concatenate([cpr[:, :, :lc], cpi[:, :, :lc]], axis=1)
        if direction == 1:
            ck = jnp.flip(ck, 2)
        lags += [jnp.concatenate([bbr.transpose(0, 2, 1), -bbi.transpose(0, 2, 1)], axis=-1),
                 ck.reshape(g, 2 * p, lc * h)]
        er, ei = pwrt[:, :, :lc].transpose(0, 2, 1), pwit[:, :, :lc].transpose(0, 2, 1)
        if direction == 0:
            er, ei = jnp.flip(er, 1), jnp.flip(ei, 1)
        bbrt, bbit = bbr.transpose(0, 2, 1), bbi.transpose(0, 2, 1)
        e1 = jnp.concatenate([er, er], -1)[:, :, None, :]
        e2 = jnp.concatenate([-ei, ei], -1)[:, :, None, :]
        b_ri = jnp.concatenate([bbrt, bbit], -1)[:, None]
        b_ir = jnp.concatenate([bbit, bbrt], -1)[:, None]
        w1_parts += [e1 * b_ri + e2 * b_ir, e1 * b_ir - e2 * b_ri]
        fr, fi = cpr[:, :, 1:lc + 1], cpi[:, :, 1:lc + 1]
        if direction == 1:
            fr, fi = jnp.flip(fr, 2), jnp.flip(fi, 2)
        w2_parts += [fr, -fi]

        def mult(zr, zi):
            return [jnp.concatenate([zr, zr], -1), jnp.concatenate([-zi, zi], -1)]

        sc_rows += mult(*_cexp(lc * dr, lc * di))
        seg_rows += mult(*_cexp((lc * seg_steps) * dr, (lc * seg_steps) * di))
    w1 = jnp.concatenate(w1_parts, axis=-1).reshape(g, lc * h, 8 * p)
    w2 = jnp.concatenate(w2_parts, axis=1).reshape(g, 4 * p, lc * h)
    sc = jnp.stack(sc_rows + seg_rows, axis=1)
    return tuple(lags) + (w1.astype(BF16), w2.astype(BF16), sc.astype(F32))


S5_GPS = 2


def _s5_kernel(x_ref, bbf_ref, cpf_ref, bbb_ref, cpb_ref, w1_ref, w2_ref, sc_ref, y_ref,
               loc_scr, sin_scr, m_scr, *, n_seg):
    gps, rows, kw = x_ref.shape
    steps = rows // SUBLANES
    sw = sc_ref.shape[2]

    for gi in range(gps):
        kf = jnp.dot(bbf_ref[gi], cpf_ref[gi], precision=lax.Precision.HIGHEST, preferred_element_type=F32)
        kb = jnp.dot(bbb_ref[gi], cpb_ref[gi], precision=lax.Precision.HIGHEST, preferred_element_type=F32)
        hch = kf.shape[0]
        lc = kw // hch
        lane = lax.broadcasted_iota(jnp.int32, kf.shape, 1)
        for s in range(lc):
            f = kf if s == 0 else jnp.where(lane >= s * hch, pltpu.roll(kf, s * hch, 1), 0.0)
            left = (lc - 1 - s) * hch
            b = kb if left == 0 else pltpu.roll(kb, kw - left, 1)
            m_scr[gi, s * hch:(s + 1) * hch, :] = (f + jnp.where(lane < (s + 1) * hch, b, 0.0)).astype(BF16)
        loc_scr[gi] = _dot(x_ref[gi], w1_ref[gi])

    def bc(gi, i):
        return jnp.broadcast_to(sc_ref[gi, i:i + 1, :], (SUBLANES, sw))

    mult = [[bc(gi, i) for i in range(8)] for gi in range(gps)]

    def step(gi, s, state):
        f, fs, b, bs = state
        a1f, a2f, a1b, a2b = mult[gi][:4]
        rf = pl.multiple_of(s * SUBLANES, SUBLANES)
        rb = pl.multiple_of((steps - 1 - s) * SUBLANES, SUBLANES)
        lf = loc_scr[gi, pl.ds(rf, SUBLANES), 0:sw]
        lfs = loc_scr[gi, pl.ds(rf, SUBLANES), sw:2 * sw]
        lb = loc_scr[gi, pl.ds(rb, SUBLANES), 2 * sw:3 * sw]
        lbs = loc_scr[gi, pl.ds(rb, SUBLANES), 3 * sw:4 * sw]
        return (a1f * f + a2f * fs + lf, a1f * fs - a2f * f + lfs,
                a1b * b + a2b * bs + lb, a1b * bs - a2b * b + lbs)

    zero = jnp.zeros((SUBLANES, sw), F32)

    def pass1(s, carry):
        return tuple(step(gi, s, carry[gi]) for gi in range(gps))

    ends = lax.fori_loop(0, steps, pass1, tuple((zero,) * 4 for _ in range(gps)), unroll=4)

    seg = lax.broadcasted_iota(jnp.int32, (SUBLANES, sw), 0) % n_seg
    enter = []
    for gi in range(gps):
        f_end, fs_end, b_end, bs_end = ends[gi]
        p1f, p2f, p1b, p2b = mult[gi][4:]
        cf, cfs, cb, cbs = zero, zero, zero, zero
        for _ in range(n_seg - 1):
            ef = f_end + p1f * cf + p2f * cfs
            efs = fs_end + p1f * cfs - p2f * cf
            eb = b_end + p1b * cb + p2b * cbs
            ebs = bs_end + p1b * cbs - p2b * cb
            cf = jnp.where(seg >= 1, pltpu.roll(ef, 1, 0), 0.0)
            cfs = jnp.where(seg >= 1, pltpu.roll(efs, 1, 0), 0.0)
            cb = jnp.where(seg <= n_seg - 2, pltpu.roll(eb, SUBLANES - 1, 0), 0.0)
            cbs = jnp.where(seg <= n_seg - 2, pltpu.roll(ebs, SUBLANES - 1, 0), 0.0)
        enter.append((cf, cfs, cb, cbs))

    def pass2(s, carry):
        rf = pl.multiple_of(s * SUBLANES, SUBLANES)
        rb = pl.multiple_of((steps - 1 - s) * SUBLANES, SUBLANES)
        for gi in range(gps):
            sin_scr[gi, pl.ds(rf, SUBLANES), 0:sw] = carry[gi][0]
            sin_scr[gi, pl.ds(rb, SUBLANES), sw:2 * sw] = carry[gi][2]
        return tuple(step(gi, s, carry[gi]) for gi in range(gps))

    lax.fori_loop(0, steps, pass2, tuple(enter), unroll=4)

    for gi in range(gps):
        y_ref[gi] = _dot(x_ref[gi], m_scr[gi]) + _dot(sin_scr[gi].astype(BF16), w2_ref[gi])


def _s5_scan(xg, lag_factors, w1, w2, sc, n_seg):
    g, rows, kw = xg.shape
    sw = sc.shape[2]
    gps = S5_GPS
    blk = lambda a: pl.BlockSpec((gps,) + a.shape[1:], lambda i: (i, 0, 0))
    return pl.pallas_call(
        functools.partial(_s5_kernel, n_seg=n_seg),
        grid=(g // gps,),
        in_specs=[blk(xg)] + [blk(a) for a in lag_factors] + [blk(w1), blk(w2), blk(sc)],
        out_specs=pl.BlockSpec((gps, rows, kw), lambda i: (i, 0, 0)),
        out_shape=jax.ShapeDtypeStruct((g, rows, kw), F32),
        scratch_shapes=[pltpu.VMEM((gps, rows, 4 * sw), F32), pltpu.VMEM((gps, rows, 2 * sw), F32),
                        pltpu.VMEM((gps, kw, kw), BF16)],
        compiler_params=pltpu.CompilerParams(dimension_semantics=("parallel",),
                                             vmem_limit_bytes=VMEM_LIMIT),
        name="s5_scan",
    )(xg, *lag_factors, w1, w2, sc)


S5_NM = 16


def _block_transpose8(groups, width):
    lane = lax.broadcasted_iota(jnp.int32, groups[0][0].shape, 1)
    for d in (4, 2, 1):
        w = width * d
        hi = ((lane // w) % 2) == 1
        nxt = []
        for v in groups:
            out = list(v)
            for i0 in range(8):
                if i0 & d:
                    continue
                i1 = i0 + d
                out[i0] = jnp.where(hi, pltpu.roll(v[i1], w, 1), v[i0])
                out[i1] = jnp.where(hi, v[i1], pltpu.roll(v[i0], 8 * width - w, 1))
            nxt.append(out)
        groups = nxt
    return groups


def _tile_copies(hbm4, tile, buf, slot, sem, nm, to_hbm):
    copies = []
    for c in range(SUBLANES):
        for j in range(S5_LC):
            h = hbm4.at[c, pl.ds(tile * nm, nm), pl.ds(j, 1), :]
            v = buf.at[slot, j, :, pl.ds(c, 1), :]
            copies.append(pltpu.make_async_copy(v, h, sem.at[slot]) if to_hbm
                          else pltpu.make_async_copy(h, v, sem.at[slot]))
    return copies


def _s5_inproj_kernel(x4_ref, gmix_ref, w_ref, xg_ref, xs, sem, *, nm):
    i = pl.program_id(0)
    n = pl.num_programs(0)
    slot = i % 2
    dm = x4_ref.shape[3]

    @pl.when(i == 0)
    def _():
        for cp in _tile_copies(x4_ref, 0, xs, 0, sem, nm, False):
            cp.start()

    @pl.when(i + 1 < n)
    def _():
        for cp in _tile_copies(x4_ref, i + 1, xs, 1 - slot, sem, nm, False):
            cp.start()

    pltpu.make_async_copy(xs.at[slot], xs.at[slot], sem.at[slot]).wait()
    rows = nm * SUBLANES

    def half(a):
        x = xs[slot, 8 * a:8 * a + 8].reshape(8 * rows, dm)
        h = _rms(x, gmix_ref[...]).astype(BF16)
        yield
        z = _dot(h, w_ref[...])
        yield
        n_oct = z.shape[1] // LANES
        blocks = [[z[j8 * rows:(j8 + 1) * rows, q * LANES:(q + 1) * LANES] for j8 in range(8)] for q in range(n_oct)]
        for q, out in enumerate(_block_transpose8(blocks, S5_GROUP)):
            for g8, b in enumerate(out):
                xg_ref[8 * q + g8, :, a * LANES:(a + 1) * LANES] = b.astype(BF16)

    _lockstep([half(a) for a in range(S5_LC // 8)])


def _s5_inproj(x, gmix, w_s5_bf, n_seg):
    b, l, dm = x.shape
    steps = l // (S5_LC * n_seg)
    nm = S5_NM
    s5w = w_s5_bf.shape[1]
    g = s5w // S5_GROUP
    x4 = x.reshape(b * n_seg, steps, S5_LC, dm)
    return pl.pallas_call(
        functools.partial(_s5_inproj_kernel, nm=nm),
        grid=(steps // nm,),
        in_specs=[pl.BlockSpec(memory_space=pl.ANY),
                  pl.BlockSpec((1, dm), lambda i: (0, 0)),
                  pl.BlockSpec((dm, s5w), lambda i: (0, 0))],
        out_specs=pl.BlockSpec((g, nm * SUBLANES, S5_LC * S5_GROUP), lambda i: (0, i, 0)),
        out_shape=jax.ShapeDtypeStruct((g, steps * SUBLANES, S5_LC * S5_GROUP), BF16),
        scratch_shapes=[pltpu.VMEM((2, S5_LC, nm, SUBLANES, dm), F32), pltpu.SemaphoreType.DMA((2,))],
        compiler_params=pltpu.CompilerParams(dimension_semantics=("arbitrary",),
                                             vmem_limit_bytes=VMEM_LIMIT),
        name="s5_inproj",
    )(x4, gmix, w_s5_bf)


def _s5_to_tokens_kernel(yg_ref, ys4_ref, zs, sem, *, nm):
    i = pl.program_id(0)
    n = pl.num_programs(0)
    slot = i % 2
    rows = nm * SUBLANES

    def wait(s):
        pltpu.make_async_copy(zs.at[s], zs.at[s], sem.at[s]).wait()

    @pl.when(i >= 2)
    def _():
        wait(slot)

    n_oct = yg_ref.shape[0] // 8
    combos = [(q, a) for q in range(n_oct) for a in range(S5_LC // 8)]
    blocks = [[yg_ref[8 * q + g8, :, a * LANES:(a + 1) * LANES] for g8 in range(8)] for q, a in combos]
    for (q, a), out in zip(combos, _block_transpose8(blocks, S5_GROUP)):
        for j8, b in enumerate(out):
            zs[slot, 8 * a + j8, :, :, q * LANES:(q + 1) * LANES] = b.reshape(nm, SUBLANES, LANES)
    for cp in _tile_copies(ys4_ref, i, zs, slot, sem, nm, True):
        cp.start()

    @pl.when(i == n - 1)
    def _():
        wait(1 - slot)
        wait(slot)


def _s5_to_tokens(yg, b, l, n_seg):
    g, rows_total, kw = yg.shape
    steps = rows_total // SUBLANES
    nm = S5_NM
    s5w = g * S5_GROUP
    assert steps // nm >= 2
    ys4 = pl.pallas_call(
        functools.partial(_s5_to_tokens_kernel, nm=nm),
        grid=(steps // nm,),
        in_specs=[pl.BlockSpec((g, nm * SUBLANES, kw), lambda i: (0, i, 0))],
        out_specs=pl.BlockSpec(memory_space=pl.ANY),
        out_shape=jax.ShapeDtypeStruct((b * n_seg, steps, S5_LC, s5w), F32),
        scratch_shapes=[pltpu.VMEM((2, S5_LC, nm, SUBLANES, s5w), F32), pltpu.SemaphoreType.DMA((2,))],
        compiler_params=pltpu.CompilerParams(dimension_semantics=("arbitrary",),
                                             vmem_limit_bytes=VMEM_LIMIT),
        name="s5_to_tokens",
    )(yg)
    return ys4.reshape(b * l, s5w)


R_E1, R_E2, R_W1, R_W2, R_RANK1, R_RANK2 = range(6)


def _mix_route_kernel(ys_ref, us5_ref, mgm_ref, x_ref, d_ref, gluw_ref, glub_ref, gs5_ref,
                      wout_ref, gffn_ref, rw_ref, rb_ref, tri_ref,
                      x2_ref, t_ref, route_ref, cnt_ref):
    tiles = []
    for k in range(x_ref.shape[0] // TM_MIX):
        rows = slice(k * TM_MIX, (k + 1) * TM_MIX)
        tiles.append(_mix_route_tile(ys_ref.at[rows, :], us5_ref.at[rows, :], mgm_ref.at[rows, :], x_ref.at[rows, :],
                                     d_ref, gluw_ref, glub_ref, gs5_ref, wout_ref, gffn_ref, rw_ref,
                                     rb_ref, tri_ref, x2_ref.at[rows, :], t_ref.at[rows, :], route_ref.at[rows, :],
                                     cnt_ref.at[k]))
    _lockstep(tiles)


def _mix_route_tile(ys_ref, us5_ref, mgm_ref, x_ref, d_ref, gluw_ref, glub_ref, gs5_ref,
                    wout_ref, gffn_ref, rw_ref, rb_ref, tri_ref,
                    x2_ref, t_ref, route_ref, cnt_ref):
    gw = mgm_ref.shape[1]
    y = ys_ref[...] + d_ref[...] * us5_ref[...]
    g = _gelu(y)
    yield
    gate = _dot(g.astype(BF16), gluw_ref[...])
    yield
    z = g * jax.nn.sigmoid(gate + glub_ref[...])
    ms5 = _rms(z, gs5_ref[...]).astype(BF16)
    yield
    mix = _dot(mgm_ref[...], wout_ref[:gw, :]) + _dot(ms5, wout_ref[gw:, :])
    yield
    x2 = x_ref[...] + mix
    x2_ref[...] = x2
    t = _rms(x2, gffn_ref[...])
    t_hi = t.astype(BF16)
    t_ref[...] = t_hi
    t_lo = (t - t_hi.astype(F32)).astype(BF16)
    yield
    hl = _dot(t_hi, rw_ref[...])
    logits = (hl[:, :LANES] + hl[:, LANES:] + _dot(t_lo, rw_ref[:, :LANES])
              + rb_ref[...])
    yield
    lane = lax.broadcasted_iota(jnp.int32, logits.shape, 1).astype(F32)
    neg = jnp.float32(-jnp.inf)

    def first_max(mask):
        vals = jnp.where(mask, logits, neg)
        mx = jnp.max(vals, axis=-1, keepdims=True)
        idx = jnp.min(jnp.where(mask & (vals == mx), lane, float(LANES)), axis=-1, keepdims=True)
        return mx, idx

    coarse = lane < N_COARSE
    m1, grp = first_max(coarse)
    p_grp = 1.0 / jnp.sum(jnp.where(coarse, jnp.exp(logits - m1), 0.0), axis=-1, keepdims=True)
    lo = N_COARSE + grp * N_FINE
    fine = (lane >= lo) & (lane < lo + N_FINE)
    v1, i1 = first_max(fine)
    v2, i2 = first_max(fine & (lane != i1))
    e21 = jnp.exp(v2 - v1)
    w1 = p_grp / (1.0 + e21)
    w2 = p_grp * e21 / (1.0 + e21)
    e1 = i1 - N_COARSE
    e2 = i2 - N_COARSE
    hit1 = lane == e1
    hit2 = lane == e2
    onehot = jnp.where(hit1 | hit2, 1.0, 0.0)
    before = _dot(tri_ref[...], onehot.astype(BF16))
    rank1 = jnp.sum(jnp.where(hit1, before, 0.0), axis=-1, keepdims=True)
    rank2 = jnp.sum(jnp.where(hit2, before, 0.0), axis=-1, keepdims=True)
    tm = onehot.shape[0]
    cnt_ref[...] = before[tm - 1:tm, :] + onehot[tm - 1:tm, :]
    rec = jnp.zeros_like(logits)
    for slot, val in ((R_E1, e1), (R_E2, e2), (R_W1, w1), (R_W2, w2),
                      (R_RANK1, rank1), (R_RANK2, rank2)):
        rec = jnp.where(lane == slot, val, rec)
    route_ref[...] = rec


def _mix_route(ys, us5, mgm, x2d, d, gluw_bf, glub, gs5, wout_bf, gffn, rw_hl, rb):
    t, dm = x2d.shape
    gw = mgm.shape[1]
    s5w = us5.shape[1]
    tm = TM_MIX * MIX_TILES
    tri = jnp.tril(jnp.ones((TM_MIX, TM_MIX), F32), -1).astype(BF16)
    const = lambda *shape: pl.BlockSpec(shape, lambda i: (0,) * len(shape))
    tile = lambda w: pl.BlockSpec((tm, w), lambda i: (i, 0))
    return pl.pallas_call(
        _mix_route_kernel,
        grid=(t // tm,),
        in_specs=[tile(s5w), tile(s5w), tile(gw), tile(dm),
                  const(1, s5w), const(s5w, s5w), const(1, s5w), const(1, s5w),
                  const(gw + s5w, dm), const(1, dm), const(dm, 2 * LANES), const(1, LANES),
                  const(TM_MIX, TM_MIX)],
        out_specs=[tile(dm), tile(dm), tile(LANES), pl.BlockSpec((MIX_TILES, 1, LANES), lambda i: (i, 0, 0))],
        out_shape=[jax.ShapeDtypeStruct((t, dm), F32),
                   jax.ShapeDtypeStruct((t, dm), BF16),
                   jax.ShapeDtypeStruct((t, LANES), F32),
                   jax.ShapeDtypeStruct((t // TM_MIX, 1, LANES), F32)],
        compiler_params=pltpu.CompilerParams(dimension_semantics=("parallel",),
                                             vmem_limit_bytes=VMEM_LIMIT),
        name="mix_route",
    )(ys, us5, mgm, x2d, d, gluw_bf, glub, gs5, wout_bf, gffn, rw_hl, rb, tri)


def _local_rows(tm):
    worst = 2 * tm + N_EXPERTS * (SEG_ALIGN - 1)
    return -(-worst // LANES) * LANES


def _segment_plan(cnt, t, tm_expert):
    c = cnt[:, 0, :N_EXPERTS].astype(jnp.int32)
    n_tok_tiles = c.shape[0]
    al = (c + SEG_ALIGN - 1) // SEG_ALIGN * SEG_ALIGN
    lbase = jnp.cumsum(al, axis=1) - al
    tot = jnp.sum(al, axis=0)
    tot_pad = (tot + tm_expert - 1) // tm_expert * tm_expert
    gbase = jnp.cumsum(tot_pad) - tot_pad
    gpos = gbase[None, :] + jnp.cumsum(al, axis=0) - al
    n_tiles_max = -(-(2 * t + n_tok_tiles * N_EXPERTS * (SEG_ALIGN - 1)) // tm_expert) + N_EXPERTS
    tile_end = jnp.cumsum(tot_pad // tm_expert)
    n_tiles = tile_end[-1:].astype(jnp.int32)
    tile_idx = jnp.arange(n_tiles_max, dtype=jnp.int32)
    tile_expert = jnp.sum((tile_idx[:, None] >= tile_end[None, :]).astype(jnp.int32), axis=1)
    last = jnp.sum((n_tiles - 1 >= tile_end).astype(jnp.int32))
    tile_expert = jnp.where(tile_idx < n_tiles, tile_expert, last).astype(jnp.int32)
    ids = jnp.arange(N_EXPERTS, dtype=jnp.int32)
    later_used = (ids[None, :] > ids[:, None]) & (tot_pad[None, :] > 0)
    next_expert = jnp.min(jnp.where(later_used, ids[None, :], N_EXPERTS), axis=1)
    next_expert = jnp.where(next_expert == N_EXPERTS, ids, next_expert).astype(jnp.int32)
    lbase_f = jnp.pad(lbase.astype(F32), ((0, 0), (0, LANES - N_EXPERTS)))[:, None, :]
    flat = lambda a: a.reshape(-1).astype(jnp.int32)
    nch = al // SEG_ALIGN
    cum = jnp.cumsum(nch, axis=1)
    q = jnp.arange(_local_rows(TM_MIX) // SEG_ALIGN, dtype=jnp.int32)[None, :, None]
    seg_of_q = jnp.sum((q >= cum[:, None, :]).astype(jnp.int32), axis=2)
    in_seg = seg_of_q[:, :, None] == jnp.arange(N_EXPERTS, dtype=jnp.int32)[None, None, :]
    pick = lambda a: jnp.sum(jnp.where(in_seg, a[:, None, :], 0), axis=2)
    dst = pick(gpos) + (q[:, :, 0] - pick(cum - nch)) * SEG_ALIGN
    plan = dict(dst=flat(dst), n_chunks=flat(cum[:, -1]),
                tail_pos=flat(gbase + tot), tail_n=flat((tot_pad - tot) // SEG_ALIGN))
    return plan, lbase_f, (tile_expert, n_tiles, next_expert), n_tiles_max * tm_expert


def _local_positions(route, lbase):
    lane = lax.broadcasted_iota(jnp.int32, route.shape, 1).astype(F32)
    out = []
    for e_lane, r_lane in ((R_E1, R_RANK1), (R_E2, R_RANK2)):
        e = route[:, e_lane:e_lane + 1]
        base = jnp.sum(jnp.where(lane == e, lbase, 0.0), axis=-1, keepdims=True)
        out.append(base + route[:, r_lane:r_lane + 1])
    return out


WAIT_GROUP = 8
ISSUE_GROUP = 4


def _segment_copies(i, dst_ref, nq_ref, local, glob, sem, to_global):
    per_tile = local.shape[0] // SEG_ALIGN
    n = nq_ref[i]

    def start(q):
        lo = local.at[pl.ds(pl.multiple_of(q * SEG_ALIGN, SEG_ALIGN), SEG_ALIGN)]
        gl = glob.at[pl.ds(pl.multiple_of(dst_ref[i * per_tile + q], SEG_ALIGN), SEG_ALIGN)]
        (pltpu.make_async_copy(lo, gl, sem) if to_global else pltpu.make_async_copy(gl, lo, sem)).start()

    def group(k, carry):
        for u in range(ISSUE_GROUP):
            start(k * ISSUE_GROUP + u)
        return carry

    def single(q, carry):
        start(q)
        return carry

    full = n // ISSUE_GROUP
    lax.fori_loop(0, full, group, 0)
    lax.fori_loop(full * ISSUE_GROUP, n, single, 0)
    return n


def _wait_chunks(n, local, glob, sem):
    def wait_rows(rows):
        def one(c, carry):
            pltpu.make_async_copy(local.at[pl.ds(0, rows)], glob.at[pl.ds(0, rows)], sem).wait()
            return carry
        return one

    lax.fori_loop(0, n // WAIT_GROUP, wait_rows(WAIT_GROUP * SEG_ALIGN), 0)
    lax.fori_loop(0, n % WAIT_GROUP, wait_rows(SEG_ALIGN), 0)


def _sort_tile(t_ref, route_ref, lbase, local):
    tm = t_ref.shape[0]
    s_rows = local.shape[0]
    lp1, lp2 = _local_positions(route_ref[...], lbase)
    lane = lax.broadcasted_iota(jnp.int32, (tm, LANES), 1)
    lp_rows = jnp.where(lane == 0, lp1, jnp.where(lane == 1, lp2, -1.0)).T
    row = lax.broadcasted_iota(jnp.int32, (s_rows, tm), 0).astype(F32)
    onehot = jnp.where((row == lp_rows[0:1, :]) | (row == lp_rows[1:2, :]), 1.0, 0.0).astype(BF16)
    yield
    local[...] = _dot(onehot, t_ref[...]).astype(BF16)


def _sort_rows_kernel(dst_ref, nq_ref, tpos_ref, tn_ref, nt_ref, t_ref, route_ref, lbase_ref,
                      xs_ref, local_scr, zero_scr, cnt_scr, sem, zsem):
    i = pl.program_id(0)
    n = pl.num_programs(0)
    slot = i % 2
    per_step = local_scr.shape[1]

    def buf(s, k):
        return local_scr.at[s, k], sem.at[s * per_step + k], s * per_step + k

    def wait_slot(s):
        for k in range(per_step):
            local, sm, c = buf(s, k)
            _wait_chunks(cnt_scr[c], local, xs_ref, sm)

    @pl.when(i >= 2)
    def _():
        wait_slot(slot)

    tiles = []
    for k in range(per_step):
        rows = slice(k * TM_MIX, (k + 1) * TM_MIX)
        tiles.append(_sort_tile(t_ref.at[rows, :], route_ref.at[rows, :], lbase_ref[k], buf(slot, k)[0]))
    _lockstep(tiles)
    for k in range(per_step):
        local, sm, c = buf(slot, k)
        cnt_scr[c] = _segment_copies(i * per_step + k, dst_ref, nq_ref, local, xs_ref, sm, True)

    @pl.when(i == n - 1)
    def _():
        @pl.when(n >= 2)
        def _():
            wait_slot(1 - slot)

        wait_slot(slot)
        zero_scr[...] = jnp.zeros_like(zero_scr)
        te = zero_scr.shape[0]
        zero_chunk = zero_scr.at[pl.ds(0, SEG_ALIGN)]

        def tail(e, total):
            def chunk(c, carry):
                dst = xs_ref.at[pl.ds(pl.multiple_of(tpos_ref[e] + c * SEG_ALIGN, SEG_ALIGN), SEG_ALIGN)]
                pltpu.make_async_copy(zero_chunk, dst, zsem).start()
                return carry

            lax.fori_loop(0, tn_ref[e], chunk, 0)
            return total + tn_ref[e]

        _wait_chunks(lax.fori_loop(0, N_EXPERTS, tail, 0), zero_scr, xs_ref, zsem)

        def unused_tile(j, carry):
            pltpu.make_async_copy(zero_scr, xs_ref.at[pl.ds(pl.multiple_of(j * te, te), te)], zsem).start()
            return carry

        def unused_wait(j, carry):
            pltpu.make_async_copy(zero_scr, xs_ref.at[pl.ds(0, te)], zsem).wait()
            return carry

        lax.fori_loop(nt_ref[0], xs_ref.shape[0] // te, unused_tile, 0)
        lax.fori_loop(nt_ref[0], xs_ref.shape[0] // te, unused_wait, 0)


def _plan_specs(plan):
    keys = ('dst', 'n_chunks', 'tail_pos', 'tail_n')
    return [plan[k] for k in keys]


MOE_TILES = 2


def _sort_rows(plan, n_tiles, t_bf, route, lbase_f, n_sorted):
    t, dm = t_bf.shape
    tm = TM_MIX * MOE_TILES
    s_rows = _local_rows(TM_MIX)
    im = lambda i, *_: (i, 0)
    return pl.pallas_call(
        _sort_rows_kernel,
        grid_spec=pltpu.PrefetchScalarGridSpec(
            num_scalar_prefetch=5,
            grid=(t // tm,),
            in_specs=[pl.BlockSpec((tm, dm), im), pl.BlockSpec((tm, LANES), im),
                      pl.BlockSpec((MOE_TILES, 1, LANES), lambda i, *_: (i, 0, 0))],
            out_specs=pl.BlockSpec(memory_space=pl.ANY),
            scratch_shapes=[pltpu.VMEM((2, MOE_TILES, s_rows, dm), BF16), pltpu.VMEM((TM_EXPERT, dm), BF16),
                            pltpu.SMEM((2 * MOE_TILES,), jnp.int32), pltpu.SemaphoreType.DMA((2 * MOE_TILES,)),
                            pltpu.SemaphoreType.DMA(())],
        ),
        out_shape=jax.ShapeDtypeStruct((n_sorted, dm), BF16),
        compiler_params=pltpu.CompilerParams(dimension_semantics=("arbitrary",),
                                             vmem_limit_bytes=VMEM_LIMIT),
        name="sort_rows",
    )(*_plan_specs(plan), n_tiles, t_bf, route, lbase_f)


def _expert_weight_copies(e, slot, hbm, stage, sem):
    return [pltpu.make_async_copy(h.at[e], s.at[slot], sem.at[slot]) for h, s in zip(hbm, stage)]


def _experts_kernel(te_ref, nt_ref, nxt_ref, xs_ref, wg_ref, wu_ref, wd_ref, ys_ref,
                    sg, su, sd, wg_bf, wu_bf, wd_bf, slot_scr, sem):
    i = pl.program_id(0)
    e = te_ref[i]
    hbm, stage = (wg_ref, wu_ref, wd_ref), (sg, su, sd)

    @pl.when(i == 0)
    def _():
        slot_scr[0] = 0
        for cp in _expert_weight_copies(e, 0, hbm, stage, sem):
            cp.start()

    @pl.when((i == 0) | (e != te_ref[jnp.maximum(i - 1, 0)]))
    def _():
        slot = slot_scr[0]
        for cp in _expert_weight_copies(e, slot, hbm, stage, sem):
            cp.wait()
        wg_bf[...] = sg[slot].astype(BF16)
        wu_bf[...] = su[slot].astype(BF16)
        wd_bf[...] = sd[slot].astype(BF16)
        nxt = nxt_ref[e]

        @pl.when(nxt != e)
        def _():
            for cp in _expert_weight_copies(nxt, 1 - slot, hbm, stage, sem):
                cp.start()

        slot_scr[0] = 1 - slot

    @pl.when(i < nt_ref[0])
    def _():
        x = xs_ref[...]
        hidden = (jax.nn.silu(_dot(x, wg_bf[...])) * _dot(x, wu_bf[...])).astype(BF16)
        ys_ref[...] = _dot(hidden, wd_bf[...]).astype(BF16)

    @pl.when(i >= nt_ref[0])
    def _():
        ys_ref[...] = jnp.zeros_like(ys_ref)


def _experts(tile_expert, n_tiles, next_expert, x_sorted, w_gate, w_up, w_down):
    n_sorted, dm = x_sorted.shape
    de = w_gate.shape[2]
    tm = TM_EXPERT
    return pl.pallas_call(
        _experts_kernel,
        grid_spec=pltpu.PrefetchScalarGridSpec(
            num_scalar_prefetch=3,
            grid=(n_sorted // tm,),
            in_specs=[pl.BlockSpec((tm, dm), lambda i, te, nt, nx: (jnp.minimum(i, nt[0] - 1), 0)),
                      pl.BlockSpec(memory_space=pl.ANY), pl.BlockSpec(memory_space=pl.ANY),
                      pl.BlockSpec(memory_space=pl.ANY)],
            out_specs=pl.BlockSpec((tm, dm), lambda i, te, nt, nx: (i, 0)),
            scratch_shapes=[pltpu.VMEM((2, dm, de), F32), pltpu.VMEM((2, dm, de), F32), pltpu.VMEM((2, de, dm), F32),
                            pltpu.VMEM((dm, de), BF16), pltpu.VMEM((dm, de), BF16), pltpu.VMEM((de, dm), BF16),
                            pltpu.SMEM((1,), jnp.int32), pltpu.SemaphoreType.DMA((2,))],
        ),
        out_shape=jax.ShapeDtypeStruct((n_sorted, dm), BF16),
        compiler_params=pltpu.CompilerParams(dimension_semantics=("arbitrary",),
                                             vmem_limit_bytes=VMEM_LIMIT),
        name="experts",
    )(tile_expert, n_tiles, next_expert, x_sorted, w_gate, w_up, w_down)


def _combine_tile(x2_ref, route_ref, lbase, gfin_ref, local, o_ref):
    tm = x2_ref.shape[0]
    s_rows = local.shape[0]
    route = route_ref[...]
    lp1, lp2 = _local_positions(route, lbase)
    w1, w2 = route[:, R_W1:R_W1 + 1], route[:, R_W2:R_W2 + 1]
    col0 = lax.broadcasted_iota(jnp.int32, (tm, KB), 1).astype(F32)
    moe = None
    for k in range(s_rows // KB):
        col = col0 + float(k * KB)
        pick = (jnp.where(col == lp1, w1, 0.0) + jnp.where(col == lp2, w2, 0.0)).astype(BF16)
        part = _dot(pick, local[k * KB:(k + 1) * KB, :])
        moe = part if moe is None else moe + part
    yield
    o_ref[...] = _rms(x2_ref[...] + moe, gfin_ref[...])


def _combine_kernel(dst_ref, nq_ref, tpos_ref, tn_ref, x2_ref, route_ref, lbase_ref, gfin_ref, ys_ref,
                    o_ref, local_scr, cnt_scr, sem):
    del tpos_ref, tn_ref
    i = pl.program_id(0)
    n = pl.num_programs(0)
    slot = i % 2
    per_step = local_scr.shape[1]
    s_rows = local_scr.shape[2]

    def fetch(step, s):
        for k in range(per_step):
            tile = step * per_step + k

            def clear(r, carry, k=k):
                local_scr[s, k, pl.ds(pl.multiple_of(r * SEG_ALIGN, SEG_ALIGN), SEG_ALIGN), :] = jnp.zeros(
                    (SEG_ALIGN, local_scr.shape[3]), BF16)
                return carry

            lax.fori_loop(nq_ref[tile], s_rows // SEG_ALIGN, clear, 0)
            cnt_scr[s * per_step + k] = _segment_copies(tile, dst_ref, nq_ref, local_scr.at[s, k], ys_ref,
                                                        sem.at[s * per_step + k], False)

    @pl.when(i == 0)
    def _():
        fetch(0, 0)

    @pl.when(i + 1 < n)
    def _():
        fetch(i + 1, 1 - slot)

    tiles = []
    for k in range(per_step):
        c = slot * per_step + k
        _wait_chunks(cnt_scr[c], local_scr.at[slot, k], ys_ref, sem.at[c])
        rows = slice(k * TM_MIX, (k + 1) * TM_MIX)
        tiles.append(_combine_tile(x2_ref.at[rows, :], route_ref.at[rows, :], lbase_ref[k], gfin_ref,
                                   local_scr.at[slot, k], o_ref.at[rows, :]))
    _lockstep(tiles)


def _combine(plan, x2, route, lbase_f, gfin, y_sorted):
    t, dm = x2.shape
    tm = TM_MIX * MOE_TILES
    s_rows = _local_rows(TM_MIX)
    im = lambda i, *_: (i, 0)
    return pl.pallas_call(
        _combine_kernel,
        grid_spec=pltpu.PrefetchScalarGridSpec(
            num_scalar_prefetch=4,
            grid=(t // tm,),
            in_specs=[pl.BlockSpec((tm, dm), im), pl.BlockSpec((tm, LANES), im),
                      pl.BlockSpec((MOE_TILES, 1, LANES), lambda i, *_: (i, 0, 0)),
                      pl.BlockSpec((1, dm), lambda i, *_: (0, 0)),
                      pl.BlockSpec(memory_space=pl.ANY)],
            out_specs=pl.BlockSpec((tm, dm), im),
            scratch_shapes=[pltpu.VMEM((2, MOE_TILES, s_rows, dm), BF16), pltpu.SMEM((2 * MOE_TILES,), jnp.int32),
                            pltpu.SemaphoreType.DMA((2 * MOE_TILES,))],
        ),
        out_shape=jax.ShapeDtypeStruct((t, dm), F32),
        compiler_params=pltpu.CompilerParams(dimension_semantics=("arbitrary",),
                                             vmem_limit_bytes=VMEM_LIMIT),
        name="combine_norm",
    )(*_plan_specs(plan), x2, route, lbase_f, gfin, y_sorted)


def _layer(x, p, s5_ops, gfin):
    b, l, dm = x.shape
    x2d = x.reshape(b * l, dm)
    mgm, us5 = _inproj_gmlp(x2d, p['gmix'], p['win'], p['lng'], p['lnb'], p['ws'], p['bs'], p['gout_gm'])
    n_seg = SUBLANES // b
    *lag_factors, w1, w2, sc = s5_ops[(l // (S5_LC * n_seg))]
    xg = _s5_inproj(x, p['gmix'], p['win_s5'], n_seg)
    yg = _s5_scan(xg, lag_factors, w1, w2, sc, n_seg)
    ys = _s5_to_tokens(yg, b, l, n_seg)
    x2, t_bf, route, counts = _mix_route(ys, us5, mgm, x2d, p['d'], p['gluw'], p['glub'], p['gout_s5'],
                                         p['wout'], p['gffn'], p['rw_hl'], p['rb'])
    plan, lbase_f, tiles, n_sorted = _segment_plan(counts, b * l, TM_EXPERT)
    x_sorted = _sort_rows(plan, tiles[1], t_bf, route, lbase_f, n_sorted)
    y_sorted = _experts(*tiles, x_sorted, p['w_gate'], p['w_up'], p['w_down'])
    out = _combine(plan, x2, route, lbase_f, gfin, y_sorted)
    return out.reshape(b, l, dm)


def kernel(x_prompt, x_sample, norm_mix_g, w_in, gm_ln_g, gm_ln_b, gm_ws, gm_bs, s5_lam_re_fwd, s5_lam_im_fwd, s5_log_step_fwd, s5_b_re_fwd, s5_b_im_fwd, s5_c_re_fwd, s5_c_im_fwd, s5_lam_re_bwd, s5_lam_im_bwd, s5_log_step_bwd, s5_b_re_bwd, s5_b_im_bwd, s5_c_re_bwd, s5_c_im_bwd, s5_d, s5_glu_w, s5_glu_b, out_norm_gm, out_norm_s5, w_out, norm_ffn_g, r1_w, r1_b, r2_w, r2_b, e_w_gate, e_w_up, e_w_down, norm_final_g):
    depth = w_in.shape[0]
    gfin = norm_final_g.reshape(1, -1).astype(F32)
    xs = [x_prompt, x_sample]
    for li in range(depth):
        row = lambda a: a[li].reshape(1, -1).astype(F32)
        dm = w_in.shape[1]
        gw = gm_ln_g.shape[1]
        hd_dim = gw // GM_HEADS
        rw = jnp.concatenate([r1_w[li], r2_w[li].transpose(1, 0, 2).reshape(dm, N_EXPERTS)], axis=1).astype(F32)
        rw = jnp.pad(rw, ((0, 0), (0, LANES - rw.shape[1])))
        rwh = rw.astype(BF16)
        rwl = (rw - rwh.astype(F32)).astype(BF16)
        rb = jnp.concatenate([r1_b[li], r2_b[li].reshape(-1)]).astype(F32)
        rb = jnp.pad(rb, (0, LANES - rb.shape[0])).reshape(1, LANES)
        p = dict(
            gmix=row(norm_mix_g), win=w_in[li].astype(BF16), win_s5=w_in[li][:, 2 * gw:].astype(BF16),
            lng=row(gm_ln_g), lnb=row(gm_ln_b),
            ws=gm_ws[li].astype(BF16),
            bs=jnp.broadcast_to(gm_bs[li].astype(F32)[:, :, None], (GM_HEADS, CHUNK, hd_dim)),
            gout_gm=row(out_norm_gm), d=row(s5_d), gluw=s5_glu_w[li].astype(BF16), glub=row(s5_glu_b),
            gout_s5=row(out_norm_s5), wout=w_out[li].astype(BF16), gffn=row(norm_ffn_g),
            rw_hl=jnp.concatenate([rwh, rwl], axis=1), rb=rb,
            w_gate=e_w_gate[li], w_up=e_w_up[li], w_down=e_w_down[li],
        )
        fwd = (s5_lam_re_fwd[li], s5_lam_im_fwd[li], s5_log_step_fwd[li], s5_b_re_fwd[li], s5_b_im_fwd[li],
               s5_c_re_fwd[li], s5_c_im_fwd[li])
        bwd = (s5_lam_re_bwd[li], s5_lam_im_bwd[li], s5_log_step_bwd[li], s5_b_re_bwd[li], s5_b_im_bwd[li],
               s5_c_re_bwd[li], s5_c_im_bwd[li])
        s5_ops = {}
        for x in xs:
            seg_steps = x.shape[1] // (S5_LC * (SUBLANES // x.shape[0]))
            if seg_steps not in s5_ops:
                s5_ops[seg_steps] = _s5_operator(fwd, bwd, S5_LC, seg_steps)
        last = li == depth - 1
        assert last, "depth > 1 needs an un-normalised layer output"
        xs = [_layer(x, p, s5_ops, gfin) for x in xs]
    return tuple(xs)
```

```python
import functools
import math

import jax
import jax.numpy as jnp
from jax import lax
from jax.experimental import pallas as pl
from jax.experimental.pallas import tpu as pltpu

F32 = jnp.float32
BF16 = jnp.bfloat16

EPS = 1e-6
LAMBDA_RE_MAX = -1e-4
GM_HEADS = 4
CHUNK = 128
S5_GROUP = 16
S5_STATE = 64
N_COARSE = 4
N_FINE = 8
N_EXPERTS = N_COARSE * N_FINE

LANES = 128
SUBLANES = 8
S5_LC = 16
VMEM_LIMIT = 56 * 1024 * 1024

TM_PROJ = 1024
TM_MIX = 512
MIX_TILES = 2
KB = 256
TM_EXPERT = 1024
SEG_ALIGN = 16


def _gelu(x):
    c = math.sqrt(2.0 / math.pi)
    half = 0.5 * x
    return half + half * jnp.tanh(x * (c + (c * 0.044715) * (x * x)))


def _rms(x, g):
    ms = jnp.mean(x * x, axis=-1, keepdims=True)
    return x * lax.rsqrt(ms + EPS) * g


def _dot(a, b):
    return jnp.dot(a, b, preferred_element_type=F32)


def _lockstep(tiles):
    while tiles:
        tiles = [t for t in tiles if next(t, "done") != "done"]


def _inproj_gmlp_kernel(x_ref, gmix_ref, win_ref, lng_ref, lnb_ref, ws_ref, bs_ref, gout_ref,
                        mgm_ref, us5_ref, y_scr):
    tm = x_ref.shape[0]
    gw = mgm_ref.shape[1]
    hd_dim = gw // GM_HEADS
    n_chunks = tm // CHUNK
    h = _rms(x_ref[...], gmix_ref[...]).astype(BF16)
    proj = _dot(h, win_ref[...])
    us5_ref[...] = proj[:, 2 * gw:]
    for hd in range(GM_HEADS):
        lo = hd * hd_dim
        vh = _gelu(proj[:, gw + lo:gw + lo + hd_dim])
        mu = jnp.mean(vh, axis=-1, keepdims=True)
        xc = vh - mu
        var = jnp.mean(xc * xc, axis=-1, keepdims=True)
        vn = (xc * lax.rsqrt(var + EPS) * lng_ref[:, lo:lo + hd_dim]
              + lnb_ref[:, lo:lo + hd_dim]).astype(BF16)
        rhs = jnp.concatenate([vn[c * CHUNK:(c + 1) * CHUNK] for c in range(n_chunks)], axis=1)
        s = _dot(ws_ref[hd], rhs)
        for c in range(n_chunks):
            sc = s[:, c * hd_dim:(c + 1) * hd_dim] + bs_ref[hd]
            u = _gelu(proj[c * CHUNK:(c + 1) * CHUNK, lo:lo + hd_dim])
            y_scr[c * CHUNK:(c + 1) * CHUNK, lo:lo + hd_dim] = u * sc
    mgm_ref[...] = _rms(y_scr[...], gout_ref[...]).astype(BF16)


def _inproj_gmlp(x2d, gmix, win_bf, lng, lnb, ws_bf, bs_b, gout):
    t, d = x2d.shape
    d_in = win_bf.shape[1]
    gw = lng.shape[1]
    s5w = d_in - 2 * gw
    tm = TM_PROJ
    const = lambda *shape: pl.BlockSpec(shape, lambda i: (0,) * len(shape))
    return pl.pallas_call(
        _inproj_gmlp_kernel,
        grid=(t // tm,),
        in_specs=[
            pl.BlockSpec((tm, d), lambda i: (i, 0)),
            const(1, d), const(d, d_in), const(1, gw), const(1, gw),
            const(GM_HEADS, CHUNK, CHUNK), const(GM_HEADS, CHUNK, gw // GM_HEADS), const(1, gw),
        ],
        out_specs=[pl.BlockSpec((tm, gw), lambda i: (i, 0)),
                   pl.BlockSpec((tm, s5w), lambda i: (i, 0))],
        out_shape=[jax.ShapeDtypeStruct((t, gw), BF16),
                   jax.ShapeDtypeStruct((t, s5w), F32)],
        scratch_shapes=[pltpu.VMEM((tm, gw), F32)],
        compiler_params=pltpu.CompilerParams(dimension_semantics=("parallel",),
                                             vmem_limit_bytes=VMEM_LIMIT),
        name="inproj_gmlp",
    )(x2d, gmix, win_bf, lng, lnb, ws_bf, bs_b, gout)


def _s5_consts(lam_re, lam_im, log_step, b_re, b_im, c_re, c_im, lc):
    lr = jnp.minimum(lam_re.astype(F32), LAMBDA_RE_MAX)
    li = lam_im.astype(F32)
    step = jnp.exp(log_step.astype(F32))[:, None]
    dr, di = lr * step, li * step
    ar, ai = _cexp(dr, di)
    nr, ni = ar - 1.0, ai
    den = lr * lr + li * li
    qr, qi = (nr * lr + ni * li) / den, (ni * lr - nr * li) / den
    br, bi = b_re.astype(F32), b_im.astype(F32)
    bbr = qr[..., None] * br - qi[..., None] * bi
    bbi = qr[..., None] * bi + qi[..., None] * br
    k = jnp.arange(lc + 1, dtype=F32)[:, None, None]
    pwr, pwi = _cexp(k * dr[None], k * di[None])
    return (dr, di), (pwr, pwi), (bbr, bbi), (c_re.astype(F32), c_im.astype(F32))


def _cexp(zr, zi):
    m = jnp.exp(zr)
    return m * jnp.cos(zi), m * jnp.sin(zi)


def _s5_operator(fwd, bwd, lc, seg_steps):
    consts = [_s5_consts(*fwd, lc), _s5_consts(*bwd, lc)]
    g, p, h = consts[0][2][0].shape
    lags, w1_parts, w2_parts, sc_rows, seg_rows = [], [], [], [], []
    for direction, (ld, pw, bb, c) in enumerate(consts):
        (dr, di), (pwr, pwi), (bbr, bbi), (cr, ci) = ld, pw, bb, c
        crt, cit = cr.transpose(0, 2, 1), ci.transpose(0, 2, 1)
        pwrt, pwit = pwr.transpose(1, 2, 0), pwi.transpose(1, 2, 0)
        cpr = crt[:, :, None, :] * pwrt[:, :, :, None] - cit[:, :, None, :] * pwit[:, :, :, None]
        cpi = crt[:, :, None, :] * pwit[:, :, :, None] + cit[:, :, None, :] * pwrt[:, :, :, None]
        ck = jnp.concatenate([cpr[:, :, :lc], cpi[:, :, :lc]], axis=1)
        if direction == 1:
            ck = jnp.flip(ck, 2)
        lags += [jnp.concatenate([bbr.transpose(0, 2, 1), -bbi.transpose(0, 2, 1)], axis=-1),
                 ck.reshape(g, 2 * p, lc * h)]
        er, ei = pwrt[:, :, :lc].transpose(0, 2, 1), pwit[:, :, :lc].transpose(0, 2, 1)
        if direction == 0:
            er, ei = jnp.flip(er, 1), jnp.flip(ei, 1)
        bbrt, bbit = bbr.transpose(0, 2, 1), bbi.transpose(0, 2, 1)
        e1 = jnp.concatenate([er, er], -1)[:, :, None, :]
        e2 = jnp.concatenate([-ei, ei], -1)[:, :, None, :]
        b_ri = jnp.concatenate([bbrt, bbit], -1)[:, None]
        b_ir = jnp.concatenate([bbit, bbrt], -1)[:, None]
        w1_parts += [e1 * b_ri + e2 * b_ir, e1 * b_ir - e2 * b_ri]
        fr, fi = cpr[:, :, 1:lc + 1], cpi[:, :, 1:lc + 1]
        if direction == 1:
            fr, fi = jnp.flip(fr, 2), jnp.flip(fi, 2)
        w2_parts += [fr, -fi]

        def mult(zr, zi):
            return [jnp.concatenate([zr, zr], -1), jnp.concatenate([-zi, zi], -1)]

        sc_rows += mult(*_cexp(lc * dr, lc * di))
        seg_rows += mult(*_cexp((lc * seg_steps) * dr, (lc * seg_steps) * di))
    w1 = jnp.concatenate(w1_parts, axis=-1).reshape(g, lc * h, 8 * p)
    w2 = jnp.concatenate(w2_parts, axis=1).reshape(g, 4 * p, lc * h)
    sc = jnp.stack(sc_rows + seg_rows, axis=1)
    return tuple(lags) + (w1.astype(BF16), w2.astype(BF16), sc.astype(F32))


S5_GPS = 2


def _s5_kernel(x_ref, bbf_ref, cpf_ref, bbb_ref, cpb_ref, w1_ref, w2_ref, sc_ref, y_ref,
               loc_scr, sin_scr, m_scr, *, n_seg):
    gps, rows, kw = x_ref.shape
    steps = rows // SUBLANES
    sw = sc_ref.shape[2]

    for gi in range(gps):
        kf = jnp.dot(bbf_ref[gi], cpf_ref[gi], precision=lax.Precision.HIGHEST, preferred_element_type=F32)
        kb = jnp.dot(bbb_ref[gi], cpb_ref[gi], precision=lax.Precision.HIGHEST, preferred_element_type=F32)
        hch = kf.shape[0]
        lc = kw // hch
        lane = lax.broadcasted_iota(jnp.int32, kf.shape, 1)
        for s in range(lc):
            f = kf if s == 0 else jnp.where(lane >= s * hch, pltpu.roll(kf, s * hch, 1), 0.0)
            left = (lc - 1 - s) * hch
            b = kb if left == 0 else pltpu.roll(kb, kw - left, 1)
            m_scr[gi, s * hch:(s + 1) * hch, :] = (f + jnp.where(lane < (s + 1) * hch, b, 0.0)).astype(BF16)
        loc_scr[gi] = _dot(x_ref[gi], w1_ref[gi])

    def bc(gi, i):
        return jnp.broadcast_to(sc_ref[gi, i:i + 1, :], (SUBLANES, sw))

    mult = [[bc(gi, i) for i in range(8)] for gi in range(gps)]

    def step(gi, s, state):
        f, fs, b, bs = state
        a1f, a2f, a1b, a2b = mult[gi][:4]
        rf = pl.multiple_of(s * SUBLANES, SUBLANES)
        rb = pl.multiple_of((steps - 1 - s) * SUBLANES, SUBLANES)
        lf = loc_scr[gi, pl.ds(rf, SUBLANES), 0:sw]
        lfs = loc_scr[gi, pl.ds(rf, SUBLANES), sw:2 * sw]
        lb = loc_scr[gi, pl.ds(rb, SUBLANES), 2 * sw:3 * sw]
        lbs = loc_scr[gi, pl.ds(rb, SUBLANES), 3 * sw:4 * sw]
        return (a1f * f + a2f * fs + lf, a1f * fs - a2f * f + lfs,
                a1b * b + a2b * bs + lb, a1b * bs - a2b * b + lbs)

    zero = jnp.zeros((SUBLANES, sw), F32)

    def pass1(s, carry):
        return tuple(step(gi, s, carry[gi]) for gi in range(gps))

    ends = lax.fori_loop(0, steps, pass1, tuple((zero,) * 4 for _ in range(gps)), unroll=4)

    seg = lax.broadcasted_iota(jnp.int32, (SUBLANES, sw), 0) % n_seg
    enter = []
    for gi in range(gps):
        f_end, fs_end, b_end, bs_end = ends[gi]
        p1f, p2f, p1b, p2b = mult[gi][4:]
        cf, cfs, cb, cbs = zero, zero, zero, zero
        for _ in range(n_seg - 1):
            ef = f_end + p1f * cf + p2f * cfs
            efs = fs_end + p1f * cfs - p2f * cf
            eb = b_end + p1b * cb + p2b * cbs
            ebs = bs_end + p1b * cbs - p2b * cb
            cf = jnp.where(seg >= 1, pltpu.roll(ef, 1, 0), 0.0)
            cfs = jnp.where(seg >= 1, pltpu.roll(efs, 1, 0), 0.0)
            cb = jnp.where(seg <= n_seg - 2, pltpu.roll(eb, SUBLANES - 1, 0), 0.0)
            cbs = jnp.where(seg <= n_seg - 2, pltpu.roll(ebs, SUBLANES - 1, 0), 0.0)
        enter.append((cf, cfs, cb, cbs))

    def pass2(s, carry):
        rf = pl.multiple_of(s * SUBLANES, SUBLANES)
        rb = pl.multiple_of((steps - 1 - s) * SUBLANES, SUBLANES)
        for gi in range(gps):
            sin_scr[gi, pl.ds(rf, SUBLANES), 0:sw] = carry[gi][0]
            sin_scr[gi, pl.ds(rb, SUBLANES), sw:2 * sw] = carry[gi][2]
        return tuple(step(gi, s, carry[gi]) for gi in range(gps))

    lax.fori_loop(0, steps, pass2, tuple(enter), unroll=4)

    for gi in range(gps):
        y_ref[gi] = _dot(x_ref[gi], m_scr[gi]) + _dot(sin_scr[gi].astype(BF16), w2_ref[gi])


def _s5_scan(xg, lag_factors, w1, w2, sc, n_seg):
    g, rows, kw = xg.shape
    sw = sc.shape[2]
    gps = S5_GPS
    blk = lambda a: pl.BlockSpec((gps,) + a.shape[1:], lambda i: (i, 0, 0))
    return pl.pallas_call(
        functools.partial(_s5_kernel, n_seg=n_seg),
        grid=(g // gps,),
        in_specs=[blk(xg)] + [blk(a) for a in lag_factors] + [blk(w1), blk(w2), blk(sc)],
        out_specs=pl.BlockSpec((gps, rows, kw), lambda i: (i, 0, 0)),
        out_shape=jax.ShapeDtypeStruct((g, rows, kw), F32),
        scratch_shapes=[pltpu.VMEM((gps, rows, 4 * sw), F32), pltpu.VMEM((gps, rows, 2 * sw), F32),
                        pltpu.VMEM((gps, kw, kw), BF16)],
        compiler_params=pltpu.CompilerParams(dimension_semantics=("parallel",),
                                             vmem_limit_bytes=VMEM_LIMIT),
        name="s5_scan",
    )(xg, *lag_factors, w1, w2, sc)


S5_NM = 16


def _block_transpose8(groups, width):
    lane = lax.broadcasted_iota(jnp.int32, groups[0][0].shape, 1)
    for d in (4, 2, 1):
        w = width * d
        hi = ((lane // w) % 2) == 1
        nxt = []
        for v in groups:
            out = list(v)
            for i0 in range(8):
                if i0 & d:
                    continue
                i1 = i0 + d
                out[i0] = jnp.where(hi, pltpu.roll(v[i1], w, 1), v[i0])
                out[i1] = jnp.where(hi, v[i1], pltpu.roll(v[i0], 8 * width - w, 1))
            nxt.append(out)
        groups = nxt
    return groups


def _tile_copies(hbm4, tile, buf, slot, sem, nm, to_hbm):
    copies = []
    for c in range(SUBLANES):
        for j in range(S5_LC):
            h = hbm4.at[c, pl.ds(tile * nm, nm), pl.ds(j, 1), :]
            v = buf.at[slot, j, :, pl.ds(c, 1), :]
            copies.append(pltpu.make_async_copy(v, h, sem.at[slot]) if to_hbm
                          else pltpu.make_async_copy(h, v, sem.at[slot]))
    return copies


def _s5_inproj_kernel(x4_ref, gmix_ref, w_ref, xg_ref, xs, sem, *, nm):
    i = pl.program_id(0)
    n = pl.num_programs(0)
    slot = i % 2
    dm = x4_ref.shape[3]

    @pl.when(i == 0)
    def _():
        for cp in _tile_copies(x4_ref, 0, xs, 0, sem, nm, False):
            cp.start()

    @pl.when(i + 1 < n)
    def _():
        for cp in _tile_copies(x4_ref, i + 1, xs, 1 - slot, sem, nm, False):
            cp.start()

    pltpu.make_async_copy(xs.at[slot], xs.at[slot], sem.at[slot]).wait()
    rows = nm * SUBLANES

    def half(a):
        x = xs[slot, 8 * a:8 * a + 8].reshape(8 * rows, dm)
        h = _rms(x, gmix_ref[...]).astype(BF16)
        yield
        z = _dot(h, w_ref[...])
        yield
        n_oct = z.shape[1] // LANES
        blocks = [[z[j8 * rows:(j8 + 1) * rows, q * LANES:(q + 1) * LANES] for j8 in range(8)] for q in range(n_oct)]
        for q, out in enumerate(_block_transpose8(blocks, S5_GROUP)):
            for g8, b in enumerate(out):
                xg_ref[8 * q + g8, :, a * LANES:(a + 1) * LANES] = b.astype(BF16)

    _lockstep([half(a) for a in range(S5_LC // 8)])


def _s5_inproj(x, gmix, w_s5_bf, n_seg):
    b, l, dm = x.shape
    steps = l // (S5_LC * n_seg)
    nm = S5_NM
    s5w = w_s5_bf.shape[1]
    g = s5w // S5_GROUP
    x4 = x.reshape(b * n_seg, steps, S5_LC, dm)
    return pl.pallas_call(
        functools.partial(_s5_inproj_kernel, nm=nm),
        grid=(steps // nm,),
        in_specs=[pl.BlockSpec(memory_space=pl.ANY),
                  pl.BlockSpec((1, dm), lambda i: (0, 0)),
                  pl.BlockSpec((dm, s5w), lambda i: (0, 0))],
        out_specs=pl.BlockSpec((g, nm * SUBLANES, S5_LC * S5_GROUP), lambda i: (0, i, 0)),
        out_shape=jax.ShapeDtypeStruct((g, steps * SUBLANES, S5_LC * S5_GROUP), BF16),
        scratch_shapes=[pltpu.VMEM((2, S5_LC, nm, SUBLANES, dm), F32), pltpu.SemaphoreType.DMA((2,))],
        compiler_params=pltpu.CompilerParams(dimension_semantics=("arbitrary",),
                                             vmem_limit_bytes=VMEM_LIMIT),
        name="s5_inproj",
    )(x4, gmix, w_s5_bf)


def _s5_to_tokens_kernel(yg_ref, ys4_ref, zs, sem, *, nm):
    i = pl.program_id(0)
    n = pl.num_programs(0)
    slot = i % 2
    rows = nm * SUBLANES

    def wait(s):
        pltpu.make_async_copy(zs.at[s], zs.at[s], sem.at[s]).wait()

    @pl.when(i >= 2)
    def _():
        wait(slot)

    n_oct = yg_ref.shape[0] // 8
    combos = [(q, a) for q in range(n_oct) for a in range(S5_LC // 8)]
    blocks = [[yg_ref[8 * q + g8, :, a * LANES:(a + 1) * LANES] for g8 in range(8)] for q, a in combos]
    for (q, a), out in zip(combos, _block_transpose8(blocks, S5_GROUP)):
        for j8, b in enumerate(out):
            zs[slot, 8 * a + j8, :, :, q * LANES:(q + 1) * LANES] = b.reshape(nm, SUBLANES, LANES)
    for cp in _tile_copies(ys4_ref, i, zs, slot, sem, nm, True):
        cp.start()

    @pl.when(i == n - 1)
    def _():
        wait(1 - slot)
        wait(slot)


def _s5_to_tokens(yg, b, l, n_seg):
    g, rows_total, kw = yg.shape
    steps = rows_total // SUBLANES
    nm = S5_NM
    s5w = g * S5_GROUP
    assert steps // nm >= 2
    ys4 = pl.pallas_call(
        functools.partial(_s5_to_tokens_kernel, nm=nm),
        grid=(steps // nm,),
        in_specs=[pl.BlockSpec((g, nm * SUBLANES, kw), lambda i: (0, i, 0))],
        out_specs=pl.BlockSpec(memory_space=pl.ANY),
        out_shape=jax.ShapeDtypeStruct((b * n_seg, steps, S5_LC, s5w), F32),
        scratch_shapes=[pltpu.VMEM((2, S5_LC, nm, SUBLANES, s5w), F32), pltpu.SemaphoreType.DMA((2,))],
        compiler_params=pltpu.CompilerParams(dimension_semantics=("arbitrary",),
                                             vmem_limit_bytes=VMEM_LIMIT),
        name="s5_to_tokens",
    )(yg)
    return ys4.reshape(b * l, s5w)


R_E1, R_E2, R_W1, R_W2, R_RANK1, R_RANK2 = range(6)


def _mix_route_kernel(ys_ref, us5_ref, mgm_ref, x_ref, d_ref, gluw_ref, glub_ref, gs5_ref,
                      wout_ref, gffn_ref, rw_ref, rb_ref, tri_ref,
                      x2_ref, t_ref, route_ref, cnt_ref):
    tiles = []
    for k in range(x_ref.shape[0] // TM_MIX):
        rows = slice(k * TM_MIX, (k + 1) * TM_MIX)
        tiles.append(_mix_route_tile(ys_ref.at[rows, :], us5_ref.at[rows, :], mgm_ref.at[rows, :], x_ref.at[rows, :],
                                     d_ref, gluw_ref, glub_ref, gs5_ref, wout_ref, gffn_ref, rw_ref,
                                     rb_ref, tri_ref, x2_ref.at[rows, :], t_ref.at[rows, :], route_ref.at[rows, :],
                                     cnt_ref.at[k]))
    _lockstep(tiles)


def _mix_route_tile(ys_ref, us5_ref, mgm_ref, x_ref, d_ref, gluw_ref, glub_ref, gs5_ref,
                    wout_ref, gffn_ref, rw_ref, rb_ref, tri_ref,
                    x2_ref, t_ref, route_ref, cnt_ref):
    gw = mgm_ref.shape[1]
    y = ys_ref[...] + d_ref[...] * us5_ref[...]
    g = _gelu(y)
    yield
    gate = _dot(g.astype(BF16), gluw_ref[...])
    yield
    z = g * jax.nn.sigmoid(gate + glub_ref[...])
    ms5 = _rms(z, gs5_ref[...]).astype(BF16)
    yield
    mix = _dot(mgm_ref[...], wout_ref[:gw, :]) + _dot(ms5, wout_ref[gw:, :])
    yield
    x2 = x_ref[...] + mix
    x2_ref[...] = x2
    t = _rms(x2, gffn_ref[...])
    t_hi = t.astype(BF16)
    t_ref[...] = t_hi
    t_lo = (t - t_hi.astype(F32)).astype(BF16)
    yield
    hl = _dot(t_hi, rw_ref[...])
    logits = (hl[:, :LANES] + hl[:, LANES:] + _dot(t_lo, rw_ref[:, :LANES])
              + rb_ref[...])
    yield
    lane = lax.broadcasted_iota(jnp.int32, logits.shape, 1).astype(F32)
    neg = jnp.float32(-jnp.inf)

    def first_max(mask):
        vals = jnp.where(mask, logits, neg)
        mx = jnp.max(vals, axis=-1, keepdims=True)
        idx = jnp.min(jnp.where(mask & (vals == mx), lane, float(LANES)), axis=-1, keepdims=True)
        return mx, idx

    coarse = lane < N_COARSE
    m1, grp = first_max(coarse)
    p_grp = 1.0 / jnp.sum(jnp.where(coarse, jnp.exp(logits - m1), 0.0), axis=-1, keepdims=True)
    lo = N_COARSE + grp * N_FINE
    fine = (lane >= lo) & (lane < lo + N_FINE)
    v1, i1 = first_max(fine)
    v2, i2 = first_max(fine & (lane != i1))
    e21 = jnp.exp(v2 - v1)
    w1 = p_grp / (1.0 + e21)
    w2 = p_grp * e21 / (1.0 + e21)
    e1 = i1 - N_COARSE
    e2 = i2 - N_COARSE
    hit1 = lane == e1
    hit2 = lane == e2
    onehot = jnp.where(hit1 | hit2, 1.0, 0.0)
    before = _dot(tri_ref[...], onehot.astype(BF16))
    rank1 = jnp.sum(jnp.where(hit1, before, 0.0), axis=-1, keepdims=True)
    rank2 = jnp.sum(jnp.where(hit2, before, 0.0), axis=-1, keepdims=True)
    tm = onehot.shape[0]
    cnt_ref[...] = before[tm - 1:tm, :] + onehot[tm - 1:tm, :]
    rec = jnp.zeros_like(logits)
    for slot, val in ((R_E1, e1), (R_E2, e2), (R_W1, w1), (R_W2, w2),
                      (R_RANK1, rank1), (R_RANK2, rank2)):
        rec = jnp.where(lane == slot, val, rec)
    route_ref[...] = rec


def _mix_route(ys, us5, mgm, x2d, d, gluw_bf, glub, gs5, wout_bf, gffn, rw_hl, rb):
    t, dm = x2d.shape
    gw = mgm.shape[1]
    s5w = us5.shape[1]
    tm = TM_MIX * MIX_TILES
    tri = jnp.tril(jnp.ones((TM_MIX, TM_MIX), F32), -1).astype(BF16)
    const = lambda *shape: pl.BlockSpec(shape, lambda i: (0,) * len(shape))
    tile = lambda w: pl.BlockSpec((tm, w), lambda i: (i, 0))
    return pl.pallas_call(
        _mix_route_kernel,
        grid=(t // tm,),
        in_specs=[tile(s5w), tile(s5w), tile(gw), tile(dm),
                  const(1, s5w), const(s5w, s5w), const(1, s5w), const(1, s5w),
                  const(gw + s5w, dm), const(1, dm), const(dm, 2 * LANES), const(1, LANES),
                  const(TM_MIX, TM_MIX)],
        out_specs=[tile(dm), tile(dm), tile(LANES), pl.BlockSpec((MIX_TILES, 1, LANES), lambda i: (i, 0, 0))],
        out_shape=[jax.ShapeDtypeStruct((t, dm), F32),
                   jax.ShapeDtypeStruct((t, dm), BF16),
                   jax.ShapeDtypeStruct((t, LANES), F32),
                   jax.ShapeDtypeStruct((t // TM_MIX, 1, LANES), F32)],
        compiler_params=pltpu.CompilerParams(dimension_semantics=("parallel",),
                                             vmem_limit_bytes=VMEM_LIMIT),
        name="mix_route",
    )(ys, us5, mgm, x2d, d, gluw_bf, glub, gs5, wout_bf, gffn, rw_hl, rb, tri)


def _local_rows(tm):
    worst = 2 * tm + N_EXPERTS * (SEG_ALIGN - 1)
    return -(-worst // LANES) * LANES


def _segment_plan(cnt, t, tm_expert):
    c = cnt[:, 0, :N_EXPERTS].astype(jnp.int32)
    n_tok_tiles = c.shape[0]
    al = (c + SEG_ALIGN - 1) // SEG_ALIGN * SEG_ALIGN
    lbase = jnp.cumsum(al, axis=1) - al
    tot = jnp.sum(al, axis=0)
    tot_pad = (tot + tm_expert - 1) // tm_expert * tm_expert
    gbase = jnp.cumsum(tot_pad) - tot_pad
    gpos = gbase[None, :] + jnp.cumsum(al, axis=0) - al
    n_tiles_max = -(-(2 * t + n_tok_tiles * N_EXPERTS * (SEG_ALIGN - 1)) // tm_expert) + N_EXPERTS
    tile_end = jnp.cumsum(tot_pad // tm_expert)
    n_tiles = tile_end[-1:].astype(jnp.int32)
    tile_idx = jnp.arange(n_tiles_max, dtype=jnp.int32)
    tile_expert = jnp.sum((tile_idx[:, None] >= tile_end[None, :]).astype(jnp.int32), axis=1)
    last = jnp.sum((n_tiles - 1 >= tile_end).astype(jnp.int32))
    tile_expert = jnp.where(tile_idx < n_tiles, tile_expert, last).astype(jnp.int32)
    ids = jnp.arange(N_EXPERTS, dtype=jnp.int32)
    later_used = (ids[None, :] > ids[:, None]) & (tot_pad[None, :] > 0)
    next_expert = jnp.min(jnp.where(later_used, ids[None, :], N_EXPERTS), axis=1)
    next_expert = jnp.where(next_expert == N_EXPERTS, ids, next_expert).astype(jnp.int32)
    region_end = jnp.sum(jnp.where(tile_expert[:, None] == ids[None, :], (gbase + tot)[None, :], 0), axis=1)
    tile_rows = jnp.clip(region_end - tile_idx * tm_expert, 0, tm_expert).astype(jnp.int32)
    lbase_f = jnp.pad(lbase.astype(F32), ((0, 0), (0, LANES - N_EXPERTS)))[:, None, :]
    flat = lambda a: a.reshape(-1).astype(jnp.int32)
    nch = al // SEG_ALIGN
    cum = jnp.cumsum(nch, axis=1)
    q = jnp.arange(_local_rows(TM_MIX) // SEG_ALIGN, dtype=jnp.int32)[None, :, None]
    seg_of_q = jnp.sum((q >= cum[:, None, :]).astype(jnp.int32), axis=2)
    in_seg = seg_of_q[:, :, None] == jnp.arange(N_EXPERTS, dtype=jnp.int32)[None, None, :]
    pick = lambda a: jnp.sum(jnp.where(in_seg, a[:, None, :], 0), axis=2)
    dst = pick(gpos) + (q[:, :, 0] - pick(cum - nch)) * SEG_ALIGN
    plan = dict(dst=flat(dst), n_chunks=flat(cum[:, -1]),
                tail_pos=flat(gbase + tot), tail_n=flat((tot_pad - tot) // SEG_ALIGN))
    return plan, lbase_f, (tile_expert, n_tiles, next_expert, tile_rows), n_tiles_max * tm_expert


def _local_positions(route, lbase):
    lane = lax.broadcasted_iota(jnp.int32, route.shape, 1).astype(F32)
    out = []
    for e_lane, r_lane in ((R_E1, R_RANK1), (R_E2, R_RANK2)):
        e = route[:, e_lane:e_lane + 1]
        base = jnp.sum(jnp.where(lane == e, lbase, 0.0), axis=-1, keepdims=True)
        out.append(base + route[:, r_lane:r_lane + 1])
    return out


WAIT_GROUP = 8
ISSUE_GROUP = 4


def _segment_copies(i, dst_ref, nq_ref, local, glob, sem, to_global):
    per_tile = local.shape[0] // SEG_ALIGN
    n = nq_ref[i]

    def start(q):
        lo = local.at[pl.ds(pl.multiple_of(q * SEG_ALIGN, SEG_ALIGN), SEG_ALIGN)]
        gl = glob.at[pl.ds(pl.multiple_of(dst_ref[i * per_tile + q], SEG_ALIGN), SEG_ALIGN)]
        (pltpu.make_async_copy(lo, gl, sem) if to_global else pltpu.make_async_copy(gl, lo, sem)).start()

    def group(k, carry):
        for u in range(ISSUE_GROUP):
            start(k * ISSUE_GROUP + u)
        return carry

    def single(q, carry):
        start(q)
        return carry

    full = n // ISSUE_GROUP
    lax.fori_loop(0, full, group, 0)
    lax.fori_loop(full * ISSUE_GROUP, n, single, 0)
    return n


def _wait_chunks(n, local, glob, sem):
    def wait_rows(rows):
        def one(c, carry):
            pltpu.make_async_copy(local.at[pl.ds(0, rows)], glob.at[pl.ds(0, rows)], sem).wait()
            return carry
        return one

    lax.fori_loop(0, n // WAIT_GROUP, wait_rows(WAIT_GROUP * SEG_ALIGN), 0)
    lax.fori_loop(0, n % WAIT_GROUP, wait_rows(SEG_ALIGN), 0)


def _sort_tile(t_ref, route_ref, lbase, local):
    tm = t_ref.shape[0]
    s_rows = local.shape[0]
    lp1, lp2 = _local_positions(route_ref[...], lbase)
    lane = lax.broadcasted_iota(jnp.int32, (tm, LANES), 1)
    lp_rows = jnp.where(lane == 0, lp1, jnp.where(lane == 1, lp2, -1.0)).T
    row = lax.broadcasted_iota(jnp.int32, (s_rows, tm), 0).astype(F32)
    onehot = jnp.where((row == lp_rows[0:1, :]) | (row == lp_rows[1:2, :]), 1.0, 0.0).astype(BF16)
    yield
    local[...] = _dot(onehot, t_ref[...]).astype(BF16)


def _sort_rows_kernel(dst_ref, nq_ref, tpos_ref, tn_ref, nt_ref, t_ref, route_ref, lbase_ref,
                      xs_ref, local_scr, zero_scr, cnt_scr, sem, zsem):
    i = pl.program_id(0)
    n = pl.num_programs(0)
    slot = i % 2
    per_step = local_scr.shape[1]

    def buf(s, k):
        return local_scr.at[s, k], sem.at[s * per_step + k], s * per_step + k

    def wait_slot(s):
        for k in range(per_step):
            local, sm, c = buf(s, k)
            _wait_chunks(cnt_scr[c], local, xs_ref, sm)

    @pl.when(i >= 2)
    def _():
        wait_slot(slot)

    tiles = []
    for k in range(per_step):
        rows = slice(k * TM_MIX, (k + 1) * TM_MIX)
        tiles.append(_sort_tile(t_ref.at[rows, :], route_ref.at[rows, :], lbase_ref[k], buf(slot, k)[0]))
    _lockstep(tiles)
    for k in range(per_step):
        local, sm, c = buf(slot, k)
        cnt_scr[c] = _segment_copies(i * per_step + k, dst_ref, nq_ref, local, xs_ref, sm, True)

    @pl.when(i == n - 1)
    def _():
        @pl.when(n >= 2)
        def _():
            wait_slot(1 - slot)

        wait_slot(slot)
        zero_scr[...] = jnp.zeros_like(zero_scr)
        te = zero_scr.shape[0]
        zero_chunk = zero_scr.at[pl.ds(0, SEG_ALIGN)]

        def tail(e, total):
            def chunk(c, carry):
                dst = xs_ref.at[pl.ds(pl.multiple_of(tpos_ref[e] + c * SEG_ALIGN, SEG_ALIGN), SEG_ALIGN)]
                pltpu.make_async_copy(zero_chunk, dst, zsem).start()
                return carry

            lax.fori_loop(0, tn_ref[e], chunk, 0)
            return total + tn_ref[e]

        _wait_chunks(lax.fori_loop(0, N_EXPERTS, tail, 0), zero_scr, xs_ref, zsem)

        def unused_tile(j, carry):
            pltpu.make_async_copy(zero_scr, xs_ref.at[pl.ds(pl.multiple_of(j * te, te), te)], zsem).start()
            return carry

        def unused_wait(j, carry):
            pltpu.make_async_copy(zero_scr, xs_ref.at[pl.ds(0, te)], zsem).wait()
            return carry

        lax.fori_loop(nt_ref[0], xs_ref.shape[0] // te, unused_tile, 0)
        lax.fori_loop(nt_ref[0], xs_ref.shape[0] // te, unused_wait, 0)


def _plan_specs(plan):
    keys = ('dst', 'n_chunks', 'tail_pos', 'tail_n')
    return [plan[k] for k in keys]


MOE_TILES = 2


def _sort_rows(plan, n_tiles, t_bf, route, lbase_f, n_sorted):
    t, dm = t_bf.shape
    tm = TM_MIX * MOE_TILES
    s_rows = _local_rows(TM_MIX)
    im = lambda i, *_: (i, 0)
    return pl.pallas_call(
        _sort_rows_kernel,
        grid_spec=pltpu.PrefetchScalarGridSpec(
            num_scalar_prefetch=5,
            grid=(t // tm,),
            in_specs=[pl.BlockSpec((tm, dm), im), pl.BlockSpec((tm, LANES), im),
                      pl.BlockSpec((MOE_TILES, 1, LANES), lambda i, *_: (i, 0, 0))],
            out_specs=pl.BlockSpec(memory_space=pl.ANY),
            scratch_shapes=[pltpu.VMEM((2, MOE_TILES, s_rows, dm), BF16), pltpu.VMEM((TM_EXPERT, dm), BF16),
                            pltpu.SMEM((2 * MOE_TILES,), jnp.int32), pltpu.SemaphoreType.DMA((2 * MOE_TILES,)),
                            pltpu.SemaphoreType.DMA(())],
        ),
        out_shape=jax.ShapeDtypeStruct((n_sorted, dm), BF16),
        compiler_params=pltpu.CompilerParams(dimension_semantics=("arbitrary",),
                                             vmem_limit_bytes=VMEM_LIMIT),
        name="sort_rows",
    )(*_plan_specs(plan), n_tiles, t_bf, route, lbase_f)


def _expert_weight_copies(e, slot, hbm, stage, sem):
    return [pltpu.make_async_copy(h.at[e], s.at[slot], sem.at[slot]) for h, s in zip(hbm, stage)]


def _experts_kernel(te_ref, nt_ref, nxt_ref, rows_ref, xs_ref, wg_ref, wu_ref, wd_ref, ys_ref,
                    sg, su, sd, wg_bf, wu_bf, wd_bf, slot_scr, sem):
    i = pl.program_id(0)
    e = te_ref[i]
    half = xs_ref.shape[0] // 2
    hbm, stage = (wg_ref, wu_ref, wd_ref), (sg, su, sd)

    @pl.when(i == 0)
    def _():
        slot_scr[0] = 0
        for cp in _expert_weight_copies(e, 0, hbm, stage, sem):
            cp.start()

    @pl.when((i == 0) | (e != te_ref[jnp.maximum(i - 1, 0)]))
    def _():
        slot = slot_scr[0]
        for cp in _expert_weight_copies(e, slot, hbm, stage, sem):
            cp.wait()
        wg_bf[...] = sg[slot].astype(BF16)
        wu_bf[...] = su[slot].astype(BF16)
        wd_bf[...] = sd[slot].astype(BF16)
        nxt = nxt_ref[e]

        @pl.when(nxt != e)
        def _():
            for cp in _expert_weight_copies(nxt, 1 - slot, hbm, stage, sem):
                cp.start()

        slot_scr[0] = 1 - slot

    def mlp(rows):
        x = xs_ref[rows, :]
        hidden = (jax.nn.silu(_dot(x, wg_bf[...])) * _dot(x, wu_bf[...])).astype(BF16)
        ys_ref[rows, :] = _dot(hidden, wd_bf[...]).astype(BF16)

    used = i < nt_ref[0]

    @pl.when(used & (rows_ref[i] > half))
    def _():
        mlp(slice(None))

    @pl.when(used & (rows_ref[i] <= half))
    def _():
        mlp(slice(0, half))
        ys_ref[half:, :] = jnp.zeros((ys_ref.shape[0] - half, ys_ref.shape[1]), ys_ref.dtype)

    @pl.when(i >= nt_ref[0])
    def _():
        ys_ref[...] = jnp.zeros_like(ys_ref)


def _experts(tile_expert, n_tiles, next_expert, tile_rows, x_sorted, w_gate, w_up, w_down):
    n_sorted, dm = x_sorted.shape
    de = w_gate.shape[2]
    tm = TM_EXPERT
    return pl.pallas_call(
        _experts_kernel,
        grid_spec=pltpu.PrefetchScalarGridSpec(
            num_scalar_prefetch=4,
            grid=(n_sorted // tm,),
            in_specs=[pl.BlockSpec((tm, dm), lambda i, te, nt, nx, nr: (jnp.minimum(i, nt[0] - 1), 0)),
                      pl.BlockSpec(memory_space=pl.ANY), pl.BlockSpec(memory_space=pl.ANY),
                      pl.BlockSpec(memory_space=pl.ANY)],
            out_specs=pl.BlockSpec((tm, dm), lambda i, te, nt, nx, nr: (i, 0)),
            scratch_shapes=[pltpu.VMEM((2, dm, de), F32), pltpu.VMEM((2, dm, de), F32), pltpu.VMEM((2, de, dm), F32),
                            pltpu.VMEM((dm, de), BF16), pltpu.VMEM((dm, de), BF16), pltpu.VMEM((de, dm), BF16),
                            pltpu.SMEM((1,), jnp.int32), pltpu.SemaphoreType.DMA((2,))],
        ),
        out_shape=jax.ShapeDtypeStruct((n_sorted, dm), BF16),
        compiler_params=pltpu.CompilerParams(dimension_semantics=("arbitrary",),
                                             vmem_limit_bytes=VMEM_LIMIT),
        name="experts",
    )(tile_expert, n_tiles, next_expert, tile_rows, x_sorted, w_gate, w_up, w_down)


def _combine_tile(x2_ref, route_ref, lbase, gfin_ref, local, o_ref):
    tm = x2_ref.shape[0]
    s_rows = local.shape[0]
    route = route_ref[...]
    lp1, lp2 = _local_positions(route, lbase)
    w1, w2 = route[:, R_W1:R_W1 + 1], route[:, R_W2:R_W2 + 1]
    col0 = lax.broadcasted_iota(jnp.int32, (tm, KB), 1).astype(F32)
    moe = None
    for k in range(s_rows // KB):
        col = col0 + float(k * KB)
        pick = (jnp.where(col == lp1, w1, 0.0) + jnp.where(col == lp2, w2, 0.0)).astype(BF16)
        part = _dot(pick, local[k * KB:(k + 1) * KB, :])
        moe = part if moe is None else moe + part
    yield
    o_ref[...] = _rms(x2_ref[...] + moe, gfin_ref[...])


def _combine_kernel(dst_ref, nq_ref, tpos_ref, tn_ref, x2_ref, route_ref, lbase_ref, gfin_ref, ys_ref,
                    o_ref, local_scr, cnt_scr, sem):
    del tpos_ref, tn_ref
    i = pl.program_id(0)
    n = pl.num_programs(0)
    slot = i % 2
    per_step = local_scr.shape[1]
    s_rows = local_scr.shape[2]

    def fetch(step, s):
        for k in range(per_step):
            tile = step * per_step + k

            def clear(r, carry, k=k):
                local_scr[s, k, pl.ds(pl.multiple_of(r * SEG_ALIGN, SEG_ALIGN), SEG_ALIGN), :] = jnp.zeros(
                    (SEG_ALIGN, local_scr.shape[3]), BF16)
                return carry

            lax.fori_loop(nq_ref[tile], s_rows // SEG_ALIGN, clear, 0)
            cnt_scr[s * per_step + k] = _segment_copies(tile, dst_ref, nq_ref, local_scr.at[s, k], ys_ref,
                                                        sem.at[s * per_step + k], False)

    @pl.when(i == 0)
    def _():
        fetch(0, 0)

    @pl.when(i + 1 < n)
    def _():
        fetch(i + 1, 1 - slot)

    tiles = []
    for k in range(per_step):
        c = slot * per_step + k
        _wait_chunks(cnt_scr[c], local_scr.at[slot, k], ys_ref, sem.at[c])
        rows = slice(k * TM_MIX, (k + 1) * TM_MIX)
        tiles.append(_combine_tile(x2_ref.at[rows, :], route_ref.at[rows, :], lbase_ref[k], gfin_ref,
                                   local_scr.at[slot, k], o_ref.at[rows, :]))
    _lockstep(tiles)


def _combine(plan, x2, route, lbase_f, gfin, y_sorted):
    t, dm = x2.shape
    tm = TM_MIX * MOE_TILES
    s_rows = _local_rows(TM_MIX)
    im = lambda i, *_: (i, 0)
    return pl.pallas_call(
        _combine_kernel,
        grid_spec=pltpu.PrefetchScalarGridSpec(
            num_scalar_prefetch=4,
            grid=(t // tm,),
            in_specs=[pl.BlockSpec((tm, dm), im), pl.BlockSpec((tm, LANES), im),
                      pl.BlockSpec((MOE_TILES, 1, LANES), lambda i, *_: (i, 0, 0)),
                      pl.BlockSpec((1, dm), lambda i, *_: (0, 0)),
                      pl.BlockSpec(memory_space=pl.ANY)],
            out_specs=pl.BlockSpec((tm, dm), im),
            scratch_shapes=[pltpu.VMEM((2, MOE_TILES, s_rows, dm), BF16), pltpu.SMEM((2 * MOE_TILES,), jnp.int32),
                            pltpu.SemaphoreType.DMA((2 * MOE_TILES,))],
        ),
        out_shape=jax.ShapeDtypeStruct((t, dm), F32),
        compiler_params=pltpu.CompilerParams(dimension_semantics=("arbitrary",),
                                             vmem_limit_bytes=VMEM_LIMIT),
        name="combine_norm",
    )(*_plan_specs(plan), x2, route, lbase_f, gfin, y_sorted)


def _layer(x, p, s5_ops, gfin):
    b, l, dm = x.shape
    x2d = x.reshape(b * l, dm)
    mgm, us5 = _inproj_gmlp(x2d, p['gmix'], p['win'], p['lng'], p['lnb'], p['ws'], p['bs'], p['gout_gm'])
    n_seg = SUBLANES // b
    *lag_factors, w1, w2, sc = s5_ops[(l // (S5_LC * n_seg))]
    xg = _s5_inproj(x, p['gmix'], p['win_s5'], n_seg)
    yg = _s5_scan(xg, lag_factors, w1, w2, sc, n_seg)
    ys = _s5_to_tokens(yg, b, l, n_seg)
    x2, t_bf, route, counts = _mix_route(ys, us5, mgm, x2d, p['d'], p['gluw'], p['glub'], p['gout_s5'],
                                         p['wout'], p['gffn'], p['rw_hl'], p['rb'])
    plan, lbase_f, tiles, n_sorted = _segment_plan(counts, b * l, TM_EXPERT)
    x_sorted = _sort_rows(plan, tiles[1], t_bf, route, lbase_f, n_sorted)
    y_sorted = _experts(*tiles, x_sorted, p['w_gate'], p['w_up'], p['w_down'])
    out = _combine(plan, x2, route, lbase_f, gfin, y_sorted)
    return out.reshape(b, l, dm)


def kernel(x_prompt, x_sample, norm_mix_g, w_in, gm_ln_g, gm_ln_b, gm_ws, gm_bs, s5_lam_re_fwd, s5_lam_im_fwd, s5_log_step_fwd, s5_b_re_fwd, s5_b_im_fwd, s5_c_re_fwd, s5_c_im_fwd, s5_lam_re_bwd, s5_lam_im_bwd, s5_log_step_bwd, s5_b_re_bwd, s5_b_im_bwd, s5_c_re_bwd, s5_c_im_bwd, s5_d, s5_glu_w, s5_glu_b, out_norm_gm, out_norm_s5, w_out, norm_ffn_g, r1_w, r1_b, r2_w, r2_b, e_w_gate, e_w_up, e_w_down, norm_final_g):
    depth = w_in.shape[0]
    gfin = norm_final_g.reshape(1, -1).astype(F32)
    xs = [x_prompt, x_sample]
    for li in range(depth):
        row = lambda a: a[li].reshape(1, -1).astype(F32)
        dm = w_in.shape[1]
        gw = gm_ln_g.shape[1]
        hd_dim = gw // GM_HEADS
        rw = jnp.concatenate([r1_w[li], r2_w[li].transpose(1, 0, 2).reshape(dm, N_EXPERTS)], axis=1).astype(F32)
        rw = jnp.pad(rw, ((0, 0), (0, LANES - rw.shape[1])))
        rwh = rw.astype(BF16)
        rwl = (rw - rwh.astype(F32)).astype(BF16)
        rb = jnp.concatenate([r1_b[li], r2_b[li].reshape(-1)]).astype(F32)
        rb = jnp.pad(rb, (0, LANES - rb.shape[0])).reshape(1, LANES)
        p = dict(
            gmix=row(norm_mix_g), win=w_in[li].astype(BF16), win_s5=w_in[li][:, 2 * gw:].astype(BF16),
            lng=row(gm_ln_g), lnb=row(gm_ln_b),
            ws=gm_ws[li].astype(BF16),
            bs=jnp.broadcast_to(gm_bs[li].astype(F32)[:, :, None], (GM_HEADS, CHUNK, hd_dim)),
            gout_gm=row(out_norm_gm), d=row(s5_d), gluw=s5_glu_w[li].astype(BF16), glub=row(s5_glu_b),
            gout_s5=row(out_norm_s5), wout=w_out[li].astype(BF16), gffn=row(norm_ffn_g),
            rw_hl=jnp.concatenate([rwh, rwl], axis=1), rb=rb,
            w_gate=e_w_gate[li], w_up=e_w_up[li], w_down=e_w_down[li],
        )
        fwd = (s5_lam_re_fwd[li], s5_lam_im_fwd[li], s5_log_step_fwd[li], s5_b_re_fwd[li], s5_b_im_fwd[li],
               s5_c_re_fwd[li], s5_c_im_fwd[li])
        bwd = (s5_lam_re_bwd[li], s5_lam_im_bwd[li], s5_log_step_bwd[li], s5_b_re_bwd[li], s5_b_im_bwd[li],
               s5_c_re_bwd[li], s5_c_im_bwd[li])
        s5_ops = {}
        for x in xs:
            seg_steps = x.shape[1] // (S5_LC * (SUBLANES // x.shape[0]))
            if seg_steps not in s5_ops:
                s5_ops[seg_steps] = _s5_operator(fwd, bwd, S5_LC, seg_steps)
        last = li == depth - 1
        assert last, "depth > 1 needs an un-normalised layer output"
        xs = [_layer(x, p, s5_ops, gfin) for x in xs]
    return tuple(xs)
```

```python
import functools
import math

import jax
import jax.numpy as jnp
from jax import lax
from jax.experimental import pallas as pl
from jax.experimental.pallas import tpu as pltpu

F32 = jnp.float32
BF16 = jnp.bfloat16

EPS = 1e-6
LAMBDA_RE_MAX = -1e-4
GM_HEADS = 4
CHUNK = 128
S5_GROUP = 16
S5_STATE = 64
N_COARSE = 4
N_FINE = 8
N_EXPERTS = N_COARSE * N_FINE

LANES = 128
SUBLANES = 8
S5_LC = 16
VMEM_LIMIT = 56 * 1024 * 1024

TM_PROJ = 1024
TM_MIX = 512
MIX_TILES = 2
KB = 256
TM_EXPERT = 1024
SEG_ALIGN = 16


def _gelu(x):
    c = math.sqrt(2.0 / math.pi)
    half = 0.5 * x
    return half + half * jnp.tanh(x * (c + (c * 0.044715) * (x * x)))


def _rms(x, g):
    ms = jnp.mean(x * x, axis=-1, keepdims=True)
    return x * lax.rsqrt(ms + EPS) * g


def _dot(a, b):
    return jnp.dot(a, b, preferred_element_type=F32)


def _lockstep(tiles):
    while tiles:
        tiles = [t for t in tiles if next(t, "done") != "done"]


def _inproj_gmlp_kernel(x_ref, gmix_ref, win_ref, lng_ref, lnb_ref, ws_ref, bs_ref, gout_ref,
                        mgm_ref, us5_ref, y_scr):
    tm = x_ref.shape[0]
    gw = mgm_ref.shape[1]
    hd_dim = gw // GM_HEADS
    n_chunks = tm // CHUNK
    h = _rms(x_ref[...], gmix_ref[...]).astype(BF16)
    proj = _dot(h, win_ref[...])
    us5_ref[...] = proj[:, 2 * gw:]
    for hd in range(GM_HEADS):
        lo = hd * hd_dim
        vh = _gelu(proj[:, gw + lo:gw + lo + hd_dim])
        mu = jnp.mean(vh, axis=-1, keepdims=True)
        xc = vh - mu
        var = jnp.mean(xc * xc, axis=-1, keepdims=True)
        vn = (xc * lax.rsqrt(var + EPS) * lng_ref[:, lo:lo + hd_dim]
              + lnb_ref[:, lo:lo + hd_dim]).astype(BF16)
        rhs = jnp.concatenate([vn[c * CHUNK:(c + 1) * CHUNK] for c in range(n_chunks)], axis=1)
        s = _dot(ws_ref[hd], rhs)
        for c in range(n_chunks):
            sc = s[:, c * hd_dim:(c + 1) * hd_dim] + bs_ref[hd]
            u = _gelu(proj[c * CHUNK:(c + 1) * CHUNK, lo:lo + hd_dim])
            y_scr[c * CHUNK:(c + 1) * CHUNK, lo:lo + hd_dim] = u * sc
    mgm_ref[...] = _rms(y_scr[...], gout_ref[...]).astype(BF16)


def _inproj_gmlp(x2d, gmix, win_bf, lng, lnb, ws_bf, bs_b, gout):
    t, d = x2d.shape
    d_in = win_bf.shape[1]
    gw = lng.shape[1]
    s5w = d_in - 2 * gw
    tm = TM_PROJ
    const = lambda *shape: pl.BlockSpec(shape, lambda i: (0,) * len(shape))
    return pl.pallas_call(
        _inproj_gmlp_kernel,
        grid=(t // tm,),
        in_specs=[
            pl.BlockSpec((tm, d), lambda i: (i, 0)),
            const(1, d), const(d, d_in), const(1, gw), const(1, gw),
            const(GM_HEADS, CHUNK, CHUNK), const(GM_HEADS, CHUNK, gw // GM_HEADS), const(1, gw),
        ],
        out_specs=[pl.BlockSpec((tm, gw), lambda i: (i, 0)),
                   pl.BlockSpec((tm, s5w), lambda i: (i, 0))],
        out_shape=[jax.ShapeDtypeStruct((t, gw), BF16),
                   jax.ShapeDtypeStruct((t, s5w), F32)],
        scratch_shapes=[pltpu.VMEM((tm, gw), F32)],
        compiler_params=pltpu.CompilerParams(dimension_semantics=("parallel",),
                                             vmem_limit_bytes=VMEM_LIMIT),
        name="inproj_gmlp",
    )(x2d, gmix, win_bf, lng, lnb, ws_bf, bs_b, gout)


def _s5_consts(lam_re, lam_im, log_step, b_re, b_im, c_re, c_im, lc):
    lr = jnp.minimum(lam_re.astype(F32), LAMBDA_RE_MAX)
    li = lam_im.astype(F32)
    step = jnp.exp(log_step.astype(F32))[:, None]
    dr, di = lr * step, li * step
    ar, ai = _cexp(dr, di)
    nr, ni = ar - 1.0, ai
    den = lr * lr + li * li
    qr, qi = (nr * lr + ni * li) / den, (ni * lr - nr * li) / den
    br, bi = b_re.astype(F32), b_im.astype(F32)
    bbr = qr[..., None] * br - qi[..., None] * bi
    bbi = qr[..., None] * bi + qi[..., None] * br
    k = jnp.arange(lc + 1, dtype=F32)[:, None, None]
    pwr, pwi = _cexp(k * dr[None], k * di[None])
    return (dr, di), (pwr, pwi), (bbr, bbi), (c_re.astype(F32), c_im.astype(F32))


def _cexp(zr, zi):
    m = jnp.exp(zr)
    return m * jnp.cos(zi), m * jnp.sin(zi)


def _s5_operator(fwd, bwd, lc, seg_steps):
    consts = [_s5_consts(*fwd, lc), _s5_consts(*bwd, lc)]
    g, p, h = consts[0][2][0].shape
    lags, w1_parts, w2_parts, sc_rows, seg_rows = [], [], [], [], []
    for direction, (ld, pw, bb, c) in enumerate(consts):
        (dr, di), (pwr, pwi), (bbr, bbi), (cr, ci) = ld, pw, bb, c
        crt, cit = cr.transpose(0, 2, 1), ci.transpose(0, 2, 1)
        pwrt, pwit = pwr.transpose(1, 2, 0), pwi.transpose(1, 2, 0)
        cpr = crt[:, :, None, :] * pwrt[:, :, :, None] - cit[:, :, None, :] * pwit[:, :, :, None]
        cpi = crt[:, :, None, :] * pwit[:, :, :, None] + cit[:, :, None, :] * pwrt[:, :, :, None]
        ck = jnp.concatenate([cpr[:, :, :lc], cpi[:, :, :lc]], axis=1)
        if direction == 1:
            ck = jnp.flip(ck, 2)
        lags += [jnp.concatenate([bbr.transpose(0, 2, 1), -bbi.transpose(0, 2, 1)], axis=-1),
                 ck.reshape(g, 2 * p, lc * h)]
        er, ei = pwrt[:, :, :lc].transpose(0, 2, 1), pwit[:, :, :lc].transpose(0, 2, 1)
        if direction == 0:
            er, ei = jnp.flip(er, 1), jnp.flip(ei, 1)
        bbrt, bbit = bbr.transpose(0, 2, 1), bbi.transpose(0, 2, 1)
        e1 = jnp.concatenate([er, er], -1)[:, :, None, :]
        e2 = jnp.concatenate([-ei, ei], -1)[:, :, None, :]
        b_ri = jnp.concatenate([bbrt, bbit], -1)[:, None]
        b_ir = jnp.concatenate([bbit, bbrt], -1)[:, None]
        w1_parts += [e1 * b_ri + e2 * b_ir, e1 * b_ir - e2 * b_ri]
        fr, fi = cpr[:, :, 1:lc + 1], cpi[:, :, 1:lc + 1]
        if direction == 1:
            fr, fi = jnp.flip(fr, 2), jnp.flip(fi, 2)
        w2_parts += [fr, -fi]

        def mult(zr, zi):
            return [jnp.concatenate([zr, zr], -1), jnp.concatenate([-zi, zi], -1)]

        sc_rows += mult(*_cexp(lc * dr, lc * di))
        seg_rows += mult(*_cexp((lc * seg_steps) * dr, (lc * seg_steps) * di))
    w1 = jnp.concatenate(w1_parts, axis=-1).reshape(g, lc * h, 8 * p)
    w2 = jnp.concatenate(w2_parts, axis=1).reshape(g, 4 * p, lc * h)
    sc = jnp.stack(sc_rows + seg_rows, axis=1)
    return tuple(lags) + (w1.astype(BF16), w2.astype(BF16), sc.astype(F32))


S5_GPS = 2


def _s5_kernel(x_ref, bbf_ref, cpf_ref, bbb_ref, cpb_ref, w1_ref, w2_ref, sc_ref, y_ref,
               loc_scr, sin_scr, m_scr, *, n_seg):
    gps, rows, kw = x_ref.shape
    steps = rows // SUBLANES
    sw = sc_ref.shape[2]

    for gi in range(gps):
        kf = jnp.dot(bbf_ref[gi], cpf_ref[gi], precision=lax.Precision.HIGHEST, preferred_element_type=F32)
        kb = jnp.dot(bbb_ref[gi], cpb_ref[gi], precision=lax.Precision.HIGHEST, preferred_element_type=F32)
        hch = kf.shape[0]
        lc = kw // hch
        lane = lax.broadcasted_iota(jnp.int32, kf.shape, 1)
        for s in range(lc):
            f = kf if s == 0 else jnp.where(lane >= s * hch, pltpu.roll(kf, s * hch, 1), 0.0)
            left = (lc - 1 - s) * hch
            b = kb if left == 0 else pltpu.roll(kb, kw - left, 1)
            m_scr[gi, s * hch:(s + 1) * hch, :] = (f + jnp.where(lane < (s + 1) * hch, b, 0.0)).astype(BF16)
        loc_scr[gi] = _dot(x_ref[gi], w1_ref[gi])

    def bc(gi, i):
        return jnp.broadcast_to(sc_ref[gi, i:i + 1, :], (SUBLANES, sw))

    mult = [[bc(gi, i) for i in range(8)] for gi in range(gps)]

    def step(gi, s, state):
        f, fs, b, bs = state
        a1f, a2f, a1b, a2b = mult[gi][:4]
        rf = pl.multiple_of(s * SUBLANES, SUBLANES)
        rb = pl.multiple_of((steps - 1 - s) * SUBLANES, SUBLANES)
        lf = loc_scr[gi, pl.ds(rf, SUBLANES), 0:sw]
        lfs = loc_scr[gi, pl.ds(rf, SUBLANES), sw:2 * sw]
        lb = loc_scr[gi, pl.ds(rb, SUBLANES), 2 * sw:3 * sw]
        lbs = loc_scr[gi, pl.ds(rb, SUBLANES), 3 * sw:4 * sw]
        return (a1f * f + a2f * fs + lf, a1f * fs - a2f * f + lfs,
                a1b * b + a2b * bs + lb, a1b * bs - a2b * b + lbs)

    zero = jnp.zeros((SUBLANES, sw), F32)

    def pass1(s, carry):
        return tuple(step(gi, s, carry[gi]) for gi in range(gps))

    ends = lax.fori_loop(0, steps, pass1, tuple((zero,) * 4 for _ in range(gps)), unroll=4)

    seg = lax.broadcasted_iota(jnp.int32, (SUBLANES, sw), 0) % n_seg
    enter = []
    for gi in range(gps):
        f_end, fs_end, b_end, bs_end = ends[gi]
        p1f, p2f, p1b, p2b = mult[gi][4:]
        cf, cfs, cb, cbs = zero, zero, zero, zero
        for _ in range(n_seg - 1):
            ef = f_end + p1f * cf + p2f * cfs
            efs = fs_end + p1f * cfs - p2f * cf
            eb = b_end + p1b * cb + p2b * cbs
            ebs = bs_end + p1b * cbs - p2b * cb
            cf = jnp.where(seg >= 1, pltpu.roll(ef, 1, 0), 0.0)
            cfs = jnp.where(seg >= 1, pltpu.roll(efs, 1, 0), 0.0)
            cb = jnp.where(seg <= n_seg - 2, pltpu.roll(eb, SUBLANES - 1, 0), 0.0)
            cbs = jnp.where(seg <= n_seg - 2, pltpu.roll(ebs, SUBLANES - 1, 0), 0.0)
        enter.append((cf, cfs, cb, cbs))

    def pass2(s, carry):
        rf = pl.multiple_of(s * SUBLANES, SUBLANES)
        rb = pl.multiple_of((steps - 1 - s) * SUBLANES, SUBLANES)
        for gi in range(gps):
            sin_scr[gi, pl.ds(rf, SUBLANES), 0:sw] = carry[gi][0]
            sin_scr[gi, pl.ds(rb, SUBLANES), sw:2 * sw] = carry[gi][2]
        return tuple(step(gi, s, carry[gi]) for gi in range(gps))

    lax.fori_loop(0, steps, pass2, tuple(enter), unroll=4)

    for gi in range(gps):
        y_ref[gi] = _dot(x_ref[gi], m_scr[gi]) + _dot(sin_scr[gi].astype(BF16), w2_ref[gi])


def _s5_scan(xg, lag_factors, w1, w2, sc, n_seg):
    g, rows, kw = xg.shape
    sw = sc.shape[2]
    gps = S5_GPS
    blk = lambda a: pl.BlockSpec((gps,) + a.shape[1:], lambda i: (i, 0, 0))
    return pl.pallas_call(
        functools.partial(_s5_kernel, n_seg=n_seg),
        grid=(g // gps,),
        in_specs=[blk(xg)] + [blk(a) for a in lag_factors] + [blk(w1), blk(w2), blk(sc)],
        out_specs=pl.BlockSpec((gps, rows, kw), lambda i: (i, 0, 0)),
        out_shape=jax.ShapeDtypeStruct((g, rows, kw), F32),
        scratch_shapes=[pltpu.VMEM((gps, rows, 4 * sw), F32), pltpu.VMEM((gps, rows, 2 * sw), F32),
                        pltpu.VMEM((gps, kw, kw), BF16)],
        compiler_params=pltpu.CompilerParams(dimension_semantics=("parallel",),
                                             vmem_limit_bytes=VMEM_LIMIT),
        name="s5_scan",
    )(xg, *lag_factors, w1, w2, sc)


S5_NM = 16


def _block_transpose8(groups, width):
    lane = lax.broadcasted_iota(jnp.int32, groups[0][0].shape, 1)
    for d in (4, 2, 1):
        w = width * d
        hi = ((lane // w) % 2) == 1
        nxt = []
        for v in groups:
            out = list(v)
            for i0 in range(8):
                if i0 & d:
                    continue
                i1 = i0 + d
                out[i0] = jnp.where(hi, pltpu.roll(v[i1], w, 1), v[i0])
                out[i1] = jnp.where(hi, v[i1], pltpu.roll(v[i0], 8 * width - w, 1))
            nxt.append(out)
        groups = nxt
    return groups


def _tile_copies(hbm4, tile, buf, slot, sem, nm, to_hbm):
    copies = []
    for c in range(SUBLANES):
        for j in range(S5_LC):
            h = hbm4.at[c, pl.ds(tile * nm, nm), pl.ds(j, 1), :]
            v = buf.at[slot, j, :, pl.ds(c, 1), :]
            copies.append(pltpu.make_async_copy(v, h, sem.at[slot]) if to_hbm
                          else pltpu.make_async_copy(h, v, sem.at[slot]))
    return copies


def _s5_inproj_kernel(x4_ref, gmix_ref, w_ref, xg_ref, xs, sem, *, nm):
    i = pl.program_id(0)
    n = pl.num_programs(0)
    slot = i % 2
    dm = x4_ref.shape[3]

    @pl.when(i == 0)
    def _():
        for cp in _tile_copies(x4_ref, 0, xs, 0, sem, nm, False):
            cp.start()

    @pl.when(i + 1 < n)
    def _():
        for cp in _tile_copies(x4_ref, i + 1, xs, 1 - slot, sem, nm, False):
            cp.start()

    pltpu.make_async_copy(xs.at[slot], xs.at[slot], sem.at[slot]).wait()
    rows = nm * SUBLANES

    def half(a):
        x = xs[slot, 8 * a:8 * a + 8].reshape(8 * rows, dm)
        h = _rms(x, gmix_ref[...]).astype(BF16)
        yield
        z = _dot(h, w_ref[...])
        yield
        n_oct = z.shape[1] // LANES
        blocks = [[z[j8 * rows:(j8 + 1) * rows, q * LANES:(q + 1) * LANES] for j8 in range(8)] for q in range(n_oct)]
        for q, out in enumerate(_block_transpose8(blocks, S5_GROUP)):
            for g8, b in enumerate(out):
                xg_ref[8 * q + g8, :, a * LANES:(a + 1) * LANES] = b.astype(BF16)

    _lockstep([half(a) for a in range(S5_LC // 8)])


def _s5_inproj(x, gmix, w_s5_bf, n_seg):
    b, l, dm = x.shape
    steps = l // (S5_LC * n_seg)
    nm = S5_NM
    s5w = w_s5_bf.shape[1]
    g = s5w // S5_GROUP
    x4 = x.reshape(b * n_seg, steps, S5_LC, dm)
    return pl.pallas_call(
        functools.partial(_s5_inproj_kernel, nm=nm),
        grid=(steps // nm,),
        in_specs=[pl.BlockSpec(memory_space=pl.ANY),
                  pl.BlockSpec((1, dm), lambda i: (0, 0)),
                  pl.BlockSpec((dm, s5w), lambda i: (0, 0))],
        out_specs=pl.BlockSpec((g, nm * SUBLANES, S5_LC * S5_GROUP), lambda i: (0, i, 0)),
        out_shape=jax.ShapeDtypeStruct((g, steps * SUBLANES, S5_LC * S5_GROUP), BF16),
        scratch_shapes=[pltpu.VMEM((2, S5_LC, nm, SUBLANES, dm), F32), pltpu.SemaphoreType.DMA((2,))],
        compiler_params=pltpu.CompilerParams(dimension_semantics=("arbitrary",),
                                             vmem_limit_bytes=VMEM_LIMIT),
        name="s5_inproj",
    )(x4, gmix, w_s5_bf)


def _s5_to_tokens_kernel(yg_ref, ys4_ref, zs, sem, *, nm):
    i = pl.program_id(0)
    n = pl.num_programs(0)
    slot = i % 2
    rows = nm * SUBLANES

    def wait(s):
        pltpu.make_async_copy(zs.at[s], zs.at[s], sem.at[s]).wait()

    @pl.when(i >= 2)
    def _():
        wait(slot)

    n_oct = yg_ref.shape[0] // 8
    combos = [(q, a) for q in range(n_oct) for a in range(S5_LC // 8)]
    blocks = [[yg_ref[8 * q + g8, :, a * LANES:(a + 1) * LANES] for g8 in range(8)] for q, a in combos]
    for (q, a), out in zip(combos, _block_transpose8(blocks, S5_GROUP)):
        for j8, b in enumerate(out):
            zs[slot, 8 * a + j8, :, :, q * LANES:(q + 1) * LANES] = b.reshape(nm, SUBLANES, LANES)
    for cp in _tile_copies(ys4_ref, i, zs, slot, sem, nm, True):
        cp.start()

    @pl.when(i == n - 1)
    def _():
        wait(1 - slot)
        wait(slot)


def _s5_to_tokens(yg, b, l, n_seg):
    g, rows_total, kw = yg.shape
    steps = rows_total // SUBLANES
    nm = S5_NM
    s5w = g * S5_GROUP
    assert steps // nm >= 2
    ys4 = pl.pallas_call(
        functools.partial(_s5_to_tokens_kernel, nm=nm),
        grid=(steps // nm,),
        in_specs=[pl.BlockSpec((g, nm * SUBLANES, kw), lambda i: (0, i, 0))],
        out_specs=pl.BlockSpec(memory_space=pl.ANY),
        out_shape=jax.ShapeDtypeStruct((b * n_seg, steps, S5_LC, s5w), F32),
        scratch_shapes=[pltpu.VMEM((2, S5_LC, nm, SUBLANES, s5w), F32), pltpu.SemaphoreType.DMA((2,))],
        compiler_params=pltpu.CompilerParams(dimension_semantics=("arbitrary",),
                                             vmem_limit_bytes=VMEM_LIMIT),
        name="s5_to_tokens",
    )(yg)
    return ys4.reshape(b * l, s5w)


R_E1, R_E2, R_W1, R_W2, R_RANK1, R_RANK2 = range(6)


def _mix_route_kernel(ys_ref, us5_ref, mgm_ref, x_ref, d_ref, gluw_ref, glub_ref, gs5_ref,
                      wout_ref, gffn_ref, rw_ref, rb_ref, tri_ref,
                      x2_ref, t_ref, route_ref, cnt_ref):
    tiles = []
    for k in range(x_ref.shape[0] // TM_MIX):
        rows = slice(k * TM_MIX, (k + 1) * TM_MIX)
        tiles.append(_mix_route_tile(ys_ref.at[rows, :], us5_ref.at[rows, :], mgm_ref.at[rows, :], x_ref.at[rows, :],
                                     d_ref, gluw_ref, glub_ref, gs5_ref, wout_ref, gffn_ref, rw_ref,
                                     rb_ref, tri_ref, x2_ref.at[rows, :], t_ref.at[rows, :], route_ref.at[rows, :],
                                     cnt_ref.at[k]))
    _lockstep(tiles)


def _mix_route_tile(ys_ref, us5_ref, mgm_ref, x_ref, d_ref, gluw_ref, glub_ref, gs5_ref,
                    wout_ref, gffn_ref, rw_ref, rb_ref, tri_ref,
                    x2_ref, t_ref, route_ref, cnt_ref):
    gw = mgm_ref.shape[1]
    y = ys_ref[...] + d_ref[...] * us5_ref[...]
    g = _gelu(y)
    yield
    gate = _dot(g.astype(BF16), gluw_ref[...])
    yield
    z = g * jax.nn.sigmoid(gate + glub_ref[...])
    ms5 = _rms(z, gs5_ref[...]).astype(BF16)
    yield
    mix = _dot(mgm_ref[...], wout_ref[:gw, :]) + _dot(ms5, wout_ref[gw:, :])
    yield
    x2 = x_ref[...] + mix
    x2_ref[...] = x2
    t = _rms(x2, gffn_ref[...])
    t_hi = t.astype(BF16)
    t_ref[...] = t_hi
    t_lo = (t - t_hi.astype(F32)).astype(BF16)
    yield
    hl = _dot(t_hi, rw_ref[...])
    logits = (hl[:, :LANES] + hl[:, LANES:] + _dot(t_lo, rw_ref[:, :LANES])
              + rb_ref[...])
    yield
    lane = lax.broadcasted_iota(jnp.int32, logits.shape, 1).astype(F32)
    neg = jnp.float32(-jnp.inf)

    def first_max(mask):
        vals = jnp.where(mask, logits, neg)
        mx = jnp.max(vals, axis=-1, keepdims=True)
        idx = jnp.min(jnp.where(mask & (vals == mx), lane, float(LANES)), axis=-1, keepdims=True)
        return mx, idx

    coarse = lane < N_COARSE
    m1, grp = first_max(coarse)
    p_grp = 1.0 / jnp.sum(jnp.where(coarse, jnp.exp(logits - m1), 0.0), axis=-1, keepdims=True)
    lo = N_COARSE + grp * N_FINE
    fine = (lane >= lo) & (lane < lo + N_FINE)
    v1, i1 = first_max(fine)
    v2, i2 = first_max(fine & (lane != i1))
    e21 = jnp.exp(v2 - v1)
    w1 = p_grp / (1.0 + e21)
    w2 = p_grp * e21 / (1.0 + e21)
    e1 = i1 - N_COARSE
    e2 = i2 - N_COARSE
    hit1 = lane == e1
    hit2 = lane == e2
    onehot = jnp.where(hit1 | hit2, 1.0, 0.0)
    before = _dot(tri_ref[...], onehot.astype(BF16))
    rank1 = jnp.sum(jnp.where(hit1, before, 0.0), axis=-1, keepdims=True)
    rank2 = jnp.sum(jnp.where(hit2, before, 0.0), axis=-1, keepdims=True)
    tm = onehot.shape[0]
    cnt_ref[...] = before[tm - 1:tm, :] + onehot[tm - 1:tm, :]
    rec = jnp.zeros_like(logits)
    for slot, val in ((R_E1, e1), (R_E2, e2), (R_W1, w1), (R_W2, w2),
                      (R_RANK1, rank1), (R_RANK2, rank2)):
        rec = jnp.where(lane == slot, val, rec)
    route_ref[...] = rec


def _mix_route(ys, us5, mgm, x2d, d, gluw_bf, glub, gs5, wout_bf, gffn, rw_hl, rb):
    t, dm = x2d.shape
    gw = mgm.shape[1]
    s5w = us5.shape[1]
    tm = TM_MIX * MIX_TILES
    tri = jnp.tril(jnp.ones((TM_MIX, TM_MIX), F32), -1).astype(BF16)
    const = lambda *shape: pl.BlockSpec(shape, lambda i: (0,) * len(shape))
    tile = lambda w: pl.BlockSpec((tm, w), lambda i: (i, 0))
    return pl.pallas_call(
        _mix_route_kernel,
        grid=(t // tm,),
        in_specs=[tile(s5w), tile(s5w), tile(gw), tile(dm),
                  const(1, s5w), const(s5w, s5w), const(1, s5w), const(1, s5w),
                  const(gw + s5w, dm), const(1, dm), const(dm, 2 * LANES), const(1, LANES),
                  const(TM_MIX, TM_MIX)],
        out_specs=[tile(dm), tile(dm), tile(LANES), pl.BlockSpec((MIX_TILES, 1, LANES), lambda i: (i, 0, 0))],
        out_shape=[jax.ShapeDtypeStruct((t, dm), F32),
                   jax.ShapeDtypeStruct((t, dm), BF16),
                   jax.ShapeDtypeStruct((t, LANES), F32),
                   jax.ShapeDtypeStruct((t // TM_MIX, 1, LANES), F32)],
        compiler_params=pltpu.CompilerParams(dimension_semantics=("parallel",),
                                             vmem_limit_bytes=VMEM_LIMIT),
        name="mix_route",
    )(ys, us5, mgm, x2d, d, gluw_bf, glub, gs5, wout_bf, gffn, rw_hl, rb, tri)


def _local_rows(tm):
    worst = 2 * tm + N_EXPERTS * (SEG_ALIGN - 1)
    return -(-worst // LANES) * LANES


def _segment_plan(cnt, t, tm_expert):
    c = cnt[:, 0, :N_EXPERTS].astype(jnp.int32)
    n_tok_tiles = c.shape[0]
    al = (c + SEG_ALIGN - 1) // SEG_ALIGN * SEG_ALIGN
    lbase = jnp.cumsum(al, axis=1) - al
    tot = jnp.sum(al, axis=0)
    tot_pad = (tot + tm_expert - 1) // tm_expert * tm_expert
    gbase = jnp.cumsum(tot_pad) - tot_pad
    gpos = gbase[None, :] + jnp.cumsum(al, axis=0) - al
    n_tiles_max = -(-(2 * t + n_tok_tiles * N_EXPERTS * (SEG_ALIGN - 1)) // tm_expert) + N_EXPERTS
    tile_end = jnp.cumsum(tot_pad // tm_expert)
    n_tiles = tile_end[-1:].astype(jnp.int32)
    tile_idx = jnp.arange(n_tiles_max, dtype=jnp.int32)
    tile_expert = jnp.sum((tile_idx[:, None] >= tile_end[None, :]).astype(jnp.int32), axis=1)
    last = jnp.sum((n_tiles - 1 >= tile_end).astype(jnp.int32))
    tile_expert = jnp.where(tile_idx < n_tiles, tile_expert, last).astype(jnp.int32)
    ids = jnp.arange(N_EXPERTS, dtype=jnp.int32)
    later_used = (ids[None, :] > ids[:, None]) & (tot_pad[None, :] > 0)
    next_expert = jnp.min(jnp.where(later_used, ids[None, :], N_EXPERTS), axis=1)
    next_expert = jnp.where(next_expert == N_EXPERTS, ids, next_expert).astype(jnp.int32)
    lbase_f = jnp.pad(lbase.astype(F32), ((0, 0), (0, LANES - N_EXPERTS)))[:, None, :]
    flat = lambda a: a.reshape(-1).astype(jnp.int32)
    nch = al // SEG_ALIGN
    cum = jnp.cumsum(nch, axis=1)
    q = jnp.arange(_local_rows(TM_MIX) // SEG_ALIGN, dtype=jnp.int32)[None, :, None]
    seg_of_q = jnp.sum((q >= cum[:, None, :]).astype(jnp.int32), axis=2)
    in_seg = seg_of_q[:, :, None] == jnp.arange(N_EXPERTS, dtype=jnp.int32)[None, None, :]
    pick = lambda a: jnp.sum(jnp.where(in_seg, a[:, None, :], 0), axis=2)
    dst = pick(gpos) + (q[:, :, 0] - pick(cum - nch)) * SEG_ALIGN
    plan = dict(dst=flat(dst), n_chunks=flat(cum[:, -1]),
                tail_pos=flat(gbase + tot), tail_n=flat((tot_pad - tot) // SEG_ALIGN))
    return plan, lbase_f, (tile_expert, n_tiles, next_expert), n_tiles_max * tm_expert


def _local_positions(route, lbase):
    lane = lax.broadcasted_iota(jnp.int32, route.shape, 1).astype(F32)
    out = []
    for e_lane, r_lane in ((R_E1, R_RANK1), (R_E2, R_RANK2)):
        e = route[:, e_lane:e_lane + 1]
        base = jnp.sum(jnp.where(lane == e, lbase, 0.0), axis=-1, keepdims=True)
        out.append(base + route[:, r_lane:r_lane + 1])
    return out


WAIT_GROUP = 8
ISSUE_GROUP = 4


def _segment_copies(i, dst_ref, nq_ref, local, glob, sem, to_global):
    per_tile = local.shape[0] // SEG_ALIGN
    n = nq_ref[i]

    def start(q):
        lo = local.at[pl.ds(pl.multiple_of(q * SEG_ALIGN, SEG_ALIGN), SEG_ALIGN)]
        gl = glob.at[pl.ds(pl.multiple_of(dst_ref[i * per_tile + q], SEG_ALIGN), SEG_ALIGN)]
        (pltpu.make_async_copy(lo, gl, sem) if to_global else pltpu.make_async_copy(gl, lo, sem)).start()

    def group(k, carry):
        for u in range(ISSUE_GROUP):
            start(k * ISSUE_GROUP + u)
        return carry

    def single(q, carry):
        start(q)
        return carry

    full = n // ISSUE_GROUP
    lax.fori_loop(0, full, group, 0)
    lax.fori_loop(full * ISSUE_GROUP, n, single, 0)
    return n


def _wait_chunks(n, local, glob, sem):
    def wait_rows(rows):
        def one(c, carry):
            pltpu.make_async_copy(local.at[pl.ds(0, rows)], glob.at[pl.ds(0, rows)], sem).wait()
            return carry
        return one

    lax.fori_loop(0, n // WAIT_GROUP, wait_rows(WAIT_GROUP * SEG_ALIGN), 0)
    lax.fori_loop(0, n % WAIT_GROUP, wait_rows(SEG_ALIGN), 0)


def _sort_tile(t_ref, route_ref, lbase, local):
    tm = t_ref.shape[0]
    s_rows = local.shape[0]
    lp1, lp2 = _local_positions(route_ref[...], lbase)
    lane = lax.broadcasted_iota(jnp.int32, (tm, LANES), 1)
    lp_rows = jnp.where(lane == 0, lp1, jnp.where(lane == 1, lp2, -1.0)).T
    row = lax.broadcasted_iota(jnp.int32, (s_rows, tm), 0).astype(F32)
    onehot = jnp.where((row == lp_rows[0:1, :]) | (row == lp_rows[1:2, :]), 1.0, 0.0).astype(BF16)
    yield
    local[...] = _dot(onehot, t_ref[...]).astype(BF16)


def _sort_rows_kernel(dst_ref, nq_ref, tpos_ref, tn_ref, nt_ref, t_ref, route_ref, lbase_ref,
                      xs_ref, local_scr, zero_scr, cnt_scr, sem, zsem):
    i = pl.program_id(0)
    n = pl.num_programs(0)
    slot = i % 2
    per_step = local_scr.shape[1]

    def buf(s, k):
        return local_scr.at[s, k], sem.at[s * per_step + k], s * per_step + k

    def wait_slot(s):
        for k in range(per_step):
            local, sm, c = buf(s, k)
            _wait_chunks(cnt_scr[c], local, xs_ref, sm)

    @pl.when(i >= 2)
    def _():
        wait_slot(slot)

    tiles = []
    for k in range(per_step):
        rows = slice(k * TM_MIX, (k + 1) * TM_MIX)
        tiles.append(_sort_tile(t_ref.at[rows, :], route_ref.at[rows, :], lbase_ref[k], buf(slot, k)[0]))
    _lockstep(tiles)
    for k in range(per_step):
        local, sm, c = buf(slot, k)
        cnt_scr[c] = _segment_copies(i * per_step + k, dst_ref, nq_ref, local, xs_ref, sm, True)

    @pl.when(i == n - 1)
    def _():
        @pl.when(n >= 2)
        def _():
            wait_slot(1 - slot)

        wait_slot(slot)
        zero_scr[...] = jnp.zeros_like(zero_scr)
        te = zero_scr.shape[0]
        zero_chunk = zero_scr.at[pl.ds(0, SEG_ALIGN)]

        def tail(e, total):
            def chunk(c, carry):
                dst = xs_ref.at[pl.ds(pl.multiple_of(tpos_ref[e] + c * SEG_ALIGN, SEG_ALIGN), SEG_ALIGN)]
                pltpu.make_async_copy(zero_chunk, dst, zsem).start()
                return carry

            lax.fori_loop(0, tn_ref[e], chunk, 0)
            return total + tn_ref[e]

        _wait_chunks(lax.fori_loop(0, N_EXPERTS, tail, 0), zero_scr, xs_ref, zsem)

        def unused_tile(j, carry):
            pltpu.make_async_copy(zero_scr, xs_ref.at[pl.ds(pl.multiple_of(j * te, te), te)], zsem).start()
            return carry

        def unused_wait(j, carry):
            pltpu.make_async_copy(zero_scr, xs_ref.at[pl.ds(0, te)], zsem).wait()
            return carry

        lax.fori_loop(nt_ref[0], xs_ref.shape[0] // te, unused_tile, 0)
        lax.fori_loop(nt_ref[0], xs_ref.shape[0] // te, unused_wait, 0)


def _plan_specs(plan):
    keys = ('dst', 'n_chunks', 'tail_pos', 'tail_n')
    return [plan[k] for k in keys]


MOE_TILES = 2


def _sort_rows(plan, n_tiles, t_bf, route, lbase_f, n_sorted):
    t, dm = t_bf.shape
    tm = TM_MIX * MOE_TILES
    s_rows = _local_rows(TM_MIX)
    im = lambda i, *_: (i, 0)
    return pl.pallas_call(
        _sort_rows_kernel,
        grid_spec=pltpu.PrefetchScalarGridSpec(
            num_scalar_prefetch=5,
            grid=(t // tm,),
            in_specs=[pl.BlockSpec((tm, dm), im), pl.BlockSpec((tm, LANES), im),
                      pl.BlockSpec((MOE_TILES, 1, LANES), lambda i, *_: (i, 0, 0))],
            out_specs=pl.BlockSpec(memory_space=pl.ANY),
            scratch_shapes=[pltpu.VMEM((2, MOE_TILES, s_rows, dm), BF16), pltpu.VMEM((TM_EXPERT, dm), BF16),
                            pltpu.SMEM((2 * MOE_TILES,), jnp.int32), pltpu.SemaphoreType.DMA((2 * MOE_TILES,)),
                            pltpu.SemaphoreType.DMA(())],
        ),
        out_shape=jax.ShapeDtypeStruct((n_sorted, dm), BF16),
        compiler_params=pltpu.CompilerParams(dimension_semantics=("arbitrary",),
                                             vmem_limit_bytes=VMEM_LIMIT),
        name="sort_rows",
    )(*_plan_specs(plan), n_tiles, t_bf, route, lbase_f)


def _expert_weight_copies(e, slot, hbm, stage, sem):
    return [pltpu.make_async_copy(h.at[e], s.at[slot], sem.at[slot]) for h, s in zip(hbm, stage)]


def _experts_kernel(te_ref, nt_ref, nxt_ref, xs_ref, wg_ref, wu_ref, wd_ref, ys_ref,
                    sg, su, sd, wg_bf, wu_bf, wd_bf, slot_scr, sem):
    i = pl.program_id(0)
    e = te_ref[i]
    hbm, stage = (wg_ref, wu_ref, wd_ref), (sg, su, sd)

    @pl.when(i == 0)
    def _():
        slot_scr[0] = 0
        for cp in _expert_weight_copies(e, 0, hbm, stage, sem):
            cp.start()

    @pl.when((i == 0) | (e != te_ref[jnp.maximum(i - 1, 0)]))
    def _():
        slot = slot_scr[0]
        for cp in _expert_weight_copies(e, slot, hbm, stage, sem):
            cp.wait()
        wg_bf[...] = sg[slot].astype(BF16)
        wu_bf[...] = su[slot].astype(BF16)
        wd_bf[...] = sd[slot].astype(BF16)
        nxt = nxt_ref[e]

        @pl.when(nxt != e)
        def _():
            for cp in _expert_weight_copies(nxt, 1 - slot, hbm, stage, sem):
                cp.start()

        slot_scr[0] = 1 - slot

    @pl.when(i < nt_ref[0])
    def _():
        x = xs_ref[...]
        hidden = (jax.nn.silu(_dot(x, wg_bf[...])) * _dot(x, wu_bf[...])).astype(BF16)
        ys_ref[...] = _dot(hidden, wd_bf[...]).astype(BF16)


def _experts(tile_expert, n_tiles, next_expert, x_sorted, w_gate, w_up, w_down):
    n_sorted, dm = x_sorted.shape
    de = w_gate.shape[2]
    tm = TM_EXPERT
    return pl.pallas_call(
        _experts_kernel,
        grid_spec=pltpu.PrefetchScalarGridSpec(
            num_scalar_prefetch=3,
            grid=(n_sorted // tm,),
            in_specs=[pl.BlockSpec((tm, dm), lambda i, te, nt, nx: (jnp.minimum(i, nt[0] - 1), 0)),
                      pl.BlockSpec(memory_space=pl.ANY), pl.BlockSpec(memory_space=pl.ANY),
                      pl.BlockSpec(memory_space=pl.ANY)],
            out_specs=pl.BlockSpec((tm, dm), lambda i, te, nt, nx: (jnp.minimum(i, nt[0] - 1), 0)),
            scratch_shapes=[pltpu.VMEM((2, dm, de), F32), pltpu.VMEM((2, dm, de), F32), pltpu.VMEM((2, de, dm), F32),
                            pltpu.VMEM((dm, de), BF16), pltpu.VMEM((dm, de), BF16), pltpu.VMEM((de, dm), BF16),
                            pltpu.SMEM((1,), jnp.int32), pltpu.SemaphoreType.DMA((2,))],
        ),
        out_shape=jax.ShapeDtypeStruct((n_sorted, dm), BF16),
        compiler_params=pltpu.CompilerParams(dimension_semantics=("arbitrary",),
                                             vmem_limit_bytes=VMEM_LIMIT),
        input_output_aliases={3: 0},
        name="experts",
    )(tile_expert, n_tiles, next_expert, x_sorted, w_gate, w_up, w_down)


def _combine_tile(x2_ref, route_ref, lbase, gfin_ref, local, o_ref):
    tm = x2_ref.shape[0]
    s_rows = local.shape[0]
    route = route_ref[...]
    lp1, lp2 = _local_positions(route, lbase)
    w1, w2 = route[:, R_W1:R_W1 + 1], route[:, R_W2:R_W2 + 1]
    col0 = lax.broadcasted_iota(jnp.int32, (tm, KB), 1).astype(F32)
    moe = None
    for k in range(s_rows // KB):
        col = col0 + float(k * KB)
        pick = (jnp.where(col == lp1, w1, 0.0) + jnp.where(col == lp2, w2, 0.0)).astype(BF16)
        part = _dot(pick, local[k * KB:(k + 1) * KB, :])
        moe = part if moe is None else moe + part
    yield
    o_ref[...] = _rms(x2_ref[...] + moe, gfin_ref[...])


def _combine_kernel(dst_ref, nq_ref, tpos_ref, tn_ref, x2_ref, route_ref, lbase_ref, gfin_ref, ys_ref,
                    o_ref, local_scr, cnt_scr, sem):
    del tpos_ref, tn_ref
    i = pl.program_id(0)
    n = pl.num_programs(0)
    slot = i % 2
    per_step = local_scr.shape[1]
    s_rows = local_scr.shape[2]

    def fetch(step, s):
        for k in range(per_step):
            tile = step * per_step + k

            def clear(r, carry, k=k):
                local_scr[s, k, pl.ds(pl.multiple_of(r * SEG_ALIGN, SEG_ALIGN), SEG_ALIGN), :] = jnp.zeros(
                    (SEG_ALIGN, local_scr.shape[3]), BF16)
                return carry

            lax.fori_loop(nq_ref[tile], s_rows // SEG_ALIGN, clear, 0)
            cnt_scr[s * per_step + k] = _segment_copies(tile, dst_ref, nq_ref, local_scr.at[s, k], ys_ref,
                                                        sem.at[s * per_step + k], False)

    @pl.when(i == 0)
    def _():
        fetch(0, 0)

    @pl.when(i + 1 < n)
    def _():
        fetch(i + 1, 1 - slot)

    tiles = []
    for k in range(per_step):
        c = slot * per_step + k
        _wait_chunks(cnt_scr[c], local_scr.at[slot, k], ys_ref, sem.at[c])
        rows = slice(k * TM_MIX, (k + 1) * TM_MIX)
        tiles.append(_combine_tile(x2_ref.at[rows, :], route_ref.at[rows, :], lbase_ref[k], gfin_ref,
                                   local_scr.at[slot, k], o_ref.at[rows, :]))
    _lockstep(tiles)


def _combine(plan, x2, route, lbase_f, gfin, y_sorted):
    t, dm = x2.shape
    tm = TM_MIX * MOE_TILES
    s_rows = _local_rows(TM_MIX)
    im = lambda i, *_: (i, 0)
    return pl.pallas_call(
        _combine_kernel,
        grid_spec=pltpu.PrefetchScalarGridSpec(
            num_scalar_prefetch=4,
            grid=(t // tm,),
            in_specs=[pl.BlockSpec((tm, dm), im), pl.BlockSpec((tm, LANES), im),
                      pl.BlockSpec((MOE_TILES, 1, LANES), lambda i, *_: (i, 0, 0)),
                      pl.BlockSpec((1, dm), lambda i, *_: (0, 0)),
                      pl.BlockSpec(memory_space=pl.ANY)],
            out_specs=pl.BlockSpec((tm, dm), im),
            scratch_shapes=[pltpu.VMEM((2, MOE_TILES, s_rows, dm), BF16), pltpu.SMEM((2 * MOE_TILES,), jnp.int32),
                            pltpu.SemaphoreType.DMA((2 * MOE_TILES,))],
        ),
        out_shape=jax.ShapeDtypeStruct((t, dm), F32),
        compiler_params=pltpu.CompilerParams(dimension_semantics=("arbitrary",),
                                             vmem_limit_bytes=VMEM_LIMIT),
        name="combine_norm",
    )(*_plan_specs(plan), x2, route, lbase_f, gfin, y_sorted)


def _layer(x, p, s5_ops, gfin):
    b, l, dm = x.shape
    x2d = x.reshape(b * l, dm)
    mgm, us5 = _inproj_gmlp(x2d, p['gmix'], p['win'], p['lng'], p['lnb'], p['ws'], p['bs'], p['gout_gm'])
    n_seg = SUBLANES // b
    *lag_factors, w1, w2, sc = s5_ops[(l // (S5_LC * n_seg))]
    xg = _s5_inproj(x, p['gmix'], p['win_s5'], n_seg)
    yg = _s5_scan(xg, lag_factors, w1, w2, sc, n_seg)
    ys = _s5_to_tokens(yg, b, l, n_seg)
    x2, t_bf, route, counts = _mix_route(ys, us5, mgm, x2d, p['d'], p['gluw'], p['glub'], p['gout_s5'],
                                         p['wout'], p['gffn'], p['rw_hl'], p['rb'])
    plan, lbase_f, tiles, n_sorted = _segment_plan(counts, b * l, TM_EXPERT)
    x_sorted = _sort_rows(plan, tiles[1], t_bf, route, lbase_f, n_sorted)
    y_sorted = _experts(*tiles, x_sorted, p['w_gate'], p['w_up'], p['w_down'])
    out = _combine(plan, x2, route, lbase_f, gfin, y_sorted)
    return out.reshape(b, l, dm)


def kernel(x_prompt, x_sample, norm_mix_g, w_in, gm_ln_g, gm_ln_b, gm_ws, gm_bs, s5_lam_re_fwd, s5_lam_im_fwd, s5_log_step_fwd, s5_b_re_fwd, s5_b_im_fwd, s5_c_re_fwd, s5_c_im_fwd, s5_lam_re_bwd, s5_lam_im_bwd, s5_log_step_bwd, s5_b_re_bwd, s5_b_im_bwd, s5_c_re_bwd, s5_c_im_bwd, s5_d, s5_glu_w, s5_glu_b, out_norm_gm, out_norm_s5, w_out, norm_ffn_g, r1_w, r1_b, r2_w, r2_b, e_w_gate, e_w_up, e_w_down, norm_final_g):
    depth = w_in.shape[0]
    gfin = norm_final_g.reshape(1, -1).astype(F32)
    xs = [x_prompt, x_sample]
    for li in range(depth):
        row = lambda a: a[li].reshape(1, -1).astype(F32)
        dm = w_in.shape[1]
        gw = gm_ln_g.shape[1]
        hd_dim = gw // GM_HEADS
        rw = jnp.concatenate([r1_w[li], r2_w[li].transpose(1, 0, 2).reshape(dm, N_EXPERTS)], axis=1).astype(F32)
        rw = jnp.pad(rw, ((0, 0), (0, LANES - rw.shape[1])))
        rwh = rw.astype(BF16)
        rwl = (rw - rwh.astype(F32)).astype(BF16)
        rb = jnp.concatenate([r1_b[li], r2_b[li].reshape(-1)]).astype(F32)
        rb = jnp.pad(rb, (0, LANES - rb.shape[0])).reshape(1, LANES)
        p = dict(
            gmix=row(norm_mix_g), win=w_in[li].astype(BF16), win_s5=w_in[li][:, 2 * gw:].astype(BF16),
            lng=row(gm_ln_g), lnb=row(gm_ln_b),
            ws=gm_ws[li].astype(BF16),
            bs=jnp.broadcast_to(gm_bs[li].astype(F32)[:, :, None], (GM_HEADS, CHUNK, hd_dim)),
            gout_gm=row(out_norm_gm), d=row(s5_d), gluw=s5_glu_w[li].astype(BF16), glub=row(s5_glu_b),
            gout_s5=row(out_norm_s5), wout=w_out[li].astype(BF16), gffn=row(norm_ffn_g),
            rw_hl=jnp.concatenate([rwh, rwl], axis=1), rb=rb,
            w_gate=e_w_gate[li], w_up=e_w_up[li], w_down=e_w_down[li],
        )
        fwd = (s5_lam_re_fwd[li], s5_lam_im_fwd[li], s5_log_step_fwd[li], s5_b_re_fwd[li], s5_b_im_fwd[li],
               s5_c_re_fwd[li], s5_c_im_fwd[li])
        bwd = (s5_lam_re_bwd[li], s5_lam_im_bwd[li], s5_log_step_bwd[li], s5_b_re_bwd[li], s5_b_im_bwd[li],
               s5_c_re_bwd[li], s5_c_im_bwd[li])
        s5_ops = {}
        for x in xs:
            seg_steps = x.shape[1] // (S5_LC * (SUBLANES // x.shape[0]))
            if seg_steps not in s5_ops:
                s5_ops[seg_steps] = _s5_operator(fwd, bwd, S5_LC, seg_steps)
        last = li == depth - 1
        assert last, "depth > 1 needs an un-normalised layer output"
        xs = [_layer(x, p, s5_ops, gfin) for x in xs]
    return tuple(xs)
```

```python
import functools
import math

import jax
import jax.numpy as jnp
from jax import lax
from jax.experimental import pallas as pl
from jax.experimental.pallas import tpu as pltpu

F32 = jnp.float32
BF16 = jnp.bfloat16

EPS = 1e-6
LAMBDA_RE_MAX = -1e-4
GM_HEADS = 4
CHUNK = 128
S5_GROUP = 16
S5_STATE = 64
N_COARSE = 4
N_FINE = 8
N_EXPERTS = N_COARSE * N_FINE

LANES = 128
SUBLANES = 8
S5_LC = 16
VMEM_LIMIT = 56 * 1024 * 1024

TM_PROJ = 1024
TM_MIX = 512
MIX_TILES = 2
KB = 256
TM_EXPERT = 1024
SEG_ALIGN = 16


def _gelu(x):
    c = math.sqrt(2.0 / math.pi)
    half = 0.5 * x
    return half + half * jnp.tanh(x * (c + (c * 0.044715) * (x * x)))


def _rms(x, g):
    ms = jnp.mean(x * x, axis=-1, keepdims=True)
    return x * lax.rsqrt(ms + EPS) * g


def _dot(a, b):
    return jnp.dot(a, b, preferred_element_type=F32)


def _lockstep(tiles):
    while tiles:
        tiles = [t for t in tiles if next(t, "done") != "done"]


def _inproj_gmlp_kernel(x_ref, gmix_ref, win_ref, lng_ref, lnb_ref, ws_ref, bs_ref, gout_ref,
                        mgm_ref, us5_ref, y_scr):
    tm = x_ref.shape[0]
    gw = mgm_ref.shape[1]
    hd_dim = gw // GM_HEADS
    n_chunks = tm // CHUNK
    h = _rms(x_ref[...], gmix_ref[...]).astype(BF16)
    proj = _dot(h, win_ref[...])
    us5_ref[...] = proj[:, 2 * gw:]
    for hd in range(GM_HEADS):
        lo = hd * hd_dim
        vh = _gelu(proj[:, gw + lo:gw + lo + hd_dim])
        mu = jnp.mean(vh, axis=-1, keepdims=True)
        xc = vh - mu
        var = jnp.mean(xc * xc, axis=-1, keepdims=True)
        vn = (xc * lax.rsqrt(var + EPS) * lng_ref[:, lo:lo + hd_dim]
              + lnb_ref[:, lo:lo + hd_dim]).astype(BF16)
        rhs = jnp.concatenate([vn[c * CHUNK:(c + 1) * CHUNK] for c in range(n_chunks)], axis=1)
        s = _dot(ws_ref[hd], rhs)
        for c in range(n_chunks):
            sc = s[:, c * hd_dim:(c + 1) * hd_dim] + bs_ref[hd]
            u = _gelu(proj[c * CHUNK:(c + 1) * CHUNK, lo:lo + hd_dim])
            y_scr[c * CHUNK:(c + 1) * CHUNK, lo:lo + hd_dim] = u * sc
    mgm_ref[...] = _rms(y_scr[...], gout_ref[...]).astype(BF16)


def _inproj_gmlp(x2d, gmix, win_bf, lng, lnb, ws_bf, bs_b, gout):
    t, d = x2d.shape
    d_in = win_bf.shape[1]
    gw = lng.shape[1]
    s5w = d_in - 2 * gw
    tm = TM_PROJ
    const = lambda *shape: pl.BlockSpec(shape, lambda i: (0,) * len(shape))
    return pl.pallas_call(
        _inproj_gmlp_kernel,
        grid=(t // tm,),
        in_specs=[
            pl.BlockSpec((tm, d), lambda i: (i, 0)),
            const(1, d), const(d, d_in), const(1, gw), const(1, gw),
            const(GM_HEADS, CHUNK, CHUNK), const(GM_HEADS, CHUNK, gw // GM_HEADS), const(1, gw),
        ],
        out_specs=[pl.BlockSpec((tm, gw), lambda i: (i, 0)),
                   pl.BlockSpec((tm, s5w), lambda i: (i, 0))],
        out_shape=[jax.ShapeDtypeStruct((t, gw), BF16),
                   jax.ShapeDtypeStruct((t, s5w), F32)],
        scratch_shapes=[pltpu.VMEM((tm, gw), F32)],
        compiler_params=pltpu.CompilerParams(dimension_semantics=("parallel",),
                                             vmem_limit_bytes=VMEM_LIMIT),
        name="inproj_gmlp",
    )(x2d, gmix, win_bf, lng, lnb, ws_bf, bs_b, gout)


def _s5_consts(lam_re, lam_im, log_step, b_re, b_im, c_re, c_im, lc):
    lr = jnp.minimum(lam_re.astype(F32), LAMBDA_RE_MAX)
    li = lam_im.astype(F32)
    step = jnp.exp(log_step.astype(F32))[:, None]
    dr, di = lr * step, li * step
    ar, ai = _cexp(dr, di)
    nr, ni = ar - 1.0, ai
    den = lr * lr + li * li
    qr, qi = (nr * lr + ni * li) / den, (ni * lr - nr * li) / den
    br, bi = b_re.astype(F32), b_im.astype(F32)
    bbr = qr[..., None] * br - qi[..., None] * bi
    bbi = qr[..., None] * bi + qi[..., None] * br
    k = jnp.arange(lc + 1, dtype=F32)[:, None, None]
    pwr, pwi = _cexp(k * dr[None], k * di[None])
    return (dr, di), (pwr, pwi), (bbr, bbi), (c_re.astype(F32), c_im.astype(F32))


def _cexp(zr, zi):
    m = jnp.exp(zr)
    return m * jnp.cos(zi), m * jnp.sin(zi)


def _s5_operator(fwd, bwd, lc, seg_steps):
    consts = [_s5_consts(*fwd, lc), _s5_consts(*bwd, lc)]
    g, p, h = consts[0][2][0].shape
    lags, w1_parts, w2_parts, sc_rows, seg_rows = [], [], [], [], []
    for direction, (ld, pw, bb, c) in enumerate(consts):
        (dr, di), (pwr, pwi), (bbr, bbi), (cr, ci) = ld, pw, bb, c
        crt, cit = cr.transpose(0, 2, 1), ci.transpose(0, 2, 1)
        pwrt, pwit = pwr.transpose(1, 2, 0), pwi.transpose(1, 2, 0)
        cpr = crt[:, :, None, :] * pwrt[:, :, :, None] - cit[:, :, None, :] * pwit[:, :, :, None]
        cpi = crt[:, :, None, :] * pwit[:, :, :, None] + cit[:, :, None, :] * pwrt[:, :, :, None]
        ck = jnp.concatenate([cpr[:, :, :lc], cpi[:, :, :lc]], axis=1)
        if direction == 1:
            ck = jnp.flip(ck, 2)
        lags += [jnp.concatenate([bbr.transpose(0, 2, 1), -bbi.transpose(0, 2, 1)], axis=-1),
                 ck.reshape(g, 2 * p, lc * h)]
        er, ei = pwrt[:, :, :lc].transpose(0, 2, 1), pwit[:, :, :lc].transpose(0, 2, 1)
        if direction == 0:
            er, ei = jnp.flip(er, 1), jnp.flip(ei, 1)
        bbrt, bbit = bbr.transpose(0, 2, 1), bbi.transpose(0, 2, 1)
        e1 = jnp.concatenate([er, er], -1)[:, :, None, :]
        e2 = jnp.concatenate([-ei, ei], -1)[:, :, None, :]
        b_ri = jnp.concatenate([bbrt, bbit], -1)[:, None]
        b_ir = jnp.concatenate([bbit, bbrt], -1)[:, None]
        w1_parts += [e1 * b_ri + e2 * b_ir, e1 * b_ir - e2 * b_ri]
        fr, fi = cpr[:, :, 1:lc + 1], cpi[:, :, 1:lc + 1]
        if direction == 1:
            fr, fi = jnp.flip(fr, 2), jnp.flip(fi, 2)
        w2_parts += [fr, -fi]

        def mult(zr, zi):
            return [jnp.concatenate([zr, zr], -1), jnp.concatenate([-zi, zi], -1)]

        sc_rows += mult(*_cexp(lc * dr, lc * di))
        seg_rows += mult(*_cexp((lc * seg_steps) * dr, (lc * seg_steps) * di))
    w1 = jnp.concatenate(w1_parts, axis=-1).reshape(g, lc * h, 8 * p)
    w2 = jnp.concatenate(w2_parts, axis=1).reshape(g, 4 * p, lc * h)
    sc = jnp.stack(sc_rows + seg_rows, axis=1)
    return tuple(lags) + (w1.astype(BF16), w2.astype(BF16), sc.astype(F32))


S5_GPS = 2


def _s5_kernel(x_ref, bbf_ref, cpf_ref, bbb_ref, cpb_ref, w1_ref, w2_ref, sc_ref, y_ref,
               loc_scr, sin_scr, m_scr, *, n_seg):
    gps, rows, kw = x_ref.shape
    steps = rows // SUBLANES
    sw = sc_ref.shape[2]

    for gi in range(gps):
        kf = jnp.dot(bbf_ref[gi], cpf_ref[gi], precision=lax.Precision.HIGHEST, preferred_element_type=F32)
        kb = jnp.dot(bbb_ref[gi], cpb_ref[gi], precision=lax.Precision.HIGHEST, preferred_element_type=F32)
        hch = kf.shape[0]
        lc = kw // hch
        lane = lax.broadcasted_iota(jnp.int32, kf.shape, 1)
        for s in range(lc):
            f = kf if s == 0 else jnp.where(lane >= s * hch, pltpu.roll(kf, s * hch, 1), 0.0)
            left = (lc - 1 - s) * hch
            b = kb if left == 0 else pltpu.roll(kb, kw - left, 1)
            m_scr[gi, s * hch:(s + 1) * hch, :] = (f + jnp.where(lane < (s + 1) * hch, b, 0.0)).astype(BF16)
        loc_scr[gi] = _dot(x_ref[gi], w1_ref[gi])

    def bc(gi, i):
        return jnp.broadcast_to(sc_ref[gi, i:i + 1, :], (SUBLANES, sw))

    mult = [[bc(gi, i) for i in range(8)] for gi in range(gps)]

    def step(gi, s, state):
        f, fs, b, bs = state
        a1f, a2f, a1b, a2b = mult[gi][:4]
        rf = pl.multiple_of(s * SUBLANES, SUBLANES)
        rb = pl.multiple_of((steps - 1 - s) * SUBLANES, SUBLANES)
        lf = loc_scr[gi, pl.ds(rf, SUBLANES), 0:sw]
        lfs = loc_scr[gi, pl.ds(rf, SUBLANES), sw:2 * sw]
        lb = loc_scr[gi, pl.ds(rb, SUBLANES), 2 * sw:3 * sw]
        lbs = loc_scr[gi, pl.ds(rb, SUBLANES), 3 * sw:4 * sw]
        return (a1f * f + a2f * fs + lf, a1f * fs - a2f * f + lfs,
                a1b * b + a2b * bs + lb, a1b * bs - a2b * b + lbs)

    zero = jnp.zeros((SUBLANES, sw), F32)

    def pass1(s, carry):
        return tuple(step(gi, s, carry[gi]) for gi in range(gps))

    ends = lax.fori_loop(0, steps, pass1, tuple((zero,) * 4 for _ in range(gps)), unroll=4)

    seg = lax.broadcasted_iota(jnp.int32, (SUBLANES, sw), 0) % n_seg
    enter = []
    for gi in range(gps):
        f_end, fs_end, b_end, bs_end = ends[gi]
        p1f, p2f, p1b, p2b = mult[gi][4:]
        cf, cfs, cb, cbs = zero, zero, zero, zero
        for _ in range(n_seg - 1):
            ef = f_end + p1f * cf + p2f * cfs
            efs = fs_end + p1f * cfs - p2f * cf
            eb = b_end + p1b * cb + p2b * cbs
            ebs = bs_end + p1b * cbs - p2b * cb
            cf = jnp.where(seg >= 1, pltpu.roll(ef, 1, 0), 0.0)
            cfs = jnp.where(seg >= 1, pltpu.roll(efs, 1, 0), 0.0)
            cb = jnp.where(seg <= n_seg - 2, pltpu.roll(eb, SUBLANES - 1, 0), 0.0)
            cbs = jnp.where(seg <= n_seg - 2, pltpu.roll(ebs, SUBLANES - 1, 0), 0.0)
        enter.append((cf, cfs, cb, cbs))

    def pass2(s, carry):
        rf = pl.multiple_of(s * SUBLANES, SUBLANES)
        rb = pl.multiple_of((steps - 1 - s) * SUBLANES, SUBLANES)
        for gi in range(gps):
            sin_scr[gi, pl.ds(rf, SUBLANES), 0:sw] = carry[gi][0]
            sin_scr[gi, pl.ds(rb, SUBLANES), sw:2 * sw] = carry[gi][2]
        return tuple(step(gi, s, carry[gi]) for gi in range(gps))

    lax.fori_loop(0, steps, pass2, tuple(enter), unroll=4)

    for gi in range(gps):
        y_ref[gi] = _dot(x_ref[gi], m_scr[gi]) + _dot(sin_scr[gi].astype(BF16), w2_ref[gi])


def _s5_scan(xg, lag_factors, w1, w2, sc, n_seg):
    g, rows, kw = xg.shape
    sw = sc.shape[2]
    gps = S5_GPS
    blk = lambda a: pl.BlockSpec((gps,) + a.shape[1:], lambda i: (i, 0, 0))
    return pl.pallas_call(
        functools.partial(_s5_kernel, n_seg=n_seg),
        grid=(g // gps,),
        in_specs=[blk(xg)] + [blk(a) for a in lag_factors] + [blk(w1), blk(w2), blk(sc)],
        out_specs=pl.BlockSpec((gps, rows, kw), lambda i: (i, 0, 0)),
        out_shape=jax.ShapeDtypeStruct((g, rows, kw), F32),
        scratch_shapes=[pltpu.VMEM((gps, rows, 4 * sw), F32), pltpu.VMEM((gps, rows, 2 * sw), F32),
                        pltpu.VMEM((gps, kw, kw), BF16)],
        compiler_params=pltpu.CompilerParams(dimension_semantics=("parallel",),
                                             vmem_limit_bytes=VMEM_LIMIT),
        name="s5_scan",
    )(xg, *lag_factors, w1, w2, sc)


S5_NM = 16


def _block_transpose8(groups, width):
    lane = lax.broadcasted_iota(jnp.int32, groups[0][0].shape, 1)
    for d in (4, 2, 1):
        w = width * d
        hi = ((lane // w) % 2) == 1
        nxt = []
        for v in groups:
            out = list(v)
            for i0 in range(8):
                if i0 & d:
                    continue
                i1 = i0 + d
                out[i0] = jnp.where(hi, pltpu.roll(v[i1], w, 1), v[i0])
                out[i1] = jnp.where(hi, v[i1], pltpu.roll(v[i0], 8 * width - w, 1))
            nxt.append(out)
        groups = nxt
    return groups


def _tile_copies(hbm4, tile, buf, slot, sem, nm, to_hbm):
    copies = []
    for c in range(SUBLANES):
        for j in range(S5_LC):
            h = hbm4.at[c, pl.ds(tile * nm, nm), pl.ds(j, 1), :]
            v = buf.at[slot, j, :, pl.ds(c, 1), :]
            copies.append(pltpu.make_async_copy(v, h, sem.at[slot]) if to_hbm
                          else pltpu.make_async_copy(h, v, sem.at[slot]))
    return copies


def _s5_inproj_kernel(x4_ref, gmix_ref, w_ref, xg_ref, xs, sem, *, nm):
    i = pl.program_id(0)
    n = pl.num_programs(0)
    slot = i % 2
    dm = x4_ref.shape[3]

    @pl.when(i == 0)
    def _():
        for cp in _tile_copies(x4_ref, 0, xs, 0, sem, nm, False):
            cp.start()

    @pl.when(i + 1 < n)
    def _():
        for cp in _tile_copies(x4_ref, i + 1, xs, 1 - slot, sem, nm, False):
            cp.start()

    pltpu.make_async_copy(xs.at[slot], xs.at[slot], sem.at[slot]).wait()
    rows = nm * SUBLANES

    def half(a):
        x = xs[slot, 8 * a:8 * a + 8].reshape(8 * rows, dm)
        h = _rms(x, gmix_ref[...]).astype(BF16)
        yield
        z = _dot(h, w_ref[...])
        yield
        n_oct = z.shape[1] // LANES
        blocks = [[z[j8 * rows:(j8 + 1) * rows, q * LANES:(q + 1) * LANES] for j8 in range(8)] for q in range(n_oct)]
        for q, out in enumerate(_block_transpose8(blocks, S5_GROUP)):
            for g8, b in enumerate(out):
                xg_ref[8 * q + g8, :, a * LANES:(a + 1) * LANES] = b.astype(BF16)

    _lockstep([half(a) for a in range(S5_LC // 8)])


def _s5_inproj(x, gmix, w_s5_bf, n_seg):
    b, l, dm = x.shape
    steps = l // (S5_LC * n_seg)
    nm = S5_NM
    s5w = w_s5_bf.shape[1]
    g = s5w // S5_GROUP
    x4 = x.reshape(b * n_seg, steps, S5_LC, dm)
    return pl.pallas_call(
        functools.partial(_s5_inproj_kernel, nm=nm),
        grid=(steps // nm,),
        in_specs=[pl.BlockSpec(memory_space=pl.ANY),
                  pl.BlockSpec((1, dm), lambda i: (0, 0)),
                  pl.BlockSpec((dm, s5w), lambda i: (0, 0))],
        out_specs=pl.BlockSpec((g, nm * SUBLANES, S5_LC * S5_GROUP), lambda i: (0, i, 0)),
        out_shape=jax.ShapeDtypeStruct((g, steps * SUBLANES, S5_LC * S5_GROUP), BF16),
        scratch_shapes=[pltpu.VMEM((2, S5_LC, nm, SUBLANES, dm), F32), pltpu.SemaphoreType.DMA((2,))],
        compiler_params=pltpu.CompilerParams(dimension_semantics=("arbitrary",),
                                             vmem_limit_bytes=VMEM_LIMIT),
        name="s5_inproj",
    )(x4, gmix, w_s5_bf)


def _s5_to_tokens_kernel(yg_ref, ys4_ref, zs, sem, *, nm):
    i = pl.program_id(0)
    n = pl.num_programs(0)
    slot = i % 2
    rows = nm * SUBLANES

    def wait(s):
        pltpu.make_async_copy(zs.at[s], zs.at[s], sem.at[s]).wait()

    @pl.when(i >= 2)
    def _():
        wait(slot)

    n_oct = yg_ref.shape[0] // 8
    combos = [(q, a) for q in range(n_oct) for a in range(S5_LC // 8)]
    blocks = [[yg_ref[8 * q + g8, :, a * LANES:(a + 1) * LANES] for g8 in range(8)] for q, a in combos]
    for (q, a), out in zip(combos, _block_transpose8(blocks, S5_GROUP)):
        for j8, b in enumerate(out):
            zs[slot, 8 * a + j8, :, :, q * LANES:(q + 1) * LANES] = b.reshape(nm, SUBLANES, LANES)
    for cp in _tile_copies(ys4_ref, i, zs, slot, sem, nm, True):
        cp.start()

    @pl.when(i == n - 1)
    def _():
        wait(1 - slot)
        wait(slot)


def _s5_to_tokens(yg, b, l, n_seg):
    g, rows_total, kw = yg.shape
    steps = rows_total // SUBLANES
    nm = S5_NM
    s5w = g * S5_GROUP
    assert steps // nm >= 2
    ys4 = pl.pallas_call(
        functools.partial(_s5_to_tokens_kernel, nm=nm),
        grid=(steps // nm,),
        in_specs=[pl.BlockSpec((g, nm * SUBLANES, kw), lambda i: (0, i, 0))],
        out_specs=pl.BlockSpec(memory_space=pl.ANY),
        out_shape=jax.ShapeDtypeStruct((b * n_seg, steps, S5_LC, s5w), F32),
        scratch_shapes=[pltpu.VMEM((2, S5_LC, nm, SUBLANES, s5w), F32), pltpu.SemaphoreType.DMA((2,))],
        compiler_params=pltpu.CompilerParams(dimension_semantics=("arbitrary",),
                                             vmem_limit_bytes=VMEM_LIMIT),
        name="s5_to_tokens",
    )(yg)
    return ys4.reshape(b * l, s5w)


R_E1, R_E2, R_W1, R_W2, R_RANK1, R_RANK2 = range(6)


def _mix_route_kernel(ys_ref, us5_ref, mgm_ref, x_ref, d_ref, gluw_ref, glub_ref, gs5_ref,
                      wout_ref, gffn_ref, rw_ref, rb_ref, tri_ref,
                      x2_ref, t_ref, route_ref, cnt_ref):
    tiles = []
    for k in range(x_ref.shape[0] // TM_MIX):
        rows = slice(k * TM_MIX, (k + 1) * TM_MIX)
        tiles.append(_mix_route_tile(ys_ref.at[rows, :], us5_ref.at[rows, :], mgm_ref.at[rows, :], x_ref.at[rows, :],
                                     d_ref, gluw_ref, glub_ref, gs5_ref, wout_ref, gffn_ref, rw_ref,
                                     rb_ref, tri_ref, x2_ref.at[rows, :], t_ref.at[rows, :], route_ref.at[rows, :],
                                     cnt_ref.at[k]))
    _lockstep(tiles)


def _mix_route_tile(ys_ref, us5_ref, mgm_ref, x_ref, d_ref, gluw_ref, glub_ref, gs5_ref,
                    wout_ref, gffn_ref, rw_ref, rb_ref, tri_ref,
                    x2_ref, t_ref, route_ref, cnt_ref):
    gw = mgm_ref.shape[1]
    y = ys_ref[...] + d_ref[...] * us5_ref[...]
    g = _gelu(y)
    yield
    gate = _dot(g.astype(BF16), gluw_ref[...])
    yield
    z = g * jax.nn.sigmoid(gate + glub_ref[...])
    ms5 = _rms(z, gs5_ref[...]).astype(BF16)
    yield
    mix = _dot(mgm_ref[...], wout_ref[:gw, :]) + _dot(ms5, wout_ref[gw:, :])
    yield
    x2 = x_ref[...] + mix
    x2_ref[...] = x2
    t = _rms(x2, gffn_ref[...])
    t_hi = t.astype(BF16)
    t_ref[...] = t_hi
    t_lo = (t - t_hi.astype(F32)).astype(BF16)
    yield
    hl = _dot(t_hi, rw_ref[...])
    logits = (hl[:, :LANES] + hl[:, LANES:] + _dot(t_lo, rw_ref[:, :LANES])
              + rb_ref[...])
    yield
    lane = lax.broadcasted_iota(jnp.int32, logits.shape, 1).astype(F32)
    neg = jnp.float32(-jnp.inf)

    def first_max(mask):
        vals = jnp.where(mask, logits, neg)
        mx = jnp.max(vals, axis=-1, keepdims=True)
        idx = jnp.min(jnp.where(mask & (vals == mx), lane, float(LANES)), axis=-1, keepdims=True)
        return mx, idx

    coarse = lane < N_COARSE
    m1, grp = first_max(coarse)
    p_grp = 1.0 / jnp.sum(jnp.where(coarse, jnp.exp(logits - m1), 0.0), axis=-1, keepdims=True)
    lo = N_COARSE + grp * N_FINE
    fine = (lane >= lo) & (lane < lo + N_FINE)
    v1, i1 = first_max(fine)
    v2, i2 = first_max(fine & (lane != i1))
    e21 = jnp.exp(v2 - v1)
    w1 = p_grp / (1.0 + e21)
    w2 = p_grp * e21 / (1.0 + e21)
    e1 = i1 - N_COARSE
    e2 = i2 - N_COARSE
    hit1 = lane == e1
    hit2 = lane == e2
    onehot = jnp.where(hit1 | hit2, 1.0, 0.0)
    before = _dot(tri_ref[...], onehot.astype(BF16))
    rank1 = jnp.sum(jnp.where(hit1, before, 0.0), axis=-1, keepdims=True)
    rank2 = jnp.sum(jnp.where(hit2, before, 0.0), axis=-1, keepdims=True)
    tm = onehot.shape[0]
    cnt_ref[...] = before[tm - 1:tm, :] + onehot[tm - 1:tm, :]
    rec = jnp.zeros_like(logits)
    for slot, val in ((R_E1, e1), (R_E2, e2), (R_W1, w1), (R_W2, w2),
                      (R_RANK1, rank1), (R_RANK2, rank2)):
        rec = jnp.where(lane == slot, val, rec)
    route_ref[...] = rec


def _mix_route(ys, us5, mgm, x2d, d, gluw_bf, glub, gs5, wout_bf, gffn, rw_hl, rb):
    t, dm = x2d.shape
    gw = mgm.shape[1]
    s5w = us5.shape[1]
    tm = TM_MIX * MIX_TILES
    tri = jnp.tril(jnp.ones((TM_MIX, TM_MIX), F32), -1).astype(BF16)
    const = lambda *shape: pl.BlockSpec(shape, lambda i: (0,) * len(shape))
    tile = lambda w: pl.BlockSpec((tm, w), lambda i: (i, 0))
    return pl.pallas_call(
        _mix_route_kernel,
        grid=(t // tm,),
        in_specs=[tile(s5w), tile(s5w), tile(gw), tile(dm),
                  const(1, s5w), const(s5w, s5w), const(1, s5w), const(1, s5w),
                  const(gw + s5w, dm), const(1, dm), const(dm, 2 * LANES), const(1, LANES),
                  const(TM_MIX, TM_MIX)],
        out_specs=[tile(dm), tile(dm), tile(LANES), pl.BlockSpec((MIX_TILES, 1, LANES), lambda i: (i, 0, 0))],
        out_shape=[jax.ShapeDtypeStruct((t, dm), F32),
                   jax.ShapeDtypeStruct((t, dm), BF16),
                   jax.ShapeDtypeStruct((t, LANES), F32),
                   jax.ShapeDtypeStruct((t // TM_MIX, 1, LANES), F32)],
        compiler_params=pltpu.CompilerParams(dimension_semantics=("parallel",),
                                             vmem_limit_bytes=VMEM_LIMIT),
        name="mix_route",
    )(ys, us5, mgm, x2d, d, gluw_bf, glub, gs5, wout_bf, gffn, rw_hl, rb, tri)


def _local_rows(tm):
    worst = 2 * tm + N_EXPERTS * (SEG_ALIGN - 1)
    return -(-worst // LANES) * LANES


def _segment_plan(cnt, t, tm_expert):
    c = cnt[:, 0, :N_EXPERTS].astype(jnp.int32)
    n_tok_tiles = c.shape[0]
    al = (c + SEG_ALIGN - 1) // SEG_ALIGN * SEG_ALIGN
    lbase = jnp.cumsum(al, axis=1) - al
    tot = jnp.sum(al, axis=0)
    tot_pad = (tot + tm_expert - 1) // tm_expert * tm_expert
    gbase = jnp.cumsum(tot_pad) - tot_pad
    gpos = gbase[None, :] + jnp.cumsum(al, axis=0) - al
    n_tiles_max = -(-(2 * t + n_tok_tiles * N_EXPERTS * (SEG_ALIGN - 1)) // tm_expert) + N_EXPERTS
    tile_end = jnp.cumsum(tot_pad // tm_expert)
    n_tiles = tile_end[-1:].astype(jnp.int32)
    tile_idx = jnp.arange(n_tiles_max, dtype=jnp.int32)
    tile_expert = jnp.sum((tile_idx[:, None] >= tile_end[None, :]).astype(jnp.int32), axis=1)
    last = jnp.sum((n_tiles - 1 >= tile_end).astype(jnp.int32))
    tile_expert = jnp.where(tile_idx < n_tiles, tile_expert, last).astype(jnp.int32)
    ids = jnp.arange(N_EXPERTS, dtype=jnp.int32)
    later_used = (ids[None, :] > ids[:, None]) & (tot_pad[None, :] > 0)
    next_expert = jnp.min(jnp.where(later_used, ids[None, :], N_EXPERTS), axis=1)
    next_expert = jnp.where(next_expert == N_EXPERTS, ids, next_expert).astype(jnp.int32)
    lbase_f = jnp.pad(lbase.astype(F32), ((0, 0), (0, LANES - N_EXPERTS)))[:, None, :]
    flat = lambda a: a.reshape(-1).astype(jnp.int32)
    nch = al // SEG_ALIGN
    cum = jnp.cumsum(nch, axis=1)
    q = jnp.arange(_local_rows(TM_MIX) // SEG_ALIGN, dtype=jnp.int32)[None, :, None]
    seg_of_q = jnp.sum((q >= cum[:, None, :]).astype(jnp.int32), axis=2)
    in_seg = seg_of_q[:, :, None] == jnp.arange(N_EXPERTS, dtype=jnp.int32)[None, None, :]
    pick = lambda a: jnp.sum(jnp.where(in_seg, a[:, None, :], 0), axis=2)
    dst = pick(gpos) + (q[:, :, 0] - pick(cum - nch)) * SEG_ALIGN
    plan = dict(dst=flat(dst), n_chunks=flat(cum[:, -1]),
                tail_pos=flat(gbase + tot), tail_n=flat((tot_pad - tot) // SEG_ALIGN))
    return plan, lbase_f, (tile_expert, n_tiles, next_expert), n_tiles_max * tm_expert


def _local_positions(route, lbase):
    lane = lax.broadcasted_iota(jnp.int32, route.shape, 1).astype(F32)
    out = []
    for e_lane, r_lane in ((R_E1, R_RANK1), (R_E2, R_RANK2)):
        e = route[:, e_lane:e_lane + 1]
        base = jnp.sum(jnp.where(lane == e, lbase, 0.0), axis=-1, keepdims=True)
        out.append(base + route[:, r_lane:r_lane + 1])
    return out


WAIT_GROUP = 8
ISSUE_GROUP = 4


def _segment_copies(i, dst_ref, nq_ref, local, glob, sem, to_global):
    per_tile = local.shape[0] // SEG_ALIGN
    n = nq_ref[i]

    def start(q):
        lo = local.at[pl.ds(pl.multiple_of(q * SEG_ALIGN, SEG_ALIGN), SEG_ALIGN)]
        gl = glob.at[pl.ds(pl.multiple_of(dst_ref[i * per_tile + q], SEG_ALIGN), SEG_ALIGN)]
        (pltpu.make_async_copy(lo, gl, sem) if to_global else pltpu.make_async_copy(gl, lo, sem)).start()

    def group(k, carry):
        for u in range(ISSUE_GROUP):
            start(k * ISSUE_GROUP + u)
        return carry

    def single(q, carry):
        start(q)
        return carry

    full = n // ISSUE_GROUP
    lax.fori_loop(0, full, group, 0)
    lax.fori_loop(full * ISSUE_GROUP, n, single, 0)
    return n


def _wait_chunks(n, local, glob, sem):
    def wait_rows(rows):
        def one(c, carry):
            pltpu.make_async_copy(local.at[pl.ds(0, rows)], glob.at[pl.ds(0, rows)], sem).wait()
            return carry
        return one

    lax.fori_loop(0, n // WAIT_GROUP, wait_rows(WAIT_GROUP * SEG_ALIGN), 0)
    lax.fori_loop(0, n % WAIT_GROUP, wait_rows(SEG_ALIGN), 0)


def _sort_tile(t_ref, route_ref, lbase, local):
    tm = t_ref.shape[0]
    s_rows = local.shape[0]
    lp1, lp2 = _local_positions(route_ref[...], lbase)
    lane = lax.broadcasted_iota(jnp.int32, (tm, LANES), 1)
    lp_rows = jnp.where(lane == 0, lp1, jnp.where(lane == 1, lp2, -1.0)).T
    row = lax.broadcasted_iota(jnp.int32, (s_rows, tm), 0).astype(F32)
    onehot = jnp.where((row == lp_rows[0:1, :]) | (row == lp_rows[1:2, :]), 1.0, 0.0).astype(BF16)
    yield
    local[...] = _dot(onehot, t_ref[...]).astype(BF16)


def _sort_rows_kernel(dst_ref, nq_ref, tpos_ref, tn_ref, nt_ref, t_ref, route_ref, lbase_ref,
                      xs_ref, local_scr, zero_scr, cnt_scr, sem, zsem):
    i = pl.program_id(0)
    n = pl.num_programs(0)
    slot = i % 2
    per_step = local_scr.shape[1]

    def buf(s, k):
        return local_scr.at[s, k], sem.at[s * per_step + k], s * per_step + k

    def wait_slot(s):
        for k in range(per_step):
            local, sm, c = buf(s, k)
            _wait_chunks(cnt_scr[c], local, xs_ref, sm)

    te = zero_scr.shape[0]
    n_row_tiles = xs_ref.shape[0] // te

    @pl.when(i == 0)
    def _():
        zero_scr[...] = jnp.zeros_like(zero_scr)
        zero_chunk = zero_scr.at[pl.ds(0, SEG_ALIGN)]

        def tail(e, carry):
            def chunk(c, carry):
                dst = xs_ref.at[pl.ds(pl.multiple_of(tpos_ref[e] + c * SEG_ALIGN, SEG_ALIGN), SEG_ALIGN)]
                pltpu.make_async_copy(zero_chunk, dst, zsem).start()
                return carry

            return lax.fori_loop(0, tn_ref[e], chunk, carry)

        lax.fori_loop(0, N_EXPERTS, tail, 0)

        def unused_tile(j, carry):
            pltpu.make_async_copy(zero_scr, xs_ref.at[pl.ds(pl.multiple_of(j * te, te), te)], zsem).start()
            return carry

        lax.fori_loop(nt_ref[0], n_row_tiles, unused_tile, 0)

    @pl.when(i >= 2)
    def _():
        wait_slot(slot)

    tiles = []
    for k in range(per_step):
        rows = slice(k * TM_MIX, (k + 1) * TM_MIX)
        tiles.append(_sort_tile(t_ref.at[rows, :], route_ref.at[rows, :], lbase_ref[k], buf(slot, k)[0]))
    _lockstep(tiles)
    for k in range(per_step):
        local, sm, c = buf(slot, k)
        cnt_scr[c] = _segment_copies(i * per_step + k, dst_ref, nq_ref, local, xs_ref, sm, True)

    @pl.when(i == n - 1)
    def _():
        @pl.when(n >= 2)
        def _():
            wait_slot(1 - slot)

        wait_slot(slot)
        _wait_chunks(lax.fori_loop(0, N_EXPERTS, lambda e, total: total + tn_ref[e], 0), zero_scr, xs_ref, zsem)

        def unused_wait(j, carry):
            pltpu.make_async_copy(zero_scr, xs_ref.at[pl.ds(0, te)], zsem).wait()
            return carry

        lax.fori_loop(nt_ref[0], n_row_tiles, unused_wait, 0)


def _plan_specs(plan):
    keys = ('dst', 'n_chunks', 'tail_pos', 'tail_n')
    return [plan[k] for k in keys]


MOE_TILES = 2


def _sort_rows(plan, n_tiles, t_bf, route, lbase_f, n_sorted):
    t, dm = t_bf.shape
    tm = TM_MIX * MOE_TILES
    s_rows = _local_rows(TM_MIX)
    im = lambda i, *_: (i, 0)
    return pl.pallas_call(
        _sort_rows_kernel,
        grid_spec=pltpu.PrefetchScalarGridSpec(
            num_scalar_prefetch=5,
            grid=(t // tm,),
            in_specs=[pl.BlockSpec((tm, dm), im), pl.BlockSpec((tm, LANES), im),
                      pl.BlockSpec((MOE_TILES, 1, LANES), lambda i, *_: (i, 0, 0))],
            out_specs=pl.BlockSpec(memory_space=pl.ANY),
            scratch_shapes=[pltpu.VMEM((2, MOE_TILES, s_rows, dm), BF16), pltpu.VMEM((TM_EXPERT, dm), BF16),
                            pltpu.SMEM((2 * MOE_TILES,), jnp.int32), pltpu.SemaphoreType.DMA((2 * MOE_TILES,)),
                            pltpu.SemaphoreType.DMA(())],
        ),
        out_shape=jax.ShapeDtypeStruct((n_sorted, dm), BF16),
        compiler_params=pltpu.CompilerParams(dimension_semantics=("arbitrary",),
                                             vmem_limit_bytes=VMEM_LIMIT),
        name="sort_rows",
    )(*_plan_specs(plan), n_tiles, t_bf, route, lbase_f)


def _expert_weight_copies(e, slot, hbm, stage, sem):
    return [pltpu.make_async_copy(h.at[e], s.at[slot], sem.at[slot]) for h, s in zip(hbm, stage)]


def _experts_kernel(te_ref, nt_ref, nxt_ref, xs_ref, wg_ref, wu_ref, wd_ref, ys_ref,
                    sg, su, sd, wg_bf, wu_bf, wd_bf, slot_scr, sem):
    i = pl.program_id(0)
    e = te_ref[i]
    hbm, stage = (wg_ref, wu_ref, wd_ref), (sg, su, sd)

    @pl.when(i == 0)
    def _():
        slot_scr[0] = 0
        for cp in _expert_weight_copies(e, 0, hbm, stage, sem):
            cp.start()

    @pl.when((i == 0) | (e != te_ref[jnp.maximum(i - 1, 0)]))
    def _():
        slot = slot_scr[0]
        for cp in _expert_weight_copies(e, slot, hbm, stage, sem):
            cp.wait()
        wg_bf[...] = sg[slot].astype(BF16)
        wu_bf[...] = su[slot].astype(BF16)
        wd_bf[...] = sd[slot].astype(BF16)
        nxt = nxt_ref[e]

        @pl.when(nxt != e)
        def _():
            for cp in _expert_weight_copies(nxt, 1 - slot, hbm, stage, sem):
                cp.start()

        slot_scr[0] = 1 - slot

    @pl.when(i < nt_ref[0])
    def _():
        x = xs_ref[...]
        hidden = (jax.nn.silu(_dot(x, wg_bf[...])) * _dot(x, wu_bf[...])).astype(BF16)
        ys_ref[...] = _dot(hidden, wd_bf[...]).astype(BF16)


def _experts(tile_expert, n_tiles, next_expert, x_sorted, w_gate, w_up, w_down):
    n_sorted, dm = x_sorted.shape
    de = w_gate.shape[2]
    tm = TM_EXPERT
    return pl.pallas_call(
        _experts_kernel,
        grid_spec=pltpu.PrefetchScalarGridSpec(
            num_scalar_prefetch=3,
            grid=(n_sorted // tm,),
            in_specs=[pl.BlockSpec((tm, dm), lambda i, te, nt, nx: (jnp.minimum(i, nt[0] - 1), 0)),
                      pl.BlockSpec(memory_space=pl.ANY), pl.BlockSpec(memory_space=pl.ANY),
                      pl.BlockSpec(memory_space=pl.ANY)],
            out_specs=pl.BlockSpec((tm, dm), lambda i, te, nt, nx: (jnp.minimum(i, nt[0] - 1), 0)),
            scratch_shapes=[pltpu.VMEM((2, dm, de), F32), pltpu.VMEM((2, dm, de), F32), pltpu.VMEM((2, de, dm), F32),
                            pltpu.VMEM((dm, de), BF16), pltpu.VMEM((dm, de), BF16), pltpu.VMEM((de, dm), BF16),
                            pltpu.SMEM((1,), jnp.int32), pltpu.SemaphoreType.DMA((2,))],
        ),
        out_shape=jax.ShapeDtypeStruct((n_sorted, dm), BF16),
        compiler_params=pltpu.CompilerParams(dimension_semantics=("arbitrary",),
                                             vmem_limit_bytes=VMEM_LIMIT),
        input_output_aliases={3: 0},
        name="experts",
    )(tile_expert, n_tiles, next_expert, x_sorted, w_gate, w_up, w_down)


def _combine_tile(x2_ref, route_ref, lbase, gfin_ref, local, o_ref):
    tm = x2_ref.shape[0]
    s_rows = local.shape[0]
    route = route_ref[...]
    lp1, lp2 = _local_positions(route, lbase)
    w1, w2 = route[:, R_W1:R_W1 + 1], route[:, R_W2:R_W2 + 1]
    col0 = lax.broadcasted_iota(jnp.int32, (tm, KB), 1).astype(F32)
    moe = None
    for k in range(s_rows // KB):
        col = col0 + float(k * KB)
        pick = (jnp.where(col == lp1, w1, 0.0) + jnp.where(col == lp2, w2, 0.0)).astype(BF16)
        part = _dot(pick, local[k * KB:(k + 1) * KB, :])
        moe = part if moe is None else moe + part
    yield
    o_ref[...] = _rms(x2_ref[...] + moe, gfin_ref[...])


def _combine_kernel(dst_ref, nq_ref, tpos_ref, tn_ref, x2_ref, route_ref, lbase_ref, gfin_ref, ys_ref,
                    o_ref, local_scr, cnt_scr, sem):
    del tpos_ref, tn_ref
    i = pl.program_id(0)
    n = pl.num_programs(0)
    slot = i % 2
    per_step = local_scr.shape[1]
    s_rows = local_scr.shape[2]

    def fetch(step, s):
        for k in range(per_step):
            tile = step * per_step + k

            def clear(r, carry, k=k):
                local_scr[s, k, pl.ds(pl.multiple_of(r * SEG_ALIGN, SEG_ALIGN), SEG_ALIGN), :] = jnp.zeros(
                    (SEG_ALIGN, local_scr.shape[3]), BF16)
                return carry

            lax.fori_loop(nq_ref[tile], s_rows // SEG_ALIGN, clear, 0)
            cnt_scr[s * per_step + k] = _segment_copies(tile, dst_ref, nq_ref, local_scr.at[s, k], ys_ref,
                                                        sem.at[s * per_step + k], False)

    @pl.when(i == 0)
    def _():
        fetch(0, 0)

    @pl.when(i + 1 < n)
    def _():
        fetch(i + 1, 1 - slot)

    tiles = []
    for k in range(per_step):
        c = slot * per_step + k
        _wait_chunks(cnt_scr[c], local_scr.at[slot, k], ys_ref, sem.at[c])
        rows = slice(k * TM_MIX, (k + 1) * TM_MIX)
        tiles.append(_combine_tile(x2_ref.at[rows, :], route_ref.at[rows, :], lbase_ref[k], gfin_ref,
                                   local_scr.at[slot, k], o_ref.at[rows, :]))
    _lockstep(tiles)


def _combine(plan, x2, route, lbase_f, gfin, y_sorted):
    t, dm = x2.shape
    tm = TM_MIX * MOE_TILES
    s_rows = _local_rows(TM_MIX)
    im = lambda i, *_: (i, 0)
    return pl.pallas_call(
        _combine_kernel,
        grid_spec=pltpu.PrefetchScalarGridSpec(
            num_scalar_prefetch=4,
            grid=(t // tm,),
            in_specs=[pl.BlockSpec((tm, dm), im), pl.BlockSpec((tm, LANES), im),
                      pl.BlockSpec((MOE_TILES, 1, LANES), lambda i, *_: (i, 0, 0)),
                      pl.BlockSpec((1, dm), lambda i, *_: (0, 0)),
                      pl.BlockSpec(memory_space=pl.ANY)],
            out_specs=pl.BlockSpec((tm, dm), im),
            scratch_shapes=[pltpu.VMEM((2, MOE_TILES, s_rows, dm), BF16), pltpu.SMEM((2 * MOE_TILES,), jnp.int32),
                            pltpu.SemaphoreType.DMA((2 * MOE_TILES,))],
        ),
        out_shape=jax.ShapeDtypeStruct((t, dm), F32),
        compiler_params=pltpu.CompilerParams(dimension_semantics=("arbitrary",),
                                             vmem_limit_bytes=VMEM_LIMIT),
        name="combine_norm",
    )(*_plan_specs(plan), x2, route, lbase_f, gfin, y_sorted)


def _layer(x, p, s5_ops, gfin):
    b, l, dm = x.shape
    x2d = x.reshape(b * l, dm)
    mgm, us5 = _inproj_gmlp(x2d, p['gmix'], p['win'], p['lng'], p['lnb'], p['ws'], p['bs'], p['gout_gm'])
    n_seg = SUBLANES // b
    *lag_factors, w1, w2, sc = s5_ops[(l // (S5_LC * n_seg))]
    xg = _s5_inproj(x, p['gmix'], p['win_s5'], n_seg)
    yg = _s5_scan(xg, lag_factors, w1, w2, sc, n_seg)
    ys = _s5_to_tokens(yg, b, l, n_seg)
    x2, t_bf, route, counts = _mix_route(ys, us5, mgm, x2d, p['d'], p['gluw'], p['glub'], p['gout_s5'],
                                         p['wout'], p['gffn'], p['rw_hl'], p['rb'])
    plan, lbase_f, tiles, n_sorted = _segment_plan(counts, b * l, TM_EXPERT)
    x_sorted = _sort_rows(plan, tiles[1], t_bf, route, lbase_f, n_sorted)
    y_sorted = _experts(*tiles, x_sorted, p['w_gate'], p['w_up'], p['w_down'])
    out = _combine(plan, x2, route, lbase_f, gfin, y_sorted)
    return out.reshape(b, l, dm)


def kernel(x_prompt, x_sample, norm_mix_g, w_in, gm_ln_g, gm_ln_b, gm_ws, gm_bs, s5_lam_re_fwd, s5_lam_im_fwd, s5_log_step_fwd, s5_b_re_fwd, s5_b_im_fwd, s5_c_re_fwd, s5_c_im_fwd, s5_lam_re_bwd, s5_lam_im_bwd, s5_log_step_bwd, s5_b_re_bwd, s5_b_im_bwd, s5_c_re_bwd, s5_c_im_bwd, s5_d, s5_glu_w, s5_glu_b, out_norm_gm, out_norm_s5, w_out, norm_ffn_g, r1_w, r1_b, r2_w, r2_b, e_w_gate, e_w_up, e_w_down, norm_final_g):
    depth = w_in.shape[0]
    gfin = norm_final_g.reshape(1, -1).astype(F32)
    xs = [x_prompt, x_sample]
    for li in range(depth):
        row = lambda a: a[li].reshape(1, -1).astype(F32)
        dm = w_in.shape[1]
        gw = gm_ln_g.shape[1]
        hd_dim = gw // GM_HEADS
        rw = jnp.concatenate([r1_w[li], r2_w[li].transpose(1, 0, 2).reshape(dm, N_EXPERTS)], axis=1).astype(F32)
        rw = jnp.pad(rw, ((0, 0), (0, LANES - rw.shape[1])))
        rwh = rw.astype(BF16)
        rwl = (rw - rwh.astype(F32)).astype(BF16)
        rb = jnp.concatenate([r1_b[li], r2_b[li].reshape(-1)]).astype(F32)
        rb = jnp.pad(rb, (0, LANES - rb.shape[0])).reshape(1, LANES)
        p = dict(
            gmix=row(norm_mix_g), win=w_in[li].astype(BF16), win_s5=w_in[li][:, 2 * gw:].astype(BF16),
            lng=row(gm_ln_g), lnb=row(gm_ln_b),
            ws=gm_ws[li].astype(BF16),
            bs=jnp.broadcast_to(gm_bs[li].astype(F32)[:, :, None], (GM_HEADS, CHUNK, hd_dim)),
            gout_gm=row(out_norm_gm), d=row(s5_d), gluw=s5_glu_w[li].astype(BF16), glub=row(s5_glu_b),
            gout_s5=row(out_norm_s5), wout=w_out[li].astype(BF16), gffn=row(norm_ffn_g),
            rw_hl=jnp.concatenate([rwh, rwl], axis=1), rb=rb,
            w_gate=e_w_gate[li], w_up=e_w_up[li], w_down=e_w_down[li],
        )
        fwd = (s5_lam_re_fwd[li], s5_lam_im_fwd[li], s5_log_step_fwd[li], s5_b_re_fwd[li], s5_b_im_fwd[li],
               s5_c_re_fwd[li], s5_c_im_fwd[li])
        bwd = (s5_lam_re_bwd[li], s5_lam_im_bwd[li], s5_log_step_bwd[li], s5_b_re_bwd[li], s5_b_im_bwd[li],
               s5_c_re_bwd[li], s5_c_im_bwd[li])
        s5_ops = {}
        for x in xs:
            seg_steps = x.shape[1] // (S5_LC * (SUBLANES // x.shape[0]))
            if seg_steps not in s5_ops:
                s5_ops[seg_steps] = _s5_operator(fwd, bwd, S5_LC, seg_steps)
        last = li == depth - 1
        assert last, "depth > 1 needs an un-normalised layer output"
        xs = [_layer(x, p, s5_ops, gfin) for x in xs]
    return tuple(xs)
```

```python
import functools
import math

import jax
import jax.numpy as jnp
from jax import lax
from jax.experimental import pallas as pl
from jax.experimental.pallas import tpu as pltpu

F32 = jnp.float32
BF16 = jnp.bfloat16

EPS = 1e-6
LAMBDA_RE_MAX = -1e-4
GM_HEADS = 4
CHUNK = 128
S5_GROUP = 16
S5_STATE = 64
N_COARSE = 4
N_FINE = 8
N_EXPERTS = N_COARSE * N_FINE

LANES = 128
SUBLANES = 8
S5_LC = 16
VMEM_LIMIT = 56 * 1024 * 1024

TM_PROJ = 1024
TM_MIX = 512
MIX_TILES = 2
KB = 256
TM_EXPERT = 1024
SEG_ALIGN = 16


def _gelu(x):
    c = math.sqrt(2.0 / math.pi)
    half = 0.5 * x
    return half + half * jnp.tanh(x * (c + (c * 0.044715) * (x * x)))


def _rms(x, g):
    ms = jnp.mean(x * x, axis=-1, keepdims=True)
    return x * lax.rsqrt(ms + EPS) * g


def _dot(a, b):
    return jnp.dot(a, b, preferred_element_type=F32)


def _lockstep(tiles):
    while tiles:
        tiles = [t for t in tiles if next(t, "done") != "done"]


def _inproj_gmlp_kernel(x_ref, gmix_ref, win_ref, lng_ref, lnb_ref, ws_ref, bs_ref, gout_ref,
                        mgm_ref, us5_ref, y_scr):
    tm = x_ref.shape[0]
    gw = mgm_ref.shape[1]
    hd_dim = gw // GM_HEADS
    n_chunks = tm // CHUNK
    h = _rms(x_ref[...], gmix_ref[...]).astype(BF16)
    proj = _dot(h, win_ref[...])
    us5_ref[...] = proj[:, 2 * gw:].astype(BF16)
    for hd in range(GM_HEADS):
        lo = hd * hd_dim
        vh = _gelu(proj[:, gw + lo:gw + lo + hd_dim])
        mu = jnp.mean(vh, axis=-1, keepdims=True)
        xc = vh - mu
        var = jnp.mean(xc * xc, axis=-1, keepdims=True)
        vn = (xc * lax.rsqrt(var + EPS) * lng_ref[:, lo:lo + hd_dim]
              + lnb_ref[:, lo:lo + hd_dim]).astype(BF16)
        rhs = jnp.concatenate([vn[c * CHUNK:(c + 1) * CHUNK] for c in range(n_chunks)], axis=1)
        s = _dot(ws_ref[hd], rhs)
        for c in range(n_chunks):
            sc = s[:, c * hd_dim:(c + 1) * hd_dim] + bs_ref[hd]
            u = _gelu(proj[c * CHUNK:(c + 1) * CHUNK, lo:lo + hd_dim])
            y_scr[c * CHUNK:(c + 1) * CHUNK, lo:lo + hd_dim] = u * sc
    mgm_ref[...] = _rms(y_scr[...], gout_ref[...]).astype(BF16)


def _inproj_gmlp(x2d, gmix, win_bf, lng, lnb, ws_bf, bs_b, gout):
    t, d = x2d.shape
    d_in = win_bf.shape[1]
    gw = lng.shape[1]
    s5w = d_in - 2 * gw
    tm = TM_PROJ
    const = lambda *shape: pl.BlockSpec(shape, lambda i: (0,) * len(shape))
    return pl.pallas_call(
        _inproj_gmlp_kernel,
        grid=(t // tm,),
        in_specs=[
            pl.BlockSpec((tm, d), lambda i: (i, 0)),
            const(1, d), const(d, d_in), const(1, gw), const(1, gw),
            const(GM_HEADS, CHUNK, CHUNK), const(GM_HEADS, CHUNK, gw // GM_HEADS), const(1, gw),
        ],
        out_specs=[pl.BlockSpec((tm, gw), lambda i: (i, 0)),
                   pl.BlockSpec((tm, s5w), lambda i: (i, 0))],
        out_shape=[jax.ShapeDtypeStruct((t, gw), BF16),
                   jax.ShapeDtypeStruct((t, s5w), BF16)],
        scratch_shapes=[pltpu.VMEM((tm, gw), F32)],
        compiler_params=pltpu.CompilerParams(dimension_semantics=("parallel",),
                                             vmem_limit_bytes=VMEM_LIMIT),
        name="inproj_gmlp",
    )(x2d, gmix, win_bf, lng, lnb, ws_bf, bs_b, gout)


def _s5_consts(lam_re, lam_im, log_step, b_re, b_im, c_re, c_im, lc):
    lr = jnp.minimum(lam_re.astype(F32), LAMBDA_RE_MAX)
    li = lam_im.astype(F32)
    step = jnp.exp(log_step.astype(F32))[:, None]
    dr, di = lr * step, li * step
    ar, ai = _cexp(dr, di)
    nr, ni = ar - 1.0, ai
    den = lr * lr + li * li
    qr, qi = (nr * lr + ni * li) / den, (ni * lr - nr * li) / den
    br, bi = b_re.astype(F32), b_im.astype(F32)
    bbr = qr[..., None] * br - qi[..., None] * bi
    bbi = qr[..., None] * bi + qi[..., None] * br
    k = jnp.arange(lc + 1, dtype=F32)[:, None, None]
    pwr, pwi = _cexp(k * dr[None], k * di[None])
    return (dr, di), (pwr, pwi), (bbr, bbi), (c_re.astype(F32), c_im.astype(F32))


def _cexp(zr, zi):
    m = jnp.exp(zr)
    return m * jnp.cos(zi), m * jnp.sin(zi)


def _s5_operator(fwd, bwd, lc, seg_steps):
    consts = [_s5_consts(*fwd, lc), _s5_consts(*bwd, lc)]
    g, p, h = consts[0][2][0].shape
    lags, w1_parts, w2_parts, sc_rows, seg_rows = [], [], [], [], []
    for direction, (ld, pw, bb, c) in enumerate(consts):
        (dr, di), (pwr, pwi), (bbr, bbi), (cr, ci) = ld, pw, bb, c
        crt, cit = cr.transpose(0, 2, 1), ci.transpose(0, 2, 1)
        pwrt, pwit = pwr.transpose(1, 2, 0), pwi.transpose(1, 2, 0)
        cpr = crt[:, :, None, :] * pwrt[:, :, :, None] - cit[:, :, None, :] * pwit[:, :, :, None]
        cpi = crt[:, :, None, :] * pwit[:, :, :, None] + cit[:, :, None, :] * pwrt[:, :, :, None]
        ck = jnp.concatenate([cpr[:, :, :lc], cpi[:, :, :lc]], axis=1)
        if direction == 1:
            ck = jnp.flip(ck, 2)
        lags += [jnp.concatenate([bbr.transpose(0, 2, 1), -bbi.transpose(0, 2, 1)], axis=-1),
                 ck.reshape(g, 2 * p, lc * h)]
        er, ei = pwrt[:, :, :lc].transpose(0, 2, 1), pwit[:, :, :lc].transpose(0, 2, 1)
        if direction == 0:
            er, ei = jnp.flip(er, 1), jnp.flip(ei, 1)
        bbrt, bbit = bbr.transpose(0, 2, 1), bbi.transpose(0, 2, 1)
        e1 = jnp.concatenate([er, er], -1)[:, :, None, :]
        e2 = jnp.concatenate([-ei, ei], -1)[:, :, None, :]
        b_ri = jnp.concatenate([bbrt, bbit], -1)[:, None]
        b_ir = jnp.concatenate([bbit, bbrt], -1)[:, None]
        w1_parts += [e1 * b_ri + e2 * b_ir, e1 * b_ir - e2 * b_ri]
        fr, fi = cpr[:, :, 1:lc + 1], cpi[:, :, 1:lc + 1]
        if direction == 1:
            fr, fi = jnp.flip(fr, 2), jnp.flip(fi, 2)
        w2_parts += [fr, -fi]

        def mult(zr, zi):
            return [jnp.concatenate([zr, zr], -1), jnp.concatenate([-zi, zi], -1)]

        sc_rows += mult(*_cexp(lc * dr, lc * di))
        seg_rows += mult(*_cexp((lc * seg_steps) * dr, (lc * seg_steps) * di))
    w1 = jnp.concatenate(w1_parts, axis=-1).reshape(g, lc * h, 8 * p)
    w2 = jnp.concatenate(w2_parts, axis=1).reshape(g, 4 * p, lc * h)
    sc = jnp.stack(sc_rows + seg_rows, axis=1)
    return tuple(lags) + (w1.astype(BF16), w2.astype(BF16), sc.astype(F32))


S5_GPS = 2


def _s5_kernel(x_ref, bbf_ref, cpf_ref, bbb_ref, cpb_ref, w1_ref, w2_ref, sc_ref, y_ref,
               loc_scr, sin_scr, m_scr, *, n_seg):
    gps, rows, kw = x_ref.shape
    steps = rows // SUBLANES
    sw = sc_ref.shape[2]

    for gi in range(gps):
        kf = jnp.dot(bbf_ref[gi], cpf_ref[gi], precision=lax.Precision.HIGHEST, preferred_element_type=F32)
        kb = jnp.dot(bbb_ref[gi], cpb_ref[gi], precision=lax.Precision.HIGHEST, preferred_element_type=F32)
        hch = kf.shape[0]
        lc = kw // hch
        lane = lax.broadcasted_iota(jnp.int32, kf.shape, 1)
        for s in range(lc):
            f = kf if s == 0 else jnp.where(lane >= s * hch, pltpu.roll(kf, s * hch, 1), 0.0)
            left = (lc - 1 - s) * hch
            b = kb if left == 0 else pltpu.roll(kb, kw - left, 1)
            m_scr[gi, s * hch:(s + 1) * hch, :] = (f + jnp.where(lane < (s + 1) * hch, b, 0.0)).astype(BF16)
        loc_scr[gi] = _dot(x_ref[gi], w1_ref[gi])

    def bc(gi, i):
        return jnp.broadcast_to(sc_ref[gi, i:i + 1, :], (SUBLANES, sw))

    mult = [[bc(gi, i) for i in range(8)] for gi in range(gps)]

    def step(gi, s, state):
        f, fs, b, bs = state
        a1f, a2f, a1b, a2b = mult[gi][:4]
        rf = pl.multiple_of(s * SUBLANES, SUBLANES)
        rb = pl.multiple_of((steps - 1 - s) * SUBLANES, SUBLANES)
        lf = loc_scr[gi, pl.ds(rf, SUBLANES), 0:sw]
        lfs = loc_scr[gi, pl.ds(rf, SUBLANES), sw:2 * sw]
        lb = loc_scr[gi, pl.ds(rb, SUBLANES), 2 * sw:3 * sw]
        lbs = loc_scr[gi, pl.ds(rb, SUBLANES), 3 * sw:4 * sw]
        return (a1f * f + a2f * fs + lf, a1f * fs - a2f * f + lfs,
                a1b * b + a2b * bs + lb, a1b * bs - a2b * b + lbs)

    zero = jnp.zeros((SUBLANES, sw), F32)

    def pass1(s, carry):
        return tuple(step(gi, s, carry[gi]) for gi in range(gps))

    ends = lax.fori_loop(0, steps, pass1, tuple((zero,) * 4 for _ in range(gps)), unroll=4)

    seg = lax.broadcasted_iota(jnp.int32, (SUBLANES, sw), 0) % n_seg
    enter = []
    for gi in range(gps):
        f_end, fs_end, b_end, bs_end = ends[gi]
        p1f, p2f, p1b, p2b = mult[gi][4:]
        cf, cfs, cb, cbs = zero, zero, zero, zero
        for _ in range(n_seg - 1):
            ef = f_end + p1f * cf + p2f * cfs
            efs = fs_end + p1f * cfs - p2f * cf
            eb = b_end + p1b * cb + p2b * cbs
            ebs = bs_end + p1b * cbs - p2b * cb
            cf = jnp.where(seg >= 1, pltpu.roll(ef, 1, 0), 0.0)
            cfs = jnp.where(seg >= 1, pltpu.roll(efs, 1, 0), 0.0)
            cb = jnp.where(seg <= n_seg - 2, pltpu.roll(eb, SUBLANES - 1, 0), 0.0)
            cbs = jnp.where(seg <= n_seg - 2, pltpu.roll(ebs, SUBLANES - 1, 0), 0.0)
        enter.append((cf, cfs, cb, cbs))

    def pass2(s, carry):
        rf = pl.multiple_of(s * SUBLANES, SUBLANES)
        rb = pl.multiple_of((steps - 1 - s) * SUBLANES, SUBLANES)
        for gi in range(gps):
            sin_scr[gi, pl.ds(rf, SUBLANES), 0:sw] = carry[gi][0]
            sin_scr[gi, pl.ds(rb, SUBLANES), sw:2 * sw] = carry[gi][2]
        return tuple(step(gi, s, carry[gi]) for gi in range(gps))

    lax.fori_loop(0, steps, pass2, tuple(enter), unroll=4)

    for gi in range(gps):
        y_ref[gi] = _dot(x_ref[gi], m_scr[gi]) + _dot(sin_scr[gi].astype(BF16), w2_ref[gi])


def _s5_scan(xg, lag_factors, w1, w2, sc, n_seg):
    g, rows, kw = xg.shape
    sw = sc.shape[2]
    gps = S5_GPS
    blk = lambda a: pl.BlockSpec((gps,) + a.shape[1:], lambda i: (i, 0, 0))
    return pl.pallas_call(
        functools.partial(_s5_kernel, n_seg=n_seg),
        grid=(g // gps,),
        in_specs=[blk(xg)] + [blk(a) for a in lag_factors] + [blk(w1), blk(w2), blk(sc)],
        out_specs=pl.BlockSpec((gps, rows, kw), lambda i: (i, 0, 0)),
        out_shape=jax.ShapeDtypeStruct((g, rows, kw), F32),
        scratch_shapes=[pltpu.VMEM((gps, rows, 4 * sw), F32), pltpu.VMEM((gps, rows, 2 * sw), F32),
                        pltpu.VMEM((gps, kw, kw), BF16)],
        compiler_params=pltpu.CompilerParams(dimension_semantics=("parallel",),
                                             vmem_limit_bytes=VMEM_LIMIT),
        name="s5_scan",
    )(xg, *lag_factors, w1, w2, sc)


S5_NM = 16


def _block_transpose8(groups, width):
    lane = lax.broadcasted_iota(jnp.int32, groups[0][0].shape, 1)
    for d in (4, 2, 1):
        w = width * d
        hi = ((lane // w) % 2) == 1
        nxt = []
        for v in groups:
            out = list(v)
            for i0 in range(8):
                if i0 & d:
                    continue
                i1 = i0 + d
                out[i0] = jnp.where(hi, pltpu.roll(v[i1], w, 1), v[i0])
                out[i1] = jnp.where(hi, v[i1], pltpu.roll(v[i0], 8 * width - w, 1))
            nxt.append(out)
        groups = nxt
    return groups


def _tile_copies(hbm4, tile, buf, slot, sem, nm, to_hbm):
    copies = []
    for c in range(SUBLANES):
        for j in range(S5_LC):
            h = hbm4.at[c, pl.ds(tile * nm, nm), pl.ds(j, 1), :]
            v = buf.at[slot, j, :, pl.ds(c, 1), :]
            copies.append(pltpu.make_async_copy(v, h, sem.at[slot]) if to_hbm
                          else pltpu.make_async_copy(h, v, sem.at[slot]))
    return copies


def _s5_inproj_kernel(x4_ref, gmix_ref, w_ref, xg_ref, xs, sem, *, nm):
    i = pl.program_id(0)
    n = pl.num_programs(0)
    slot = i % 2
    dm = x4_ref.shape[3]

    @pl.when(i == 0)
    def _():
        for cp in _tile_copies(x4_ref, 0, xs, 0, sem, nm, False):
            cp.start()

    @pl.when(i + 1 < n)
    def _():
        for cp in _tile_copies(x4_ref, i + 1, xs, 1 - slot, sem, nm, False):
            cp.start()

    pltpu.make_async_copy(xs.at[slot], xs.at[slot], sem.at[slot]).wait()
    rows = nm * SUBLANES

    def half(a):
        x = xs[slot, 8 * a:8 * a + 8].reshape(8 * rows, dm)
        h = _rms(x, gmix_ref[...]).astype(BF16)
        yield
        z = _dot(h, w_ref[...])
        yield
        n_oct = z.shape[1] // LANES
        blocks = [[z[j8 * rows:(j8 + 1) * rows, q * LANES:(q + 1) * LANES] for j8 in range(8)] for q in range(n_oct)]
        for q, out in enumerate(_block_transpose8(blocks, S5_GROUP)):
            for g8, b in enumerate(out):
                xg_ref[8 * q + g8, :, a * LANES:(a + 1) * LANES] = b.astype(BF16)

    _lockstep([half(a) for a in range(S5_LC // 8)])


def _s5_inproj(x, gmix, w_s5_bf, n_seg):
    b, l, dm = x.shape
    steps = l // (S5_LC * n_seg)
    nm = S5_NM
    s5w = w_s5_bf.shape[1]
    g = s5w // S5_GROUP
    x4 = x.reshape(b * n_seg, steps, S5_LC, dm)
    return pl.pallas_call(
        functools.partial(_s5_inproj_kernel, nm=nm),
        grid=(steps // nm,),
        in_specs=[pl.BlockSpec(memory_space=pl.ANY),
                  pl.BlockSpec((1, dm), lambda i: (0, 0)),
                  pl.BlockSpec((dm, s5w), lambda i: (0, 0))],
        out_specs=pl.BlockSpec((g, nm * SUBLANES, S5_LC * S5_GROUP), lambda i: (0, i, 0)),
        out_shape=jax.ShapeDtypeStruct((g, steps * SUBLANES, S5_LC * S5_GROUP), BF16),
        scratch_shapes=[pltpu.VMEM((2, S5_LC, nm, SUBLANES, dm), F32), pltpu.SemaphoreType.DMA((2,))],
        compiler_params=pltpu.CompilerParams(dimension_semantics=("arbitrary",),
                                             vmem_limit_bytes=VMEM_LIMIT),
        name="s5_inproj",
    )(x4, gmix, w_s5_bf)


def _s5_to_tokens_kernel(yg_ref, ys4_ref, zs, sem, *, nm):
    i = pl.program_id(0)
    n = pl.num_programs(0)
    slot = i % 2
    rows = nm * SUBLANES

    def wait(s):
        pltpu.make_async_copy(zs.at[s], zs.at[s], sem.at[s]).wait()

    @pl.when(i >= 2)
    def _():
        wait(slot)

    n_oct = yg_ref.shape[0] // 8
    combos = [(q, a) for q in range(n_oct) for a in range(S5_LC // 8)]
    blocks = [[yg_ref[8 * q + g8, :, a * LANES:(a + 1) * LANES] for g8 in range(8)] for q, a in combos]
    for (q, a), out in zip(combos, _block_transpose8(blocks, S5_GROUP)):
        for j8, b in enumerate(out):
            zs[slot, 8 * a + j8, :, :, q * LANES:(q + 1) * LANES] = b.reshape(nm, SUBLANES, LANES)
    for cp in _tile_copies(ys4_ref, i, zs, slot, sem, nm, True):
        cp.start()

    @pl.when(i == n - 1)
    def _():
        wait(1 - slot)
        wait(slot)


def _s5_to_tokens(yg, b, l, n_seg):
    g, rows_total, kw = yg.shape
    steps = rows_total // SUBLANES
    nm = S5_NM
    s5w = g * S5_GROUP
    assert steps // nm >= 2
    ys4 = pl.pallas_call(
        functools.partial(_s5_to_tokens_kernel, nm=nm),
        grid=(steps // nm,),
        in_specs=[pl.BlockSpec((g, nm * SUBLANES, kw), lambda i: (0, i, 0))],
        out_specs=pl.BlockSpec(memory_space=pl.ANY),
        out_shape=jax.ShapeDtypeStruct((b * n_seg, steps, S5_LC, s5w), F32),
        scratch_shapes=[pltpu.VMEM((2, S5_LC, nm, SUBLANES, s5w), F32), pltpu.SemaphoreType.DMA((2,))],
        compiler_params=pltpu.CompilerParams(dimension_semantics=("arbitrary",),
                                             vmem_limit_bytes=VMEM_LIMIT),
        name="s5_to_tokens",
    )(yg)
    return ys4.reshape(b * l, s5w)


R_E1, R_E2, R_W1, R_W2, R_RANK1, R_RANK2 = range(6)


def _mix_route_kernel(ys_ref, us5_ref, mgm_ref, x_ref, d_ref, gluw_ref, glub_ref, gs5_ref,
                      wout_ref, gffn_ref, rw_ref, rb_ref, tri_ref,
                      x2_ref, t_ref, route_ref, cnt_ref):
    tiles = []
    for k in range(x_ref.shape[0] // TM_MIX):
        rows = slice(k * TM_MIX, (k + 1) * TM_MIX)
        tiles.append(_mix_route_tile(ys_ref.at[rows, :], us5_ref.at[rows, :], mgm_ref.at[rows, :], x_ref.at[rows, :],
                                     d_ref, gluw_ref, glub_ref, gs5_ref, wout_ref, gffn_ref, rw_ref,
                                     rb_ref, tri_ref, x2_ref.at[rows, :], t_ref.at[rows, :], route_ref.at[rows, :],
                                     cnt_ref.at[k]))
    _lockstep(tiles)


def _mix_route_tile(ys_ref, us5_ref, mgm_ref, x_ref, d_ref, gluw_ref, glub_ref, gs5_ref,
                    wout_ref, gffn_ref, rw_ref, rb_ref, tri_ref,
                    x2_ref, t_ref, route_ref, cnt_ref):
    gw = mgm_ref.shape[1]
    y = ys_ref[...] + d_ref[...] * us5_ref[...].astype(F32)
    g = _gelu(y)
    yield
    gate = _dot(g.astype(BF16), gluw_ref[...])
    yield
    z = g * jax.nn.sigmoid(gate + glub_ref[...])
    ms5 = _rms(z, gs5_ref[...]).astype(BF16)
    yield
    mix = _dot(mgm_ref[...], wout_ref[:gw, :]) + _dot(ms5, wout_ref[gw:, :])
    yield
    x2 = x_ref[...] + mix
    x2_ref[...] = x2
    t = _rms(x2, gffn_ref[...])
    t_hi = t.astype(BF16)
    t_ref[...] = t_hi
    t_lo = (t - t_hi.astype(F32)).astype(BF16)
    yield
    hl = _dot(t_hi, rw_ref[...])
    logits = (hl[:, :LANES] + hl[:, LANES:] + _dot(t_lo, rw_ref[:, :LANES])
              + rb_ref[...])
    yield
    lane = lax.broadcasted_iota(jnp.int32, logits.shape, 1).astype(F32)
    neg = jnp.float32(-jnp.inf)

    def first_max(mask):
        vals = jnp.where(mask, logits, neg)
        mx = jnp.max(vals, axis=-1, keepdims=True)
        idx = jnp.min(jnp.where(mask & (vals == mx), lane, float(LANES)), axis=-1, keepdims=True)
        return mx, idx

    coarse = lane < N_COARSE
    m1, grp = first_max(coarse)
    p_grp = 1.0 / jnp.sum(jnp.where(coarse, jnp.exp(logits - m1), 0.0), axis=-1, keepdims=True)
    lo = N_COARSE + grp * N_FINE
    fine = (lane >= lo) & (lane < lo + N_FINE)
    v1, i1 = first_max(fine)
    v2, i2 = first_max(fine & (lane != i1))
    e21 = jnp.exp(v2 - v1)
    w1 = p_grp / (1.0 + e21)
    w2 = p_grp * e21 / (1.0 + e21)
    e1 = i1 - N_COARSE
    e2 = i2 - N_COARSE
    hit1 = lane == e1
    hit2 = lane == e2
    onehot = jnp.where(hit1 | hit2, 1.0, 0.0)
    before = _dot(tri_ref[...], onehot.astype(BF16))
    rank1 = jnp.sum(jnp.where(hit1, before, 0.0), axis=-1, keepdims=True)
    rank2 = jnp.sum(jnp.where(hit2, before, 0.0), axis=-1, keepdims=True)
    tm = onehot.shape[0]
    cnt_ref[...] = before[tm - 1:tm, :] + onehot[tm - 1:tm, :]
    rec = jnp.zeros_like(logits)
    for slot, val in ((R_E1, e1), (R_E2, e2), (R_W1, w1), (R_W2, w2),
                      (R_RANK1, rank1), (R_RANK2, rank2)):
        rec = jnp.where(lane == slot, val, rec)
    route_ref[...] = rec


def _mix_route(ys, us5, mgm, x2d, d, gluw_bf, glub, gs5, wout_bf, gffn, rw_hl, rb):
    t, dm = x2d.shape
    gw = mgm.shape[1]
    s5w = us5.shape[1]
    tm = TM_MIX * MIX_TILES
    tri = jnp.tril(jnp.ones((TM_MIX, TM_MIX), F32), -1).astype(BF16)
    const = lambda *shape: pl.BlockSpec(shape, lambda i: (0,) * len(shape))
    tile = lambda w: pl.BlockSpec((tm, w), lambda i: (i, 0))
    return pl.pallas_call(
        _mix_route_kernel,
        grid=(t // tm,),
        in_specs=[tile(s5w), tile(s5w), tile(gw), tile(dm),
                  const(1, s5w), const(s5w, s5w), const(1, s5w), const(1, s5w),
                  const(gw + s5w, dm), const(1, dm), const(dm, 2 * LANES), const(1, LANES),
                  const(TM_MIX, TM_MIX)],
        out_specs=[tile(dm), tile(dm), tile(LANES), pl.BlockSpec((MIX_TILES, 1, LANES), lambda i: (i, 0, 0))],
        out_shape=[jax.ShapeDtypeStruct((t, dm), F32),
                   jax.ShapeDtypeStruct((t, dm), BF16),
                   jax.ShapeDtypeStruct((t, LANES), F32),
                   jax.ShapeDtypeStruct((t // TM_MIX, 1, LANES), F32)],
        compiler_params=pltpu.CompilerParams(dimension_semantics=("parallel",),
                                             vmem_limit_bytes=VMEM_LIMIT),
        name="mix_route",
    )(ys, us5, mgm, x2d, d, gluw_bf, glub, gs5, wout_bf, gffn, rw_hl, rb, tri)


def _local_rows(tm):
    worst = 2 * tm + N_EXPERTS * (SEG_ALIGN - 1)
    return -(-worst // LANES) * LANES


def _segment_plan(cnt, t, tm_expert):
    c = cnt[:, 0, :N_EXPERTS].astype(jnp.int32)
    n_tok_tiles = c.shape[0]
    al = (c + SEG_ALIGN - 1) // SEG_ALIGN * SEG_ALIGN
    lbase = jnp.cumsum(al, axis=1) - al
    tot = jnp.sum(al, axis=0)
    tot_pad = (tot + tm_expert - 1) // tm_expert * tm_expert
    gbase = jnp.cumsum(tot_pad) - tot_pad
    gpos = gbase[None, :] + jnp.cumsum(al, axis=0) - al
    n_tiles_max = -(-(2 * t + n_tok_tiles * N_EXPERTS * (SEG_ALIGN - 1)) // tm_expert) + N_EXPERTS
    tile_end = jnp.cumsum(tot_pad // tm_expert)
    n_tiles = tile_end[-1:].astype(jnp.int32)
    tile_idx = jnp.arange(n_tiles_max, dtype=jnp.int32)
    tile_expert = jnp.sum((tile_idx[:, None] >= tile_end[None, :]).astype(jnp.int32), axis=1)
    last = jnp.sum((n_tiles - 1 >= tile_end).astype(jnp.int32))
    tile_expert = jnp.where(tile_idx < n_tiles, tile_expert, last).astype(jnp.int32)
    ids = jnp.arange(N_EXPERTS, dtype=jnp.int32)
    later_used = (ids[None, :] > ids[:, None]) & (tot_pad[None, :] > 0)
    next_expert = jnp.min(jnp.where(later_used, ids[None, :], N_EXPERTS), axis=1)
    next_expert = jnp.where(next_expert == N_EXPERTS, ids, next_expert).astype(jnp.int32)
    lbase_f = jnp.pad(lbase.astype(F32), ((0, 0), (0, LANES - N_EXPERTS)))[:, None, :]
    flat = lambda a: a.reshape(-1).astype(jnp.int32)
    nch = al // SEG_ALIGN
    cum = jnp.cumsum(nch, axis=1)
    q = jnp.arange(_local_rows(TM_MIX) // SEG_ALIGN, dtype=jnp.int32)[None, :, None]
    seg_of_q = jnp.sum((q >= cum[:, None, :]).astype(jnp.int32), axis=2)
    in_seg = seg_of_q[:, :, None] == jnp.arange(N_EXPERTS, dtype=jnp.int32)[None, None, :]
    pick = lambda a: jnp.sum(jnp.where(in_seg, a[:, None, :], 0), axis=2)
    dst = pick(gpos) + (q[:, :, 0] - pick(cum - nch)) * SEG_ALIGN
    plan = dict(dst=flat(dst), n_chunks=flat(cum[:, -1]),
                tail_pos=flat(gbase + tot), tail_n=flat((tot_pad - tot) // SEG_ALIGN))
    return plan, lbase_f, (tile_expert, n_tiles, next_expert), n_tiles_max * tm_expert


def _local_positions(route, lbase):
    lane = lax.broadcasted_iota(jnp.int32, route.shape, 1).astype(F32)
    out = []
    for e_lane, r_lane in ((R_E1, R_RANK1), (R_E2, R_RANK2)):
        e = route[:, e_lane:e_lane + 1]
        base = jnp.sum(jnp.where(lane == e, lbase, 0.0), axis=-1, keepdims=True)
        out.append(base + route[:, r_lane:r_lane + 1])
    return out


WAIT_GROUP = 8
ISSUE_GROUP = 4


def _segment_copies(i, dst_ref, nq_ref, local, glob, sem, to_global):
    per_tile = local.shape[0] // SEG_ALIGN
    n = nq_ref[i]

    def start(q):
        lo = local.at[pl.ds(pl.multiple_of(q * SEG_ALIGN, SEG_ALIGN), SEG_ALIGN)]
        gl = glob.at[pl.ds(pl.multiple_of(dst_ref[i * per_tile + q], SEG_ALIGN), SEG_ALIGN)]
        (pltpu.make_async_copy(lo, gl, sem) if to_global else pltpu.make_async_copy(gl, lo, sem)).start()

    def group(k, carry):
        for u in range(ISSUE_GROUP):
            start(k * ISSUE_GROUP + u)
        return carry

    def single(q, carry):
        start(q)
        return carry

    full = n // ISSUE_GROUP
    lax.fori_loop(0, full, group, 0)
    lax.fori_loop(full * ISSUE_GROUP, n, single, 0)
    return n


def _wait_chunks(n, local, glob, sem):
    def wait_rows(rows):
        def one(c, carry):
            pltpu.make_async_copy(local.at[pl.ds(0, rows)], glob.at[pl.ds(0, rows)], sem).wait()
            return carry
        return one

    lax.fori_loop(0, n // WAIT_GROUP, wait_rows(WAIT_GROUP * SEG_ALIGN), 0)
    lax.fori_loop(0, n % WAIT_GROUP, wait_rows(SEG_ALIGN), 0)


def _sort_tile(t_ref, route_ref, lbase, local):
    tm = t_ref.shape[0]
    s_rows = local.shape[0]
    lp1, lp2 = _local_positions(route_ref[...], lbase)
    lane = lax.broadcasted_iota(jnp.int32, (tm, LANES), 1)
    lp_rows = jnp.where(lane == 0, lp1, jnp.where(lane == 1, lp2, -1.0)).T
    row = lax.broadcasted_iota(jnp.int32, (s_rows, tm), 0).astype(F32)
    onehot = jnp.where((row == lp_rows[0:1, :]) | (row == lp_rows[1:2, :]), 1.0, 0.0).astype(BF16)
    yield
    local[...] = _dot(onehot, t_ref[...]).astype(BF16)


def _sort_rows_kernel(dst_ref, nq_ref, tpos_ref, tn_ref, nt_ref, t_ref, route_ref, lbase_ref,
                      xs_ref, local_scr, zero_scr, cnt_scr, sem, zsem):
    i = pl.program_id(0)
    n = pl.num_programs(0)
    slot = i % 2
    per_step = local_scr.shape[1]

    def buf(s, k):
        return local_scr.at[s, k], sem.at[s * per_step + k], s * per_step + k

    def wait_slot(s):
        for k in range(per_step):
            local, sm, c = buf(s, k)
            _wait_chunks(cnt_scr[c], local, xs_ref, sm)

    te = zero_scr.shape[0]
    n_row_tiles = xs_ref.shape[0] // te

    @pl.when(i == 0)
    def _():
        zero_scr[...] = jnp.zeros_like(zero_scr)
        zero_chunk = zero_scr.at[pl.ds(0, SEG_ALIGN)]

        def tail(e, carry):
            def chunk(c, carry):
                dst = xs_ref.at[pl.ds(pl.multiple_of(tpos_ref[e] + c * SEG_ALIGN, SEG_ALIGN), SEG_ALIGN)]
                pltpu.make_async_copy(zero_chunk, dst, zsem).start()
                return carry

            return lax.fori_loop(0, tn_ref[e], chunk, carry)

        lax.fori_loop(0, N_EXPERTS, tail, 0)

        def unused_tile(j, carry):
            pltpu.make_async_copy(zero_scr, xs_ref.at[pl.ds(pl.multiple_of(j * te, te), te)], zsem).start()
            return carry

        lax.fori_loop(nt_ref[0], n_row_tiles, unused_tile, 0)

    @pl.when(i >= 2)
    def _():
        wait_slot(slot)

    tiles = []
    for k in range(per_step):
        rows = slice(k * TM_MIX, (k + 1) * TM_MIX)
        tiles.append(_sort_tile(t_ref.at[rows, :], route_ref.at[rows, :], lbase_ref[k], buf(slot, k)[0]))
    _lockstep(tiles)
    for k in range(per_step):
        local, sm, c = buf(slot, k)
        cnt_scr[c] = _segment_copies(i * per_step + k, dst_ref, nq_ref, local, xs_ref, sm, True)

    @pl.when(i == n - 1)
    def _():
        @pl.when(n >= 2)
        def _():
            wait_slot(1 - slot)

        wait_slot(slot)
        _wait_chunks(lax.fori_loop(0, N_EXPERTS, lambda e, total: total + tn_ref[e], 0), zero_scr, xs_ref, zsem)

        def unused_wait(j, carry):
            pltpu.make_async_copy(zero_scr, xs_ref.at[pl.ds(0, te)], zsem).wait()
            return carry

        lax.fori_loop(nt_ref[0], n_row_tiles, unused_wait, 0)


def _plan_specs(plan):
    keys = ('dst', 'n_chunks', 'tail_pos', 'tail_n')
    return [plan[k] for k in keys]


MOE_TILES = 2


def _sort_rows(plan, n_tiles, t_bf, route, lbase_f, n_sorted):
    t, dm = t_bf.shape
    tm = TM_MIX * MOE_TILES
    s_rows = _local_rows(TM_MIX)
    im = lambda i, *_: (i, 0)
    return pl.pallas_call(
        _sort_rows_kernel,
        grid_spec=pltpu.PrefetchScalarGridSpec(
            num_scalar_prefetch=5,
            grid=(t // tm,),
            in_specs=[pl.BlockSpec((tm, dm), im), pl.BlockSpec((tm, LANES), im),
                      pl.BlockSpec((MOE_TILES, 1, LANES), lambda i, *_: (i, 0, 0))],
            out_specs=pl.BlockSpec(memory_space=pl.ANY),
            scratch_shapes=[pltpu.VMEM((2, MOE_TILES, s_rows, dm), BF16), pltpu.VMEM((TM_EXPERT, dm), BF16),
                            pltpu.SMEM((2 * MOE_TILES,), jnp.int32), pltpu.SemaphoreType.DMA((2 * MOE_TILES,)),
                            pltpu.SemaphoreType.DMA(())],
        ),
        out_shape=jax.ShapeDtypeStruct((n_sorted, dm), BF16),
        compiler_params=pltpu.CompilerParams(dimension_semantics=("arbitrary",),
                                             vmem_limit_bytes=VMEM_LIMIT),
        name="sort_rows",
    )(*_plan_specs(plan), n_tiles, t_bf, route, lbase_f)


def _expert_weight_copies(e, slot, hbm, stage, sem):
    return [pltpu.make_async_copy(h.at[e], s.at[slot], sem.at[slot]) for h, s in zip(hbm, stage)]


def _experts_kernel(te_ref, nt_ref, nxt_ref, xs_ref, wg_ref, wu_ref, wd_ref, ys_ref,
                    sg, su, sd, wg_bf, wu_bf, wd_bf, slot_scr, sem):
    i = pl.program_id(0)
    e = te_ref[i]
    hbm, stage = (wg_ref, wu_ref, wd_ref), (sg, su, sd)

    @pl.when(i == 0)
    def _():
        slot_scr[0] = 0
        for cp in _expert_weight_copies(e, 0, hbm, stage, sem):
            cp.start()

    @pl.when((i == 0) | (e != te_ref[jnp.maximum(i - 1, 0)]))
    def _():
        slot = slot_scr[0]
        for cp in _expert_weight_copies(e, slot, hbm, stage, sem):
            cp.wait()
        wg_bf[...] = sg[slot].astype(BF16)
        wu_bf[...] = su[slot].astype(BF16)
        wd_bf[...] = sd[slot].astype(BF16)
        nxt = nxt_ref[e]

        @pl.when(nxt != e)
        def _():
            for cp in _expert_weight_copies(nxt, 1 - slot, hbm, stage, sem):
                cp.start()

        slot_scr[0] = 1 - slot

    @pl.when(i < nt_ref[0])
    def _():
        x = xs_ref[...]
        hidden = (jax.nn.silu(_dot(x, wg_bf[...])) * _dot(x, wu_bf[...])).astype(BF16)
        ys_ref[...] = _dot(hidden, wd_bf[...]).astype(BF16)


def _experts(tile_expert, n_tiles, next_expert, x_sorted, w_gate, w_up, w_down):
    n_sorted, dm = x_sorted.shape
    de = w_gate.shape[2]
    tm = TM_EXPERT
    return pl.pallas_call(
        _experts_kernel,
        grid_spec=pltpu.PrefetchScalarGridSpec(
            num_scalar_prefetch=3,
            grid=(n_sorted // tm,),
            in_specs=[pl.BlockSpec((tm, dm), lambda i, te, nt, nx: (jnp.minimum(i, nt[0] - 1), 0)),
                      pl.BlockSpec(memory_space=pl.ANY), pl.BlockSpec(memory_space=pl.ANY),
                      pl.BlockSpec(memory_space=pl.ANY)],
            out_specs=pl.BlockSpec((tm, dm), lambda i, te, nt, nx: (jnp.minimum(i, nt[0] - 1), 0)),
            scratch_shapes=[pltpu.VMEM((2, dm, de), F32), pltpu.VMEM((2, dm, de), F32), pltpu.VMEM((2, de, dm), F32),
                            pltpu.VMEM((dm, de), BF16), pltpu.VMEM((dm, de), BF16), pltpu.VMEM((de, dm), BF16),
                            pltpu.SMEM((1,), jnp.int32), pltpu.SemaphoreType.DMA((2,))],
        ),
        out_shape=jax.ShapeDtypeStruct((n_sorted, dm), BF16),
        compiler_params=pltpu.CompilerParams(dimension_semantics=("arbitrary",),
                                             vmem_limit_bytes=VMEM_LIMIT),
        input_output_aliases={3: 0},
        name="experts",
    )(tile_expert, n_tiles, next_expert, x_sorted, w_gate, w_up, w_down)


def _combine_tile(x2_ref, route_ref, lbase, gfin_ref, local, o_ref):
    tm = x2_ref.shape[0]
    s_rows = local.shape[0]
    route = route_ref[...]
    lp1, lp2 = _local_positions(route, lbase)
    w1, w2 = route[:, R_W1:R_W1 + 1], route[:, R_W2:R_W2 + 1]
    col0 = lax.broadcasted_iota(jnp.int32, (tm, KB), 1).astype(F32)
    moe = None
    for k in range(s_rows // KB):
        col = col0 + float(k * KB)
        pick = (jnp.where(col == lp1, w1, 0.0) + jnp.where(col == lp2, w2, 0.0)).astype(BF16)
        part = _dot(pick, local[k * KB:(k + 1) * KB, :])
        moe = part if moe is None else moe + part
    yield
    o_ref[...] = _rms(x2_ref[...] + moe, gfin_ref[...])


def _combine_kernel(dst_ref, nq_ref, tpos_ref, tn_ref, x2_ref, route_ref, lbase_ref, gfin_ref, ys_ref,
                    o_ref, local_scr, cnt_scr, sem):
    del tpos_ref, tn_ref
    i = pl.program_id(0)
    n = pl.num_programs(0)
    slot = i % 2
    per_step = local_scr.shape[1]
    s_rows = local_scr.shape[2]

    def fetch(step, s):
        for k in range(per_step):
            tile = step * per_step + k

            def clear(r, carry, k=k):
                local_scr[s, k, pl.ds(pl.multiple_of(r * SEG_ALIGN, SEG_ALIGN), SEG_ALIGN), :] = jnp.zeros(
                    (SEG_ALIGN, local_scr.shape[3]), BF16)
                return carry

            lax.fori_loop(nq_ref[tile], s_rows // SEG_ALIGN, clear, 0)
            cnt_scr[s * per_step + k] = _segment_copies(tile, dst_ref, nq_ref, local_scr.at[s, k], ys_ref,
                                                        sem.at[s * per_step + k], False)

    @pl.when(i == 0)
    def _():
        fetch(0, 0)

    @pl.when(i + 1 < n)
    def _():
        fetch(i + 1, 1 - slot)

    tiles = []
    for k in range(per_step):
        c = slot * per_step + k
        _wait_chunks(cnt_scr[c], local_scr.at[slot, k], ys_ref, sem.at[c])
        rows = slice(k * TM_MIX, (k + 1) * TM_MIX)
        tiles.append(_combine_tile(x2_ref.at[rows, :], route_ref.at[rows, :], lbase_ref[k], gfin_ref,
                                   local_scr.at[slot, k], o_ref.at[rows, :]))
    _lockstep(tiles)


def _combine(plan, x2, route, lbase_f, gfin, y_sorted):
    t, dm = x2.shape
    tm = TM_MIX * MOE_TILES
    s_rows = _local_rows(TM_MIX)
    im = lambda i, *_: (i, 0)
    return pl.pallas_call(
        _combine_kernel,
        grid_spec=pltpu.PrefetchScalarGridSpec(
            num_scalar_prefetch=4,
            grid=(t // tm,),
            in_specs=[pl.BlockSpec((tm, dm), im), pl.BlockSpec((tm, LANES), im),
                      pl.BlockSpec((MOE_TILES, 1, LANES), lambda i, *_: (i, 0, 0)),
                      pl.BlockSpec((1, dm), lambda i, *_: (0, 0)),
                      pl.BlockSpec(memory_space=pl.ANY)],
            out_specs=pl.BlockSpec((tm, dm), im),
            scratch_shapes=[pltpu.VMEM((2, MOE_TILES, s_rows, dm), BF16), pltpu.SMEM((2 * MOE_TILES,), jnp.int32),
                            pltpu.SemaphoreType.DMA((2 * MOE_TILES,))],
        ),
        out_shape=jax.ShapeDtypeStruct((t, dm), F32),
        compiler_params=pltpu.CompilerParams(dimension_semantics=("arbitrary",),
                                             vmem_limit_bytes=VMEM_LIMIT),
        name="combine_norm",
    )(*_plan_specs(plan), x2, route, lbase_f, gfin, y_sorted)


def _layer(x, p, s5_ops, gfin):
    b, l, dm = x.shape
    x2d = x.reshape(b * l, dm)
    mgm, us5 = _inproj_gmlp(x2d, p['gmix'], p['win'], p['lng'], p['lnb'], p['ws'], p['bs'], p['gout_gm'])
    n_seg = SUBLANES // b
    *lag_factors, w1, w2, sc = s5_ops[(l // (S5_LC * n_seg))]
    xg = _s5_inproj(x, p['gmix'], p['win_s5'], n_seg)
    yg = _s5_scan(xg, lag_factors, w1, w2, sc, n_seg)
    ys = _s5_to_tokens(yg, b, l, n_seg)
    x2, t_bf, route, counts = _mix_route(ys, us5, mgm, x2d, p['d'], p['gluw'], p['glub'], p['gout_s5'],
                                         p['wout'], p['gffn'], p['rw_hl'], p['rb'])
    plan, lbase_f, tiles, n_sorted = _segment_plan(counts, b * l, TM_EXPERT)
    x_sorted = _sort_rows(plan, tiles[1], t_bf, route, lbase_f, n_sorted)
    y_sorted = _experts(*tiles, x_sorted, p['w_gate'], p['w_up'], p['w_down'])
    out = _combine(plan, x2, route, lbase_f, gfin, y_sorted)
    return out.reshape(b, l, dm)


def kernel(x_prompt, x_sample, norm_mix_g, w_in, gm_ln_g, gm_ln_b, gm_ws, gm_bs, s5_lam_re_fwd, s5_lam_im_fwd, s5_log_step_fwd, s5_b_re_fwd, s5_b_im_fwd, s5_c_re_fwd, s5_c_im_fwd, s5_lam_re_bwd, s5_lam_im_bwd, s5_log_step_bwd, s5_b_re_bwd, s5_b_im_bwd, s5_c_re_bwd, s5_c_im_bwd, s5_d, s5_glu_w, s5_glu_b, out_norm_gm, out_norm_s5, w_out, norm_ffn_g, r1_w, r1_b, r2_w, r2_b, e_w_gate, e_w_up, e_w_down, norm_final_g):
    depth = w_in.shape[0]
    gfin = norm_final_g.reshape(1, -1).astype(F32)
    xs = [x_prompt, x_sample]
    for li in range(depth):
        row = lambda a: a[li].reshape(1, -1).astype(F32)
        dm = w_in.shape[1]
        gw = gm_ln_g.shape[1]
        hd_dim = gw // GM_HEADS
        rw = jnp.concatenate([r1_w[li], r2_w[li].transpose(1, 0, 2).reshape(dm, N_EXPERTS)], axis=1).astype(F32)
        rw = jnp.pad(rw, ((0, 0), (0, LANES - rw.shape[1])))
        rwh = rw.astype(BF16)
        rwl = (rw - rwh.astype(F32)).astype(BF16)
        rb = jnp.concatenate([r1_b[li], r2_b[li].reshape(-1)]).astype(F32)
        rb = jnp.pad(rb, (0, LANES - rb.shape[0])).reshape(1, LANES)
        p = dict(
            gmix=row(norm_mix_g), win=w_in[li].astype(BF16), win_s5=w_in[li][:, 2 * gw:].astype(BF16),
            lng=row(gm_ln_g), lnb=row(gm_ln_b),
            ws=gm_ws[li].astype(BF16),
            bs=jnp.broadcast_to(gm_bs[li].astype(F32)[:, :, None], (GM_HEADS, CHUNK, hd_dim)),
            gout_gm=row(out_norm_gm), d=row(s5_d), gluw=s5_glu_w[li].astype(BF16), glub=row(s5_glu_b),
            gout_s5=row(out_norm_s5), wout=w_out[li].astype(BF16), gffn=row(norm_ffn_g),
            rw_hl=jnp.concatenate([rwh, rwl], axis=1), rb=rb,
            w_gate=e_w_gate[li], w_up=e_w_up[li], w_down=e_w_down[li],
        )
        fwd = (s5_lam_re_fwd[li], s5_lam_im_fwd[li], s5_log_step_fwd[li], s5_b_re_fwd[li], s5_b_im_fwd[li],
               s5_c_re_fwd[li], s5_c_im_fwd[li])
        bwd = (s5_lam_re_bwd[li], s5_lam_im_bwd[li], s5_log_step_bwd[li], s5_b_re_bwd[li], s5_b_im_bwd[li],
               s5_c_re_bwd[li], s5_c_im_bwd[li])
        s5_ops = {}
        for x in xs:
            seg_steps = x.shape[1] // (S5_LC * (SUBLANES // x.shape[0]))
            if seg_steps not in s5_ops:
                s5_ops[seg_steps] = _s5_operator(fwd, bwd, S5_LC, seg_steps)
        last = li == depth - 1
        assert last, "depth > 1 needs an un-normalised layer output"
        xs = [_layer(x, p, s5_ops, gfin) for x in xs]
    return tuple(xs)
```

```python
import functools
import math

import jax
import jax.numpy as jnp
from jax import lax
from jax.experimental import pallas as pl
from jax.experimental.pallas import tpu as pltpu

F32 = jnp.float32
BF16 = jnp.bfloat16

EPS = 1e-6
LAMBDA_RE_MAX = -1e-4
GM_HEADS = 4
CHUNK = 128
S5_GROUP = 16
S5_STATE = 64
N_COARSE = 4
N_FINE = 8
N_EXPERTS = N_COARSE * N_FINE

LANES = 128
SUBLANES = 8
S5_LC = 16
VMEM_LIMIT = 56 * 1024 * 1024

TM_PROJ = 1024
TM_MIX = 512
MIX_TILES = 2
KB = 256
TM_EXPERT = 1024
SEG_ALIGN = 16


def _gelu(x):
    c = math.sqrt(2.0 / math.pi)
    half = 0.5 * x
    return half + half * jnp.tanh(x * (c + (c * 0.044715) * (x * x)))


def _rms(x, g):
    ms = jnp.mean(x * x, axis=-1, keepdims=True)
    return x * lax.rsqrt(ms + EPS) * g


def _dot(a, b):
    return jnp.dot(a, b, preferred_element_type=F32)


def _lockstep(tiles):
    while tiles:
        tiles = [t for t in tiles if next(t, "done") != "done"]


def _inproj_gmlp_kernel(x_ref, gmix_ref, win_ref, lng_ref, lnb_ref, ws_ref, bs_ref, gout_ref,
                        mgm_ref, us5_ref, y_scr):
    tm = x_ref.shape[0]
    gw = mgm_ref.shape[1]
    hd_dim = gw // GM_HEADS
    n_chunks = tm // CHUNK
    h = _rms(x_ref[...], gmix_ref[...]).astype(BF16)
    proj = _dot(h, win_ref[...])
    us5_ref[...] = proj[:, 2 * gw:]
    for hd in range(GM_HEADS):
        lo = hd * hd_dim
        vh = _gelu(proj[:, gw + lo:gw + lo + hd_dim])
        mu = jnp.mean(vh, axis=-1, keepdims=True)
        xc = vh - mu
        var = jnp.mean(xc * xc, axis=-1, keepdims=True)
        vn = (xc * lax.rsqrt(var + EPS) * lng_ref[:, lo:lo + hd_dim]
              + lnb_ref[:, lo:lo + hd_dim]).astype(BF16)
        rhs = jnp.concatenate([vn[c * CHUNK:(c + 1) * CHUNK] for c in range(n_chunks)], axis=1)
        s = _dot(ws_ref[hd], rhs)
        for c in range(n_chunks):
            sc = s[:, c * hd_dim:(c + 1) * hd_dim] + bs_ref[hd]
            u = _gelu(proj[c * CHUNK:(c + 1) * CHUNK, lo:lo + hd_dim])
            y_scr[c * CHUNK:(c + 1) * CHUNK, lo:lo + hd_dim] = u * sc
    mgm_ref[...] = _rms(y_scr[...], gout_ref[...]).astype(BF16)


def _inproj_gmlp(x2d, gmix, win_bf, lng, lnb, ws_bf, bs_b, gout):
    t, d = x2d.shape
    d_in = win_bf.shape[1]
    gw = lng.shape[1]
    s5w = d_in - 2 * gw
    tm = TM_PROJ
    const = lambda *shape: pl.BlockSpec(shape, lambda i: (0,) * len(shape))
    return pl.pallas_call(
        _inproj_gmlp_kernel,
        grid=(t // tm,),
        in_specs=[
            pl.BlockSpec((tm, d), lambda i: (i, 0)),
            const(1, d), const(d, d_in), const(1, gw), const(1, gw),
            const(GM_HEADS, CHUNK, CHUNK), const(GM_HEADS, CHUNK, gw // GM_HEADS), const(1, gw),
        ],
        out_specs=[pl.BlockSpec((tm, gw), lambda i: (i, 0)),
                   pl.BlockSpec((tm, s5w), lambda i: (i, 0))],
        out_shape=[jax.ShapeDtypeStruct((t, gw), BF16),
                   jax.ShapeDtypeStruct((t, s5w), F32)],
        scratch_shapes=[pltpu.VMEM((tm, gw), F32)],
        compiler_params=pltpu.CompilerParams(dimension_semantics=("parallel",),
                                             vmem_limit_bytes=VMEM_LIMIT),
        name="inproj_gmlp",
    )(x2d, gmix, win_bf, lng, lnb, ws_bf, bs_b, gout)


def _s5_consts(lam_re, lam_im, log_step, b_re, b_im, c_re, c_im, lc):
    lr = jnp.minimum(lam_re.astype(F32), LAMBDA_RE_MAX)
    li = lam_im.astype(F32)
    step = jnp.exp(log_step.astype(F32))[:, None]
    dr, di = lr * step, li * step
    ar, ai = _cexp(dr, di)
    nr, ni = ar - 1.0, ai
    den = lr * lr + li * li
    qr, qi = (nr * lr + ni * li) / den, (ni * lr - nr * li) / den
    br, bi = b_re.astype(F32), b_im.astype(F32)
    bbr = qr[..., None] * br - qi[..., None] * bi
    bbi = qr[..., None] * bi + qi[..., None] * br
    k = jnp.arange(lc + 1, dtype=F32)[:, None, None]
    pwr, pwi = _cexp(k * dr[None], k * di[None])
    return (dr, di), (pwr, pwi), (bbr, bbi), (c_re.astype(F32), c_im.astype(F32))


def _cexp(zr, zi):
    m = jnp.exp(zr)
    return m * jnp.cos(zi), m * jnp.sin(zi)


def _s5_operator(fwd, bwd, lc, seg_steps):
    consts = [_s5_consts(*fwd, lc), _s5_consts(*bwd, lc)]
    g, p, h = consts[0][2][0].shape
    lags, w1_parts, w2_parts, sc_rows, seg_rows = [], [], [], [], []
    for direction, (ld, pw, bb, c) in enumerate(consts):
        (dr, di), (pwr, pwi), (bbr, bbi), (cr, ci) = ld, pw, bb, c
        crt, cit = cr.transpose(0, 2, 1), ci.transpose(0, 2, 1)
        pwrt, pwit = pwr.transpose(1, 2, 0), pwi.transpose(1, 2, 0)
        cpr = crt[:, :, None, :] * pwrt[:, :, :, None] - cit[:, :, None, :] * pwit[:, :, :, None]
        cpi = crt[:, :, None, :] * pwit[:, :, :, None] + cit[:, :, None, :] * pwrt[:, :, :, None]
        ck = jnp.concatenate([cpr[:, :, :lc], cpi[:, :, :lc]], axis=1)
        if direction == 1:
            ck = jnp.flip(ck, 2)
        lags += [jnp.concatenate([bbr.transpose(0, 2, 1), -bbi.transpose(0, 2, 1)], axis=-1),
                 ck.reshape(g, 2 * p, lc * h)]
        er, ei = pwrt[:, :, :lc].transpose(0, 2, 1), pwit[:, :, :lc].transpose(0, 2, 1)
        if direction == 0:
            er, ei = jnp.flip(er, 1), jnp.flip(ei, 1)
        bbrt, bbit = bbr.transpose(0, 2, 1), bbi.transpose(0, 2, 1)
        e1 = jnp.concatenate([er, er], -1)[:, :, None, :]
        e2 = jnp.concatenate([-ei, ei], -1)[:, :, None, :]
        b_ri = jnp.concatenate([bbrt, bbit], -1)[:, None]
        b_ir = jnp.concatenate([bbit, bbrt], -1)[:, None]
        w1_parts += [e1 * b_ri + e2 * b_ir, e1 * b_ir - e2 * b_ri]
        fr, fi = cpr[:, :, 1:lc + 1], cpi[:, :, 1:lc + 1]
        if direction == 1:
            fr, fi = jnp.flip(fr, 2), jnp.flip(fi, 2)
        w2_parts += [fr, -fi]

        def mult(zr, zi):
            return [jnp.concatenate([zr, zr], -1), jnp.concatenate([-zi, zi], -1)]

        sc_rows += mult(*_cexp(lc * dr, lc * di))
        seg_rows += mult(*_cexp((lc * seg_steps) * dr, (lc * seg_steps) * di))
    w1 = jnp.concatenate(w1_parts, axis=-1).reshape(g, lc * h, 8 * p)
    w2 = jnp.concatenate(w2_parts, axis=1).reshape(g, 4 * p, lc * h)
    sc = jnp.stack(sc_rows + seg_rows, axis=1)
    return tuple(lags) + (w1.astype(BF16), w2.astype(BF16), sc.astype(F32))


S5_GPS = 2


def _s5_kernel(x_ref, bbf_ref, cpf_ref, bbb_ref, cpb_ref, w1_ref, w2_ref, sc_ref, y_ref,
               loc_scr, sin_scr, m_scr, *, n_seg):
    gps, rows, kw = x_ref.shape
    steps = rows // SUBLANES
    sw = sc_ref.shape[2]

    for gi in range(gps):
        kf = jnp.dot(bbf_ref[gi], cpf_ref[gi], precision=lax.Precision.HIGHEST, preferred_element_type=F32)
        kb = jnp.dot(bbb_ref[gi], cpb_ref[gi], precision=lax.Precision.HIGHEST, preferred_element_type=F32)
        hch = kf.shape[0]
        lc = kw // hch
        lane = lax.broadcasted_iota(jnp.int32, kf.shape, 1)
        for s in range(lc):
            f = kf if s == 0 else jnp.where(lane >= s * hch, pltpu.roll(kf, s * hch, 1), 0.0)
            left = (lc - 1 - s) * hch
            b = kb if left == 0 else pltpu.roll(kb, kw - left, 1)
            m_scr[gi, s * hch:(s + 1) * hch, :] = (f + jnp.where(lane < (s + 1) * hch, b, 0.0)).astype(BF16)
        loc_scr[gi] = _dot(x_ref[gi], w1_ref[gi])

    def bc(gi, i):
        return jnp.broadcast_to(sc_ref[gi, i:i + 1, :], (SUBLANES, sw))

    mult = [[bc(gi, i) for i in range(8)] for gi in range(gps)]

    def step(gi, s, state):
        f, fs, b, bs = state
        a1f, a2f, a1b, a2b = mult[gi][:4]
        rf = pl.multiple_of(s * SUBLANES, SUBLANES)
        rb = pl.multiple_of((steps - 1 - s) * SUBLANES, SUBLANES)
        lf = loc_scr[gi, pl.ds(rf, SUBLANES), 0:sw]
        lfs = loc_scr[gi, pl.ds(rf, SUBLANES), sw:2 * sw]
        lb = loc_scr[gi, pl.ds(rb, SUBLANES), 2 * sw:3 * sw]
        lbs = loc_scr[gi, pl.ds(rb, SUBLANES), 3 * sw:4 * sw]
        return (a1f * f + a2f * fs + lf, a1f * fs - a2f * f + lfs,
                a1b * b + a2b * bs + lb, a1b * bs - a2b * b + lbs)

    zero = jnp.zeros((SUBLANES, sw), F32)

    def pass1(s, carry):
        return tuple(step(gi, s, carry[gi]) for gi in range(gps))

    ends = lax.fori_loop(0, steps, pass1, tuple((zero,) * 4 for _ in range(gps)), unroll=4)

    seg = lax.broadcasted_iota(jnp.int32, (SUBLANES, sw), 0) % n_seg
    enter = []
    for gi in range(gps):
        f_end, fs_end, b_end, bs_end = ends[gi]
        p1f, p2f, p1b, p2b = mult[gi][4:]
        cf, cfs, cb, cbs = zero, zero, zero, zero
        for _ in range(n_seg - 1):
            ef = f_end + p1f * cf + p2f * cfs
            efs = fs_end + p1f * cfs - p2f * cf
            eb = b_end + p1b * cb + p2b * cbs
            ebs = bs_end + p1b * cbs - p2b * cb
            cf = jnp.where(seg >= 1, pltpu.roll(ef, 1, 0), 0.0)
            cfs = jnp.where(seg >= 1, pltpu.roll(efs, 1, 0), 0.0)
            cb = jnp.where(seg <= n_seg - 2, pltpu.roll(eb, SUBLANES - 1, 0), 0.0)
            cbs = jnp.where(seg <= n_seg - 2, pltpu.roll(ebs, SUBLANES - 1, 0), 0.0)
        enter.append((cf, cfs, cb, cbs))

    def pass2(s, carry):
        rf = pl.multiple_of(s * SUBLANES, SUBLANES)
        rb = pl.multiple_of((steps - 1 - s) * SUBLANES, SUBLANES)
        for gi in range(gps):
            sin_scr[gi, pl.ds(rf, SUBLANES), 0:sw] = carry[gi][0]
            sin_scr[gi, pl.ds(rb, SUBLANES), sw:2 * sw] = carry[gi][2]
        return tuple(step(gi, s, carry[gi]) for gi in range(gps))

    lax.fori_loop(0, steps, pass2, tuple(enter), unroll=4)

    for gi in range(gps):
        y_ref[gi] = _dot(x_ref[gi], m_scr[gi]) + _dot(sin_scr[gi].astype(BF16), w2_ref[gi])


def _s5_scan(xg, lag_factors, w1, w2, sc, n_seg):
    g, rows, kw = xg.shape
    sw = sc.shape[2]
    gps = S5_GPS
    blk = lambda a: pl.BlockSpec((gps,) + a.shape[1:], lambda i: (i, 0, 0))
    return pl.pallas_call(
        functools.partial(_s5_kernel, n_seg=n_seg),
        grid=(g // gps,),
        in_specs=[blk(xg)] + [blk(a) for a in lag_factors] + [blk(w1), blk(w2), blk(sc)],
        out_specs=pl.BlockSpec((gps, rows, kw), lambda i: (i, 0, 0)),
        out_shape=jax.ShapeDtypeStruct((g, rows, kw), F32),
        scratch_shapes=[pltpu.VMEM((gps, rows, 4 * sw), F32), pltpu.VMEM((gps, rows, 2 * sw), F32),
                        pltpu.VMEM((gps, kw, kw), BF16)],
        compiler_params=pltpu.CompilerParams(dimension_semantics=("parallel",),
                                             vmem_limit_bytes=VMEM_LIMIT),
        name="s5_scan",
    )(xg, *lag_factors, w1, w2, sc)


S5_NM = 16


def _block_transpose8(groups, width):
    lane = lax.broadcasted_iota(jnp.int32, groups[0][0].shape, 1)
    for d in (4, 2, 1):
        w = width * d
        hi = ((lane // w) % 2) == 1
        nxt = []
        for v in groups:
            out = list(v)
            for i0 in range(8):
                if i0 & d:
                    continue
                i1 = i0 + d
                out[i0] = jnp.where(hi, pltpu.roll(v[i1], w, 1), v[i0])
                out[i1] = jnp.where(hi, v[i1], pltpu.roll(v[i0], 8 * width - w, 1))
            nxt.append(out)
        groups = nxt
    return groups


def _tile_copies(hbm4, tile, buf, slot, sem, nm, to_hbm):
    copies = []
    for c in range(SUBLANES):
        for j in range(S5_LC):
            h = hbm4.at[c, pl.ds(tile * nm, nm), pl.ds(j, 1), :]
            v = buf.at[slot, j, :, pl.ds(c, 1), :]
            copies.append(pltpu.make_async_copy(v, h, sem.at[slot]) if to_hbm
                          else pltpu.make_async_copy(h, v, sem.at[slot]))
    return copies


def _start_alternating(copies):
    for k, cp in enumerate(copies):
        cp.start(priority=k % N_DMA_PRIORITIES)


def _s5_inproj_kernel(x4_ref, gmix_ref, w_ref, xg_ref, xs, sem, *, nm):
    i = pl.program_id(0)
    n = pl.num_programs(0)
    slot = i % 2
    dm = x4_ref.shape[3]

    @pl.when(i == 0)
    def _():
        _start_alternating(_tile_copies(x4_ref, 0, xs, 0, sem, nm, False))

    @pl.when(i + 1 < n)
    def _():
        _start_alternating(_tile_copies(x4_ref, i + 1, xs, 1 - slot, sem, nm, False))

    pltpu.make_async_copy(xs.at[slot], xs.at[slot], sem.at[slot]).wait()
    rows = nm * SUBLANES

    def half(a):
        x = xs[slot, 8 * a:8 * a + 8].reshape(8 * rows, dm)
        h = _rms(x, gmix_ref[...]).astype(BF16)
        yield
        z = _dot(h, w_ref[...])
        yield
        n_oct = z.shape[1] // LANES
        blocks = [[z[j8 * rows:(j8 + 1) * rows, q * LANES:(q + 1) * LANES] for j8 in range(8)] for q in range(n_oct)]
        for q, out in enumerate(_block_transpose8(blocks, S5_GROUP)):
            for g8, b in enumerate(out):
                xg_ref[8 * q + g8, :, a * LANES:(a + 1) * LANES] = b.astype(BF16)

    _lockstep([half(a) for a in range(S5_LC // 8)])


def _s5_inproj(x, gmix, w_s5_bf, n_seg):
    b, l, dm = x.shape
    steps = l // (S5_LC * n_seg)
    nm = S5_NM
    s5w = w_s5_bf.shape[1]
    g = s5w // S5_GROUP
    x4 = x.reshape(b * n_seg, steps, S5_LC, dm)
    return pl.pallas_call(
        functools.partial(_s5_inproj_kernel, nm=nm),
        grid=(steps // nm,),
        in_specs=[pl.BlockSpec(memory_space=pl.ANY),
                  pl.BlockSpec((1, dm), lambda i: (0, 0)),
                  pl.BlockSpec((dm, s5w), lambda i: (0, 0))],
        out_specs=pl.BlockSpec((g, nm * SUBLANES, S5_LC * S5_GROUP), lambda i: (0, i, 0)),
        out_shape=jax.ShapeDtypeStruct((g, steps * SUBLANES, S5_LC * S5_GROUP), BF16),
        scratch_shapes=[pltpu.VMEM((2, S5_LC, nm, SUBLANES, dm), F32), pltpu.SemaphoreType.DMA((2,))],
        compiler_params=pltpu.CompilerParams(dimension_semantics=("arbitrary",),
                                             vmem_limit_bytes=VMEM_LIMIT),
        name="s5_inproj",
    )(x4, gmix, w_s5_bf)


def _s5_to_tokens_kernel(yg_ref, ys4_ref, zs, sem, *, nm):
    i = pl.program_id(0)
    n = pl.num_programs(0)
    slot = i % 2
    rows = nm * SUBLANES

    def wait(s):
        pltpu.make_async_copy(zs.at[s], zs.at[s], sem.at[s]).wait()

    @pl.when(i >= 2)
    def _():
        wait(slot)

    n_oct = yg_ref.shape[0] // 8
    combos = [(q, a) for q in range(n_oct) for a in range(S5_LC // 8)]
    blocks = [[yg_ref[8 * q + g8, :, a * LANES:(a + 1) * LANES] for g8 in range(8)] for q, a in combos]
    for (q, a), out in zip(combos, _block_transpose8(blocks, S5_GROUP)):
        for j8, b in enumerate(out):
            zs[slot, 8 * a + j8, :, :, q * LANES:(q + 1) * LANES] = b.reshape(nm, SUBLANES, LANES)
    _start_alternating(_tile_copies(ys4_ref, i, zs, slot, sem, nm, True))

    @pl.when(i == n - 1)
    def _():
        wait(1 - slot)
        wait(slot)


def _s5_to_tokens(yg, b, l, n_seg):
    g, rows_total, kw = yg.shape
    steps = rows_total // SUBLANES
    nm = S5_NM
    s5w = g * S5_GROUP
    assert steps // nm >= 2
    ys4 = pl.pallas_call(
        functools.partial(_s5_to_tokens_kernel, nm=nm),
        grid=(steps // nm,),
        in_specs=[pl.BlockSpec((g, nm * SUBLANES, kw), lambda i: (0, i, 0))],
        out_specs=pl.BlockSpec(memory_space=pl.ANY),
        out_shape=jax.ShapeDtypeStruct((b * n_seg, steps, S5_LC, s5w), F32),
        scratch_shapes=[pltpu.VMEM((2, S5_LC, nm, SUBLANES, s5w), F32), pltpu.SemaphoreType.DMA((2,))],
        compiler_params=pltpu.CompilerParams(dimension_semantics=("arbitrary",),
                                             vmem_limit_bytes=VMEM_LIMIT),
        name="s5_to_tokens",
    )(yg)
    return ys4.reshape(b * l, s5w)


R_E1, R_E2, R_W1, R_W2, R_RANK1, R_RANK2 = range(6)


def _mix_route_kernel(ys_ref, us5_ref, mgm_ref, x_ref, d_ref, gluw_ref, glub_ref, gs5_ref,
                      wout_ref, gffn_ref, rw_ref, rb_ref, tri_ref,
                      x2_ref, t_ref, route_ref, cnt_ref):
    tiles = []
    for k in range(x_ref.shape[0] // TM_MIX):
        rows = slice(k * TM_MIX, (k + 1) * TM_MIX)
        tiles.append(_mix_route_tile(ys_ref.at[rows, :], us5_ref.at[rows, :], mgm_ref.at[rows, :], x_ref.at[rows, :],
                                     d_ref, gluw_ref, glub_ref, gs5_ref, wout_ref, gffn_ref, rw_ref,
                                     rb_ref, tri_ref, x2_ref.at[rows, :], t_ref.at[rows, :], route_ref.at[rows, :],
                                     cnt_ref.at[k]))
    _lockstep(tiles)


def _mix_route_tile(ys_ref, us5_ref, mgm_ref, x_ref, d_ref, gluw_ref, glub_ref, gs5_ref,
                    wout_ref, gffn_ref, rw_ref, rb_ref, tri_ref,
                    x2_ref, t_ref, route_ref, cnt_ref):
    gw = mgm_ref.shape[1]
    y = ys_ref[...] + d_ref[...] * us5_ref[...]
    g = _gelu(y)
    yield
    gate = _dot(g.astype(BF16), gluw_ref[...])
    yield
    z = g * jax.nn.sigmoid(gate + glub_ref[...])
    ms5 = _rms(z, gs5_ref[...]).astype(BF16)
    yield
    mix = _dot(mgm_ref[...], wout_ref[:gw, :]) + _dot(ms5, wout_ref[gw:, :])
    yield
    x2 = x_ref[...] + mix
    x2_ref[...] = x2
    t = _rms(x2, gffn_ref[...])
    t_hi = t.astype(BF16)
    t_ref[...] = t_hi
    t_lo = (t - t_hi.astype(F32)).astype(BF16)
    yield
    hl = _dot(t_hi, rw_ref[...])
    logits = (hl[:, :LANES] + hl[:, LANES:] + _dot(t_lo, rw_ref[:, :LANES])
              + rb_ref[...])
    yield
    lane = lax.broadcasted_iota(jnp.int32, logits.shape, 1).astype(F32)
    neg = jnp.float32(-jnp.inf)

    def first_max(mask):
        vals = jnp.where(mask, logits, neg)
        mx = jnp.max(vals, axis=-1, keepdims=True)
        idx = jnp.min(jnp.where(mask & (vals == mx), lane, float(LANES)), axis=-1, keepdims=True)
        return mx, idx

    coarse = lane < N_COARSE
    m1, grp = first_max(coarse)
    p_grp = 1.0 / jnp.sum(jnp.where(coarse, jnp.exp(logits - m1), 0.0), axis=-1, keepdims=True)
    lo = N_COARSE + grp * N_FINE
    fine = (lane >= lo) & (lane < lo + N_FINE)
    v1, i1 = first_max(fine)
    v2, i2 = first_max(fine & (lane != i1))
    e21 = jnp.exp(v2 - v1)
    w1 = p_grp / (1.0 + e21)
    w2 = p_grp * e21 / (1.0 + e21)
    e1 = i1 - N_COARSE
    e2 = i2 - N_COARSE
    hit1 = lane == e1
    hit2 = lane == e2
    onehot = jnp.where(hit1 | hit2, 1.0, 0.0)
    before = _dot(tri_ref[...], onehot.astype(BF16))
    rank1 = jnp.sum(jnp.where(hit1, before, 0.0), axis=-1, keepdims=True)
    rank2 = jnp.sum(jnp.where(hit2, before, 0.0), axis=-1, keepdims=True)
    tm = onehot.shape[0]
    cnt_ref[...] = before[tm - 1:tm, :] + onehot[tm - 1:tm, :]
    rec = jnp.zeros_like(logits)
    for slot, val in ((R_E1, e1), (R_E2, e2), (R_W1, w1), (R_W2, w2),
                      (R_RANK1, rank1), (R_RANK2, rank2)):
        rec = jnp.where(lane == slot, val, rec)
    route_ref[...] = rec


def _mix_route(ys, us5, mgm, x2d, d, gluw_bf, glub, gs5, wout_bf, gffn, rw_hl, rb):
    t, dm = x2d.shape
    gw = mgm.shape[1]
    s5w = us5.shape[1]
    tm = TM_MIX * MIX_TILES
    tri = jnp.tril(jnp.ones((TM_MIX, TM_MIX), F32), -1).astype(BF16)
    const = lambda *shape: pl.BlockSpec(shape, lambda i: (0,) * len(shape))
    tile = lambda w: pl.BlockSpec((tm, w), lambda i: (i, 0))
    return pl.pallas_call(
        _mix_route_kernel,
        grid=(t // tm,),
        in_specs=[tile(s5w), tile(s5w), tile(gw), tile(dm),
                  const(1, s5w), const(s5w, s5w), const(1, s5w), const(1, s5w),
                  const(gw + s5w, dm), const(1, dm), const(dm, 2 * LANES), const(1, LANES),
                  const(TM_MIX, TM_MIX)],
        out_specs=[tile(dm), tile(dm), tile(LANES), pl.BlockSpec((MIX_TILES, 1, LANES), lambda i: (i, 0, 0))],
        out_shape=[jax.ShapeDtypeStruct((t, dm), F32),
                   jax.ShapeDtypeStruct((t, dm), BF16),
                   jax.ShapeDtypeStruct((t, LANES), F32),
                   jax.ShapeDtypeStruct((t // TM_MIX, 1, LANES), F32)],
        compiler_params=pltpu.CompilerParams(dimension_semantics=("parallel",),
                                             vmem_limit_bytes=VMEM_LIMIT),
        name="mix_route",
    )(ys, us5, mgm, x2d, d, gluw_bf, glub, gs5, wout_bf, gffn, rw_hl, rb, tri)


def _local_rows(tm):
    worst = 2 * tm + N_EXPERTS * (SEG_ALIGN - 1)
    return -(-worst // LANES) * LANES


def _segment_plan(cnt, t, tm_expert):
    c = cnt[:, 0, :N_EXPERTS].astype(jnp.int32)
    n_tok_tiles = c.shape[0]
    al = (c + SEG_ALIGN - 1) // SEG_ALIGN * SEG_ALIGN
    lbase = jnp.cumsum(al, axis=1) - al
    tot = jnp.sum(al, axis=0)
    tot_pad = (tot + tm_expert - 1) // tm_expert * tm_expert
    gbase = jnp.cumsum(tot_pad) - tot_pad
    gpos = gbase[None, :] + jnp.cumsum(al, axis=0) - al
    n_tiles_max = -(-(2 * t + n_tok_tiles * N_EXPERTS * (SEG_ALIGN - 1)) // tm_expert) + N_EXPERTS
    tile_end = jnp.cumsum(tot_pad // tm_expert)
    n_tiles = tile_end[-1:].astype(jnp.int32)
    tile_idx = jnp.arange(n_tiles_max, dtype=jnp.int32)
    tile_expert = jnp.sum((tile_idx[:, None] >= tile_end[None, :]).astype(jnp.int32), axis=1)
    last = jnp.sum((n_tiles - 1 >= tile_end).astype(jnp.int32))
    tile_expert = jnp.where(tile_idx < n_tiles, tile_expert, last).astype(jnp.int32)
    ids = jnp.arange(N_EXPERTS, dtype=jnp.int32)
    later_used = (ids[None, :] > ids[:, None]) & (tot_pad[None, :] > 0)
    next_expert = jnp.min(jnp.where(later_used, ids[None, :], N_EXPERTS), axis=1)
    next_expert = jnp.where(next_expert == N_EXPERTS, ids, next_expert).astype(jnp.int32)
    lbase_f = jnp.pad(lbase.astype(F32), ((0, 0), (0, LANES - N_EXPERTS)))[:, None, :]
    flat = lambda a: a.reshape(-1).astype(jnp.int32)
    nch = al // SEG_ALIGN
    cum = jnp.cumsum(nch, axis=1)
    q = jnp.arange(_local_rows(TM_MIX) // SEG_ALIGN, dtype=jnp.int32)[None, :, None]
    seg_of_q = jnp.sum((q >= cum[:, None, :]).astype(jnp.int32), axis=2)
    in_seg = seg_of_q[:, :, None] == jnp.arange(N_EXPERTS, dtype=jnp.int32)[None, None, :]
    pick = lambda a: jnp.sum(jnp.where(in_seg, a[:, None, :], 0), axis=2)
    dst = pick(gpos) + (q[:, :, 0] - pick(cum - nch)) * SEG_ALIGN
    plan = dict(dst=flat(dst), n_chunks=flat(cum[:, -1]),
                tail_pos=flat(gbase + tot), tail_n=flat((tot_pad - tot) // SEG_ALIGN))
    return plan, lbase_f, (tile_expert, n_tiles, next_expert), n_tiles_max * tm_expert


def _local_positions(route, lbase):
    lane = lax.broadcasted_iota(jnp.int32, route.shape, 1).astype(F32)
    out = []
    for e_lane, r_lane in ((R_E1, R_RANK1), (R_E2, R_RANK2)):
        e = route[:, e_lane:e_lane + 1]
        base = jnp.sum(jnp.where(lane == e, lbase, 0.0), axis=-1, keepdims=True)
        out.append(base + route[:, r_lane:r_lane + 1])
    return out


WAIT_GROUP = 8
ISSUE_GROUP = 4
N_DMA_PRIORITIES = 2


def _segment_copies(i, dst_ref, nq_ref, local, glob, sem, to_global):
    per_tile = local.shape[0] // SEG_ALIGN
    n = nq_ref[i]

    def start(q, priority=0):
        lo = local.at[pl.ds(pl.multiple_of(q * SEG_ALIGN, SEG_ALIGN), SEG_ALIGN)]
        gl = glob.at[pl.ds(pl.multiple_of(dst_ref[i * per_tile + q], SEG_ALIGN), SEG_ALIGN)]
        cp = pltpu.make_async_copy(lo, gl, sem) if to_global else pltpu.make_async_copy(gl, lo, sem)
        cp.start(priority=priority)

    def group(k, carry):
        for u in range(ISSUE_GROUP):
            start(k * ISSUE_GROUP + u, u % N_DMA_PRIORITIES)
        return carry

    def single(q, carry):
        start(q)
        return carry

    full = n // ISSUE_GROUP
    lax.fori_loop(0, full, group, 0)
    lax.fori_loop(full * ISSUE_GROUP, n, single, 0)
    return n


def _wait_chunks(n, local, glob, sem):
    def wait_rows(rows):
        def one(c, carry):
            pltpu.make_async_copy(local.at[pl.ds(0, rows)], glob.at[pl.ds(0, rows)], sem).wait()
            return carry
        return one

    lax.fori_loop(0, n // WAIT_GROUP, wait_rows(WAIT_GROUP * SEG_ALIGN), 0)
    lax.fori_loop(0, n % WAIT_GROUP, wait_rows(SEG_ALIGN), 0)


def _sort_tile(t_ref, route_ref, lbase, local):
    tm = t_ref.shape[0]
    s_rows = local.shape[0]
    lp1, lp2 = _local_positions(route_ref[...], lbase)
    lane = lax.broadcasted_iota(jnp.int32, (tm, LANES), 1)
    lp_rows = jnp.where(lane == 0, lp1, jnp.where(lane == 1, lp2, -1.0)).T
    row = lax.broadcasted_iota(jnp.int32, (s_rows, tm), 0).astype(F32)
    onehot = jnp.where((row == lp_rows[0:1, :]) | (row == lp_rows[1:2, :]), 1.0, 0.0).astype(BF16)
    yield
    local[...] = _dot(onehot, t_ref[...]).astype(BF16)


def _sort_rows_kernel(dst_ref, nq_ref, tpos_ref, tn_ref, nt_ref, t_ref, route_ref, lbase_ref,
                      xs_ref, local_scr, zero_scr, cnt_scr, sem, zsem):
    i = pl.program_id(0)
    n = pl.num_programs(0)
    slot = i % 2
    per_step = local_scr.shape[1]

    def buf(s, k):
        return local_scr.at[s, k], sem.at[s * per_step + k], s * per_step + k

    def wait_slot(s):
        for k in range(per_step):
            local, sm, c = buf(s, k)
            _wait_chunks(cnt_scr[c], local, xs_ref, sm)

    te = zero_scr.shape[0]
    n_row_tiles = xs_ref.shape[0] // te

    @pl.when(i == 0)
    def _():
        zero_scr[...] = jnp.zeros_like(zero_scr)
        zero_chunk = zero_scr.at[pl.ds(0, SEG_ALIGN)]

        def tail(e, carry):
            def chunk(c, carry):
                dst = xs_ref.at[pl.ds(pl.multiple_of(tpos_ref[e] + c * SEG_ALIGN, SEG_ALIGN), SEG_ALIGN)]
                pltpu.make_async_copy(zero_chunk, dst, zsem).start()
                return carry

            return lax.fori_loop(0, tn_ref[e], chunk, carry)

        lax.fori_loop(0, N_EXPERTS, tail, 0)

        def unused_tile(j, carry):
            pltpu.make_async_copy(zero_scr, xs_ref.at[pl.ds(pl.multiple_of(j * te, te), te)], zsem).start()
            return carry

        lax.fori_loop(nt_ref[0], n_row_tiles, unused_tile, 0)

    @pl.when(i >= 2)
    def _():
        wait_slot(slot)

    tiles = []
    for k in range(per_step):
        rows = slice(k * TM_MIX, (k + 1) * TM_MIX)
        tiles.append(_sort_tile(t_ref.at[rows, :], route_ref.at[rows, :], lbase_ref[k], buf(slot, k)[0]))
    _lockstep(tiles)
    for k in range(per_step):
        local, sm, c = buf(slot, k)
        cnt_scr[c] = _segment_copies(i * per_step + k, dst_ref, nq_ref, local, xs_ref, sm, True)

    @pl.when(i == n - 1)
    def _():
        @pl.when(n >= 2)
        def _():
            wait_slot(1 - slot)

        wait_slot(slot)
        _wait_chunks(lax.fori_loop(0, N_EXPERTS, lambda e, total: total + tn_ref[e], 0), zero_scr, xs_ref, zsem)

        def unused_wait(j, carry):
            pltpu.make_async_copy(zero_scr, xs_ref.at[pl.ds(0, te)], zsem).wait()
            return carry

        lax.fori_loop(nt_ref[0], n_row_tiles, unused_wait, 0)


def _plan_specs(plan):
    keys = ('dst', 'n_chunks', 'tail_pos', 'tail_n')
    return [plan[k] for k in keys]


MOE_TILES = 2


def _sort_rows(plan, n_tiles, t_bf, route, lbase_f, n_sorted):
    t, dm = t_bf.shape
    tm = TM_MIX * MOE_TILES
    s_rows = _local_rows(TM_MIX)
    im = lambda i, *_: (i, 0)
    return pl.pallas_call(
        _sort_rows_kernel,
        grid_spec=pltpu.PrefetchScalarGridSpec(
            num_scalar_prefetch=5,
            grid=(t // tm,),
            in_specs=[pl.BlockSpec((tm, dm), im), pl.BlockSpec((tm, LANES), im),
                      pl.BlockSpec((MOE_TILES, 1, LANES), lambda i, *_: (i, 0, 0))],
            out_specs=pl.BlockSpec(memory_space=pl.ANY),
            scratch_shapes=[pltpu.VMEM((2, MOE_TILES, s_rows, dm), BF16), pltpu.VMEM((TM_EXPERT, dm), BF16),
                            pltpu.SMEM((2 * MOE_TILES,), jnp.int32), pltpu.SemaphoreType.DMA((2 * MOE_TILES,)),
                            pltpu.SemaphoreType.DMA(())],
        ),
        out_shape=jax.ShapeDtypeStruct((n_sorted, dm), BF16),
        compiler_params=pltpu.CompilerParams(dimension_semantics=("arbitrary",),
                                             vmem_limit_bytes=VMEM_LIMIT),
        name="sort_rows",
    )(*_plan_specs(plan), n_tiles, t_bf, route, lbase_f)


def _expert_weight_copies(e, slot, hbm, stage, sem):
    return [pltpu.make_async_copy(h.at[e], s.at[slot], sem.at[slot]) for h, s in zip(hbm, stage)]


def _experts_kernel(te_ref, nt_ref, nxt_ref, xs_ref, wg_ref, wu_ref, wd_ref, ys_ref,
                    sg, su, sd, wg_bf, wu_bf, wd_bf, slot_scr, sem):
    i = pl.program_id(0)
    e = te_ref[i]
    hbm, stage = (wg_ref, wu_ref, wd_ref), (sg, su, sd)

    @pl.when(i == 0)
    def _():
        slot_scr[0] = 0
        for cp in _expert_weight_copies(e, 0, hbm, stage, sem):
            cp.start()

    @pl.when((i == 0) | (e != te_ref[jnp.maximum(i - 1, 0)]))
    def _():
        slot = slot_scr[0]
        for cp in _expert_weight_copies(e, slot, hbm, stage, sem):
            cp.wait()
        wg_bf[...] = sg[slot].astype(BF16)
        wu_bf[...] = su[slot].astype(BF16)
        wd_bf[...] = sd[slot].astype(BF16)
        nxt = nxt_ref[e]

        @pl.when(nxt != e)
        def _():
            for cp in _expert_weight_copies(nxt, 1 - slot, hbm, stage, sem):
                cp.start()

        slot_scr[0] = 1 - slot

    @pl.when(i < nt_ref[0])
    def _():
        x = xs_ref[...]
        hidden = (jax.nn.silu(_dot(x, wg_bf[...])) * _dot(x, wu_bf[...])).astype(BF16)
        ys_ref[...] = _dot(hidden, wd_bf[...]).astype(BF16)


def _experts(tile_expert, n_tiles, next_expert, x_sorted, w_gate, w_up, w_down):
    n_sorted, dm = x_sorted.shape
    de = w_gate.shape[2]
    tm = TM_EXPERT
    return pl.pallas_call(
        _experts_kernel,
        grid_spec=pltpu.PrefetchScalarGridSpec(
            num_scalar_prefetch=3,
            grid=(n_sorted // tm,),
            in_specs=[pl.BlockSpec((tm, dm), lambda i, te, nt, nx: (jnp.minimum(i, nt[0] - 1), 0)),
                      pl.BlockSpec(memory_space=pl.ANY), pl.BlockSpec(memory_space=pl.ANY),
                      pl.BlockSpec(memory_space=pl.ANY)],
            out_specs=pl.BlockSpec((tm, dm), lambda i, te, nt, nx: (jnp.minimum(i, nt[0] - 1), 0)),
            scratch_shapes=[pltpu.VMEM((2, dm, de), F32), pltpu.VMEM((2, dm, de), F32), pltpu.VMEM((2, de, dm), F32),
                            pltpu.VMEM((dm, de), BF16), pltpu.VMEM((dm, de), BF16), pltpu.VMEM((de, dm), BF16),
                            pltpu.SMEM((1,), jnp.int32), pltpu.SemaphoreType.DMA((2,))],
        ),
        out_shape=jax.ShapeDtypeStruct((n_sorted, dm), BF16),
        compiler_params=pltpu.CompilerParams(dimension_semantics=("arbitrary",),
                                             vmem_limit_bytes=VMEM_LIMIT),
        input_output_aliases={3: 0},
        name="experts",
    )(tile_expert, n_tiles, next_expert, x_sorted, w_gate, w_up, w_down)


def _combine_tile(x2_ref, route_ref, lbase, gfin_ref, local, o_ref):
    tm = x2_ref.shape[0]
    s_rows = local.shape[0]
    route = route_ref[...]
    lp1, lp2 = _local_positions(route, lbase)
    w1, w2 = route[:, R_W1:R_W1 + 1], route[:, R_W2:R_W2 + 1]
    col0 = lax.broadcasted_iota(jnp.int32, (tm, KB), 1).astype(F32)
    moe = None
    for k in range(s_rows // KB):
        col = col0 + float(k * KB)
        pick = (jnp.where(col == lp1, w1, 0.0) + jnp.where(col == lp2, w2, 0.0)).astype(BF16)
        part = _dot(pick, local[k * KB:(k + 1) * KB, :])
        moe = part if moe is None else moe + part
    yield
    o_ref[...] = _rms(x2_ref[...] + moe, gfin_ref[...])


def _combine_kernel(dst_ref, nq_ref, tpos_ref, tn_ref, x2_ref, route_ref, lbase_ref, gfin_ref, ys_ref,
                    o_ref, local_scr, cnt_scr, sem):
    del tpos_ref, tn_ref
    i = pl.program_id(0)
    n = pl.num_programs(0)
    slot = i % 2
    per_step = local_scr.shape[1]
    s_rows = local_scr.shape[2]

    def fetch(step, s):
        for k in range(per_step):
            tile = step * per_step + k

            def clear(r, carry, k=k):
                local_scr[s, k, pl.ds(pl.multiple_of(r * SEG_ALIGN, SEG_ALIGN), SEG_ALIGN), :] = jnp.zeros(
                    (SEG_ALIGN, local_scr.shape[3]), BF16)
                return carry

            lax.fori_loop(nq_ref[tile], s_rows // SEG_ALIGN, clear, 0)
            cnt_scr[s * per_step + k] = _segment_copies(tile, dst_ref, nq_ref, local_scr.at[s, k], ys_ref,
                                                        sem.at[s * per_step + k], False)

    @pl.when(i == 0)
    def _():
        fetch(0, 0)

    @pl.when(i + 1 < n)
    def _():
        fetch(i + 1, 1 - slot)

    tiles = []
    for k in range(per_step):
        c = slot * per_step + k
        _wait_chunks(cnt_scr[c], local_scr.at[slot, k], ys_ref, sem.at[c])
        rows = slice(k * TM_MIX, (k + 1) * TM_MIX)
        tiles.append(_combine_tile(x2_ref.at[rows, :], route_ref.at[rows, :], lbase_ref[k], gfin_ref,
                                   local_scr.at[slot, k], o_ref.at[rows, :]))
    _lockstep(tiles)


def _combine(plan, x2, route, lbase_f, gfin, y_sorted):
    t, dm = x2.shape
    tm = TM_MIX * MOE_TILES
    s_rows = _local_rows(TM_MIX)
    im = lambda i, *_: (i, 0)
    return pl.pallas_call(
        _combine_kernel,
        grid_spec=pltpu.PrefetchScalarGridSpec(
            num_scalar_prefetch=4,
            grid=(t // tm,),
            in_specs=[pl.BlockSpec((tm, dm), im), pl.BlockSpec((tm, LANES), im),
                      pl.BlockSpec((MOE_TILES, 1, LANES), lambda i, *_: (i, 0, 0)),
                      pl.BlockSpec((1, dm), lambda i, *_: (0, 0)),
                      pl.BlockSpec(memory_space=pl.ANY)],
            out_specs=pl.BlockSpec((tm, dm), im),
            scratch_shapes=[pltpu.VMEM((2, MOE_TILES, s_rows, dm), BF16), pltpu.SMEM((2 * MOE_TILES,), jnp.int32),
                            pltpu.SemaphoreType.DMA((2 * MOE_TILES,))],
        ),
        out_shape=jax.ShapeDtypeStruct((t, dm), F32),
        compiler_params=pltpu.CompilerParams(dimension_semantics=("arbitrary",),
                                             vmem_limit_bytes=VMEM_LIMIT),
        name="combine_norm",
    )(*_plan_specs(plan), x2, route, lbase_f, gfin, y_sorted)


def _layer(x, p, s5_ops, gfin):
    b, l, dm = x.shape
    x2d = x.reshape(b * l, dm)
    mgm, us5 = _inproj_gmlp(x2d, p['gmix'], p['win'], p['lng'], p['lnb'], p['ws'], p['bs'], p['gout_gm'])
    n_seg = SUBLANES // b
    *lag_factors, w1, w2, sc = s5_ops[(l // (S5_LC * n_seg))]
    xg = _s5_inproj(x, p['gmix'], p['win_s5'], n_seg)
    yg = _s5_scan(xg, lag_factors, w1, w2, sc, n_seg)
    ys = _s5_to_tokens(yg, b, l, n_seg)
    x2, t_bf, route, counts = _mix_route(ys, us5, mgm, x2d, p['d'], p['gluw'], p['glub'], p['gout_s5'],
                                         p['wout'], p['gffn'], p['rw_hl'], p['rb'])
    plan, lbase_f, tiles, n_sorted = _segment_plan(counts, b * l, TM_EXPERT)
    x_sorted = _sort_rows(plan, tiles[1], t_bf, route, lbase_f, n_sorted)
    y_sorted = _experts(*tiles, x_sorted, p['w_gate'], p['w_up'], p['w_down'])
    out = _combine(plan, x2, route, lbase_f, gfin, y_sorted)
    return out.reshape(b, l, dm)


def kernel(x_prompt, x_sample, norm_mix_g, w_in, gm_ln_g, gm_ln_b, gm_ws, gm_bs, s5_lam_re_fwd, s5_lam_im_fwd, s5_log_step_fwd, s5_b_re_fwd, s5_b_im_fwd, s5_c_re_fwd, s5_c_im_fwd, s5_lam_re_bwd, s5_lam_im_bwd, s5_log_step_bwd, s5_b_re_bwd, s5_b_im_bwd, s5_c_re_bwd, s5_c_im_bwd, s5_d, s5_glu_w, s5_glu_b, out_norm_gm, out_norm_s5, w_out, norm_ffn_g, r1_w, r1_b, r2_w, r2_b, e_w_gate, e_w_up, e_w_down, norm_final_g):
    depth = w_in.shape[0]
    gfin = norm_final_g.reshape(1, -1).astype(F32)
    xs = [x_prompt, x_sample]
    for li in range(depth):
        row = lambda a: a[li].reshape(1, -1).astype(F32)
        dm = w_in.shape[1]
        gw = gm_ln_g.shape[1]
        hd_dim = gw // GM_HEADS
        rw = jnp.concatenate([r1_w[li], r2_w[li].transpose(1, 0, 2).reshape(dm, N_EXPERTS)], axis=1).astype(F32)
        rw = jnp.pad(rw, ((0, 0), (0, LANES - rw.shape[1])))
        rwh = rw.astype(BF16)
        rwl = (rw - rwh.astype(F32)).astype(BF16)
        rb = jnp.concatenate([r1_b[li], r2_b[li].reshape(-1)]).astype(F32)
        rb = jnp.pad(rb, (0, LANES - rb.shape[0])).reshape(1, LANES)
        p = dict(
            gmix=row(norm_mix_g), win=w_in[li].astype(BF16), win_s5=w_in[li][:, 2 * gw:].astype(BF16),
            lng=row(gm_ln_g), lnb=row(gm_ln_b),
            ws=gm_ws[li].astype(BF16),
            bs=jnp.broadcast_to(gm_bs[li].astype(F32)[:, :, None], (GM_HEADS, CHUNK, hd_dim)),
            gout_gm=row(out_norm_gm), d=row(s5_d), gluw=s5_glu_w[li].astype(BF16), glub=row(s5_glu_b),
            gout_s5=row(out_norm_s5), wout=w_out[li].astype(BF16), gffn=row(norm_ffn_g),
            rw_hl=jnp.concatenate([rwh, rwl], axis=1), rb=rb,
            w_gate=e_w_gate[li], w_up=e_w_up[li], w_down=e_w_down[li],
        )
        fwd = (s5_lam_re_fwd[li], s5_lam_im_fwd[li], s5_log_step_fwd[li], s5_b_re_fwd[li], s5_b_im_fwd[li],
               s5_c_re_fwd[li], s5_c_im_fwd[li])
        bwd = (s5_lam_re_bwd[li], s5_lam_im_bwd[li], s5_log_step_bwd[li], s5_b_re_bwd[li], s5_b_im_bwd[li],
               s5_c_re_bwd[li], s5_c_im_bwd[li])
        s5_ops = {}
        for x in xs:
            seg_steps = x.shape[1] // (S5_LC * (SUBLANES // x.shape[0]))
            if seg_steps not in s5_ops:
                s5_ops[seg_steps] = _s5_operator(fwd, bwd, S5_LC, seg_steps)
        last = li == depth - 1
        assert last, "depth > 1 needs an un-normalised layer output"
        xs = [_layer(x, p, s5_ops, gfin) for x in xs]
    return tuple(xs)
```

```python
import functools
import math

import jax
import jax.numpy as jnp
from jax import lax
from jax.experimental import pallas as pl
from jax.experimental.pallas import tpu as pltpu

F32 = jnp.float32
BF16 = jnp.bfloat16

EPS = 1e-6
LAMBDA_RE_MAX = -1e-4
GM_HEADS = 4
CHUNK = 128
S5_GROUP = 16
S5_STATE = 64
N_COARSE = 4
N_FINE = 8
N_EXPERTS = N_COARSE * N_FINE

LANES = 128
SUBLANES = 8
S5_LC = 16
VMEM_LIMIT = 56 * 1024 * 1024

TM_PROJ = 1024
TM_MIX = 512
MIX_TILES = 2
KB = 256
TM_EXPERT = 1024
SEG_ALIGN = 16


def _gelu(x):
    c = math.sqrt(2.0 / math.pi)
    half = 0.5 * x
    return half + half * jnp.tanh(x * (c + (c * 0.044715) * (x * x)))


def _rms(x, g):
    ms = jnp.mean(x * x, axis=-1, keepdims=True)
    return x * lax.rsqrt(ms + EPS) * g


def _dot(a, b):
    return jnp.dot(a, b, preferred_element_type=F32)


def _lockstep(tiles):
    while tiles:
        tiles = [t for t in tiles if next(t, "done") != "done"]


def _inproj_gmlp_kernel(x_ref, gmix_ref, win_ref, lng_ref, lnb_ref, ws_ref, bs_ref, gout_ref,
                        mgm_ref, us5_ref, y_scr):
    tm = x_ref.shape[0]
    gw = mgm_ref.shape[1]
    hd_dim = gw // GM_HEADS
    n_chunks = tm // CHUNK
    h = _rms(x_ref[...], gmix_ref[...]).astype(BF16)
    proj = _dot(h, win_ref[...])
    us5_ref[...] = proj[:, 2 * gw:]
    for hd in range(GM_HEADS):
        lo = hd * hd_dim
        vh = _gelu(proj[:, gw + lo:gw + lo + hd_dim])
        mu = jnp.mean(vh, axis=-1, keepdims=True)
        xc = vh - mu
        var = jnp.mean(xc * xc, axis=-1, keepdims=True)
        vn = (xc * lax.rsqrt(var + EPS) * lng_ref[:, lo:lo + hd_dim]
              + lnb_ref[:, lo:lo + hd_dim]).astype(BF16)
        rhs = jnp.concatenate([vn[c * CHUNK:(c + 1) * CHUNK] for c in range(n_chunks)], axis=1)
        s = _dot(ws_ref[hd], rhs)
        for c in range(n_chunks):
            sc = s[:, c * hd_dim:(c + 1) * hd_dim] + bs_ref[hd]
            u = _gelu(proj[c * CHUNK:(c + 1) * CHUNK, lo:lo + hd_dim])
            y_scr[c * CHUNK:(c + 1) * CHUNK, lo:lo + hd_dim] = u * sc
    mgm_ref[...] = _rms(y_scr[...], gout_ref[...]).astype(BF16)


def _inproj_gmlp(x2d, gmix, win_bf, lng, lnb, ws_bf, bs_b, gout):
    t, d = x2d.shape
    d_in = win_bf.shape[1]
    gw = lng.shape[1]
    s5w = d_in - 2 * gw
    tm = TM_PROJ
    const = lambda *shape: pl.BlockSpec(shape, lambda i: (0,) * len(shape))
    return pl.pallas_call(
        _inproj_gmlp_kernel,
        grid=(t // tm,),
        in_specs=[
            pl.BlockSpec((tm, d), lambda i: (i, 0)),
            const(1, d), const(d, d_in), const(1, gw), const(1, gw),
            const(GM_HEADS, CHUNK, CHUNK), const(GM_HEADS, CHUNK, gw // GM_HEADS), const(1, gw),
        ],
        out_specs=[pl.BlockSpec((tm, gw), lambda i: (i, 0)),
                   pl.BlockSpec((tm, s5w), lambda i: (i, 0))],
        out_shape=[jax.ShapeDtypeStruct((t, gw), BF16),
                   jax.ShapeDtypeStruct((t, s5w), F32)],
        scratch_shapes=[pltpu.VMEM((tm, gw), F32)],
        compiler_params=pltpu.CompilerParams(dimension_semantics=("parallel",),
                                             vmem_limit_bytes=VMEM_LIMIT),
        name="inproj_gmlp",
    )(x2d, gmix, win_bf, lng, lnb, ws_bf, bs_b, gout)


def _s5_consts(lam_re, lam_im, log_step, b_re, b_im, c_re, c_im, lc):
    lr = jnp.minimum(lam_re.astype(F32), LAMBDA_RE_MAX)
    li = lam_im.astype(F32)
    step = jnp.exp(log_step.astype(F32))[:, None]
    dr, di = lr * step, li * step
    ar, ai = _cexp(dr, di)
    nr, ni = ar - 1.0, ai
    den = lr * lr + li * li
    qr, qi = (nr * lr + ni * li) / den, (ni * lr - nr * li) / den
    br, bi = b_re.astype(F32), b_im.astype(F32)
    bbr = qr[..., None] * br - qi[..., None] * bi
    bbi = qr[..., None] * bi + qi[..., None] * br
    k = jnp.arange(lc + 1, dtype=F32)[:, None, None]
    pwr, pwi = _cexp(k * dr[None], k * di[None])
    return (dr, di), (pwr, pwi), (bbr, bbi), (c_re.astype(F32), c_im.astype(F32))


def _cexp(zr, zi):
    m = jnp.exp(zr)
    return m * jnp.cos(zi), m * jnp.sin(zi)


def _s5_operator(fwd, bwd, lc, seg_steps):
    consts = [_s5_consts(*fwd, lc), _s5_consts(*bwd, lc)]
    g, p, h = consts[0][2][0].shape
    lags, w1_parts, w2_parts, sc_rows, seg_rows = [], [], [], [], []
    for direction, (ld, pw, bb, c) in enumerate(consts):
        (dr, di), (pwr, pwi), (bbr, bbi), (cr, ci) = ld, pw, bb, c
        crt, cit = cr.transpose(0, 2, 1), ci.transpose(0, 2, 1)
        pwrt, pwit = pwr.transpose(1, 2, 0), pwi.transpose(1, 2, 0)
        cpr = crt[:, :, None, :] * pwrt[:, :, :, None] - cit[:, :, None, :] * pwit[:, :, :, None]
        cpi = crt[:, :, None, :] * pwit[:, :, :, None] + cit[:, :, None, :] * pwrt[:, :, :, None]
        ck = jnp.concatenate([cpr[:, :, :lc], cpi[:, :, :lc]], axis=1)
        if direction == 1:
            ck = jnp.flip(ck, 2)
        lags += [jnp.concatenate([bbr.transpose(0, 2, 1), -bbi.transpose(0, 2, 1)], axis=-1),
                 ck.reshape(g, 2 * p, lc * h)]
        er, ei = pwrt[:, :, :lc].transpose(0, 2, 1), pwit[:, :, :lc].transpose(0, 2, 1)
        if direction == 0:
            er, ei = jnp.flip(er, 1), jnp.flip(ei, 1)
        bbrt, bbit = bbr.transpose(0, 2, 1), bbi.transpose(0, 2, 1)
        e1 = jnp.concatenate([er, er], -1)[:, :, None, :]
        e2 = jnp.concatenate([-ei, ei], -1)[:, :, None, :]
        b_ri = jnp.concatenate([bbrt, bbit], -1)[:, None]
        b_ir = jnp.concatenate([bbit, bbrt], -1)[:, None]
        w1_parts += [e1 * b_ri + e2 * b_ir, e1 * b_ir - e2 * b_ri]
        fr, fi = cpr[:, :, 1:lc + 1], cpi[:, :, 1:lc + 1]
        if direction == 1:
            fr, fi = jnp.flip(fr, 2), jnp.flip(fi, 2)
        w2_parts += [fr, -fi]

        def mult(zr, zi):
            return [jnp.concatenate([zr, zr], -1), jnp.concatenate([-zi, zi], -1)]

        sc_rows += mult(*_cexp(lc * dr, lc * di))
        seg_rows += mult(*_cexp((lc * seg_steps) * dr, (lc * seg_steps) * di))
    w1 = jnp.concatenate(w1_parts, axis=-1).reshape(g, lc * h, 8 * p)
    w2 = jnp.concatenate(w2_parts, axis=1).reshape(g, 4 * p, lc * h)
    sc = jnp.stack(sc_rows + seg_rows, axis=1)
    return tuple(lags) + (w1.astype(BF16), w2.astype(BF16), sc.astype(F32))


S5_GPS = 2


def _s5_kernel(x_ref, bbf_ref, cpf_ref, bbb_ref, cpb_ref, w1_ref, w2_ref, sc_ref, y_ref,
               loc_scr, sin_scr, m_scr, *, n_seg):
    gps, rows, kw = x_ref.shape
    steps = rows // SUBLANES
    sw = sc_ref.shape[2]

    for gi in range(gps):
        kf = jnp.dot(bbf_ref[gi], cpf_ref[gi], precision=lax.Precision.HIGHEST, preferred_element_type=F32)
        kb = jnp.dot(bbb_ref[gi], cpb_ref[gi], precision=lax.Precision.HIGHEST, preferred_element_type=F32)
        hch = kf.shape[0]
        lc = kw // hch
        lane = lax.broadcasted_iota(jnp.int32, kf.shape, 1)
        for s in range(lc):
            f = kf if s == 0 else jnp.where(lane >= s * hch, pltpu.roll(kf, s * hch, 1), 0.0)
            left = (lc - 1 - s) * hch
            b = kb if left == 0 else pltpu.roll(kb, kw - left, 1)
            m_scr[gi, s * hch:(s + 1) * hch, :] = (f + jnp.where(lane < (s + 1) * hch, b, 0.0)).astype(BF16)
        loc_scr[gi] = _dot(x_ref[gi], w1_ref[gi])

    def bc(gi, i):
        return jnp.broadcast_to(sc_ref[gi, i:i + 1, :], (SUBLANES, sw))

    mult = [[bc(gi, i) for i in range(8)] for gi in range(gps)]

    def step(gi, s, state):
        f, fs, b, bs = state
        a1f, a2f, a1b, a2b = mult[gi][:4]
        rf = pl.multiple_of(s * SUBLANES, SUBLANES)
        rb = pl.multiple_of((steps - 1 - s) * SUBLANES, SUBLANES)
        lf = loc_scr[gi, pl.ds(rf, SUBLANES), 0:sw]
        lfs = loc_scr[gi, pl.ds(rf, SUBLANES), sw:2 * sw]
        lb = loc_scr[gi, pl.ds(rb, SUBLANES), 2 * sw:3 * sw]
        lbs = loc_scr[gi, pl.ds(rb, SUBLANES), 3 * sw:4 * sw]
        return (a1f * f + a2f * fs + lf, a1f * fs - a2f * f + lfs,
                a1b * b + a2b * bs + lb, a1b * bs - a2b * b + lbs)

    zero = jnp.zeros((SUBLANES, sw), F32)

    def pass1(s, carry):
        return tuple(step(gi, s, carry[gi]) for gi in range(gps))

    ends = lax.fori_loop(0, steps, pass1, tuple((zero,) * 4 for _ in range(gps)), unroll=4)

    seg = lax.broadcasted_iota(jnp.int32, (SUBLANES, sw), 0) % n_seg
    enter = []
    for gi in range(gps):
        f_end, fs_end, b_end, bs_end = ends[gi]
        p1f, p2f, p1b, p2b = mult[gi][4:]
        cf, cfs, cb, cbs = zero, zero, zero, zero
        for _ in range(n_seg - 1):
            ef = f_end + p1f * cf + p2f * cfs
            efs = fs_end + p1f * cfs - p2f * cf
            eb = b_end + p1b * cb + p2b * cbs
            ebs = bs_end + p1b * cbs - p2b * cb
            cf = jnp.where(seg >= 1, pltpu.roll(ef, 1, 0), 0.0)
            cfs = jnp.where(seg >= 1, pltpu.roll(efs, 1, 0), 0.0)
            cb = jnp.where(seg <= n_seg - 2, pltpu.roll(eb, SUBLANES - 1, 0), 0.0)
            cbs = jnp.where(seg <= n_seg - 2, pltpu.roll(ebs, SUBLANES - 1, 0), 0.0)
        enter.append((cf, cfs, cb, cbs))

    def pass2(s, carry):
        rf = pl.multiple_of(s * SUBLANES, SUBLANES)
        rb = pl.multiple_of((steps - 1 - s) * SUBLANES, SUBLANES)
        for gi in range(gps):
            sin_scr[gi, pl.ds(rf, SUBLANES), 0:sw] = carry[gi][0]
            sin_scr[gi, pl.ds(rb, SUBLANES), sw:2 * sw] = carry[gi][2]
        return tuple(step(gi, s, carry[gi]) for gi in range(gps))

    lax.fori_loop(0, steps, pass2, tuple(enter), unroll=4)

    for gi in range(gps):
        y_ref[gi] = _dot(x_ref[gi], m_scr[gi]) + _dot(sin_scr[gi].astype(BF16), w2_ref[gi])


def _s5_scan(xg, lag_factors, w1, w2, sc, n_seg):
    g, rows, kw = xg.shape
    sw = sc.shape[2]
    gps = S5_GPS
    blk = lambda a: pl.BlockSpec((gps,) + a.shape[1:], lambda i: (i, 0, 0))
    return pl.pallas_call(
        functools.partial(_s5_kernel, n_seg=n_seg),
        grid=(g // gps,),
        in_specs=[blk(xg)] + [blk(a) for a in lag_factors] + [blk(w1), blk(w2), blk(sc)],
        out_specs=pl.BlockSpec((gps, rows, kw), lambda i: (i, 0, 0)),
        out_shape=jax.ShapeDtypeStruct((g, rows, kw), F32),
        scratch_shapes=[pltpu.VMEM((gps, rows, 4 * sw), F32), pltpu.VMEM((gps, rows, 2 * sw), F32),
                        pltpu.VMEM((gps, kw, kw), BF16)],
        compiler_params=pltpu.CompilerParams(dimension_semantics=("parallel",),
                                             vmem_limit_bytes=VMEM_LIMIT),
        name="s5_scan",
    )(xg, *lag_factors, w1, w2, sc)


S5_NM = 16


def _block_transpose8(groups, width):
    lane = lax.broadcasted_iota(jnp.int32, groups[0][0].shape, 1)
    for d in (4, 2, 1):
        w = width * d
        hi = ((lane // w) % 2) == 1
        nxt = []
        for v in groups:
            out = list(v)
            for i0 in range(8):
                if i0 & d:
                    continue
                i1 = i0 + d
                out[i0] = jnp.where(hi, pltpu.roll(v[i1], w, 1), v[i0])
                out[i1] = jnp.where(hi, v[i1], pltpu.roll(v[i0], 8 * width - w, 1))
            nxt.append(out)
        groups = nxt
    return groups


def _tile_copies(hbm4, tile, buf, slot, sem, nm, to_hbm):
    copies = []
    for c in range(SUBLANES):
        for j in range(S5_LC):
            h = hbm4.at[c, pl.ds(tile * nm, nm), pl.ds(j, 1), :]
            v = buf.at[slot, j, :, pl.ds(c, 1), :]
            copies.append(pltpu.make_async_copy(v, h, sem.at[slot]) if to_hbm
                          else pltpu.make_async_copy(h, v, sem.at[slot]))
    return copies


def _s5_inproj_kernel(x4_ref, gmix_ref, w_ref, xg_ref, xs, sem, *, nm):
    i = pl.program_id(0)
    n = pl.num_programs(0)
    slot = i % 2
    dm = x4_ref.shape[3]

    @pl.when(i == 0)
    def _():
        for cp in _tile_copies(x4_ref, 0, xs, 0, sem, nm, False):
            cp.start()

    @pl.when(i + 1 < n)
    def _():
        for cp in _tile_copies(x4_ref, i + 1, xs, 1 - slot, sem, nm, False):
            cp.start()

    pltpu.make_async_copy(xs.at[slot], xs.at[slot], sem.at[slot]).wait()
    rows = nm * SUBLANES

    def half(a):
        x = xs[slot, 8 * a:8 * a + 8].reshape(8 * rows, dm)
        h = _rms(x, gmix_ref[...]).astype(BF16)
        yield
        z = _dot(h, w_ref[...])
        yield
        n_oct = z.shape[1] // LANES
        blocks = [[z[j8 * rows:(j8 + 1) * rows, q * LANES:(q + 1) * LANES] for j8 in range(8)] for q in range(n_oct)]
        for q, out in enumerate(_block_transpose8(blocks, S5_GROUP)):
            for g8, b in enumerate(out):
                xg_ref[8 * q + g8, :, a * LANES:(a + 1) * LANES] = b.astype(BF16)

    _lockstep([half(a) for a in range(S5_LC // 8)])


def _s5_inproj(x, gmix, w_s5_bf, n_seg):
    b, l, dm = x.shape
    steps = l // (S5_LC * n_seg)
    nm = S5_NM
    s5w = w_s5_bf.shape[1]
    g = s5w // S5_GROUP
    x4 = x.reshape(b * n_seg, steps, S5_LC, dm)
    return pl.pallas_call(
        functools.partial(_s5_inproj_kernel, nm=nm),
        grid=(steps // nm,),
        in_specs=[pl.BlockSpec(memory_space=pl.ANY),
                  pl.BlockSpec((1, dm), lambda i: (0, 0)),
                  pl.BlockSpec((dm, s5w), lambda i: (0, 0))],
        out_specs=pl.BlockSpec((g, nm * SUBLANES, S5_LC * S5_GROUP), lambda i: (0, i, 0)),
        out_shape=jax.ShapeDtypeStruct((g, steps * SUBLANES, S5_LC * S5_GROUP), BF16),
        scratch_shapes=[pltpu.VMEM((2, S5_LC, nm, SUBLANES, dm), F32), pltpu.SemaphoreType.DMA((2,))],
        compiler_params=pltpu.CompilerParams(dimension_semantics=("arbitrary",),
                                             vmem_limit_bytes=VMEM_LIMIT),
        name="s5_inproj",
    )(x4, gmix, w_s5_bf)


def _s5_to_tokens_kernel(yg_ref, ys4_ref, zs, sem, *, nm):
    i = pl.program_id(0)
    n = pl.num_programs(0)
    slot = i % 2
    rows = nm * SUBLANES

    def wait(s):
        pltpu.make_async_copy(zs.at[s], zs.at[s], sem.at[s]).wait()

    @pl.when(i >= 2)
    def _():
        wait(slot)

    n_oct = yg_ref.shape[0] // 8
    combos = [(q, a) for q in range(n_oct) for a in range(S5_LC // 8)]
    blocks = [[yg_ref[8 * q + g8, :, a * LANES:(a + 1) * LANES] for g8 in range(8)] for q, a in combos]
    for (q, a), out in zip(combos, _block_transpose8(blocks, S5_GROUP)):
        for j8, b in enumerate(out):
            zs[slot, 8 * a + j8, :, :, q * LANES:(q + 1) * LANES] = b.reshape(nm, SUBLANES, LANES)
    for cp in _tile_copies(ys4_ref, i, zs, slot, sem, nm, True):
        cp.start()

    @pl.when(i == n - 1)
    def _():
        wait(1 - slot)
        wait(slot)


def _s5_to_tokens(yg, b, l, n_seg):
    g, rows_total, kw = yg.shape
    steps = rows_total // SUBLANES
    nm = S5_NM
    s5w = g * S5_GROUP
    assert steps // nm >= 2
    ys4 = pl.pallas_call(
        functools.partial(_s5_to_tokens_kernel, nm=nm),
        grid=(steps // nm,),
        in_specs=[pl.BlockSpec((g, nm * SUBLANES, kw), lambda i: (0, i, 0))],
        out_specs=pl.BlockSpec(memory_space=pl.ANY),
        out_shape=jax.ShapeDtypeStruct((b * n_seg, steps, S5_LC, s5w), F32),
        scratch_shapes=[pltpu.VMEM((2, S5_LC, nm, SUBLANES, s5w), F32), pltpu.SemaphoreType.DMA((2,))],
        compiler_params=pltpu.CompilerParams(dimension_semantics=("arbitrary",),
                                             vmem_limit_bytes=VMEM_LIMIT),
        name="s5_to_tokens",
    )(yg)
    return ys4.reshape(b * l, s5w)


R_E1, R_E2, R_W1, R_W2, R_RANK1, R_RANK2 = range(6)


def _mix_route_kernel(ys_ref, us5_ref, mgm_ref, x_ref, d_ref, gluw_ref, glub_ref, gs5_ref,
                      wout_ref, gffn_ref, rw_ref, rb_ref, tri_ref,
                      x2_ref, t_ref, route_ref, cnt_ref):
    tiles = []
    for k in range(x_ref.shape[0] // TM_MIX):
        rows = slice(k * TM_MIX, (k + 1) * TM_MIX)
        tiles.append(_mix_route_tile(ys_ref.at[rows, :], us5_ref.at[rows, :], mgm_ref.at[rows, :], x_ref.at[rows, :],
                                     d_ref, gluw_ref, glub_ref, gs5_ref, wout_ref, gffn_ref, rw_ref,
                                     rb_ref, tri_ref, x2_ref.at[rows, :], t_ref.at[rows, :], route_ref.at[rows, :],
                                     cnt_ref.at[k]))
    _lockstep(tiles)


def _mix_route_tile(ys_ref, us5_ref, mgm_ref, x_ref, d_ref, gluw_ref, glub_ref, gs5_ref,
                    wout_ref, gffn_ref, rw_ref, rb_ref, tri_ref,
                    x2_ref, t_ref, route_ref, cnt_ref):
    gw = mgm_ref.shape[1]
    y = ys_ref[...] + d_ref[...] * us5_ref[...]
    g = _gelu(y)
    yield
    gate = _dot(g.astype(BF16), gluw_ref[...])
    yield
    z = g * jax.nn.sigmoid(gate + glub_ref[...])
    ms5 = _rms(z, gs5_ref[...]).astype(BF16)
    yield
    mix = _dot(mgm_ref[...], wout_ref[:gw, :]) + _dot(ms5, wout_ref[gw:, :])
    yield
    x2 = x_ref[...] + mix
    x2_ref[...] = x2
    t = _rms(x2, gffn_ref[...])
    t_hi = t.astype(BF16)
    t_ref[...] = t_hi
    t_lo = (t - t_hi.astype(F32)).astype(BF16)
    yield
    hl = _dot(t_hi, rw_ref[...])
    logits = (hl[:, :LANES] + hl[:, LANES:] + _dot(t_lo, rw_ref[:, :LANES])
              + rb_ref[...])
    yield
    lane = lax.broadcasted_iota(jnp.int32, logits.shape, 1).astype(F32)
    neg = jnp.float32(-jnp.inf)

    def first_max(mask):
        vals = jnp.where(mask, logits, neg)
        mx = jnp.max(vals, axis=-1, keepdims=True)
        idx = jnp.min(jnp.where(mask & (vals == mx), lane, float(LANES)), axis=-1, keepdims=True)
        return mx, idx

    coarse = lane < N_COARSE
    m1, grp = first_max(coarse)
    p_grp = 1.0 / jnp.sum(jnp.where(coarse, jnp.exp(logits - m1), 0.0), axis=-1, keepdims=True)
    lo = N_COARSE + grp * N_FINE
    fine = (lane >= lo) & (lane < lo + N_FINE)
    v1, i1 = first_max(fine)
    v2, i2 = first_max(fine & (lane != i1))
    e21 = jnp.exp(v2 - v1)
    w1 = p_grp / (1.0 + e21)
    w2 = p_grp * e21 / (1.0 + e21)
    e1 = i1 - N_COARSE
    e2 = i2 - N_COARSE
    hit1 = lane == e1
    hit2 = lane == e2
    onehot = jnp.where(hit1 | hit2, 1.0, 0.0)
    before = _dot(tri_ref[...], onehot.astype(BF16))
    rank1 = jnp.sum(jnp.where(hit1, before, 0.0), axis=-1, keepdims=True)
    rank2 = jnp.sum(jnp.where(hit2, before, 0.0), axis=-1, keepdims=True)
    tm = onehot.shape[0]
    cnt_ref[...] = before[tm - 1:tm, :] + onehot[tm - 1:tm, :]
    rec = jnp.zeros_like(logits)
    for slot, val in ((R_E1, e1), (R_E2, e2), (R_W1, w1), (R_W2, w2),
                      (R_RANK1, rank1), (R_RANK2, rank2)):
        rec = jnp.where(lane == slot, val, rec)
    route_ref[...] = rec


def _mix_route(ys, us5, mgm, x2d, d, gluw_bf, glub, gs5, wout_bf, gffn, rw_hl, rb):
    t, dm = x2d.shape
    gw = mgm.shape[1]
    s5w = us5.shape[1]
    tm = TM_MIX * MIX_TILES
    tri = jnp.tril(jnp.ones((TM_MIX, TM_MIX), F32), -1).astype(BF16)
    const = lambda *shape: pl.BlockSpec(shape, lambda i: (0,) * len(shape))
    tile = lambda w: pl.BlockSpec((tm, w), lambda i: (i, 0))
    return pl.pallas_call(
        _mix_route_kernel,
        grid=(t // tm,),
        in_specs=[tile(s5w), tile(s5w), tile(gw), tile(dm),
                  const(1, s5w), const(s5w, s5w), const(1, s5w), const(1, s5w),
                  const(gw + s5w, dm), const(1, dm), const(dm, 2 * LANES), const(1, LANES),
                  const(TM_MIX, TM_MIX)],
        out_specs=[tile(dm), tile(dm), tile(LANES), pl.BlockSpec((MIX_TILES, 1, LANES), lambda i: (i, 0, 0))],
        out_shape=[jax.ShapeDtypeStruct((t, dm), F32),
                   jax.ShapeDtypeStruct((t, dm), BF16),
                   jax.ShapeDtypeStruct((t, LANES), F32),
                   jax.ShapeDtypeStruct((t // TM_MIX, 1, LANES), F32)],
        compiler_params=pltpu.CompilerParams(dimension_semantics=("parallel",),
                                             vmem_limit_bytes=VMEM_LIMIT),
        name="mix_route",
    )(ys, us5, mgm, x2d, d, gluw_bf, glub, gs5, wout_bf, gffn, rw_hl, rb, tri)


def _local_rows(tm):
    worst = 2 * tm + N_EXPERTS * (SEG_ALIGN - 1)
    return -(-worst // LANES) * LANES


def _segment_plan(cnt, t, tm_expert):
    c = cnt[:, 0, :N_EXPERTS].astype(jnp.int32)
    n_tok_tiles = c.shape[0]
    al = (c + SEG_ALIGN - 1) // SEG_ALIGN * SEG_ALIGN
    lbase = jnp.cumsum(al, axis=1) - al
    tot = jnp.sum(al, axis=0)
    tot_pad = (tot + tm_expert - 1) // tm_expert * tm_expert
    gbase = jnp.cumsum(tot_pad) - tot_pad
    gpos = gbase[None, :] + jnp.cumsum(al, axis=0) - al
    n_tiles_max = -(-(2 * t + n_tok_tiles * N_EXPERTS * (SEG_ALIGN - 1)) // tm_expert) + N_EXPERTS
    tile_end = jnp.cumsum(tot_pad // tm_expert)
    n_tiles = tile_end[-1:].astype(jnp.int32)
    tile_idx = jnp.arange(n_tiles_max, dtype=jnp.int32)
    tile_expert = jnp.sum((tile_idx[:, None] >= tile_end[None, :]).astype(jnp.int32), axis=1)
    last = jnp.sum((n_tiles - 1 >= tile_end).astype(jnp.int32))
    tile_expert = jnp.where(tile_idx < n_tiles, tile_expert, last).astype(jnp.int32)
    ids = jnp.arange(N_EXPERTS, dtype=jnp.int32)
    later_used = (ids[None, :] > ids[:, None]) & (tot_pad[None, :] > 0)
    next_expert = jnp.min(jnp.where(later_used, ids[None, :], N_EXPERTS), axis=1)
    next_expert = jnp.where(next_expert == N_EXPERTS, ids, next_expert).astype(jnp.int32)
    lbase_f = jnp.pad(lbase.astype(F32), ((0, 0), (0, LANES - N_EXPERTS)))[:, None, :]
    flat = lambda a: a.reshape(-1).astype(jnp.int32)
    nch = al // SEG_ALIGN
    cum = jnp.cumsum(nch, axis=1)
    q = jnp.arange(_local_rows(TM_MIX) // SEG_ALIGN, dtype=jnp.int32)[None, :, None]
    seg_of_q = jnp.sum((q >= cum[:, None, :]).astype(jnp.int32), axis=2)
    in_seg = seg_of_q[:, :, None] == jnp.arange(N_EXPERTS, dtype=jnp.int32)[None, None, :]
    pick = lambda a: jnp.sum(jnp.where(in_seg, a[:, None, :], 0), axis=2)
    dst = pick(gpos) + (q[:, :, 0] - pick(cum - nch)) * SEG_ALIGN
    plan = dict(dst=flat(dst), n_chunks=flat(cum[:, -1]),
                tail_pos=flat(gbase + tot), tail_n=flat((tot_pad - tot) // SEG_ALIGN))
    return plan, lbase_f, (tile_expert, n_tiles, next_expert), n_tiles_max * tm_expert


def _local_positions(route, lbase):
    lane = lax.broadcasted_iota(jnp.int32, route.shape, 1).astype(F32)
    out = []
    for e_lane, r_lane in ((R_E1, R_RANK1), (R_E2, R_RANK2)):
        e = route[:, e_lane:e_lane + 1]
        base = jnp.sum(jnp.where(lane == e, lbase, 0.0), axis=-1, keepdims=True)
        out.append(base + route[:, r_lane:r_lane + 1])
    return out


WAIT_GROUP = 8
ISSUE_GROUP = 4
ZERO_FILL_PRIORITY = 1


def _segment_copies(i, dst_ref, nq_ref, local, glob, sem, to_global):
    per_tile = local.shape[0] // SEG_ALIGN
    n = nq_ref[i]

    def start(q):
        lo = local.at[pl.ds(pl.multiple_of(q * SEG_ALIGN, SEG_ALIGN), SEG_ALIGN)]
        gl = glob.at[pl.ds(pl.multiple_of(dst_ref[i * per_tile + q], SEG_ALIGN), SEG_ALIGN)]
        (pltpu.make_async_copy(lo, gl, sem) if to_global else pltpu.make_async_copy(gl, lo, sem)).start()

    def group(k, carry):
        for u in range(ISSUE_GROUP):
            start(k * ISSUE_GROUP + u)
        return carry

    def single(q, carry):
        start(q)
        return carry

    full = n // ISSUE_GROUP
    lax.fori_loop(0, full, group, 0)
    lax.fori_loop(full * ISSUE_GROUP, n, single, 0)
    return n


def _wait_chunks(n, local, glob, sem):
    def wait_rows(rows):
        def one(c, carry):
            pltpu.make_async_copy(local.at[pl.ds(0, rows)], glob.at[pl.ds(0, rows)], sem).wait()
            return carry
        return one

    lax.fori_loop(0, n // WAIT_GROUP, wait_rows(WAIT_GROUP * SEG_ALIGN), 0)
    lax.fori_loop(0, n % WAIT_GROUP, wait_rows(SEG_ALIGN), 0)


def _sort_tile(t_ref, route_ref, lbase, local):
    tm = t_ref.shape[0]
    s_rows = local.shape[0]
    lp1, lp2 = _local_positions(route_ref[...], lbase)
    lane = lax.broadcasted_iota(jnp.int32, (tm, LANES), 1)
    lp_rows = jnp.where(lane == 0, lp1, jnp.where(lane == 1, lp2, -1.0)).T
    row = lax.broadcasted_iota(jnp.int32, (s_rows, tm), 0).astype(F32)
    onehot = jnp.where((row == lp_rows[0:1, :]) | (row == lp_rows[1:2, :]), 1.0, 0.0).astype(BF16)
    yield
    local[...] = _dot(onehot, t_ref[...]).astype(BF16)


def _sort_rows_kernel(dst_ref, nq_ref, tpos_ref, tn_ref, nt_ref, t_ref, route_ref, lbase_ref,
                      xs_ref, local_scr, zero_scr, cnt_scr, sem, zsem):
    i = pl.program_id(0)
    n = pl.num_programs(0)
    slot = i % 2
    per_step = local_scr.shape[1]

    def buf(s, k):
        return local_scr.at[s, k], sem.at[s * per_step + k], s * per_step + k

    def wait_slot(s):
        for k in range(per_step):
            local, sm, c = buf(s, k)
            _wait_chunks(cnt_scr[c], local, xs_ref, sm)

    te = zero_scr.shape[0]
    n_row_tiles = xs_ref.shape[0] // te

    @pl.when(i == 0)
    def _():
        zero_scr[...] = jnp.zeros_like(zero_scr)
        zero_chunk = zero_scr.at[pl.ds(0, SEG_ALIGN)]

        def tail(e, carry):
            def chunk(c, carry):
                dst = xs_ref.at[pl.ds(pl.multiple_of(tpos_ref[e] + c * SEG_ALIGN, SEG_ALIGN), SEG_ALIGN)]
                pltpu.make_async_copy(zero_chunk, dst, zsem).start(priority=ZERO_FILL_PRIORITY)
                return carry

            return lax.fori_loop(0, tn_ref[e], chunk, carry)

        lax.fori_loop(0, N_EXPERTS, tail, 0)

        def unused_tile(j, carry):
            pltpu.make_async_copy(zero_scr, xs_ref.at[pl.ds(pl.multiple_of(j * te, te), te)],
                                  zsem).start(priority=ZERO_FILL_PRIORITY)
            return carry

        lax.fori_loop(nt_ref[0], n_row_tiles, unused_tile, 0)

    @pl.when(i >= 2)
    def _():
        wait_slot(slot)

    tiles = []
    for k in range(per_step):
        rows = slice(k * TM_MIX, (k + 1) * TM_MIX)
        tiles.append(_sort_tile(t_ref.at[rows, :], route_ref.at[rows, :], lbase_ref[k], buf(slot, k)[0]))
    _lockstep(tiles)
    for k in range(per_step):
        local, sm, c = buf(slot, k)
        cnt_scr[c] = _segment_copies(i * per_step + k, dst_ref, nq_ref, local, xs_ref, sm, True)

    @pl.when(i == n - 1)
    def _():
        @pl.when(n >= 2)
        def _():
            wait_slot(1 - slot)

        wait_slot(slot)
        _wait_chunks(lax.fori_loop(0, N_EXPERTS, lambda e, total: total + tn_ref[e], 0), zero_scr, xs_ref, zsem)

        def unused_wait(j, carry):
            pltpu.make_async_copy(zero_scr, xs_ref.at[pl.ds(0, te)], zsem).wait()
            return carry

        lax.fori_loop(nt_ref[0], n_row_tiles, unused_wait, 0)


def _plan_specs(plan):
    keys = ('dst', 'n_chunks', 'tail_pos', 'tail_n')
    return [plan[k] for k in keys]


MOE_TILES = 2


def _sort_rows(plan, n_tiles, t_bf, route, lbase_f, n_sorted):
    t, dm = t_bf.shape
    tm = TM_MIX * MOE_TILES
    s_rows = _local_rows(TM_MIX)
    im = lambda i, *_: (i, 0)
    return pl.pallas_call(
        _sort_rows_kernel,
        grid_spec=pltpu.PrefetchScalarGridSpec(
            num_scalar_prefetch=5,
            grid=(t // tm,),
            in_specs=[pl.BlockSpec((tm, dm), im), pl.BlockSpec((tm, LANES), im),
                      pl.BlockSpec((MOE_TILES, 1, LANES), lambda i, *_: (i, 0, 0))],
            out_specs=pl.BlockSpec(memory_space=pl.ANY),
            scratch_shapes=[pltpu.VMEM((2, MOE_TILES, s_rows, dm), BF16), pltpu.VMEM((TM_EXPERT, dm), BF16),
                            pltpu.SMEM((2 * MOE_TILES,), jnp.int32), pltpu.SemaphoreType.DMA((2 * MOE_TILES,)),
                            pltpu.SemaphoreType.DMA(())],
        ),
        out_shape=jax.ShapeDtypeStruct((n_sorted, dm), BF16),
        compiler_params=pltpu.CompilerParams(dimension_semantics=("arbitrary",),
                                             vmem_limit_bytes=VMEM_LIMIT),
        name="sort_rows",
    )(*_plan_specs(plan), n_tiles, t_bf, route, lbase_f)


def _expert_weight_copies(e, slot, hbm, stage, sem):
    return [pltpu.make_async_copy(h.at[e], s.at[slot], sem.at[slot]) for h, s in zip(hbm, stage)]


def _experts_kernel(te_ref, nt_ref, nxt_ref, xs_ref, wg_ref, wu_ref, wd_ref, ys_ref,
                    sg, su, sd, wg_bf, wu_bf, wd_bf, slot_scr, sem):
    i = pl.program_id(0)
    e = te_ref[i]
    hbm, stage = (wg_ref, wu_ref, wd_ref), (sg, su, sd)

    @pl.when(i == 0)
    def _():
        slot_scr[0] = 0
        for cp in _expert_weight_copies(e, 0, hbm, stage, sem):
            cp.start()

    @pl.when((i == 0) | (e != te_ref[jnp.maximum(i - 1, 0)]))
    def _():
        slot = slot_scr[0]
        for cp in _expert_weight_copies(e, slot, hbm, stage, sem):
            cp.wait()
        wg_bf[...] = sg[slot].astype(BF16)
        wu_bf[...] = su[slot].astype(BF16)
        wd_bf[...] = sd[slot].astype(BF16)
        nxt = nxt_ref[e]

        @pl.when(nxt != e)
        def _():
            for cp in _expert_weight_copies(nxt, 1 - slot, hbm, stage, sem):
                cp.start()

        slot_scr[0] = 1 - slot

    @pl.when(i < nt_ref[0])
    def _():
        x = xs_ref[...]
        hidden = (jax.nn.silu(_dot(x, wg_bf[...])) * _dot(x, wu_bf[...])).astype(BF16)
        ys_ref[...] = _dot(hidden, wd_bf[...]).astype(BF16)


def _experts(tile_expert, n_tiles, next_expert, x_sorted, w_gate, w_up, w_down):
    n_sorted, dm = x_sorted.shape
    de = w_gate.shape[2]
    tm = TM_EXPERT
    return pl.pallas_call(
        _experts_kernel,
        grid_spec=pltpu.PrefetchScalarGridSpec(
            num_scalar_prefetch=3,
            grid=(n_sorted // tm,),
            in_specs=[pl.BlockSpec((tm, dm), lambda i, te, nt, nx: (jnp.minimum(i, nt[0] - 1), 0)),
                      pl.BlockSpec(memory_space=pl.ANY), pl.BlockSpec(memory_space=pl.ANY),
                      pl.BlockSpec(memory_space=pl.ANY)],
            out_specs=pl.BlockSpec((tm, dm), lambda i, te, nt, nx: (jnp.minimum(i, nt[0] - 1), 0)),
            scratch_shapes=[pltpu.VMEM((2, dm, de), F32), pltpu.VMEM((2, dm, de), F32), pltpu.VMEM((2, de, dm), F32),
                            pltpu.VMEM((dm, de), BF16), pltpu.VMEM((dm, de), BF16), pltpu.VMEM((de, dm), BF16),
                            pltpu.SMEM((1,), jnp.int32), pltpu.SemaphoreType.DMA((2,))],
        ),
        out_shape=jax.ShapeDtypeStruct((n_sorted, dm), BF16),
        compiler_params=pltpu.CompilerParams(dimension_semantics=("arbitrary",),
                                             vmem_limit_bytes=VMEM_LIMIT),
        input_output_aliases={3: 0},
        name="experts",
    )(tile_expert, n_tiles, next_expert, x_sorted, w_gate, w_up, w_down)


def _combine_tile(x2_ref, route_ref, lbase, gfin_ref, local, o_ref):
    tm = x2_ref.shape[0]
    s_rows = local.shape[0]
    route = route_ref[...]
    lp1, lp2 = _local_positions(route, lbase)
    w1, w2 = route[:, R_W1:R_W1 + 1], route[:, R_W2:R_W2 + 1]
    col0 = lax.broadcasted_iota(jnp.int32, (tm, KB), 1).astype(F32)
    moe = None
    for k in range(s_rows // KB):
        col = col0 + float(k * KB)
        pick = (jnp.where(col == lp1, w1, 0.0) + jnp.where(col == lp2, w2, 0.0)).astype(BF16)
        part = _dot(pick, local[k * KB:(k + 1) * KB, :])
        moe = part if moe is None else moe + part
    yield
    o_ref[...] = _rms(x2_ref[...] + moe, gfin_ref[...])


def _combine_kernel(dst_ref, nq_ref, tpos_ref, tn_ref, x2_ref, route_ref, lbase_ref, gfin_ref, ys_ref,
                    o_ref, local_scr, cnt_scr, sem):
    del tpos_ref, tn_ref
    i = pl.program_id(0)
    n = pl.num_programs(0)
    slot = i % 2
    per_step = local_scr.shape[1]
    s_rows = local_scr.shape[2]

    def fetch(step, s):
        for k in range(per_step):
            tile = step * per_step + k

            def clear(r, carry, k=k):
                local_scr[s, k, pl.ds(pl.multiple_of(r * SEG_ALIGN, SEG_ALIGN), SEG_ALIGN), :] = jnp.zeros(
                    (SEG_ALIGN, local_scr.shape[3]), BF16)
                return carry

            lax.fori_loop(nq_ref[tile], s_rows // SEG_ALIGN, clear, 0)
            cnt_scr[s * per_step + k] = _segment_copies(tile, dst_ref, nq_ref, local_scr.at[s, k], ys_ref,
                                                        sem.at[s * per_step + k], False)

    @pl.when(i == 0)
    def _():
        fetch(0, 0)

    @pl.when(i + 1 < n)
    def _():
        fetch(i + 1, 1 - slot)

    tiles = []
    for k in range(per_step):
        c = slot * per_step + k
        _wait_chunks(cnt_scr[c], local_scr.at[slot, k], ys_ref, sem.at[c])
        rows = slice(k * TM_MIX, (k + 1) * TM_MIX)
        tiles.append(_combine_tile(x2_ref.at[rows, :], route_ref.at[rows, :], lbase_ref[k], gfin_ref,
                                   local_scr.at[slot, k], o_ref.at[rows, :]))
    _lockstep(tiles)


def _combine(plan, x2, route, lbase_f, gfin, y_sorted):
    t, dm = x2.shape
    tm = TM_MIX * MOE_TILES
    s_rows = _local_rows(TM_MIX)
    im = lambda i, *_: (i, 0)
    return pl.pallas_call(
        _combine_kernel,
        grid_spec=pltpu.PrefetchScalarGridSpec(
            num_scalar_prefetch=4,
            grid=(t // tm,),
            in_specs=[pl.BlockSpec((tm, dm), im), pl.BlockSpec((tm, LANES), im),
                      pl.BlockSpec((MOE_TILES, 1, LANES), lambda i, *_: (i, 0, 0)),
                      pl.BlockSpec((1, dm), lambda i, *_: (0, 0)),
                      pl.BlockSpec(memory_space=pl.ANY)],
            out_specs=pl.BlockSpec((tm, dm), im),
            scratch_shapes=[pltpu.VMEM((2, MOE_TILES, s_rows, dm), BF16), pltpu.SMEM((2 * MOE_TILES,), jnp.int32),
                            pltpu.SemaphoreType.DMA((2 * MOE_TILES,))],
        ),
        out_shape=jax.ShapeDtypeStruct((t, dm), F32),
        compiler_params=pltpu.CompilerParams(dimension_semantics=("arbitrary",),
                                             vmem_limit_bytes=VMEM_LIMIT),
        name="combine_norm",
    )(*_plan_specs(plan), x2, route, lbase_f, gfin, y_sorted)


def _layer(x, p, s5_ops, gfin):
    b, l, dm = x.shape
    x2d = x.reshape(b * l, dm)
    mgm, us5 = _inproj_gmlp(x2d, p['gmix'], p['win'], p['lng'], p['lnb'], p['ws'], p['bs'], p['gout_gm'])
    n_seg = SUBLANES // b
    *lag_factors, w1, w2, sc = s5_ops[(l // (S5_LC * n_seg))]
    xg = _s5_inproj(x, p['gmix'], p['win_s5'], n_seg)
    yg = _s5_scan(xg, lag_factors, w1, w2, sc, n_seg)
    ys = _s5_to_tokens(yg, b, l, n_seg)
    x2, t_bf, route, counts = _mix_route(ys, us5, mgm, x2d, p['d'], p['gluw'], p['glub'], p['gout_s5'],
                                         p['wout'], p['gffn'], p['rw_hl'], p['rb'])
    plan, lbase_f, tiles, n_sorted = _segment_plan(counts, b * l, TM_EXPERT)
    x_sorted = _sort_rows(plan, tiles[1], t_bf, route, lbase_f, n_sorted)
    y_sorted = _experts(*tiles, x_sorted, p['w_gate'], p['w_up'], p['w_down'])
    out = _combine(plan, x2, route, lbase_f, gfin, y_sorted)
    return out.reshape(b, l, dm)


def kernel(x_prompt, x_sample, norm_mix_g, w_in, gm_ln_g, gm_ln_b, gm_ws, gm_bs, s5_lam_re_fwd, s5_lam_im_fwd, s5_log_step_fwd, s5_b_re_fwd, s5_b_im_fwd, s5_c_re_fwd, s5_c_im_fwd, s5_lam_re_bwd, s5_lam_im_bwd, s5_log_step_bwd, s5_b_re_bwd, s5_b_im_bwd, s5_c_re_bwd, s5_c_im_bwd, s5_d, s5_glu_w, s5_glu_b, out_norm_gm, out_norm_s5, w_out, norm_ffn_g, r1_w, r1_b, r2_w, r2_b, e_w_gate, e_w_up, e_w_down, norm_final_g):
    depth = w_in.shape[0]
    gfin = norm_final_g.reshape(1, -1).astype(F32)
    xs = [x_prompt, x_sample]
    for li in range(depth):
        row = lambda a: a[li].reshape(1, -1).astype(F32)
        dm = w_in.shape[1]
        gw = gm_ln_g.shape[1]
        hd_dim = gw // GM_HEADS
        rw = jnp.concatenate([r1_w[li], r2_w[li].transpose(1, 0, 2).reshape(dm, N_EXPERTS)], axis=1).astype(F32)
        rw = jnp.pad(rw, ((0, 0), (0, LANES - rw.shape[1])))
        rwh = rw.astype(BF16)
        rwl = (rw - rwh.astype(F32)).astype(BF16)
        rb = jnp.concatenate([r1_b[li], r2_b[li].reshape(-1)]).astype(F32)
        rb = jnp.pad(rb, (0, LANES - rb.shape[0])).reshape(1, LANES)
        p = dict(
            gmix=row(norm_mix_g), win=w_in[li].astype(BF16), win_s5=w_in[li][:, 2 * gw:].astype(BF16),
            lng=row(gm_ln_g), lnb=row(gm_ln_b),
            ws=gm_ws[li].astype(BF16),
            bs=jnp.broadcast_to(gm_bs[li].astype(F32)[:, :, None], (GM_HEADS, CHUNK, hd_dim)),
            gout_gm=row(out_norm_gm), d=row(s5_d), gluw=s5_glu_w[li].astype(BF16), glub=row(s5_glu_b),
            gout_s5=row(out_norm_s5), wout=w_out[li].astype(BF16), gffn=row(norm_ffn_g),
            rw_hl=jnp.concatenate([rwh, rwl], axis=1), rb=rb,
            w_gate=e_w_gate[li], w_up=e_w_up[li], w_down=e_w_down[li],
        )
        fwd = (s5_lam_re_fwd[li], s5_lam_im_fwd[li], s5_log_step_fwd[li], s5_b_re_fwd[li], s5_b_im_fwd[li],
               s5_c_re_fwd[li], s5_c_im_fwd[li])
        bwd = (s5_lam_re_bwd[li], s5_lam_im_bwd[li], s5_log_step_bwd[li], s5_b_re_bwd[li], s5_b_im_bwd[li],
               s5_c_re_bwd[li], s5_c_im_bwd[li])
        s5_ops = {}
        for x in xs:
            seg_steps = x.shape[1] // (S5_LC * (SUBLANES // x.shape[0]))
            if seg_steps not in s5_ops:
                s5_ops[seg_steps] = _s5_operator(fwd, bwd, S5_LC, seg_steps)
        last = li == depth - 1
        assert last, "depth > 1 needs an un-normalised layer output"
        xs = [_layer(x, p, s5_ops, gfin) for x in xs]
    return tuple(xs)
```
